```python
import jax, jax.numpy as jnp
from jax import lax
import numpy as np

D_MODEL = 2048
BATCH = 8
SEQ = 4096
DEPTH = 2

N_MIXERS = 2
N_MLA_LAYERS = (DEPTH + 1) // 2
N_CONV_LAYERS = DEPTH // 2
MLA_HEADS = 16
QK_NOPE_DIM = 128
QK_ROPE_DIM = 64
V_HEAD_DIM = 128
Q_LORA_RANK = 512
KV_LORA_RANK = 512
MLA_LATENT_DIM = Q_LORA_RANK + KV_LORA_RANK + QK_ROPE_DIM
ROPE_THETA = 10000.0
CONV_WIDTH = 3
D_FF = 4 * D_MODEL
Q_BLOCK = 128
NORM_EPS = 1e-6
N_MOD = 6

kernel_name = "hybrid_mla_shortconv_adaln_sandwich"


def rmsnorm(x, g):
    x32 = x.astype(jnp.float32)
    y = x32 * lax.rsqrt(jnp.mean(x32 * x32, axis=-1, keepdims=True) + NORM_EPS)
    return (y * g.astype(jnp.float32)).astype(x.dtype)


def rope_cos_sin(positions):
    inv_freq = ROPE_THETA ** (-jnp.arange(0, QK_ROPE_DIM, 2, dtype=jnp.float32) / QK_ROPE_DIM)
    ang = positions.astype(jnp.float32)[..., None] * inv_freq
    return jnp.cos(ang), jnp.sin(ang)


def apply_rope(t, cos, sin):
    t32 = t.astype(jnp.float32)
    half = QK_ROPE_DIM // 2
    t1, t2 = t32[..., :half], t32[..., half:]
    out = jnp.concatenate([t1 * cos - t2 * sin, t2 * cos + t1 * sin], axis=-1)
    return out.astype(t.dtype)


def mla_mixer(h, positions, w_in, g_q, g_kv, w_uq, w_ukv, w_o):
    B, S, _ = h.shape
    lat = h @ w_in
    c_q = rmsnorm(lat[..., :Q_LORA_RANK], g_q)
    c_kv = rmsnorm(lat[..., Q_LORA_RANK:Q_LORA_RANK + KV_LORA_RANK], g_kv)
    k_rope = lat[..., Q_LORA_RANK + KV_LORA_RANK:]
    cos, sin = rope_cos_sin(positions)
    k_rope = apply_rope(k_rope, cos, sin)
    q = jnp.einsum('bsr,rhd->bshd', c_q, w_uq)
    q_nope = q[..., :QK_NOPE_DIM]
    q_rope = apply_rope(q[..., QK_NOPE_DIM:], cos[:, :, None, :], sin[:, :, None, :])
    kv = jnp.einsum('bsr,rhd->bshd', c_kv, w_ukv)
    k_nope, v = kv[..., :QK_NOPE_DIM], kv[..., QK_NOPE_DIM:]

    n_blk = S // Q_BLOCK
    scale = (QK_NOPE_DIM + QK_ROPE_DIM) ** -0.5
    qn_blocks = q_nope.reshape(B, n_blk, Q_BLOCK, MLA_HEADS, QK_NOPE_DIM).transpose(1, 0, 2, 3, 4)
    qr_blocks = q_rope.reshape(B, n_blk, Q_BLOCK, MLA_HEADS, QK_ROPE_DIM).transpose(1, 0, 2, 3, 4)
    starts = jnp.arange(n_blk, dtype=jnp.int32) * Q_BLOCK
    k_idx = jnp.arange(S, dtype=jnp.int32)

    def attend(args):
        qn, qr, start = args
        s = (jnp.einsum('bqhd,bkhd->bhqk', qn, k_nope, preferred_element_type=jnp.float32)
             + jnp.einsum('bqhd,bkd->bhqk', qr, k_rope, preferred_element_type=jnp.float32)) * scale
        q_idx = start + jnp.arange(Q_BLOCK, dtype=jnp.int32)
        causal = k_idx[None, :] <= q_idx[:, None]
        s = jnp.where(causal, s, jnp.finfo(jnp.float32).min)
        p = jax.nn.softmax(s, axis=-1)
        return jnp.einsum('bhqk,bkhd->bqhd', p.astype(v.dtype), v)

    o = lax.map(attend, (qn_blocks, qr_blocks, starts))
    o = o.transpose(1, 0, 2, 3, 4).reshape(B, S, MLA_HEADS * V_HEAD_DIM)
    return o @ w_o


def short_conv_mixer(h, w_in, conv_w, w_out):
    proj = h @ w_in
    b_gate = proj[..., :D_MODEL]
    c_gate = proj[..., D_MODEL:2 * D_MODEL]
    u = proj[..., 2 * D_MODEL:]
    z = c_gate * u
    z = lax.conv_general_dilated(z, conv_w[:, None, :].astype(z.dtype), window_strides=(1,),
                                 padding=[(CONV_WIDTH - 1, 0)],
                                 dimension_numbers=('NWC', 'WIO', 'NWC'),
                                 feature_group_count=D_MODEL)
    return (b_gate * z) @ w_out


def sq_relu_mlp(h, w_up, w_down):
    a = jax.nn.relu(h @ w_up)
    return (a * a) @ w_down


def _fwd_setup_inputs(seed: int = 0) -> dict:
    key = jax.random.key(seed)
    ks = jax.random.split(key, 20)
    f32 = jnp.float32

    def nrm(k, shape, fan_in, mult=1.0):
        return jax.random.normal(k, shape, f32) * (mult * fan_in ** -0.5)

    x = jax.random.normal(ks[0], (BATCH, SEQ, D_MODEL), f32)
    c = jax.random.normal(ks[1], (BATCH, D_MODEL), f32)
    offsets = jax.random.randint(ks[2], (BATCH, 1), 0, 1024, dtype=jnp.int32)
    positions = jnp.arange(SEQ, dtype=jnp.int32)[None, :] + offsets
    w_mod = nrm(ks[3], (DEPTH, D_MODEL, N_MOD * D_MODEL), D_MODEL, 0.5)
    b_mod = 0.02 * jax.random.normal(ks[4], (DEPTH, N_MOD * D_MODEL), f32)
    norm_g = 1.0 + 0.05 * jax.random.normal(ks[5], (DEPTH, 4, D_MODEL), f32)
    mla_w_in = nrm(ks[6], (N_MLA_LAYERS, D_MODEL, MLA_LATENT_DIM), D_MODEL)
    mla_g_q = 1.0 + 0.05 * jax.random.normal(ks[7], (N_MLA_LAYERS, Q_LORA_RANK), f32)
    mla_g_kv = 1.0 + 0.05 * jax.random.normal(ks[8], (N_MLA_LAYERS, KV_LORA_RANK), f32)
    mla_w_uq = nrm(ks[9], (N_MLA_LAYERS, Q_LORA_RANK, MLA_HEADS, QK_NOPE_DIM + QK_ROPE_DIM), Q_LORA_RANK)
    mla_w_ukv = nrm(ks[10], (N_MLA_LAYERS, KV_LORA_RANK, MLA_HEADS, QK_NOPE_DIM + V_HEAD_DIM), KV_LORA_RANK)
    mla_w_o = nrm(ks[11], (N_MLA_LAYERS, MLA_HEADS * V_HEAD_DIM, D_MODEL), MLA_HEADS * V_HEAD_DIM)
    conv_w_in = nrm(ks[12], (N_CONV_LAYERS, D_MODEL, 3 * D_MODEL), D_MODEL)
    conv_w = nrm(ks[13], (N_CONV_LAYERS, CONV_WIDTH, D_MODEL), CONV_WIDTH)
    conv_w_out = nrm(ks[14], (N_CONV_LAYERS, D_MODEL, D_MODEL), D_MODEL)
    mlp_w_up = nrm(ks[15], (DEPTH, D_MODEL, D_FF), D_MODEL)
    mlp_w_down = nrm(ks[16], (DEPTH, D_FF, D_MODEL), D_FF)
    return {"x": x, "c": c, "positions": positions, "w_mod": w_mod, "b_mod": b_mod,
            "norm_g": norm_g, "mla_w_in": mla_w_in, "mla_g_q": mla_g_q, "mla_g_kv": mla_g_kv,
            "mla_w_uq": mla_w_uq, "mla_w_ukv": mla_w_ukv, "mla_w_o": mla_w_o,
            "conv_w_in": conv_w_in, "conv_w": conv_w, "conv_w_out": conv_w_out,
            "mlp_w_up": mlp_w_up, "mlp_w_down": mlp_w_down}


def _fwd_reference(x, c, positions, w_mod, b_mod, norm_g, mla_w_in, mla_g_q, mla_g_kv,
              mla_w_uq, mla_w_ukv, mla_w_o, conv_w_in, conv_w, conv_w_out,
              mlp_w_up, mlp_w_down):
    cond = jax.nn.silu(c)
    for i in range(DEPTH):
        mod = (cond @ w_mod[i] + b_mod[i])[:, None, :]
        sh1, sc1, g1, sh2, sc2, g2 = jnp.split(mod, N_MOD, axis=-1)
        h = rmsnorm(x, norm_g[i, 0]) * (1.0 + sc1) + sh1
        j = i // N_MIXERS
        if i % N_MIXERS == 0:
            y = mla_mixer(h, positions, mla_w_in[j], mla_g_q[j], mla_g_kv[j],
                          mla_w_uq[j], mla_w_ukv[j], mla_w_o[j])
        else:
            y = short_conv_mixer(h, conv_w_in[j], conv_w[j], conv_w_out[j])
        x = x + g1 * rmsnorm(y, norm_g[i, 1])
        h = rmsnorm(x, norm_g[i, 2]) * (1.0 + sc2) + sh2
        y = sq_relu_mlp(h, mlp_w_up[i], mlp_w_down[i])
        x = x + g2 * rmsnorm(y, norm_g[i, 3])
    return x


import jax as _jax
import jax.numpy as _jnp

TWIN_FORMAT = 'train_step'
FWD_PARAMS = ['x', 'c', 'positions', 'w_mod', 'b_mod', 'norm_g', 'mla_w_in', 'mla_g_q', 'mla_g_kv', 'mla_w_uq', 'mla_w_ukv', 'mla_w_o', 'conv_w_in', 'conv_w', 'conv_w_out', 'mlp_w_up', 'mlp_w_down']
TWIN_WEIGHTS = ['w_mod', 'b_mod', 'norm_g', 'mla_w_in', 'mla_g_q', 'mla_g_kv', 'mla_w_uq', 'mla_w_ukv', 'mla_w_o', 'conv_w_in', 'conv_w', 'conv_w_out', 'mlp_w_up', 'mlp_w_down']
TWIN_DIFF_INPUT = 'x'
TWIN_INPUTS = ['x', 'c', 'positions', 'w_mod', 'b_mod', 'norm_g', 'mla_w_in', 'mla_g_q', 'mla_g_kv', 'mla_w_uq', 'mla_w_ukv', 'mla_w_o', 'conv_w_in', 'conv_w', 'conv_w_out', 'mlp_w_up', 'mlp_w_down', 'loss_target', 'm_w_mod', 'm_b_mod', 'm_norm_g', 'm_mla_w_in', 'm_mla_g_q', 'm_mla_g_kv', 'm_mla_w_uq', 'm_mla_w_ukv', 'm_mla_w_o', 'm_conv_w_in', 'm_conv_w', 'm_conv_w_out', 'm_mlp_w_up', 'm_mlp_w_down', 'v_w_mod', 'v_b_mod', 'v_norm_g', 'v_mla_w_in', 'v_mla_g_q', 'v_mla_g_kv', 'v_mla_w_uq', 'v_mla_w_ukv', 'v_mla_w_o', 'v_conv_w_in', 'v_conv_w', 'v_conv_w_out', 'v_mlp_w_up', 'v_mlp_w_down']
TWIN_OUTPUTS = ['loss', 'grad_x', 'grad_w_mod', 'grad_b_mod', 'grad_norm_g', 'grad_mla_w_in', 'grad_mla_g_q', 'grad_mla_g_kv', 'grad_mla_w_uq', 'grad_mla_w_ukv', 'grad_mla_w_o', 'grad_conv_w_in', 'grad_conv_w', 'grad_conv_w_out', 'grad_mlp_w_up', 'grad_mlp_w_down', 'delta_w_mod', 'delta_b_mod', 'delta_norm_g', 'delta_mla_w_in', 'delta_mla_g_q', 'delta_mla_g_kv', 'delta_mla_w_uq', 'delta_mla_w_ukv', 'delta_mla_w_o', 'delta_conv_w_in', 'delta_conv_w', 'delta_conv_w_out', 'delta_mlp_w_up', 'delta_mlp_w_down', 'new_m_w_mod', 'new_m_b_mod', 'new_m_norm_g', 'new_m_mla_w_in', 'new_m_mla_g_q', 'new_m_mla_g_kv', 'new_m_mla_w_uq', 'new_m_mla_w_ukv', 'new_m_mla_w_o', 'new_m_conv_w_in', 'new_m_conv_w', 'new_m_conv_w_out', 'new_m_mlp_w_up', 'new_m_mlp_w_down', 'new_v_w_mod', 'new_v_b_mod', 'new_v_norm_g', 'new_v_mla_w_in', 'new_v_mla_g_q', 'new_v_mla_g_kv', 'new_v_mla_w_uq', 'new_v_mla_w_ukv', 'new_v_mla_w_o', 'new_v_conv_w_in', 'new_v_conv_w', 'new_v_conv_w_out', 'new_v_mlp_w_up', 'new_v_mlp_w_down']
TWIN_LEAF_KINDS = {'loss': 'loss', 'grad_x': 'grad_x', 'grad_w_mod': 'grad_w', 'grad_b_mod': 'grad_w', 'grad_norm_g': 'grad_w', 'grad_mla_w_in': 'grad_w', 'grad_mla_g_q': 'grad_w', 'grad_mla_g_kv': 'grad_w', 'grad_mla_w_uq': 'grad_w', 'grad_mla_w_ukv': 'grad_w', 'grad_mla_w_o': 'grad_w', 'grad_conv_w_in': 'grad_w', 'grad_conv_w': 'grad_w', 'grad_conv_w_out': 'grad_w', 'grad_mlp_w_up': 'grad_w', 'grad_mlp_w_down': 'grad_w', 'delta_w_mod': 'delta_w', 'delta_b_mod': 'delta_w', 'delta_norm_g': 'delta_w', 'delta_mla_w_in': 'delta_w', 'delta_mla_g_q': 'delta_w', 'delta_mla_g_kv': 'delta_w', 'delta_mla_w_uq': 'delta_w', 'delta_mla_w_ukv': 'delta_w', 'delta_mla_w_o': 'delta_w', 'delta_conv_w_in': 'delta_w', 'delta_conv_w': 'delta_w', 'delta_conv_w_out': 'delta_w', 'delta_mlp_w_up': 'delta_w', 'delta_mlp_w_down': 'delta_w', 'new_m_w_mod': 'new_m', 'new_m_b_mod': 'new_m', 'new_m_norm_g': 'new_m', 'new_m_mla_w_in': 'new_m', 'new_m_mla_g_q': 'new_m', 'new_m_mla_g_kv': 'new_m', 'new_m_mla_w_uq': 'new_m', 'new_m_mla_w_ukv': 'new_m', 'new_m_mla_w_o': 'new_m', 'new_m_conv_w_in': 'new_m', 'new_m_conv_w': 'new_m', 'new_m_conv_w_out': 'new_m', 'new_m_mlp_w_up': 'new_m', 'new_m_mlp_w_down': 'new_m', 'new_v_w_mod': 'new_v', 'new_v_b_mod': 'new_v', 'new_v_norm_g': 'new_v', 'new_v_mla_w_in': 'new_v', 'new_v_mla_g_q': 'new_v', 'new_v_mla_g_kv': 'new_v', 'new_v_mla_w_uq': 'new_v', 'new_v_mla_w_ukv': 'new_v', 'new_v_mla_w_o': 'new_v', 'new_v_conv_w_in': 'new_v', 'new_v_conv_w': 'new_v', 'new_v_conv_w_out': 'new_v', 'new_v_mlp_w_up': 'new_v', 'new_v_mlp_w_down': 'new_v'}


def _forward(args):
    return _fwd_reference(*[args[k] for k in FWD_PARAMS])


def _output_shape():
    def fwd():
        inp = _fwd_setup_inputs(0)
        return _fwd_reference(*[inp[k] for k in FWD_PARAMS])
    out = _jax.eval_shape(fwd)
    return out.shape, out.dtype

N_MICROBATCH = 1
ADAM_LR = 0.001
ADAM_B1 = 0.9
ADAM_B2 = 0.999
ADAM_EPS = 1e-08
ADAM_WD = 0.01
ADAM_STEP = 10
PER_EXAMPLE_BATCH_AXIS = {'x': 0, 'c': 0, 'positions': 0, 'loss_target': 0}
SHARED_INPUTS = []
_WEIGHT_DTYPES = {'w_mod': _jnp.float32, 'b_mod': _jnp.float32, 'norm_g': _jnp.float32, 'mla_w_in': _jnp.float32, 'mla_g_q': _jnp.float32, 'mla_g_kv': _jnp.float32, 'mla_w_uq': _jnp.float32, 'mla_w_ukv': _jnp.float32, 'mla_w_o': _jnp.float32, 'conv_w_in': _jnp.float32, 'conv_w': _jnp.float32, 'conv_w_out': _jnp.float32, 'mlp_w_up': _jnp.float32, 'mlp_w_down': _jnp.float32}
MOMENT_SCALE = {'w_mod': 9.288057e-01, 'b_mod': 1.797779e+00, 'norm_g': 1.293938e+00, 'mla_w_in': 1.073553e+00, 'mla_g_q': 5.592133e-02, 'mla_g_kv': 1.506352e+00, 'mla_w_uq': 2.273119e-02, 'mla_w_ukv': 5.254461e-01, 'mla_w_o': 7.422078e-01, 'conv_w_in': 5.136418e-02, 'conv_w': 5.199090e-02, 'conv_w_out': 5.433986e-02, 'mlp_w_up': 8.914186e-02, 'mlp_w_down': 3.263780e-01}


def _to_microbatches(a, axis):
    t = _jnp.moveaxis(a, axis, 0)
    t = t.reshape((N_MICROBATCH, t.shape[0] // N_MICROBATCH) + t.shape[1:])
    return _jnp.moveaxis(t, 1, axis + 1)


def setup_inputs(seed: int = 0) -> dict:
    inp = _fwd_setup_inputs(seed)
    key = _jax.random.fold_in(_jax.random.key(seed), 7919)
    shape, _ = _output_shape()
    out = dict(inp)
    out["loss_target"] = _jax.random.normal(_jax.random.fold_in(key, 0), shape, _jnp.float32)
    for i, name in enumerate(TWIN_WEIGHTS):
        w = inp[name].astype(_jnp.float32)
        if MOMENT_SCALE is None:
            s = _jnp.sqrt(_jnp.mean(_jnp.square(w)) + 1e-30)
        else:
            s = MOMENT_SCALE[name]
        km, kv = _jax.random.split(_jax.random.fold_in(key, i + 1))
        out[name] = w
        out["m_" + name] = s * _jax.random.normal(km, w.shape, _jnp.float32)
        out["v_" + name] = (s * s) * _jax.random.uniform(kv, w.shape, _jnp.float32, 0.5, 1.5)
    if N_MICROBATCH > 1:
        for name, axis in PER_EXAMPLE_BATCH_AXIS.items():
            out[name] = _to_microbatches(out[name], axis)
    return {'x': out['x'], 'c': out['c'], 'positions': out['positions'], 'w_mod': out['w_mod'], 'b_mod': out['b_mod'], 'norm_g': out['norm_g'], 'mla_w_in': out['mla_w_in'], 'mla_g_q': out['mla_g_q'], 'mla_g_kv': out['mla_g_kv'], 'mla_w_uq': out['mla_w_uq'], 'mla_w_ukv': out['mla_w_ukv'], 'mla_w_o': out['mla_w_o'], 'conv_w_in': out['conv_w_in'], 'conv_w': out['conv_w'], 'conv_w_out': out['conv_w_out'], 'mlp_w_up': out['mlp_w_up'], 'mlp_w_down': out['mlp_w_down'], 'loss_target': out['loss_target'], 'm_w_mod': out['m_w_mod'], 'm_b_mod': out['m_b_mod'], 'm_norm_g': out['m_norm_g'], 'm_mla_w_in': out['m_mla_w_in'], 'm_mla_g_q': out['m_mla_g_q'], 'm_mla_g_kv': out['m_mla_g_kv'], 'm_mla_w_uq': out['m_mla_w_uq'], 'm_mla_w_ukv': out['m_mla_w_ukv'], 'm_mla_w_o': out['m_mla_w_o'], 'm_conv_w_in': out['m_conv_w_in'], 'm_conv_w': out['m_conv_w'], 'm_conv_w_out': out['m_conv_w_out'], 'm_mlp_w_up': out['m_mlp_w_up'], 'm_mlp_w_down': out['m_mlp_w_down'], 'v_w_mod': out['v_w_mod'], 'v_b_mod': out['v_b_mod'], 'v_norm_g': out['v_norm_g'], 'v_mla_w_in': out['v_mla_w_in'], 'v_mla_g_q': out['v_mla_g_q'], 'v_mla_g_kv': out['v_mla_g_kv'], 'v_mla_w_uq': out['v_mla_w_uq'], 'v_mla_w_ukv': out['v_mla_w_ukv'], 'v_mla_w_o': out['v_mla_w_o'], 'v_conv_w_in': out['v_conv_w_in'], 'v_conv_w': out['v_conv_w'], 'v_conv_w_out': out['v_conv_w_out'], 'v_mlp_w_up': out['v_mlp_w_up'], 'v_mlp_w_down': out['v_mlp_w_down']}


def _loss(weights, diff, rest, loss_target):
    with _jax.named_scope("forward"):
        args = {**rest, TWIN_DIFF_INPUT: diff, **{k: w.astype(_WEIGHT_DTYPES[k]) for k, w in weights.items()}}
        y = _forward(args)
    with _jax.named_scope("loss_head"):
        err = _jnp.square(y.astype(_jnp.float32) - loss_target)
        return 0.5 * _jnp.sum(_jnp.mean(err, axis=-1)) if err.ndim else 0.5 * err


def _adamw(w, g, m, v):
    m = ADAM_B1 * m + (1.0 - ADAM_B1) * g
    v = ADAM_B2 * v + (1.0 - ADAM_B2) * _jnp.square(g)
    m_hat = m / (1.0 - ADAM_B1 ** ADAM_STEP)
    v_hat = v / (1.0 - ADAM_B2 ** ADAM_STEP)
    delta = -ADAM_LR * (m_hat / (_jnp.sqrt(v_hat) + ADAM_EPS) + ADAM_WD * w)
    return delta, m, v


def reference(x, c, positions, w_mod, b_mod, norm_g, mla_w_in, mla_g_q, mla_g_kv, mla_w_uq, mla_w_ukv, mla_w_o, conv_w_in, conv_w, conv_w_out, mlp_w_up, mlp_w_down, loss_target, m_w_mod, m_b_mod, m_norm_g, m_mla_w_in, m_mla_g_q, m_mla_g_kv, m_mla_w_uq, m_mla_w_ukv, m_mla_w_o, m_conv_w_in, m_conv_w, m_conv_w_out, m_mlp_w_up, m_mlp_w_down, v_w_mod, v_b_mod, v_norm_g, v_mla_w_in, v_mla_g_q, v_mla_g_kv, v_mla_w_uq, v_mla_w_ukv, v_mla_w_o, v_conv_w_in, v_conv_w, v_conv_w_out, v_mlp_w_up, v_mlp_w_down):
    given = dict(x=x, c=c, positions=positions, w_mod=w_mod, b_mod=b_mod, norm_g=norm_g, mla_w_in=mla_w_in, mla_g_q=mla_g_q, mla_g_kv=mla_g_kv, mla_w_uq=mla_w_uq, mla_w_ukv=mla_w_ukv, mla_w_o=mla_w_o, conv_w_in=conv_w_in, conv_w=conv_w, conv_w_out=conv_w_out, mlp_w_up=mlp_w_up, mlp_w_down=mlp_w_down, loss_target=loss_target, m_w_mod=m_w_mod, m_b_mod=m_b_mod, m_norm_g=m_norm_g, m_mla_w_in=m_mla_w_in, m_mla_g_q=m_mla_g_q, m_mla_g_kv=m_mla_g_kv, m_mla_w_uq=m_mla_w_uq, m_mla_w_ukv=m_mla_w_ukv, m_mla_w_o=m_mla_w_o, m_conv_w_in=m_conv_w_in, m_conv_w=m_conv_w, m_conv_w_out=m_conv_w_out, m_mlp_w_up=m_mlp_w_up, m_mlp_w_down=m_mlp_w_down, v_w_mod=v_w_mod, v_b_mod=v_b_mod, v_norm_g=v_norm_g, v_mla_w_in=v_mla_w_in, v_mla_g_q=v_mla_g_q, v_mla_g_kv=v_mla_g_kv, v_mla_w_uq=v_mla_w_uq, v_mla_w_ukv=v_mla_w_ukv, v_mla_w_o=v_mla_w_o, v_conv_w_in=v_conv_w_in, v_conv_w=v_conv_w, v_conv_w_out=v_conv_w_out, v_mlp_w_up=v_mlp_w_up, v_mlp_w_down=v_mlp_w_down)
    weights = {n: given[n] for n in TWIN_WEIGHTS}
    shared = {n: given[n] for n in SHARED_INPUTS}
    per_example = {n: given[n] for n in ['x', 'c', 'positions']}
    grad_fn = _jax.value_and_grad(_loss, argnums=(0, 1))

    def one_microbatch(ex, loss_target):
        ex = dict(ex)
        diff = ex.pop(TWIN_DIFF_INPUT)
        return grad_fn(weights, diff, {**shared, **ex}, loss_target)

    if N_MICROBATCH == 1:
        loss, (grad_w, grad_x) = one_microbatch(per_example, given["loss_target"])
    else:
        def body(carry, xs):
            loss_sum, grad_sum = carry
            l_k, (gw_k, gx_k) = one_microbatch(xs[0], xs[1])
            with _jax.named_scope("update"):
                return (loss_sum + l_k, _jax.tree.map(_jnp.add, grad_sum, gw_k)), gx_k

        init = (_jnp.zeros((), _jnp.float32), _jax.tree.map(_jnp.zeros_like, weights))
        (loss, grad_w), grad_x = _jax.lax.scan(body, init, (per_example, given["loss_target"]))
    with _jax.named_scope("update"):
        delta_w, new_m, new_v = {}, {}, {}
        for n in TWIN_WEIGHTS:
            delta_w[n], new_m[n], new_v[n] = _adamw(weights[n], grad_w[n], given["m_" + n], given["v_" + n])
    return (loss, grad_x, *[grad_w[n] for n in TWIN_WEIGHTS], *[delta_w[n] for n in TWIN_WEIGHTS],
            *[new_m[n] for n in TWIN_WEIGHTS], *[new_v[n] for n in TWIN_WEIGHTS])
```

```python
import jax
import jax.numpy as jnp
from jax import lax
from jax.experimental import pallas as pl
from jax.experimental.pallas import tpu as pltpu

F32 = jnp.float32
BF16 = jnp.bfloat16
NORM_EPS = 1e-6
ROPE_THETA = 10000.0
QK_NOPE = 128
QK_ROPE = 64
V_HEAD = 128
LANES = 128
QK_PAD = QK_NOPE + LANES
ADAM_LR, ADAM_B1, ADAM_B2, ADAM_EPS, ADAM_WD, ADAM_STEP = 0.001, 0.9, 0.999, 1e-08, 0.01, 10
VMEM_LIMIT_BYTES = 48 * 1024 * 1024
N_CHIPS = 4
MESH_ID = pl.DeviceIdType.MESH
ANY = pl.BlockSpec(memory_space=pl.ANY)
NEG_INF = float("-inf")

DIMS_NN = (((1,), (0,)), ((), ()))
DIMS_NT = (((1,), (1,)), ((), ()))
DIMS_TN = (((0,), (0,)), ((), ()))


def _cparams(*sem):
    return pltpu.CompilerParams(dimension_semantics=sem, vmem_limit_bytes=VMEM_LIMIT_BYTES)


def _row_tile(rows, row_bytes, limit=2 * 1024 * 1024, mult=16):
    if rows * row_bytes <= limit or rows % mult:
        return rows
    best = mult
    t = mult
    while t <= rows:
        if rows % t == 0 and t * row_bytes <= limit:
            best = t
        t += mult
    return best


def _rms(v):
    return lax.rsqrt(jnp.mean(v * v, axis=-1, keepdims=True) + NORM_EPS)


def _mm(name, a, b, mode, M, N, K, outs, *, a_spec=None, b_spec=None, out_specs=None, epilogue=None,
        extras=(), alias=None, tm=1024, tn=1024, tk=2048):
    if mode == "tn":
        tk = min(tk, 512)
    tm, tn, tk = min(tm, M), min(tn, N), min(tk, K)
    assert M % tm == 0 and N % tn == 0 and K % tk == 0, (name, M, N, K)
    nk = K // tk
    if a_spec is None:
        a_spec = {"nn": pl.BlockSpec((tm, tk), lambda i, j, k: (i, k)),
                  "nt": pl.BlockSpec((tm, tk), lambda i, j, k: (i, k)),
                  "tn": pl.BlockSpec((tk, tm), lambda i, j, k: (k, i))}[mode]
    else:
        a_spec = a_spec(tm, tn, tk)
    if b_spec is None:
        b_spec = {"nn": pl.BlockSpec((tk, tn), lambda i, j, k: (k, j)),
                  "nt": pl.BlockSpec((tn, tk), lambda i, j, k: (j, k)),
                  "tn": pl.BlockSpec((tk, tn), lambda i, j, k: (k, j))}[mode]
    else:
        b_spec = b_spec(tm, tn, tk)
    if out_specs is None:
        out_specs = [pl.BlockSpec((tm, tn), lambda i, j, k: (i, j)) for _ in outs]
    else:
        out_specs = [s(tm, tn, tk) for s in out_specs]
    dims = {"nn": DIMS_NN, "nt": DIMS_NT, "tn": DIMS_TN}[mode]
    ne, no = len(extras), len(outs)

    def body(*refs):
        a_ref, b_ref = refs[0], refs[1]
        ex = refs[2:2 + ne]
        o = refs[2 + ne + (alias is not None):2 + ne + (alias is not None) + no]
        part = lax.dot_general(a_ref[...].astype(BF16), b_ref[...].astype(BF16), dims,
                               preferred_element_type=F32)

        def finish(total):
            vals = epilogue(total, *[e[...] for e in ex]) if epilogue is not None else (total,)
            for r, v in zip(o, vals):
                r[...] = v.astype(r.dtype)

        if nk == 1:
            finish(part)
        else:
            acc = refs[-1]
            kk = pl.program_id(2)

            @pl.when(kk == 0)
            def _():
                acc[...] = part

            @pl.when(kk > 0)
            def _():
                acc[...] += part

            @pl.when(kk == nk - 1)
            def _():
                finish(acc[...])

    operands = [a, b] + [e[0] for e in extras]
    in_specs = [a_spec, b_spec] + [e[1](tm, tn, tk) for e in extras]
    aliases = {}
    if alias is not None:
        operands.append(alias)
        in_specs.append(ANY)
        aliases = {len(operands) - 1: 0}
    res = pl.pallas_call(
        body, name=name, grid=(M // tm, N // tn, nk),
        in_specs=in_specs, out_specs=out_specs, out_shape=list(outs),
        scratch_shapes=[pltpu.VMEM((tm, tn), F32)] if nk > 1 else [],
        input_output_aliases=aliases,
        compiler_params=_cparams("parallel", "parallel", "arbitrary"),
    )(*operands)
    return res


def _sds(shape, dtype):
    return jax.ShapeDtypeStruct(tuple(shape), dtype)


def _rope(t, cos_p, sin_lo, sin_hi):
    return t * cos_p + pltpu.roll(t, LANES - QK_ROPE // 2, 1) * sin_lo + pltpu.roll(t, QK_ROPE // 2, 1) * sin_hi


def _rope_t(d, cos_p, sin_lo, sin_hi):
    return d * cos_p + pltpu.roll(d * sin_lo, QK_ROPE // 2, 1) + pltpu.roll(d * sin_hi, LANES - QK_ROPE // 2, 1)


def _vec_spec(d):
    return pl.BlockSpec((1, d), lambda i: (0, 0))


def _fwd_boundary(name, x_prev, y, gate, ng_post, ng_pre, sc, sh):
    S, D = x_prev.shape
    ts = min(256, S)
    has_y = y is not None
    row = pl.BlockSpec((ts, D), lambda i: (i, 0))

    def body(*refs):
        if has_y:
            x_ref, y_ref, g_ref, ngp_ref, ngn_ref, sc_ref, sh_ref, xo_ref, h_ref = refs
            yv = y_ref[...]
            xn = x_ref[...] + g_ref[...] * (yv * _rms(yv) * ngp_ref[...])
            xo_ref[...] = xn
        else:
            x_ref, ngn_ref, sc_ref, sh_ref, h_ref = refs
            xn = x_ref[...]
        hn = xn * _rms(xn) * ngn_ref[...]
        h_ref[...] = (hn * (1.0 + sc_ref[...]) + sh_ref[...]).astype(BF16)

    vec = _vec_spec(D)
    if has_y:
        operands = (x_prev, y, gate, ng_post, ng_pre, sc, sh)
        in_specs = [row, row, vec, vec, vec, vec, vec]
        out_shape = [_sds((S, D), F32), _sds((S, D), BF16)]
        out_specs = [row, row]
    else:
        operands = (x_prev, ng_pre, sc, sh)
        in_specs = [row, vec, vec, vec]
        out_shape = [_sds((S, D), BF16)]
        out_specs = [row]
    return pl.pallas_call(body, name=name, grid=(S // ts,), in_specs=in_specs, out_specs=out_specs,
                          out_shape=out_shape, compiler_params=_cparams("parallel"))(*operands)


def _acc_rows(sums_ref, rows):
    for r, v in rows:
        sums_ref[r:r + 1, :] += jnp.sum(v, axis=0, keepdims=True)


def _post_norm_bwd(dxt, yv, gate, ng_post, sums_ref, dy_ref):
    r1 = _rms(yv)
    yhat = yv * r1
    dn = dxt * gate
    u = dn * ng_post
    dy = r1 * (u - yhat * jnp.mean(u * yhat, axis=-1, keepdims=True))
    dy_ref[...] = dy.astype(dy_ref.dtype)
    _acc_rows(sums_ref, [(3, dxt * (yhat * ng_post)), (4, dn * yhat)])


def _loss_boundary(name, x_prev, y, gate, ng_post, target):
    S, D = x_prev.shape
    ts = min(256, S)
    row = pl.BlockSpec((ts, D), lambda i: (i, 0))
    vec = _vec_spec(D)

    def body(x_ref, y_ref, g_ref, ngp_ref, t_ref, dx_ref, dy_ref, sums_ref, loss_ref):
        @pl.when(pl.program_id(0) == 0)
        def _():
            sums_ref[...] = jnp.zeros_like(sums_ref)
            loss_ref[...] = jnp.zeros_like(loss_ref)

        yv = y_ref[...]
        xf = x_ref[...] + g_ref[...] * (yv * _rms(yv) * ngp_ref[...])
        err = xf - t_ref[...]
        loss_ref[...] += 0.5 * jnp.sum(jnp.mean(err * err, axis=-1, keepdims=True))
        dxt = err / D
        dx_ref[...] = dxt
        _post_norm_bwd(dxt, yv, g_ref[...], ngp_ref[...], sums_ref, dy_ref)

    return pl.pallas_call(
        body, name=name, grid=(S // ts,),
        in_specs=[row, row, vec, vec, row],
        out_specs=[row, row, pl.BlockSpec((8, D), lambda i: (0, 0)), pl.BlockSpec((8, LANES), lambda i: (0, 0))],
        out_shape=[_sds((S, D), F32), _sds((S, D), BF16), _sds((8, D), F32), _sds((8, LANES), F32)],
        compiler_params=_cparams("arbitrary"))(x_prev, y, gate, ng_post, target)


def _bwd_boundary(name, dx_new, dh, x_new, y, gate, ng_post, ng_pre, sc):
    S, D = x_new.shape
    ts = min(256, S)
    has_y = y is not None
    row = pl.BlockSpec((ts, D), lambda i: (i, 0))
    vec = _vec_spec(D)

    def body(*refs):
        if has_y:
            dxn_ref, dh_ref, x_ref, y_ref, g_ref, ngp_ref, ngn_ref, sc_ref, dxo_ref, dy_ref, sums_ref = refs
        else:
            dxn_ref, dh_ref, x_ref, ngn_ref, sc_ref, dxo_ref, sums_ref = refs

        @pl.when(pl.program_id(0) == 0)
        def _():
            sums_ref[...] = jnp.zeros_like(sums_ref)

        xv = x_ref[...]
        dhv = dh_ref[...]
        ngn = ngn_ref[...]
        r2 = _rms(xv)
        xhat = xv * r2
        dn_pre = dhv * (1.0 + sc_ref[...])
        u2 = dn_pre * ngn
        dxt = dxn_ref[...] + r2 * (u2 - xhat * jnp.mean(u2 * xhat, axis=-1, keepdims=True))
        dxo_ref[...] = dxt
        _acc_rows(sums_ref, [(0, dhv), (1, dhv * (xhat * ngn)), (2, dn_pre * xhat)])
        if has_y:
            _post_norm_bwd(dxt, y_ref[...], g_ref[...], ngp_ref[...], sums_ref, dy_ref)

    sums_spec = pl.BlockSpec((8, D), lambda i: (0, 0))
    if has_y:
        operands = (dx_new, dh, x_new, y, gate, ng_post, ng_pre, sc)
        in_specs = [row, row, row, row, vec, vec, vec, vec]
        out_shape = [_sds((S, D), F32), _sds((S, D), BF16), _sds((8, D), F32)]
        out_specs = [row, row, sums_spec]
    else:
        operands = (dx_new, dh, x_new, ng_pre, sc)
        in_specs = [row, row, row, vec, vec]
        out_shape = [_sds((S, D), F32), _sds((8, D), F32)]
        out_specs = [row, sums_spec]
    return pl.pallas_call(body, name=name, grid=(S // ts,), in_specs=in_specs, out_specs=out_specs,
                          out_shape=out_shape, compiler_params=_cparams("arbitrary"))(*operands)


def _latent_fwd(lat, g_q, g_kv, rope_tabs, rank):
    S, W = lat.shape
    ts = min(256, S)
    tab = pl.BlockSpec((ts, LANES), lambda i: (i, 0))

    def body(lat_ref, gq_ref, gkv_ref, cos_ref, slo_ref, shi_ref, cq_ref, ckv_ref, kr_ref):
        lq = lat_ref[:, 0:rank]
        lkv = lat_ref[:, rank:2 * rank]
        cq_ref[...] = (lq * _rms(lq) * gq_ref[...]).astype(BF16)
        ckv_ref[...] = (lkv * _rms(lkv) * gkv_ref[...]).astype(BF16)
        kr_ref[...] = _rope(lat_ref[:, 2 * rank:W], cos_ref[...], slo_ref[...], shi_ref[...]).astype(BF16)

    return pl.pallas_call(
        body, name="mla_latent_fwd", grid=(S // ts,),
        in_specs=[pl.BlockSpec((ts, W), lambda i: (i, 0)), _vec_spec(rank), _vec_spec(rank), tab, tab, tab],
        out_specs=[pl.BlockSpec((ts, rank), lambda i: (i, 0)), pl.BlockSpec((ts, rank), lambda i: (i, 0)), tab],
        out_shape=[_sds((S, rank), BF16), _sds((S, rank), BF16), _sds((S, LANES), BF16)],
        compiler_params=_cparams("parallel"))(lat, g_q, g_kv, *rope_tabs)


def _latent_bwd(lat, dcq, dckv, dkr, g_q, g_kv, rope_tabs, rank):
    S, W = lat.shape
    ts = min(256, S)
    tab = pl.BlockSpec((ts, LANES), lambda i: (i, 0))
    half = pl.BlockSpec((ts, rank), lambda i: (i, 0))

    def body(lat_ref, dcq_ref, dckv_ref, dkr_ref, gq_ref, gkv_ref, cos_ref, slo_ref, shi_ref, dlat_ref, sums_ref):
        @pl.when(pl.program_id(0) == 0)
        def _():
            sums_ref[...] = jnp.zeros_like(sums_ref)

        def norm_bwd(v, dn, g, r):
            rr = _rms(v)
            vhat = v * rr
            u = dn * g
            sums_ref[r:r + 1, :] += jnp.sum(dn * vhat, axis=0, keepdims=True)
            return rr * (u - vhat * jnp.mean(u * vhat, axis=-1, keepdims=True))

        dlat_ref[:, 0:rank] = norm_bwd(lat_ref[:, 0:rank], dcq_ref[...], gq_ref[...], 0).astype(BF16)
        dlat_ref[:, rank:2 * rank] = norm_bwd(lat_ref[:, rank:2 * rank], dckv_ref[...], gkv_ref[...], 1).astype(BF16)
        dlat_ref[:, 2 * rank:W] = _rope_t(dkr_ref[...], cos_ref[...], slo_ref[...], shi_ref[...]).astype(BF16)

    return pl.pallas_call(
        body, name="mla_latent_bwd", grid=(S // ts,),
        in_specs=[pl.BlockSpec((ts, W), lambda i: (i, 0)), half, half, tab, _vec_spec(rank), _vec_spec(rank),
                  tab, tab, tab],
        out_specs=[pl.BlockSpec((ts, W), lambda i: (i, 0)), pl.BlockSpec((8, rank), lambda i: (0, 0))],
        out_shape=[_sds((S, W), BF16), _sds((8, rank), F32)],
        compiler_params=_cparams("arbitrary"))(lat, dcq, dckv, dkr, g_q, g_kv, *rope_tabs)


def _attn_tiles(S):
    t = min(512, S)
    return t, S // t


def _attn_fwd(q, kv, kr, heads, scale):
    S = q.shape[0]
    t, nb = _attn_tiles(S)

    def body(q_ref, kv_ref, kr_ref, o_ref, lse_ref, m_scr, l_scr, acc_scr):
        qi, ki = pl.program_id(1), pl.program_id(2)

        @pl.when(ki == 0)
        def _():
            m_scr[...] = jnp.full_like(m_scr, NEG_INF)
            l_scr[...] = jnp.zeros_like(l_scr)
            acc_scr[...] = jnp.zeros_like(acc_scr)

        @pl.when(ki <= qi)
        def _():
            kcat = jnp.concatenate([kv_ref[:, 0:QK_NOPE], kr_ref[...]], axis=1)
            s = lax.dot_general(q_ref[...], kcat, DIMS_NT, preferred_element_type=F32) * scale
            rows = lax.broadcasted_iota(jnp.int32, (t, t), 0)
            cols = lax.broadcasted_iota(jnp.int32, (t, t), 1)
            s = jnp.where((ki < qi) | (cols <= rows), s, NEG_INF)
            m_prev = m_scr[...]
            m_new = jnp.maximum(m_prev, jnp.max(s, axis=-1, keepdims=True))
            alpha = jnp.exp(m_prev - m_new)
            p = jnp.exp(s - m_new)
            l_scr[...] = alpha * l_scr[...] + jnp.sum(p, axis=-1, keepdims=True)
            acc_scr[...] = alpha * acc_scr[...] + lax.dot_general(
                p.astype(BF16), kv_ref[:, QK_NOPE:QK_NOPE + V_HEAD], DIMS_NN, preferred_element_type=F32)
            m_scr[...] = m_new

        @pl.when(ki == nb - 1)
        def _():
            o_ref[...] = (acc_scr[...] / l_scr[...]).astype(BF16)
            lse_ref[...] = m_scr[...] + jnp.log(l_scr[...])

    return pl.pallas_call(
        body, name="mla_attn_fwd", grid=(heads, nb, nb),
        in_specs=[pl.BlockSpec((t, QK_PAD), lambda h, qi, ki: (qi, h)),
                  pl.BlockSpec((t, QK_PAD), lambda h, qi, ki: (jnp.minimum(ki, qi), h)),
                  pl.BlockSpec((t, LANES), lambda h, qi, ki: (jnp.minimum(ki, qi), 0))],
        out_specs=[pl.BlockSpec((t, V_HEAD), lambda h, qi, ki: (qi, h)),
                   pl.BlockSpec((None, t, 1), lambda h, qi, ki: (h, qi, 0))],
        out_shape=[_sds((S, heads * V_HEAD), BF16), _sds((heads, S, 1), F32)],
        scratch_shapes=[pltpu.VMEM((t, 1), F32), pltpu.VMEM((t, 1), F32), pltpu.VMEM((t, V_HEAD), F32)],
        compiler_params=_cparams("parallel", "parallel", "arbitrary"))(q, kv, kr)


def _attn_bwd(q, kv, kr, o, do, lse, rope_tabs, heads, scale):
    S = q.shape[0]
    t, nb = _attn_tiles(S)

    def body(q_ref, kv_ref, kr_ref, o_ref, do_ref, lse_ref, cos_ref, slo_ref, shi_ref,
             dq_ref, dkv_ref, dkr_ref, dq_scr, dk_scr, dv_scr, dkr_scr):
        h, ki, qi = pl.program_id(0), pl.program_id(1), pl.program_id(2)
        q_rows = pl.ds(pl.multiple_of(qi * t, t), t)
        k_rows = pl.ds(pl.multiple_of(ki * t, t), t)

        @pl.when((ki == 0) & (qi == 0))
        def _():
            dq_scr[...] = jnp.zeros_like(dq_scr)

        @pl.when((h == 0) & (ki == 0) & (qi == 0))
        def _():
            dkr_scr[...] = jnp.zeros_like(dkr_scr)

        @pl.when(qi == 0)
        def _():
            dk_scr[...] = jnp.zeros_like(dk_scr)
            dv_scr[...] = jnp.zeros_like(dv_scr)

        @pl.when(qi >= ki)
        def _():
            qv = q_ref[...]
            kcat = jnp.concatenate([kv_ref[:, 0:QK_NOPE], kr_ref[...]], axis=1)
            s = lax.dot_general(qv, kcat, DIMS_NT, preferred_element_type=F32) * scale
            rows = lax.broadcasted_iota(jnp.int32, (t, t), 0)
            cols = lax.broadcasted_iota(jnp.int32, (t, t), 1)
            p = jnp.where((ki < qi) | (cols <= rows), jnp.exp(s - lse_ref[...]), 0.0)
            dov = do_ref[...]
            dv_scr[...] += lax.dot_general(p.astype(BF16), dov, DIMS_TN, preferred_element_type=F32)
            dp = lax.dot_general(dov, kv_ref[:, QK_NOPE:QK_NOPE + V_HEAD], DIMS_NT, preferred_element_type=F32)
            delta = jnp.sum(dov.astype(F32) * o_ref[...].astype(F32), axis=-1, keepdims=True)
            ds = (p * (dp - delta) * scale).astype(BF16)
            dk_scr[...] += lax.dot_general(ds, qv, DIMS_TN, preferred_element_type=F32)
            dq_scr[q_rows, :] += lax.dot_general(ds, kcat, DIMS_NN, preferred_element_type=F32)

        @pl.when(qi == nb - 1)
        def _():
            dkv_ref[...] = jnp.concatenate([dk_scr[:, 0:QK_NOPE], dv_scr[...]], axis=1).astype(BF16)
            dkr_scr[k_rows, :] += dk_scr[:, QK_NOPE:QK_PAD]

        @pl.when(ki == nb - 1)
        def _():
            dqv = dq_scr[q_rows, :]
            dq_ref[q_rows, :] = jnp.concatenate(
                [dqv[:, 0:QK_NOPE], _rope_t(dqv[:, QK_NOPE:QK_PAD], cos_ref[...], slo_ref[...], shi_ref[...])],
                axis=1).astype(BF16)

        @pl.when((h == heads - 1) & (ki == nb - 1) & (qi == nb - 1))
        def _():
            dkr_ref[...] = dkr_scr[...]

    qmap = lambda h, ki, qi: (jnp.maximum(qi, ki), h)
    tab = pl.BlockSpec((t, LANES), lambda h, ki, qi: (qi, 0))
    return pl.pallas_call(
        body, name="mla_attn_bwd", grid=(heads, nb, nb),
        in_specs=[pl.BlockSpec((t, QK_PAD), qmap),
                  pl.BlockSpec((t, QK_PAD), lambda h, ki, qi: (ki, h)),
                  pl.BlockSpec((t, LANES), lambda h, ki, qi: (ki, 0)),
                  pl.BlockSpec((t, V_HEAD), qmap),
                  pl.BlockSpec((t, V_HEAD), qmap),
                  pl.BlockSpec((None, t, 1), lambda h, ki, qi: (h, jnp.maximum(qi, ki), 0)),
                  tab, tab, tab],
        out_specs=[pl.BlockSpec((S, QK_PAD), lambda h, ki, qi: (0, h)),
                   pl.BlockSpec((t, QK_PAD), lambda h, ki, qi: (ki, h)),
                   pl.BlockSpec((S, LANES), lambda h, ki, qi: (0, 0))],
        out_shape=[_sds((S, heads * QK_PAD), BF16), _sds((S, heads * QK_PAD), BF16), _sds((S, LANES), F32)],
        scratch_shapes=[pltpu.VMEM((S, QK_PAD), F32), pltpu.VMEM((t, QK_PAD), F32), pltpu.VMEM((t, V_HEAD), F32),
                        pltpu.VMEM((S, LANES), F32)],
        compiler_params=_cparams("arbitrary", "arbitrary", "arbitrary"))(q, kv, kr, o, do, lse, *rope_tabs)


def _shift_down(z, n, rows):
    return jnp.where(rows >= n, pltpu.roll(z, n, 0), 0.0)


def _shift_up(z, n, rows, S):
    return jnp.where(rows < S - n, pltpu.roll(z, S - n, 0), 0.0)


def _conv_specs(S, tc):
    strip = lambda p: pl.BlockSpec((None, S, tc), lambda j: (p, 0, j))
    return strip(0), strip(1), strip(2), pl.BlockSpec((3, tc), lambda j: (0, j))


def _conv_fwd(proj3, w):
    _, S, D = proj3.shape
    tc = LANES

    def body(b_ref, c_ref, u_ref, w_ref, out_ref):
        z = c_ref[...] * u_ref[...]
        rows = lax.broadcasted_iota(jnp.int32, (S, tc), 0)
        zc = w_ref[0:1, :] * _shift_down(z, 2, rows) + w_ref[1:2, :] * _shift_down(z, 1, rows) + w_ref[2:3, :] * z
        out_ref[...] = (b_ref[...] * zc).astype(BF16)

    return pl.pallas_call(
        body, name="conv_fwd", grid=(D // tc,), in_specs=list(_conv_specs(S, tc)),
        out_specs=pl.BlockSpec((S, tc), lambda j: (0, j)), out_shape=_sds((S, D), BF16),
        compiler_params=_cparams("parallel"))(proj3, proj3, proj3, w)


def _conv_bwd(dbz, proj3, w):
    _, S, D = proj3.shape
    tc = LANES

    def body(d_ref, b_ref, c_ref, u_ref, w_ref, dp_ref, dw_ref):
        cv, uv, dv = c_ref[...], u_ref[...], d_ref[...]
        z = cv * uv
        rows = lax.broadcasted_iota(jnp.int32, (S, tc), 0)
        z1, z2 = _shift_down(z, 1, rows), _shift_down(z, 2, rows)
        zc = w_ref[0:1, :] * z2 + w_ref[1:2, :] * z1 + w_ref[2:3, :] * z
        dp_ref[0] = (dv * zc).astype(BF16)
        dzc = dv * b_ref[...]
        dz = w_ref[2:3, :] * dzc + w_ref[1:2, :] * _shift_up(dzc, 1, rows, S) + w_ref[0:1, :] * _shift_up(dzc, 2, rows, S)
        dp_ref[1] = (dz * uv).astype(BF16)
        dp_ref[2] = (dz * cv).astype(BF16)
        dw_ref[0:1, :] = jnp.sum(dzc * z2, axis=0, keepdims=True)
        dw_ref[1:2, :] = jnp.sum(dzc * z1, axis=0, keepdims=True)
        dw_ref[2:3, :] = jnp.sum(dzc * z, axis=0, keepdims=True)

    sb, sc_, su, sw = _conv_specs(S, tc)
    return pl.pallas_call(
        body, name="conv_bwd", grid=(D // tc,),
        in_specs=[pl.BlockSpec((S, tc), lambda j: (0, j)), sb, sc_, su, sw],
        out_specs=[pl.BlockSpec((3, S, tc), lambda j: (0, 0, j)), pl.BlockSpec((3, tc), lambda j: (0, j))],
        out_shape=[_sds((3, S, D), BF16), _sds((3, D), F32)],
        compiler_params=_cparams("parallel"))(dbz, proj3, proj3, proj3, w)


def _silu(c_all):
    def body(c_ref, o_ref):
        cv = c_ref[...]
        o_ref[...] = cv * (1.0 / (1.0 + jnp.exp(-cv)))

    vm = pl.BlockSpec(memory_space=pltpu.VMEM)
    return pl.pallas_call(body, name="cond_silu", in_specs=[vm], out_specs=vm, out_shape=_sds(c_all.shape, F32))(c_all)


def _mod_fwd(cond, w_mod, b_cols):
    L, D, ncol = w_mod.shape
    B = cond.shape[0]
    tk, tn = min(512, D), min(1024, ncol)
    nk = D // tk

    def body(c_ref, w_ref, b_ref, out_ref, acc):
        kk = pl.program_id(2)
        part = lax.dot_general(c_ref[...].astype(BF16), w_ref[...].astype(BF16), DIMS_NN, preferred_element_type=F32)

        @pl.when(kk == 0)
        def _():
            acc[...] = part

        @pl.when(kk > 0)
        def _():
            acc[...] += part

        @pl.when(kk == nk - 1)
        def _():
            out_ref[...] = acc[...] + b_ref[...]

    return pl.pallas_call(
        body, name="mod_fwd", grid=(L, ncol // tn, nk),
        in_specs=[pl.BlockSpec((B, tk), lambda l, j, k: (0, k)),
                  pl.BlockSpec((None, tk, tn), lambda l, j, k: (l, k, j)),
                  pl.BlockSpec((None, 1, tn), lambda l, j, k: (l, 0, j))],
        out_specs=pl.BlockSpec((None, B, tn), lambda l, j, k: (l, 0, j)),
        out_shape=_sds((L, B, ncol), F32),
        scratch_shapes=[pltpu.VMEM((B, tn), F32)],
        compiler_params=_cparams("parallel", "parallel", "arbitrary"))(cond, w_mod, b_cols)


def _adamw_math(w, g, m, v):
    m = ADAM_B1 * m + (1.0 - ADAM_B1) * g
    v = ADAM_B2 * v + (1.0 - ADAM_B2) * (g * g)
    m_hat = m / (1.0 - ADAM_B1 ** ADAM_STEP)
    v_hat = v / (1.0 - ADAM_B2 ** ADAM_STEP)
    delta = -ADAM_LR * (m_hat / (jnp.sqrt(v_hat) + ADAM_EPS) + ADAM_WD * w)
    return delta, m, v


def _adamw(name, w, g, m, v):
    shape = w.shape
    cols = shape[-1] if w.ndim <= 3 else shape[-2] * shape[-1]
    rows = w.size // cols
    w2, g2, m2, v2 = (t.reshape(rows, cols) for t in (w, g, m, v))
    tr = _row_tile(rows, cols * 4, limit=1024 * 1024, mult=8)
    spec = pl.BlockSpec((tr, cols), lambda i: (i, 0))

    def body(w_ref, g_ref, m_ref, v_ref, d_ref, nm_ref, nv_ref):
        d, nm, nv = _adamw_math(w_ref[...], g_ref[...], m_ref[...], v_ref[...])
        d_ref[...] = d
        nm_ref[...] = nm
        nv_ref[...] = nv

    outs = pl.pallas_call(body, name=name, grid=(rows // tr,), in_specs=[spec] * 4, out_specs=[spec] * 3,
                          out_shape=[_sds((rows, cols), F32)] * 3, compiler_params=_cparams("parallel"))(w2, g2, m2, v2)
    return tuple(t.reshape(shape) for t in outs)


def _adamw_mod(w, cond_t, dmod_cols, m, v):
    L, D, ncol = w.shape
    B = cond_t.shape[1]
    tr, tc = min(256, D), min(1024, ncol)
    blk = pl.BlockSpec((None, tr, tc), lambda l, i, j: (l, i, j))

    def body(w_ref, ct_ref, dm_ref, m_ref, v_ref, g_ref, d_ref, nm_ref, nv_ref):
        g = lax.dot_general(ct_ref[...], dm_ref[...], DIMS_NN, precision=lax.Precision.HIGHEST,
                            preferred_element_type=F32)
        d, nm, nv = _adamw_math(w_ref[...], g, m_ref[...], v_ref[...])
        g_ref[...] = g
        d_ref[...] = d
        nm_ref[...] = nm
        nv_ref[...] = nv

    return pl.pallas_call(
        body, name="adamw_w_mod", grid=(L, D // tr, ncol // tc),
        in_specs=[blk, pl.BlockSpec((tr, B), lambda l, i, j: (i, 0)),
                  pl.BlockSpec((None, B, tc), lambda l, i, j: (l, 0, j)), blk, blk],
        out_specs=[blk] * 4, out_shape=[_sds((L, D, ncol), F32)] * 4,
        compiler_params=_cparams("parallel", "parallel", "parallel"))(w, cond_t, dmod_cols, m, v)


def _cast_bf16(name, w):
    shape = w.shape
    cols = shape[-1]
    rows = w.size // cols
    tr = _row_tile(rows, cols * 4)
    spec = pl.BlockSpec((tr, cols), lambda i: (i, 0))

    def body(w_ref, o_ref):
        o_ref[...] = w_ref[...].astype(BF16)

    out = pl.pallas_call(body, name=name, grid=(rows // tr,), in_specs=[spec], out_specs=spec,
                         out_shape=_sds((rows, cols), BF16), compiler_params=_cparams("parallel"))(w.reshape(rows, cols))
    return out.reshape(shape)


def _pair_sum(name, g5, ra, c_idx):
    L, A, _, Rh, Cc = g5.shape
    tr = _row_tile(Rh, Cc * 4)

    def body(c_ref, g_ref, r_ref, o_ref):
        o_ref[...] = (g_ref[...].astype(F32) + r_ref[...].astype(F32)).astype(BF16)

    blk = pl.BlockSpec((None, None, tr, Cc), lambda l, a, i, c_ref: (l, a, i, 0))
    return pl.pallas_call(
        body, name=name,
        grid_spec=pltpu.PrefetchScalarGridSpec(
            num_scalar_prefetch=1, grid=(L, A, Rh // tr),
            in_specs=[pl.BlockSpec((None, None, None, tr, Cc), lambda l, a, i, c_ref: (l, a, c_ref[0], i, 0)), blk],
            out_specs=blk),
        out_shape=_sds((L, A, Rh, Cc), BF16),
        compiler_params=_cparams("parallel", "parallel", "parallel"))(c_idx, g5, ra)


def _chip_sum(name, p, rb, k_idx, kind):
    L, A, Rh, Cc = p.shape
    C = rb.shape[-1]
    tr = _row_tile(Rh, C * 4)
    if kind == "row":
        own = pl.BlockSpec((None, None, tr, C), lambda l, i, k_ref: (l, k_ref[0], i, 0))
    else:
        own = pl.BlockSpec((None, None, tr, C), lambda l, i, k_ref: (l, 0, i, k_ref[0]))
    peer = lambda j: pl.BlockSpec((None, None, tr, C), lambda l, i, k_ref: (j, l, i, 0))

    def body(k_ref, p_ref, r0_ref, r1_ref, r2_ref, o_ref):
        o_ref[...] = ((p_ref[...].astype(F32) + r0_ref[...].astype(F32)) + r1_ref[...].astype(F32)) + r2_ref[...].astype(F32)

    return pl.pallas_call(
        body, name=name,
        grid_spec=pltpu.PrefetchScalarGridSpec(
            num_scalar_prefetch=1, grid=(L, Rh // tr),
            in_specs=[own, peer(0), peer(1), peer(2)],
            out_specs=pl.BlockSpec((None, tr, C), lambda l, i, k_ref: (l, i, 0))),
        out_shape=_sds((L, Rh, C), F32),
        compiler_params=_cparams("parallel", "parallel"))(k_idx, p, rb, rb, rb)


def _mesh_place():
    x, y, c = lax.axis_index("x"), lax.axis_index("y"), lax.axis_index("c")
    chips = [(1 - x, y), (x, 1 - y), (1 - x, 1 - y)]
    return x, y, c, chips


def _remote(src, dst, send_sem, recv_sem, to):
    return pltpu.make_async_remote_copy(src_ref=src, dst_ref=dst, send_sem=send_sem, recv_sem=recv_sem,
                                        device_id=to, device_id_type=MESH_ID)


def _small_allgather(name, v, with_sum=False):
    R, N = v.shape

    def body(*refs):
        if with_sum:
            x_ref, out_ref, sum_ref, send_sems, recv_sems, local_sem = refs
        else:
            x_ref, out_ref, send_sems, recv_sems, local_sem = refs
        x, y, c, chips = _mesh_place()
        me, sibling = (x, y, c), (x, y, 1 - c)

        def rows(px, py, pc):
            return out_ref.at[pl.ds((4 * px + 2 * py + pc) * R, R), :]

        def copy(k, block, to, src=None):
            return _remote(rows(*block) if src is None else src, rows(*block), send_sems.at[k], recv_sems.at[k], to)

        mine = pltpu.make_async_copy(x_ref, rows(*me), local_sem)
        mine.start()
        first = [copy(0, me, sibling, src=x_ref)]
        first += [copy(1 + j, me, (*chip, c), src=x_ref) for j, chip in enumerate(chips)]
        for cp in first:
            cp.start()
        passed = [copy(4 + j, (*chip, c), sibling) for j, chip in enumerate(chips)]
        for j, chip in enumerate(chips):
            copy(1 + j, (*chip, c), me).wait_recv()
            passed[j].start()
        copy(0, sibling, me).wait_recv()
        for j, chip in enumerate(chips):
            copy(4 + j, (*chip, 1 - c), me).wait_recv()
        for cp in first + passed:
            cp.wait_send()
        mine.wait()
        if with_sum:
            total = out_ref[0:R, :]
            for p in range(1, 8):
                total = total + out_ref[p * R:(p + 1) * R, :]
            sum_ref[...] = total

    vm = pl.BlockSpec(memory_space=pltpu.VMEM)
    out_shape = [_sds((8 * R, N), F32)] + ([_sds((R, N), F32)] if with_sum else [])
    res = pl.pallas_call(
        body, name=name, out_shape=out_shape, in_specs=[vm], out_specs=[vm] * len(out_shape),
        scratch_shapes=[pltpu.SemaphoreType.DMA((7,)), pltpu.SemaphoreType.DMA((7,)), pltpu.SemaphoreType.DMA],
        compiler_params=pltpu.CompilerParams(vmem_limit_bytes=VMEM_LIMIT_BYTES))(v)
    return res if with_sum else res[0]


def _full_place(ref, kind, C, kk, half):
    if kind == "row":
        return ref.at[:, kk, half]
    return ref.at[:, half, :, pl.ds(pl.multiple_of(kk * C, LANES), C)]


def _full_shard(ref, kind, C, kk):
    if kind == "row":
        return ref.at[:, kk]
    return ref.at[:, :, :, pl.ds(pl.multiple_of(kk * C, LANES), C)]


def _gather_weights(shards, kinds):
    n = len(shards)

    def body(*refs):
        ins, outs = refs[:n], refs[n:2 * n]
        send_sems, recv_sems, local_sems = refs[2 * n:]
        x, y, c, chips = _mesh_place()
        k = 2 * x + y
        me, sibling = (x, y, c), (x, y, 1 - c)

        def place(a, kk, half):
            return _full_place(outs[a], kinds[a], shards[a].shape[3], kk, half)

        def copy(a, j, src, dst, to):
            return _remote(src, dst, send_sems.at[6 * a + j], recv_sems.at[6 * a + j], to)

        local = []
        for a in range(n):
            cp = pltpu.make_async_copy(ins[a], _full_shard(outs[a], kinds[a], shards[a].shape[3], k), local_sems.at[a])
            cp.start()
            local.append(cp)
        sends = []
        for j, chip in enumerate(chips):
            for a in range(n):
                cp = copy(a, j, ins[a].at[:, c], place(a, k, c), (*chip, c))
                cp.start()
                sends.append(cp)
        for j, chip in enumerate(chips):
            kj = 2 * chip[0] + chip[1]
            for a in range(n):
                landed = place(a, kj, c)
                copy(a, j, landed, landed, me).wait_recv()
                cp = copy(a, 3 + j, landed, landed, sibling)
                cp.start()
                sends.append(cp)
        for j, chip in enumerate(chips):
            kj = 2 * chip[0] + chip[1]
            for a in range(n):
                other = place(a, kj, 1 - c)
                copy(a, 3 + j, other, other, me).wait_recv()
        for cp in sends:
            cp.wait_send()
        for cp in local:
            cp.wait()

    out_shape = []
    for s, kind in zip(shards, kinds):
        L, _, Rh, C = s.shape
        out_shape.append(_sds((L, N_CHIPS, 2, Rh, C) if kind == "row" else (L, 2, Rh, N_CHIPS * C), BF16))
    return pl.pallas_call(
        body, name="gather_weights", out_shape=out_shape, in_specs=[ANY] * n, out_specs=[ANY] * n,
        scratch_shapes=[pltpu.SemaphoreType.DMA((6 * n,)), pltpu.SemaphoreType.DMA((6 * n,)),
                        pltpu.SemaphoreType.DMA((n,))],
        compiler_params=pltpu.CompilerParams(vmem_limit_bytes=VMEM_LIMIT_BYTES))(*shards)


def _pair_exchange(g5s):
    n = len(g5s)

    def body(*refs):
        ins, outs = refs[:n], refs[n:2 * n]
        send_sems, recv_sems = refs[2 * n:]
        x, y, c, _ = _mesh_place()
        cps = []
        for a in range(n):
            cp = _remote(ins[a].at[:, :, 1 - c], outs[a], send_sems.at[a], recv_sems.at[a], (x, y, 1 - c))
            cp.start()
            cps.append(cp)
        for cp in cps:
            cp.wait_recv()
        for cp in cps:
            cp.wait_send()

    out_shape = [_sds((g.shape[0], g.shape[1], g.shape[3], g.shape[4]), BF16) for g in g5s]
    return pl.pallas_call(
        body, name="grad_pair_exchange", out_shape=out_shape, in_specs=[ANY] * n, out_specs=[ANY] * n,
        scratch_shapes=[pltpu.SemaphoreType.DMA((n,)), pltpu.SemaphoreType.DMA((n,))],
        compiler_params=pltpu.CompilerParams(vmem_limit_bytes=VMEM_LIMIT_BYTES))(*g5s)


def _chip_scatter(ps, kinds, shard_cols):
    n = len(ps)

    def body(*refs):
        ins, outs = refs[:n], refs[n:2 * n]
        send_sems, recv_sems = refs[2 * n:]
        x, y, c, chips = _mesh_place()
        cps = []
        for j, chip in enumerate(chips):
            kj = 2 * chip[0] + chip[1]
            for a in range(n):
                C = shard_cols[a]
                src = ins[a].at[:, kj] if kinds[a] == "row" else ins[a].at[:, 0, :, pl.ds(pl.multiple_of(kj * C, LANES), C)]
                cp = _remote(src, outs[a].at[j], send_sems.at[3 * a + j], recv_sems.at[3 * a + j], (*chip, c))
                cp.start()
                cps.append(cp)
        for cp in cps:
            cp.wait_recv()
        for cp in cps:
            cp.wait_send()

    out_shape = [_sds((3, p.shape[0], p.shape[2], C), BF16) for p, C in zip(ps, shard_cols)]
    return pl.pallas_call(
        body, name="grad_chip_scatter", out_shape=out_shape, in_specs=[ANY] * n, out_specs=[ANY] * n,
        scratch_shapes=[pltpu.SemaphoreType.DMA((3 * n,)), pltpu.SemaphoreType.DMA((3 * n,))],
        compiler_params=pltpu.CompilerParams(vmem_limit_bytes=VMEM_LIMIT_BYTES))(*ps)


def _pair_share(fs):
    n = len(fs)

    def body(*refs):
        ins, outs = refs[:n], refs[n:2 * n]
        send_sems, recv_sems, local_sems = refs[2 * n:]
        x, y, c, _ = _mesh_place()
        cps, local = [], []
        for a in range(n):
            lc = pltpu.make_async_copy(ins[a], outs[a].at[:, c], local_sems.at[a])
            lc.start()
            local.append(lc)
            cp = _remote(ins[a], outs[a].at[:, c], send_sems.at[a], recv_sems.at[a], (x, y, 1 - c))
            cp.start()
            cps.append(cp)
        for a in range(n):
            theirs = outs[a].at[:, 1 - c]
            _remote(theirs, theirs, send_sems.at[a], recv_sems.at[a], (x, y, c)).wait_recv()
        for cp in cps:
            cp.wait_send()
        for lc in local:
            lc.wait()

    out_shape = [_sds((f.shape[0], 2, f.shape[1], f.shape[2]), F32) for f in fs]
    return pl.pallas_call(
        body, name="grad_pair_share", out_shape=out_shape, in_specs=[ANY] * n, out_specs=[ANY] * n,
        scratch_shapes=[pltpu.SemaphoreType.DMA((n,)), pltpu.SemaphoreType.DMA((n,)), pltpu.SemaphoreType.DMA((n,))],
        compiler_params=pltpu.CompilerParams(vmem_limit_bytes=VMEM_LIMIT_BYTES))(*fs)


def _pack_rows(parts, lane_mult=1024):
    flat = jnp.concatenate([p.reshape(-1).astype(F32) for p in parts])
    n = -(-flat.shape[0] // (8 * lane_mult)) * lane_mult
    return jnp.pad(flat, (0, 8 * n - flat.shape[0])).reshape(8, n)


def _relu2(acc):
    r = jnp.maximum(acc, 0.0)
    return r, r * r


def _times_2r(acc, r):
    return (acc * (2.0 * r.astype(F32)),)


def kernel(x, c, positions, w_mod, b_mod, norm_g, mla_w_in, mla_g_q, mla_g_kv, mla_w_uq, mla_w_ukv, mla_w_o, conv_w_in, conv_w, conv_w_out, mlp_w_up, mlp_w_down, loss_target, m_w_mod, m_b_mod, m_norm_g, m_mla_w_in, m_mla_g_q, m_mla_g_kv, m_mla_w_uq, m_mla_w_ukv, m_mla_w_o, m_conv_w_in, m_conv_w, m_conv_w_out, m_mlp_w_up, m_mlp_w_down, v_w_mod, v_b_mod, v_norm_g, v_mla_w_in, v_mla_g_q, v_mla_g_kv, v_mla_w_uq, v_mla_w_ukv, v_mla_w_o, v_conv_w_in, v_conv_w, v_conv_w_out, v_mlp_w_up, v_mlp_w_down):
    S, D = x.shape[1], x.shape[2]
    Dq = D // N_CHIPS
    ncol = w_mod.shape[2]
    n_mod = N_CHIPS * ncol // D
    F = mlp_w_up.shape[2] * N_CHIPS
    lat_dim = mla_w_in.shape[2]
    rank = mla_g_q.shape[1]
    H = mla_w_uq.shape[2]
    d_qk = mla_w_uq.shape[3]
    assert mla_g_kv.shape[1] == rank and lat_dim == 2 * rank + QK_ROPE and d_qk == QK_NOPE + QK_ROPE
    assert mla_w_ukv.shape[3] == QK_NOPE + V_HEAD and x.shape[0] == 1 and n_mod == 6
    assert norm_g.shape[0] == 2 and mla_w_in.shape[0] == 1 and conv_w_in.shape[0] == 1
    lat_pad = 2 * rank + LANES
    scale = float(d_qk) ** -0.5

    xi, yi, ci = lax.axis_index("x"), lax.axis_index("y"), lax.axis_index("c")
    chip = 2 * xi + yi
    dev = 2 * chip + ci
    c_idx = jnp.reshape(ci, (1,)).astype(jnp.int32)
    k_idx = jnp.reshape(chip, (1,)).astype(jnp.int32)

    n1 = D + 2 * D + 3 * Dq
    g1 = _small_allgather("gather_small_inputs", _pack_rows([c, norm_g, conv_w])).reshape(8, -1)
    c_all = g1[:, :D]
    by_chip = g1[0::2]
    norm_full = jnp.concatenate([by_chip[kk, D:3 * D].reshape(2, 4, Dq) for kk in range(N_CHIPS)], axis=-1)
    convw_full = jnp.concatenate([by_chip[kk, 3 * D:n1].reshape(3, Dq) for kk in range(N_CHIPS)], axis=-1)

    b_cols = lax.dynamic_slice(b_mod, (0, chip * ncol), (2, ncol)).reshape(2, 1, ncol)
    cond_all = _silu(c_all)
    mod_cols = _mod_fwd(cond_all, w_mod, b_cols)
    g2 = _small_allgather("gather_mod", _pack_rows([mod_cols]))
    g2 = g2.reshape(8, -1)[0::2, :2 * 8 * ncol].reshape(N_CHIPS, 2, 8, ncol)
    mod_all = jnp.transpose(g2, (2, 1, 0, 3)).reshape(8, 2, n_mod * D)
    mod_me = lax.dynamic_index_in_dim(mod_all, dev, axis=0, keepdims=False)
    mods = [[mod_me[l, i * D:(i + 1) * D].reshape(1, D) for i in range(n_mod)] for l in range(2)]
    ng = [[norm_full[l, i].reshape(1, D) for i in range(4)] for l in range(2)]

    pos = positions[0].astype(F32)
    inv_freq = ROPE_THETA ** (-jnp.arange(0, QK_ROPE, 2, dtype=F32) / QK_ROPE)
    ang = pos[:, None] * inv_freq
    cos, sin = jnp.cos(ang), jnp.sin(ang)
    zero = jnp.zeros_like(cos)
    rope_tabs = (jnp.concatenate([cos, cos, zero, zero], axis=1),
                 jnp.concatenate([-sin, zero, zero, zero], axis=1),
                 jnp.concatenate([zero, sin, zero, zero], axis=1))

    weights = [("mla_w_in", mla_w_in, "row"), ("mla_w_uq", mla_w_uq.reshape(1, rank // N_CHIPS, H * d_qk), "row"),
               ("mla_w_ukv", mla_w_ukv.reshape(1, rank // N_CHIPS, H * QK_PAD), "row"), ("mla_w_o", mla_w_o, "row"),
               ("conv_w_in", conv_w_in, "col"), ("conv_w_out", conv_w_out, "row"),
               ("mlp_w_up", mlp_w_up, "col"), ("mlp_w_down", mlp_w_down, "row")]
    kinds = [k for _, _, k in weights]
    shard_shapes = [w.shape for _, w, _ in weights]
    shards = [_cast_bf16("cast_" + nm, w).reshape(w.shape[0], 2, w.shape[1] // 2, w.shape[2]) for nm, w, _ in weights]
    full = _gather_weights(shards, kinds)

    def full2d(i):
        L, R, C = shard_shapes[i]
        return full[i].reshape((L, N_CHIPS * R, C) if kinds[i] == "row" else (L, R, N_CHIPS * C))

    w_in_p = jnp.pad(full2d(0)[0], ((0, 0), (0, lat_pad - lat_dim)))
    w_q_p = jnp.pad(full2d(1)[0].reshape(rank, H, d_qk), ((0, 0), (0, 0), (0, QK_PAD - d_qk))).reshape(rank, H * QK_PAD)
    w_ukv, w_o, w_cin, w_cout = full2d(2)[0], full2d(3)[0], full2d(4)[0], full2d(5)[0]
    w_up, w_down = full2d(6), full2d(7)
    HV = H * V_HEAD

    def lead(l, shape_fn):
        return lambda tm, tn, tk: shape_fn(l, tm, tn, tk)

    def mlp_fwd(tag, l, h):
        r, a2 = _mm("mlp_up_" + tag, h, w_up, "nn", S, F, D, [_sds((S, F), BF16)] * 2, epilogue=_relu2,
                    b_spec=lead(l, lambda l, tm, tn, tk: pl.BlockSpec((None, tk, tn), lambda i, j, k: (l, k, j))))
        (y,) = _mm("mlp_down_" + tag, a2, w_down, "nn", S, D, F, [_sds((S, D), F32)],
                   b_spec=lead(l, lambda l, tm, tn, tk: pl.BlockSpec((None, tk, tn), lambda i, j, k: (l, k, j))))
        return r, a2, y

    def mlp_bwd(tag, l, h, r, a2, dy, dw_up_prev, dw_down_prev):
        (da,) = _mm("mlp_down_dx_" + tag, dy, w_down, "nt", S, F, D, [_sds((S, F), BF16)], epilogue=_times_2r,
                    b_spec=lead(l, lambda l, tm, tn, tk: pl.BlockSpec((None, tn, tk), lambda i, j, k: (l, j, k))),
                    extras=[(r, lambda tm, tn, tk: pl.BlockSpec((tm, tn), lambda i, j, k: (i, j)))])
        (dw_down,) = _mm("mlp_down_dw_" + tag, a2, dy, "tn", F, D, S, [_sds((2, F, D), BF16)], alias=dw_down_prev,
                         out_specs=[lead(l, lambda l, tm, tn, tk: pl.BlockSpec((None, tm, tn), lambda i, j, k: (l, i, j)))])
        (dh,) = _mm("mlp_up_dx_" + tag, da, w_up, "nt", S, D, F, [_sds((S, D), F32)],
                    b_spec=lead(l, lambda l, tm, tn, tk: pl.BlockSpec((None, tn, tk), lambda i, j, k: (l, j, k))))
        (dw_up,) = _mm("mlp_up_dw_" + tag, h, da, "tn", D, F, S, [_sds((2, D, F), BF16)], alias=dw_up_prev,
                       out_specs=[lead(l, lambda l, tm, tn, tk: pl.BlockSpec((None, tm, tn), lambda i, j, k: (l, i, j)))])
        return dh, dw_up, dw_down

    x0 = x[0]
    sh1, sc1, gt1, sh2, sc2, gt2 = mods[0]
    (h1,) = _fwd_boundary("fwd_boundary_0", x0, None, None, None, ng[0][0], sc1, sh1)
    (lat,) = _mm("mla_in", h1, w_in_p, "nn", S, lat_pad, D, [_sds((S, lat_pad), F32)], tn=lat_pad)
    cq, ckv, kr = _latent_fwd(lat, mla_g_q, mla_g_kv, rope_tabs, rank)

    def rope_q(acc, cos_p, sin_lo, sin_hi):
        parts = []
        for hh in range(acc.shape[1] // QK_PAD):
            parts.append(acc[:, hh * QK_PAD:hh * QK_PAD + QK_NOPE])
            parts.append(_rope(acc[:, hh * QK_PAD + QK_NOPE:(hh + 1) * QK_PAD], cos_p, sin_lo, sin_hi))
        return (jnp.concatenate(parts, axis=1),)

    tab_extra = lambda tm, tn, tk: pl.BlockSpec((tm, LANES), lambda i, j, k: (i, 0))
    (q,) = _mm("mla_q", cq, w_q_p, "nn", S, H * QK_PAD, rank, [_sds((S, H * QK_PAD), BF16)], epilogue=rope_q,
               extras=[(t, tab_extra) for t in rope_tabs], tn=2 * QK_PAD)
    (kv,) = _mm("mla_kv", ckv, w_ukv, "nn", S, H * QK_PAD, rank, [_sds((S, H * QK_PAD), BF16)])
    o, lse = _attn_fwd(q, kv, kr, H, scale)
    (y1,) = _mm("mla_out", o, w_o, "nn", S, D, HV, [_sds((S, D), F32)])
    x1, h2 = _fwd_boundary("fwd_boundary_1", x0, y1, gt1, ng[0][1], ng[0][2], sc2, sh2)
    r2, a2, y2 = mlp_fwd("0", 0, h2)

    sh1b, sc1b, gt1b, sh2b, sc2b, gt2b = mods[1]
    x2, h3 = _fwd_boundary("fwd_boundary_2", x1, y2, gt2, ng[0][3], ng[1][0], sc1b, sh1b)
    nD = lambda tn: D // tn
    (proj3,) = _mm("conv_in", h3, w_cin, "nn", S, 3 * D, D, [_sds((3, S, D), F32)], tn=min(1024, D),
                   out_specs=[lambda tm, tn, tk: pl.BlockSpec((None, tm, tn), lambda i, j, k: (j // nD(tn), i, j % nD(tn)))])
    bz = _conv_fwd(proj3, convw_full)
    (y3,) = _mm("conv_out", bz, w_cout, "nn", S, D, D, [_sds((S, D), F32)])
    x3, h4 = _fwd_boundary("fwd_boundary_3", x2, y3, gt1b, ng[1][1], ng[1][2], sc2b, sh2b)
    r4, a4, y4 = mlp_fwd("1", 1, h4)

    dx4, dy4, sums_l, loss_acc = _loss_boundary("loss_boundary", x3, y4, gt2b, ng[1][3], loss_target[0])
    loss = lax.psum(loss_acc[0, 0], ("x", "y", "c"))

    dh4, dw_up, dw_down = mlp_bwd("1", 1, h4, r4, a4, dy4, None, None)
    dx3, dy3, sums_3 = _bwd_boundary("bwd_boundary_3", dx4, dh4, x3, y3, gt1b, ng[1][1], ng[1][2], sc2b)

    (dbz,) = _mm("conv_out_dx", dy3, w_cout, "nt", S, D, D, [_sds((S, D), F32)])
    (dw_cout,) = _mm("conv_out_dw", bz, dy3, "tn", D, D, S, [_sds((D, D), BF16)])
    dproj3, dconvw = _conv_bwd(dbz, proj3, convw_full)
    (dh3,) = _mm("conv_in_dx", dproj3, w_cin, "nt", S, D, 3 * D, [_sds((S, D), F32)], tk=min(1024, D),
                 a_spec=lambda tm, tn, tk: pl.BlockSpec((None, tm, tk), lambda i, j, k: (k // (D // tk), i, k % (D // tk))))
    (dw_cin,) = _mm("conv_in_dw", h3, dproj3, "tn", D, 3 * D, S, [_sds((D, 3 * D), BF16)], tn=min(1024, D),
                    b_spec=lambda tm, tn, tk: pl.BlockSpec((None, tk, tn), lambda i, j, k: (j // nD(tn), k, j % nD(tn))))
    dx2, dy2, sums_2 = _bwd_boundary("bwd_boundary_2", dx3, dh3, x2, y2, gt2, ng[0][3], ng[1][0], sc1b)

    dh2, dw_up, dw_down = mlp_bwd("0", 0, h2, r2, a2, dy2, dw_up, dw_down)
    dx1, dy1, sums_1 = _bwd_boundary("bwd_boundary_1", dx2, dh2, x1, y1, gt1, ng[0][1], ng[0][2], sc2)

    (do,) = _mm("mla_out_dx", dy1, w_o, "nt", S, HV, D, [_sds((S, HV), BF16)])
    (dw_o,) = _mm("mla_out_dw", o, dy1, "tn", HV, D, S, [_sds((HV, D), BF16)])
    dq, dkv, dkr = _attn_bwd(q, kv, kr, o, do, lse, rope_tabs, H, scale)
    (dcq,) = _mm("mla_q_dx", dq, w_q_p, "nt", S, rank, H * QK_PAD, [_sds((S, rank), F32)])
    (dw_q_p,) = _mm("mla_q_dw", cq, dq, "tn", rank, H * QK_PAD, S, [_sds((rank, H * QK_PAD), BF16)])
    (dckv,) = _mm("mla_kv_dx", dkv, w_ukv, "nt", S, rank, H * QK_PAD, [_sds((S, rank), F32)])
    (dw_ukv,) = _mm("mla_kv_dw", ckv, dkv, "tn", rank, H * QK_PAD, S, [_sds((rank, H * QK_PAD), BF16)])
    dlat, sums_lat = _latent_bwd(lat, dcq, dckv, dkr, mla_g_q, mla_g_kv, rope_tabs, rank)
    (dh1,) = _mm("mla_in_dx", dlat, w_in_p, "nt", S, D, lat_pad, [_sds((S, D), F32)])
    (dw_in_p,) = _mm("mla_in_dw", h1, dlat, "tn", D, lat_pad, S, [_sds((D, lat_pad), BF16)], tn=lat_pad)
    grad_x, sums_0 = _bwd_boundary("bwd_boundary_0", dx1, dh1, x0, None, None, None, ng[0][0], sc1)

    dmod0 = [sums_0[0], sums_0[1], sums_1[3], sums_1[0], sums_1[1], sums_2[3]]
    dmod1 = [sums_2[0], sums_2[1], sums_3[3], sums_3[0], sums_3[1], sums_l[3]]
    dng0 = [sums_0[2], sums_1[4], sums_1[2], sums_2[4]]
    dng1 = [sums_2[2], sums_3[4], sums_3[2], sums_l[4]]
    small = _pack_rows(dmod0 + dmod1 + dng0 + dng1 + [sums_lat[0], sums_lat[1], dconvw], lane_mult=LANES)
    gathered, total = _small_allgather("gather_small_grads", small, with_sum=True)
    n_dm = 2 * n_mod * D
    dmod_all = gathered.reshape(8, -1)[:, :n_dm].reshape(8, 2, n_mod * D)
    total = total.reshape(-1)
    g_b_mod = total[:n_dm].reshape(2, n_mod * D)
    g_norm = lax.dynamic_slice(total[n_dm:n_dm + 8 * D].reshape(2, 4, D), (0, 0, chip * Dq), (2, 4, Dq))
    off = n_dm + 8 * D
    g_gq = total[off:off + rank].reshape(1, rank)
    g_gkv = total[off + rank:off + 2 * rank].reshape(1, rank)
    off += 2 * rank
    g_convw = lax.dynamic_slice(total[off:off + 3 * D].reshape(1, 3, D), (0, 0, chip * Dq), (1, 3, Dq))

    dw_list = [dw_in_p[:, :lat_dim][None], dw_q_p.reshape(rank, H, QK_PAD)[:, :, :d_qk].reshape(1, rank, H * d_qk),
               dw_ukv[None], dw_o[None], dw_cin[None], dw_cout[None], dw_up, dw_down]
    g5s = []
    for (L, R, C), kind, g in zip(shard_shapes, kinds, dw_list):
        g5s.append(g.reshape((L, N_CHIPS, 2, R // 2, C) if kind == "row" else (L, 1, 2, R // 2, N_CHIPS * C)))
    ras = _pair_exchange(g5s)
    ps = [_pair_sum("pair_sum_" + nm, g5, ra, c_idx) for (nm, _, _), g5, ra in zip(weights, g5s, ras)]
    rbs = _chip_scatter(ps, kinds, [s[2] for s in shard_shapes])
    fs = [_chip_sum("chip_sum_" + nm, p, rb, k_idx, kind) for (nm, _, kind), p, rb in zip(weights, ps, rbs)]
    finals = _pair_share(fs)
    orig = [mla_w_in, mla_w_uq, mla_w_ukv, mla_w_o, conv_w_in, conv_w_out, mlp_w_up, mlp_w_down]
    big_grads = [f.reshape(w.shape) for f, w in zip(finals, orig)]

    dmod_cols = jnp.transpose(lax.dynamic_slice(dmod_all.reshape(8, 2, N_CHIPS, ncol), (0, 0, chip, 0), (8, 2, 1, ncol))
                              .reshape(8, 2, ncol), (1, 0, 2))
    g_w_mod, d_w_mod, nm_w_mod, nv_w_mod = _adamw_mod(w_mod, cond_all.T, dmod_cols, m_w_mod, v_w_mod)

    names = ["b_mod", "norm_g", "mla_w_in", "mla_g_q", "mla_g_kv", "mla_w_uq", "mla_w_ukv", "mla_w_o",
             "conv_w_in", "conv_w", "conv_w_out", "mlp_w_up", "mlp_w_down"]
    ws = [b_mod, norm_g, mla_w_in, mla_g_q, mla_g_kv, mla_w_uq, mla_w_ukv, mla_w_o, conv_w_in, conv_w, conv_w_out,
          mlp_w_up, mlp_w_down]
    ms = [m_b_mod, m_norm_g, m_mla_w_in, m_mla_g_q, m_mla_g_kv, m_mla_w_uq, m_mla_w_ukv, m_mla_w_o, m_conv_w_in,
          m_conv_w, m_conv_w_out, m_mlp_w_up, m_mlp_w_down]
    vs = [v_b_mod, v_norm_g, v_mla_w_in, v_mla_g_q, v_mla_g_kv, v_mla_w_uq, v_mla_w_ukv, v_mla_w_o, v_conv_w_in,
          v_conv_w, v_conv_w_out, v_mlp_w_up, v_mlp_w_down]
    gs = [g_b_mod, g_norm, big_grads[0], g_gq, g_gkv, big_grads[1], big_grads[2], big_grads[3], big_grads[4],
          g_convw, big_grads[5], big_grads[6], big_grads[7]]
    grads, deltas, new_ms, new_vs = [g_w_mod], [d_w_mod], [nm_w_mod], [nv_w_mod]
    for nm, w, g, m, v in zip(names, ws, gs, ms, vs):
        d, nm_, nv_ = _adamw("adamw_" + nm, w, g, m, v)
        grads.append(g)
        deltas.append(d)
        new_ms.append(nm_)
        new_vs.append(nv_)
    return (loss, grad_x[None], *grads, *deltas, *new_ms, *new_vs)
```

```python
import jax
import jax.numpy as jnp
from jax import lax
from jax.experimental import pallas as pl
from jax.experimental.pallas import tpu as pltpu

F32 = jnp.float32
BF16 = jnp.bfloat16
NORM_EPS = 1e-6
ROPE_THETA = 10000.0
QK_NOPE = 128
QK_ROPE = 64
V_HEAD = 128
LANES = 128
QK_PAD = QK_NOPE + LANES
ADAM_LR, ADAM_B1, ADAM_B2, ADAM_EPS, ADAM_WD, ADAM_STEP = 0.001, 0.9, 0.999, 1e-08, 0.01, 10
VMEM_LIMIT_BYTES = 48 * 1024 * 1024
N_CHIPS = 4
MESH_ID = pl.DeviceIdType.MESH
ANY = pl.BlockSpec(memory_space=pl.ANY)
NEG_INF = float("-inf")

DIMS_NN = (((1,), (0,)), ((), ()))
DIMS_NT = (((1,), (1,)), ((), ()))
DIMS_TN = (((0,), (0,)), ((), ()))


def _cparams(*sem):
    return pltpu.CompilerParams(dimension_semantics=sem, vmem_limit_bytes=VMEM_LIMIT_BYTES)


def _row_tile(rows, row_bytes, limit=2 * 1024 * 1024, mult=16):
    if rows * row_bytes <= limit or rows % mult:
        return rows
    best = mult
    t = mult
    while t <= rows:
        if rows % t == 0 and t * row_bytes <= limit:
            best = t
        t += mult
    return best


def _rms(v):
    return lax.rsqrt(jnp.mean(v * v, axis=-1, keepdims=True) + NORM_EPS)


def _mm(name, a, b, mode, M, N, K, outs, *, a_spec=None, b_spec=None, out_specs=None, epilogue=None,
        extras=(), alias=None, tm=1024, tn=1024, tk=2048):
    if mode == "tn":
        tk = min(tk, 512)
    tm, tn, tk = min(tm, M), min(tn, N), min(tk, K)
    assert M % tm == 0 and N % tn == 0 and K % tk == 0, (name, M, N, K)
    nk = K // tk
    if a_spec is None:
        a_spec = {"nn": pl.BlockSpec((tm, tk), lambda i, j, k: (i, k)),
                  "nt": pl.BlockSpec((tm, tk), lambda i, j, k: (i, k)),
                  "tn": pl.BlockSpec((tk, tm), lambda i, j, k: (k, i))}[mode]
    else:
        a_spec = a_spec(tm, tn, tk)
    if b_spec is None:
        b_spec = {"nn": pl.BlockSpec((tk, tn), lambda i, j, k: (k, j)),
                  "nt": pl.BlockSpec((tn, tk), lambda i, j, k: (j, k)),
                  "tn": pl.BlockSpec((tk, tn), lambda i, j, k: (k, j))}[mode]
    else:
        b_spec = b_spec(tm, tn, tk)
    if out_specs is None:
        out_specs = [pl.BlockSpec((tm, tn), lambda i, j, k: (i, j)) for _ in outs]
    else:
        out_specs = [s(tm, tn, tk) for s in out_specs]
    dims = {"nn": DIMS_NN, "nt": DIMS_NT, "tn": DIMS_TN}[mode]
    ne, no = len(extras), len(outs)

    def body(*refs):
        a_ref, b_ref = refs[0], refs[1]
        ex = refs[2:2 + ne]
        o = refs[2 + ne + (alias is not None):2 + ne + (alias is not None) + no]
        part = lax.dot_general(a_ref[...].astype(BF16), b_ref[...].astype(BF16), dims,
                               preferred_element_type=F32)

        def finish(total):
            vals = epilogue(total, *[e[...] for e in ex]) if epilogue is not None else (total,)
            for r, v in zip(o, vals):
                r[...] = v.astype(r.dtype)

        if nk == 1:
            finish(part)
        else:
            acc = refs[-1]
            kk = pl.program_id(2)

            @pl.when(kk == 0)
            def _():
                acc[...] = part

            @pl.when(kk > 0)
            def _():
                acc[...] += part

            @pl.when(kk == nk - 1)
            def _():
                finish(acc[...])

    operands = [a, b] + [e[0] for e in extras]
    in_specs = [a_spec, b_spec] + [e[1](tm, tn, tk) for e in extras]
    aliases = {}
    if alias is not None:
        operands.append(alias)
        in_specs.append(ANY)
        aliases = {len(operands) - 1: 0}
    res = pl.pallas_call(
        body, name=name, grid=(M // tm, N // tn, nk),
        in_specs=in_specs, out_specs=out_specs, out_shape=list(outs),
        scratch_shapes=[pltpu.VMEM((tm, tn), F32)] if nk > 1 else [],
        input_output_aliases=aliases,
        compiler_params=_cparams("parallel", "parallel", "arbitrary"),
    )(*operands)
    return res


def _sds(shape, dtype):
    return jax.ShapeDtypeStruct(tuple(shape), dtype)


def _rope(t, cos_p, sin_lo, sin_hi):
    return t * cos_p + pltpu.roll(t, LANES - QK_ROPE // 2, 1) * sin_lo + pltpu.roll(t, QK_ROPE // 2, 1) * sin_hi


def _rope_t(d, cos_p, sin_lo, sin_hi):
    return d * cos_p + pltpu.roll(d * sin_lo, QK_ROPE // 2, 1) + pltpu.roll(d * sin_hi, LANES - QK_ROPE // 2, 1)


def _vec_spec(d):
    return pl.BlockSpec((1, d), lambda i: (0, 0))


def _fwd_boundary(name, x_prev, y, gate, ng_post, ng_pre, sc, sh):
    S, D = x_prev.shape
    ts = min(256, S)
    has_y = y is not None
    row = pl.BlockSpec((ts, D), lambda i: (i, 0))

    def body(*refs):
        if has_y:
            x_ref, y_ref, g_ref, ngp_ref, ngn_ref, sc_ref, sh_ref, xo_ref, h_ref = refs
            yv = y_ref[...]
            xn = x_ref[...] + g_ref[...] * (yv * _rms(yv) * ngp_ref[...])
            xo_ref[...] = xn
        else:
            x_ref, ngn_ref, sc_ref, sh_ref, h_ref = refs
            xn = x_ref[...]
        hn = xn * _rms(xn) * ngn_ref[...]
        h_ref[...] = (hn * (1.0 + sc_ref[...]) + sh_ref[...]).astype(BF16)

    vec = _vec_spec(D)
    if has_y:
        operands = (x_prev, y, gate, ng_post, ng_pre, sc, sh)
        in_specs = [row, row, vec, vec, vec, vec, vec]
        out_shape = [_sds((S, D), F32), _sds((S, D), BF16)]
        out_specs = [row, row]
    else:
        operands = (x_prev, ng_pre, sc, sh)
        in_specs = [row, vec, vec, vec]
        out_shape = [_sds((S, D), BF16)]
        out_specs = [row]
    return pl.pallas_call(body, name=name, grid=(S // ts,), in_specs=in_specs, out_specs=out_specs,
                          out_shape=out_shape, compiler_params=_cparams("parallel"))(*operands)


def _acc_rows(sums_ref, rows):
    for r, v in rows:
        sums_ref[r:r + 1, :] += jnp.sum(v, axis=0, keepdims=True)


def _post_norm_bwd(dxt, yv, gate, ng_post, sums_ref, dy_ref):
    r1 = _rms(yv)
    yhat = yv * r1
    dn = dxt * gate
    u = dn * ng_post
    dy = r1 * (u - yhat * jnp.mean(u * yhat, axis=-1, keepdims=True))
    dy_ref[...] = dy.astype(dy_ref.dtype)
    _acc_rows(sums_ref, [(3, dxt * (yhat * ng_post)), (4, dn * yhat)])


def _loss_boundary(name, x_prev, y, gate, ng_post, target):
    S, D = x_prev.shape
    ts = min(256, S)
    row = pl.BlockSpec((ts, D), lambda i: (i, 0))
    vec = _vec_spec(D)

    def body(x_ref, y_ref, g_ref, ngp_ref, t_ref, dx_ref, dy_ref, sums_ref, loss_ref):
        @pl.when(pl.program_id(0) == 0)
        def _():
            sums_ref[...] = jnp.zeros_like(sums_ref)
            loss_ref[...] = jnp.zeros_like(loss_ref)

        yv = y_ref[...]
        xf = x_ref[...] + g_ref[...] * (yv * _rms(yv) * ngp_ref[...])
        err = xf - t_ref[...]
        loss_ref[...] += 0.5 * jnp.sum(jnp.mean(err * err, axis=-1, keepdims=True))
        dxt = err / D
        dx_ref[...] = dxt
        _post_norm_bwd(dxt, yv, g_ref[...], ngp_ref[...], sums_ref, dy_ref)

    return pl.pallas_call(
        body, name=name, grid=(S // ts,),
        in_specs=[row, row, vec, vec, row],
        out_specs=[row, row, pl.BlockSpec((8, D), lambda i: (0, 0)), pl.BlockSpec((8, LANES), lambda i: (0, 0))],
        out_shape=[_sds((S, D), F32), _sds((S, D), BF16), _sds((8, D), F32), _sds((8, LANES), F32)],
        compiler_params=_cparams("arbitrary"))(x_prev, y, gate, ng_post, target)


def _bwd_boundary(name, dx_new, dh, x_new, y, gate, ng_post, ng_pre, sc):
    S, D = x_new.shape
    ts = min(256, S)
    has_y = y is not None
    row = pl.BlockSpec((ts, D), lambda i: (i, 0))
    vec = _vec_spec(D)

    def body(*refs):
        if has_y:
            dxn_ref, dh_ref, x_ref, y_ref, g_ref, ngp_ref, ngn_ref, sc_ref, dxo_ref, dy_ref, sums_ref = refs
        else:
            dxn_ref, dh_ref, x_ref, ngn_ref, sc_ref, dxo_ref, sums_ref = refs

        @pl.when(pl.program_id(0) == 0)
        def _():
            sums_ref[...] = jnp.zeros_like(sums_ref)

        xv = x_ref[...]
        dhv = dh_ref[...]
        ngn = ngn_ref[...]
        r2 = _rms(xv)
        xhat = xv * r2
        dn_pre = dhv * (1.0 + sc_ref[...])
        u2 = dn_pre * ngn
        dxt = dxn_ref[...] + r2 * (u2 - xhat * jnp.mean(u2 * xhat, axis=-1, keepdims=True))
        dxo_ref[...] = dxt
        _acc_rows(sums_ref, [(0, dhv), (1, dhv * (xhat * ngn)), (2, dn_pre * xhat)])
        if has_y:
            _post_norm_bwd(dxt, y_ref[...], g_ref[...], ngp_ref[...], sums_ref, dy_ref)

    sums_spec = pl.BlockSpec((8, D), lambda i: (0, 0))
    if has_y:
        operands = (dx_new, dh, x_new, y, gate, ng_post, ng_pre, sc)
        in_specs = [row, row, row, row, vec, vec, vec, vec]
        out_shape = [_sds((S, D), F32), _sds((S, D), BF16), _sds((8, D), F32)]
        out_specs = [row, row, sums_spec]
    else:
        operands = (dx_new, dh, x_new, ng_pre, sc)
        in_specs = [row, row, row, vec, vec]
        out_shape = [_sds((S, D), F32), _sds((8, D), F32)]
        out_specs = [row, sums_spec]
    return pl.pallas_call(body, name=name, grid=(S // ts,), in_specs=in_specs, out_specs=out_specs,
                          out_shape=out_shape, compiler_params=_cparams("arbitrary"))(*operands)


def _latent_fwd(lat, g_q, g_kv, rope_tabs, rank):
    S, W = lat.shape
    ts = min(256, S)
    tab = pl.BlockSpec((ts, LANES), lambda i: (i, 0))

    def body(lat_ref, gq_ref, gkv_ref, cos_ref, slo_ref, shi_ref, cq_ref, ckv_ref, kr_ref):
        lq = lat_ref[:, 0:rank]
        lkv = lat_ref[:, rank:2 * rank]
        cq_ref[...] = (lq * _rms(lq) * gq_ref[...]).astype(BF16)
        ckv_ref[...] = (lkv * _rms(lkv) * gkv_ref[...]).astype(BF16)
        kr_ref[...] = _rope(lat_ref[:, 2 * rank:W], cos_ref[...], slo_ref[...], shi_ref[...]).astype(BF16)

    return pl.pallas_call(
        body, name="mla_latent_fwd", grid=(S // ts,),
        in_specs=[pl.BlockSpec((ts, W), lambda i: (i, 0)), _vec_spec(rank), _vec_spec(rank), tab, tab, tab],
        out_specs=[pl.BlockSpec((ts, rank), lambda i: (i, 0)), pl.BlockSpec((ts, rank), lambda i: (i, 0)), tab],
        out_shape=[_sds((S, rank), BF16), _sds((S, rank), BF16), _sds((S, LANES), BF16)],
        compiler_params=_cparams("parallel"))(lat, g_q, g_kv, *rope_tabs)


def _latent_bwd(lat, dcq, dckv, dkr, g_q, g_kv, rope_tabs, rank):
    S, W = lat.shape
    ts = min(256, S)
    tab = pl.BlockSpec((ts, LANES), lambda i: (i, 0))
    half = pl.BlockSpec((ts, rank), lambda i: (i, 0))

    def body(lat_ref, dcq_ref, dckv_ref, dkr_ref, gq_ref, gkv_ref, cos_ref, slo_ref, shi_ref, dlat_ref, sums_ref):
        @pl.when(pl.program_id(0) == 0)
        def _():
            sums_ref[...] = jnp.zeros_like(sums_ref)

        def norm_bwd(v, dn, g, r):
            rr = _rms(v)
            vhat = v * rr
            u = dn * g
            sums_ref[r:r + 1, :] += jnp.sum(dn * vhat, axis=0, keepdims=True)
            return rr * (u - vhat * jnp.mean(u * vhat, axis=-1, keepdims=True))

        dlat_ref[:, 0:rank] = norm_bwd(lat_ref[:, 0:rank], dcq_ref[...], gq_ref[...], 0).astype(BF16)
        dlat_ref[:, rank:2 * rank] = norm_bwd(lat_ref[:, rank:2 * rank], dckv_ref[...], gkv_ref[...], 1).astype(BF16)
        dlat_ref[:, 2 * rank:W] = _rope_t(dkr_ref[...], cos_ref[...], slo_ref[...], shi_ref[...]).astype(BF16)

    return pl.pallas_call(
        body, name="mla_latent_bwd", grid=(S // ts,),
        in_specs=[pl.BlockSpec((ts, W), lambda i: (i, 0)), half, half, tab, _vec_spec(rank), _vec_spec(rank),
                  tab, tab, tab],
        out_specs=[pl.BlockSpec((ts, W), lambda i: (i, 0)), pl.BlockSpec((8, rank), lambda i: (0, 0))],
        out_shape=[_sds((S, W), BF16), _sds((8, rank), F32)],
        compiler_params=_cparams("arbitrary"))(lat, dcq, dckv, dkr, g_q, g_kv, *rope_tabs)


def _attn_tiles(S):
    t = min(512, S)
    return t, S // t


def _attn_fwd(q, kv, kr, heads, scale):
    S = q.shape[0]
    t, nb = _attn_tiles(S)

    def body(q_ref, kv_ref, kr_ref, o_ref, lse_ref, m_scr, l_scr, acc_scr):
        qi, ki = pl.program_id(1), pl.program_id(2)

        @pl.when(ki == 0)
        def _():
            m_scr[...] = jnp.full_like(m_scr, NEG_INF)
            l_scr[...] = jnp.zeros_like(l_scr)
            acc_scr[...] = jnp.zeros_like(acc_scr)

        @pl.when(ki <= qi)
        def _():
            kcat = jnp.concatenate([kv_ref[:, 0:QK_NOPE], kr_ref[...]], axis=1)
            s = lax.dot_general(q_ref[...], kcat, DIMS_NT, preferred_element_type=F32) * scale
            rows = lax.broadcasted_iota(jnp.int32, (t, t), 0)
            cols = lax.broadcasted_iota(jnp.int32, (t, t), 1)
            s = jnp.where((ki < qi) | (cols <= rows), s, NEG_INF)
            m_prev = m_scr[...]
            m_new = jnp.maximum(m_prev, jnp.max(s, axis=-1, keepdims=True))
            alpha = jnp.exp(m_prev - m_new)
            p = jnp.exp(s - m_new)
            l_scr[...] = alpha * l_scr[...] + jnp.sum(p, axis=-1, keepdims=True)
            acc_scr[...] = alpha * acc_scr[...] + lax.dot_general(
                p.astype(BF16), kv_ref[:, QK_NOPE:QK_NOPE + V_HEAD], DIMS_NN, preferred_element_type=F32)
            m_scr[...] = m_new

        @pl.when(ki == nb - 1)
        def _():
            o_ref[...] = (acc_scr[...] / l_scr[...]).astype(BF16)
            lse_ref[...] = m_scr[...] + jnp.log(l_scr[...])

    return pl.pallas_call(
        body, name="mla_attn_fwd", grid=(heads, nb, nb),
        in_specs=[pl.BlockSpec((t, QK_PAD), lambda h, qi, ki: (qi, h)),
                  pl.BlockSpec((t, QK_PAD), lambda h, qi, ki: (jnp.minimum(ki, qi), h)),
                  pl.BlockSpec((t, LANES), lambda h, qi, ki: (jnp.minimum(ki, qi), 0))],
        out_specs=[pl.BlockSpec((t, V_HEAD), lambda h, qi, ki: (qi, h)),
                   pl.BlockSpec((None, t, 1), lambda h, qi, ki: (h, qi, 0))],
        out_shape=[_sds((S, heads * V_HEAD), BF16), _sds((heads, S, 1), F32)],
        scratch_shapes=[pltpu.VMEM((t, 1), F32), pltpu.VMEM((t, 1), F32), pltpu.VMEM((t, V_HEAD), F32)],
        compiler_params=_cparams("parallel", "parallel", "arbitrary"))(q, kv, kr)


def _attn_bwd(q, kv, kr, o, do, lse, rope_tabs, heads, scale):
    S = q.shape[0]
    t, nb = _attn_tiles(S)

    def body(q_ref, kv_ref, kr_ref, o_ref, do_ref, lse_ref, cos_ref, slo_ref, shi_ref,
             dq_ref, dkv_ref, dkr_ref, dq_scr, dk_scr, dv_scr, dkr_scr):
        h, ki, qi = pl.program_id(0), pl.program_id(1), pl.program_id(2)
        q_rows = pl.ds(pl.multiple_of(qi * t, t), t)
        k_rows = pl.ds(pl.multiple_of(ki * t, t), t)

        @pl.when((ki == 0) & (qi == 0))
        def _():
            dq_scr[...] = jnp.zeros_like(dq_scr)

        @pl.when((h == 0) & (ki == 0) & (qi == 0))
        def _():
            dkr_scr[...] = jnp.zeros_like(dkr_scr)

        @pl.when(qi == 0)
        def _():
            dk_scr[...] = jnp.zeros_like(dk_scr)
            dv_scr[...] = jnp.zeros_like(dv_scr)

        @pl.when(qi >= ki)
        def _():
            qv = q_ref[...]
            kcat = jnp.concatenate([kv_ref[:, 0:QK_NOPE], kr_ref[...]], axis=1)
            s = lax.dot_general(qv, kcat, DIMS_NT, preferred_element_type=F32) * scale
            rows = lax.broadcasted_iota(jnp.int32, (t, t), 0)
            cols = lax.broadcasted_iota(jnp.int32, (t, t), 1)
            p = jnp.where((ki < qi) | (cols <= rows), jnp.exp(s - lse_ref[...]), 0.0)
            dov = do_ref[...]
            dv_scr[...] += lax.dot_general(p.astype(BF16), dov, DIMS_TN, preferred_element_type=F32)
            dp = lax.dot_general(dov, kv_ref[:, QK_NOPE:QK_NOPE + V_HEAD], DIMS_NT, preferred_element_type=F32)
            delta = jnp.sum(dov.astype(F32) * o_ref[...].astype(F32), axis=-1, keepdims=True)
            ds = (p * (dp - delta) * scale).astype(BF16)
            dk_scr[...] += lax.dot_general(ds, qv, DIMS_TN, preferred_element_type=F32)
            dq_scr[q_rows, :] += lax.dot_general(ds, kcat, DIMS_NN, preferred_element_type=F32)

        @pl.when(qi == nb - 1)
        def _():
            dkv_ref[...] = jnp.concatenate([dk_scr[:, 0:QK_NOPE], dv_scr[...]], axis=1).astype(BF16)
            dkr_scr[k_rows, :] += dk_scr[:, QK_NOPE:QK_PAD]

        @pl.when(ki == nb - 1)
        def _():
            dqv = dq_scr[q_rows, :]
            dq_ref[q_rows, :] = jnp.concatenate(
                [dqv[:, 0:QK_NOPE], _rope_t(dqv[:, QK_NOPE:QK_PAD], cos_ref[...], slo_ref[...], shi_ref[...])],
                axis=1).astype(BF16)

        @pl.when((h == heads - 1) & (ki == nb - 1) & (qi == nb - 1))
        def _():
            dkr_ref[...] = dkr_scr[...]

    qmap = lambda h, ki, qi: (jnp.maximum(qi, ki), h)
    tab = pl.BlockSpec((t, LANES), lambda h, ki, qi: (qi, 0))
    return pl.pallas_call(
        body, name="mla_attn_bwd", grid=(heads, nb, nb),
        in_specs=[pl.BlockSpec((t, QK_PAD), qmap),
                  pl.BlockSpec((t, QK_PAD), lambda h, ki, qi: (ki, h)),
                  pl.BlockSpec((t, LANES), lambda h, ki, qi: (ki, 0)),
                  pl.BlockSpec((t, V_HEAD), qmap),
                  pl.BlockSpec((t, V_HEAD), qmap),
                  pl.BlockSpec((None, t, 1), lambda h, ki, qi: (h, jnp.maximum(qi, ki), 0)),
                  tab, tab, tab],
        out_specs=[pl.BlockSpec((S, QK_PAD), lambda h, ki, qi: (0, h)),
                   pl.BlockSpec((t, QK_PAD), lambda h, ki, qi: (ki, h)),
                   pl.BlockSpec((S, LANES), lambda h, ki, qi: (0, 0))],
        out_shape=[_sds((S, heads * QK_PAD), BF16), _sds((S, heads * QK_PAD), BF16), _sds((S, LANES), F32)],
        scratch_shapes=[pltpu.VMEM((S, QK_PAD), F32), pltpu.VMEM((t, QK_PAD), F32), pltpu.VMEM((t, V_HEAD), F32),
                        pltpu.VMEM((S, LANES), F32)],
        compiler_params=_cparams("arbitrary", "arbitrary", "arbitrary"))(q, kv, kr, o, do, lse, *rope_tabs)


def _shift_down(z, n, rows):
    return jnp.where(rows >= n, pltpu.roll(z, n, 0), 0.0)


def _shift_up(z, n, rows, S):
    return jnp.where(rows < S - n, pltpu.roll(z, S - n, 0), 0.0)


def _conv_specs(S, tc):
    strip = lambda p: pl.BlockSpec((None, S, tc), lambda j: (p, 0, j))
    return strip(0), strip(1), strip(2), pl.BlockSpec((3, tc), lambda j: (0, j))


def _conv_fwd(proj3, w):
    _, S, D = proj3.shape
    tc = LANES

    def body(b_ref, c_ref, u_ref, w_ref, out_ref):
        z = c_ref[...] * u_ref[...]
        rows = lax.broadcasted_iota(jnp.int32, (S, tc), 0)
        zc = w_ref[0:1, :] * _shift_down(z, 2, rows) + w_ref[1:2, :] * _shift_down(z, 1, rows) + w_ref[2:3, :] * z
        out_ref[...] = (b_ref[...] * zc).astype(BF16)

    return pl.pallas_call(
        body, name="conv_fwd", grid=(D // tc,), in_specs=list(_conv_specs(S, tc)),
        out_specs=pl.BlockSpec((S, tc), lambda j: (0, j)), out_shape=_sds((S, D), BF16),
        compiler_params=_cparams("parallel"))(proj3, proj3, proj3, w)


def _conv_bwd(dbz, proj3, w):
    _, S, D = proj3.shape
    tc = LANES

    def body(d_ref, b_ref, c_ref, u_ref, w_ref, dp_ref, dw_ref):
        cv, uv, dv = c_ref[...], u_ref[...], d_ref[...]
        z = cv * uv
        rows = lax.broadcasted_iota(jnp.int32, (S, tc), 0)
        z1, z2 = _shift_down(z, 1, rows), _shift_down(z, 2, rows)
        zc = w_ref[0:1, :] * z2 + w_ref[1:2, :] * z1 + w_ref[2:3, :] * z
        dp_ref[0] = (dv * zc).astype(BF16)
        dzc = dv * b_ref[...]
        dz = w_ref[2:3, :] * dzc + w_ref[1:2, :] * _shift_up(dzc, 1, rows, S) + w_ref[0:1, :] * _shift_up(dzc, 2, rows, S)
        dp_ref[1] = (dz * uv).astype(BF16)
        dp_ref[2] = (dz * cv).astype(BF16)
        dw_ref[0:1, :] = jnp.sum(dzc * z2, axis=0, keepdims=True)
        dw_ref[1:2, :] = jnp.sum(dzc * z1, axis=0, keepdims=True)
        dw_ref[2:3, :] = jnp.sum(dzc * z, axis=0, keepdims=True)

    sb, sc_, su, sw = _conv_specs(S, tc)
    return pl.pallas_call(
        body, name="conv_bwd", grid=(D // tc,),
        in_specs=[pl.BlockSpec((S, tc), lambda j: (0, j)), sb, sc_, su, sw],
        out_specs=[pl.BlockSpec((3, S, tc), lambda j: (0, 0, j)), pl.BlockSpec((3, tc), lambda j: (0, j))],
        out_shape=[_sds((3, S, D), BF16), _sds((3, D), F32)],
        compiler_params=_cparams("parallel"))(dbz, proj3, proj3, proj3, w)


def _silu(c_all):
    def body(c_ref, o_ref):
        cv = c_ref[...]
        o_ref[...] = cv * (1.0 / (1.0 + jnp.exp(-cv)))

    vm = pl.BlockSpec(memory_space=pltpu.VMEM)
    return pl.pallas_call(body, name="cond_silu", in_specs=[vm], out_specs=vm, out_shape=_sds(c_all.shape, F32))(c_all)


def _mod_fwd(cond, w_mod, b_cols):
    L, D, ncol = w_mod.shape
    B = cond.shape[0]
    tk, tn = min(512, D), min(1024, ncol)
    nk = D // tk

    def body(c_ref, w_ref, b_ref, out_ref, acc):
        kk = pl.program_id(2)
        part = lax.dot_general(c_ref[...].astype(BF16), w_ref[...].astype(BF16), DIMS_NN, preferred_element_type=F32)

        @pl.when(kk == 0)
        def _():
            acc[...] = part

        @pl.when(kk > 0)
        def _():
            acc[...] += part

        @pl.when(kk == nk - 1)
        def _():
            out_ref[...] = acc[...] + b_ref[...]

    return pl.pallas_call(
        body, name="mod_fwd", grid=(L, ncol // tn, nk),
        in_specs=[pl.BlockSpec((B, tk), lambda l, j, k: (0, k)),
                  pl.BlockSpec((None, tk, tn), lambda l, j, k: (l, k, j)),
                  pl.BlockSpec((None, 1, tn), lambda l, j, k: (l, 0, j))],
        out_specs=pl.BlockSpec((None, B, tn), lambda l, j, k: (l, 0, j)),
        out_shape=_sds((L, B, ncol), F32),
        scratch_shapes=[pltpu.VMEM((B, tn), F32)],
        compiler_params=_cparams("parallel", "parallel", "arbitrary"))(cond, w_mod, b_cols)


def _adamw_math(w, g, m, v):
    m = ADAM_B1 * m + (1.0 - ADAM_B1) * g
    v = ADAM_B2 * v + (1.0 - ADAM_B2) * (g * g)
    m_hat = m / (1.0 - ADAM_B1 ** ADAM_STEP)
    v_hat = v / (1.0 - ADAM_B2 ** ADAM_STEP)
    delta = -ADAM_LR * (m_hat / (jnp.sqrt(v_hat) + ADAM_EPS) + ADAM_WD * w)
    return delta, m, v


def _adamw(name, w, g, m, v):
    shape = w.shape
    cols = shape[-1] if w.ndim <= 3 else shape[-2] * shape[-1]
    rows = w.size // cols
    w2, g2, m2, v2 = (t.reshape(rows, cols) for t in (w, g, m, v))
    tr = _row_tile(rows, cols * 4, limit=1024 * 1024, mult=8)
    spec = pl.BlockSpec((tr, cols), lambda i: (i, 0))

    def body(w_ref, g_ref, m_ref, v_ref, d_ref, nm_ref, nv_ref):
        d, nm, nv = _adamw_math(w_ref[...], g_ref[...], m_ref[...], v_ref[...])
        d_ref[...] = d
        nm_ref[...] = nm
        nv_ref[...] = nv

    outs = pl.pallas_call(body, name=name, grid=(rows // tr,), in_specs=[spec] * 4, out_specs=[spec] * 3,
                          out_shape=[_sds((rows, cols), F32)] * 3, compiler_params=_cparams("parallel"))(w2, g2, m2, v2)
    return tuple(t.reshape(shape) for t in outs)


def _adamw_mod(w, cond_t, dmod_cols, m, v):
    L, D, ncol = w.shape
    B = cond_t.shape[1]
    tr, tc = min(256, D), min(1024, ncol)
    blk = pl.BlockSpec((None, tr, tc), lambda l, i, j: (l, i, j))

    def body(w_ref, ct_ref, dm_ref, m_ref, v_ref, g_ref, d_ref, nm_ref, nv_ref):
        g = lax.dot_general(ct_ref[...], dm_ref[...], DIMS_NN, precision=lax.Precision.HIGHEST,
                            preferred_element_type=F32)
        d, nm, nv = _adamw_math(w_ref[...], g, m_ref[...], v_ref[...])
        g_ref[...] = g
        d_ref[...] = d
        nm_ref[...] = nm
        nv_ref[...] = nv

    return pl.pallas_call(
        body, name="adamw_w_mod", grid=(L, D // tr, ncol // tc),
        in_specs=[blk, pl.BlockSpec((tr, B), lambda l, i, j: (i, 0)),
                  pl.BlockSpec((None, B, tc), lambda l, i, j: (l, 0, j)), blk, blk],
        out_specs=[blk] * 4, out_shape=[_sds((L, D, ncol), F32)] * 4,
        compiler_params=_cparams("parallel", "parallel", "parallel"))(w, cond_t, dmod_cols, m, v)


def _cast_into_full(name, w, kind, k_idx):
    L, R, C = w.shape
    Rh = R // 2
    tr = _row_tile(Rh, C * 4)
    if kind == "row":
        out_shape = (L, N_CHIPS, 2, Rh, C)
        out_spec = pl.BlockSpec((None, None, None, tr, C), lambda l, h, i, k_ref: (l, k_ref[0], h, i, 0))
    else:
        out_shape = (L, 2, Rh, N_CHIPS * C)
        out_spec = pl.BlockSpec((None, None, tr, C), lambda l, h, i, k_ref: (l, h, i, k_ref[0]))

    def body(k_ref, w_ref, o_ref):
        o_ref[...] = w_ref[...].astype(BF16)

    return pl.pallas_call(
        body, name=name,
        grid_spec=pltpu.PrefetchScalarGridSpec(
            num_scalar_prefetch=1, grid=(L, 2, Rh // tr),
            in_specs=[pl.BlockSpec((None, None, tr, C), lambda l, h, i, k_ref: (l, h, i, 0))],
            out_specs=out_spec),
        out_shape=_sds(out_shape, BF16),
        compiler_params=_cparams("parallel", "parallel", "parallel"))(k_idx, w.reshape(L, 2, Rh, C))


def _pair_sum(name, g5, ra, c_idx):
    L, A, _, Rh, Cc = g5.shape
    tr = _row_tile(Rh, Cc * 4)

    def body(c_ref, g_ref, r_ref, o_ref):
        o_ref[...] = (g_ref[...].astype(F32) + r_ref[...].astype(F32)).astype(BF16)

    blk = pl.BlockSpec((None, None, tr, Cc), lambda l, a, i, c_ref: (l, a, i, 0))
    return pl.pallas_call(
        body, name=name,
        grid_spec=pltpu.PrefetchScalarGridSpec(
            num_scalar_prefetch=1, grid=(L, A, Rh // tr),
            in_specs=[pl.BlockSpec((None, None, None, tr, Cc), lambda l, a, i, c_ref: (l, a, c_ref[0], i, 0)), blk],
            out_specs=blk),
        out_shape=_sds((L, A, Rh, Cc), BF16),
        compiler_params=_cparams("parallel", "parallel", "parallel"))(c_idx, g5, ra)


def _chip_sum(name, p, rb, kc_idx, kind):
    L, A, Rh, Cc = p.shape
    C = rb.shape[-1]
    tr = _row_tile(Rh, C * 4)
    if kind == "row":
        own = pl.BlockSpec((None, None, tr, C), lambda l, i, kc: (l, kc[0], i, 0))
    else:
        own = pl.BlockSpec((None, None, tr, C), lambda l, i, kc: (l, 0, i, kc[0]))
    peer = lambda j: pl.BlockSpec((None, None, tr, C), lambda l, i, kc: (j, l, i, 0))

    def body(kc_ref, p_ref, r0_ref, r1_ref, r2_ref, o_ref):
        o_ref[...] = ((p_ref[...].astype(F32) + r0_ref[...].astype(F32)) + r1_ref[...].astype(F32)) + r2_ref[...].astype(F32)

    return pl.pallas_call(
        body, name=name,
        grid_spec=pltpu.PrefetchScalarGridSpec(
            num_scalar_prefetch=1, grid=(L, Rh // tr),
            in_specs=[own, peer(0), peer(1), peer(2)],
            out_specs=pl.BlockSpec((None, None, tr, C), lambda l, i, kc: (l, kc[1], i, 0))),
        out_shape=_sds((L, 2, Rh, C), F32),
        compiler_params=_cparams("parallel", "parallel"))(kc_idx, p, rb, rb, rb)


def _mesh_place():
    x, y, c = lax.axis_index("x"), lax.axis_index("y"), lax.axis_index("c")
    chips = [(1 - x, y), (x, 1 - y), (1 - x, 1 - y)]
    return x, y, c, chips


def _remote(src, dst, send_sem, recv_sem, to):
    return pltpu.make_async_remote_copy(src_ref=src, dst_ref=dst, send_sem=send_sem, recv_sem=recv_sem,
                                        device_id=to, device_id_type=MESH_ID)


def _small_allgather(name, v, with_sum=False):
    R, N = v.shape

    def body(*refs):
        if with_sum:
            x_ref, out_ref, sum_ref, send_sems, recv_sems, local_sem = refs
        else:
            x_ref, out_ref, send_sems, recv_sems, local_sem = refs
        x, y, c, chips = _mesh_place()
        me, sibling = (x, y, c), (x, y, 1 - c)

        def rows(px, py, pc):
            return out_ref.at[pl.ds((4 * px + 2 * py + pc) * R, R), :]

        def copy(k, block, to, src=None):
            return _remote(rows(*block) if src is None else src, rows(*block), send_sems.at[k], recv_sems.at[k], to)

        mine = pltpu.make_async_copy(x_ref, rows(*me), local_sem)
        mine.start()
        first = [copy(0, me, sibling, src=x_ref)]
        first += [copy(1 + j, me, (*chip, c), src=x_ref) for j, chip in enumerate(chips)]
        for cp in first:
            cp.start()
        passed = [copy(4 + j, (*chip, c), sibling) for j, chip in enumerate(chips)]
        for j, chip in enumerate(chips):
            copy(1 + j, (*chip, c), me).wait_recv()
            passed[j].start()
        copy(0, sibling, me).wait_recv()
        for j, chip in enumerate(chips):
            copy(4 + j, (*chip, 1 - c), me).wait_recv()
        for cp in first + passed:
            cp.wait_send()
        mine.wait()
        if with_sum:
            total = out_ref[0:R, :]
            for p in range(1, 8):
                total = total + out_ref[p * R:(p + 1) * R, :]
            sum_ref[...] = total

    vm = pl.BlockSpec(memory_space=pltpu.VMEM)
    out_shape = [_sds((8 * R, N), F32)] + ([_sds((R, N), F32)] if with_sum else [])
    res = pl.pallas_call(
        body, name=name, out_shape=out_shape, in_specs=[vm], out_specs=[vm] * len(out_shape),
        scratch_shapes=[pltpu.SemaphoreType.DMA((7,)), pltpu.SemaphoreType.DMA((7,)), pltpu.SemaphoreType.DMA],
        compiler_params=pltpu.CompilerParams(vmem_limit_bytes=VMEM_LIMIT_BYTES))(v)
    return res if with_sum else res[0]


def _full_place(ref, kind, C, kk, half):
    if kind == "row":
        return ref.at[:, kk, half]
    return ref.at[:, half, :, pl.ds(pl.multiple_of(kk * C, LANES), C)]


def _gather_weights(fulls, kinds, shard_cols):
    n = len(fulls)

    def body(*refs):
        outs = refs[n:2 * n]
        send_sems, recv_sems = refs[2 * n:]
        x, y, c, chips = _mesh_place()
        k = 2 * x + y
        me, sibling = (x, y, c), (x, y, 1 - c)

        def place(a, kk, half):
            return _full_place(outs[a], kinds[a], shard_cols[a], kk, half)

        def copy(a, j, src, dst, to):
            return _remote(src, dst, send_sems.at[6 * a + j], recv_sems.at[6 * a + j], to)

        sends = []
        for j, chip in enumerate(chips):
            for a in range(n):
                cp = copy(a, j, place(a, k, c), place(a, k, c), (*chip, c))
                cp.start()
                sends.append(cp)
        for j, chip in enumerate(chips):
            kj = 2 * chip[0] + chip[1]
            for a in range(n):
                landed = place(a, kj, c)
                copy(a, j, landed, landed, me).wait_recv()
                cp = copy(a, 3 + j, landed, landed, sibling)
                cp.start()
                sends.append(cp)
        for j, chip in enumerate(chips):
            kj = 2 * chip[0] + chip[1]
            for a in range(n):
                other = place(a, kj, 1 - c)
                copy(a, 3 + j, other, other, me).wait_recv()
        for cp in sends:
            cp.wait_send()

    return pl.pallas_call(
        body, name="gather_weights", out_shape=[_sds(f.shape, BF16) for f in fulls],
        in_specs=[ANY] * n, out_specs=[ANY] * n, input_output_aliases={a: a for a in range(n)},
        scratch_shapes=[pltpu.SemaphoreType.DMA((6 * n,)), pltpu.SemaphoreType.DMA((6 * n,))],
        compiler_params=pltpu.CompilerParams(vmem_limit_bytes=VMEM_LIMIT_BYTES))(*fulls)


def _pair_exchange(g5s):
    n = len(g5s)

    def body(*refs):
        ins, outs = refs[:n], refs[n:2 * n]
        send_sems, recv_sems = refs[2 * n:]
        x, y, c, _ = _mesh_place()
        cps = []
        for a in range(n):
            cp = _remote(ins[a].at[:, :, 1 - c], outs[a], send_sems.at[a], recv_sems.at[a], (x, y, 1 - c))
            cp.start()
            cps.append(cp)
        for cp in cps:
            cp.wait_recv()
        for cp in cps:
            cp.wait_send()

    out_shape = [_sds((g.shape[0], g.shape[1], g.shape[3], g.shape[4]), BF16) for g in g5s]
    return pl.pallas_call(
        body, name="grad_pair_exchange", out_shape=out_shape, in_specs=[ANY] * n, out_specs=[ANY] * n,
        scratch_shapes=[pltpu.SemaphoreType.DMA((n,)), pltpu.SemaphoreType.DMA((n,))],
        compiler_params=pltpu.CompilerParams(vmem_limit_bytes=VMEM_LIMIT_BYTES))(*g5s)


def _chip_scatter(ps, kinds, shard_cols):
    n = len(ps)

    def body(*refs):
        ins, outs = refs[:n], refs[n:2 * n]
        send_sems, recv_sems = refs[2 * n:]
        x, y, c, chips = _mesh_place()
        cps = []
        for j, chip in enumerate(chips):
            kj = 2 * chip[0] + chip[1]
            for a in range(n):
                C = shard_cols[a]
                src = ins[a].at[:, kj] if kinds[a] == "row" else ins[a].at[:, 0, :, pl.ds(pl.multiple_of(kj * C, LANES), C)]
                cp = _remote(src, outs[a].at[j], send_sems.at[3 * a + j], recv_sems.at[3 * a + j], (*chip, c))
                cp.start()
                cps.append(cp)
        for cp in cps:
            cp.wait_recv()
        for cp in cps:
            cp.wait_send()

    out_shape = [_sds((3, p.shape[0], p.shape[2], C), BF16) for p, C in zip(ps, shard_cols)]
    return pl.pallas_call(
        body, name="grad_chip_scatter", out_shape=out_shape, in_specs=[ANY] * n, out_specs=[ANY] * n,
        scratch_shapes=[pltpu.SemaphoreType.DMA((3 * n,)), pltpu.SemaphoreType.DMA((3 * n,))],
        compiler_params=pltpu.CompilerParams(vmem_limit_bytes=VMEM_LIMIT_BYTES))(*ps)


def _pair_share(fs):
    n = len(fs)

    def body(*refs):
        outs = refs[n:2 * n]
        send_sems, recv_sems = refs[2 * n:]
        x, y, c, _ = _mesh_place()
        cps = []
        for a in range(n):
            mine = outs[a].at[:, c]
            cp = _remote(mine, mine, send_sems.at[a], recv_sems.at[a], (x, y, 1 - c))
            cp.start()
            cps.append(cp)
        for a in range(n):
            theirs = outs[a].at[:, 1 - c]
            _remote(theirs, theirs, send_sems.at[a], recv_sems.at[a], (x, y, c)).wait_recv()
        for cp in cps:
            cp.wait_send()

    return pl.pallas_call(
        body, name="grad_pair_share", out_shape=[_sds(f.shape, F32) for f in fs],
        in_specs=[ANY] * n, out_specs=[ANY] * n, input_output_aliases={a: a for a in range(n)},
        scratch_shapes=[pltpu.SemaphoreType.DMA((n,)), pltpu.SemaphoreType.DMA((n,))],
        compiler_params=pltpu.CompilerParams(vmem_limit_bytes=VMEM_LIMIT_BYTES))(*fs)


def _pack_rows(parts, lane_mult=1024):
    flat = jnp.concatenate([p.reshape(-1).astype(F32) for p in parts])
    n = -(-flat.shape[0] // (8 * lane_mult)) * lane_mult
    return jnp.pad(flat, (0, 8 * n - flat.shape[0])).reshape(8, n)


def _relu2(acc):
    r = jnp.maximum(acc, 0.0)
    return r, r * r


def _times_2r(acc, r):
    return (acc * (2.0 * r.astype(F32)),)


def kernel(x, c, positions, w_mod, b_mod, norm_g, mla_w_in, mla_g_q, mla_g_kv, mla_w_uq, mla_w_ukv, mla_w_o, conv_w_in, conv_w, conv_w_out, mlp_w_up, mlp_w_down, loss_target, m_w_mod, m_b_mod, m_norm_g, m_mla_w_in, m_mla_g_q, m_mla_g_kv, m_mla_w_uq, m_mla_w_ukv, m_mla_w_o, m_conv_w_in, m_conv_w, m_conv_w_out, m_mlp_w_up, m_mlp_w_down, v_w_mod, v_b_mod, v_norm_g, v_mla_w_in, v_mla_g_q, v_mla_g_kv, v_mla_w_uq, v_mla_w_ukv, v_mla_w_o, v_conv_w_in, v_conv_w, v_conv_w_out, v_mlp_w_up, v_mlp_w_down):
    S, D = x.shape[1], x.shape[2]
    Dq = D // N_CHIPS
    ncol = w_mod.shape[2]
    n_mod = N_CHIPS * ncol // D
    F = mlp_w_up.shape[2] * N_CHIPS
    lat_dim = mla_w_in.shape[2]
    rank = mla_g_q.shape[1]
    H = mla_w_uq.shape[2]
    d_qk = mla_w_uq.shape[3]
    assert mla_g_kv.shape[1] == rank and lat_dim == 2 * rank + QK_ROPE and d_qk == QK_NOPE + QK_ROPE
    assert mla_w_ukv.shape[3] == QK_NOPE + V_HEAD and x.shape[0] == 1 and n_mod == 6
    assert norm_g.shape[0] == 2 and mla_w_in.shape[0] == 1 and conv_w_in.shape[0] == 1
    lat_pad = 2 * rank + LANES
    scale = float(d_qk) ** -0.5

    xi, yi, ci = lax.axis_index("x"), lax.axis_index("y"), lax.axis_index("c")
    chip = 2 * xi + yi
    dev = 2 * chip + ci
    c_idx = jnp.reshape(ci, (1,)).astype(jnp.int32)
    k_idx = jnp.reshape(chip, (1,)).astype(jnp.int32)

    n1 = D + 2 * D + 3 * Dq
    g1 = _small_allgather("gather_small_inputs", _pack_rows([c, norm_g, conv_w])).reshape(8, -1)
    c_all = g1[:, :D]
    by_chip = g1[0::2]
    norm_full = jnp.concatenate([by_chip[kk, D:3 * D].reshape(2, 4, Dq) for kk in range(N_CHIPS)], axis=-1)
    convw_full = jnp.concatenate([by_chip[kk, 3 * D:n1].reshape(3, Dq) for kk in range(N_CHIPS)], axis=-1)

    b_cols = lax.dynamic_slice(b_mod, (0, chip * ncol), (2, ncol)).reshape(2, 1, ncol)
    cond_all = _silu(c_all)
    mod_cols = _mod_fwd(cond_all, w_mod, b_cols)
    g2 = _small_allgather("gather_mod", _pack_rows([mod_cols]))
    g2 = g2.reshape(8, -1)[0::2, :2 * 8 * ncol].reshape(N_CHIPS, 2, 8, ncol)
    mod_all = jnp.transpose(g2, (2, 1, 0, 3)).reshape(8, 2, n_mod * D)
    mod_me = lax.dynamic_index_in_dim(mod_all, dev, axis=0, keepdims=False)
    mods = [[mod_me[l, i * D:(i + 1) * D].reshape(1, D) for i in range(n_mod)] for l in range(2)]
    ng = [[norm_full[l, i].reshape(1, D) for i in range(4)] for l in range(2)]

    pos = positions[0].astype(F32)
    inv_freq = ROPE_THETA ** (-jnp.arange(0, QK_ROPE, 2, dtype=F32) / QK_ROPE)
    ang = pos[:, None] * inv_freq
    cos, sin = jnp.cos(ang), jnp.sin(ang)
    zero = jnp.zeros_like(cos)
    rope_tabs = (jnp.concatenate([cos, cos, zero, zero], axis=1),
                 jnp.concatenate([-sin, zero, zero, zero], axis=1),
                 jnp.concatenate([zero, sin, zero, zero], axis=1))

    weights = [("mla_w_in", mla_w_in, "row"), ("mla_w_uq", mla_w_uq.reshape(1, rank // N_CHIPS, H * d_qk), "row"),
               ("mla_w_ukv", mla_w_ukv.reshape(1, rank // N_CHIPS, H * QK_PAD), "row"), ("mla_w_o", mla_w_o, "row"),
               ("conv_w_in", conv_w_in, "col"), ("conv_w_out", conv_w_out, "row"),
               ("mlp_w_up", mlp_w_up, "col"), ("mlp_w_down", mlp_w_down, "row")]
    kinds = [k for _, _, k in weights]
    shard_shapes = [w.shape for _, w, _ in weights]
    shard_cols = [s[2] for s in shard_shapes]
    full = _gather_weights([_cast_into_full("cast_" + nm, w, kind, k_idx) for nm, w, kind in weights], kinds, shard_cols)

    def full2d(i):
        L, R, C = shard_shapes[i]
        return full[i].reshape((L, N_CHIPS * R, C) if kinds[i] == "row" else (L, R, N_CHIPS * C))

    w_in_p = jnp.pad(full2d(0)[0], ((0, 0), (0, lat_pad - lat_dim)))
    w_q_p = jnp.pad(full2d(1)[0].reshape(rank, H, d_qk), ((0, 0), (0, 0), (0, QK_PAD - d_qk))).reshape(rank, H * QK_PAD)
    w_ukv, w_o, w_cin, w_cout = full2d(2)[0], full2d(3)[0], full2d(4)[0], full2d(5)[0]
    w_up, w_down = full2d(6), full2d(7)
    HV = H * V_HEAD

    def lead(l, shape_fn):
        return lambda tm, tn, tk: shape_fn(l, tm, tn, tk)

    def mlp_fwd(tag, l, h):
        r, a2 = _mm("mlp_up_" + tag, h, w_up, "nn", S, F, D, [_sds((S, F), BF16)] * 2, epilogue=_relu2,
                    b_spec=lead(l, lambda l, tm, tn, tk: pl.BlockSpec((None, tk, tn), lambda i, j, k: (l, k, j))))
        (y,) = _mm("mlp_down_" + tag, a2, w_down, "nn", S, D, F, [_sds((S, D), F32)],
                   b_spec=lead(l, lambda l, tm, tn, tk: pl.BlockSpec((None, tk, tn), lambda i, j, k: (l, k, j))))
        return r, a2, y

    def mlp_bwd(tag, l, h, r, a2, dy, dw_up_prev, dw_down_prev):
        (da,) = _mm("mlp_down_dx_" + tag, dy, w_down, "nt", S, F, D, [_sds((S, F), BF16)], epilogue=_times_2r,
                    b_spec=lead(l, lambda l, tm, tn, tk: pl.BlockSpec((None, tn, tk), lambda i, j, k: (l, j, k))),
                    extras=[(r, lambda tm, tn, tk: pl.BlockSpec((tm, tn), lambda i, j, k: (i, j)))])
        (dw_down,) = _mm("mlp_down_dw_" + tag, a2, dy, "tn", F, D, S, [_sds((2, F, D), BF16)], alias=dw_down_prev,
                         out_specs=[lead(l, lambda l, tm, tn, tk: pl.BlockSpec((None, tm, tn), lambda i, j, k: (l, i, j)))])
        (dh,) = _mm("mlp_up_dx_" + tag, da, w_up, "nt", S, D, F, [_sds((S, D), F32)],
                    b_spec=lead(l, lambda l, tm, tn, tk: pl.BlockSpec((None, tn, tk), lambda i, j, k: (l, j, k))))
        (dw_up,) = _mm("mlp_up_dw_" + tag, h, da, "tn", D, F, S, [_sds((2, D, F), BF16)], alias=dw_up_prev,
                       out_specs=[lead(l, lambda l, tm, tn, tk: pl.BlockSpec((None, tm, tn), lambda i, j, k: (l, i, j)))])
        return dh, dw_up, dw_down

    x0 = x[0]
    sh1, sc1, gt1, sh2, sc2, gt2 = mods[0]
    (h1,) = _fwd_boundary("fwd_boundary_0", x0, None, None, None, ng[0][0], sc1, sh1)
    (lat,) = _mm("mla_in", h1, w_in_p, "nn", S, lat_pad, D, [_sds((S, lat_pad), F32)], tn=lat_pad)
    cq, ckv, kr = _latent_fwd(lat, mla_g_q, mla_g_kv, rope_tabs, rank)

    def rope_q(acc, cos_p, sin_lo, sin_hi):
        parts = []
        for hh in range(acc.shape[1] // QK_PAD):
            parts.append(acc[:, hh * QK_PAD:hh * QK_PAD + QK_NOPE])
            parts.append(_rope(acc[:, hh * QK_PAD + QK_NOPE:(hh + 1) * QK_PAD], cos_p, sin_lo, sin_hi))
        return (jnp.concatenate(parts, axis=1),)

    tab_extra = lambda tm, tn, tk: pl.BlockSpec((tm, LANES), lambda i, j, k: (i, 0))
    (q,) = _mm("mla_q", cq, w_q_p, "nn", S, H * QK_PAD, rank, [_sds((S, H * QK_PAD), BF16)], epilogue=rope_q,
               extras=[(t, tab_extra) for t in rope_tabs], tn=2 * QK_PAD)
    (kv,) = _mm("mla_kv", ckv, w_ukv, "nn", S, H * QK_PAD, rank, [_sds((S, H * QK_PAD), BF16)])
    o, lse = _attn_fwd(q, kv, kr, H, scale)
    (y1,) = _mm("mla_out", o, w_o, "nn", S, D, HV, [_sds((S, D), F32)])
    x1, h2 = _fwd_boundary("fwd_boundary_1", x0, y1, gt1, ng[0][1], ng[0][2], sc2, sh2)
    r2, a2, y2 = mlp_fwd("0", 0, h2)

    sh1b, sc1b, gt1b, sh2b, sc2b, gt2b = mods[1]
    x2, h3 = _fwd_boundary("fwd_boundary_2", x1, y2, gt2, ng[0][3], ng[1][0], sc1b, sh1b)
    nD = lambda tn: D // tn
    (proj3,) = _mm("conv_in", h3, w_cin, "nn", S, 3 * D, D, [_sds((3, S, D), F32)], tn=min(1024, D),
                   out_specs=[lambda tm, tn, tk: pl.BlockSpec((None, tm, tn), lambda i, j, k: (j // nD(tn), i, j % nD(tn)))])
    bz = _conv_fwd(proj3, convw_full)
    (y3,) = _mm("conv_out", bz, w_cout, "nn", S, D, D, [_sds((S, D), F32)])
    x3, h4 = _fwd_boundary("fwd_boundary_3", x2, y3, gt1b, ng[1][1], ng[1][2], sc2b, sh2b)
    r4, a4, y4 = mlp_fwd("1", 1, h4)

    dx4, dy4, sums_l, loss_acc = _loss_boundary("loss_boundary", x3, y4, gt2b, ng[1][3], loss_target[0])
    loss = lax.psum(loss_acc[0, 0], ("x", "y", "c"))

    dh4, dw_up, dw_down = mlp_bwd("1", 1, h4, r4, a4, dy4, None, None)
    dx3, dy3, sums_3 = _bwd_boundary("bwd_boundary_3", dx4, dh4, x3, y3, gt1b, ng[1][1], ng[1][2], sc2b)

    (dbz,) = _mm("conv_out_dx", dy3, w_cout, "nt", S, D, D, [_sds((S, D), F32)])
    (dw_cout,) = _mm("conv_out_dw", bz, dy3, "tn", D, D, S, [_sds((D, D), BF16)])
    dproj3, dconvw = _conv_bwd(dbz, proj3, convw_full)
    (dh3,) = _mm("conv_in_dx", dproj3, w_cin, "nt", S, D, 3 * D, [_sds((S, D), F32)], tk=min(1024, D),
                 a_spec=lambda tm, tn, tk: pl.BlockSpec((None, tm, tk), lambda i, j, k: (k // (D // tk), i, k % (D // tk))))
    (dw_cin,) = _mm("conv_in_dw", h3, dproj3, "tn", D, 3 * D, S, [_sds((D, 3 * D), BF16)], tn=min(1024, D),
                    b_spec=lambda tm, tn, tk: pl.BlockSpec((None, tk, tn), lambda i, j, k: (j // nD(tn), k, j % nD(tn))))
    dx2, dy2, sums_2 = _bwd_boundary("bwd_boundary_2", dx3, dh3, x2, y2, gt2, ng[0][3], ng[1][0], sc1b)

    dh2, dw_up, dw_down = mlp_bwd("0", 0, h2, r2, a2, dy2, dw_up, dw_down)
    dx1, dy1, sums_1 = _bwd_boundary("bwd_boundary_1", dx2, dh2, x1, y1, gt1, ng[0][1], ng[0][2], sc2)

    (do,) = _mm("mla_out_dx", dy1, w_o, "nt", S, HV, D, [_sds((S, HV), BF16)])
    (dw_o,) = _mm("mla_out_dw", o, dy1, "tn", HV, D, S, [_sds((HV, D), BF16)])
    dq, dkv, dkr = _attn_bwd(q, kv, kr, o, do, lse, rope_tabs, H, scale)
    (dcq,) = _mm("mla_q_dx", dq, w_q_p, "nt", S, rank, H * QK_PAD, [_sds((S, rank), F32)])
    (dw_q_p,) = _mm("mla_q_dw", cq, dq, "tn", rank, H * QK_PAD, S, [_sds((rank, H * QK_PAD), BF16)])
    (dckv,) = _mm("mla_kv_dx", dkv, w_ukv, "nt", S, rank, H * QK_PAD, [_sds((S, rank), F32)])
    (dw_ukv,) = _mm("mla_kv_dw", ckv, dkv, "tn", rank, H * QK_PAD, S, [_sds((rank, H * QK_PAD), BF16)])
    dlat, sums_lat = _latent_bwd(lat, dcq, dckv, dkr, mla_g_q, mla_g_kv, rope_tabs, rank)
    (dh1,) = _mm("mla_in_dx", dlat, w_in_p, "nt", S, D, lat_pad, [_sds((S, D), F32)])
    (dw_in_p,) = _mm("mla_in_dw", h1, dlat, "tn", D, lat_pad, S, [_sds((D, lat_pad), BF16)], tn=lat_pad)
    grad_x, sums_0 = _bwd_boundary("bwd_boundary_0", dx1, dh1, x0, None, None, None, ng[0][0], sc1)

    dmod0 = [sums_0[0], sums_0[1], sums_1[3], sums_1[0], sums_1[1], sums_2[3]]
    dmod1 = [sums_2[0], sums_2[1], sums_3[3], sums_3[0], sums_3[1], sums_l[3]]
    dng0 = [sums_0[2], sums_1[4], sums_1[2], sums_2[4]]
    dng1 = [sums_2[2], sums_3[4], sums_3[2], sums_l[4]]
    small = _pack_rows(dmod0 + dmod1 + dng0 + dng1 + [sums_lat[0], sums_lat[1], dconvw], lane_mult=LANES)
    gathered, total = _small_allgather("gather_small_grads", small, with_sum=True)
    n_dm = 2 * n_mod * D
    dmod_all = gathered.reshape(8, -1)[:, :n_dm].reshape(8, 2, n_mod * D)
    total = total.reshape(-1)
    g_b_mod = total[:n_dm].reshape(2, n_mod * D)
    g_norm = lax.dynamic_slice(total[n_dm:n_dm + 8 * D].reshape(2, 4, D), (0, 0, chip * Dq), (2, 4, Dq))
    off = n_dm + 8 * D
    g_gq = total[off:off + rank].reshape(1, rank)
    g_gkv = total[off + rank:off + 2 * rank].reshape(1, rank)
    off += 2 * rank
    g_convw = lax.dynamic_slice(total[off:off + 3 * D].reshape(1, 3, D), (0, 0, chip * Dq), (1, 3, Dq))

    dw_list = [dw_in_p[:, :lat_dim][None], dw_q_p.reshape(rank, H, QK_PAD)[:, :, :d_qk].reshape(1, rank, H * d_qk),
               dw_ukv[None], dw_o[None], dw_cin[None], dw_cout[None], dw_up, dw_down]
    g5s = []
    for (L, R, C), kind, g in zip(shard_shapes, kinds, dw_list):
        g5s.append(g.reshape((L, N_CHIPS, 2, R // 2, C) if kind == "row" else (L, 1, 2, R // 2, N_CHIPS * C)))
    ras = _pair_exchange(g5s)
    ps = [_pair_sum("pair_sum_" + nm, g5, ra, c_idx) for (nm, _, _), g5, ra in zip(weights, g5s, ras)]
    rbs = _chip_scatter(ps, kinds, shard_cols)
    kc_idx = jnp.stack([chip, ci]).astype(jnp.int32)
    fs = [_chip_sum("chip_sum_" + nm, p, rb, kc_idx, kind) for (nm, _, kind), p, rb in zip(weights, ps, rbs)]
    finals = _pair_share(fs)
    orig = [mla_w_in, mla_w_uq, mla_w_ukv, mla_w_o, conv_w_in, conv_w_out, mlp_w_up, mlp_w_down]
    big_grads = [f.reshape(w.shape) for f, w in zip(finals, orig)]

    dmod_cols = jnp.transpose(lax.dynamic_slice(dmod_all.reshape(8, 2, N_CHIPS, ncol), (0, 0, chip, 0), (8, 2, 1, ncol))
                              .reshape(8, 2, ncol), (1, 0, 2))
    g_w_mod, d_w_mod, nm_w_mod, nv_w_mod = _adamw_mod(w_mod, cond_all.T, dmod_cols, m_w_mod, v_w_mod)

    names = ["b_mod", "norm_g", "mla_w_in", "mla_g_q", "mla_g_kv", "mla_w_uq", "mla_w_ukv", "mla_w_o",
             "conv_w_in", "conv_w", "conv_w_out", "mlp_w_up", "mlp_w_down"]
    ws = [b_mod, norm_g, mla_w_in, mla_g_q, mla_g_kv, mla_w_uq, mla_w_ukv, mla_w_o, conv_w_in, conv_w, conv_w_out,
          mlp_w_up, mlp_w_down]
    ms = [m_b_mod, m_norm_g, m_mla_w_in, m_mla_g_q, m_mla_g_kv, m_mla_w_uq, m_mla_w_ukv, m_mla_w_o, m_conv_w_in,
          m_conv_w, m_conv_w_out, m_mlp_w_up, m_mlp_w_down]
    vs = [v_b_mod, v_norm_g, v_mla_w_in, v_mla_g_q, v_mla_g_kv, v_mla_w_uq, v_mla_w_ukv, v_mla_w_o, v_conv_w_in,
          v_conv_w, v_conv_w_out, v_mlp_w_up, v_mlp_w_down]
    gs = [g_b_mod, g_norm, big_grads[0], g_gq, g_gkv, big_grads[1], big_grads[2], big_grads[3], big_grads[4],
          g_convw, big_grads[5], big_grads[6], big_grads[7]]
    grads, deltas, new_ms, new_vs = [g_w_mod], [d_w_mod], [nm_w_mod], [nv_w_mod]
    for nm, w, g, m, v in zip(names, ws, gs, ms, vs):
        d, nm_, nv_ = _adamw("adamw_" + nm, w, g, m, v)
        grads.append(g)
        deltas.append(d)
        new_ms.append(nm_)
        new_vs.append(nv_)
    return (loss, grad_x[None], *grads, *deltas, *new_ms, *new_vs)
```

```python
from typing import NamedTuple

import jax
import jax.numpy as jnp
from jax import lax
from jax.experimental import pallas as pl
from jax.experimental.pallas import tpu as pltpu

F32 = jnp.float32
BF16 = jnp.bfloat16
NORM_EPS = 1e-6
ROPE_THETA = 10000.0
QK_NOPE = 128
QK_ROPE = 64
V_HEAD = 128
LANES = 128
QK_PAD = QK_NOPE + LANES
ADAM_LR, ADAM_B1, ADAM_B2, ADAM_EPS, ADAM_WD, ADAM_STEP = 0.001, 0.9, 0.999, 1e-08, 0.01, 10
VMEM_LIMIT_BYTES = 48 * 1024 * 1024
N_CHIPS = 4
MESH_ID = pl.DeviceIdType.MESH
ANY = pl.BlockSpec(memory_space=pl.ANY)
NEG_INF = float("-inf")

DIMS_NN = (((1,), (0,)), ((), ()))
DIMS_NT = (((1,), (1,)), ((), ()))
DIMS_TN = (((0,), (0,)), ((), ()))


def _cparams(*sem):
    return pltpu.CompilerParams(dimension_semantics=sem, vmem_limit_bytes=VMEM_LIMIT_BYTES)


def _row_tile(rows, row_bytes, limit=2 * 1024 * 1024, mult=16):
    if rows * row_bytes <= limit or rows % mult:
        return rows
    best = mult
    t = mult
    while t <= rows:
        if rows % t == 0 and t * row_bytes <= limit:
            best = t
        t += mult
    return best


def _rms(v):
    return lax.rsqrt(jnp.mean(v * v, axis=-1, keepdims=True) + NORM_EPS)


def _mm(name, a, b, mode, M, N, K, outs, *, a_spec=None, b_spec=None, out_specs=None, epilogue=None,
        extras=(), alias=None, tm=1024, tn=1024, tk=2048):
    if mode == "tn":
        tk = min(tk, 512)
    tm, tn, tk = min(tm, M), min(tn, N), min(tk, K)
    assert M % tm == 0 and N % tn == 0 and K % tk == 0, (name, M, N, K)
    nk = K // tk
    if a_spec is None:
        a_spec = {"nn": pl.BlockSpec((tm, tk), lambda i, j, k: (i, k)),
                  "nt": pl.BlockSpec((tm, tk), lambda i, j, k: (i, k)),
                  "tn": pl.BlockSpec((tk, tm), lambda i, j, k: (k, i))}[mode]
    else:
        a_spec = a_spec(tm, tn, tk)
    if b_spec is None:
        b_spec = {"nn": pl.BlockSpec((tk, tn), lambda i, j, k: (k, j)),
                  "nt": pl.BlockSpec((tn, tk), lambda i, j, k: (j, k)),
                  "tn": pl.BlockSpec((tk, tn), lambda i, j, k: (k, j))}[mode]
    else:
        b_spec = b_spec(tm, tn, tk)
    if out_specs is None:
        out_specs = [pl.BlockSpec((tm, tn), lambda i, j, k: (i, j)) for _ in outs]
    else:
        out_specs = [s(tm, tn, tk) for s in out_specs]
    dims = {"nn": DIMS_NN, "nt": DIMS_NT, "tn": DIMS_TN}[mode]
    ne, no = len(extras), len(outs)

    def body(*refs):
        a_ref, b_ref = refs[0], refs[1]
        ex = refs[2:2 + ne]
        o = refs[2 + ne + (alias is not None):2 + ne + (alias is not None) + no]
        part = lax.dot_general(a_ref[...].astype(BF16), b_ref[...].astype(BF16), dims,
                               preferred_element_type=F32)

        def finish(total):
            vals = epilogue(total, *[e[...] for e in ex]) if epilogue is not None else (total,)
            for r, v in zip(o, vals):
                r[...] = v.astype(r.dtype)

        if nk == 1:
            finish(part)
        else:
            acc = refs[-1]
            kk = pl.program_id(2)

            @pl.when(kk == 0)
            def _():
                acc[...] = part

            @pl.when(kk > 0)
            def _():
                acc[...] += part

            @pl.when(kk == nk - 1)
            def _():
                finish(acc[...])

    operands = [a, b] + [e[0] for e in extras]
    in_specs = [a_spec, b_spec] + [e[1](tm, tn, tk) for e in extras]
    aliases = {}
    if alias is not None:
        operands.append(alias)
        in_specs.append(ANY)
        aliases = {len(operands) - 1: 0}
    res = pl.pallas_call(
        body, name=name, grid=(M // tm, N // tn, nk),
        in_specs=in_specs, out_specs=out_specs, out_shape=list(outs),
        scratch_shapes=[pltpu.VMEM((tm, tn), F32)] if nk > 1 else [],
        input_output_aliases=aliases,
        compiler_params=_cparams("parallel", "parallel", "arbitrary"),
    )(*operands)
    return res


def _sds(shape, dtype):
    return jax.ShapeDtypeStruct(tuple(shape), dtype)


def _rope(t, cos_p, sin_lo, sin_hi):
    return t * cos_p + pltpu.roll(t, LANES - QK_ROPE // 2, 1) * sin_lo + pltpu.roll(t, QK_ROPE // 2, 1) * sin_hi


def _rope_t(d, cos_p, sin_lo, sin_hi):
    return d * cos_p + pltpu.roll(d * sin_lo, QK_ROPE // 2, 1) + pltpu.roll(d * sin_hi, LANES - QK_ROPE // 2, 1)


def _vec_spec(d):
    return pl.BlockSpec((1, d), lambda i: (0, 0))


def _fwd_boundary(name, x_prev, y, gate, ng_post, ng_pre, sc, sh):
    S, D = x_prev.shape
    ts = min(256, S)
    has_y = y is not None
    row = pl.BlockSpec((ts, D), lambda i: (i, 0))

    def body(*refs):
        if has_y:
            x_ref, y_ref, g_ref, ngp_ref, ngn_ref, sc_ref, sh_ref, xo_ref, h_ref = refs
            yv = y_ref[...]
            xn = x_ref[...] + g_ref[...] * (yv * _rms(yv) * ngp_ref[...])
            xo_ref[...] = xn
        else:
            x_ref, ngn_ref, sc_ref, sh_ref, h_ref = refs
            xn = x_ref[...]
        hn = xn * _rms(xn) * ngn_ref[...]
        h_ref[...] = (hn * (1.0 + sc_ref[...]) + sh_ref[...]).astype(BF16)

    vec = _vec_spec(D)
    if has_y:
        operands = (x_prev, y, gate, ng_post, ng_pre, sc, sh)
        in_specs = [row, row, vec, vec, vec, vec, vec]
        out_shape = [_sds((S, D), F32), _sds((S, D), BF16)]
        out_specs = [row, row]
    else:
        operands = (x_prev, ng_pre, sc, sh)
        in_specs = [row, vec, vec, vec]
        out_shape = [_sds((S, D), BF16)]
        out_specs = [row]
    return pl.pallas_call(body, name=name, grid=(S // ts,), in_specs=in_specs, out_specs=out_specs,
                          out_shape=out_shape, compiler_params=_cparams("parallel"))(*operands)


def _acc_rows(sums_ref, rows):
    for r, v in rows:
        sums_ref[r:r + 1, :] += jnp.sum(v, axis=0, keepdims=True)


def _post_norm_bwd(dxt, yv, gate, ng_post, sums_ref, dy_ref):
    r1 = _rms(yv)
    yhat = yv * r1
    dn = dxt * gate
    u = dn * ng_post
    dy = r1 * (u - yhat * jnp.mean(u * yhat, axis=-1, keepdims=True))
    dy_ref[...] = dy.astype(dy_ref.dtype)
    _acc_rows(sums_ref, [(3, dxt * (yhat * ng_post)), (4, dn * yhat)])


def _loss_boundary(name, x_prev, y, gate, ng_post, target):
    S, D = x_prev.shape
    ts = min(256, S)
    row = pl.BlockSpec((ts, D), lambda i: (i, 0))
    vec = _vec_spec(D)

    def body(x_ref, y_ref, g_ref, ngp_ref, t_ref, dx_ref, dy_ref, sums_ref, loss_ref):
        @pl.when(pl.program_id(0) == 0)
        def _():
            sums_ref[...] = jnp.zeros_like(sums_ref)
            loss_ref[...] = jnp.zeros_like(loss_ref)

        yv = y_ref[...]
        xf = x_ref[...] + g_ref[...] * (yv * _rms(yv) * ngp_ref[...])
        err = xf - t_ref[...]
        loss_ref[...] += 0.5 * jnp.sum(jnp.mean(err * err, axis=-1, keepdims=True))
        dxt = err / D
        dx_ref[...] = dxt
        _post_norm_bwd(dxt, yv, g_ref[...], ngp_ref[...], sums_ref, dy_ref)

    return pl.pallas_call(
        body, name=name, grid=(S // ts,),
        in_specs=[row, row, vec, vec, row],
        out_specs=[row, row, pl.BlockSpec((8, D), lambda i: (0, 0)), pl.BlockSpec((8, LANES), lambda i: (0, 0))],
        out_shape=[_sds((S, D), F32), _sds((S, D), BF16), _sds((8, D), F32), _sds((8, LANES), F32)],
        compiler_params=_cparams("arbitrary"))(x_prev, y, gate, ng_post, target)


def _bwd_boundary(name, dx_new, dh, x_new, y, gate, ng_post, ng_pre, sc):
    S, D = x_new.shape
    ts = min(256, S)
    has_y = y is not None
    row = pl.BlockSpec((ts, D), lambda i: (i, 0))
    vec = _vec_spec(D)

    def body(*refs):
        if has_y:
            dxn_ref, dh_ref, x_ref, y_ref, g_ref, ngp_ref, ngn_ref, sc_ref, dxo_ref, dy_ref, sums_ref = refs
        else:
            dxn_ref, dh_ref, x_ref, ngn_ref, sc_ref, dxo_ref, sums_ref = refs

        @pl.when(pl.program_id(0) == 0)
        def _():
            sums_ref[...] = jnp.zeros_like(sums_ref)

        xv = x_ref[...]
        dhv = dh_ref[...]
        ngn = ngn_ref[...]
        r2 = _rms(xv)
        xhat = xv * r2
        dn_pre = dhv * (1.0 + sc_ref[...])
        u2 = dn_pre * ngn
        dxt = dxn_ref[...] + r2 * (u2 - xhat * jnp.mean(u2 * xhat, axis=-1, keepdims=True))
        dxo_ref[...] = dxt
        _acc_rows(sums_ref, [(0, dhv), (1, dhv * (xhat * ngn)), (2, dn_pre * xhat)])
        if has_y:
            _post_norm_bwd(dxt, y_ref[...], g_ref[...], ngp_ref[...], sums_ref, dy_ref)

    sums_spec = pl.BlockSpec((8, D), lambda i: (0, 0))
    if has_y:
        operands = (dx_new, dh, x_new, y, gate, ng_post, ng_pre, sc)
        in_specs = [row, row, row, row, vec, vec, vec, vec]
        out_shape = [_sds((S, D), F32), _sds((S, D), BF16), _sds((8, D), F32)]
        out_specs = [row, row, sums_spec]
    else:
        operands = (dx_new, dh, x_new, ng_pre, sc)
        in_specs = [row, row, row, vec, vec]
        out_shape = [_sds((S, D), F32), _sds((8, D), F32)]
        out_specs = [row, sums_spec]
    return pl.pallas_call(body, name=name, grid=(S // ts,), in_specs=in_specs, out_specs=out_specs,
                          out_shape=out_shape, compiler_params=_cparams("arbitrary"))(*operands)


def _latent_fwd(lat, g_q, g_kv, rope_tabs, rank):
    S, W = lat.shape
    ts = min(256, S)
    tab = pl.BlockSpec((ts, LANES), lambda i: (i, 0))

    def body(lat_ref, gq_ref, gkv_ref, cos_ref, slo_ref, shi_ref, cq_ref, ckv_ref, kr_ref):
        lq = lat_ref[:, 0:rank]
        lkv = lat_ref[:, rank:2 * rank]
        cq_ref[...] = (lq * _rms(lq) * gq_ref[...]).astype(BF16)
        ckv_ref[...] = (lkv * _rms(lkv) * gkv_ref[...]).astype(BF16)
        kr_ref[...] = _rope(lat_ref[:, 2 * rank:W], cos_ref[...], slo_ref[...], shi_ref[...]).astype(BF16)

    return pl.pallas_call(
        body, name="mla_latent_fwd", grid=(S // ts,),
        in_specs=[pl.BlockSpec((ts, W), lambda i: (i, 0)), _vec_spec(rank), _vec_spec(rank), tab, tab, tab],
        out_specs=[pl.BlockSpec((ts, rank), lambda i: (i, 0)), pl.BlockSpec((ts, rank), lambda i: (i, 0)), tab],
        out_shape=[_sds((S, rank), BF16), _sds((S, rank), BF16), _sds((S, LANES), BF16)],
        compiler_params=_cparams("parallel"))(lat, g_q, g_kv, *rope_tabs)


def _latent_bwd(lat, dcq, dckv, dkr, g_q, g_kv, rope_tabs, rank):
    S, W = lat.shape
    ts = min(256, S)
    tab = pl.BlockSpec((ts, LANES), lambda i: (i, 0))
    half = pl.BlockSpec((ts, rank), lambda i: (i, 0))

    def body(lat_ref, dcq_ref, dckv_ref, dkr_ref, gq_ref, gkv_ref, cos_ref, slo_ref, shi_ref, dlat_ref, sums_ref):
        @pl.when(pl.program_id(0) == 0)
        def _():
            sums_ref[...] = jnp.zeros_like(sums_ref)

        def norm_bwd(v, dn, g, r):
            rr = _rms(v)
            vhat = v * rr
            u = dn * g
            sums_ref[r:r + 1, :] += jnp.sum(dn * vhat, axis=0, keepdims=True)
            return rr * (u - vhat * jnp.mean(u * vhat, axis=-1, keepdims=True))

        dlat_ref[:, 0:rank] = norm_bwd(lat_ref[:, 0:rank], dcq_ref[...], gq_ref[...], 0).astype(BF16)
        dlat_ref[:, rank:2 * rank] = norm_bwd(lat_ref[:, rank:2 * rank], dckv_ref[...], gkv_ref[...], 1).astype(BF16)
        dlat_ref[:, 2 * rank:W] = _rope_t(dkr_ref[...], cos_ref[...], slo_ref[...], shi_ref[...]).astype(BF16)

    return pl.pallas_call(
        body, name="mla_latent_bwd", grid=(S // ts,),
        in_specs=[pl.BlockSpec((ts, W), lambda i: (i, 0)), half, half, tab, _vec_spec(rank), _vec_spec(rank),
                  tab, tab, tab],
        out_specs=[pl.BlockSpec((ts, W), lambda i: (i, 0)), pl.BlockSpec((8, rank), lambda i: (0, 0))],
        out_shape=[_sds((S, W), BF16), _sds((8, rank), F32)],
        compiler_params=_cparams("arbitrary"))(lat, dcq, dckv, dkr, g_q, g_kv, *rope_tabs)


def _attn_tiles(S):
    t = min(512, S)
    return t, S // t


class Rider(NamedTuple):
    operands: tuple
    out_shape: tuple
    aliases: dict
    sems: tuple
    start: object
    finish: object


NO_RIDER = Rider((), (), {}, (), None, None)


def _causal_mask(t):
    return lax.broadcasted_iota(jnp.int32, (t, t), 1) <= lax.broadcasted_iota(jnp.int32, (t, t), 0)


def _attn_fwd(q, kv, kr, heads, scale, rider=NO_RIDER):
    S = q.shape[0]
    t, nb = _attn_tiles(S)
    n_ri, n_ro = len(rider.operands), len(rider.out_shape)

    def body(*refs):
        q_ref, kv_ref, kr_ref = refs[:3]
        r_in = refs[3:3 + n_ri]
        o_ref, lse_ref = refs[3 + n_ri:5 + n_ri]
        r_out = refs[5 + n_ri:5 + n_ri + n_ro]
        m_scr, l_scr, acc_scr = refs[5 + n_ri + n_ro:8 + n_ri + n_ro]
        r_sems = refs[8 + n_ri + n_ro:]
        h, qi, ki = pl.program_id(0), pl.program_id(1), pl.program_id(2)

        if rider.start is not None:
            @pl.when((h == 0) & (qi == 0) & (ki == 0))
            def _():
                rider.start(r_in, r_out, r_sems)

        @pl.when(ki == 0)
        def _():
            m_scr[...] = jnp.full_like(m_scr, NEG_INF)
            l_scr[...] = jnp.zeros_like(l_scr)
            acc_scr[...] = jnp.zeros_like(acc_scr)

        def step(diagonal):
            kcat = jnp.concatenate([kv_ref[:, 0:QK_NOPE], kr_ref[...]], axis=1)
            s = lax.dot_general(q_ref[...], kcat, DIMS_NT, preferred_element_type=F32) * scale
            if diagonal:
                s = jnp.where(_causal_mask(t), s, NEG_INF)
            m_prev = m_scr[...]
            m_new = jnp.maximum(m_prev, jnp.max(s, axis=-1, keepdims=True))
            alpha = jnp.exp(m_prev - m_new)
            p = jnp.exp(s - m_new)
            l_scr[...] = alpha * l_scr[...] + jnp.sum(p, axis=-1, keepdims=True)
            acc_scr[...] = alpha * acc_scr[...] + lax.dot_general(
                p.astype(BF16), kv_ref[:, QK_NOPE:QK_NOPE + V_HEAD], DIMS_NN, preferred_element_type=F32)
            m_scr[...] = m_new

        @pl.when(ki < qi)
        def _():
            step(False)

        @pl.when(ki == qi)
        def _():
            step(True)

        @pl.when(ki == nb - 1)
        def _():
            o_ref[...] = (acc_scr[...] / l_scr[...]).astype(BF16)
            lse_ref[...] = m_scr[...] + jnp.log(l_scr[...])

        if rider.finish is not None:
            @pl.when((h == heads - 1) & (qi == nb - 1) & (ki == nb - 1))
            def _():
                rider.finish(r_in, r_out, r_sems)

    res = pl.pallas_call(
        body, name="mla_attn_fwd", grid=(heads, nb, nb),
        in_specs=[pl.BlockSpec((t, QK_PAD), lambda h, qi, ki: (qi, h)),
                  pl.BlockSpec((t, QK_PAD), lambda h, qi, ki: (jnp.minimum(ki, qi), h)),
                  pl.BlockSpec((t, LANES), lambda h, qi, ki: (jnp.minimum(ki, qi), 0))] + [ANY] * n_ri,
        out_specs=[pl.BlockSpec((t, V_HEAD), lambda h, qi, ki: (qi, h)),
                   pl.BlockSpec((None, t, 1), lambda h, qi, ki: (h, qi, 0))] + [ANY] * n_ro,
        out_shape=[_sds((S, heads * V_HEAD), BF16), _sds((heads, S, 1), F32)] + list(rider.out_shape),
        scratch_shapes=[pltpu.VMEM((t, 1), F32), pltpu.VMEM((t, 1), F32), pltpu.VMEM((t, V_HEAD), F32)]
        + list(rider.sems),
        input_output_aliases={3 + i: 2 + o for i, o in rider.aliases.items()},
        compiler_params=_cparams("arbitrary", "arbitrary", "arbitrary"))(q, kv, kr, *rider.operands)
    return res[0], res[1], res[2:]


def _attn_bwd(q, kv, kr, o, do, lse, rope_tabs, heads, scale, rider=NO_RIDER):
    S = q.shape[0]
    t, nb = _attn_tiles(S)
    n_ri, n_ro = len(rider.operands), len(rider.out_shape)

    def body(*refs):
        q_ref, kv_ref, kr_ref, o_ref, do_ref, lse_ref, cos_ref, slo_ref, shi_ref = refs[:9]
        r_in = refs[9:9 + n_ri]
        dq_ref, dkv_ref, dkr_ref = refs[9 + n_ri:12 + n_ri]
        r_out = refs[12 + n_ri:12 + n_ri + n_ro]
        dq_scr, dk_scr, dv_scr, dkr_scr = refs[12 + n_ri + n_ro:16 + n_ri + n_ro]
        r_sems = refs[16 + n_ri + n_ro:]
        h, ki, qi = pl.program_id(0), pl.program_id(1), pl.program_id(2)
        q_rows = pl.ds(pl.multiple_of(qi * t, t), t)
        k_rows = pl.ds(pl.multiple_of(ki * t, t), t)

        if rider.start is not None:
            @pl.when((h == 0) & (ki == 0) & (qi == 0))
            def _():
                rider.start(r_in, r_out, r_sems)

        @pl.when((ki == 0) & (qi == 0))
        def _():
            dq_scr[...] = jnp.zeros_like(dq_scr)

        @pl.when((h == 0) & (ki == 0) & (qi == 0))
        def _():
            dkr_scr[...] = jnp.zeros_like(dkr_scr)

        @pl.when(qi == 0)
        def _():
            dk_scr[...] = jnp.zeros_like(dk_scr)
            dv_scr[...] = jnp.zeros_like(dv_scr)

        def step(diagonal):
            qv = q_ref[...]
            kcat = jnp.concatenate([kv_ref[:, 0:QK_NOPE], kr_ref[...]], axis=1)
            s = lax.dot_general(qv, kcat, DIMS_NT, preferred_element_type=F32) * scale
            p = jnp.exp(s - lse_ref[...])
            if diagonal:
                p = jnp.where(_causal_mask(t), p, 0.0)
            dov = do_ref[...]
            dv_scr[...] += lax.dot_general(p.astype(BF16), dov, DIMS_TN, preferred_element_type=F32)
            dp = lax.dot_general(dov, kv_ref[:, QK_NOPE:QK_NOPE + V_HEAD], DIMS_NT, preferred_element_type=F32)
            delta = jnp.sum(dov.astype(F32) * o_ref[...].astype(F32), axis=-1, keepdims=True)
            ds = (p * (dp - delta) * scale).astype(BF16)
            dk_scr[...] += lax.dot_general(ds, qv, DIMS_TN, preferred_element_type=F32)
            dq_scr[q_rows, :] += lax.dot_general(ds, kcat, DIMS_NN, preferred_element_type=F32)

        @pl.when(qi > ki)
        def _():
            step(False)

        @pl.when(qi == ki)
        def _():
            step(True)

        @pl.when(qi == nb - 1)
        def _():
            dkv_ref[...] = jnp.concatenate([dk_scr[:, 0:QK_NOPE], dv_scr[...]], axis=1).astype(BF16)
            dkr_scr[k_rows, :] += dk_scr[:, QK_NOPE:QK_PAD]

        @pl.when(ki == nb - 1)
        def _():
            dqv = dq_scr[q_rows, :]
            dq_ref[q_rows, :] = jnp.concatenate(
                [dqv[:, 0:QK_NOPE], _rope_t(dqv[:, QK_NOPE:QK_PAD], cos_ref[...], slo_ref[...], shi_ref[...])],
                axis=1).astype(BF16)

        @pl.when((h == heads - 1) & (ki == nb - 1) & (qi == nb - 1))
        def _():
            dkr_ref[...] = dkr_scr[...]
            if rider.finish is not None:
                rider.finish(r_in, r_out, r_sems)

    qmap = lambda h, ki, qi: (jnp.maximum(qi, ki), h)
    tab = pl.BlockSpec((t, LANES), lambda h, ki, qi: (qi, 0))
    res = pl.pallas_call(
        body, name="mla_attn_bwd", grid=(heads, nb, nb),
        in_specs=[pl.BlockSpec((t, QK_PAD), qmap),
                  pl.BlockSpec((t, QK_PAD), lambda h, ki, qi: (ki, h)),
                  pl.BlockSpec((t, LANES), lambda h, ki, qi: (ki, 0)),
                  pl.BlockSpec((t, V_HEAD), qmap),
                  pl.BlockSpec((t, V_HEAD), qmap),
                  pl.BlockSpec((None, t, 1), lambda h, ki, qi: (h, jnp.maximum(qi, ki), 0)),
                  tab, tab, tab] + [ANY] * n_ri,
        out_specs=[pl.BlockSpec((S, QK_PAD), lambda h, ki, qi: (0, h)),
                   pl.BlockSpec((t, QK_PAD), lambda h, ki, qi: (ki, h)),
                   pl.BlockSpec((S, LANES), lambda h, ki, qi: (0, 0))] + [ANY] * n_ro,
        out_shape=[_sds((S, heads * QK_PAD), BF16), _sds((S, heads * QK_PAD), BF16), _sds((S, LANES), F32)]
        + list(rider.out_shape),
        scratch_shapes=[pltpu.VMEM((S, QK_PAD), F32), pltpu.VMEM((t, QK_PAD), F32), pltpu.VMEM((t, V_HEAD), F32),
                        pltpu.VMEM((S, LANES), F32)] + list(rider.sems),
        input_output_aliases={9 + i: 3 + o for i, o in rider.aliases.items()},
        compiler_params=_cparams("arbitrary", "arbitrary", "arbitrary"))(q, kv, kr, o, do, lse, *rope_tabs, *rider.operands)
    return res[0], res[1], res[2], res[3:]


def _shift_down(z, n, rows):
    return jnp.where(rows >= n, pltpu.roll(z, n, 0), 0.0)


def _shift_up(z, n, rows, S):
    return jnp.where(rows < S - n, pltpu.roll(z, S - n, 0), 0.0)


def _conv_specs(S, tc):
    strip = lambda p: pl.BlockSpec((None, S, tc), lambda j: (p, 0, j))
    return strip(0), strip(1), strip(2), pl.BlockSpec((3, tc), lambda j: (0, j))


def _conv_fwd(proj3, w):
    _, S, D = proj3.shape
    tc = LANES

    def body(b_ref, c_ref, u_ref, w_ref, out_ref):
        z = c_ref[...] * u_ref[...]
        rows = lax.broadcasted_iota(jnp.int32, (S, tc), 0)
        zc = w_ref[0:1, :] * _shift_down(z, 2, rows) + w_ref[1:2, :] * _shift_down(z, 1, rows) + w_ref[2:3, :] * z
        out_ref[...] = (b_ref[...] * zc).astype(BF16)

    return pl.pallas_call(
        body, name="conv_fwd", grid=(D // tc,), in_specs=list(_conv_specs(S, tc)),
        out_specs=pl.BlockSpec((S, tc), lambda j: (0, j)), out_shape=_sds((S, D), BF16),
        compiler_params=_cparams("parallel"))(proj3, proj3, proj3, w)


def _conv_bwd(dbz, proj3, w):
    _, S, D = proj3.shape
    tc = LANES

    def body(d_ref, b_ref, c_ref, u_ref, w_ref, dp_ref, dw_ref):
        cv, uv, dv = c_ref[...], u_ref[...], d_ref[...]
        z = cv * uv
        rows = lax.broadcasted_iota(jnp.int32, (S, tc), 0)
        z1, z2 = _shift_down(z, 1, rows), _shift_down(z, 2, rows)
        zc = w_ref[0:1, :] * z2 + w_ref[1:2, :] * z1 + w_ref[2:3, :] * z
        dp_ref[0] = (dv * zc).astype(BF16)
        dzc = dv * b_ref[...]
        dz = w_ref[2:3, :] * dzc + w_ref[1:2, :] * _shift_up(dzc, 1, rows, S) + w_ref[0:1, :] * _shift_up(dzc, 2, rows, S)
        dp_ref[1] = (dz * uv).astype(BF16)
        dp_ref[2] = (dz * cv).astype(BF16)
        dw_ref[0:1, :] = jnp.sum(dzc * z2, axis=0, keepdims=True)
        dw_ref[1:2, :] = jnp.sum(dzc * z1, axis=0, keepdims=True)
        dw_ref[2:3, :] = jnp.sum(dzc * z, axis=0, keepdims=True)

    sb, sc_, su, sw = _conv_specs(S, tc)
    return pl.pallas_call(
        body, name="conv_bwd", grid=(D // tc,),
        in_specs=[pl.BlockSpec((S, tc), lambda j: (0, j)), sb, sc_, su, sw],
        out_specs=[pl.BlockSpec((3, S, tc), lambda j: (0, 0, j)), pl.BlockSpec((3, tc), lambda j: (0, j))],
        out_shape=[_sds((3, S, D), BF16), _sds((3, D), F32)],
        compiler_params=_cparams("parallel"))(dbz, proj3, proj3, proj3, w)


def _silu(c_all):
    def body(c_ref, o_ref):
        cv = c_ref[...]
        o_ref[...] = cv * (1.0 / (1.0 + jnp.exp(-cv)))

    vm = pl.BlockSpec(memory_space=pltpu.VMEM)
    return pl.pallas_call(body, name="cond_silu", in_specs=[vm], out_specs=vm, out_shape=_sds(c_all.shape, F32))(c_all)


def _mod_fwd(cond, w_mod, b_cols):
    L, D, ncol = w_mod.shape
    B = cond.shape[0]
    tk, tn = min(512, D), min(1024, ncol)
    nk = D // tk

    def body(c_ref, w_ref, b_ref, out_ref, acc):
        kk = pl.program_id(2)
        part = lax.dot_general(c_ref[...].astype(BF16), w_ref[...].astype(BF16), DIMS_NN, preferred_element_type=F32)

        @pl.when(kk == 0)
        def _():
            acc[...] = part

        @pl.when(kk > 0)
        def _():
            acc[...] += part

        @pl.when(kk == nk - 1)
        def _():
            out_ref[...] = acc[...] + b_ref[...]

    return pl.pallas_call(
        body, name="mod_fwd", grid=(L, ncol // tn, nk),
        in_specs=[pl.BlockSpec((B, tk), lambda l, j, k: (0, k)),
                  pl.BlockSpec((None, tk, tn), lambda l, j, k: (l, k, j)),
                  pl.BlockSpec((None, 1, tn), lambda l, j, k: (l, 0, j))],
        out_specs=pl.BlockSpec((None, B, tn), lambda l, j, k: (l, 0, j)),
        out_shape=_sds((L, B, ncol), F32),
        scratch_shapes=[pltpu.VMEM((B, tn), F32)],
        compiler_params=_cparams("parallel", "parallel", "arbitrary"))(cond, w_mod, b_cols)


def _adamw_math(w, g, m, v):
    m = ADAM_B1 * m + (1.0 - ADAM_B1) * g
    v = ADAM_B2 * v + (1.0 - ADAM_B2) * (g * g)
    m_hat = m / (1.0 - ADAM_B1 ** ADAM_STEP)
    v_hat = v / (1.0 - ADAM_B2 ** ADAM_STEP)
    delta = -ADAM_LR * (m_hat / (jnp.sqrt(v_hat) + ADAM_EPS) + ADAM_WD * w)
    return delta, m, v


def _adamw(name, w, g, m, v):
    shape = w.shape
    cols = shape[-1] if w.ndim <= 3 else shape[-2] * shape[-1]
    rows = w.size // cols
    w2, g2, m2, v2 = (t.reshape(rows, cols) for t in (w, g, m, v))
    tr = _row_tile(rows, cols * 4, limit=1024 * 1024, mult=8)
    spec = pl.BlockSpec((tr, cols), lambda i: (i, 0))

    def body(w_ref, g_ref, m_ref, v_ref, d_ref, nm_ref, nv_ref):
        d, nm, nv = _adamw_math(w_ref[...], g_ref[...], m_ref[...], v_ref[...])
        d_ref[...] = d
        nm_ref[...] = nm
        nv_ref[...] = nv

    outs = pl.pallas_call(body, name=name, grid=(rows // tr,), in_specs=[spec] * 4, out_specs=[spec] * 3,
                          out_shape=[_sds((rows, cols), F32)] * 3, compiler_params=_cparams("parallel"))(w2, g2, m2, v2)
    return tuple(t.reshape(shape) for t in outs)


def _adamw_mod(w, cond_t, dmod_cols, m, v):
    L, D, ncol = w.shape
    B = cond_t.shape[1]
    tr, tc = min(256, D), min(1024, ncol)
    blk = pl.BlockSpec((None, tr, tc), lambda l, i, j: (l, i, j))

    def body(w_ref, ct_ref, dm_ref, m_ref, v_ref, g_ref, d_ref, nm_ref, nv_ref):
        g = lax.dot_general(ct_ref[...], dm_ref[...], DIMS_NN, precision=lax.Precision.HIGHEST,
                            preferred_element_type=F32)
        d, nm, nv = _adamw_math(w_ref[...], g, m_ref[...], v_ref[...])
        g_ref[...] = g
        d_ref[...] = d
        nm_ref[...] = nm
        nv_ref[...] = nv

    return pl.pallas_call(
        body, name="adamw_w_mod", grid=(L, D // tr, ncol // tc),
        in_specs=[blk, pl.BlockSpec((tr, B), lambda l, i, j: (i, 0)),
                  pl.BlockSpec((None, B, tc), lambda l, i, j: (l, 0, j)), blk, blk],
        out_specs=[blk] * 4, out_shape=[_sds((L, D, ncol), F32)] * 4,
        compiler_params=_cparams("parallel", "parallel", "parallel"))(w, cond_t, dmod_cols, m, v)


def _cast_into_full(name, w, kind, k_idx):
    L, R, C = w.shape
    Rh = R // 2
    tr = _row_tile(Rh, C * 4)
    if kind == "row":
        out_shape = (L, N_CHIPS, 2, Rh, C)
        out_spec = pl.BlockSpec((None, None, None, tr, C), lambda l, h, i, k_ref: (l, k_ref[0], h, i, 0))
    else:
        out_shape = (L, 2, Rh, N_CHIPS * C)
        out_spec = pl.BlockSpec((None, None, tr, C), lambda l, h, i, k_ref: (l, h, i, k_ref[0]))

    def body(k_ref, w_ref, o_ref):
        o_ref[...] = w_ref[...].astype(BF16)

    return pl.pallas_call(
        body, name=name,
        grid_spec=pltpu.PrefetchScalarGridSpec(
            num_scalar_prefetch=1, grid=(L, 2, Rh // tr),
            in_specs=[pl.BlockSpec((None, None, tr, C), lambda l, h, i, k_ref: (l, h, i, 0))],
            out_specs=out_spec),
        out_shape=_sds(out_shape, BF16),
        compiler_params=_cparams("parallel", "parallel", "parallel"))(k_idx, w.reshape(L, 2, Rh, C))


def _pair_sum(name, g5, ra, c_idx):
    L, A, _, Rh, Cc = g5.shape
    tr = _row_tile(Rh, Cc * 4)

    def body(c_ref, g_ref, r_ref, o_ref):
        o_ref[...] = (g_ref[...].astype(F32) + r_ref[...].astype(F32)).astype(BF16)

    blk = pl.BlockSpec((None, None, tr, Cc), lambda l, a, i, c_ref: (l, a, i, 0))
    return pl.pallas_call(
        body, name=name,
        grid_spec=pltpu.PrefetchScalarGridSpec(
            num_scalar_prefetch=1, grid=(L, A, Rh // tr),
            in_specs=[pl.BlockSpec((None, None, None, tr, Cc), lambda l, a, i, c_ref: (l, a, c_ref[0], i, 0)), blk],
            out_specs=blk),
        out_shape=_sds((L, A, Rh, Cc), BF16),
        compiler_params=_cparams("parallel", "parallel", "parallel"))(c_idx, g5, ra)


def _chip_sum(name, p, rb, kc_idx, kind):
    L, A, Rh, Cc = p.shape
    C = rb.shape[-1]
    tr = _row_tile(Rh, C * 4)
    if kind == "row":
        own = pl.BlockSpec((None, None, tr, C), lambda l, i, kc: (l, kc[0], i, 0))
    else:
        own = pl.BlockSpec((None, None, tr, C), lambda l, i, kc: (l, 0, i, kc[0]))
    peer = lambda j: pl.BlockSpec((None, None, tr, C), lambda l, i, kc: (j, l, i, 0))

    def body(kc_ref, p_ref, r0_ref, r1_ref, r2_ref, o_ref):
        o_ref[...] = ((p_ref[...].astype(F32) + r0_ref[...].astype(F32)) + r1_ref[...].astype(F32)) + r2_ref[...].astype(F32)

    return pl.pallas_call(
        body, name=name,
        grid_spec=pltpu.PrefetchScalarGridSpec(
            num_scalar_prefetch=1, grid=(L, Rh // tr),
            in_specs=[own, peer(0), peer(1), peer(2)],
            out_specs=pl.BlockSpec((None, None, tr, C), lambda l, i, kc: (l, kc[1], i, 0))),
        out_shape=_sds((L, 2, Rh, C), F32),
        compiler_params=_cparams("parallel", "parallel"))(kc_idx, p, rb, rb, rb)


def _mesh_place():
    x, y, c = lax.axis_index("x"), lax.axis_index("y"), lax.axis_index("c")
    chips = [(1 - x, y), (x, 1 - y), (1 - x, 1 - y)]
    return x, y, c, chips


def _remote(src, dst, send_sem, recv_sem, to):
    return pltpu.make_async_remote_copy(src_ref=src, dst_ref=dst, send_sem=send_sem, recv_sem=recv_sem,
                                        device_id=to, device_id_type=MESH_ID)


def _small_allgather(name, v, with_sum=False):
    R, N = v.shape

    def body(*refs):
        if with_sum:
            x_ref, out_ref, sum_ref, send_sems, recv_sems, local_sem = refs
        else:
            x_ref, out_ref, send_sems, recv_sems, local_sem = refs
        x, y, c, chips = _mesh_place()
        me, sibling = (x, y, c), (x, y, 1 - c)

        def rows(px, py, pc):
            return out_ref.at[pl.ds((4 * px + 2 * py + pc) * R, R), :]

        def copy(k, block, to, src=None):
            return _remote(rows(*block) if src is None else src, rows(*block), send_sems.at[k], recv_sems.at[k], to)

        mine = pltpu.make_async_copy(x_ref, rows(*me), local_sem)
        mine.start()
        first = [copy(0, me, sibling, src=x_ref)]
        first += [copy(1 + j, me, (*chip, c), src=x_ref) for j, chip in enumerate(chips)]
        for cp in first:
            cp.start()
        passed = [copy(4 + j, (*chip, c), sibling) for j, chip in enumerate(chips)]
        for j, chip in enumerate(chips):
            copy(1 + j, (*chip, c), me).wait_recv()
            passed[j].start()
        copy(0, sibling, me).wait_recv()
        for j, chip in enumerate(chips):
            copy(4 + j, (*chip, 1 - c), me).wait_recv()
        for cp in first + passed:
            cp.wait_send()
        mine.wait()
        if with_sum:
            total = out_ref[0:R, :]
            for p in range(1, 8):
                total = total + out_ref[p * R:(p + 1) * R, :]
            sum_ref[...] = total

    vm = pl.BlockSpec(memory_space=pltpu.VMEM)
    out_shape = [_sds((8 * R, N), F32)] + ([_sds((R, N), F32)] if with_sum else [])
    res = pl.pallas_call(
        body, name=name, out_shape=out_shape, in_specs=[vm], out_specs=[vm] * len(out_shape),
        scratch_shapes=[pltpu.SemaphoreType.DMA((7,)), pltpu.SemaphoreType.DMA((7,)), pltpu.SemaphoreType.DMA],
        compiler_params=pltpu.CompilerParams(vmem_limit_bytes=VMEM_LIMIT_BYTES))(v)
    return res if with_sum else res[0]


def _full_place(ref, kind, C, kk, half):
    if kind == "row":
        return ref.at[:, kk, half]
    return ref.at[:, half, :, pl.ds(pl.multiple_of(kk * C, LANES), C)]


def _gather_rider(fulls, kinds, shard_cols):
    n = len(fulls)

    def copies(outs, sems):
        x, y, c, chips = _mesh_place()
        k = 2 * x + y
        place = lambda a, kk, half: _full_place(outs[a], kinds[a], shard_cols[a], kk, half)
        copy = lambda a, j, ref, to: _remote(ref, ref, sems[0].at[6 * a + j], sems[1].at[6 * a + j], to)
        return (x, y, c), chips, k, place, copy

    def start(_, outs, sems):
        (x, y, c), chips, k, place, copy = copies(outs, sems)
        for j, chip in enumerate(chips):
            for a in range(n):
                copy(a, j, place(a, k, c), (*chip, c)).start()

    def finish(_, outs, sems):
        (x, y, c), chips, k, place, copy = copies(outs, sems)
        me, sibling = (x, y, c), (x, y, 1 - c)
        for j, chip in enumerate(chips):
            kj = 2 * chip[0] + chip[1]
            for a in range(n):
                copy(a, j, place(a, kj, c), me).wait_recv()
                copy(a, 3 + j, place(a, kj, c), sibling).start()
        for j, chip in enumerate(chips):
            kj = 2 * chip[0] + chip[1]
            for a in range(n):
                copy(a, 3 + j, place(a, kj, 1 - c), me).wait_recv()
        for j, chip in enumerate(chips):
            kj = 2 * chip[0] + chip[1]
            for a in range(n):
                copy(a, j, place(a, k, c), (*chip, c)).wait_send()
                copy(a, 3 + j, place(a, kj, c), sibling).wait_send()

    return Rider(tuple(fulls), tuple(_sds(f.shape, BF16) for f in fulls), {a: a for a in range(n)},
                 (pltpu.SemaphoreType.DMA((6 * n,)), pltpu.SemaphoreType.DMA((6 * n,))), start, finish)


def _scatter_rider(ps, kinds, shard_cols):
    n = len(ps)

    def copies(ins, outs, sems):
        x, y, c, chips = _mesh_place()
        cps = []
        for j, chip in enumerate(chips):
            kj = 2 * chip[0] + chip[1]
            for a in range(n):
                C = shard_cols[a]
                src = ins[a].at[:, kj] if kinds[a] == "row" else ins[a].at[:, 0, :, pl.ds(pl.multiple_of(kj * C, LANES), C)]
                cps.append(_remote(src, outs[a].at[j], sems[0].at[3 * a + j], sems[1].at[3 * a + j], (*chip, c)))
        return cps

    def start(ins, outs, sems):
        for cp in copies(ins, outs, sems):
            cp.start()

    def finish(ins, outs, sems):
        cps = copies(ins, outs, sems)
        for cp in cps:
            cp.wait_recv()
        for cp in cps:
            cp.wait_send()

    out_shape = tuple(_sds((3, p.shape[0], p.shape[2], C), BF16) for p, C in zip(ps, shard_cols))
    return Rider(tuple(ps), out_shape, {}, (pltpu.SemaphoreType.DMA((3 * n,)), pltpu.SemaphoreType.DMA((3 * n,))),
                 start, finish)


def _run_rider(name, rider):
    n_in, n_out = len(rider.operands), len(rider.out_shape)

    def body(*refs):
        ins, outs, sems = refs[:n_in], refs[n_in:n_in + n_out], refs[n_in + n_out:]
        rider.start(ins, outs, sems)
        rider.finish(ins, outs, sems)

    return pl.pallas_call(
        body, name=name, out_shape=list(rider.out_shape), in_specs=[ANY] * n_in, out_specs=[ANY] * n_out,
        input_output_aliases=dict(rider.aliases), scratch_shapes=list(rider.sems),
        compiler_params=pltpu.CompilerParams(vmem_limit_bytes=VMEM_LIMIT_BYTES))(*rider.operands)


def _pair_exchange(name, g5s):
    n = len(g5s)

    def body(*refs):
        ins, outs = refs[:n], refs[n:2 * n]
        send_sems, recv_sems = refs[2 * n:]
        x, y, c, _ = _mesh_place()
        cps = []
        for a in range(n):
            cp = _remote(ins[a].at[:, :, 1 - c], outs[a], send_sems.at[a], recv_sems.at[a], (x, y, 1 - c))
            cp.start()
            cps.append(cp)
        for cp in cps:
            cp.wait_recv()
        for cp in cps:
            cp.wait_send()

    out_shape = [_sds((g.shape[0], g.shape[1], g.shape[3], g.shape[4]), BF16) for g in g5s]
    return pl.pallas_call(
        body, name=name, out_shape=out_shape, in_specs=[ANY] * n, out_specs=[ANY] * n,
        scratch_shapes=[pltpu.SemaphoreType.DMA((n,)), pltpu.SemaphoreType.DMA((n,))],
        compiler_params=pltpu.CompilerParams(vmem_limit_bytes=VMEM_LIMIT_BYTES))(*g5s)


def _pair_share(fs):
    n = len(fs)

    def body(*refs):
        outs = refs[n:2 * n]
        send_sems, recv_sems = refs[2 * n:]
        x, y, c, _ = _mesh_place()
        cps = []
        for a in range(n):
            mine = outs[a].at[:, c]
            cp = _remote(mine, mine, send_sems.at[a], recv_sems.at[a], (x, y, 1 - c))
            cp.start()
            cps.append(cp)
        for a in range(n):
            theirs = outs[a].at[:, 1 - c]
            _remote(theirs, theirs, send_sems.at[a], recv_sems.at[a], (x, y, c)).wait_recv()
        for cp in cps:
            cp.wait_send()

    return pl.pallas_call(
        body, name="grad_pair_share", out_shape=[_sds(f.shape, F32) for f in fs],
        in_specs=[ANY] * n, out_specs=[ANY] * n, input_output_aliases={a: a for a in range(n)},
        scratch_shapes=[pltpu.SemaphoreType.DMA((n,)), pltpu.SemaphoreType.DMA((n,))],
        compiler_params=pltpu.CompilerParams(vmem_limit_bytes=VMEM_LIMIT_BYTES))(*fs)


def _pack_rows(parts, lane_mult=1024):
    flat = jnp.concatenate([p.reshape(-1).astype(F32) for p in parts])
    n = -(-flat.shape[0] // (8 * lane_mult)) * lane_mult
    return jnp.pad(flat, (0, 8 * n - flat.shape[0])).reshape(8, n)


def _relu2(acc):
    r = jnp.maximum(acc, 0.0)
    return r, r * r


def _times_2r(acc, r):
    return (acc * (2.0 * r.astype(F32)),)


def kernel(x, c, positions, w_mod, b_mod, norm_g, mla_w_in, mla_g_q, mla_g_kv, mla_w_uq, mla_w_ukv, mla_w_o, conv_w_in, conv_w, conv_w_out, mlp_w_up, mlp_w_down, loss_target, m_w_mod, m_b_mod, m_norm_g, m_mla_w_in, m_mla_g_q, m_mla_g_kv, m_mla_w_uq, m_mla_w_ukv, m_mla_w_o, m_conv_w_in, m_conv_w, m_conv_w_out, m_mlp_w_up, m_mlp_w_down, v_w_mod, v_b_mod, v_norm_g, v_mla_w_in, v_mla_g_q, v_mla_g_kv, v_mla_w_uq, v_mla_w_ukv, v_mla_w_o, v_conv_w_in, v_conv_w, v_conv_w_out, v_mlp_w_up, v_mlp_w_down):
    S, D = x.shape[1], x.shape[2]
    Dq = D // N_CHIPS
    ncol = w_mod.shape[2]
    n_mod = N_CHIPS * ncol // D
    F = mlp_w_up.shape[2] * N_CHIPS
    lat_dim = mla_w_in.shape[2]
    rank = mla_g_q.shape[1]
    H = mla_w_uq.shape[2]
    d_qk = mla_w_uq.shape[3]
    assert mla_g_kv.shape[1] == rank and lat_dim == 2 * rank + QK_ROPE and d_qk == QK_NOPE + QK_ROPE
    assert mla_w_ukv.shape[3] == QK_NOPE + V_HEAD and x.shape[0] == 1 and n_mod == 6
    assert norm_g.shape[0] == 2 and mla_w_in.shape[0] == 1 and conv_w_in.shape[0] == 1
    lat_pad = 2 * rank + LANES
    scale = float(d_qk) ** -0.5

    xi, yi, ci = lax.axis_index("x"), lax.axis_index("y"), lax.axis_index("c")
    chip = 2 * xi + yi
    dev = 2 * chip + ci
    c_idx = jnp.reshape(ci, (1,)).astype(jnp.int32)
    k_idx = jnp.reshape(chip, (1,)).astype(jnp.int32)

    n1 = D + 2 * D + 3 * Dq
    g1 = _small_allgather("gather_small_inputs", _pack_rows([c, norm_g, conv_w])).reshape(8, -1)
    c_all = g1[:, :D]
    by_chip = g1[0::2]
    norm_full = jnp.concatenate([by_chip[kk, D:3 * D].reshape(2, 4, Dq) for kk in range(N_CHIPS)], axis=-1)
    convw_full = jnp.concatenate([by_chip[kk, 3 * D:n1].reshape(3, Dq) for kk in range(N_CHIPS)], axis=-1)

    b_cols = lax.dynamic_slice(b_mod, (0, chip * ncol), (2, ncol)).reshape(2, 1, ncol)
    cond_all = _silu(c_all)
    mod_cols = _mod_fwd(cond_all, w_mod, b_cols)
    g2 = _small_allgather("gather_mod", _pack_rows([mod_cols]))
    g2 = g2.reshape(8, -1)[0::2, :2 * 8 * ncol].reshape(N_CHIPS, 2, 8, ncol)
    mod_all = jnp.transpose(g2, (2, 1, 0, 3)).reshape(8, 2, n_mod * D)
    mod_me = lax.dynamic_index_in_dim(mod_all, dev, axis=0, keepdims=False)
    mods = [[mod_me[l, i * D:(i + 1) * D].reshape(1, D) for i in range(n_mod)] for l in range(2)]
    ng = [[norm_full[l, i].reshape(1, D) for i in range(4)] for l in range(2)]

    pos = positions[0].astype(F32)
    inv_freq = ROPE_THETA ** (-jnp.arange(0, QK_ROPE, 2, dtype=F32) / QK_ROPE)
    ang = pos[:, None] * inv_freq
    cos, sin = jnp.cos(ang), jnp.sin(ang)
    zero = jnp.zeros_like(cos)
    rope_tabs = (jnp.concatenate([cos, cos, zero, zero], axis=1),
                 jnp.concatenate([-sin, zero, zero, zero], axis=1),
                 jnp.concatenate([zero, sin, zero, zero], axis=1))

    weights = [("mla_w_in", mla_w_in, "row"), ("mla_w_uq", mla_w_uq.reshape(1, rank // N_CHIPS, H * d_qk), "row"),
               ("mla_w_ukv", mla_w_ukv.reshape(1, rank // N_CHIPS, H * QK_PAD), "row"), ("mla_w_o", mla_w_o, "row"),
               ("conv_w_in", conv_w_in, "col"), ("conv_w_out", conv_w_out, "row"),
               ("mlp_w_up", mlp_w_up, "col"), ("mlp_w_down", mlp_w_down, "row")]
    kinds = [k for _, _, k in weights]
    shard_shapes = [w.shape for _, w, _ in weights]
    shard_cols = [s[2] for s in shard_shapes]
    casted = [_cast_into_full("cast_" + nm, w, kind, k_idx) for nm, w, kind in weights]
    n_mla = 4
    full = list(_run_rider("gather_weights_mla", _gather_rider(casted[:n_mla], kinds[:n_mla], shard_cols[:n_mla])))
    gather_rest = _gather_rider(casted[n_mla:], kinds[n_mla:], shard_cols[n_mla:])

    def full2d(i):
        L, R, C = shard_shapes[i]
        return full[i].reshape((L, N_CHIPS * R, C) if kinds[i] == "row" else (L, R, N_CHIPS * C))

    def to_g5(i, g):
        L, R, C = shard_shapes[i]
        return g.reshape((L, N_CHIPS, 2, R // 2, C) if kinds[i] == "row" else (L, 1, 2, R // 2, N_CHIPS * C))

    w_in_p = jnp.pad(full2d(0)[0], ((0, 0), (0, lat_pad - lat_dim)))
    w_q_p = jnp.pad(full2d(1)[0].reshape(rank, H, d_qk), ((0, 0), (0, 0), (0, QK_PAD - d_qk))).reshape(rank, H * QK_PAD)
    w_ukv, w_o = full2d(2)[0], full2d(3)[0]
    HV = H * V_HEAD

    def lead(l, shape_fn):
        return lambda tm, tn, tk: shape_fn(l, tm, tn, tk)

    def mlp_fwd(tag, l, h):
        r, a2 = _mm("mlp_up_" + tag, h, w_up, "nn", S, F, D, [_sds((S, F), BF16)] * 2, epilogue=_relu2,
                    b_spec=lead(l, lambda l, tm, tn, tk: pl.BlockSpec((None, tk, tn), lambda i, j, k: (l, k, j))))
        (y,) = _mm("mlp_down_" + tag, a2, w_down, "nn", S, D, F, [_sds((S, D), F32)],
                   b_spec=lead(l, lambda l, tm, tn, tk: pl.BlockSpec((None, tk, tn), lambda i, j, k: (l, k, j))))
        return r, a2, y

    def mlp_bwd(tag, l, h, r, a2, dy, dw_up_prev, dw_down_prev):
        (da,) = _mm("mlp_down_dx_" + tag, dy, w_down, "nt", S, F, D, [_sds((S, F), BF16)], epilogue=_times_2r,
                    b_spec=lead(l, lambda l, tm, tn, tk: pl.BlockSpec((None, tn, tk), lambda i, j, k: (l, j, k))),
                    extras=[(r, lambda tm, tn, tk: pl.BlockSpec((tm, tn), lambda i, j, k: (i, j)))])
        (dw_down,) = _mm("mlp_down_dw_" + tag, a2, dy, "tn", F, D, S, [_sds((2, F, D), BF16)], alias=dw_down_prev,
                         out_specs=[lead(l, lambda l, tm, tn, tk: pl.BlockSpec((None, tm, tn), lambda i, j, k: (l, i, j)))])
        (dh,) = _mm("mlp_up_dx_" + tag, da, w_up, "nt", S, D, F, [_sds((S, D), F32)],
                    b_spec=lead(l, lambda l, tm, tn, tk: pl.BlockSpec((None, tn, tk), lambda i, j, k: (l, j, k))))
        (dw_up,) = _mm("mlp_up_dw_" + tag, h, da, "tn", D, F, S, [_sds((2, D, F), BF16)], alias=dw_up_prev,
                       out_specs=[lead(l, lambda l, tm, tn, tk: pl.BlockSpec((None, tm, tn), lambda i, j, k: (l, i, j)))])
        return dh, dw_up, dw_down

    x0 = x[0]
    sh1, sc1, gt1, sh2, sc2, gt2 = mods[0]
    (h1,) = _fwd_boundary("fwd_boundary_0", x0, None, None, None, ng[0][0], sc1, sh1)
    (lat,) = _mm("mla_in", h1, w_in_p, "nn", S, lat_pad, D, [_sds((S, lat_pad), F32)], tn=lat_pad)
    cq, ckv, kr = _latent_fwd(lat, mla_g_q, mla_g_kv, rope_tabs, rank)

    def rope_q(acc, cos_p, sin_lo, sin_hi):
        parts = []
        for hh in range(acc.shape[1] // QK_PAD):
            parts.append(acc[:, hh * QK_PAD:hh * QK_PAD + QK_NOPE])
            parts.append(_rope(acc[:, hh * QK_PAD + QK_NOPE:(hh + 1) * QK_PAD], cos_p, sin_lo, sin_hi))
        return (jnp.concatenate(parts, axis=1),)

    tab_extra = lambda tm, tn, tk: pl.BlockSpec((tm, LANES), lambda i, j, k: (i, 0))
    (q,) = _mm("mla_q", cq, w_q_p, "nn", S, H * QK_PAD, rank, [_sds((S, H * QK_PAD), BF16)], epilogue=rope_q,
               extras=[(t, tab_extra) for t in rope_tabs], tn=2 * QK_PAD)
    (kv,) = _mm("mla_kv", ckv, w_ukv, "nn", S, H * QK_PAD, rank, [_sds((S, H * QK_PAD), BF16)])
    o, lse, gathered_rest = _attn_fwd(q, kv, kr, H, scale, gather_rest)
    full += list(gathered_rest)
    w_cin, w_cout, w_up, w_down = full2d(4)[0], full2d(5)[0], full2d(6), full2d(7)
    (y1,) = _mm("mla_out", o, w_o, "nn", S, D, HV, [_sds((S, D), F32)])
    x1, h2 = _fwd_boundary("fwd_boundary_1", x0, y1, gt1, ng[0][1], ng[0][2], sc2, sh2)
    r2, a2, y2 = mlp_fwd("0", 0, h2)

    sh1b, sc1b, gt1b, sh2b, sc2b, gt2b = mods[1]
    x2, h3 = _fwd_boundary("fwd_boundary_2", x1, y2, gt2, ng[0][3], ng[1][0], sc1b, sh1b)
    nD = lambda tn: D // tn
    (proj3,) = _mm("conv_in", h3, w_cin, "nn", S, 3 * D, D, [_sds((3, S, D), F32)], tn=min(1024, D),
                   out_specs=[lambda tm, tn, tk: pl.BlockSpec((None, tm, tn), lambda i, j, k: (j // nD(tn), i, j % nD(tn)))])
    bz = _conv_fwd(proj3, convw_full)
    (y3,) = _mm("conv_out", bz, w_cout, "nn", S, D, D, [_sds((S, D), F32)])
    x3, h4 = _fwd_boundary("fwd_boundary_3", x2, y3, gt1b, ng[1][1], ng[1][2], sc2b, sh2b)
    r4, a4, y4 = mlp_fwd("1", 1, h4)

    dx4, dy4, sums_l, loss_acc = _loss_boundary("loss_boundary", x3, y4, gt2b, ng[1][3], loss_target[0])
    loss = lax.psum(loss_acc[0, 0], ("x", "y", "c"))

    dh4, dw_up, dw_down = mlp_bwd("1", 1, h4, r4, a4, dy4, None, None)
    dx3, dy3, sums_3 = _bwd_boundary("bwd_boundary_3", dx4, dh4, x3, y3, gt1b, ng[1][1], ng[1][2], sc2b)

    (dbz,) = _mm("conv_out_dx", dy3, w_cout, "nt", S, D, D, [_sds((S, D), F32)])
    (dw_cout,) = _mm("conv_out_dw", bz, dy3, "tn", D, D, S, [_sds((D, D), BF16)])
    dproj3, dconvw = _conv_bwd(dbz, proj3, convw_full)
    (dh3,) = _mm("conv_in_dx", dproj3, w_cin, "nt", S, D, 3 * D, [_sds((S, D), F32)], tk=min(1024, D),
                 a_spec=lambda tm, tn, tk: pl.BlockSpec((None, tm, tk), lambda i, j, k: (k // (D // tk), i, k % (D // tk))))
    (dw_cin,) = _mm("conv_in_dw", h3, dproj3, "tn", D, 3 * D, S, [_sds((D, 3 * D), BF16)], tn=min(1024, D),
                    b_spec=lambda tm, tn, tk: pl.BlockSpec((None, tk, tn), lambda i, j, k: (j // nD(tn), k, j % nD(tn))))
    dx2, dy2, sums_2 = _bwd_boundary("bwd_boundary_2", dx3, dh3, x2, y2, gt2, ng[0][3], ng[1][0], sc1b)

    dh2, dw_up, dw_down = mlp_bwd("0", 0, h2, r2, a2, dy2, dw_up, dw_down)
    dx1, dy1, sums_1 = _bwd_boundary("bwd_boundary_1", dx2, dh2, x1, y1, gt1, ng[0][1], ng[0][2], sc2)

    def pair_sums(tag, idx, dws):
        g5s = [to_g5(i, g) for i, g in zip(idx, dws)]
        ras = _pair_exchange("grad_pair_exchange_" + tag, g5s)
        return [_pair_sum("pair_sum_" + weights[i][0], g5, ra, c_idx) for i, g5, ra in zip(idx, g5s, ras)]

    rest = list(range(n_mla, len(weights)))
    ps_rest = pair_sums("rest", rest, [dw_cin[None], dw_cout[None], dw_up, dw_down])
    scatter_rest = _scatter_rider(ps_rest, kinds[n_mla:], shard_cols[n_mla:])

    (do,) = _mm("mla_out_dx", dy1, w_o, "nt", S, HV, D, [_sds((S, HV), BF16)])
    (dw_o,) = _mm("mla_out_dw", o, dy1, "tn", HV, D, S, [_sds((HV, D), BF16)])
    dq, dkv, dkr, rbs_rest = _attn_bwd(q, kv, kr, o, do, lse, rope_tabs, H, scale, scatter_rest)
    (dcq,) = _mm("mla_q_dx", dq, w_q_p, "nt", S, rank, H * QK_PAD, [_sds((S, rank), F32)])
    (dw_q_p,) = _mm("mla_q_dw", cq, dq, "tn", rank, H * QK_PAD, S, [_sds((rank, H * QK_PAD), BF16)])
    (dckv,) = _mm("mla_kv_dx", dkv, w_ukv, "nt", S, rank, H * QK_PAD, [_sds((S, rank), F32)])
    (dw_ukv,) = _mm("mla_kv_dw", ckv, dkv, "tn", rank, H * QK_PAD, S, [_sds((rank, H * QK_PAD), BF16)])
    dlat, sums_lat = _latent_bwd(lat, dcq, dckv, dkr, mla_g_q, mla_g_kv, rope_tabs, rank)
    (dh1,) = _mm("mla_in_dx", dlat, w_in_p, "nt", S, D, lat_pad, [_sds((S, D), F32)])
    (dw_in_p,) = _mm("mla_in_dw", h1, dlat, "tn", D, lat_pad, S, [_sds((D, lat_pad), BF16)], tn=lat_pad)
    grad_x, sums_0 = _bwd_boundary("bwd_boundary_0", dx1, dh1, x0, None, None, None, ng[0][0], sc1)

    dmod0 = [sums_0[0], sums_0[1], sums_1[3], sums_1[0], sums_1[1], sums_2[3]]
    dmod1 = [sums_2[0], sums_2[1], sums_3[3], sums_3[0], sums_3[1], sums_l[3]]
    dng0 = [sums_0[2], sums_1[4], sums_1[2], sums_2[4]]
    dng1 = [sums_2[2], sums_3[4], sums_3[2], sums_l[4]]
    small = _pack_rows(dmod0 + dmod1 + dng0 + dng1 + [sums_lat[0], sums_lat[1], dconvw], lane_mult=LANES)
    gathered, total = _small_allgather("gather_small_grads", small, with_sum=True)
    n_dm = 2 * n_mod * D
    dmod_all = gathered.reshape(8, -1)[:, :n_dm].reshape(8, 2, n_mod * D)
    total = total.reshape(-1)
    g_b_mod = total[:n_dm].reshape(2, n_mod * D)
    g_norm = lax.dynamic_slice(total[n_dm:n_dm + 8 * D].reshape(2, 4, D), (0, 0, chip * Dq), (2, 4, Dq))
    off = n_dm + 8 * D
    g_gq = total[off:off + rank].reshape(1, rank)
    g_gkv = total[off + rank:off + 2 * rank].reshape(1, rank)
    off += 2 * rank
    g_convw = lax.dynamic_slice(total[off:off + 3 * D].reshape(1, 3, D), (0, 0, chip * Dq), (1, 3, Dq))

    dw_mla = [dw_in_p[:, :lat_dim][None], dw_q_p.reshape(rank, H, QK_PAD)[:, :, :d_qk].reshape(1, rank, H * d_qk),
              dw_ukv[None], dw_o[None]]
    ps_mla = pair_sums("mla", list(range(n_mla)), dw_mla)
    rbs_mla = _run_rider("grad_chip_scatter_mla", _scatter_rider(ps_mla, kinds[:n_mla], shard_cols[:n_mla]))
    ps, rbs = ps_mla + ps_rest, list(rbs_mla) + list(rbs_rest)
    kc_idx = jnp.stack([chip, ci]).astype(jnp.int32)
    fs = [_chip_sum("chip_sum_" + nm, p, rb, kc_idx, kind) for (nm, _, kind), p, rb in zip(weights, ps, rbs)]
    finals = _pair_share(fs)
    orig = [mla_w_in, mla_w_uq, mla_w_ukv, mla_w_o, conv_w_in, conv_w_out, mlp_w_up, mlp_w_down]
    big_grads = [f.reshape(w.shape) for f, w in zip(finals, orig)]

    dmod_cols = jnp.transpose(lax.dynamic_slice(dmod_all.reshape(8, 2, N_CHIPS, ncol), (0, 0, chip, 0), (8, 2, 1, ncol))
                              .reshape(8, 2, ncol), (1, 0, 2))
    g_w_mod, d_w_mod, nm_w_mod, nv_w_mod = _adamw_mod(w_mod, cond_all.T, dmod_cols, m_w_mod, v_w_mod)

    names = ["b_mod", "norm_g", "mla_w_in", "mla_g_q", "mla_g_kv", "mla_w_uq", "mla_w_ukv", "mla_w_o",
             "conv_w_in", "conv_w", "conv_w_out", "mlp_w_up", "mlp_w_down"]
    ws = [b_mod, norm_g, mla_w_in, mla_g_q, mla_g_kv, mla_w_uq, mla_w_ukv, mla_w_o, conv_w_in, conv_w, conv_w_out,
          mlp_w_up, mlp_w_down]
    ms = [m_b_mod, m_norm_g, m_mla_w_in, m_mla_g_q, m_mla_g_kv, m_mla_w_uq, m_mla_w_ukv, m_mla_w_o, m_conv_w_in,
          m_conv_w, m_conv_w_out, m_mlp_w_up, m_mlp_w_down]
    vs = [v_b_mod, v_norm_g, v_mla_w_in, v_mla_g_q, v_mla_g_kv, v_mla_w_uq, v_mla_w_ukv, v_mla_w_o, v_conv_w_in,
          v_conv_w, v_conv_w_out, v_mlp_w_up, v_mlp_w_down]
    gs = [g_b_mod, g_norm, big_grads[0], g_gq, g_gkv, big_grads[1], big_grads[2], big_grads[3], big_grads[4],
          g_convw, big_grads[5], big_grads[6], big_grads[7]]
    grads, deltas, new_ms, new_vs = [g_w_mod], [d_w_mod], [nm_w_mod], [nv_w_mod]
    for nm, w, g, m, v in zip(names, ws, gs, ms, vs):
        d, nm_, nv_ = _adamw("adamw_" + nm, w, g, m, v)
        grads.append(g)
        deltas.append(d)
        new_ms.append(nm_)
        new_vs.append(nv_)
    return (loss, grad_x[None], *grads, *deltas, *new_ms, *new_vs)
```

```python
from typing import NamedTuple

import jax
import jax.numpy as jnp
from jax import lax
from jax.experimental import pallas as pl
from jax.experimental.pallas import tpu as pltpu

F32 = jnp.float32
BF16 = jnp.bfloat16
NORM_EPS = 1e-6
ROPE_THETA = 10000.0
QK_NOPE = 128
QK_ROPE = 64
V_HEAD = 128
LANES = 128
QK_PAD = QK_NOPE + LANES
ADAM_LR, ADAM_B1, ADAM_B2, ADAM_EPS, ADAM_WD, ADAM_STEP = 0.001, 0.9, 0.999, 1e-08, 0.01, 10
VMEM_LIMIT_BYTES = 48 * 1024 * 1024
N_CHIPS = 4
MESH_ID = pl.DeviceIdType.MESH
ANY = pl.BlockSpec(memory_space=pl.ANY)
NEG_INF = float("-inf")

DIMS_NN = (((1,), (0,)), ((), ()))
DIMS_NT = (((1,), (1,)), ((), ()))
DIMS_TN = (((0,), (0,)), ((), ()))


def _cparams(*sem):
    return pltpu.CompilerParams(dimension_semantics=sem, vmem_limit_bytes=VMEM_LIMIT_BYTES)


def _row_tile(rows, row_bytes, limit=2 * 1024 * 1024, mult=16):
    if rows * row_bytes <= limit or rows % mult:
        return rows
    best = mult
    t = mult
    while t <= rows:
        if rows % t == 0 and t * row_bytes <= limit:
            best = t
        t += mult
    return best


def _rms(v):
    return lax.rsqrt(jnp.mean(v * v, axis=-1, keepdims=True) + NORM_EPS)


class Rider(NamedTuple):
    operands: tuple
    out_shape: tuple
    aliases: dict
    sems: tuple
    start: object
    finish: object


NO_RIDER = Rider((), (), {}, (), None, None)


def _mm(name, a, b, mode, M, N, K, outs, *, a_spec=None, b_spec=None, out_specs=None, epilogue=None,
        extras=(), rider=NO_RIDER, tm=1024, tn=1024, tk=2048):
    tm, tn, tk = min(tm, M), min(tn, N), min(tk, K)
    assert M % tm == 0 and N % tn == 0 and K % tk == 0, (name, M, N, K)
    nk = K // tk
    if a_spec is None:
        a_spec = {"nn": pl.BlockSpec((tm, tk), lambda i, j, k: (i, k)),
                  "nt": pl.BlockSpec((tm, tk), lambda i, j, k: (i, k)),
                  "tn": pl.BlockSpec((tk, tm), lambda i, j, k: (k, i))}[mode]
    else:
        a_spec = a_spec(tm, tn, tk)
    if b_spec is None:
        b_spec = {"nn": pl.BlockSpec((tk, tn), lambda i, j, k: (k, j)),
                  "nt": pl.BlockSpec((tn, tk), lambda i, j, k: (j, k)),
                  "tn": pl.BlockSpec((tk, tn), lambda i, j, k: (k, j))}[mode]
    else:
        b_spec = b_spec(tm, tn, tk)
    if out_specs is None:
        out_specs = [pl.BlockSpec((tm, tn), lambda i, j, k: (i, j)) for _ in outs]
    else:
        out_specs = [s(tm, tn, tk) for s in out_specs]
    dims = {"nn": DIMS_NN, "nt": DIMS_NT, "tn": DIMS_TN}[mode]
    ne, no = len(extras), len(outs)
    n_ri, n_ro = len(rider.operands), len(rider.out_shape)
    grid = (M // tm, N // tn, nk)

    def body(*refs):
        a_ref, b_ref = refs[0], refs[1]
        ex = refs[2:2 + ne]
        r_in = refs[2 + ne:2 + ne + n_ri]
        o = refs[2 + ne + n_ri:2 + ne + n_ri + no]
        r_out = refs[2 + ne + n_ri + no:2 + ne + n_ri + no + n_ro]
        scratch = refs[2 + ne + n_ri + no + n_ro:]
        r_sems = scratch[1:] if nk > 1 else scratch
        ii, jj, kk = pl.program_id(0), pl.program_id(1), pl.program_id(2)

        if rider.start is not None:
            @pl.when((ii == 0) & (jj == 0) & (kk == 0))
            def _():
                rider.start(r_in, r_out, r_sems)

        part = lax.dot_general(a_ref[...].astype(BF16), b_ref[...].astype(BF16), dims,
                               preferred_element_type=F32)

        def finish(total):
            vals = epilogue(total, *[e[...] for e in ex]) if epilogue is not None else (total,)
            for r, v in zip(o, vals):
                r[...] = v.astype(r.dtype)

        if nk == 1:
            finish(part)
        else:
            acc = scratch[0]

            @pl.when(kk == 0)
            def _():
                acc[...] = part

            @pl.when(kk > 0)
            def _():
                acc[...] += part

            @pl.when(kk == nk - 1)
            def _():
                finish(acc[...])

        if rider.finish is not None:
            @pl.when((ii == grid[0] - 1) & (jj == grid[1] - 1) & (kk == nk - 1))
            def _():
                rider.finish(r_in, r_out, r_sems)

    operands = [a, b] + [e[0] for e in extras] + list(rider.operands)
    in_specs = [a_spec, b_spec] + [e[1](tm, tn, tk) for e in extras] + [ANY] * n_ri
    hosted = rider.start is not None
    res = pl.pallas_call(
        body, name=name, grid=grid,
        in_specs=in_specs, out_specs=out_specs + [ANY] * n_ro, out_shape=list(outs) + list(rider.out_shape),
        scratch_shapes=([pltpu.VMEM((tm, tn), F32)] if nk > 1 else []) + list(rider.sems),
        input_output_aliases={2 + ne + i: no + r for i, r in rider.aliases.items()},
        compiler_params=_cparams(*(("arbitrary",) * 3 if hosted else ("parallel", "parallel", "arbitrary"))),
    )(*operands)
    return (res[:no], res[no:]) if hosted else res


def _sds(shape, dtype):
    return jax.ShapeDtypeStruct(tuple(shape), dtype)


def _rope(t, cos_p, sin_lo, sin_hi):
    return t * cos_p + pltpu.roll(t, LANES - QK_ROPE // 2, 1) * sin_lo + pltpu.roll(t, QK_ROPE // 2, 1) * sin_hi


def _rope_t(d, cos_p, sin_lo, sin_hi):
    return d * cos_p + pltpu.roll(d * sin_lo, QK_ROPE // 2, 1) + pltpu.roll(d * sin_hi, LANES - QK_ROPE // 2, 1)


def _vec_spec(d):
    return pl.BlockSpec((1, d), lambda i: (0, 0))


def _fwd_boundary(name, x_prev, y, gate, ng_post, ng_pre, sc, sh):
    S, D = x_prev.shape
    ts = min(256, S)
    has_y = y is not None
    row = pl.BlockSpec((ts, D), lambda i: (i, 0))

    def body(*refs):
        if has_y:
            x_ref, y_ref, g_ref, ngp_ref, ngn_ref, sc_ref, sh_ref, xo_ref, h_ref = refs
            yv = y_ref[...]
            xn = x_ref[...] + g_ref[...] * (yv * _rms(yv) * ngp_ref[...])
            xo_ref[...] = xn
        else:
            x_ref, ngn_ref, sc_ref, sh_ref, h_ref = refs
            xn = x_ref[...]
        hn = xn * _rms(xn) * ngn_ref[...]
        h_ref[...] = (hn * (1.0 + sc_ref[...]) + sh_ref[...]).astype(BF16)

    vec = _vec_spec(D)
    if has_y:
        operands = (x_prev, y, gate, ng_post, ng_pre, sc, sh)
        in_specs = [row, row, vec, vec, vec, vec, vec]
        out_shape = [_sds((S, D), F32), _sds((S, D), BF16)]
        out_specs = [row, row]
    else:
        operands = (x_prev, ng_pre, sc, sh)
        in_specs = [row, vec, vec, vec]
        out_shape = [_sds((S, D), BF16)]
        out_specs = [row]
    return pl.pallas_call(body, name=name, grid=(S // ts,), in_specs=in_specs, out_specs=out_specs,
                          out_shape=out_shape, compiler_params=_cparams("parallel"))(*operands)


def _acc_rows(sums_ref, rows):
    for r, v in rows:
        sums_ref[r:r + 1, :] += jnp.sum(v, axis=0, keepdims=True)


def _post_norm_bwd(dxt, yv, gate, ng_post, sums_ref, dy_ref):
    r1 = _rms(yv)
    yhat = yv * r1
    dn = dxt * gate
    u = dn * ng_post
    dy = r1 * (u - yhat * jnp.mean(u * yhat, axis=-1, keepdims=True))
    dy_ref[...] = dy.astype(dy_ref.dtype)
    _acc_rows(sums_ref, [(3, dxt * (yhat * ng_post)), (4, dn * yhat)])


def _loss_boundary(name, x_prev, y, gate, ng_post, target):
    S, D = x_prev.shape
    ts = min(256, S)
    row = pl.BlockSpec((ts, D), lambda i: (i, 0))
    vec = _vec_spec(D)

    def body(x_ref, y_ref, g_ref, ngp_ref, t_ref, dx_ref, dy_ref, sums_ref, loss_ref):
        @pl.when(pl.program_id(0) == 0)
        def _():
            sums_ref[...] = jnp.zeros_like(sums_ref)
            loss_ref[...] = jnp.zeros_like(loss_ref)

        yv = y_ref[...]
        xf = x_ref[...] + g_ref[...] * (yv * _rms(yv) * ngp_ref[...])
        err = xf - t_ref[...]
        loss_ref[...] += 0.5 * jnp.sum(jnp.mean(err * err, axis=-1, keepdims=True))
        dxt = err / D
        dx_ref[...] = dxt
        _post_norm_bwd(dxt, yv, g_ref[...], ngp_ref[...], sums_ref, dy_ref)

    return pl.pallas_call(
        body, name=name, grid=(S // ts,),
        in_specs=[row, row, vec, vec, row],
        out_specs=[row, row, pl.BlockSpec((8, D), lambda i: (0, 0)), pl.BlockSpec((8, LANES), lambda i: (0, 0))],
        out_shape=[_sds((S, D), F32), _sds((S, D), BF16), _sds((8, D), F32), _sds((8, LANES), F32)],
        compiler_params=_cparams("arbitrary"))(x_prev, y, gate, ng_post, target)


def _bwd_boundary(name, dx_new, dh, x_new, y, gate, ng_post, ng_pre, sc):
    S, D = x_new.shape
    ts = min(256, S)
    has_y = y is not None
    row = pl.BlockSpec((ts, D), lambda i: (i, 0))
    vec = _vec_spec(D)

    def body(*refs):
        if has_y:
            dxn_ref, dh_ref, x_ref, y_ref, g_ref, ngp_ref, ngn_ref, sc_ref, dxo_ref, dy_ref, sums_ref = refs
        else:
            dxn_ref, dh_ref, x_ref, ngn_ref, sc_ref, dxo_ref, sums_ref = refs

        @pl.when(pl.program_id(0) == 0)
        def _():
            sums_ref[...] = jnp.zeros_like(sums_ref)

        xv = x_ref[...]
        dhv = dh_ref[...]
        ngn = ngn_ref[...]
        r2 = _rms(xv)
        xhat = xv * r2
        dn_pre = dhv * (1.0 + sc_ref[...])
        u2 = dn_pre * ngn
        dxt = dxn_ref[...] + r2 * (u2 - xhat * jnp.mean(u2 * xhat, axis=-1, keepdims=True))
        dxo_ref[...] = dxt
        _acc_rows(sums_ref, [(0, dhv), (1, dhv * (xhat * ngn)), (2, dn_pre * xhat)])
        if has_y:
            _post_norm_bwd(dxt, y_ref[...], g_ref[...], ngp_ref[...], sums_ref, dy_ref)

    sums_spec = pl.BlockSpec((8, D), lambda i: (0, 0))
    if has_y:
        operands = (dx_new, dh, x_new, y, gate, ng_post, ng_pre, sc)
        in_specs = [row, row, row, row, vec, vec, vec, vec]
        out_shape = [_sds((S, D), F32), _sds((S, D), BF16), _sds((8, D), F32)]
        out_specs = [row, row, sums_spec]
    else:
        operands = (dx_new, dh, x_new, ng_pre, sc)
        in_specs = [row, row, row, vec, vec]
        out_shape = [_sds((S, D), F32), _sds((8, D), F32)]
        out_specs = [row, sums_spec]
    return pl.pallas_call(body, name=name, grid=(S // ts,), in_specs=in_specs, out_specs=out_specs,
                          out_shape=out_shape, compiler_params=_cparams("arbitrary"))(*operands)


def _latent_fwd(lat, g_q, g_kv, rope_tabs, rank):
    S, W = lat.shape
    ts = min(256, S)
    tab = pl.BlockSpec((ts, LANES), lambda i: (i, 0))

    def body(lat_ref, gq_ref, gkv_ref, cos_ref, slo_ref, shi_ref, cq_ref, ckv_ref, kr_ref):
        lq = lat_ref[:, 0:rank]
        lkv = lat_ref[:, rank:2 * rank]
        cq_ref[...] = (lq * _rms(lq) * gq_ref[...]).astype(BF16)
        ckv_ref[...] = (lkv * _rms(lkv) * gkv_ref[...]).astype(BF16)
        kr_ref[...] = _rope(lat_ref[:, 2 * rank:W], cos_ref[...], slo_ref[...], shi_ref[...]).astype(BF16)

    return pl.pallas_call(
        body, name="mla_latent_fwd", grid=(S // ts,),
        in_specs=[pl.BlockSpec((ts, W), lambda i: (i, 0)), _vec_spec(rank), _vec_spec(rank), tab, tab, tab],
        out_specs=[pl.BlockSpec((ts, rank), lambda i: (i, 0)), pl.BlockSpec((ts, rank), lambda i: (i, 0)), tab],
        out_shape=[_sds((S, rank), BF16), _sds((S, rank), BF16), _sds((S, LANES), BF16)],
        compiler_params=_cparams("parallel"))(lat, g_q, g_kv, *rope_tabs)


def _latent_bwd(lat, dcq, dckv, dkr, g_q, g_kv, rope_tabs, rank):
    S, W = lat.shape
    ts = min(256, S)
    tab = pl.BlockSpec((ts, LANES), lambda i: (i, 0))
    half = pl.BlockSpec((ts, rank), lambda i: (i, 0))

    def body(lat_ref, dcq_ref, dckv_ref, dkr_ref, gq_ref, gkv_ref, cos_ref, slo_ref, shi_ref, dlat_ref, sums_ref):
        @pl.when(pl.program_id(0) == 0)
        def _():
            sums_ref[...] = jnp.zeros_like(sums_ref)

        def norm_bwd(v, dn, g, r):
            rr = _rms(v)
            vhat = v * rr
            u = dn * g
            sums_ref[r:r + 1, :] += jnp.sum(dn * vhat, axis=0, keepdims=True)
            return rr * (u - vhat * jnp.mean(u * vhat, axis=-1, keepdims=True))

        dlat_ref[:, 0:rank] = norm_bwd(lat_ref[:, 0:rank], dcq_ref[...], gq_ref[...], 0).astype(BF16)
        dlat_ref[:, rank:2 * rank] = norm_bwd(lat_ref[:, rank:2 * rank], dckv_ref[...], gkv_ref[...], 1).astype(BF16)
        dlat_ref[:, 2 * rank:W] = _rope_t(dkr_ref[...], cos_ref[...], slo_ref[...], shi_ref[...]).astype(BF16)

    return pl.pallas_call(
        body, name="mla_latent_bwd", grid=(S // ts,),
        in_specs=[pl.BlockSpec((ts, W), lambda i: (i, 0)), half, half, tab, _vec_spec(rank), _vec_spec(rank),
                  tab, tab, tab],
        out_specs=[pl.BlockSpec((ts, W), lambda i: (i, 0)), pl.BlockSpec((8, rank), lambda i: (0, 0))],
        out_shape=[_sds((S, W), BF16), _sds((8, rank), F32)],
        compiler_params=_cparams("arbitrary"))(lat, dcq, dckv, dkr, g_q, g_kv, *rope_tabs)


def _attn_tiles(S):
    t = min(512, S)
    return t, S // t


def _causal_mask(t):
    return lax.broadcasted_iota(jnp.int32, (t, t), 1) <= lax.broadcasted_iota(jnp.int32, (t, t), 0)


def _attn_fwd(q, kv, kr, heads, scale, rider=NO_RIDER):
    S = q.shape[0]
    t, nb = _attn_tiles(S)
    G = 2 if heads % 2 == 0 else 1
    n_ri, n_ro = len(rider.operands), len(rider.out_shape)

    def body(*refs):
        q_ref, kv_ref, kr_ref = refs[:3]
        r_in = refs[3:3 + n_ri]
        o_ref, lse_ref = refs[3 + n_ri:5 + n_ri]
        r_out = refs[5 + n_ri:5 + n_ri + n_ro]
        m_scr, acc_scr = refs[5 + n_ri + n_ro:7 + n_ri + n_ro]
        r_sems = refs[7 + n_ri + n_ro:]
        h, qi, ki = pl.program_id(0), pl.program_id(1), pl.program_id(2)

        if rider.start is not None:
            @pl.when((h == 0) & (qi == 0) & (ki == 0))
            def _():
                rider.start(r_in, r_out, r_sems)

        @pl.when(ki == 0)
        def _():
            m_scr[...] = jnp.full_like(m_scr, NEG_INF)
            acc_scr[...] = jnp.zeros_like(acc_scr)

        def step(diagonal):
            ones = jnp.ones((t, LANES), BF16)
            for g in range(G):
                kcat = jnp.concatenate([kv_ref[:, g * QK_PAD:g * QK_PAD + QK_NOPE], kr_ref[...]], axis=1)
                vext = jnp.concatenate([kv_ref[:, g * QK_PAD + QK_NOPE:(g + 1) * QK_PAD], ones], axis=1)
                s = lax.dot_general(q_ref[:, g * QK_PAD:(g + 1) * QK_PAD], kcat, DIMS_NT,
                                    preferred_element_type=F32) * scale
                if diagonal:
                    s = jnp.where(_causal_mask(t), s, NEG_INF)
                m_prev = m_scr[g]
                m_new = jnp.maximum(m_prev, jnp.max(s, axis=-1, keepdims=True))
                alpha = jnp.exp(m_prev - m_new)
                p = jnp.exp(s - jnp.tile(m_new, (1, t // LANES)))
                acc_scr[g] = jnp.tile(alpha, (1, 2)) * acc_scr[g] + lax.dot_general(
                    p.astype(BF16), vext, DIMS_NN, preferred_element_type=F32)
                m_scr[g] = m_new

        @pl.when(ki < qi)
        def _():
            step(False)

        @pl.when(ki == qi)
        def _():
            step(True)

        @pl.when(ki == nb - 1)
        def _():
            for g in range(G):
                acc = acc_scr[g]
                o_ref[:, g * V_HEAD:(g + 1) * V_HEAD] = (acc[:, 0:V_HEAD] / acc[:, V_HEAD:2 * V_HEAD]).astype(BF16)
                lse_ref[g] = m_scr[g] + jnp.log(acc[:, V_HEAD:2 * V_HEAD])

        if rider.finish is not None:
            @pl.when((h == heads // G - 1) & (qi == nb - 1) & (ki == nb - 1))
            def _():
                rider.finish(r_in, r_out, r_sems)

    res = pl.pallas_call(
        body, name="mla_attn_fwd", grid=(heads // G, nb, nb),
        in_specs=[pl.BlockSpec((t, G * QK_PAD), lambda h, qi, ki: (qi, h)),
                  pl.BlockSpec((t, G * QK_PAD), lambda h, qi, ki: (jnp.minimum(ki, qi), h)),
                  pl.BlockSpec((t, LANES), lambda h, qi, ki: (jnp.minimum(ki, qi), 0))] + [ANY] * n_ri,
        out_specs=[pl.BlockSpec((t, G * V_HEAD), lambda h, qi, ki: (qi, h)),
                   pl.BlockSpec((G, t, LANES), lambda h, qi, ki: (h, qi, 0))] + [ANY] * n_ro,
        out_shape=[_sds((S, heads * V_HEAD), BF16), _sds((heads, S, LANES), F32)] + list(rider.out_shape),
        scratch_shapes=[pltpu.VMEM((G, t, LANES), F32), pltpu.VMEM((G, t, 2 * V_HEAD), F32)] + list(rider.sems),
        input_output_aliases={3 + i: 2 + o for i, o in rider.aliases.items()},
        compiler_params=_cparams("arbitrary", "arbitrary", "arbitrary"))(q, kv, kr, *rider.operands)
    return res[0], res[1], res[2:]


def _attn_delta(o, do, heads):
    S = o.shape[0]
    t, nb = _attn_tiles(S)

    def body(o_ref, do_ref, out_ref):
        d = jnp.sum(do_ref[...].astype(F32) * o_ref[...].astype(F32), axis=-1, keepdims=True)
        out_ref[...] = jnp.broadcast_to(d, (t, LANES))

    blk = pl.BlockSpec((t, V_HEAD), lambda h, i: (i, h))
    return pl.pallas_call(
        body, name="mla_attn_delta", grid=(heads, nb), in_specs=[blk, blk],
        out_specs=pl.BlockSpec((None, t, LANES), lambda h, i: (h, i, 0)),
        out_shape=_sds((heads, S, LANES), F32), compiler_params=_cparams("parallel", "parallel"))(o, do)


def _attn_bwd(q, kv, kr, delta, do, lse, rope_tabs, heads, scale, rider=NO_RIDER):
    S = q.shape[0]
    t, nb = _attn_tiles(S)
    n_ri, n_ro = len(rider.operands), len(rider.out_shape)
    rep = t // LANES

    def body(*refs):
        q_ref, kv_ref, kr_ref, delta_ref, do_ref, lse_ref, cos_ref, slo_ref, shi_ref = refs[:9]
        r_in = refs[9:9 + n_ri]
        dq_ref, dkv_ref, dkr_ref = refs[9 + n_ri:12 + n_ri]
        r_out = refs[12 + n_ri:12 + n_ri + n_ro]
        dq_scr, dk_scr, dv_scr, dkr_scr = refs[12 + n_ri + n_ro:16 + n_ri + n_ro]
        r_sems = refs[16 + n_ri + n_ro:]
        h, ki, qi = pl.program_id(0), pl.program_id(1), pl.program_id(2)
        q_rows = pl.ds(pl.multiple_of(qi * t, t), t)
        k_rows = pl.ds(pl.multiple_of(ki * t, t), t)

        if rider.start is not None:
            @pl.when((h == 0) & (ki == 0) & (qi == 0))
            def _():
                rider.start(r_in, r_out, r_sems)

        @pl.when((ki == 0) & (qi == 0))
        def _():
            dq_scr[...] = jnp.zeros_like(dq_scr)

        @pl.when((h == 0) & (ki == 0) & (qi == 0))
        def _():
            dkr_scr[...] = jnp.zeros_like(dkr_scr)

        @pl.when(qi == 0)
        def _():
            dk_scr[...] = jnp.zeros_like(dk_scr)
            dv_scr[...] = jnp.zeros_like(dv_scr)

        def step(diagonal):
            qv = q_ref[...]
            kcat = jnp.concatenate([kv_ref[:, 0:QK_NOPE], kr_ref[...]], axis=1)
            s = lax.dot_general(qv, kcat, DIMS_NT, preferred_element_type=F32) * scale
            p = jnp.exp(s - jnp.tile(lse_ref[...], (1, rep)))
            if diagonal:
                p = jnp.where(_causal_mask(t), p, 0.0)
            dov = do_ref[...]
            dv_scr[...] += lax.dot_general(p.astype(BF16), dov, DIMS_TN, preferred_element_type=F32)
            dp = lax.dot_general(dov, kv_ref[:, QK_NOPE:QK_NOPE + V_HEAD], DIMS_NT, preferred_element_type=F32)
            ds = (p * (dp - jnp.tile(delta_ref[...], (1, rep))) * scale).astype(BF16)
            dk_scr[...] += lax.dot_general(ds, qv, DIMS_TN, preferred_element_type=F32)
            dq_scr[q_rows, :] += lax.dot_general(ds, kcat, DIMS_NN, preferred_element_type=F32)

        @pl.when(qi > ki)
        def _():
            step(False)

        @pl.when(qi == ki)
        def _():
            step(True)

        @pl.when(qi == nb - 1)
        def _():
            dkv_ref[...] = jnp.concatenate([dk_scr[:, 0:QK_NOPE], dv_scr[...]], axis=1).astype(BF16)
            dkr_scr[k_rows, :] += dk_scr[:, QK_NOPE:QK_PAD]

        @pl.when(ki == nb - 1)
        def _():
            dqv = dq_scr[q_rows, :]
            dq_ref[q_rows, :] = jnp.concatenate(
                [dqv[:, 0:QK_NOPE], _rope_t(dqv[:, QK_NOPE:QK_PAD], cos_ref[...], slo_ref[...], shi_ref[...])],
                axis=1).astype(BF16)

        @pl.when((h == heads - 1) & (ki == nb - 1) & (qi == nb - 1))
        def _():
            dkr_ref[...] = dkr_scr[...]
            if rider.finish is not None:
                rider.finish(r_in, r_out, r_sems)

    qmap = lambda h, ki, qi: (jnp.maximum(qi, ki), h)
    stat = pl.BlockSpec((None, t, LANES), lambda h, ki, qi: (h, jnp.maximum(qi, ki), 0))
    tab = pl.BlockSpec((t, LANES), lambda h, ki, qi: (qi, 0))
    res = pl.pallas_call(
        body, name="mla_attn_bwd", grid=(heads, nb, nb),
        in_specs=[pl.BlockSpec((t, QK_PAD), qmap),
                  pl.BlockSpec((t, QK_PAD), lambda h, ki, qi: (ki, h)),
                  pl.BlockSpec((t, LANES), lambda h, ki, qi: (ki, 0)),
                  stat,
                  pl.BlockSpec((t, V_HEAD), qmap),
                  stat,
                  tab, tab, tab] + [ANY] * n_ri,
        out_specs=[pl.BlockSpec((S, QK_PAD), lambda h, ki, qi: (0, h)),
                   pl.BlockSpec((t, QK_PAD), lambda h, ki, qi: (ki, h)),
                   pl.BlockSpec((S, LANES), lambda h, ki, qi: (0, 0))] + [ANY] * n_ro,
        out_shape=[_sds((S, heads * QK_PAD), BF16), _sds((S, heads * QK_PAD), BF16), _sds((S, LANES), F32)]
        + list(rider.out_shape),
        scratch_shapes=[pltpu.VMEM((S, QK_PAD), F32), pltpu.VMEM((t, QK_PAD), F32), pltpu.VMEM((t, V_HEAD), F32),
                        pltpu.VMEM((S, LANES), F32)] + list(rider.sems),
        input_output_aliases={9 + i: 3 + o for i, o in rider.aliases.items()},
        compiler_params=_cparams("arbitrary", "arbitrary", "arbitrary"))(q, kv, kr, delta, do, lse, *rope_tabs, *rider.operands)
    return res[0], res[1], res[2], res[3:]


def _shift_down(z, n, rows):
    return jnp.where(rows >= n, pltpu.roll(z, n, 0), 0.0)


def _shift_up(z, n, rows, S):
    return jnp.where(rows < S - n, pltpu.roll(z, S - n, 0), 0.0)


def _conv_specs(S, tc):
    strip = lambda p: pl.BlockSpec((None, S, tc), lambda j: (p, 0, j))
    return strip(0), strip(1), strip(2), pl.BlockSpec((3, tc), lambda j: (0, j))


def _conv_fwd(proj3, w):
    _, S, D = proj3.shape
    tc = LANES

    def body(b_ref, c_ref, u_ref, w_ref, out_ref):
        z = c_ref[...] * u_ref[...]
        rows = lax.broadcasted_iota(jnp.int32, (S, tc), 0)
        zc = w_ref[0:1, :] * _shift_down(z, 2, rows) + w_ref[1:2, :] * _shift_down(z, 1, rows) + w_ref[2:3, :] * z
        out_ref[...] = (b_ref[...] * zc).astype(BF16)

    return pl.pallas_call(
        body, name="conv_fwd", grid=(D // tc,), in_specs=list(_conv_specs(S, tc)),
        out_specs=pl.BlockSpec((S, tc), lambda j: (0, j)), out_shape=_sds((S, D), BF16),
        compiler_params=_cparams("parallel"))(proj3, proj3, proj3, w)


def _conv_bwd(dbz, proj3, w):
    _, S, D = proj3.shape
    tc = LANES

    def body(d_ref, b_ref, c_ref, u_ref, w_ref, dp_ref, dw_ref):
        cv, uv, dv = c_ref[...], u_ref[...], d_ref[...]
        z = cv * uv
        rows = lax.broadcasted_iota(jnp.int32, (S, tc), 0)
        z1, z2 = _shift_down(z, 1, rows), _shift_down(z, 2, rows)
        zc = w_ref[0:1, :] * z2 + w_ref[1:2, :] * z1 + w_ref[2:3, :] * z
        dp_ref[0] = (dv * zc).astype(BF16)
        dzc = dv * b_ref[...]
        dz = w_ref[2:3, :] * dzc + w_ref[1:2, :] * _shift_up(dzc, 1, rows, S) + w_ref[0:1, :] * _shift_up(dzc, 2, rows, S)
        dp_ref[1] = (dz * uv).astype(BF16)
        dp_ref[2] = (dz * cv).astype(BF16)
        dw_ref[0:1, :] = jnp.sum(dzc * z2, axis=0, keepdims=True)
        dw_ref[1:2, :] = jnp.sum(dzc * z1, axis=0, keepdims=True)
        dw_ref[2:3, :] = jnp.sum(dzc * z, axis=0, keepdims=True)

    sb, sc_, su, sw = _conv_specs(S, tc)
    return pl.pallas_call(
        body, name="conv_bwd", grid=(D // tc,),
        in_specs=[pl.BlockSpec((S, tc), lambda j: (0, j)), sb, sc_, su, sw],
        out_specs=[pl.BlockSpec((3, S, tc), lambda j: (0, 0, j)), pl.BlockSpec((3, tc), lambda j: (0, j))],
        out_shape=[_sds((3, S, D), BF16), _sds((3, D), F32)],
        compiler_params=_cparams("parallel"))(dbz, proj3, proj3, proj3, w)


def _silu(c_all):
    def body(c_ref, o_ref):
        cv = c_ref[...]
        o_ref[...] = cv * (1.0 / (1.0 + jnp.exp(-cv)))

    vm = pl.BlockSpec(memory_space=pltpu.VMEM)
    return pl.pallas_call(body, name="cond_silu", in_specs=[vm], out_specs=vm, out_shape=_sds(c_all.shape, F32))(c_all)


def _mod_fwd(cond, w_mod, b_cols):
    L, D, ncol = w_mod.shape
    B = cond.shape[0]
    tk, tn = min(512, D), min(1024, ncol)
    nk = D // tk

    def body(c_ref, w_ref, b_ref, out_ref, acc):
        kk = pl.program_id(2)
        part = lax.dot_general(c_ref[...].astype(BF16), w_ref[...].astype(BF16), DIMS_NN, preferred_element_type=F32)

        @pl.when(kk == 0)
        def _():
            acc[...] = part

        @pl.when(kk > 0)
        def _():
            acc[...] += part

        @pl.when(kk == nk - 1)
        def _():
            out_ref[...] = acc[...] + b_ref[...]

    return pl.pallas_call(
        body, name="mod_fwd", grid=(L, ncol // tn, nk),
        in_specs=[pl.BlockSpec((B, tk), lambda l, j, k: (0, k)),
                  pl.BlockSpec((None, tk, tn), lambda l, j, k: (l, k, j)),
                  pl.BlockSpec((None, 1, tn), lambda l, j, k: (l, 0, j))],
        out_specs=pl.BlockSpec((None, B, tn), lambda l, j, k: (l, 0, j)),
        out_shape=_sds((L, B, ncol), F32),
        scratch_shapes=[pltpu.VMEM((B, tn), F32)],
        compiler_params=_cparams("parallel", "parallel", "arbitrary"))(cond, w_mod, b_cols)


def _adamw_math(w, g, m, v):
    m = ADAM_B1 * m + (1.0 - ADAM_B1) * g
    v = ADAM_B2 * v + (1.0 - ADAM_B2) * (g * g)
    m_hat = m / (1.0 - ADAM_B1 ** ADAM_STEP)
    v_hat = v / (1.0 - ADAM_B2 ** ADAM_STEP)
    delta = -ADAM_LR * (m_hat / (jnp.sqrt(v_hat) + ADAM_EPS) + ADAM_WD * w)
    return delta, m, v


def _adamw(name, w, g, m, v):
    shape = w.shape
    cols = shape[-1] if w.ndim <= 3 else shape[-2] * shape[-1]
    rows = w.size // cols
    w2, g2, m2, v2 = (t.reshape(rows, cols) for t in (w, g, m, v))
    tr = _row_tile(rows, cols * 4, limit=1024 * 1024, mult=8)
    spec = pl.BlockSpec((tr, cols), lambda i: (i, 0))

    def body(w_ref, g_ref, m_ref, v_ref, d_ref, nm_ref, nv_ref):
        d, nm, nv = _adamw_math(w_ref[...], g_ref[...], m_ref[...], v_ref[...])
        d_ref[...] = d
        nm_ref[...] = nm
        nv_ref[...] = nv

    outs = pl.pallas_call(body, name=name, grid=(rows // tr,), in_specs=[spec] * 4, out_specs=[spec] * 3,
                          out_shape=[_sds((rows, cols), F32)] * 3, compiler_params=_cparams("parallel"))(w2, g2, m2, v2)
    return tuple(t.reshape(shape) for t in outs)


def _adamw_mod(w, cond_t, dmod_cols, m, v):
    L, D, ncol = w.shape
    B = cond_t.shape[1]
    tr, tc = min(256, D), min(1024, ncol)
    blk = pl.BlockSpec((None, tr, tc), lambda l, i, j: (l, i, j))

    def body(w_ref, ct_ref, dm_ref, m_ref, v_ref, g_ref, d_ref, nm_ref, nv_ref):
        g = lax.dot_general(ct_ref[...], dm_ref[...], DIMS_NN, precision=lax.Precision.HIGHEST,
                            preferred_element_type=F32)
        d, nm, nv = _adamw_math(w_ref[...], g, m_ref[...], v_ref[...])
        g_ref[...] = g
        d_ref[...] = d
        nm_ref[...] = nm
        nv_ref[...] = nv

    return pl.pallas_call(
        body, name="adamw_w_mod", grid=(L, D // tr, ncol // tc),
        in_specs=[blk, pl.BlockSpec((tr, B), lambda l, i, j: (i, 0)),
                  pl.BlockSpec((None, B, tc), lambda l, i, j: (l, 0, j)), blk, blk],
        out_specs=[blk] * 4, out_shape=[_sds((L, D, ncol), F32)] * 4,
        compiler_params=_cparams("parallel", "parallel", "parallel"))(w, cond_t, dmod_cols, m, v)


def _cast_into_full(name, w, kind, k_idx):
    L, R, C = w.shape
    Rh = R // 2
    tr = _row_tile(Rh, C * 4)
    if kind == "row":
        out_shape = (L, N_CHIPS, 2, Rh, C)
        out_spec = pl.BlockSpec((None, None, None, tr, C), lambda l, h, i, k_ref: (l, k_ref[0], h, i, 0))
    else:
        out_shape = (L, 2, Rh, N_CHIPS * C)
        out_spec = pl.BlockSpec((None, None, tr, C), lambda l, h, i, k_ref: (l, h, i, k_ref[0]))

    def body(k_ref, w_ref, o_ref):
        o_ref[...] = w_ref[...].astype(BF16)

    return pl.pallas_call(
        body, name=name,
        grid_spec=pltpu.PrefetchScalarGridSpec(
            num_scalar_prefetch=1, grid=(L, 2, Rh // tr),
            in_specs=[pl.BlockSpec((None, None, tr, C), lambda l, h, i, k_ref: (l, h, i, 0))],
            out_specs=out_spec),
        out_shape=_sds(out_shape, BF16),
        compiler_params=_cparams("parallel", "parallel", "parallel"))(k_idx, w.reshape(L, 2, Rh, C))


def _pair_sum(name, g5, ra, c_idx):
    L, A, _, Rh, Cc = g5.shape
    tr = _row_tile(Rh, Cc * 4)

    def body(c_ref, g_ref, r_ref, o_ref):
        o_ref[...] = (g_ref[...].astype(F32) + r_ref[...].astype(F32)).astype(BF16)

    blk = pl.BlockSpec((None, None, tr, Cc), lambda l, a, i, c_ref: (l, a, i, 0))
    return pl.pallas_call(
        body, name=name,
        grid_spec=pltpu.PrefetchScalarGridSpec(
            num_scalar_prefetch=1, grid=(L, A, Rh // tr),
            in_specs=[pl.BlockSpec((None, None, None, tr, Cc), lambda l, a, i, c_ref: (l, a, c_ref[0], i, 0)), blk],
            out_specs=blk),
        out_shape=_sds((L, A, Rh, Cc), BF16),
        compiler_params=_cparams("parallel", "parallel", "parallel"))(c_idx, g5, ra)


def _chip_sum(name, p, rb, kc_idx, kind, layer=0, n_layers=1, prev=None):
    _, A, Rh, Cc = p.shape
    C = rb.shape[-1]
    tr = _row_tile(Rh, C * 4)
    if kind == "row":
        own = pl.BlockSpec((None, None, tr, C), lambda i, kc: (0, kc[0], i, 0))
    else:
        own = pl.BlockSpec((None, None, tr, C), lambda i, kc: (0, 0, i, kc[0]))
    peer = lambda j: pl.BlockSpec((None, None, tr, C), lambda i, kc: (j, 0, i, 0))

    def body(kc_ref, p_ref, r0_ref, r1_ref, r2_ref, *rest):
        o_ref = rest[-1]
        o_ref[...] = ((p_ref[...].astype(F32) + r0_ref[...].astype(F32)) + r1_ref[...].astype(F32)) + r2_ref[...].astype(F32)

    operands = [kc_idx, p, rb, rb, rb] + ([prev] if prev is not None else [])
    return pl.pallas_call(
        body, name=name,
        grid_spec=pltpu.PrefetchScalarGridSpec(
            num_scalar_prefetch=1, grid=(Rh // tr,),
            in_specs=[own, peer(0), peer(1), peer(2)] + ([ANY] if prev is not None else []),
            out_specs=pl.BlockSpec((None, None, tr, C), lambda i, kc: (layer, kc[1], i, 0))),
        out_shape=_sds((n_layers, 2, Rh, C), F32),
        input_output_aliases={5: 0} if prev is not None else {},
        compiler_params=_cparams("parallel"))(*operands)


def _mesh_place():
    x, y, c = lax.axis_index("x"), lax.axis_index("y"), lax.axis_index("c")
    chips = [(1 - x, y), (x, 1 - y), (1 - x, 1 - y)]
    return x, y, c, chips


def _remote(src, dst, send_sem, recv_sem, to):
    return pltpu.make_async_remote_copy(src_ref=src, dst_ref=dst, send_sem=send_sem, recv_sem=recv_sem,
                                        device_id=to, device_id_type=MESH_ID)


def _small_allgather(name, v, with_sum=False):
    R, N = v.shape

    def body(*refs):
        if with_sum:
            x_ref, out_ref, sum_ref, send_sems, recv_sems, local_sem = refs
        else:
            x_ref, out_ref, send_sems, recv_sems, local_sem = refs
        x, y, c, chips = _mesh_place()
        me, sibling = (x, y, c), (x, y, 1 - c)

        def rows(px, py, pc):
            return out_ref.at[pl.ds((4 * px + 2 * py + pc) * R, R), :]

        def copy(k, block, to, src=None):
            return _remote(rows(*block) if src is None else src, rows(*block), send_sems.at[k], recv_sems.at[k], to)

        mine = pltpu.make_async_copy(x_ref, rows(*me), local_sem)
        mine.start()
        first = [copy(0, me, sibling, src=x_ref)]
        first += [copy(1 + j, me, (*chip, c), src=x_ref) for j, chip in enumerate(chips)]
        for cp in first:
            cp.start()
        passed = [copy(4 + j, (*chip, c), sibling) for j, chip in enumerate(chips)]
        for j, chip in enumerate(chips):
            copy(1 + j, (*chip, c), me).wait_recv()
            passed[j].start()
        copy(0, sibling, me).wait_recv()
        for j, chip in enumerate(chips):
            copy(4 + j, (*chip, 1 - c), me).wait_recv()
        for cp in first + passed:
            cp.wait_send()
        mine.wait()
        if with_sum:
            total = out_ref[0:R, :]
            for p in range(1, 8):
                total = total + out_ref[p * R:(p + 1) * R, :]
            sum_ref[...] = total

    vm = pl.BlockSpec(memory_space=pltpu.VMEM)
    out_shape = [_sds((8 * R, N), F32)] + ([_sds((R, N), F32)] if with_sum else [])
    res = pl.pallas_call(
        body, name=name, out_shape=out_shape, in_specs=[vm], out_specs=[vm] * len(out_shape),
        scratch_shapes=[pltpu.SemaphoreType.DMA((7,)), pltpu.SemaphoreType.DMA((7,)), pltpu.SemaphoreType.DMA],
        compiler_params=pltpu.CompilerParams(vmem_limit_bytes=VMEM_LIMIT_BYTES))(v)
    return res if with_sum else res[0]


def _full_place(ref, kind, C, kk, half, layer=None):
    lead = slice(None) if layer is None else pl.ds(layer, 1)
    if kind == "row":
        return ref.at[lead, kk, half]
    return ref.at[lead, half, :, pl.ds(pl.multiple_of(kk * C, LANES), C)]


def _gather_rider(fulls, kinds, shard_cols, layers=None):
    n = len(fulls)
    layers = layers or [None] * n

    def copies(outs, sems):
        x, y, c, chips = _mesh_place()
        k = 2 * x + y
        place = lambda a, kk, half: _full_place(outs[a], kinds[a], shard_cols[a], kk, half, layers[a])
        copy = lambda a, j, ref, to: _remote(ref, ref, sems[0].at[6 * a + j], sems[1].at[6 * a + j], to)
        return (x, y, c), chips, k, place, copy

    def start(_, outs, sems):
        (x, y, c), chips, k, place, copy = copies(outs, sems)
        for j, chip in enumerate(chips):
            for a in range(n):
                copy(a, j, place(a, k, c), (*chip, c)).start()

    def finish(_, outs, sems):
        (x, y, c), chips, k, place, copy = copies(outs, sems)
        me, sibling = (x, y, c), (x, y, 1 - c)
        for j, chip in enumerate(chips):
            kj = 2 * chip[0] + chip[1]
            for a in range(n):
                copy(a, j, place(a, kj, c), me).wait_recv()
                copy(a, 3 + j, place(a, kj, c), sibling).start()
        for j, chip in enumerate(chips):
            kj = 2 * chip[0] + chip[1]
            for a in range(n):
                copy(a, 3 + j, place(a, kj, 1 - c), me).wait_recv()
        for j, chip in enumerate(chips):
            kj = 2 * chip[0] + chip[1]
            for a in range(n):
                copy(a, j, place(a, k, c), (*chip, c)).wait_send()
                copy(a, 3 + j, place(a, kj, c), sibling).wait_send()

    return Rider(tuple(fulls), tuple(_sds(f.shape, BF16) for f in fulls), {a: a for a in range(n)},
                 (pltpu.SemaphoreType.DMA((6 * n,)), pltpu.SemaphoreType.DMA((6 * n,))), start, finish)


def _scatter_rider(ps, kinds, shard_cols):
    n = len(ps)

    def copies(ins, outs, sems):
        x, y, c, chips = _mesh_place()
        cps = []
        for j, chip in enumerate(chips):
            kj = 2 * chip[0] + chip[1]
            for a in range(n):
                C = shard_cols[a]
                src = ins[a].at[:, kj] if kinds[a] == "row" else ins[a].at[:, 0, :, pl.ds(pl.multiple_of(kj * C, LANES), C)]
                cps.append(_remote(src, outs[a].at[j], sems[0].at[3 * a + j], sems[1].at[3 * a + j], (*chip, c)))
        return cps

    def start(ins, outs, sems):
        for cp in copies(ins, outs, sems):
            cp.start()

    def finish(ins, outs, sems):
        cps = copies(ins, outs, sems)
        for cp in cps:
            cp.wait_recv()
        for cp in cps:
            cp.wait_send()

    out_shape = tuple(_sds((3, p.shape[0], p.shape[2], C), BF16) for p, C in zip(ps, shard_cols))
    return Rider(tuple(ps), out_shape, {}, (pltpu.SemaphoreType.DMA((3 * n,)), pltpu.SemaphoreType.DMA((3 * n,))),
                 start, finish)


def _run_rider(name, rider):
    n_in, n_out = len(rider.operands), len(rider.out_shape)

    def body(*refs):
        ins, outs, sems = refs[:n_in], refs[n_in:n_in + n_out], refs[n_in + n_out:]
        rider.start(ins, outs, sems)
        rider.finish(ins, outs, sems)

    return pl.pallas_call(
        body, name=name, out_shape=list(rider.out_shape), in_specs=[ANY] * n_in, out_specs=[ANY] * n_out,
        input_output_aliases=dict(rider.aliases), scratch_shapes=list(rider.sems),
        compiler_params=pltpu.CompilerParams(vmem_limit_bytes=VMEM_LIMIT_BYTES))(*rider.operands)


def _pair_exchange(name, g5s):
    n = len(g5s)

    def body(*refs):
        ins, outs = refs[:n], refs[n:2 * n]
        send_sems, recv_sems = refs[2 * n:]
        x, y, c, _ = _mesh_place()
        cps = []
        for a in range(n):
            cp = _remote(ins[a].at[:, :, 1 - c], outs[a], send_sems.at[a], recv_sems.at[a], (x, y, 1 - c))
            cp.start()
            cps.append(cp)
        for cp in cps:
            cp.wait_recv()
        for cp in cps:
            cp.wait_send()

    out_shape = [_sds((g.shape[0], g.shape[1], g.shape[3], g.shape[4]), BF16) for g in g5s]
    return pl.pallas_call(
        body, name=name, out_shape=out_shape, in_specs=[ANY] * n, out_specs=[ANY] * n,
        scratch_shapes=[pltpu.SemaphoreType.DMA((n,)), pltpu.SemaphoreType.DMA((n,))],
        compiler_params=pltpu.CompilerParams(vmem_limit_bytes=VMEM_LIMIT_BYTES))(*g5s)


def _pair_share(fs):
    n = len(fs)

    def body(*refs):
        outs = refs[n:2 * n]
        send_sems, recv_sems = refs[2 * n:]
        x, y, c, _ = _mesh_place()
        cps = []
        for a in range(n):
            mine = outs[a].at[:, c]
            cp = _remote(mine, mine, send_sems.at[a], recv_sems.at[a], (x, y, 1 - c))
            cp.start()
            cps.append(cp)
        for a in range(n):
            theirs = outs[a].at[:, 1 - c]
            _remote(theirs, theirs, send_sems.at[a], recv_sems.at[a], (x, y, c)).wait_recv()
        for cp in cps:
            cp.wait_send()

    return pl.pallas_call(
        body, name="grad_pair_share", out_shape=[_sds(f.shape, F32) for f in fs],
        in_specs=[ANY] * n, out_specs=[ANY] * n, input_output_aliases={a: a for a in range(n)},
        scratch_shapes=[pltpu.SemaphoreType.DMA((n,)), pltpu.SemaphoreType.DMA((n,))],
        compiler_params=pltpu.CompilerParams(vmem_limit_bytes=VMEM_LIMIT_BYTES))(*fs)


def _pack_rows(parts, lane_mult=1024):
    flat = jnp.concatenate([p.reshape(-1).astype(F32) for p in parts])
    n = -(-flat.shape[0] // (8 * lane_mult)) * lane_mult
    return jnp.pad(flat, (0, 8 * n - flat.shape[0])).reshape(8, n)


def _relu2(acc):
    r = jnp.maximum(acc, 0.0)
    return r, r * r


def _times_2r(acc, r):
    return (acc * (2.0 * r.astype(F32)),)


def kernel(x, c, positions, w_mod, b_mod, norm_g, mla_w_in, mla_g_q, mla_g_kv, mla_w_uq, mla_w_ukv, mla_w_o, conv_w_in, conv_w, conv_w_out, mlp_w_up, mlp_w_down, loss_target, m_w_mod, m_b_mod, m_norm_g, m_mla_w_in, m_mla_g_q, m_mla_g_kv, m_mla_w_uq, m_mla_w_ukv, m_mla_w_o, m_conv_w_in, m_conv_w, m_conv_w_out, m_mlp_w_up, m_mlp_w_down, v_w_mod, v_b_mod, v_norm_g, v_mla_w_in, v_mla_g_q, v_mla_g_kv, v_mla_w_uq, v_mla_w_ukv, v_mla_w_o, v_conv_w_in, v_conv_w, v_conv_w_out, v_mlp_w_up, v_mlp_w_down):
    S, D = x.shape[1], x.shape[2]
    Dq = D // N_CHIPS
    ncol = w_mod.shape[2]
    n_mod = N_CHIPS * ncol // D
    F = mlp_w_up.shape[2] * N_CHIPS
    lat_dim = mla_w_in.shape[2]
    rank = mla_g_q.shape[1]
    H = mla_w_uq.shape[2]
    d_qk = mla_w_uq.shape[3]
    assert mla_g_kv.shape[1] == rank and lat_dim == 2 * rank + QK_ROPE and d_qk == QK_NOPE + QK_ROPE
    assert mla_w_ukv.shape[3] == QK_NOPE + V_HEAD and x.shape[0] == 1 and n_mod == 6
    assert norm_g.shape[0] == 2 and mla_w_in.shape[0] == 1 and conv_w_in.shape[0] == 1
    lat_pad = 2 * rank + LANES
    scale = float(d_qk) ** -0.5

    xi, yi, ci = lax.axis_index("x"), lax.axis_index("y"), lax.axis_index("c")
    chip = 2 * xi + yi
    dev = 2 * chip + ci
    c_idx = jnp.reshape(ci, (1,)).astype(jnp.int32)
    k_idx = jnp.reshape(chip, (1,)).astype(jnp.int32)

    n1 = D + 2 * D + 3 * Dq
    g1 = _small_allgather("gather_small_inputs", _pack_rows([c, norm_g, conv_w])).reshape(8, -1)
    c_all = g1[:, :D]
    by_chip = g1[0::2]
    norm_full = jnp.concatenate([by_chip[kk, D:3 * D].reshape(2, 4, Dq) for kk in range(N_CHIPS)], axis=-1)
    convw_full = jnp.concatenate([by_chip[kk, 3 * D:n1].reshape(3, Dq) for kk in range(N_CHIPS)], axis=-1)

    b_cols = lax.dynamic_slice(b_mod, (0, chip * ncol), (2, ncol)).reshape(2, 1, ncol)
    cond_all = _silu(c_all)
    mod_cols = _mod_fwd(cond_all, w_mod, b_cols)
    g2 = _small_allgather("gather_mod", _pack_rows([mod_cols]))
    g2 = g2.reshape(8, -1)[0::2, :2 * 8 * ncol].reshape(N_CHIPS, 2, 8, ncol)
    mod_all = jnp.transpose(g2, (2, 1, 0, 3)).reshape(8, 2, n_mod * D)
    mod_me = lax.dynamic_index_in_dim(mod_all, dev, axis=0, keepdims=False)
    mods = [[mod_me[l, i * D:(i + 1) * D].reshape(1, D) for i in range(n_mod)] for l in range(2)]
    ng = [[norm_full[l, i].reshape(1, D) for i in range(4)] for l in range(2)]

    pos = positions[0].astype(F32)
    inv_freq = ROPE_THETA ** (-jnp.arange(0, QK_ROPE, 2, dtype=F32) / QK_ROPE)
    ang = pos[:, None] * inv_freq
    cos, sin = jnp.cos(ang), jnp.sin(ang)
    zero = jnp.zeros_like(cos)
    rope_tabs = (jnp.concatenate([cos, cos, zero, zero], axis=1),
                 jnp.concatenate([-sin, zero, zero, zero], axis=1),
                 jnp.concatenate([zero, sin, zero, zero], axis=1))

    weights = [("mla_w_in", mla_w_in, "row"), ("mla_w_uq", mla_w_uq.reshape(1, rank // N_CHIPS, H * d_qk), "row"),
               ("mla_w_ukv", mla_w_ukv.reshape(1, rank // N_CHIPS, H * QK_PAD), "row"), ("mla_w_o", mla_w_o, "row"),
               ("conv_w_in", conv_w_in, "col"), ("conv_w_out", conv_w_out, "row"),
               ("mlp_w_up", mlp_w_up, "col"), ("mlp_w_down", mlp_w_down, "row")]
    kinds = [k for _, _, k in weights]
    shard_shapes = [w.shape for _, w, _ in weights]
    shard_cols = [s[2] for s in shard_shapes]
    casted = [_cast_into_full("cast_" + nm, w, kind, k_idx) for nm, w, kind in weights]
    W_IN, W_UQ, W_UKV, W_O, W_CIN, W_COUT, W_UP, W_DOWN = range(8)
    mla_idx = [W_IN, W_UQ, W_UKV, W_O]

    def view(i, buf):
        L, R, C = shard_shapes[i]
        return buf.reshape((L, N_CHIPS * R, C) if kinds[i] == "row" else (L, R, N_CHIPS * C))

    def gather_of(bufs, idx, layers=None):
        return _gather_rider(bufs, [kinds[i] for i in idx], [shard_cols[i] for i in idx], layers)

    def scatter_of(ps, idx):
        return _scatter_rider(ps, [kinds[i] for i in idx], [shard_cols[i] for i in idx])

    def pair_sums(tag, items):
        g5s = []
        for _, i, g in items:
            _, R, C = shard_shapes[i]
            g5s.append(g.reshape((1, N_CHIPS, 2, R // 2, C) if kinds[i] == "row" else (1, 1, 2, R // 2, N_CHIPS * C)))
        ras = _pair_exchange("grad_pair_exchange_" + tag, g5s)
        return [_pair_sum("pair_sum_" + nm, g5, ra, c_idx) for (nm, _, _), g5, ra in zip(items, g5s, ras)]

    got = _run_rider("gather_weights_mla", gather_of([casted[i] for i in mla_idx], mla_idx))
    w_in_p = jnp.pad(view(W_IN, got[0])[0], ((0, 0), (0, lat_pad - lat_dim)))
    w_q_p = jnp.pad(view(W_UQ, got[1])[0].reshape(rank, H, d_qk), ((0, 0), (0, 0), (0, QK_PAD - d_qk))).reshape(rank, H * QK_PAD)
    w_ukv, w_o = view(W_UKV, got[2])[0], view(W_O, got[3])[0]
    HV = H * V_HEAD

    def layer_b(l, transposed):
        if transposed:
            return lambda tm, tn, tk: pl.BlockSpec((None, tn, tk), lambda i, j, k: (l, j, k))
        return lambda tm, tn, tk: pl.BlockSpec((None, tk, tn), lambda i, j, k: (l, k, j))

    def mlp_up(tag, l, h, w, rider=NO_RIDER):
        return _mm("mlp_up_" + tag, h, w, "nn", S, F, D, [_sds((S, F), BF16)] * 2, epilogue=_relu2,
                   b_spec=layer_b(l, False), rider=rider)

    def mlp_down(tag, l, a2, w, rider=NO_RIDER):
        return _mm("mlp_down_" + tag, a2, w, "nn", S, D, F, [_sds((S, D), F32)], b_spec=layer_b(l, False), rider=rider)

    def mlp_bwd(tag, l, h, r, a2, dy, rider=NO_RIDER):
        res = _mm("mlp_down_dx_" + tag, dy, w_down, "nt", S, F, D, [_sds((S, F), BF16)], epilogue=_times_2r,
                  b_spec=layer_b(l, True), rider=rider,
                  extras=[(r, lambda tm, tn, tk: pl.BlockSpec((tm, tn), lambda i, j, k: (i, j)))])
        (da,), carried = res if rider.start is not None else (res, ())
        (dw_down,) = _mm("mlp_down_dw_" + tag, a2, dy, "tn", F, D, S, [_sds((F, D), BF16)])
        (dh,) = _mm("mlp_up_dx_" + tag, da, w_up, "nt", S, D, F, [_sds((S, D), F32)], b_spec=layer_b(l, True))
        (dw_up,) = _mm("mlp_up_dw_" + tag, h, da, "tn", D, F, S, [_sds((D, F), BF16)])
        return dh, dw_up, dw_down, carried

    x0 = x[0]
    sh1, sc1, gt1, sh2, sc2, gt2 = mods[0]
    (h1,) = _fwd_boundary("fwd_boundary_0", x0, None, None, None, ng[0][0], sc1, sh1)
    (lat,) = _mm("mla_in", h1, w_in_p, "nn", S, lat_pad, D, [_sds((S, lat_pad), F32)], tn=lat_pad)
    cq, ckv, kr = _latent_fwd(lat, mla_g_q, mla_g_kv, rope_tabs, rank)

    def rope_q(acc, cos_p, sin_lo, sin_hi):
        parts = []
        for hh in range(acc.shape[1] // QK_PAD):
            parts.append(acc[:, hh * QK_PAD:hh * QK_PAD + QK_NOPE])
            parts.append(_rope(acc[:, hh * QK_PAD + QK_NOPE:(hh + 1) * QK_PAD], cos_p, sin_lo, sin_hi))
        return (jnp.concatenate(parts, axis=1),)

    tab_extra = lambda tm, tn, tk: pl.BlockSpec((tm, LANES), lambda i, j, k: (i, 0))
    (q,) = _mm("mla_q", cq, w_q_p, "nn", S, H * QK_PAD, rank, [_sds((S, H * QK_PAD), BF16)], epilogue=rope_q,
               extras=[(t, tab_extra) for t in rope_tabs], tn=2 * QK_PAD)
    (kv,) = _mm("mla_kv", ckv, w_ukv, "nn", S, H * QK_PAD, rank, [_sds((S, H * QK_PAD), BF16)])
    rest_idx = [W_CIN, W_COUT, W_UP, W_DOWN]
    o, lse, (cin_buf, cout_buf, up_buf, down_buf) = _attn_fwd(
        q, kv, kr, H, scale, gather_of([casted[i] for i in rest_idx], rest_idx, [None, None, 0, 0]))
    w_cin, w_cout = view(W_CIN, cin_buf)[0], view(W_COUT, cout_buf)[0]
    (y1,) = _mm("mla_out", o, w_o, "nn", S, D, HV, [_sds((S, D), F32)])
    x1, h2 = _fwd_boundary("fwd_boundary_1", x0, y1, gt1, ng[0][1], ng[0][2], sc2, sh2)
    (r2, a2), (down_buf,) = mlp_up("0", 0, h2, view(W_UP, up_buf), gather_of([down_buf], [W_DOWN], [1]))
    (y2,), (up_buf,) = mlp_down("0", 0, a2, view(W_DOWN, down_buf), gather_of([up_buf], [W_UP], [1]))
    w_up, w_down = view(W_UP, up_buf), view(W_DOWN, down_buf)

    sh1b, sc1b, gt1b, sh2b, sc2b, gt2b = mods[1]
    x2, h3 = _fwd_boundary("fwd_boundary_2", x1, y2, gt2, ng[0][3], ng[1][0], sc1b, sh1b)
    nD = lambda tn: D // tn
    (proj3,) = _mm("conv_in", h3, w_cin, "nn", S, 3 * D, D, [_sds((3, S, D), F32)], tn=min(1024, D),
                   out_specs=[lambda tm, tn, tk: pl.BlockSpec((None, tm, tn), lambda i, j, k: (j // nD(tn), i, j % nD(tn)))])
    bz = _conv_fwd(proj3, convw_full)
    (y3,) = _mm("conv_out", bz, w_cout, "nn", S, D, D, [_sds((S, D), F32)])
    x3, h4 = _fwd_boundary("fwd_boundary_3", x2, y3, gt1b, ng[1][1], ng[1][2], sc2b, sh2b)
    r4, a4 = mlp_up("1", 1, h4, w_up)
    (y4,) = mlp_down("1", 1, a4, w_down)

    dx4, dy4, sums_l, loss_acc = _loss_boundary("loss_boundary", x3, y4, gt2b, ng[1][3], loss_target[0])
    loss = lax.psum(loss_acc[0, 0], ("x", "y", "c"))

    dh4, dw_up1, dw_down1, _ = mlp_bwd("1", 1, h4, r4, a4, dy4)
    ps_up1, ps_down1 = pair_sums("mlp_1", [("mlp_w_up_1", W_UP, dw_up1), ("mlp_w_down_1", W_DOWN, dw_down1)])
    dx3, dy3, sums_3 = _bwd_boundary("bwd_boundary_3", dx4, dh4, x3, y3, gt1b, ng[1][1], ng[1][2], sc2b)

    (dbz,) = _mm("conv_out_dx", dy3, w_cout, "nt", S, D, D, [_sds((S, D), F32)])
    (dw_cout,) = _mm("conv_out_dw", bz, dy3, "tn", D, D, S, [_sds((D, D), BF16)])
    dproj3, dconvw = _conv_bwd(dbz, proj3, convw_full)
    (dh3,), (rb_up1,) = _mm(
        "conv_in_dx", dproj3, w_cin, "nt", S, D, 3 * D, [_sds((S, D), F32)], tk=min(1024, D), rider=scatter_of([ps_up1], [W_UP]),
        a_spec=lambda tm, tn, tk: pl.BlockSpec((None, tm, tk), lambda i, j, k: (k // (D // tk), i, k % (D // tk))))
    (dw_cin,), (rb_down1,) = _mm(
        "conv_in_dw", h3, dproj3, "tn", D, 3 * D, S, [_sds((D, 3 * D), BF16)], tn=min(1024, D),
        rider=scatter_of([ps_down1], [W_DOWN]),
        b_spec=lambda tm, tn, tk: pl.BlockSpec((None, tk, tn), lambda i, j, k: (j // nD(tn), k, j % nD(tn))))
    ps_cin, ps_cout = pair_sums("conv", [("conv_w_in", W_CIN, dw_cin), ("conv_w_out", W_COUT, dw_cout)])
    dx2, dy2, sums_2 = _bwd_boundary("bwd_boundary_2", dx3, dh3, x2, y2, gt2, ng[0][3], ng[1][0], sc1b)

    dh2, dw_up0, dw_down0, (rb_cin, rb_cout) = mlp_bwd("0", 0, h2, r2, a2, dy2, scatter_of([ps_cin, ps_cout], [W_CIN, W_COUT]))
    ps_up0, ps_down0 = pair_sums("mlp_0", [("mlp_w_up_0", W_UP, dw_up0), ("mlp_w_down_0", W_DOWN, dw_down0)])
    dx1, dy1, sums_1 = _bwd_boundary("bwd_boundary_1", dx2, dh2, x1, y1, gt1, ng[0][1], ng[0][2], sc2)

    (do,) = _mm("mla_out_dx", dy1, w_o, "nt", S, HV, D, [_sds((S, HV), BF16)])
    (dw_o,) = _mm("mla_out_dw", o, dy1, "tn", HV, D, S, [_sds((HV, D), BF16)])
    dq, dkv, dkr, (rb_up0, rb_down0) = _attn_bwd(q, kv, kr, _attn_delta(o, do, H), do, lse, rope_tabs, H, scale,
                                                 scatter_of([ps_up0, ps_down0], [W_UP, W_DOWN]))
    (dcq,) = _mm("mla_q_dx", dq, w_q_p, "nt", S, rank, H * QK_PAD, [_sds((S, rank), F32)])
    (dw_q_p,) = _mm("mla_q_dw", cq, dq, "tn", rank, H * QK_PAD, S, [_sds((rank, H * QK_PAD), BF16)])
    (dckv,) = _mm("mla_kv_dx", dkv, w_ukv, "nt", S, rank, H * QK_PAD, [_sds((S, rank), F32)])
    (dw_ukv,) = _mm("mla_kv_dw", ckv, dkv, "tn", rank, H * QK_PAD, S, [_sds((rank, H * QK_PAD), BF16)])
    dlat, sums_lat = _latent_bwd(lat, dcq, dckv, dkr, mla_g_q, mla_g_kv, rope_tabs, rank)
    (dh1,) = _mm("mla_in_dx", dlat, w_in_p, "nt", S, D, lat_pad, [_sds((S, D), F32)])
    (dw_in_p,) = _mm("mla_in_dw", h1, dlat, "tn", D, lat_pad, S, [_sds((D, lat_pad), BF16)], tn=lat_pad)
    grad_x, sums_0 = _bwd_boundary("bwd_boundary_0", dx1, dh1, x0, None, None, None, ng[0][0], sc1)

    dmod0 = [sums_0[0], sums_0[1], sums_1[3], sums_1[0], sums_1[1], sums_2[3]]
    dmod1 = [sums_2[0], sums_2[1], sums_3[3], sums_3[0], sums_3[1], sums_l[3]]
    dng0 = [sums_0[2], sums_1[4], sums_1[2], sums_2[4]]
    dng1 = [sums_2[2], sums_3[4], sums_3[2], sums_l[4]]
    small = _pack_rows(dmod0 + dmod1 + dng0 + dng1 + [sums_lat[0], sums_lat[1], dconvw], lane_mult=LANES)
    gathered, total = _small_allgather("gather_small_grads", small, with_sum=True)
    n_dm = 2 * n_mod * D
    dmod_all = gathered.reshape(8, -1)[:, :n_dm].reshape(8, 2, n_mod * D)
    total = total.reshape(-1)
    g_b_mod = total[:n_dm].reshape(2, n_mod * D)
    g_norm = lax.dynamic_slice(total[n_dm:n_dm + 8 * D].reshape(2, 4, D), (0, 0, chip * Dq), (2, 4, Dq))
    off = n_dm + 8 * D
    g_gq = total[off:off + rank].reshape(1, rank)
    g_gkv = total[off + rank:off + 2 * rank].reshape(1, rank)
    off += 2 * rank
    g_convw = lax.dynamic_slice(total[off:off + 3 * D].reshape(1, 3, D), (0, 0, chip * Dq), (1, 3, Dq))

    dw_mla = [dw_in_p[:, :lat_dim], dw_q_p.reshape(rank, H, QK_PAD)[:, :, :d_qk].reshape(rank, H * d_qk), dw_ukv, dw_o]
    ps_mla = pair_sums("mla", [(weights[i][0], i, g) for i, g in zip(mla_idx, dw_mla)])
    rbs_mla = _run_rider("grad_chip_scatter_mla", scatter_of(ps_mla, mla_idx))
    kc_idx = jnp.stack([chip, ci]).astype(jnp.int32)
    fs = [_chip_sum("chip_sum_" + weights[i][0], p, rb, kc_idx, kinds[i])
          for i, p, rb in zip(mla_idx + [W_CIN, W_COUT], ps_mla + [ps_cin, ps_cout], list(rbs_mla) + [rb_cin, rb_cout])]
    for i, (p1, r1), (p0, r0) in [(W_UP, (ps_up1, rb_up1), (ps_up0, rb_up0)), (W_DOWN, (ps_down1, rb_down1), (ps_down0, rb_down0))]:
        f = _chip_sum("chip_sum_" + weights[i][0] + "_1", p1, r1, kc_idx, kinds[i], layer=1, n_layers=2)
        fs.append(_chip_sum("chip_sum_" + weights[i][0] + "_0", p0, r0, kc_idx, kinds[i], layer=0, n_layers=2, prev=f))
    finals = _pair_share(fs)
    orig = [mla_w_in, mla_w_uq, mla_w_ukv, mla_w_o, conv_w_in, conv_w_out, mlp_w_up, mlp_w_down]
    big_grads = [f.reshape(w.shape) for f, w in zip(finals, orig)]

    dmod_cols = jnp.transpose(lax.dynamic_slice(dmod_all.reshape(8, 2, N_CHIPS, ncol), (0, 0, chip, 0), (8, 2, 1, ncol))
                              .reshape(8, 2, ncol), (1, 0, 2))
    g_w_mod, d_w_mod, nm_w_mod, nv_w_mod = _adamw_mod(w_mod, cond_all.T, dmod_cols, m_w_mod, v_w_mod)

    names = ["b_mod", "norm_g", "mla_w_in", "mla_g_q", "mla_g_kv", "mla_w_uq", "mla_w_ukv", "mla_w_o",
             "conv_w_in", "conv_w", "conv_w_out", "mlp_w_up", "mlp_w_down"]
    ws = [b_mod, norm_g, mla_w_in, mla_g_q, mla_g_kv, mla_w_uq, mla_w_ukv, mla_w_o, conv_w_in, conv_w, conv_w_out,
          mlp_w_up, mlp_w_down]
    ms = [m_b_mod, m_norm_g, m_mla_w_in, m_mla_g_q, m_mla_g_kv, m_mla_w_uq, m_mla_w_ukv, m_mla_w_o, m_conv_w_in,
          m_conv_w, m_conv_w_out, m_mlp_w_up, m_mlp_w_down]
    vs = [v_b_mod, v_norm_g, v_mla_w_in, v_mla_g_q, v_mla_g_kv, v_mla_w_uq, v_mla_w_ukv, v_mla_w_o, v_conv_w_in,
          v_conv_w, v_conv_w_out, v_mlp_w_up, v_mlp_w_down]
    gs = [g_b_mod, g_norm, big_grads[0], g_gq, g_gkv, big_grads[1], big_grads[2], big_grads[3], big_grads[4],
          g_convw, big_grads[5], big_grads[6], big_grads[7]]
    grads, deltas, new_ms, new_vs = [g_w_mod], [d_w_mod], [nm_w_mod], [nv_w_mod]
    for nm, w, g, m, v in zip(names, ws, gs, ms, vs):
        d, nm_, nv_ = _adamw("adamw_" + nm, w, g, m, v)
        grads.append(g)
        deltas.append(d)
        new_ms.append(nm_)
        new_vs.append(nv_)
    return (loss, grad_x[None], *grads, *deltas, *new_ms, *new_vs)
```

```python
from typing import NamedTuple

import jax
import jax.numpy as jnp
from jax import lax
from jax.experimental import pallas as pl
from jax.experimental.pallas import tpu as pltpu

F32 = jnp.float32
BF16 = jnp.bfloat16
NORM_EPS = 1e-6
ROPE_THETA = 10000.0
QK_NOPE = 128
QK_ROPE = 64
V_HEAD = 128
LANES = 128
QK_PAD = QK_NOPE + LANES
ADAM_LR, ADAM_B1, ADAM_B2, ADAM_EPS, ADAM_WD, ADAM_STEP = 0.001, 0.9, 0.999, 1e-08, 0.01, 10
VMEM_LIMIT_BYTES = 48 * 1024 * 1024
N_CHIPS = 4
MESH_ID = pl.DeviceIdType.MESH
ANY = pl.BlockSpec(memory_space=pl.ANY)
NEG_INF = float("-inf")

DIMS_NN = (((1,), (0,)), ((), ()))
DIMS_NT = (((1,), (1,)), ((), ()))
DIMS_TN = (((0,), (0,)), ((), ()))


def _cparams(*sem):
    return pltpu.CompilerParams(dimension_semantics=sem, vmem_limit_bytes=VMEM_LIMIT_BYTES)


def _row_tile(rows, row_bytes, limit=2 * 1024 * 1024, mult=16):
    if rows * row_bytes <= limit or rows % mult:
        return rows
    best = mult
    t = mult
    while t <= rows:
        if rows % t == 0 and t * row_bytes <= limit:
            best = t
        t += mult
    return best


def _rms(v):
    return lax.rsqrt(jnp.mean(v * v, axis=-1, keepdims=True) + NORM_EPS)


class Rider(NamedTuple):
    operands: tuple
    out_shape: tuple
    aliases: dict
    sems: tuple
    start: object
    finish: object


NO_RIDER = Rider((), (), {}, (), None, None)


def _mm(name, a, b, mode, M, N, K, outs, *, a_spec=None, b_spec=None, out_specs=None, epilogue=None,
        extras=(), rider=NO_RIDER, tm=1024, tn=1024, tk=2048):
    tm, tn, tk = min(tm, M), min(tn, N), min(tk, K)
    assert M % tm == 0 and N % tn == 0 and K % tk == 0, (name, M, N, K)
    nk = K // tk
    if a_spec is None:
        a_spec = {"nn": pl.BlockSpec((tm, tk), lambda i, j, k: (i, k)),
                  "nt": pl.BlockSpec((tm, tk), lambda i, j, k: (i, k)),
                  "tn": pl.BlockSpec((tk, tm), lambda i, j, k: (k, i))}[mode]
    else:
        a_spec = a_spec(tm, tn, tk)
    if b_spec is None:
        b_spec = {"nn": pl.BlockSpec((tk, tn), lambda i, j, k: (k, j)),
                  "nt": pl.BlockSpec((tn, tk), lambda i, j, k: (j, k)),
                  "tn": pl.BlockSpec((tk, tn), lambda i, j, k: (k, j))}[mode]
    else:
        b_spec = b_spec(tm, tn, tk)
    if out_specs is None:
        out_specs = [pl.BlockSpec((tm, tn), lambda i, j, k: (i, j)) for _ in outs]
    else:
        out_specs = [s(tm, tn, tk) for s in out_specs]
    dims = {"nn": DIMS_NN, "nt": DIMS_NT, "tn": DIMS_TN}[mode]
    ne, no = len(extras), len(outs)
    n_ri, n_ro = len(rider.operands), len(rider.out_shape)
    grid = (M // tm, N // tn, nk)

    def body(*refs):
        a_ref, b_ref = refs[0], refs[1]
        ex = refs[2:2 + ne]
        r_in = refs[2 + ne:2 + ne + n_ri]
        o = refs[2 + ne + n_ri:2 + ne + n_ri + no]
        r_out = refs[2 + ne + n_ri + no:2 + ne + n_ri + no + n_ro]
        scratch = refs[2 + ne + n_ri + no + n_ro:]
        r_sems = scratch[1:] if nk > 1 else scratch
        ii, jj, kk = pl.program_id(0), pl.program_id(1), pl.program_id(2)

        if rider.start is not None:
            @pl.when((ii == 0) & (jj == 0) & (kk == 0))
            def _():
                rider.start(r_in, r_out, r_sems)

        part = lax.dot_general(a_ref[...].astype(BF16), b_ref[...].astype(BF16), dims,
                               preferred_element_type=F32)

        def finish(total):
            vals = epilogue(total, *[e[...] for e in ex]) if epilogue is not None else (total,)
            for r, v in zip(o, vals):
                r[...] = v.astype(r.dtype)

        if nk == 1:
            finish(part)
        else:
            acc = scratch[0]

            @pl.when(kk == 0)
            def _():
                acc[...] = part

            @pl.when(kk > 0)
            def _():
                acc[...] += part

            @pl.when(kk == nk - 1)
            def _():
                finish(acc[...])

        if rider.finish is not None:
            @pl.when((ii == grid[0] - 1) & (jj == grid[1] - 1) & (kk == nk - 1))
            def _():
                rider.finish(r_in, r_out, r_sems)

    operands = [a, b] + [e[0] for e in extras] + list(rider.operands)
    in_specs = [a_spec, b_spec] + [e[1](tm, tn, tk) for e in extras] + [ANY] * n_ri
    hosted = rider.start is not None
    res = pl.pallas_call(
        body, name=name, grid=grid,
        in_specs=in_specs, out_specs=out_specs + [ANY] * n_ro, out_shape=list(outs) + list(rider.out_shape),
        scratch_shapes=([pltpu.VMEM((tm, tn), F32)] if nk > 1 else []) + list(rider.sems),
        input_output_aliases={2 + ne + i: no + r for i, r in rider.aliases.items()},
        compiler_params=_cparams(*(("arbitrary",) * 3 if hosted else ("parallel", "parallel", "arbitrary"))),
    )(*operands)
    return (res[:no], res[no:]) if hosted else res


def _sds(shape, dtype):
    return jax.ShapeDtypeStruct(tuple(shape), dtype)


def _rope(t, cos_p, sin_lo, sin_hi):
    return t * cos_p + pltpu.roll(t, LANES - QK_ROPE // 2, 1) * sin_lo + pltpu.roll(t, QK_ROPE // 2, 1) * sin_hi


def _rope_t(d, cos_p, sin_lo, sin_hi):
    return d * cos_p + pltpu.roll(d * sin_lo, QK_ROPE // 2, 1) + pltpu.roll(d * sin_hi, LANES - QK_ROPE // 2, 1)


def _vec_spec(d):
    return pl.BlockSpec((1, d), lambda i: (0, 0))


def _fwd_boundary(name, x_prev, y, gate, ng_post, ng_pre, sc, sh):
    S, D = x_prev.shape
    ts = min(256, S)
    has_y = y is not None
    row = pl.BlockSpec((ts, D), lambda i: (i, 0))

    def body(*refs):
        if has_y:
            x_ref, y_ref, g_ref, ngp_ref, ngn_ref, sc_ref, sh_ref, xo_ref, h_ref = refs
            yv = y_ref[...]
            xn = x_ref[...] + g_ref[...] * (yv * _rms(yv) * ngp_ref[...])
            xo_ref[...] = xn
        else:
            x_ref, ngn_ref, sc_ref, sh_ref, h_ref = refs
            xn = x_ref[...]
        hn = xn * _rms(xn) * ngn_ref[...]
        h_ref[...] = (hn * (1.0 + sc_ref[...]) + sh_ref[...]).astype(BF16)

    vec = _vec_spec(D)
    if has_y:
        operands = (x_prev, y, gate, ng_post, ng_pre, sc, sh)
        in_specs = [row, row, vec, vec, vec, vec, vec]
        out_shape = [_sds((S, D), F32), _sds((S, D), BF16)]
        out_specs = [row, row]
    else:
        operands = (x_prev, ng_pre, sc, sh)
        in_specs = [row, vec, vec, vec]
        out_shape = [_sds((S, D), BF16)]
        out_specs = [row]
    return pl.pallas_call(body, name=name, grid=(S // ts,), in_specs=in_specs, out_specs=out_specs,
                          out_shape=out_shape, compiler_params=_cparams("parallel"))(*operands)


def _acc_rows(sums_ref, rows):
    for r, v in rows:
        sums_ref[r:r + 1, :] += jnp.sum(v, axis=0, keepdims=True)


def _post_norm_bwd(dxt, yv, gate, ng_post, sums_ref, dy_ref):
    r1 = _rms(yv)
    yhat = yv * r1
    dn = dxt * gate
    u = dn * ng_post
    dy = r1 * (u - yhat * jnp.mean(u * yhat, axis=-1, keepdims=True))
    dy_ref[...] = dy.astype(dy_ref.dtype)
    _acc_rows(sums_ref, [(3, dxt * (yhat * ng_post)), (4, dn * yhat)])


def _loss_boundary(name, x_prev, y, gate, ng_post, target):
    S, D = x_prev.shape
    ts = min(256, S)
    row = pl.BlockSpec((ts, D), lambda i: (i, 0))
    vec = _vec_spec(D)

    def body(x_ref, y_ref, g_ref, ngp_ref, t_ref, dx_ref, dy_ref, sums_ref, loss_ref):
        @pl.when(pl.program_id(0) == 0)
        def _():
            sums_ref[...] = jnp.zeros_like(sums_ref)
            loss_ref[...] = jnp.zeros_like(loss_ref)

        yv = y_ref[...]
        xf = x_ref[...] + g_ref[...] * (yv * _rms(yv) * ngp_ref[...])
        err = xf - t_ref[...]
        loss_ref[...] += 0.5 * jnp.sum(jnp.mean(err * err, axis=-1, keepdims=True))
        dxt = err / D
        dx_ref[...] = dxt
        _post_norm_bwd(dxt, yv, g_ref[...], ngp_ref[...], sums_ref, dy_ref)

    return pl.pallas_call(
        body, name=name, grid=(S // ts,),
        in_specs=[row, row, vec, vec, row],
        out_specs=[row, row, pl.BlockSpec((8, D), lambda i: (0, 0)), pl.BlockSpec((8, LANES), lambda i: (0, 0))],
        out_shape=[_sds((S, D), F32), _sds((S, D), BF16), _sds((8, D), F32), _sds((8, LANES), F32)],
        compiler_params=_cparams("arbitrary"))(x_prev, y, gate, ng_post, target)


def _bwd_boundary(name, dx_new, dh, x_new, y, gate, ng_post, ng_pre, sc):
    S, D = x_new.shape
    ts = min(256, S)
    has_y = y is not None
    row = pl.BlockSpec((ts, D), lambda i: (i, 0))
    vec = _vec_spec(D)

    def body(*refs):
        if has_y:
            dxn_ref, dh_ref, x_ref, y_ref, g_ref, ngp_ref, ngn_ref, sc_ref, dxo_ref, dy_ref, sums_ref = refs
        else:
            dxn_ref, dh_ref, x_ref, ngn_ref, sc_ref, dxo_ref, sums_ref = refs

        @pl.when(pl.program_id(0) == 0)
        def _():
            sums_ref[...] = jnp.zeros_like(sums_ref)

        xv = x_ref[...]
        dhv = dh_ref[...]
        ngn = ngn_ref[...]
        r2 = _rms(xv)
        xhat = xv * r2
        dn_pre = dhv * (1.0 + sc_ref[...])
        u2 = dn_pre * ngn
        dxt = dxn_ref[...] + r2 * (u2 - xhat * jnp.mean(u2 * xhat, axis=-1, keepdims=True))
        dxo_ref[...] = dxt
        _acc_rows(sums_ref, [(0, dhv), (1, dhv * (xhat * ngn)), (2, dn_pre * xhat)])
        if has_y:
            _post_norm_bwd(dxt, y_ref[...], g_ref[...], ngp_ref[...], sums_ref, dy_ref)

    sums_spec = pl.BlockSpec((8, D), lambda i: (0, 0))
    if has_y:
        operands = (dx_new, dh, x_new, y, gate, ng_post, ng_pre, sc)
        in_specs = [row, row, row, row, vec, vec, vec, vec]
        out_shape = [_sds((S, D), F32), _sds((S, D), BF16), _sds((8, D), F32)]
        out_specs = [row, row, sums_spec]
    else:
        operands = (dx_new, dh, x_new, ng_pre, sc)
        in_specs = [row, row, row, vec, vec]
        out_shape = [_sds((S, D), F32), _sds((8, D), F32)]
        out_specs = [row, sums_spec]
    return pl.pallas_call(body, name=name, grid=(S // ts,), in_specs=in_specs, out_specs=out_specs,
                          out_shape=out_shape, compiler_params=_cparams("arbitrary"))(*operands)


def _latent_fwd(lat, g_q, g_kv, rope_tabs, rank):
    S, W = lat.shape
    ts = min(256, S)
    tab = pl.BlockSpec((ts, LANES), lambda i: (i, 0))

    def body(lat_ref, gq_ref, gkv_ref, cos_ref, slo_ref, shi_ref, cq_ref, ckv_ref, kr_ref):
        lq = lat_ref[:, 0:rank]
        lkv = lat_ref[:, rank:2 * rank]
        cq_ref[...] = (lq * _rms(lq) * gq_ref[...]).astype(BF16)
        ckv_ref[...] = (lkv * _rms(lkv) * gkv_ref[...]).astype(BF16)
        kr_ref[...] = _rope(lat_ref[:, 2 * rank:W], cos_ref[...], slo_ref[...], shi_ref[...]).astype(BF16)

    return pl.pallas_call(
        body, name="mla_latent_fwd", grid=(S // ts,),
        in_specs=[pl.BlockSpec((ts, W), lambda i: (i, 0)), _vec_spec(rank), _vec_spec(rank), tab, tab, tab],
        out_specs=[pl.BlockSpec((ts, rank), lambda i: (i, 0)), pl.BlockSpec((ts, rank), lambda i: (i, 0)), tab],
        out_shape=[_sds((S, rank), BF16), _sds((S, rank), BF16), _sds((S, LANES), BF16)],
        compiler_params=_cparams("parallel"))(lat, g_q, g_kv, *rope_tabs)


def _latent_bwd(lat, dcq, dckv, dkr, g_q, g_kv, rope_tabs, rank):
    S, W = lat.shape
    ts = min(256, S)
    tab = pl.BlockSpec((ts, LANES), lambda i: (i, 0))
    half = pl.BlockSpec((ts, rank), lambda i: (i, 0))

    def body(lat_ref, dcq_ref, dckv_ref, dkr_ref, gq_ref, gkv_ref, cos_ref, slo_ref, shi_ref, dlat_ref, sums_ref):
        @pl.when(pl.program_id(0) == 0)
        def _():
            sums_ref[...] = jnp.zeros_like(sums_ref)

        def norm_bwd(v, dn, g, r):
            rr = _rms(v)
            vhat = v * rr
            u = dn * g
            sums_ref[r:r + 1, :] += jnp.sum(dn * vhat, axis=0, keepdims=True)
            return rr * (u - vhat * jnp.mean(u * vhat, axis=-1, keepdims=True))

        dlat_ref[:, 0:rank] = norm_bwd(lat_ref[:, 0:rank], dcq_ref[...], gq_ref[...], 0).astype(BF16)
        dlat_ref[:, rank:2 * rank] = norm_bwd(lat_ref[:, rank:2 * rank], dckv_ref[...], gkv_ref[...], 1).astype(BF16)
        dlat_ref[:, 2 * rank:W] = _rope_t(dkr_ref[...], cos_ref[...], slo_ref[...], shi_ref[...]).astype(BF16)

    return pl.pallas_call(
        body, name="mla_latent_bwd", grid=(S // ts,),
        in_specs=[pl.BlockSpec((ts, W), lambda i: (i, 0)), half, half, tab, _vec_spec(rank), _vec_spec(rank),
                  tab, tab, tab],
        out_specs=[pl.BlockSpec((ts, W), lambda i: (i, 0)), pl.BlockSpec((8, rank), lambda i: (0, 0))],
        out_shape=[_sds((S, W), BF16), _sds((8, rank), F32)],
        compiler_params=_cparams("arbitrary"))(lat, dcq, dckv, dkr, g_q, g_kv, *rope_tabs)


def _attn_tiles(S):
    t = min(512, S)
    return t, S // t


def _causal_mask(t):
    return lax.broadcasted_iota(jnp.int32, (t, t), 1) <= lax.broadcasted_iota(jnp.int32, (t, t), 0)


def _attn_fwd(q, kv, kr, heads, scale, rider=NO_RIDER):
    S = q.shape[0]
    t, nb = _attn_tiles(S)
    G = 2 if heads % 2 == 0 else 1
    n_ri, n_ro = len(rider.operands), len(rider.out_shape)

    def body(*refs):
        q_ref, kv_ref, kr_ref = refs[:3]
        r_in = refs[3:3 + n_ri]
        o_ref, lse_ref = refs[3 + n_ri:5 + n_ri]
        r_out = refs[5 + n_ri:5 + n_ri + n_ro]
        m_scr, acc_scr = refs[5 + n_ri + n_ro:7 + n_ri + n_ro]
        r_sems = refs[7 + n_ri + n_ro:]
        h, qi, ki = pl.program_id(0), pl.program_id(1), pl.program_id(2)

        if rider.start is not None:
            @pl.when((h == 0) & (qi == 0) & (ki == 0))
            def _():
                rider.start(r_in, r_out, r_sems)

        @pl.when(ki == 0)
        def _():
            m_scr[...] = jnp.full_like(m_scr, NEG_INF)
            acc_scr[...] = jnp.zeros_like(acc_scr)

        def step(diagonal):
            ones = jnp.ones((t, LANES), BF16)
            for g in range(G):
                kcat = jnp.concatenate([kv_ref[:, g * QK_PAD:g * QK_PAD + QK_NOPE], kr_ref[...]], axis=1)
                vext = jnp.concatenate([kv_ref[:, g * QK_PAD + QK_NOPE:(g + 1) * QK_PAD], ones], axis=1)
                s = lax.dot_general(q_ref[:, g * QK_PAD:(g + 1) * QK_PAD], kcat, DIMS_NT,
                                    preferred_element_type=F32) * scale
                if diagonal:
                    s = jnp.where(_causal_mask(t), s, NEG_INF)
                m_prev = m_scr[g]
                m_new = jnp.maximum(m_prev, jnp.max(s, axis=-1, keepdims=True))
                alpha = jnp.exp(m_prev - m_new)
                p = jnp.exp(s - jnp.tile(m_new, (1, t // LANES)))
                acc_scr[g] = jnp.tile(alpha, (1, 2)) * acc_scr[g] + lax.dot_general(
                    p.astype(BF16), vext, DIMS_NN, preferred_element_type=F32)
                m_scr[g] = m_new

        @pl.when(ki < qi)
        def _():
            step(False)

        @pl.when(ki == qi)
        def _():
            step(True)

        @pl.when(ki == nb - 1)
        def _():
            for g in range(G):
                acc = acc_scr[g]
                o_ref[:, g * V_HEAD:(g + 1) * V_HEAD] = (acc[:, 0:V_HEAD] / acc[:, V_HEAD:2 * V_HEAD]).astype(BF16)
                lse_ref[g] = m_scr[g] + jnp.log(acc[:, V_HEAD:2 * V_HEAD])

        if rider.finish is not None:
            @pl.when((h == heads // G - 1) & (qi == nb - 1) & (ki == nb - 1))
            def _():
                rider.finish(r_in, r_out, r_sems)

    res = pl.pallas_call(
        body, name="mla_attn_fwd", grid=(heads // G, nb, nb),
        in_specs=[pl.BlockSpec((t, G * QK_PAD), lambda h, qi, ki: (qi, h)),
                  pl.BlockSpec((t, G * QK_PAD), lambda h, qi, ki: (jnp.minimum(ki, qi), h)),
                  pl.BlockSpec((t, LANES), lambda h, qi, ki: (jnp.minimum(ki, qi), 0))] + [ANY] * n_ri,
        out_specs=[pl.BlockSpec((t, G * V_HEAD), lambda h, qi, ki: (qi, h)),
                   pl.BlockSpec((G, t, LANES), lambda h, qi, ki: (h, qi, 0))] + [ANY] * n_ro,
        out_shape=[_sds((S, heads * V_HEAD), BF16), _sds((heads, S, LANES), F32)] + list(rider.out_shape),
        scratch_shapes=[pltpu.VMEM((G, t, LANES), F32), pltpu.VMEM((G, t, 2 * V_HEAD), F32)] + list(rider.sems),
        input_output_aliases={3 + i: 2 + o for i, o in rider.aliases.items()},
        compiler_params=_cparams("arbitrary", "arbitrary", "arbitrary"))(q, kv, kr, *rider.operands)
    return res[0], res[1], res[2:]


def _attn_delta(o, do, heads):
    S = o.shape[0]
    t, nb = _attn_tiles(S)

    def body(o_ref, do_ref, out_ref):
        d = jnp.sum(do_ref[...].astype(F32) * o_ref[...].astype(F32), axis=-1, keepdims=True)
        out_ref[...] = jnp.broadcast_to(d, (t, LANES))

    blk = pl.BlockSpec((t, V_HEAD), lambda h, i: (i, h))
    return pl.pallas_call(
        body, name="mla_attn_delta", grid=(heads, nb), in_specs=[blk, blk],
        out_specs=pl.BlockSpec((None, t, LANES), lambda h, i: (h, i, 0)),
        out_shape=_sds((heads, S, LANES), F32), compiler_params=_cparams("parallel", "parallel"))(o, do)


def _attn_bwd(q, kv, kr, delta, do, lse, rope_tabs, heads, scale, rider=NO_RIDER):
    S = q.shape[0]
    t, nb = _attn_tiles(S)
    n_ri, n_ro = len(rider.operands), len(rider.out_shape)
    rep = t // LANES

    def body(*refs):
        q_ref, kv_ref, kr_ref, delta_ref, do_ref, lse_ref, cos_ref, slo_ref, shi_ref = refs[:9]
        r_in = refs[9:9 + n_ri]
        dq_ref, dkv_ref, dkr_ref = refs[9 + n_ri:12 + n_ri]
        r_out = refs[12 + n_ri:12 + n_ri + n_ro]
        dq_scr, dk_scr, dv_scr, dkr_scr = refs[12 + n_ri + n_ro:16 + n_ri + n_ro]
        r_sems = refs[16 + n_ri + n_ro:]
        h, ki, qi = pl.program_id(0), pl.program_id(1), pl.program_id(2)
        q_rows = pl.ds(pl.multiple_of(qi * t, t), t)
        k_rows = pl.ds(pl.multiple_of(ki * t, t), t)

        if rider.start is not None:
            @pl.when((h == 0) & (ki == 0) & (qi == 0))
            def _():
                rider.start(r_in, r_out, r_sems)

        @pl.when((ki == 0) & (qi == 0))
        def _():
            dq_scr[...] = jnp.zeros_like(dq_scr)

        @pl.when((h == 0) & (ki == 0) & (qi == 0))
        def _():
            dkr_scr[...] = jnp.zeros_like(dkr_scr)

        @pl.when(qi == 0)
        def _():
            dk_scr[...] = jnp.zeros_like(dk_scr)
            dv_scr[...] = jnp.zeros_like(dv_scr)

        def step(diagonal):
            qv = q_ref[...]
            kcat = jnp.concatenate([kv_ref[:, 0:QK_NOPE], kr_ref[...]], axis=1)
            s = lax.dot_general(qv, kcat, DIMS_NT, preferred_element_type=F32) * scale
            p = jnp.exp(s - jnp.tile(lse_ref[...], (1, rep)))
            if diagonal:
                p = jnp.where(_causal_mask(t), p, 0.0)
            dov = do_ref[...]
            dv_scr[...] += lax.dot_general(p.astype(BF16), dov, DIMS_TN, preferred_element_type=F32)
            dp = lax.dot_general(dov, kv_ref[:, QK_NOPE:QK_NOPE + V_HEAD], DIMS_NT, preferred_element_type=F32)
            ds = (p * (dp - jnp.tile(delta_ref[...], (1, rep))) * scale).astype(BF16)
            dk_scr[...] += lax.dot_general(ds, qv, DIMS_TN, preferred_element_type=F32)
            dq_scr[q_rows, :] += lax.dot_general(ds, kcat, DIMS_NN, preferred_element_type=F32)

        @pl.when(qi > ki)
        def _():
            step(False)

        @pl.when(qi == ki)
        def _():
            step(True)

        @pl.when(qi == nb - 1)
        def _():
            dkv_ref[...] = jnp.concatenate([dk_scr[:, 0:QK_NOPE], dv_scr[...]], axis=1).astype(BF16)
            dkr_scr[k_rows, :] += dk_scr[:, QK_NOPE:QK_PAD]

        @pl.when(ki == nb - 1)
        def _():
            dqv = dq_scr[q_rows, :]
            dq_ref[q_rows, :] = jnp.concatenate(
                [dqv[:, 0:QK_NOPE], _rope_t(dqv[:, QK_NOPE:QK_PAD], cos_ref[...], slo_ref[...], shi_ref[...])],
                axis=1).astype(BF16)

        @pl.when((h == heads - 1) & (ki == nb - 1) & (qi == nb - 1))
        def _():
            dkr_ref[...] = dkr_scr[...]
            if rider.finish is not None:
                rider.finish(r_in, r_out, r_sems)

    qmap = lambda h, ki, qi: (jnp.maximum(qi, ki), h)
    stat = pl.BlockSpec((None, t, LANES), lambda h, ki, qi: (h, jnp.maximum(qi, ki), 0))
    tab = pl.BlockSpec((t, LANES), lambda h, ki, qi: (qi, 0))
    res = pl.pallas_call(
        body, name="mla_attn_bwd", grid=(heads, nb, nb),
        in_specs=[pl.BlockSpec((t, QK_PAD), qmap),
                  pl.BlockSpec((t, QK_PAD), lambda h, ki, qi: (ki, h)),
                  pl.BlockSpec((t, LANES), lambda h, ki, qi: (ki, 0)),
                  stat,
                  pl.BlockSpec((t, V_HEAD), qmap),
                  stat,
                  tab, tab, tab] + [ANY] * n_ri,
        out_specs=[pl.BlockSpec((S, QK_PAD), lambda h, ki, qi: (0, h)),
                   pl.BlockSpec((t, QK_PAD), lambda h, ki, qi: (ki, h)),
                   pl.BlockSpec((S, LANES), lambda h, ki, qi: (0, 0))] + [ANY] * n_ro,
        out_shape=[_sds((S, heads * QK_PAD), BF16), _sds((S, heads * QK_PAD), BF16), _sds((S, LANES), F32)]
        + list(rider.out_shape),
        scratch_shapes=[pltpu.VMEM((S, QK_PAD), F32), pltpu.VMEM((t, QK_PAD), F32), pltpu.VMEM((t, V_HEAD), F32),
                        pltpu.VMEM((S, LANES), F32)] + list(rider.sems),
        input_output_aliases={9 + i: 3 + o for i, o in rider.aliases.items()},
        compiler_params=_cparams("arbitrary", "arbitrary", "arbitrary"))(q, kv, kr, delta, do, lse, *rope_tabs, *rider.operands)
    return res[0], res[1], res[2], res[3:]


def _causal_pairs(nb, q_major):
    if q_major:
        pairs = [(qi, ki) for qi in range(nb) for ki in range(qi + 1)]
    else:
        pairs = [(qi, ki) for ki in range(nb) for qi in range(ki, nb)]
    return jnp.array([p[0] for p in pairs], jnp.int32), jnp.array([p[1] for p in pairs], jnp.int32), len(pairs)


def _heads_per_step(heads):
    return 2 if heads % 2 == 0 else 1


def _attn_fwd_tri(q, kv, kr, heads, scale, rider=NO_RIDER):
    S = q.shape[0]
    t, nb = _attn_tiles(S)
    G = _heads_per_step(heads)
    q_tab, k_tab, n_pairs = _causal_pairs(nb, True)
    n_ri, n_ro = len(rider.operands), len(rider.out_shape)

    def body(qt_ref, kt_ref, *refs):
        q_ref, kv_ref, kr_ref = refs[:3]
        r_in = refs[3:3 + n_ri]
        o_ref, lse_ref = refs[3 + n_ri:5 + n_ri]
        r_out = refs[5 + n_ri:5 + n_ri + n_ro]
        m_scr, acc_scr = refs[5 + n_ri + n_ro:7 + n_ri + n_ro]
        r_sems = refs[7 + n_ri + n_ro:]
        h, p = pl.program_id(0), pl.program_id(1)
        qi, ki = qt_ref[p], kt_ref[p]

        if rider.start is not None:
            @pl.when((h == 0) & (p == 0))
            def _():
                rider.start(r_in, r_out, r_sems)

        @pl.when(ki == 0)
        def _():
            m_scr[...] = jnp.full_like(m_scr, NEG_INF)
            acc_scr[...] = jnp.zeros_like(acc_scr)

        def step(diagonal):
            ones = jnp.ones((t, LANES), BF16)
            for g in range(G):
                kcat = jnp.concatenate([kv_ref[:, g * QK_PAD:g * QK_PAD + QK_NOPE], kr_ref[...]], axis=1)
                vext = jnp.concatenate([kv_ref[:, g * QK_PAD + QK_NOPE:(g + 1) * QK_PAD], ones], axis=1)
                s = lax.dot_general(q_ref[:, g * QK_PAD:(g + 1) * QK_PAD], kcat, DIMS_NT,
                                    preferred_element_type=F32) * scale
                if diagonal:
                    s = jnp.where(_causal_mask(t), s, NEG_INF)
                m_prev = m_scr[g]
                m_new = jnp.maximum(m_prev, jnp.max(s, axis=-1, keepdims=True))
                alpha = jnp.exp(m_prev - m_new)
                pr = jnp.exp(s - jnp.tile(m_new, (1, t // LANES)))
                acc_scr[g] = jnp.tile(alpha, (1, 2)) * acc_scr[g] + lax.dot_general(
                    pr.astype(BF16), vext, DIMS_NN, preferred_element_type=F32)
                m_scr[g] = m_new

        @pl.when(ki < qi)
        def _():
            step(False)

        @pl.when(ki == qi)
        def _():
            step(True)
            for g in range(G):
                acc = acc_scr[g]
                o_ref[:, g * V_HEAD:(g + 1) * V_HEAD] = (acc[:, 0:V_HEAD] / acc[:, V_HEAD:2 * V_HEAD]).astype(BF16)
                lse_ref[g] = m_scr[g] + jnp.log(acc[:, V_HEAD:2 * V_HEAD])

        if rider.finish is not None:
            @pl.when((h == heads // G - 1) & (p == n_pairs - 1))
            def _():
                rider.finish(r_in, r_out, r_sems)

    res = pl.pallas_call(
        body, name="mla_attn_fwd",
        grid_spec=pltpu.PrefetchScalarGridSpec(
            num_scalar_prefetch=2, grid=(heads // G, n_pairs),
            in_specs=[pl.BlockSpec((t, G * QK_PAD), lambda h, p, qt, kt: (qt[p], h)),
                      pl.BlockSpec((t, G * QK_PAD), lambda h, p, qt, kt: (kt[p], h)),
                      pl.BlockSpec((t, LANES), lambda h, p, qt, kt: (kt[p], 0))] + [ANY] * n_ri,
            out_specs=[pl.BlockSpec((t, G * V_HEAD), lambda h, p, qt, kt: (qt[p], h)),
                       pl.BlockSpec((G, t, LANES), lambda h, p, qt, kt: (h, qt[p], 0))] + [ANY] * n_ro,
            scratch_shapes=[pltpu.VMEM((G, t, LANES), F32), pltpu.VMEM((G, t, 2 * V_HEAD), F32)] + list(rider.sems)),
        out_shape=[_sds((S, heads * V_HEAD), BF16), _sds((heads, S, LANES), F32)] + list(rider.out_shape),
        input_output_aliases={5 + i: 2 + o for i, o in rider.aliases.items()},
        compiler_params=_cparams("arbitrary", "arbitrary"))(q_tab, k_tab, q, kv, kr, *rider.operands)
    return res[0], res[1], res[2:]


def _attn_bwd_tri(q, kv, kr, delta, do, lse, rope_tabs, heads, scale, rider=NO_RIDER):
    S = q.shape[0]
    t, nb = _attn_tiles(S)
    G = _heads_per_step(heads)
    q_tab, k_tab, n_pairs = _causal_pairs(nb, False)
    n_ri, n_ro = len(rider.operands), len(rider.out_shape)
    rep = t // LANES

    def body(qt_ref, kt_ref, *refs):
        q_ref, kv_ref, kr_ref, delta_ref, do_ref, lse_ref, cos_ref, slo_ref, shi_ref = refs[:9]
        r_in = refs[9:9 + n_ri]
        dq_ref, dkv_ref, dkr_ref = refs[9 + n_ri:12 + n_ri]
        r_out = refs[12 + n_ri:12 + n_ri + n_ro]
        dq_scr, dk_scr, dv_scr, dkr_scr = refs[12 + n_ri + n_ro:16 + n_ri + n_ro]
        r_sems = refs[16 + n_ri + n_ro:]
        h, p = pl.program_id(0), pl.program_id(1)
        qi, ki = qt_ref[p], kt_ref[p]
        q_rows = pl.ds(pl.multiple_of(qi * t, t), t)
        k_rows = pl.ds(pl.multiple_of(ki * t, t), t)

        if rider.start is not None:
            @pl.when((h == 0) & (p == 0))
            def _():
                rider.start(r_in, r_out, r_sems)

        @pl.when(p == 0)
        def _():
            dq_scr[...] = jnp.zeros_like(dq_scr)

        @pl.when((h == 0) & (p == 0))
        def _():
            dkr_scr[...] = jnp.zeros_like(dkr_scr)

        @pl.when(qi == ki)
        def _():
            dk_scr[...] = jnp.zeros_like(dk_scr)
            dv_scr[...] = jnp.zeros_like(dv_scr)

        def step(diagonal):
            for g in range(G):
                qv = q_ref[:, g * QK_PAD:(g + 1) * QK_PAD]
                kcat = jnp.concatenate([kv_ref[:, g * QK_PAD:g * QK_PAD + QK_NOPE], kr_ref[...]], axis=1)
                s = lax.dot_general(qv, kcat, DIMS_NT, preferred_element_type=F32) * scale
                pr = jnp.exp(s - jnp.tile(lse_ref[g], (1, rep)))
                if diagonal:
                    pr = jnp.where(_causal_mask(t), pr, 0.0)
                dov = do_ref[:, g * V_HEAD:(g + 1) * V_HEAD]
                dv_scr[g] += lax.dot_general(pr.astype(BF16), dov, DIMS_TN, preferred_element_type=F32)
                dp = lax.dot_general(dov, kv_ref[:, g * QK_PAD + QK_NOPE:(g + 1) * QK_PAD], DIMS_NT,
                                     preferred_element_type=F32)
                ds = (pr * (dp - jnp.tile(delta_ref[g], (1, rep))) * scale).astype(BF16)
                dk_scr[g] += lax.dot_general(ds, qv, DIMS_TN, preferred_element_type=F32)
                dq_scr[q_rows, g * QK_PAD:(g + 1) * QK_PAD] += lax.dot_general(ds, kcat, DIMS_NN,
                                                                               preferred_element_type=F32)

        @pl.when(qi > ki)
        def _():
            step(False)

        @pl.when(qi == ki)
        def _():
            step(True)
            for g in range(G):
                dqv = dq_scr[q_rows, g * QK_PAD:(g + 1) * QK_PAD]
                dq_ref[q_rows, g * QK_PAD:(g + 1) * QK_PAD] = jnp.concatenate(
                    [dqv[:, 0:QK_NOPE], _rope_t(dqv[:, QK_NOPE:QK_PAD], cos_ref[...], slo_ref[...], shi_ref[...])],
                    axis=1).astype(BF16)

        @pl.when(qi == nb - 1)
        def _():
            for g in range(G):
                dkv_ref[:, g * QK_PAD:(g + 1) * QK_PAD] = jnp.concatenate(
                    [dk_scr[g][:, 0:QK_NOPE], dv_scr[g]], axis=1).astype(BF16)
                dkr_scr[k_rows, :] += dk_scr[g][:, QK_NOPE:QK_PAD]

        @pl.when((h == heads // G - 1) & (p == n_pairs - 1))
        def _():
            dkr_ref[...] = dkr_scr[...]
            if rider.finish is not None:
                rider.finish(r_in, r_out, r_sems)

    q_blk = lambda w: pl.BlockSpec((t, G * w), lambda h, p, qt, kt: (qt[p], h))
    stat = pl.BlockSpec((G, t, LANES), lambda h, p, qt, kt: (h, qt[p], 0))
    tab = pl.BlockSpec((t, LANES), lambda h, p, qt, kt: (kt[p], 0))
    res = pl.pallas_call(
        body, name="mla_attn_bwd",
        grid_spec=pltpu.PrefetchScalarGridSpec(
            num_scalar_prefetch=2, grid=(heads // G, n_pairs),
            in_specs=[q_blk(QK_PAD),
                      pl.BlockSpec((t, G * QK_PAD), lambda h, p, qt, kt: (kt[p], h)),
                      tab, stat, q_blk(V_HEAD), stat, tab, tab, tab] + [ANY] * n_ri,
            out_specs=[pl.BlockSpec((S, G * QK_PAD), lambda h, p, qt, kt: (0, h)),
                       pl.BlockSpec((t, G * QK_PAD), lambda h, p, qt, kt: (kt[p], h)),
                       pl.BlockSpec((S, LANES), lambda h, p, qt, kt: (0, 0))] + [ANY] * n_ro,
            scratch_shapes=[pltpu.VMEM((S, G * QK_PAD), F32), pltpu.VMEM((G, t, QK_PAD), F32),
                            pltpu.VMEM((G, t, V_HEAD), F32), pltpu.VMEM((S, LANES), F32)] + list(rider.sems)),
        out_shape=[_sds((S, heads * QK_PAD), BF16), _sds((S, heads * QK_PAD), BF16), _sds((S, LANES), F32)]
        + list(rider.out_shape),
        input_output_aliases={11 + i: 3 + o for i, o in rider.aliases.items()},
        compiler_params=_cparams("arbitrary", "arbitrary"))(q_tab, k_tab, q, kv, kr, delta, do, lse, *rope_tabs,
                                                            *rider.operands)
    return res[0], res[1], res[2], res[3:]


def _shift_down(z, n, rows):
    return jnp.where(rows >= n, pltpu.roll(z, n, 0), 0.0)


def _shift_up(z, n, rows, S):
    return jnp.where(rows < S - n, pltpu.roll(z, S - n, 0), 0.0)


def _conv_specs(S, tc):
    strip = lambda p: pl.BlockSpec((None, S, tc), lambda j: (p, 0, j))
    return strip(0), strip(1), strip(2), pl.BlockSpec((3, tc), lambda j: (0, j))


def _conv_fwd(proj3, w):
    _, S, D = proj3.shape
    tc = LANES

    def body(b_ref, c_ref, u_ref, w_ref, out_ref):
        z = c_ref[...] * u_ref[...]
        rows = lax.broadcasted_iota(jnp.int32, (S, tc), 0)
        zc = w_ref[0:1, :] * _shift_down(z, 2, rows) + w_ref[1:2, :] * _shift_down(z, 1, rows) + w_ref[2:3, :] * z
        out_ref[...] = (b_ref[...] * zc).astype(BF16)

    return pl.pallas_call(
        body, name="conv_fwd", grid=(D // tc,), in_specs=list(_conv_specs(S, tc)),
        out_specs=pl.BlockSpec((S, tc), lambda j: (0, j)), out_shape=_sds((S, D), BF16),
        compiler_params=_cparams("parallel"))(proj3, proj3, proj3, w)


def _conv_bwd(dbz, proj3, w):
    _, S, D = proj3.shape
    tc = LANES

    def body(d_ref, b_ref, c_ref, u_ref, w_ref, dp_ref, dw_ref):
        cv, uv, dv = c_ref[...], u_ref[...], d_ref[...]
        z = cv * uv
        rows = lax.broadcasted_iota(jnp.int32, (S, tc), 0)
        z1, z2 = _shift_down(z, 1, rows), _shift_down(z, 2, rows)
        zc = w_ref[0:1, :] * z2 + w_ref[1:2, :] * z1 + w_ref[2:3, :] * z
        dp_ref[0] = (dv * zc).astype(BF16)
        dzc = dv * b_ref[...]
        dz = w_ref[2:3, :] * dzc + w_ref[1:2, :] * _shift_up(dzc, 1, rows, S) + w_ref[0:1, :] * _shift_up(dzc, 2, rows, S)
        dp_ref[1] = (dz * uv).astype(BF16)
        dp_ref[2] = (dz * cv).astype(BF16)
        dw_ref[0:1, :] = jnp.sum(dzc * z2, axis=0, keepdims=True)
        dw_ref[1:2, :] = jnp.sum(dzc * z1, axis=0, keepdims=True)
        dw_ref[2:3, :] = jnp.sum(dzc * z, axis=0, keepdims=True)

    sb, sc_, su, sw = _conv_specs(S, tc)
    return pl.pallas_call(
        body, name="conv_bwd", grid=(D // tc,),
        in_specs=[pl.BlockSpec((S, tc), lambda j: (0, j)), sb, sc_, su, sw],
        out_specs=[pl.BlockSpec((3, S, tc), lambda j: (0, 0, j)), pl.BlockSpec((3, tc), lambda j: (0, j))],
        out_shape=[_sds((3, S, D), BF16), _sds((3, D), F32)],
        compiler_params=_cparams("parallel"))(dbz, proj3, proj3, proj3, w)


def _silu(c_all):
    def body(c_ref, o_ref):
        cv = c_ref[...]
        o_ref[...] = cv * (1.0 / (1.0 + jnp.exp(-cv)))

    vm = pl.BlockSpec(memory_space=pltpu.VMEM)
    return pl.pallas_call(body, name="cond_silu", in_specs=[vm], out_specs=vm, out_shape=_sds(c_all.shape, F32))(c_all)


def _mod_fwd(cond, w_mod, b_cols):
    L, D, ncol = w_mod.shape
    B = cond.shape[0]
    tk, tn = min(512, D), min(1024, ncol)
    nk = D // tk

    def body(c_ref, w_ref, b_ref, out_ref, acc):
        kk = pl.program_id(2)
        part = lax.dot_general(c_ref[...].astype(BF16), w_ref[...].astype(BF16), DIMS_NN, preferred_element_type=F32)

        @pl.when(kk == 0)
        def _():
            acc[...] = part

        @pl.when(kk > 0)
        def _():
            acc[...] += part

        @pl.when(kk == nk - 1)
        def _():
            out_ref[...] = acc[...] + b_ref[...]

    return pl.pallas_call(
        body, name="mod_fwd", grid=(L, ncol // tn, nk),
        in_specs=[pl.BlockSpec((B, tk), lambda l, j, k: (0, k)),
                  pl.BlockSpec((None, tk, tn), lambda l, j, k: (l, k, j)),
                  pl.BlockSpec((None, 1, tn), lambda l, j, k: (l, 0, j))],
        out_specs=pl.BlockSpec((None, B, tn), lambda l, j, k: (l, 0, j)),
        out_shape=_sds((L, B, ncol), F32),
        scratch_shapes=[pltpu.VMEM((B, tn), F32)],
        compiler_params=_cparams("parallel", "parallel", "arbitrary"))(cond, w_mod, b_cols)


def _adamw_math(w, g, m, v):
    m = ADAM_B1 * m + (1.0 - ADAM_B1) * g
    v = ADAM_B2 * v + (1.0 - ADAM_B2) * (g * g)
    m_hat = m / (1.0 - ADAM_B1 ** ADAM_STEP)
    v_hat = v / (1.0 - ADAM_B2 ** ADAM_STEP)
    delta = -ADAM_LR * (m_hat / (jnp.sqrt(v_hat) + ADAM_EPS) + ADAM_WD * w)
    return delta, m, v


def _adamw(name, w, g, m, v):
    shape = w.shape
    cols = shape[-1] if w.ndim <= 3 else shape[-2] * shape[-1]
    rows = w.size // cols
    w2, g2, m2, v2 = (t.reshape(rows, cols) for t in (w, g, m, v))
    tr = _row_tile(rows, cols * 4, limit=1024 * 1024, mult=8)
    spec = pl.BlockSpec((tr, cols), lambda i: (i, 0))

    def body(w_ref, g_ref, m_ref, v_ref, d_ref, nm_ref, nv_ref):
        d, nm, nv = _adamw_math(w_ref[...], g_ref[...], m_ref[...], v_ref[...])
        d_ref[...] = d
        nm_ref[...] = nm
        nv_ref[...] = nv

    outs = pl.pallas_call(body, name=name, grid=(rows // tr,), in_specs=[spec] * 4, out_specs=[spec] * 3,
                          out_shape=[_sds((rows, cols), F32)] * 3, compiler_params=_cparams("parallel"))(w2, g2, m2, v2)
    return tuple(t.reshape(shape) for t in outs)


def _adamw_mod(w, cond_t, dmod_cols, m, v):
    L, D, ncol = w.shape
    B = cond_t.shape[1]
    tr, tc = min(256, D), min(1024, ncol)
    blk = pl.BlockSpec((None, tr, tc), lambda l, i, j: (l, i, j))

    def body(w_ref, ct_ref, dm_ref, m_ref, v_ref, g_ref, d_ref, nm_ref, nv_ref):
        g = lax.dot_general(ct_ref[...], dm_ref[...], DIMS_NN, precision=lax.Precision.HIGHEST,
                            preferred_element_type=F32)
        d, nm, nv = _adamw_math(w_ref[...], g, m_ref[...], v_ref[...])
        g_ref[...] = g
        d_ref[...] = d
        nm_ref[...] = nm
        nv_ref[...] = nv

    return pl.pallas_call(
        body, name="adamw_w_mod", grid=(L, D // tr, ncol // tc),
        in_specs=[blk, pl.BlockSpec((tr, B), lambda l, i, j: (i, 0)),
                  pl.BlockSpec((None, B, tc), lambda l, i, j: (l, 0, j)), blk, blk],
        out_specs=[blk] * 4, out_shape=[_sds((L, D, ncol), F32)] * 4,
        compiler_params=_cparams("parallel", "parallel", "parallel"))(w, cond_t, dmod_cols, m, v)


def _cast_into_full(name, w, kind, k_idx):
    L, R, C = w.shape
    Rh = R // 2
    tr = _row_tile(Rh, C * 4)
    if kind == "row":
        out_shape = (L, N_CHIPS, 2, Rh, C)
        out_spec = pl.BlockSpec((None, None, None, tr, C), lambda l, h, i, k_ref: (l, k_ref[0], h, i, 0))
    else:
        out_shape = (L, 2, Rh, N_CHIPS * C)
        out_spec = pl.BlockSpec((None, None, tr, C), lambda l, h, i, k_ref: (l, h, i, k_ref[0]))

    def body(k_ref, w_ref, o_ref):
        o_ref[...] = w_ref[...].astype(BF16)

    return pl.pallas_call(
        body, name=name,
        grid_spec=pltpu.PrefetchScalarGridSpec(
            num_scalar_prefetch=1, grid=(L, 2, Rh // tr),
            in_specs=[pl.BlockSpec((None, None, tr, C), lambda l, h, i, k_ref: (l, h, i, 0))],
            out_specs=out_spec),
        out_shape=_sds(out_shape, BF16),
        compiler_params=_cparams("parallel", "parallel", "parallel"))(k_idx, w.reshape(L, 2, Rh, C))


def _pair_sum(name, g5, ra, c_idx):
    L, A, _, Rh, Cc = g5.shape
    tr = _row_tile(Rh, Cc * 4)

    def body(c_ref, g_ref, r_ref, o_ref):
        o_ref[...] = (g_ref[...].astype(F32) + r_ref[...].astype(F32)).astype(BF16)

    blk = pl.BlockSpec((None, None, tr, Cc), lambda l, a, i, c_ref: (l, a, i, 0))
    return pl.pallas_call(
        body, name=name,
        grid_spec=pltpu.PrefetchScalarGridSpec(
            num_scalar_prefetch=1, grid=(L, A, Rh // tr),
            in_specs=[pl.BlockSpec((None, None, None, tr, Cc), lambda l, a, i, c_ref: (l, a, c_ref[0], i, 0)), blk],
            out_specs=blk),
        out_shape=_sds((L, A, Rh, Cc), BF16),
        compiler_params=_cparams("parallel", "parallel", "parallel"))(c_idx, g5, ra)


def _chip_sum(name, p, rb, kc_idx, kind, layer=0, n_layers=1, prev=None):
    _, A, Rh, Cc = p.shape
    C = rb.shape[-1]
    tr = _row_tile(Rh, C * 4)
    if kind == "row":
        own = pl.BlockSpec((None, None, tr, C), lambda i, kc: (0, kc[0], i, 0))
    else:
        own = pl.BlockSpec((None, None, tr, C), lambda i, kc: (0, 0, i, kc[0]))
    peer = lambda j: pl.BlockSpec((None, None, tr, C), lambda i, kc: (j, 0, i, 0))

    def body(kc_ref, p_ref, r0_ref, r1_ref, r2_ref, *rest):
        o_ref = rest[-1]
        o_ref[...] = ((p_ref[...].astype(F32) + r0_ref[...].astype(F32)) + r1_ref[...].astype(F32)) + r2_ref[...].astype(F32)

    operands = [kc_idx, p, rb, rb, rb] + ([prev] if prev is not None else [])
    return pl.pallas_call(
        body, name=name,
        grid_spec=pltpu.PrefetchScalarGridSpec(
            num_scalar_prefetch=1, grid=(Rh // tr,),
            in_specs=[own, peer(0), peer(1), peer(2)] + ([ANY] if prev is not None else []),
            out_specs=pl.BlockSpec((None, None, tr, C), lambda i, kc: (layer, kc[1], i, 0))),
        out_shape=_sds((n_layers, 2, Rh, C), F32),
        input_output_aliases={5: 0} if prev is not None else {},
        compiler_params=_cparams("parallel"))(*operands)


def _mesh_place():
    x, y, c = lax.axis_index("x"), lax.axis_index("y"), lax.axis_index("c")
    chips = [(1 - x, y), (x, 1 - y), (1 - x, 1 - y)]
    return x, y, c, chips


def _remote(src, dst, send_sem, recv_sem, to):
    return pltpu.make_async_remote_copy(src_ref=src, dst_ref=dst, send_sem=send_sem, recv_sem=recv_sem,
                                        device_id=to, device_id_type=MESH_ID)


def _small_allgather(name, v, with_sum=False):
    R, N = v.shape

    def body(*refs):
        if with_sum:
            x_ref, out_ref, sum_ref, send_sems, recv_sems, local_sem = refs
        else:
            x_ref, out_ref, send_sems, recv_sems, local_sem = refs
        x, y, c, chips = _mesh_place()
        me, sibling = (x, y, c), (x, y, 1 - c)

        def rows(px, py, pc):
            return out_ref.at[pl.ds((4 * px + 2 * py + pc) * R, R), :]

        def copy(k, block, to, src=None):
            return _remote(rows(*block) if src is None else src, rows(*block), send_sems.at[k], recv_sems.at[k], to)

        mine = pltpu.make_async_copy(x_ref, rows(*me), local_sem)
        mine.start()
        first = [copy(0, me, sibling, src=x_ref)]
        first += [copy(1 + j, me, (*chip, c), src=x_ref) for j, chip in enumerate(chips)]
        for cp in first:
            cp.start()
        passed = [copy(4 + j, (*chip, c), sibling) for j, chip in enumerate(chips)]
        for j, chip in enumerate(chips):
            copy(1 + j, (*chip, c), me).wait_recv()
            passed[j].start()
        copy(0, sibling, me).wait_recv()
        for j, chip in enumerate(chips):
            copy(4 + j, (*chip, 1 - c), me).wait_recv()
        for cp in first + passed:
            cp.wait_send()
        mine.wait()
        if with_sum:
            total = out_ref[0:R, :]
            for p in range(1, 8):
                total = total + out_ref[p * R:(p + 1) * R, :]
            sum_ref[...] = total

    vm = pl.BlockSpec(memory_space=pltpu.VMEM)
    out_shape = [_sds((8 * R, N), F32)] + ([_sds((R, N), F32)] if with_sum else [])
    res = pl.pallas_call(
        body, name=name, out_shape=out_shape, in_specs=[vm], out_specs=[vm] * len(out_shape),
        scratch_shapes=[pltpu.SemaphoreType.DMA((7,)), pltpu.SemaphoreType.DMA((7,)), pltpu.SemaphoreType.DMA],
        compiler_params=pltpu.CompilerParams(vmem_limit_bytes=VMEM_LIMIT_BYTES))(v)
    return res if with_sum else res[0]


def _full_place(ref, kind, C, kk, half, layer=None):
    lead = slice(None) if layer is None else pl.ds(layer, 1)
    if kind == "row":
        return ref.at[lead, kk, half]
    return ref.at[lead, half, :, pl.ds(pl.multiple_of(kk * C, LANES), C)]


def _gather_rider(fulls, kinds, shard_cols, layers=None):
    n = len(fulls)
    layers = layers or [None] * n

    def copies(outs, sems):
        x, y, c, chips = _mesh_place()
        k = 2 * x + y
        place = lambda a, kk, half: _full_place(outs[a], kinds[a], shard_cols[a], kk, half, layers[a])
        copy = lambda a, j, ref, to: _remote(ref, ref, sems[0].at[6 * a + j], sems[1].at[6 * a + j], to)
        return (x, y, c), chips, k, place, copy

    def start(_, outs, sems):
        (x, y, c), chips, k, place, copy = copies(outs, sems)
        for j, chip in enumerate(chips):
            for a in range(n):
                copy(a, j, place(a, k, c), (*chip, c)).start()

    def finish(_, outs, sems):
        (x, y, c), chips, k, place, copy = copies(outs, sems)
        me, sibling = (x, y, c), (x, y, 1 - c)
        for j, chip in enumerate(chips):
            kj = 2 * chip[0] + chip[1]
            for a in range(n):
                copy(a, j, place(a, kj, c), me).wait_recv()
                copy(a, 3 + j, place(a, kj, c), sibling).start()
        for j, chip in enumerate(chips):
            kj = 2 * chip[0] + chip[1]
            for a in range(n):
                copy(a, 3 + j, place(a, kj, 1 - c), me).wait_recv()
        for j, chip in enumerate(chips):
            kj = 2 * chip[0] + chip[1]
            for a in range(n):
                copy(a, j, place(a, k, c), (*chip, c)).wait_send()
                copy(a, 3 + j, place(a, kj, c), sibling).wait_send()

    return Rider(tuple(fulls), tuple(_sds(f.shape, BF16) for f in fulls), {a: a for a in range(n)},
                 (pltpu.SemaphoreType.DMA((6 * n,)), pltpu.SemaphoreType.DMA((6 * n,))), start, finish)


def _scatter_rider(ps, kinds, shard_cols):
    n = len(ps)

    def copies(ins, outs, sems):
        x, y, c, chips = _mesh_place()
        cps = []
        for j, chip in enumerate(chips):
            kj = 2 * chip[0] + chip[1]
            for a in range(n):
                C = shard_cols[a]
                src = ins[a].at[:, kj] if kinds[a] == "row" else ins[a].at[:, 0, :, pl.ds(pl.multiple_of(kj * C, LANES), C)]
                cps.append(_remote(src, outs[a].at[j], sems[0].at[3 * a + j], sems[1].at[3 * a + j], (*chip, c)))
        return cps

    def start(ins, outs, sems):
        for cp in copies(ins, outs, sems):
            cp.start()

    def finish(ins, outs, sems):
        cps = copies(ins, outs, sems)
        for cp in cps:
            cp.wait_recv()
        for cp in cps:
            cp.wait_send()

    out_shape = tuple(_sds((3, p.shape[0], p.shape[2], C), BF16) for p, C in zip(ps, shard_cols))
    return Rider(tuple(ps), out_shape, {}, (pltpu.SemaphoreType.DMA((3 * n,)), pltpu.SemaphoreType.DMA((3 * n,))),
                 start, finish)


def _run_rider(name, rider):
    n_in, n_out = len(rider.operands), len(rider.out_shape)

    def body(*refs):
        ins, outs, sems = refs[:n_in], refs[n_in:n_in + n_out], refs[n_in + n_out:]
        rider.start(ins, outs, sems)
        rider.finish(ins, outs, sems)

    return pl.pallas_call(
        body, name=name, out_shape=list(rider.out_shape), in_specs=[ANY] * n_in, out_specs=[ANY] * n_out,
        input_output_aliases=dict(rider.aliases), scratch_shapes=list(rider.sems),
        compiler_params=pltpu.CompilerParams(vmem_limit_bytes=VMEM_LIMIT_BYTES))(*rider.operands)


def _pair_exchange(name, g5s):
    n = len(g5s)

    def body(*refs):
        ins, outs = refs[:n], refs[n:2 * n]
        send_sems, recv_sems = refs[2 * n:]
        x, y, c, _ = _mesh_place()
        cps = []
        for a in range(n):
            cp = _remote(ins[a].at[:, :, 1 - c], outs[a], send_sems.at[a], recv_sems.at[a], (x, y, 1 - c))
            cp.start()
            cps.append(cp)
        for cp in cps:
            cp.wait_recv()
        for cp in cps:
            cp.wait_send()

    out_shape = [_sds((g.shape[0], g.shape[1], g.shape[3], g.shape[4]), BF16) for g in g5s]
    return pl.pallas_call(
        body, name=name, out_shape=out_shape, in_specs=[ANY] * n, out_specs=[ANY] * n,
        scratch_shapes=[pltpu.SemaphoreType.DMA((n,)), pltpu.SemaphoreType.DMA((n,))],
        compiler_params=pltpu.CompilerParams(vmem_limit_bytes=VMEM_LIMIT_BYTES))(*g5s)


def _pair_share(fs):
    n = len(fs)

    def body(*refs):
        outs = refs[n:2 * n]
        send_sems, recv_sems = refs[2 * n:]
        x, y, c, _ = _mesh_place()
        cps = []
        for a in range(n):
            mine = outs[a].at[:, c]
            cp = _remote(mine, mine, send_sems.at[a], recv_sems.at[a], (x, y, 1 - c))
            cp.start()
            cps.append(cp)
        for a in range(n):
            theirs = outs[a].at[:, 1 - c]
            _remote(theirs, theirs, send_sems.at[a], recv_sems.at[a], (x, y, c)).wait_recv()
        for cp in cps:
            cp.wait_send()

    return pl.pallas_call(
        body, name="grad_pair_share", out_shape=[_sds(f.shape, F32) for f in fs],
        in_specs=[ANY] * n, out_specs=[ANY] * n, input_output_aliases={a: a for a in range(n)},
        scratch_shapes=[pltpu.SemaphoreType.DMA((n,)), pltpu.SemaphoreType.DMA((n,))],
        compiler_params=pltpu.CompilerParams(vmem_limit_bytes=VMEM_LIMIT_BYTES))(*fs)


def _pack_rows(parts, lane_mult=1024):
    flat = jnp.concatenate([p.reshape(-1).astype(F32) for p in parts])
    n = -(-flat.shape[0] // (8 * lane_mult)) * lane_mult
    return jnp.pad(flat, (0, 8 * n - flat.shape[0])).reshape(8, n)


def _relu2(acc):
    r = jnp.maximum(acc, 0.0)
    return r, r * r


def _times_2r(acc, r):
    return (acc * (2.0 * r.astype(F32)),)


def kernel(x, c, positions, w_mod, b_mod, norm_g, mla_w_in, mla_g_q, mla_g_kv, mla_w_uq, mla_w_ukv, mla_w_o, conv_w_in, conv_w, conv_w_out, mlp_w_up, mlp_w_down, loss_target, m_w_mod, m_b_mod, m_norm_g, m_mla_w_in, m_mla_g_q, m_mla_g_kv, m_mla_w_uq, m_mla_w_ukv, m_mla_w_o, m_conv_w_in, m_conv_w, m_conv_w_out, m_mlp_w_up, m_mlp_w_down, v_w_mod, v_b_mod, v_norm_g, v_mla_w_in, v_mla_g_q, v_mla_g_kv, v_mla_w_uq, v_mla_w_ukv, v_mla_w_o, v_conv_w_in, v_conv_w, v_conv_w_out, v_mlp_w_up, v_mlp_w_down):
    S, D = x.shape[1], x.shape[2]
    Dq = D // N_CHIPS
    ncol = w_mod.shape[2]
    n_mod = N_CHIPS * ncol // D
    F = mlp_w_up.shape[2] * N_CHIPS
    lat_dim = mla_w_in.shape[2]
    rank = mla_g_q.shape[1]
    H = mla_w_uq.shape[2]
    d_qk = mla_w_uq.shape[3]
    assert mla_g_kv.shape[1] == rank and lat_dim == 2 * rank + QK_ROPE and d_qk == QK_NOPE + QK_ROPE
    assert mla_w_ukv.shape[3] == QK_NOPE + V_HEAD and x.shape[0] == 1 and n_mod == 6
    assert norm_g.shape[0] == 2 and mla_w_in.shape[0] == 1 and conv_w_in.shape[0] == 1
    lat_pad = 2 * rank + LANES
    scale = float(d_qk) ** -0.5

    xi, yi, ci = lax.axis_index("x"), lax.axis_index("y"), lax.axis_index("c")
    chip = 2 * xi + yi
    dev = 2 * chip + ci
    c_idx = jnp.reshape(ci, (1,)).astype(jnp.int32)
    k_idx = jnp.reshape(chip, (1,)).astype(jnp.int32)

    n1 = D + 2 * D + 3 * Dq
    g1 = _small_allgather("gather_small_inputs", _pack_rows([c, norm_g, conv_w])).reshape(8, -1)
    c_all = g1[:, :D]
    by_chip = g1[0::2]
    norm_full = jnp.concatenate([by_chip[kk, D:3 * D].reshape(2, 4, Dq) for kk in range(N_CHIPS)], axis=-1)
    convw_full = jnp.concatenate([by_chip[kk, 3 * D:n1].reshape(3, Dq) for kk in range(N_CHIPS)], axis=-1)

    b_cols = lax.dynamic_slice(b_mod, (0, chip * ncol), (2, ncol)).reshape(2, 1, ncol)
    cond_all = _silu(c_all)
    mod_cols = _mod_fwd(cond_all, w_mod, b_cols)
    g2 = _small_allgather("gather_mod", _pack_rows([mod_cols]))
    g2 = g2.reshape(8, -1)[0::2, :2 * 8 * ncol].reshape(N_CHIPS, 2, 8, ncol)
    mod_all = jnp.transpose(g2, (2, 1, 0, 3)).reshape(8, 2, n_mod * D)
    mod_me = lax.dynamic_index_in_dim(mod_all, dev, axis=0, keepdims=False)
    mods = [[mod_me[l, i * D:(i + 1) * D].reshape(1, D) for i in range(n_mod)] for l in range(2)]
    ng = [[norm_full[l, i].reshape(1, D) for i in range(4)] for l in range(2)]

    pos = positions[0].astype(F32)
    inv_freq = ROPE_THETA ** (-jnp.arange(0, QK_ROPE, 2, dtype=F32) / QK_ROPE)
    ang = pos[:, None] * inv_freq
    cos, sin = jnp.cos(ang), jnp.sin(ang)
    zero = jnp.zeros_like(cos)
    rope_tabs = (jnp.concatenate([cos, cos, zero, zero], axis=1),
                 jnp.concatenate([-sin, zero, zero, zero], axis=1),
                 jnp.concatenate([zero, sin, zero, zero], axis=1))

    weights = [("mla_w_in", mla_w_in, "row"), ("mla_w_uq", mla_w_uq.reshape(1, rank // N_CHIPS, H * d_qk), "row"),
               ("mla_w_ukv", mla_w_ukv.reshape(1, rank // N_CHIPS, H * QK_PAD), "row"), ("mla_w_o", mla_w_o, "row"),
               ("conv_w_in", conv_w_in, "col"), ("conv_w_out", conv_w_out, "row"),
               ("mlp_w_up", mlp_w_up, "col"), ("mlp_w_down", mlp_w_down, "row")]
    kinds = [k for _, _, k in weights]
    shard_shapes = [w.shape for _, w, _ in weights]
    shard_cols = [s[2] for s in shard_shapes]
    casted = [_cast_into_full("cast_" + nm, w, kind, k_idx) for nm, w, kind in weights]
    W_IN, W_UQ, W_UKV, W_O, W_CIN, W_COUT, W_UP, W_DOWN = range(8)
    mla_idx = [W_IN, W_UQ, W_UKV, W_O]

    def view(i, buf):
        L, R, C = shard_shapes[i]
        return buf.reshape((L, N_CHIPS * R, C) if kinds[i] == "row" else (L, R, N_CHIPS * C))

    def gather_of(bufs, idx, layers=None):
        return _gather_rider(bufs, [kinds[i] for i in idx], [shard_cols[i] for i in idx], layers)

    def scatter_of(ps, idx):
        return _scatter_rider(ps, [kinds[i] for i in idx], [shard_cols[i] for i in idx])

    def pair_sums(tag, items):
        g5s = []
        for _, i, g in items:
            _, R, C = shard_shapes[i]
            g5s.append(g.reshape((1, N_CHIPS, 2, R // 2, C) if kinds[i] == "row" else (1, 1, 2, R // 2, N_CHIPS * C)))
        ras = _pair_exchange("grad_pair_exchange_" + tag, g5s)
        return [_pair_sum("pair_sum_" + nm, g5, ra, c_idx) for (nm, _, _), g5, ra in zip(items, g5s, ras)]

    got = _run_rider("gather_weights_mla", gather_of([casted[i] for i in mla_idx], mla_idx))
    w_in_p = jnp.pad(view(W_IN, got[0])[0], ((0, 0), (0, lat_pad - lat_dim)))
    w_q_p = jnp.pad(view(W_UQ, got[1])[0].reshape(rank, H, d_qk), ((0, 0), (0, 0), (0, QK_PAD - d_qk))).reshape(rank, H * QK_PAD)
    w_ukv, w_o = view(W_UKV, got[2])[0], view(W_O, got[3])[0]
    HV = H * V_HEAD

    def layer_b(l, transposed):
        if transposed:
            return lambda tm, tn, tk: pl.BlockSpec((None, tn, tk), lambda i, j, k: (l, j, k))
        return lambda tm, tn, tk: pl.BlockSpec((None, tk, tn), lambda i, j, k: (l, k, j))

    def mlp_up(tag, l, h, w, rider=NO_RIDER):
        return _mm("mlp_up_" + tag, h, w, "nn", S, F, D, [_sds((S, F), BF16)] * 2, epilogue=_relu2,
                   b_spec=layer_b(l, False), rider=rider)

    def mlp_down(tag, l, a2, w, rider=NO_RIDER):
        return _mm("mlp_down_" + tag, a2, w, "nn", S, D, F, [_sds((S, D), F32)], b_spec=layer_b(l, False), rider=rider)

    def mlp_bwd(tag, l, h, r, a2, dy, rider=NO_RIDER):
        res = _mm("mlp_down_dx_" + tag, dy, w_down, "nt", S, F, D, [_sds((S, F), BF16)], epilogue=_times_2r,
                  b_spec=layer_b(l, True), rider=rider,
                  extras=[(r, lambda tm, tn, tk: pl.BlockSpec((tm, tn), lambda i, j, k: (i, j)))])
        (da,), carried = res if rider.start is not None else (res, ())
        (dw_down,) = _mm("mlp_down_dw_" + tag, a2, dy, "tn", F, D, S, [_sds((F, D), BF16)])
        (dh,) = _mm("mlp_up_dx_" + tag, da, w_up, "nt", S, D, F, [_sds((S, D), F32)], b_spec=layer_b(l, True))
        (dw_up,) = _mm("mlp_up_dw_" + tag, h, da, "tn", D, F, S, [_sds((D, F), BF16)])
        return dh, dw_up, dw_down, carried

    x0 = x[0]
    sh1, sc1, gt1, sh2, sc2, gt2 = mods[0]
    (h1,) = _fwd_boundary("fwd_boundary_0", x0, None, None, None, ng[0][0], sc1, sh1)
    (lat,) = _mm("mla_in", h1, w_in_p, "nn", S, lat_pad, D, [_sds((S, lat_pad), F32)], tn=lat_pad)
    cq, ckv, kr = _latent_fwd(lat, mla_g_q, mla_g_kv, rope_tabs, rank)

    def rope_q(acc, cos_p, sin_lo, sin_hi):
        parts = []
        for hh in range(acc.shape[1] // QK_PAD):
            parts.append(acc[:, hh * QK_PAD:hh * QK_PAD + QK_NOPE])
            parts.append(_rope(acc[:, hh * QK_PAD + QK_NOPE:(hh + 1) * QK_PAD], cos_p, sin_lo, sin_hi))
        return (jnp.concatenate(parts, axis=1),)

    tab_extra = lambda tm, tn, tk: pl.BlockSpec((tm, LANES), lambda i, j, k: (i, 0))
    (q,) = _mm("mla_q", cq, w_q_p, "nn", S, H * QK_PAD, rank, [_sds((S, H * QK_PAD), BF16)], epilogue=rope_q,
               extras=[(t, tab_extra) for t in rope_tabs], tn=2 * QK_PAD)
    (kv,) = _mm("mla_kv", ckv, w_ukv, "nn", S, H * QK_PAD, rank, [_sds((S, H * QK_PAD), BF16)])
    rest_idx = [W_COUT, W_UP, W_DOWN]
    o, lse, (cout_buf, up_buf, down_buf) = _attn_fwd_tri(
        q, kv, kr, H, scale, gather_of([casted[i] for i in rest_idx], rest_idx, [None, 0, 0]))
    (y1,) = _mm("mla_out", o, w_o, "nn", S, D, HV, [_sds((S, D), F32)])
    x1, h2 = _fwd_boundary("fwd_boundary_1", x0, y1, gt1, ng[0][1], ng[0][2], sc2, sh2)
    (r2, a2), (cin_buf,) = mlp_up("0", 0, h2, view(W_UP, up_buf), gather_of([casted[W_CIN]], [W_CIN]))
    (y2,), (up_buf,) = mlp_down("0", 0, a2, view(W_DOWN, down_buf), gather_of([up_buf], [W_UP], [1]))
    w_cin, w_cout, w_up = view(W_CIN, cin_buf)[0], view(W_COUT, cout_buf)[0], view(W_UP, up_buf)

    sh1b, sc1b, gt1b, sh2b, sc2b, gt2b = mods[1]
    x2, h3 = _fwd_boundary("fwd_boundary_2", x1, y2, gt2, ng[0][3], ng[1][0], sc1b, sh1b)
    nD = lambda tn: D // tn
    (proj3,) = _mm("conv_in", h3, w_cin, "nn", S, 3 * D, D, [_sds((3, S, D), F32)], tn=min(1024, D),
                   out_specs=[lambda tm, tn, tk: pl.BlockSpec((None, tm, tn), lambda i, j, k: (j // nD(tn), i, j % nD(tn)))])
    bz = _conv_fwd(proj3, convw_full)
    (y3,) = _mm("conv_out", bz, w_cout, "nn", S, D, D, [_sds((S, D), F32)])
    x3, h4 = _fwd_boundary("fwd_boundary_3", x2, y3, gt1b, ng[1][1], ng[1][2], sc2b, sh2b)
    (r4, a4), (down_buf,) = mlp_up("1", 1, h4, w_up, gather_of([down_buf], [W_DOWN], [1]))
    w_down = view(W_DOWN, down_buf)
    (y4,) = mlp_down("1", 1, a4, w_down)

    dx4, dy4, sums_l, loss_acc = _loss_boundary("loss_boundary", x3, y4, gt2b, ng[1][3], loss_target[0])
    loss = lax.psum(loss_acc[0, 0], ("x", "y", "c"))

    dh4, dw_up1, dw_down1, _ = mlp_bwd("1", 1, h4, r4, a4, dy4)
    ps_up1, ps_down1 = pair_sums("mlp_1", [("mlp_w_up_1", W_UP, dw_up1), ("mlp_w_down_1", W_DOWN, dw_down1)])
    dx3, dy3, sums_3 = _bwd_boundary("bwd_boundary_3", dx4, dh4, x3, y3, gt1b, ng[1][1], ng[1][2], sc2b)

    (dbz,) = _mm("conv_out_dx", dy3, w_cout, "nt", S, D, D, [_sds((S, D), F32)])
    (dw_cout,) = _mm("conv_out_dw", bz, dy3, "tn", D, D, S, [_sds((D, D), BF16)])
    dproj3, dconvw = _conv_bwd(dbz, proj3, convw_full)
    (dh3,), (rb_up1,) = _mm(
        "conv_in_dx", dproj3, w_cin, "nt", S, D, 3 * D, [_sds((S, D), F32)], tk=min(1024, D), rider=scatter_of([ps_up1], [W_UP]),
        a_spec=lambda tm, tn, tk: pl.BlockSpec((None, tm, tk), lambda i, j, k: (k // (D // tk), i, k % (D // tk))))
    (dw_cin,), (rb_down1,) = _mm(
        "conv_in_dw", h3, dproj3, "tn", D, 3 * D, S, [_sds((D, 3 * D), BF16)], tn=min(1024, D),
        rider=scatter_of([ps_down1], [W_DOWN]),
        b_spec=lambda tm, tn, tk: pl.BlockSpec((None, tk, tn), lambda i, j, k: (j // nD(tn), k, j % nD(tn))))
    ps_cin, ps_cout = pair_sums("conv", [("conv_w_in", W_CIN, dw_cin), ("conv_w_out", W_COUT, dw_cout)])
    dx2, dy2, sums_2 = _bwd_boundary("bwd_boundary_2", dx3, dh3, x2, y2, gt2, ng[0][3], ng[1][0], sc1b)

    dh2, dw_up0, dw_down0, _ = mlp_bwd("0", 0, h2, r2, a2, dy2)
    ps_up0, ps_down0 = pair_sums("mlp_0", [("mlp_w_up_0", W_UP, dw_up0), ("mlp_w_down_0", W_DOWN, dw_down0)])
    dx1, dy1, sums_1 = _bwd_boundary("bwd_boundary_1", dx2, dh2, x1, y1, gt1, ng[0][1], ng[0][2], sc2)

    (do,) = _mm("mla_out_dx", dy1, w_o, "nt", S, HV, D, [_sds((S, HV), BF16)])
    (dw_o,) = _mm("mla_out_dw", o, dy1, "tn", HV, D, S, [_sds((HV, D), BF16)])
    dq, dkv, dkr, (rb_up0, rb_down0, rb_cin, rb_cout) = _attn_bwd_tri(
        q, kv, kr, _attn_delta(o, do, H), do, lse, rope_tabs, H, scale,
        scatter_of([ps_up0, ps_down0, ps_cin, ps_cout], [W_UP, W_DOWN, W_CIN, W_COUT]))
    (dcq,) = _mm("mla_q_dx", dq, w_q_p, "nt", S, rank, H * QK_PAD, [_sds((S, rank), F32)])
    (dw_q_p,) = _mm("mla_q_dw", cq, dq, "tn", rank, H * QK_PAD, S, [_sds((rank, H * QK_PAD), BF16)])
    (dckv,) = _mm("mla_kv_dx", dkv, w_ukv, "nt", S, rank, H * QK_PAD, [_sds((S, rank), F32)])
    (dw_ukv,) = _mm("mla_kv_dw", ckv, dkv, "tn", rank, H * QK_PAD, S, [_sds((rank, H * QK_PAD), BF16)])
    dlat, sums_lat = _latent_bwd(lat, dcq, dckv, dkr, mla_g_q, mla_g_kv, rope_tabs, rank)
    (dh1,) = _mm("mla_in_dx", dlat, w_in_p, "nt", S, D, lat_pad, [_sds((S, D), F32)])
    (dw_in_p,) = _mm("mla_in_dw", h1, dlat, "tn", D, lat_pad, S, [_sds((D, lat_pad), BF16)], tn=lat_pad)
    grad_x, sums_0 = _bwd_boundary("bwd_boundary_0", dx1, dh1, x0, None, None, None, ng[0][0], sc1)

    dmod0 = [sums_0[0], sums_0[1], sums_1[3], sums_1[0], sums_1[1], sums_2[3]]
    dmod1 = [sums_2[0], sums_2[1], sums_3[3], sums_3[0], sums_3[1], sums_l[3]]
    dng0 = [sums_0[2], sums_1[4], sums_1[2], sums_2[4]]
    dng1 = [sums_2[2], sums_3[4], sums_3[2], sums_l[4]]
    small = _pack_rows(dmod0 + dmod1 + dng0 + dng1 + [sums_lat[0], sums_lat[1], dconvw], lane_mult=LANES)
    gathered, total = _small_allgather("gather_small_grads", small, with_sum=True)
    n_dm = 2 * n_mod * D
    dmod_all = gathered.reshape(8, -1)[:, :n_dm].reshape(8, 2, n_mod * D)
    total = total.reshape(-1)
    g_b_mod = total[:n_dm].reshape(2, n_mod * D)
    g_norm = lax.dynamic_slice(total[n_dm:n_dm + 8 * D].reshape(2, 4, D), (0, 0, chip * Dq), (2, 4, Dq))
    off = n_dm + 8 * D
    g_gq = total[off:off + rank].reshape(1, rank)
    g_gkv = total[off + rank:off + 2 * rank].reshape(1, rank)
    off += 2 * rank
    g_convw = lax.dynamic_slice(total[off:off + 3 * D].reshape(1, 3, D), (0, 0, chip * Dq), (1, 3, Dq))

    dw_mla = [dw_in_p[:, :lat_dim], dw_q_p.reshape(rank, H, QK_PAD)[:, :, :d_qk].reshape(rank, H * d_qk), dw_ukv, dw_o]
    ps_mla = pair_sums("mla", [(weights[i][0], i, g) for i, g in zip(mla_idx, dw_mla)])
    rbs_mla = _run_rider("grad_chip_scatter_mla", scatter_of(ps_mla, mla_idx))
    kc_idx = jnp.stack([chip, ci]).astype(jnp.int32)
    fs = [_chip_sum("chip_sum_" + weights[i][0], p, rb, kc_idx, kinds[i])
          for i, p, rb in zip(mla_idx + [W_CIN, W_COUT], ps_mla + [ps_cin, ps_cout], list(rbs_mla) + [rb_cin, rb_cout])]
    for i, (p1, r1), (p0, r0) in [(W_UP, (ps_up1, rb_up1), (ps_up0, rb_up0)), (W_DOWN, (ps_down1, rb_down1), (ps_down0, rb_down0))]:
        f = _chip_sum("chip_sum_" + weights[i][0] + "_1", p1, r1, kc_idx, kinds[i], layer=1, n_layers=2)
        fs.append(_chip_sum("chip_sum_" + weights[i][0] + "_0", p0, r0, kc_idx, kinds[i], layer=0, n_layers=2, prev=f))
    finals = _pair_share(fs)
    orig = [mla_w_in, mla_w_uq, mla_w_ukv, mla_w_o, conv_w_in, conv_w_out, mlp_w_up, mlp_w_down]
    big_grads = [f.reshape(w.shape) for f, w in zip(finals, orig)]

    dmod_cols = jnp.transpose(lax.dynamic_slice(dmod_all.reshape(8, 2, N_CHIPS, ncol), (0, 0, chip, 0), (8, 2, 1, ncol))
                              .reshape(8, 2, ncol), (1, 0, 2))
    g_w_mod, d_w_mod, nm_w_mod, nv_w_mod = _adamw_mod(w_mod, cond_all.T, dmod_cols, m_w_mod, v_w_mod)

    names = ["b_mod", "norm_g", "mla_w_in", "mla_g_q", "mla_g_kv", "mla_w_uq", "mla_w_ukv", "mla_w_o",
             "conv_w_in", "conv_w", "conv_w_out", "mlp_w_up", "mlp_w_down"]
    ws = [b_mod, norm_g, mla_w_in, mla_g_q, mla_g_kv, mla_w_uq, mla_w_ukv, mla_w_o, conv_w_in, conv_w, conv_w_out,
          mlp_w_up, mlp_w_down]
    ms = [m_b_mod, m_norm_g, m_mla_w_in, m_mla_g_q, m_mla_g_kv, m_mla_w_uq, m_mla_w_ukv, m_mla_w_o, m_conv_w_in,
          m_conv_w, m_conv_w_out, m_mlp_w_up, m_mlp_w_down]
    vs = [v_b_mod, v_norm_g, v_mla_w_in, v_mla_g_q, v_mla_g_kv, v_mla_w_uq, v_mla_w_ukv, v_mla_w_o, v_conv_w_in,
          v_conv_w, v_conv_w_out, v_mlp_w_up, v_mlp_w_down]
    gs = [g_b_mod, g_norm, big_grads[0], g_gq, g_gkv, big_grads[1], big_grads[2], big_grads[3], big_grads[4],
          g_convw, big_grads[5], big_grads[6], big_grads[7]]
    grads, deltas, new_ms, new_vs = [g_w_mod], [d_w_mod], [nm_w_mod], [nv_w_mod]
    for nm, w, g, m, v in zip(names, ws, gs, ms, vs):
        d, nm_, nv_ = _adamw("adamw_" + nm, w, g, m, v)
        grads.append(g)
        deltas.append(d)
        new_ms.append(nm_)
        new_vs.append(nv_)
    return (loss, grad_x[None], *grads, *deltas, *new_ms, *new_vs)
```

```python
from typing import NamedTuple

import jax
import jax.numpy as jnp
from jax import lax
from jax.experimental import pallas as pl
from jax.experimental.pallas import tpu as pltpu

F32 = jnp.float32
BF16 = jnp.bfloat16
NORM_EPS = 1e-6
ROPE_THETA = 10000.0
QK_NOPE = 128
QK_ROPE = 64
V_HEAD = 128
LANES = 128
QK_PAD = QK_NOPE + LANES
ADAM_LR, ADAM_B1, ADAM_B2, ADAM_EPS, ADAM_WD, ADAM_STEP = 0.001, 0.9, 0.999, 1e-08, 0.01, 10
VMEM_LIMIT_BYTES = 48 * 1024 * 1024
N_CHIPS = 4
MESH_ID = pl.DeviceIdType.MESH
ANY = pl.BlockSpec(memory_space=pl.ANY)
NEG_INF = float("-inf")

DIMS_NN = (((1,), (0,)), ((), ()))
DIMS_NT = (((1,), (1,)), ((), ()))
DIMS_TN = (((0,), (0,)), ((), ()))


def _cparams(*sem):
    return pltpu.CompilerParams(dimension_semantics=sem, vmem_limit_bytes=VMEM_LIMIT_BYTES)


def _row_tile(rows, row_bytes, limit=2 * 1024 * 1024, mult=16):
    if rows * row_bytes <= limit or rows % mult:
        return rows
    best = mult
    t = mult
    while t <= rows:
        if rows % t == 0 and t * row_bytes <= limit:
            best = t
        t += mult
    return best


def _rms(v):
    return lax.rsqrt(jnp.mean(v * v, axis=-1, keepdims=True) + NORM_EPS)


class Rider(NamedTuple):
    operands: tuple
    out_shape: tuple
    aliases: dict
    sems: tuple
    start: object
    finish: object


NO_RIDER = Rider((), (), {}, (), None, None)


def _mm(name, a, b, mode, M, N, K, outs, *, a_spec=None, b_spec=None, out_specs=None, epilogue=None,
        extras=(), rider=NO_RIDER, tm=1024, tn=1024, tk=2048):
    tm, tn, tk = min(tm, M), min(tn, N), min(tk, K)
    assert M % tm == 0 and N % tn == 0 and K % tk == 0, (name, M, N, K)
    nk = K // tk
    if a_spec is None:
        a_spec = {"nn": pl.BlockSpec((tm, tk), lambda i, j, k: (i, k)),
                  "nt": pl.BlockSpec((tm, tk), lambda i, j, k: (i, k)),
                  "tn": pl.BlockSpec((tk, tm), lambda i, j, k: (k, i))}[mode]
    else:
        a_spec = a_spec(tm, tn, tk)
    if b_spec is None:
        b_spec = {"nn": pl.BlockSpec((tk, tn), lambda i, j, k: (k, j)),
                  "nt": pl.BlockSpec((tn, tk), lambda i, j, k: (j, k)),
                  "tn": pl.BlockSpec((tk, tn), lambda i, j, k: (k, j))}[mode]
    else:
        b_spec = b_spec(tm, tn, tk)
    if out_specs is None:
        out_specs = [pl.BlockSpec((tm, tn), lambda i, j, k: (i, j)) for _ in outs]
    else:
        out_specs = [s(tm, tn, tk) for s in out_specs]
    dims = {"nn": DIMS_NN, "nt": DIMS_NT, "tn": DIMS_TN}[mode]
    ne, no = len(extras), len(outs)
    n_ri, n_ro = len(rider.operands), len(rider.out_shape)
    grid = (M // tm, N // tn, nk)

    def body(*refs):
        a_ref, b_ref = refs[0], refs[1]
        ex = refs[2:2 + ne]
        r_in = refs[2 + ne:2 + ne + n_ri]
        o = refs[2 + ne + n_ri:2 + ne + n_ri + no]
        r_out = refs[2 + ne + n_ri + no:2 + ne + n_ri + no + n_ro]
        scratch = refs[2 + ne + n_ri + no + n_ro:]
        r_sems = scratch[1:] if nk > 1 else scratch
        ii, jj, kk = pl.program_id(0), pl.program_id(1), pl.program_id(2)

        if rider.start is not None:
            @pl.when((ii == 0) & (jj == 0) & (kk == 0))
            def _():
                rider.start(r_in, r_out, r_sems)

        part = lax.dot_general(a_ref[...].astype(BF16), b_ref[...].astype(BF16), dims,
                               preferred_element_type=F32)

        def finish(total):
            vals = epilogue(total, *[e[...] for e in ex]) if epilogue is not None else (total,)
            for r, v in zip(o, vals):
                r[...] = v.astype(r.dtype)

        if nk == 1:
            finish(part)
        else:
            acc = scratch[0]

            @pl.when(kk == 0)
            def _():
                acc[...] = part

            @pl.when(kk > 0)
            def _():
                acc[...] += part

            @pl.when(kk == nk - 1)
            def _():
                finish(acc[...])

        if rider.finish is not None:
            @pl.when((ii == grid[0] - 1) & (jj == grid[1] - 1) & (kk == nk - 1))
            def _():
                rider.finish(r_in, r_out, r_sems)

    operands = [a, b] + [e[0] for e in extras] + list(rider.operands)
    in_specs = [a_spec, b_spec] + [e[1](tm, tn, tk) for e in extras] + [ANY] * n_ri
    hosted = rider.start is not None
    res = pl.pallas_call(
        body, name=name, grid=grid,
        in_specs=in_specs, out_specs=out_specs + [ANY] * n_ro, out_shape=list(outs) + list(rider.out_shape),
        scratch_shapes=([pltpu.VMEM((tm, tn), F32)] if nk > 1 else []) + list(rider.sems),
        input_output_aliases={2 + ne + i: no + r for i, r in rider.aliases.items()},
        compiler_params=_cparams(*(("arbitrary",) * 3 if hosted else ("parallel", "parallel", "arbitrary"))),
    )(*operands)
    return (res[:no], res[no:]) if hosted else res


def _sds(shape, dtype):
    return jax.ShapeDtypeStruct(tuple(shape), dtype)


def _rope(t, cos_p, sin_lo, sin_hi):
    return t * cos_p + pltpu.roll(t, LANES - QK_ROPE // 2, 1) * sin_lo + pltpu.roll(t, QK_ROPE // 2, 1) * sin_hi


def _rope_t(d, cos_p, sin_lo, sin_hi):
    return d * cos_p + pltpu.roll(d * sin_lo, QK_ROPE // 2, 1) + pltpu.roll(d * sin_hi, LANES - QK_ROPE // 2, 1)


def _vec_spec(d):
    return pl.BlockSpec((1, d), lambda i: (0, 0))


def _fwd_boundary(name, x_prev, y, gate, ng_post, ng_pre, sc, sh):
    S, D = x_prev.shape
    ts = min(256, S)
    has_y = y is not None
    row = pl.BlockSpec((ts, D), lambda i: (i, 0))

    def body(*refs):
        if has_y:
            x_ref, y_ref, g_ref, ngp_ref, ngn_ref, sc_ref, sh_ref, xo_ref, h_ref = refs
            yv = y_ref[...]
            xn = x_ref[...] + g_ref[...] * (yv * _rms(yv) * ngp_ref[...])
            xo_ref[...] = xn
        else:
            x_ref, ngn_ref, sc_ref, sh_ref, h_ref = refs
            xn = x_ref[...]
        hn = xn * _rms(xn) * ngn_ref[...]
        h_ref[...] = (hn * (1.0 + sc_ref[...]) + sh_ref[...]).astype(BF16)

    vec = _vec_spec(D)
    if has_y:
        operands = (x_prev, y, gate, ng_post, ng_pre, sc, sh)
        in_specs = [row, row, vec, vec, vec, vec, vec]
        out_shape = [_sds((S, D), F32), _sds((S, D), BF16)]
        out_specs = [row, row]
    else:
        operands = (x_prev, ng_pre, sc, sh)
        in_specs = [row, vec, vec, vec]
        out_shape = [_sds((S, D), BF16)]
        out_specs = [row]
    return pl.pallas_call(body, name=name, grid=(S // ts,), in_specs=in_specs, out_specs=out_specs,
                          out_shape=out_shape, compiler_params=_cparams("parallel"))(*operands)


def _acc_rows(sums_ref, rows):
    for r, v in rows:
        sums_ref[r:r + 1, :] += jnp.sum(v, axis=0, keepdims=True)


def _post_norm_bwd(dxt, yv, gate, ng_post, sums_ref, dy_ref):
    r1 = _rms(yv)
    yhat = yv * r1
    dn = dxt * gate
    u = dn * ng_post
    dy = r1 * (u - yhat * jnp.mean(u * yhat, axis=-1, keepdims=True))
    dy_ref[...] = dy.astype(dy_ref.dtype)
    _acc_rows(sums_ref, [(3, dxt * (yhat * ng_post)), (4, dn * yhat)])


def _loss_boundary(name, x_prev, y, gate, ng_post, target):
    S, D = x_prev.shape
    ts = min(256, S)
    row = pl.BlockSpec((ts, D), lambda i: (i, 0))
    vec = _vec_spec(D)

    def body(x_ref, y_ref, g_ref, ngp_ref, t_ref, dx_ref, dy_ref, sums_ref, loss_ref):
        @pl.when(pl.program_id(0) == 0)
        def _():
            sums_ref[...] = jnp.zeros_like(sums_ref)
            loss_ref[...] = jnp.zeros_like(loss_ref)

        yv = y_ref[...]
        xf = x_ref[...] + g_ref[...] * (yv * _rms(yv) * ngp_ref[...])
        err = xf - t_ref[...]
        loss_ref[...] += 0.5 * jnp.sum(jnp.mean(err * err, axis=-1, keepdims=True))
        dxt = err / D
        dx_ref[...] = dxt
        _post_norm_bwd(dxt, yv, g_ref[...], ngp_ref[...], sums_ref, dy_ref)

    return pl.pallas_call(
        body, name=name, grid=(S // ts,),
        in_specs=[row, row, vec, vec, row],
        out_specs=[row, row, pl.BlockSpec((8, D), lambda i: (0, 0)), pl.BlockSpec((8, LANES), lambda i: (0, 0))],
        out_shape=[_sds((S, D), F32), _sds((S, D), BF16), _sds((8, D), F32), _sds((8, LANES), F32)],
        compiler_params=_cparams("arbitrary"))(x_prev, y, gate, ng_post, target)


def _bwd_boundary(name, dx_new, dh, x_new, y, gate, ng_post, ng_pre, sc, rider=NO_RIDER):
    S, D = x_new.shape
    ts = min(256, S)
    has_y = y is not None
    row = pl.BlockSpec((ts, D), lambda i: (i, 0))
    vec = _vec_spec(D)
    n_in, n_out = (8, 3) if has_y else (5, 2)
    n_ri, n_ro = len(rider.operands), len(rider.out_shape)

    def body(*refs):
        r_in = refs[n_in:n_in + n_ri]
        r_out = refs[n_in + n_ri + n_out:n_in + n_ri + n_out + n_ro]
        r_sems = refs[n_in + n_ri + n_out + n_ro:]
        own = refs[:n_in] + refs[n_in + n_ri:n_in + n_ri + n_out]
        if has_y:
            dxn_ref, dh_ref, x_ref, y_ref, g_ref, ngp_ref, ngn_ref, sc_ref, dxo_ref, dy_ref, sums_ref = own
        else:
            dxn_ref, dh_ref, x_ref, ngn_ref, sc_ref, dxo_ref, sums_ref = own

        @pl.when(pl.program_id(0) == 0)
        def _():
            sums_ref[...] = jnp.zeros_like(sums_ref)
            if rider.start is not None:
                rider.start(r_in, r_out, r_sems)

        xv = x_ref[...]
        dhv = dh_ref[...]
        ngn = ngn_ref[...]
        r2 = _rms(xv)
        xhat = xv * r2
        dn_pre = dhv * (1.0 + sc_ref[...])
        u2 = dn_pre * ngn
        dxt = dxn_ref[...] + r2 * (u2 - xhat * jnp.mean(u2 * xhat, axis=-1, keepdims=True))
        dxo_ref[...] = dxt
        _acc_rows(sums_ref, [(0, dhv), (1, dhv * (xhat * ngn)), (2, dn_pre * xhat)])
        if has_y:
            _post_norm_bwd(dxt, y_ref[...], g_ref[...], ngp_ref[...], sums_ref, dy_ref)

        if rider.finish is not None:
            @pl.when(pl.program_id(0) == S // ts - 1)
            def _():
                rider.finish(r_in, r_out, r_sems)

    sums_spec = pl.BlockSpec((8, D), lambda i: (0, 0))
    if has_y:
        operands = (dx_new, dh, x_new, y, gate, ng_post, ng_pre, sc)
        in_specs = [row, row, row, row, vec, vec, vec, vec]
        out_shape = [_sds((S, D), F32), _sds((S, D), BF16), _sds((8, D), F32)]
        out_specs = [row, row, sums_spec]
    else:
        operands = (dx_new, dh, x_new, ng_pre, sc)
        in_specs = [row, row, row, vec, vec]
        out_shape = [_sds((S, D), F32), _sds((8, D), F32)]
        out_specs = [row, sums_spec]
    res = pl.pallas_call(
        body, name=name, grid=(S // ts,), in_specs=in_specs + [ANY] * n_ri, out_specs=out_specs + [ANY] * n_ro,
        out_shape=out_shape + list(rider.out_shape), scratch_shapes=list(rider.sems),
        input_output_aliases={n_in + i: n_out + o for i, o in rider.aliases.items()},
        compiler_params=_cparams("arbitrary"))(*operands, *rider.operands)
    return (*res[:n_out], res[n_out:])


def _latent_fwd(lat, g_q, g_kv, rope_tabs, rank):
    S, W = lat.shape
    ts = min(256, S)
    tab = pl.BlockSpec((ts, LANES), lambda i: (i, 0))

    def body(lat_ref, gq_ref, gkv_ref, cos_ref, slo_ref, shi_ref, cq_ref, ckv_ref, kr_ref):
        lq = lat_ref[:, 0:rank]
        lkv = lat_ref[:, rank:2 * rank]
        cq_ref[...] = (lq * _rms(lq) * gq_ref[...]).astype(BF16)
        ckv_ref[...] = (lkv * _rms(lkv) * gkv_ref[...]).astype(BF16)
        kr_ref[...] = _rope(lat_ref[:, 2 * rank:W], cos_ref[...], slo_ref[...], shi_ref[...]).astype(BF16)

    return pl.pallas_call(
        body, name="mla_latent_fwd", grid=(S // ts,),
        in_specs=[pl.BlockSpec((ts, W), lambda i: (i, 0)), _vec_spec(rank), _vec_spec(rank), tab, tab, tab],
        out_specs=[pl.BlockSpec((ts, rank), lambda i: (i, 0)), pl.BlockSpec((ts, rank), lambda i: (i, 0)), tab],
        out_shape=[_sds((S, rank), BF16), _sds((S, rank), BF16), _sds((S, LANES), BF16)],
        compiler_params=_cparams("parallel"))(lat, g_q, g_kv, *rope_tabs)


def _latent_bwd(lat, dcq, dckv, dkr, g_q, g_kv, rope_tabs, rank):
    S, W = lat.shape
    ts = min(256, S)
    tab = pl.BlockSpec((ts, LANES), lambda i: (i, 0))
    half = pl.BlockSpec((ts, rank), lambda i: (i, 0))

    def body(lat_ref, dcq_ref, dckv_ref, dkr_ref, gq_ref, gkv_ref, cos_ref, slo_ref, shi_ref, dlat_ref, sums_ref):
        @pl.when(pl.program_id(0) == 0)
        def _():
            sums_ref[...] = jnp.zeros_like(sums_ref)

        def norm_bwd(v, dn, g, r):
            rr = _rms(v)
            vhat = v * rr
            u = dn * g
            sums_ref[r:r + 1, :] += jnp.sum(dn * vhat, axis=0, keepdims=True)
            return rr * (u - vhat * jnp.mean(u * vhat, axis=-1, keepdims=True))

        dlat_ref[:, 0:rank] = norm_bwd(lat_ref[:, 0:rank], dcq_ref[...], gq_ref[...], 0).astype(BF16)
        dlat_ref[:, rank:2 * rank] = norm_bwd(lat_ref[:, rank:2 * rank], dckv_ref[...], gkv_ref[...], 1).astype(BF16)
        dlat_ref[:, 2 * rank:W] = _rope_t(dkr_ref[...], cos_ref[...], slo_ref[...], shi_ref[...]).astype(BF16)

    return pl.pallas_call(
        body, name="mla_latent_bwd", grid=(S // ts,),
        in_specs=[pl.BlockSpec((ts, W), lambda i: (i, 0)), half, half, tab, _vec_spec(rank), _vec_spec(rank),
                  tab, tab, tab],
        out_specs=[pl.BlockSpec((ts, W), lambda i: (i, 0)), pl.BlockSpec((8, rank), lambda i: (0, 0))],
        out_shape=[_sds((S, W), BF16), _sds((8, rank), F32)],
        compiler_params=_cparams("arbitrary"))(lat, dcq, dckv, dkr, g_q, g_kv, *rope_tabs)


def _attn_tiles(S):
    t = min(512, S)
    return t, S // t


def _causal_mask(t):
    return lax.broadcasted_iota(jnp.int32, (t, t), 1) <= lax.broadcasted_iota(jnp.int32, (t, t), 0)


def _attn_fwd(q, kv, kr, heads, scale, rider=NO_RIDER):
    S = q.shape[0]
    t, nb = _attn_tiles(S)
    G = 2 if heads % 2 == 0 else 1
    n_ri, n_ro = len(rider.operands), len(rider.out_shape)

    def body(*refs):
        q_ref, kv_ref, kr_ref = refs[:3]
        r_in = refs[3:3 + n_ri]
        o_ref, lse_ref = refs[3 + n_ri:5 + n_ri]
        r_out = refs[5 + n_ri:5 + n_ri + n_ro]
        m_scr, acc_scr = refs[5 + n_ri + n_ro:7 + n_ri + n_ro]
        r_sems = refs[7 + n_ri + n_ro:]
        h, qi, ki = pl.program_id(0), pl.program_id(1), pl.program_id(2)

        if rider.start is not None:
            @pl.when((h == 0) & (qi == 0) & (ki == 0))
            def _():
                rider.start(r_in, r_out, r_sems)

        @pl.when(ki == 0)
        def _():
            m_scr[...] = jnp.full_like(m_scr, NEG_INF)
            acc_scr[...] = jnp.zeros_like(acc_scr)

        def step(diagonal):
            ones = jnp.ones((t, LANES), BF16)
            for g in range(G):
                kcat = jnp.concatenate([kv_ref[:, g * QK_PAD:g * QK_PAD + QK_NOPE], kr_ref[...]], axis=1)
                vext = jnp.concatenate([kv_ref[:, g * QK_PAD + QK_NOPE:(g + 1) * QK_PAD], ones], axis=1)
                s = lax.dot_general(q_ref[:, g * QK_PAD:(g + 1) * QK_PAD], kcat, DIMS_NT,
                                    preferred_element_type=F32) * scale
                if diagonal:
                    s = jnp.where(_causal_mask(t), s, NEG_INF)
                m_prev = m_scr[g]
                m_new = jnp.maximum(m_prev, jnp.max(s, axis=-1, keepdims=True))
                alpha = jnp.exp(m_prev - m_new)
                p = jnp.exp(s - jnp.tile(m_new, (1, t // LANES)))
                acc_scr[g] = jnp.tile(alpha, (1, 2)) * acc_scr[g] + lax.dot_general(
                    p.astype(BF16), vext, DIMS_NN, preferred_element_type=F32)
                m_scr[g] = m_new

        @pl.when(ki < qi)
        def _():
            step(False)

        @pl.when(ki == qi)
        def _():
            step(True)

        @pl.when(ki == nb - 1)
        def _():
            for g in range(G):
                acc = acc_scr[g]
                o_ref[:, g * V_HEAD:(g + 1) * V_HEAD] = (acc[:, 0:V_HEAD] / acc[:, V_HEAD:2 * V_HEAD]).astype(BF16)
                lse_ref[g] = m_scr[g] + jnp.log(acc[:, V_HEAD:2 * V_HEAD])

        if rider.finish is not None:
            @pl.when((h == heads // G - 1) & (qi == nb - 1) & (ki == nb - 1))
            def _():
                rider.finish(r_in, r_out, r_sems)

    res = pl.pallas_call(
        body, name="mla_attn_fwd", grid=(heads // G, nb, nb),
        in_specs=[pl.BlockSpec((t, G * QK_PAD), lambda h, qi, ki: (qi, h)),
                  pl.BlockSpec((t, G * QK_PAD), lambda h, qi, ki: (jnp.minimum(ki, qi), h)),
                  pl.BlockSpec((t, LANES), lambda h, qi, ki: (jnp.minimum(ki, qi), 0))] + [ANY] * n_ri,
        out_specs=[pl.BlockSpec((t, G * V_HEAD), lambda h, qi, ki: (qi, h)),
                   pl.BlockSpec((G, t, LANES), lambda h, qi, ki: (h, qi, 0))] + [ANY] * n_ro,
        out_shape=[_sds((S, heads * V_HEAD), BF16), _sds((heads, S, LANES), F32)] + list(rider.out_shape),
        scratch_shapes=[pltpu.VMEM((G, t, LANES), F32), pltpu.VMEM((G, t, 2 * V_HEAD), F32)] + list(rider.sems),
        input_output_aliases={3 + i: 2 + o for i, o in rider.aliases.items()},
        compiler_params=_cparams("arbitrary", "arbitrary", "arbitrary"))(q, kv, kr, *rider.operands)
    return res[0], res[1], res[2:]


def _attn_delta(o, do, heads):
    S = o.shape[0]
    t, nb = _attn_tiles(S)

    def body(o_ref, do_ref, out_ref):
        d = jnp.sum(do_ref[...].astype(F32) * o_ref[...].astype(F32), axis=-1, keepdims=True)
        out_ref[...] = jnp.broadcast_to(d, (t, LANES))

    blk = pl.BlockSpec((t, V_HEAD), lambda h, i: (i, h))
    return pl.pallas_call(
        body, name="mla_attn_delta", grid=(heads, nb), in_specs=[blk, blk],
        out_specs=pl.BlockSpec((None, t, LANES), lambda h, i: (h, i, 0)),
        out_shape=_sds((heads, S, LANES), F32), compiler_params=_cparams("parallel", "parallel"))(o, do)


def _attn_bwd(q, kv, kr, delta, do, lse, rope_tabs, heads, scale, rider=NO_RIDER):
    S = q.shape[0]
    t, nb = _attn_tiles(S)
    n_ri, n_ro = len(rider.operands), len(rider.out_shape)
    rep = t // LANES

    def body(*refs):
        q_ref, kv_ref, kr_ref, delta_ref, do_ref, lse_ref, cos_ref, slo_ref, shi_ref = refs[:9]
        r_in = refs[9:9 + n_ri]
        dq_ref, dkv_ref, dkr_ref = refs[9 + n_ri:12 + n_ri]
        r_out = refs[12 + n_ri:12 + n_ri + n_ro]
        dq_scr, dk_scr, dv_scr, dkr_scr = refs[12 + n_ri + n_ro:16 + n_ri + n_ro]
        r_sems = refs[16 + n_ri + n_ro:]
        h, ki, qi = pl.program_id(0), pl.program_id(1), pl.program_id(2)
        q_rows = pl.ds(pl.multiple_of(qi * t, t), t)
        k_rows = pl.ds(pl.multiple_of(ki * t, t), t)

        if rider.start is not None:
            @pl.when((h == 0) & (ki == 0) & (qi == 0))
            def _():
                rider.start(r_in, r_out, r_sems)

        @pl.when((ki == 0) & (qi == 0))
        def _():
            dq_scr[...] = jnp.zeros_like(dq_scr)

        @pl.when((h == 0) & (ki == 0) & (qi == 0))
        def _():
            dkr_scr[...] = jnp.zeros_like(dkr_scr)

        @pl.when(qi == 0)
        def _():
            dk_scr[...] = jnp.zeros_like(dk_scr)
            dv_scr[...] = jnp.zeros_like(dv_scr)

        def step(diagonal):
            qv = q_ref[...]
            kcat = jnp.concatenate([kv_ref[:, 0:QK_NOPE], kr_ref[...]], axis=1)
            s = lax.dot_general(qv, kcat, DIMS_NT, preferred_element_type=F32) * scale
            p = jnp.exp(s - jnp.tile(lse_ref[...], (1, rep)))
            if diagonal:
                p = jnp.where(_causal_mask(t), p, 0.0)
            dov = do_ref[...]
            dv_scr[...] += lax.dot_general(p.astype(BF16), dov, DIMS_TN, preferred_element_type=F32)
            dp = lax.dot_general(dov, kv_ref[:, QK_NOPE:QK_NOPE + V_HEAD], DIMS_NT, preferred_element_type=F32)
            ds = (p * (dp - jnp.tile(delta_ref[...], (1, rep))) * scale).astype(BF16)
            dk_scr[...] += lax.dot_general(ds, qv, DIMS_TN, preferred_element_type=F32)
            dq_scr[q_rows, :] += lax.dot_general(ds, kcat, DIMS_NN, preferred_element_type=F32)

        @pl.when(qi > ki)
        def _():
            step(False)

        @pl.when(qi == ki)
        def _():
            step(True)

        @pl.when(qi == nb - 1)
        def _():
            dkv_ref[...] = jnp.concatenate([dk_scr[:, 0:QK_NOPE], dv_scr[...]], axis=1).astype(BF16)
            dkr_scr[k_rows, :] += dk_scr[:, QK_NOPE:QK_PAD]

        @pl.when(ki == nb - 1)
        def _():
            dqv = dq_scr[q_rows, :]
            dq_ref[q_rows, :] = jnp.concatenate(
                [dqv[:, 0:QK_NOPE], _rope_t(dqv[:, QK_NOPE:QK_PAD], cos_ref[...], slo_ref[...], shi_ref[...])],
                axis=1).astype(BF16)

        @pl.when((h == heads - 1) & (ki == nb - 1) & (qi == nb - 1))
        def _():
            dkr_ref[...] = dkr_scr[...]
            if rider.finish is not None:
                rider.finish(r_in, r_out, r_sems)

    qmap = lambda h, ki, qi: (jnp.maximum(qi, ki), h)
    stat = pl.BlockSpec((None, t, LANES), lambda h, ki, qi: (h, jnp.maximum(qi, ki), 0))
    tab = pl.BlockSpec((t, LANES), lambda h, ki, qi: (qi, 0))
    res = pl.pallas_call(
        body, name="mla_attn_bwd", grid=(heads, nb, nb),
        in_specs=[pl.BlockSpec((t, QK_PAD), qmap),
                  pl.BlockSpec((t, QK_PAD), lambda h, ki, qi: (ki, h)),
                  pl.BlockSpec((t, LANES), lambda h, ki, qi: (ki, 0)),
                  stat,
                  pl.BlockSpec((t, V_HEAD), qmap),
                  stat,
                  tab, tab, tab] + [ANY] * n_ri,
        out_specs=[pl.BlockSpec((S, QK_PAD), lambda h, ki, qi: (0, h)),
                   pl.BlockSpec((t, QK_PAD), lambda h, ki, qi: (ki, h)),
                   pl.BlockSpec((S, LANES), lambda h, ki, qi: (0, 0))] + [ANY] * n_ro,
        out_shape=[_sds((S, heads * QK_PAD), BF16), _sds((S, heads * QK_PAD), BF16), _sds((S, LANES), F32)]
        + list(rider.out_shape),
        scratch_shapes=[pltpu.VMEM((S, QK_PAD), F32), pltpu.VMEM((t, QK_PAD), F32), pltpu.VMEM((t, V_HEAD), F32),
                        pltpu.VMEM((S, LANES), F32)] + list(rider.sems),
        input_output_aliases={9 + i: 3 + o for i, o in rider.aliases.items()},
        compiler_params=_cparams("arbitrary", "arbitrary", "arbitrary"))(q, kv, kr, delta, do, lse, *rope_tabs, *rider.operands)
    return res[0], res[1], res[2], res[3:]


def _causal_pairs(nb, q_major):
    if q_major:
        pairs = [(qi, ki) for qi in range(nb) for ki in range(qi + 1)]
    else:
        pairs = [(qi, ki) for ki in range(nb) for qi in range(ki, nb)]
    return jnp.array([p[0] for p in pairs], jnp.int32), jnp.array([p[1] for p in pairs], jnp.int32), len(pairs)


def _heads_per_step(heads):
    return 2 if heads % 2 == 0 else 1


def _attn_fwd_tri(q, kv, kr, heads, scale, rider=NO_RIDER):
    S = q.shape[0]
    t, nb = _attn_tiles(S)
    G = _heads_per_step(heads)
    q_tab, k_tab, n_pairs = _causal_pairs(nb, True)
    n_ri, n_ro = len(rider.operands), len(rider.out_shape)

    def body(qt_ref, kt_ref, *refs):
        q_ref, kv_ref, kr_ref = refs[:3]
        r_in = refs[3:3 + n_ri]
        o_ref, lse_ref = refs[3 + n_ri:5 + n_ri]
        r_out = refs[5 + n_ri:5 + n_ri + n_ro]
        m_scr, acc_scr = refs[5 + n_ri + n_ro:7 + n_ri + n_ro]
        r_sems = refs[7 + n_ri + n_ro:]
        h, p = pl.program_id(0), pl.program_id(1)
        qi, ki = qt_ref[p], kt_ref[p]

        if rider.start is not None:
            @pl.when((h == 0) & (p == 0))
            def _():
                rider.start(r_in, r_out, r_sems)

        @pl.when(ki == 0)
        def _():
            m_scr[...] = jnp.full_like(m_scr, NEG_INF)
            acc_scr[...] = jnp.zeros_like(acc_scr)

        def step(diagonal):
            ones = jnp.ones((t, LANES), BF16)
            for g in range(G):
                kcat = jnp.concatenate([kv_ref[:, g * QK_PAD:g * QK_PAD + QK_NOPE], kr_ref[...]], axis=1)
                vext = jnp.concatenate([kv_ref[:, g * QK_PAD + QK_NOPE:(g + 1) * QK_PAD], ones], axis=1)
                s = lax.dot_general(q_ref[:, g * QK_PAD:(g + 1) * QK_PAD], kcat, DIMS_NT,
                                    preferred_element_type=F32) * scale
                if diagonal:
                    s = jnp.where(_causal_mask(t), s, NEG_INF)
                m_prev = m_scr[g]
                m_new = jnp.maximum(m_prev, jnp.max(s, axis=-1, keepdims=True))
                alpha = jnp.exp(m_prev - m_new)
                pr = jnp.exp(s - jnp.tile(m_new, (1, t // LANES)))
                acc_scr[g] = jnp.tile(alpha, (1, 2)) * acc_scr[g] + lax.dot_general(
                    pr.astype(BF16), vext, DIMS_NN, preferred_element_type=F32)
                m_scr[g] = m_new

        @pl.when(ki < qi)
        def _():
            step(False)

        @pl.when(ki == qi)
        def _():
            step(True)
            for g in range(G):
                acc = acc_scr[g]
                o_ref[:, g * V_HEAD:(g + 1) * V_HEAD] = (acc[:, 0:V_HEAD] / acc[:, V_HEAD:2 * V_HEAD]).astype(BF16)
                lse_ref[g] = m_scr[g] + jnp.log(acc[:, V_HEAD:2 * V_HEAD])

        if rider.finish is not None:
            @pl.when((h == heads // G - 1) & (p == n_pairs - 1))
            def _():
                rider.finish(r_in, r_out, r_sems)

    res = pl.pallas_call(
        body, name="mla_attn_fwd",
        grid_spec=pltpu.PrefetchScalarGridSpec(
            num_scalar_prefetch=2, grid=(heads // G, n_pairs),
            in_specs=[pl.BlockSpec((t, G * QK_PAD), lambda h, p, qt, kt: (qt[p], h)),
                      pl.BlockSpec((t, G * QK_PAD), lambda h, p, qt, kt: (kt[p], h)),
                      pl.BlockSpec((t, LANES), lambda h, p, qt, kt: (kt[p], 0))] + [ANY] * n_ri,
            out_specs=[pl.BlockSpec((t, G * V_HEAD), lambda h, p, qt, kt: (qt[p], h)),
                       pl.BlockSpec((G, t, LANES), lambda h, p, qt, kt: (h, qt[p], 0))] + [ANY] * n_ro,
            scratch_shapes=[pltpu.VMEM((G, t, LANES), F32), pltpu.VMEM((G, t, 2 * V_HEAD), F32)] + list(rider.sems)),
        out_shape=[_sds((S, heads * V_HEAD), BF16), _sds((heads, S, LANES), F32)] + list(rider.out_shape),
        input_output_aliases={5 + i: 2 + o for i, o in rider.aliases.items()},
        compiler_params=_cparams("arbitrary", "arbitrary"))(q_tab, k_tab, q, kv, kr, *rider.operands)
    return res[0], res[1], res[2:]


def _attn_bwd_tri(q, kv, kr, o, do, lse, rope_tabs, heads, scale, rider=NO_RIDER):
    S = q.shape[0]
    t, nb = _attn_tiles(S)
    G = _heads_per_step(heads)
    q_tab, k_tab, n_pairs = _causal_pairs(nb, False)
    n_ri, n_ro = len(rider.operands), len(rider.out_shape)
    rep = t // LANES

    def body(qt_ref, kt_ref, *refs):
        q_ref, kv_ref, kr_ref, o_ref, do_ref, lse_ref, cos_ref, slo_ref, shi_ref = refs[:9]
        r_in = refs[9:9 + n_ri]
        dq_ref, dkv_ref, dkr_ref = refs[9 + n_ri:12 + n_ri]
        r_out = refs[12 + n_ri:12 + n_ri + n_ro]
        dq_scr, dk_scr, dv_scr, dkr_scr, delta_scr = refs[12 + n_ri + n_ro:17 + n_ri + n_ro]
        r_sems = refs[17 + n_ri + n_ro:]
        h, p = pl.program_id(0), pl.program_id(1)
        qi, ki = qt_ref[p], kt_ref[p]
        q_rows = pl.ds(pl.multiple_of(qi * t, t), t)
        k_rows = pl.ds(pl.multiple_of(ki * t, t), t)

        @pl.when(ki == 0)
        def _():
            for g in range(G):
                cols = slice(g * V_HEAD, (g + 1) * V_HEAD)
                d = jnp.sum(do_ref[:, cols].astype(F32) * o_ref[:, cols].astype(F32), axis=-1, keepdims=True)
                delta_scr[g, q_rows, :] = jnp.broadcast_to(d, (t, LANES))

        if rider.start is not None:
            @pl.when((h == 0) & (p == 0))
            def _():
                rider.start(r_in, r_out, r_sems)

        @pl.when(p == 0)
        def _():
            dq_scr[...] = jnp.zeros_like(dq_scr)

        @pl.when((h == 0) & (p == 0))
        def _():
            dkr_scr[...] = jnp.zeros_like(dkr_scr)

        @pl.when(qi == ki)
        def _():
            dk_scr[...] = jnp.zeros_like(dk_scr)
            dv_scr[...] = jnp.zeros_like(dv_scr)

        def step(diagonal):
            for g in range(G):
                qv = q_ref[:, g * QK_PAD:(g + 1) * QK_PAD]
                kcat = jnp.concatenate([kv_ref[:, g * QK_PAD:g * QK_PAD + QK_NOPE], kr_ref[...]], axis=1)
                s = lax.dot_general(qv, kcat, DIMS_NT, preferred_element_type=F32) * scale
                pr = jnp.exp(s - jnp.tile(lse_ref[g], (1, rep)))
                if diagonal:
                    pr = jnp.where(_causal_mask(t), pr, 0.0)
                dov = do_ref[:, g * V_HEAD:(g + 1) * V_HEAD]
                dv_scr[g] += lax.dot_general(pr.astype(BF16), dov, DIMS_TN, preferred_element_type=F32)
                dp = lax.dot_general(dov, kv_ref[:, g * QK_PAD + QK_NOPE:(g + 1) * QK_PAD], DIMS_NT,
                                     preferred_element_type=F32)
                ds = (pr * (dp - jnp.tile(delta_scr[g, q_rows, :], (1, rep))) * scale).astype(BF16)
                dk_scr[g] += lax.dot_general(ds, qv, DIMS_TN, preferred_element_type=F32)
                dq_scr[q_rows, g * QK_PAD:(g + 1) * QK_PAD] += lax.dot_general(ds, kcat, DIMS_NN,
                                                                               preferred_element_type=F32)

        @pl.when(qi > ki)
        def _():
            step(False)

        @pl.when(qi == ki)
        def _():
            step(True)
            for g in range(G):
                dqv = dq_scr[q_rows, g * QK_PAD:(g + 1) * QK_PAD]
                dq_ref[q_rows, g * QK_PAD:(g + 1) * QK_PAD] = jnp.concatenate(
                    [dqv[:, 0:QK_NOPE], _rope_t(dqv[:, QK_NOPE:QK_PAD], cos_ref[...], slo_ref[...], shi_ref[...])],
                    axis=1).astype(BF16)

        @pl.when(qi == nb - 1)
        def _():
            for g in range(G):
                dkv_ref[:, g * QK_PAD:(g + 1) * QK_PAD] = jnp.concatenate(
                    [dk_scr[g][:, 0:QK_NOPE], dv_scr[g]], axis=1).astype(BF16)
                dkr_scr[k_rows, :] += dk_scr[g][:, QK_NOPE:QK_PAD]

        @pl.when((h == heads // G - 1) & (p == n_pairs - 1))
        def _():
            dkr_ref[...] = dkr_scr[...]
            if rider.finish is not None:
                rider.finish(r_in, r_out, r_sems)

    q_blk = lambda w: pl.BlockSpec((t, G * w), lambda h, p, qt, kt: (qt[p], h))
    stat = pl.BlockSpec((G, t, LANES), lambda h, p, qt, kt: (h, qt[p], 0))
    tab = pl.BlockSpec((t, LANES), lambda h, p, qt, kt: (kt[p], 0))
    res = pl.pallas_call(
        body, name="mla_attn_bwd",
        grid_spec=pltpu.PrefetchScalarGridSpec(
            num_scalar_prefetch=2, grid=(heads // G, n_pairs),
            in_specs=[q_blk(QK_PAD),
                      pl.BlockSpec((t, G * QK_PAD), lambda h, p, qt, kt: (kt[p], h)),
                      tab, q_blk(V_HEAD), q_blk(V_HEAD), stat, tab, tab, tab] + [ANY] * n_ri,
            out_specs=[pl.BlockSpec((S, G * QK_PAD), lambda h, p, qt, kt: (0, h)),
                       pl.BlockSpec((t, G * QK_PAD), lambda h, p, qt, kt: (kt[p], h)),
                       pl.BlockSpec((S, LANES), lambda h, p, qt, kt: (0, 0))] + [ANY] * n_ro,
            scratch_shapes=[pltpu.VMEM((S, G * QK_PAD), F32), pltpu.VMEM((G, t, QK_PAD), F32),
                            pltpu.VMEM((G, t, V_HEAD), F32), pltpu.VMEM((S, LANES), F32),
                            pltpu.VMEM((G, S, LANES), F32)] + list(rider.sems)),
        out_shape=[_sds((S, heads * QK_PAD), BF16), _sds((S, heads * QK_PAD), BF16), _sds((S, LANES), F32)]
        + list(rider.out_shape),
        input_output_aliases={11 + i: 3 + o for i, o in rider.aliases.items()},
        compiler_params=_cparams("arbitrary", "arbitrary"))(q_tab, k_tab, q, kv, kr, o, do, lse, *rope_tabs,
                                                            *rider.operands)
    return res[0], res[1], res[2], res[3:]


def _shift_down(z, n, rows):
    return jnp.where(rows >= n, pltpu.roll(z, n, 0), 0.0)


def _shift_up(z, n, rows, S):
    return jnp.where(rows < S - n, pltpu.roll(z, S - n, 0), 0.0)


def _conv_specs(S, tc):
    strip = lambda p: pl.BlockSpec((None, S, tc), lambda j: (p, 0, j))
    return strip(0), strip(1), strip(2), pl.BlockSpec((3, tc), lambda j: (0, j))


def _conv_fwd(proj3, w):
    _, S, D = proj3.shape
    tc = LANES

    def body(b_ref, c_ref, u_ref, w_ref, out_ref):
        z = c_ref[...] * u_ref[...]
        rows = lax.broadcasted_iota(jnp.int32, (S, tc), 0)
        zc = w_ref[0:1, :] * _shift_down(z, 2, rows) + w_ref[1:2, :] * _shift_down(z, 1, rows) + w_ref[2:3, :] * z
        out_ref[...] = (b_ref[...] * zc).astype(BF16)

    return pl.pallas_call(
        body, name="conv_fwd", grid=(D // tc,), in_specs=list(_conv_specs(S, tc)),
        out_specs=pl.BlockSpec((S, tc), lambda j: (0, j)), out_shape=_sds((S, D), BF16),
        compiler_params=_cparams("parallel"))(proj3, proj3, proj3, w)


def _conv_bwd(dbz, proj3, w):
    _, S, D = proj3.shape
    tc = LANES

    def body(d_ref, b_ref, c_ref, u_ref, w_ref, dp_ref, dw_ref):
        cv, uv, dv = c_ref[...], u_ref[...], d_ref[...]
        z = cv * uv
        rows = lax.broadcasted_iota(jnp.int32, (S, tc), 0)
        z1, z2 = _shift_down(z, 1, rows), _shift_down(z, 2, rows)
        zc = w_ref[0:1, :] * z2 + w_ref[1:2, :] * z1 + w_ref[2:3, :] * z
        dp_ref[0] = (dv * zc).astype(BF16)
        dzc = dv * b_ref[...]
        dz = w_ref[2:3, :] * dzc + w_ref[1:2, :] * _shift_up(dzc, 1, rows, S) + w_ref[0:1, :] * _shift_up(dzc, 2, rows, S)
        dp_ref[1] = (dz * uv).astype(BF16)
        dp_ref[2] = (dz * cv).astype(BF16)
        dw_ref[0:1, :] = jnp.sum(dzc * z2, axis=0, keepdims=True)
        dw_ref[1:2, :] = jnp.sum(dzc * z1, axis=0, keepdims=True)
        dw_ref[2:3, :] = jnp.sum(dzc * z, axis=0, keepdims=True)

    sb, sc_, su, sw = _conv_specs(S, tc)
    return pl.pallas_call(
        body, name="conv_bwd", grid=(D // tc,),
        in_specs=[pl.BlockSpec((S, tc), lambda j: (0, j)), sb, sc_, su, sw],
        out_specs=[pl.BlockSpec((3, S, tc), lambda j: (0, 0, j)), pl.BlockSpec((3, tc), lambda j: (0, j))],
        out_shape=[_sds((3, S, D), BF16), _sds((3, D), F32)],
        compiler_params=_cparams("parallel"))(dbz, proj3, proj3, proj3, w)


def _silu(c_all):
    def body(c_ref, o_ref):
        cv = c_ref[...]
        o_ref[...] = cv * (1.0 / (1.0 + jnp.exp(-cv)))

    vm = pl.BlockSpec(memory_space=pltpu.VMEM)
    return pl.pallas_call(body, name="cond_silu", in_specs=[vm], out_specs=vm, out_shape=_sds(c_all.shape, F32))(c_all)


def _mod_fwd(cond, w_mod, b_cols):
    L, D, ncol = w_mod.shape
    B = cond.shape[0]
    tk, tn = min(512, D), min(1024, ncol)
    nk = D // tk

    def body(c_ref, w_ref, b_ref, out_ref, acc):
        kk = pl.program_id(2)
        part = lax.dot_general(c_ref[...].astype(BF16), w_ref[...].astype(BF16), DIMS_NN, preferred_element_type=F32)

        @pl.when(kk == 0)
        def _():
            acc[...] = part

        @pl.when(kk > 0)
        def _():
            acc[...] += part

        @pl.when(kk == nk - 1)
        def _():
            out_ref[...] = acc[...] + b_ref[...]

    return pl.pallas_call(
        body, name="mod_fwd", grid=(L, ncol // tn, nk),
        in_specs=[pl.BlockSpec((B, tk), lambda l, j, k: (0, k)),
                  pl.BlockSpec((None, tk, tn), lambda l, j, k: (l, k, j)),
                  pl.BlockSpec((None, 1, tn), lambda l, j, k: (l, 0, j))],
        out_specs=pl.BlockSpec((None, B, tn), lambda l, j, k: (l, 0, j)),
        out_shape=_sds((L, B, ncol), F32),
        scratch_shapes=[pltpu.VMEM((B, tn), F32)],
        compiler_params=_cparams("parallel", "parallel", "arbitrary"))(cond, w_mod, b_cols)


def _adamw_math(w, g, m, v):
    m = ADAM_B1 * m + (1.0 - ADAM_B1) * g
    v = ADAM_B2 * v + (1.0 - ADAM_B2) * (g * g)
    m_hat = m / (1.0 - ADAM_B1 ** ADAM_STEP)
    v_hat = v / (1.0 - ADAM_B2 ** ADAM_STEP)
    delta = -ADAM_LR * (m_hat / (jnp.sqrt(v_hat) + ADAM_EPS) + ADAM_WD * w)
    return delta, m, v


def _adamw(name, w, g, m, v):
    shape = w.shape
    cols = shape[-1] if w.ndim <= 3 else shape[-2] * shape[-1]
    rows = w.size // cols
    w2, g2, m2, v2 = (t.reshape(rows, cols) for t in (w, g, m, v))
    tr = _row_tile(rows, cols * 4, limit=1024 * 1024, mult=8)
    spec = pl.BlockSpec((tr, cols), lambda i: (i, 0))

    def body(w_ref, g_ref, m_ref, v_ref, d_ref, nm_ref, nv_ref):
        d, nm, nv = _adamw_math(w_ref[...], g_ref[...], m_ref[...], v_ref[...])
        d_ref[...] = d
        nm_ref[...] = nm
        nv_ref[...] = nv

    outs = pl.pallas_call(body, name=name, grid=(rows // tr,), in_specs=[spec] * 4, out_specs=[spec] * 3,
                          out_shape=[_sds((rows, cols), F32)] * 3, compiler_params=_cparams("parallel"))(w2, g2, m2, v2)
    return tuple(t.reshape(shape) for t in outs)


def _adamw_mod(w, cond_t, dmod_cols, m, v, rider=NO_RIDER):
    L, D, ncol = w.shape
    B = cond_t.shape[1]
    tr, tc = min(256, D), min(1024, ncol)
    blk = pl.BlockSpec((None, tr, tc), lambda l, i, j: (l, i, j))
    grid = (L, D // tr, ncol // tc)
    n_ri, n_ro = len(rider.operands), len(rider.out_shape)

    def body(*refs):
        w_ref, ct_ref, dm_ref, m_ref, v_ref = refs[:5]
        r_in = refs[5:5 + n_ri]
        g_ref, d_ref, nm_ref, nv_ref = refs[5 + n_ri:9 + n_ri]
        r_out = refs[9 + n_ri:9 + n_ri + n_ro]
        r_sems = refs[9 + n_ri + n_ro:]
        ids = [pl.program_id(a) for a in range(3)]

        if rider.start is not None:
            @pl.when((ids[0] == 0) & (ids[1] == 0) & (ids[2] == 0))
            def _():
                rider.start(r_in, r_out, r_sems)

        g = lax.dot_general(ct_ref[...], dm_ref[...], DIMS_NN, precision=lax.Precision.HIGHEST,
                            preferred_element_type=F32)
        d, nm, nv = _adamw_math(w_ref[...], g, m_ref[...], v_ref[...])
        g_ref[...] = g
        d_ref[...] = d
        nm_ref[...] = nm
        nv_ref[...] = nv

        if rider.finish is not None:
            @pl.when((ids[0] == grid[0] - 1) & (ids[1] == grid[1] - 1) & (ids[2] == grid[2] - 1))
            def _():
                rider.finish(r_in, r_out, r_sems)

    hosted = rider.start is not None
    res = pl.pallas_call(
        body, name="adamw_w_mod", grid=grid,
        in_specs=[blk, pl.BlockSpec((tr, B), lambda l, i, j: (i, 0)),
                  pl.BlockSpec((None, B, tc), lambda l, i, j: (l, 0, j)), blk, blk] + [ANY] * n_ri,
        out_specs=[blk] * 4 + [ANY] * n_ro, out_shape=[_sds((L, D, ncol), F32)] * 4 + list(rider.out_shape),
        scratch_shapes=list(rider.sems), input_output_aliases={5 + i: 4 + o for i, o in rider.aliases.items()},
        compiler_params=_cparams(*(("arbitrary",) * 3 if hosted else ("parallel",) * 3)))(
            w, cond_t, dmod_cols, m, v, *rider.operands)
    return (*res[:4], res[4:])


def _cast_into_full(name, w, kind, k_idx):
    L, R, C = w.shape
    Rh = R // 2
    tr = _row_tile(Rh, C * 4)
    if kind == "row":
        out_shape = (L, N_CHIPS, 2, Rh, C)
        out_spec = pl.BlockSpec((None, None, None, tr, C), lambda l, h, i, k_ref: (l, k_ref[0], h, i, 0))
    else:
        out_shape = (L, 2, Rh, N_CHIPS * C)
        out_spec = pl.BlockSpec((None, None, tr, C), lambda l, h, i, k_ref: (l, h, i, k_ref[0]))

    def body(k_ref, w_ref, o_ref):
        o_ref[...] = w_ref[...].astype(BF16)

    return pl.pallas_call(
        body, name=name,
        grid_spec=pltpu.PrefetchScalarGridSpec(
            num_scalar_prefetch=1, grid=(L, 2, Rh // tr),
            in_specs=[pl.BlockSpec((None, None, tr, C), lambda l, h, i, k_ref: (l, h, i, 0))],
            out_specs=out_spec),
        out_shape=_sds(out_shape, BF16),
        compiler_params=_cparams("parallel", "parallel", "parallel"))(k_idx, w.reshape(L, 2, Rh, C))


def _pair_sum(name, g5, ra, c_idx):
    L, A, _, Rh, Cc = g5.shape
    tr = _row_tile(Rh, Cc * 4)

    def body(c_ref, g_ref, r_ref, o_ref):
        o_ref[...] = (g_ref[...].astype(F32) + r_ref[...].astype(F32)).astype(BF16)

    blk = pl.BlockSpec((None, None, tr, Cc), lambda l, a, i, c_ref: (l, a, i, 0))
    return pl.pallas_call(
        body, name=name,
        grid_spec=pltpu.PrefetchScalarGridSpec(
            num_scalar_prefetch=1, grid=(L, A, Rh // tr),
            in_specs=[pl.BlockSpec((None, None, None, tr, Cc), lambda l, a, i, c_ref: (l, a, c_ref[0], i, 0)), blk],
            out_specs=blk),
        out_shape=_sds((L, A, Rh, Cc), BF16),
        compiler_params=_cparams("parallel", "parallel", "parallel"))(c_idx, g5, ra)


def _chip_sum(name, p, rb, kc_idx, kind, layer=0, n_layers=1, prev=None):
    _, A, Rh, Cc = p.shape
    C = rb.shape[-1]
    tr = _row_tile(Rh, C * 4)
    if kind == "row":
        own = pl.BlockSpec((None, None, tr, C), lambda i, kc: (0, kc[0], i, 0))
    else:
        own = pl.BlockSpec((None, None, tr, C), lambda i, kc: (0, 0, i, kc[0]))
    peer = lambda j: pl.BlockSpec((None, None, tr, C), lambda i, kc: (j, 0, i, 0))

    def body(kc_ref, p_ref, r0_ref, r1_ref, r2_ref, *rest):
        o_ref = rest[-1]
        o_ref[...] = ((p_ref[...].astype(F32) + r0_ref[...].astype(F32)) + r1_ref[...].astype(F32)) + r2_ref[...].astype(F32)

    operands = [kc_idx, p, rb, rb, rb] + ([prev] if prev is not None else [])
    return pl.pallas_call(
        body, name=name,
        grid_spec=pltpu.PrefetchScalarGridSpec(
            num_scalar_prefetch=1, grid=(Rh // tr,),
            in_specs=[own, peer(0), peer(1), peer(2)] + ([ANY] if prev is not None else []),
            out_specs=pl.BlockSpec((None, None, tr, C), lambda i, kc: (layer, kc[1], i, 0))),
        out_shape=_sds((n_layers, 2, Rh, C), F32),
        input_output_aliases={5: 0} if prev is not None else {},
        compiler_params=_cparams("parallel"))(*operands)


def _mesh_place():
    x, y, c = lax.axis_index("x"), lax.axis_index("y"), lax.axis_index("c")
    chips = [(1 - x, y), (x, 1 - y), (1 - x, 1 - y)]
    return x, y, c, chips


def _remote(src, dst, send_sem, recv_sem, to):
    return pltpu.make_async_remote_copy(src_ref=src, dst_ref=dst, send_sem=send_sem, recv_sem=recv_sem,
                                        device_id=to, device_id_type=MESH_ID)


def _small_allgather(name, v, with_sum=False):
    R, N = v.shape

    def body(*refs):
        if with_sum:
            x_ref, out_ref, sum_ref, send_sems, recv_sems, local_sem = refs
        else:
            x_ref, out_ref, send_sems, recv_sems, local_sem = refs
        x, y, c, chips = _mesh_place()
        me, sibling = (x, y, c), (x, y, 1 - c)

        def rows(px, py, pc):
            return out_ref.at[pl.ds((4 * px + 2 * py + pc) * R, R), :]

        def copy(k, block, to, src=None):
            return _remote(rows(*block) if src is None else src, rows(*block), send_sems.at[k], recv_sems.at[k], to)

        mine = pltpu.make_async_copy(x_ref, rows(*me), local_sem)
        mine.start()
        first = [copy(0, me, sibling, src=x_ref)]
        first += [copy(1 + j, me, (*chip, c), src=x_ref) for j, chip in enumerate(chips)]
        for cp in first:
            cp.start()
        passed = [copy(4 + j, (*chip, c), sibling) for j, chip in enumerate(chips)]
        for j, chip in enumerate(chips):
            copy(1 + j, (*chip, c), me).wait_recv()
            passed[j].start()
        copy(0, sibling, me).wait_recv()
        for j, chip in enumerate(chips):
            copy(4 + j, (*chip, 1 - c), me).wait_recv()
        for cp in first + passed:
            cp.wait_send()
        mine.wait()
        if with_sum:
            total = out_ref[0:R, :]
            for p in range(1, 8):
                total = total + out_ref[p * R:(p + 1) * R, :]
            sum_ref[...] = total

    vm = pl.BlockSpec(memory_space=pltpu.VMEM)
    out_shape = [_sds((8 * R, N), F32)] + ([_sds((R, N), F32)] if with_sum else [])
    res = pl.pallas_call(
        body, name=name, out_shape=out_shape, in_specs=[vm], out_specs=[vm] * len(out_shape),
        scratch_shapes=[pltpu.SemaphoreType.DMA((7,)), pltpu.SemaphoreType.DMA((7,)), pltpu.SemaphoreType.DMA],
        compiler_params=pltpu.CompilerParams(vmem_limit_bytes=VMEM_LIMIT_BYTES))(v)
    return res if with_sum else res[0]


def _full_place(ref, kind, C, kk, half, layer=None):
    lead = slice(None) if layer is None else pl.ds(layer, 1)
    if kind == "row":
        return ref.at[lead, kk, half]
    return ref.at[lead, half, :, pl.ds(pl.multiple_of(kk * C, LANES), C)]


def _gather_rider(fulls, kinds, shard_cols, layers=None):
    n = len(fulls)
    layers = layers or [None] * n

    def copies(outs, sems):
        x, y, c, chips = _mesh_place()
        k = 2 * x + y
        place = lambda a, kk, half: _full_place(outs[a], kinds[a], shard_cols[a], kk, half, layers[a])
        copy = lambda a, j, ref, to: _remote(ref, ref, sems[0].at[6 * a + j], sems[1].at[6 * a + j], to)
        return (x, y, c), chips, k, place, copy

    def start(_, outs, sems):
        (x, y, c), chips, k, place, copy = copies(outs, sems)
        for j, chip in enumerate(chips):
            for a in range(n):
                copy(a, j, place(a, k, c), (*chip, c)).start()

    def finish(_, outs, sems):
        (x, y, c), chips, k, place, copy = copies(outs, sems)
        me, sibling = (x, y, c), (x, y, 1 - c)
        for j, chip in enumerate(chips):
            kj = 2 * chip[0] + chip[1]
            for a in range(n):
                copy(a, j, place(a, kj, c), me).wait_recv()
                copy(a, 3 + j, place(a, kj, c), sibling).start()
        for j, chip in enumerate(chips):
            kj = 2 * chip[0] + chip[1]
            for a in range(n):
                copy(a, 3 + j, place(a, kj, 1 - c), me).wait_recv()
        for j, chip in enumerate(chips):
            kj = 2 * chip[0] + chip[1]
            for a in range(n):
                copy(a, j, place(a, k, c), (*chip, c)).wait_send()
                copy(a, 3 + j, place(a, kj, c), sibling).wait_send()

    return Rider(tuple(fulls), tuple(_sds(f.shape, BF16) for f in fulls), {a: a for a in range(n)},
                 (pltpu.SemaphoreType.DMA((6 * n,)), pltpu.SemaphoreType.DMA((6 * n,))), start, finish)


def _scatter_rider(ps, kinds, shard_cols):
    n = len(ps)

    def copies(ins, outs, sems):
        x, y, c, chips = _mesh_place()
        cps = []
        for j, chip in enumerate(chips):
            kj = 2 * chip[0] + chip[1]
            for a in range(n):
                C = shard_cols[a]
                src = ins[a].at[:, kj] if kinds[a] == "row" else ins[a].at[:, 0, :, pl.ds(pl.multiple_of(kj * C, LANES), C)]
                cps.append(_remote(src, outs[a].at[j], sems[0].at[3 * a + j], sems[1].at[3 * a + j], (*chip, c)))
        return cps

    def start(ins, outs, sems):
        for cp in copies(ins, outs, sems):
            cp.start()

    def finish(ins, outs, sems):
        cps = copies(ins, outs, sems)
        for cp in cps:
            cp.wait_recv()
        for cp in cps:
            cp.wait_send()

    out_shape = tuple(_sds((3, p.shape[0], p.shape[2], C), BF16) for p, C in zip(ps, shard_cols))
    return Rider(tuple(ps), out_shape, {}, (pltpu.SemaphoreType.DMA((3 * n,)), pltpu.SemaphoreType.DMA((3 * n,))),
                 start, finish)


def _run_rider(name, rider):
    n_in, n_out = len(rider.operands), len(rider.out_shape)

    def body(*refs):
        ins, outs, sems = refs[:n_in], refs[n_in:n_in + n_out], refs[n_in + n_out:]
        rider.start(ins, outs, sems)
        rider.finish(ins, outs, sems)

    return pl.pallas_call(
        body, name=name, out_shape=list(rider.out_shape), in_specs=[ANY] * n_in, out_specs=[ANY] * n_out,
        input_output_aliases=dict(rider.aliases), scratch_shapes=list(rider.sems),
        compiler_params=pltpu.CompilerParams(vmem_limit_bytes=VMEM_LIMIT_BYTES))(*rider.operands)


def _exchange_rider(g5s):
    n = len(g5s)

    def copies(ins, outs, sems):
        x, y, c, _ = _mesh_place()
        return [_remote(ins[a].at[:, :, 1 - c], outs[a], sems[0].at[a], sems[1].at[a], (x, y, 1 - c)) for a in range(n)]

    def start(ins, outs, sems):
        for cp in copies(ins, outs, sems):
            cp.start()

    def finish(ins, outs, sems):
        cps = copies(ins, outs, sems)
        for cp in cps:
            cp.wait_recv()
        for cp in cps:
            cp.wait_send()

    out_shape = tuple(_sds((g.shape[0], g.shape[1], g.shape[3], g.shape[4]), BF16) for g in g5s)
    return Rider(tuple(g5s), out_shape, {}, (pltpu.SemaphoreType.DMA((n,)), pltpu.SemaphoreType.DMA((n,))), start, finish)


def _share_rider(fs):
    n = len(fs)

    def start(_, outs, sems):
        x, y, c, _p = _mesh_place()
        for a in range(n):
            mine = outs[a].at[:, c]
            _remote(mine, mine, sems[0].at[a], sems[1].at[a], (x, y, 1 - c)).start()

    def finish(_, outs, sems):
        x, y, c, _p = _mesh_place()
        for a in range(n):
            theirs = outs[a].at[:, 1 - c]
            _remote(theirs, theirs, sems[0].at[a], sems[1].at[a], (x, y, c)).wait_recv()
        for a in range(n):
            mine = outs[a].at[:, c]
            _remote(mine, mine, sems[0].at[a], sems[1].at[a], (x, y, 1 - c)).wait_send()

    return Rider(tuple(fs), tuple(_sds(f.shape, F32) for f in fs), {a: a for a in range(n)},
                 (pltpu.SemaphoreType.DMA((n,)), pltpu.SemaphoreType.DMA((n,))), start, finish)


def _both_riders(r1, r2):
    ni, no, ns = len(r1.operands), len(r1.out_shape), len(r1.sems)
    aliases = dict(r1.aliases)
    aliases.update({ni + i: no + o for i, o in r2.aliases.items()})

    def start(ins, outs, sems):
        r1.start(ins[:ni], outs[:no], sems[:ns])
        r2.start(ins[ni:], outs[no:], sems[ns:])

    def finish(ins, outs, sems):
        r1.finish(ins[:ni], outs[:no], sems[:ns])
        r2.finish(ins[ni:], outs[no:], sems[ns:])

    return Rider(r1.operands + r2.operands, r1.out_shape + r2.out_shape, aliases, r1.sems + r2.sems, start, finish)


def _pack_rows(parts, lane_mult=1024):
    flat = jnp.concatenate([p.reshape(-1).astype(F32) for p in parts])
    n = -(-flat.shape[0] // (8 * lane_mult)) * lane_mult
    return jnp.pad(flat, (0, 8 * n - flat.shape[0])).reshape(8, n)


def _relu2(acc):
    r = jnp.maximum(acc, 0.0)
    return r, r * r


def _times_2r(acc, r):
    return (acc * (2.0 * r.astype(F32)),)


def kernel(x, c, positions, w_mod, b_mod, norm_g, mla_w_in, mla_g_q, mla_g_kv, mla_w_uq, mla_w_ukv, mla_w_o, conv_w_in, conv_w, conv_w_out, mlp_w_up, mlp_w_down, loss_target, m_w_mod, m_b_mod, m_norm_g, m_mla_w_in, m_mla_g_q, m_mla_g_kv, m_mla_w_uq, m_mla_w_ukv, m_mla_w_o, m_conv_w_in, m_conv_w, m_conv_w_out, m_mlp_w_up, m_mlp_w_down, v_w_mod, v_b_mod, v_norm_g, v_mla_w_in, v_mla_g_q, v_mla_g_kv, v_mla_w_uq, v_mla_w_ukv, v_mla_w_o, v_conv_w_in, v_conv_w, v_conv_w_out, v_mlp_w_up, v_mlp_w_down):
    S, D = x.shape[1], x.shape[2]
    Dq = D // N_CHIPS
    ncol = w_mod.shape[2]
    n_mod = N_CHIPS * ncol // D
    F = mlp_w_up.shape[2] * N_CHIPS
    lat_dim = mla_w_in.shape[2]
    rank = mla_g_q.shape[1]
    H = mla_w_uq.shape[2]
    d_qk = mla_w_uq.shape[3]
    assert mla_g_kv.shape[1] == rank and lat_dim == 2 * rank + QK_ROPE and d_qk == QK_NOPE + QK_ROPE
    assert mla_w_ukv.shape[3] == QK_NOPE + V_HEAD and x.shape[0] == 1 and n_mod == 6
    assert norm_g.shape[0] == 2 and mla_w_in.shape[0] == 1 and conv_w_in.shape[0] == 1
    lat_pad = 2 * rank + LANES
    scale = float(d_qk) ** -0.5

    xi, yi, ci = lax.axis_index("x"), lax.axis_index("y"), lax.axis_index("c")
    chip = 2 * xi + yi
    dev = 2 * chip + ci
    c_idx = jnp.reshape(ci, (1,)).astype(jnp.int32)
    k_idx = jnp.reshape(chip, (1,)).astype(jnp.int32)

    n1 = D + 2 * D + 3 * Dq
    g1 = _small_allgather("gather_small_inputs", _pack_rows([c, norm_g, conv_w])).reshape(8, -1)
    c_all = g1[:, :D]
    by_chip = g1[0::2]
    norm_full = jnp.concatenate([by_chip[kk, D:3 * D].reshape(2, 4, Dq) for kk in range(N_CHIPS)], axis=-1)
    convw_full = jnp.concatenate([by_chip[kk, 3 * D:n1].reshape(3, Dq) for kk in range(N_CHIPS)], axis=-1)

    b_cols = lax.dynamic_slice(b_mod, (0, chip * ncol), (2, ncol)).reshape(2, 1, ncol)
    cond_all = _silu(c_all)
    mod_cols = _mod_fwd(cond_all, w_mod, b_cols)
    g2 = _small_allgather("gather_mod", _pack_rows([mod_cols]))
    g2 = g2.reshape(8, -1)[0::2, :2 * 8 * ncol].reshape(N_CHIPS, 2, 8, ncol)
    mod_all = jnp.transpose(g2, (2, 1, 0, 3)).reshape(8, 2, n_mod * D)
    mod_me = lax.dynamic_index_in_dim(mod_all, dev, axis=0, keepdims=False)
    mods = [[mod_me[l, i * D:(i + 1) * D].reshape(1, D) for i in range(n_mod)] for l in range(2)]
    ng = [[norm_full[l, i].reshape(1, D) for i in range(4)] for l in range(2)]

    pos = positions[0].astype(F32)
    inv_freq = ROPE_THETA ** (-jnp.arange(0, QK_ROPE, 2, dtype=F32) / QK_ROPE)
    ang = pos[:, None] * inv_freq
    cos, sin = jnp.cos(ang), jnp.sin(ang)
    zero = jnp.zeros_like(cos)
    rope_tabs = (jnp.concatenate([cos, cos, zero, zero], axis=1),
                 jnp.concatenate([-sin, zero, zero, zero], axis=1),
                 jnp.concatenate([zero, sin, zero, zero], axis=1))

    weights = [("mla_w_in", mla_w_in, "row"), ("mla_w_uq", mla_w_uq.reshape(1, rank // N_CHIPS, H * d_qk), "row"),
               ("mla_w_ukv", mla_w_ukv.reshape(1, rank // N_CHIPS, H * QK_PAD), "row"), ("mla_w_o", mla_w_o, "row"),
               ("conv_w_in", conv_w_in, "col"), ("conv_w_out", conv_w_out, "row"),
               ("mlp_w_up", mlp_w_up, "col"), ("mlp_w_down", mlp_w_down, "row")]
    kinds = [k for _, _, k in weights]
    shard_shapes = [w.shape for _, w, _ in weights]
    shard_cols = [s[2] for s in shard_shapes]
    casted = [_cast_into_full("cast_" + nm, w, kind, k_idx) for nm, w, kind in weights]
    W_IN, W_UQ, W_UKV, W_O, W_CIN, W_COUT, W_UP, W_DOWN = range(8)
    mla_idx = [W_IN, W_UQ, W_UKV, W_O]

    def view(i, buf):
        L, R, C = shard_shapes[i]
        return buf.reshape((L, N_CHIPS * R, C) if kinds[i] == "row" else (L, R, N_CHIPS * C))

    def gather_of(bufs, idx, layers=None):
        return _gather_rider(bufs, [kinds[i] for i in idx], [shard_cols[i] for i in idx], layers)

    def scatter_of(ps, idx):
        return _scatter_rider(ps, [kinds[i] for i in idx], [shard_cols[i] for i in idx])

    def halves(items):
        g5s = []
        for _, i, g in items:
            _, R, C = shard_shapes[i]
            g5s.append(g.reshape((1, N_CHIPS, 2, R // 2, C) if kinds[i] == "row" else (1, 1, 2, R // 2, N_CHIPS * C)))
        return g5s

    def pair_sums(items, g5s, ras):
        return [_pair_sum("pair_sum_" + nm, g5, ra, c_idx) for (nm, _, _), g5, ra in zip(items, g5s, ras)]

    got = _run_rider("gather_weights_mla", gather_of([casted[i] for i in mla_idx], mla_idx))
    w_in_p = jnp.pad(view(W_IN, got[0])[0], ((0, 0), (0, lat_pad - lat_dim)))
    w_q_p = jnp.pad(view(W_UQ, got[1])[0].reshape(rank, H, d_qk), ((0, 0), (0, 0), (0, QK_PAD - d_qk))).reshape(rank, H * QK_PAD)
    w_ukv, w_o = view(W_UKV, got[2])[0], view(W_O, got[3])[0]
    HV = H * V_HEAD

    def layer_b(l, transposed):
        if transposed:
            return lambda tm, tn, tk: pl.BlockSpec((None, tn, tk), lambda i, j, k: (l, j, k))
        return lambda tm, tn, tk: pl.BlockSpec((None, tk, tn), lambda i, j, k: (l, k, j))

    def mlp_up(tag, l, h, w, rider=NO_RIDER):
        return _mm("mlp_up_" + tag, h, w, "nn", S, F, D, [_sds((S, F), BF16)] * 2, epilogue=_relu2,
                   b_spec=layer_b(l, False), rider=rider)

    def mlp_down(tag, l, a2, w, rider=NO_RIDER):
        return _mm("mlp_down_" + tag, a2, w, "nn", S, D, F, [_sds((S, D), F32)], b_spec=layer_b(l, False), rider=rider)

    def mlp_bwd(tag, l, h, r, a2, dy, rider=NO_RIDER):
        res = _mm("mlp_down_dx_" + tag, dy, w_down, "nt", S, F, D, [_sds((S, F), BF16)], epilogue=_times_2r,
                  b_spec=layer_b(l, True), rider=rider,
                  extras=[(r, lambda tm, tn, tk: pl.BlockSpec((tm, tn), lambda i, j, k: (i, j)))])
        (da,), carried = res if rider.start is not None else (res, ())
        (dw_down,) = _mm("mlp_down_dw_" + tag, a2, dy, "tn", F, D, S, [_sds((F, D), BF16)])
        (dh,) = _mm("mlp_up_dx_" + tag, da, w_up, "nt", S, D, F, [_sds((S, D), F32)], b_spec=layer_b(l, True))
        (dw_up,) = _mm("mlp_up_dw_" + tag, h, da, "tn", D, F, S, [_sds((D, F), BF16)])
        return dh, dw_up, dw_down, carried

    x0 = x[0]
    sh1, sc1, gt1, sh2, sc2, gt2 = mods[0]
    (h1,) = _fwd_boundary("fwd_boundary_0", x0, None, None, None, ng[0][0], sc1, sh1)
    (lat,) = _mm("mla_in", h1, w_in_p, "nn", S, lat_pad, D, [_sds((S, lat_pad), F32)], tn=lat_pad)
    cq, ckv, kr = _latent_fwd(lat, mla_g_q, mla_g_kv, rope_tabs, rank)

    def rope_q(acc, cos_p, sin_lo, sin_hi):
        parts = []
        for hh in range(acc.shape[1] // QK_PAD):
            parts.append(acc[:, hh * QK_PAD:hh * QK_PAD + QK_NOPE])
            parts.append(_rope(acc[:, hh * QK_PAD + QK_NOPE:(hh + 1) * QK_PAD], cos_p, sin_lo, sin_hi))
        return (jnp.concatenate(parts, axis=1),)

    tab_extra = lambda tm, tn, tk: pl.BlockSpec((tm, LANES), lambda i, j, k: (i, 0))
    (q,) = _mm("mla_q", cq, w_q_p, "nn", S, H * QK_PAD, rank, [_sds((S, H * QK_PAD), BF16)], epilogue=rope_q,
               extras=[(t, tab_extra) for t in rope_tabs], tn=2 * QK_PAD)
    (kv,) = _mm("mla_kv", ckv, w_ukv, "nn", S, H * QK_PAD, rank, [_sds((S, H * QK_PAD), BF16)])
    rest_idx = [W_COUT, W_UP, W_DOWN]
    o, lse, (cout_buf, up_buf, down_buf) = _attn_fwd_tri(
        q, kv, kr, H, scale, gather_of([casted[i] for i in rest_idx], rest_idx, [None, 0, 0]))
    (y1,) = _mm("mla_out", o, w_o, "nn", S, D, HV, [_sds((S, D), F32)])
    x1, h2 = _fwd_boundary("fwd_boundary_1", x0, y1, gt1, ng[0][1], ng[0][2], sc2, sh2)
    (r2, a2), (cin_buf,) = mlp_up("0", 0, h2, view(W_UP, up_buf), gather_of([casted[W_CIN]], [W_CIN]))
    (y2,), (up_buf,) = mlp_down("0", 0, a2, view(W_DOWN, down_buf), gather_of([up_buf], [W_UP], [1]))
    w_cin, w_cout, w_up = view(W_CIN, cin_buf)[0], view(W_COUT, cout_buf)[0], view(W_UP, up_buf)

    sh1b, sc1b, gt1b, sh2b, sc2b, gt2b = mods[1]
    x2, h3 = _fwd_boundary("fwd_boundary_2", x1, y2, gt2, ng[0][3], ng[1][0], sc1b, sh1b)
    nD = lambda tn: D // tn
    (proj3,) = _mm("conv_in", h3, w_cin, "nn", S, 3 * D, D, [_sds((3, S, D), F32)], tn=min(1024, D),
                   out_specs=[lambda tm, tn, tk: pl.BlockSpec((None, tm, tn), lambda i, j, k: (j // nD(tn), i, j % nD(tn)))])
    bz = _conv_fwd(proj3, convw_full)
    (y3,) = _mm("conv_out", bz, w_cout, "nn", S, D, D, [_sds((S, D), F32)])
    x3, h4 = _fwd_boundary("fwd_boundary_3", x2, y3, gt1b, ng[1][1], ng[1][2], sc2b, sh2b)
    (r4, a4), (down_buf,) = mlp_up("1", 1, h4, w_up, gather_of([down_buf], [W_DOWN], [1]))
    w_down = view(W_DOWN, down_buf)
    (y4,) = mlp_down("1", 1, a4, w_down)

    dx4, dy4, sums_l, loss_acc = _loss_boundary("loss_boundary", x3, y4, gt2b, ng[1][3], loss_target[0])
    loss = lax.psum(loss_acc[0, 0], ("x", "y", "c"))

    dh4, dw_up1, dw_down1, _ = mlp_bwd("1", 1, h4, r4, a4, dy4)
    items = [("mlp_w_up_1", W_UP, dw_up1), ("mlp_w_down_1", W_DOWN, dw_down1)]
    g5s = halves(items)
    dx3, dy3, sums_3, ras = _bwd_boundary("bwd_boundary_3", dx4, dh4, x3, y3, gt1b, ng[1][1], ng[1][2], sc2b,
                                          _exchange_rider(g5s))
    ps_up1, ps_down1 = pair_sums(items, g5s, ras)

    (dbz,) = _mm("conv_out_dx", dy3, w_cout, "nt", S, D, D, [_sds((S, D), F32)])
    (dw_cout,) = _mm("conv_out_dw", bz, dy3, "tn", D, D, S, [_sds((D, D), BF16)])
    dproj3, dconvw = _conv_bwd(dbz, proj3, convw_full)
    (dh3,), (rb_up1,) = _mm(
        "conv_in_dx", dproj3, w_cin, "nt", S, D, 3 * D, [_sds((S, D), F32)], tk=min(1024, D), rider=scatter_of([ps_up1], [W_UP]),
        a_spec=lambda tm, tn, tk: pl.BlockSpec((None, tm, tk), lambda i, j, k: (k // (D // tk), i, k % (D // tk))))
    (dw_cin,), (rb_down1,) = _mm(
        "conv_in_dw", h3, dproj3, "tn", D, 3 * D, S, [_sds((D, 3 * D), BF16)], tn=min(1024, D),
        rider=scatter_of([ps_down1], [W_DOWN]),
        b_spec=lambda tm, tn, tk: pl.BlockSpec((None, tk, tn), lambda i, j, k: (j // nD(tn), k, j % nD(tn))))
    items = [("conv_w_in", W_CIN, dw_cin), ("conv_w_out", W_COUT, dw_cout)]
    g5s = halves(items)
    dx2, dy2, sums_2, ras = _bwd_boundary("bwd_boundary_2", dx3, dh3, x2, y2, gt2, ng[0][3], ng[1][0], sc1b,
                                          _exchange_rider(g5s))
    ps_cin, ps_cout = pair_sums(items, g5s, ras)

    dh2, dw_up0, dw_down0, _ = mlp_bwd("0", 0, h2, r2, a2, dy2)
    items = [("mlp_w_up_0", W_UP, dw_up0), ("mlp_w_down_0", W_DOWN, dw_down0)]
    g5s = halves(items)
    dx1, dy1, sums_1, ras = _bwd_boundary("bwd_boundary_1", dx2, dh2, x1, y1, gt1, ng[0][1], ng[0][2], sc2,
                                          _exchange_rider(g5s))
    ps_up0, ps_down0 = pair_sums(items, g5s, ras)

    (do,) = _mm("mla_out_dx", dy1, w_o, "nt", S, HV, D, [_sds((S, HV), BF16)])
    (dw_o,) = _mm("mla_out_dw", o, dy1, "tn", HV, D, S, [_sds((HV, D), BF16)])
    dq, dkv, dkr, (rb_up0, rb_down0, rb_cin, rb_cout) = _attn_bwd_tri(
        q, kv, kr, o, do, lse, rope_tabs, H, scale,
        scatter_of([ps_up0, ps_down0, ps_cin, ps_cout], [W_UP, W_DOWN, W_CIN, W_COUT]))
    (dcq,) = _mm("mla_q_dx", dq, w_q_p, "nt", S, rank, H * QK_PAD, [_sds((S, rank), F32)])
    (dw_q_p,) = _mm("mla_q_dw", cq, dq, "tn", rank, H * QK_PAD, S, [_sds((rank, H * QK_PAD), BF16)])
    (dckv,) = _mm("mla_kv_dx", dkv, w_ukv, "nt", S, rank, H * QK_PAD, [_sds((S, rank), F32)])
    (dw_ukv,) = _mm("mla_kv_dw", ckv, dkv, "tn", rank, H * QK_PAD, S, [_sds((rank, H * QK_PAD), BF16)])
    dlat, sums_lat = _latent_bwd(lat, dcq, dckv, dkr, mla_g_q, mla_g_kv, rope_tabs, rank)
    (dh1,) = _mm("mla_in_dx", dlat, w_in_p, "nt", S, D, lat_pad, [_sds((S, D), F32)])
    (dw_in_p,) = _mm("mla_in_dw", h1, dlat, "tn", D, lat_pad, S, [_sds((D, lat_pad), BF16)], tn=lat_pad)
    dw_mla = [dw_in_p[:, :lat_dim], dw_q_p.reshape(rank, H, QK_PAD)[:, :, :d_qk].reshape(rank, H * d_qk), dw_ukv, dw_o]
    items = [(weights[i][0], i, g) for i, g in zip(mla_idx, dw_mla)]
    g5s = halves(items)
    grad_x, sums_0, ras = _bwd_boundary("bwd_boundary_0", dx1, dh1, x0, None, None, None, ng[0][0], sc1,
                                        _exchange_rider(g5s))
    ps_mla = pair_sums(items, g5s, ras)

    kc_idx = jnp.stack([chip, ci]).astype(jnp.int32)
    fs_rest = [_chip_sum("chip_sum_" + weights[i][0], p, rb, kc_idx, kinds[i])
               for i, p, rb in [(W_CIN, ps_cin, rb_cin), (W_COUT, ps_cout, rb_cout)]]
    for i, (p1, r1), (p0, r0) in [(W_UP, (ps_up1, rb_up1), (ps_up0, rb_up0)), (W_DOWN, (ps_down1, rb_down1), (ps_down0, rb_down0))]:
        f = _chip_sum("chip_sum_" + weights[i][0] + "_1", p1, r1, kc_idx, kinds[i], layer=1, n_layers=2)
        fs_rest.append(_chip_sum("chip_sum_" + weights[i][0] + "_0", p0, r0, kc_idx, kinds[i], layer=0, n_layers=2, prev=f))

    dmod0 = [sums_0[0], sums_0[1], sums_1[3], sums_1[0], sums_1[1], sums_2[3]]
    dmod1 = [sums_2[0], sums_2[1], sums_3[3], sums_3[0], sums_3[1], sums_l[3]]
    dng0 = [sums_0[2], sums_1[4], sums_1[2], sums_2[4]]
    dng1 = [sums_2[2], sums_3[4], sums_3[2], sums_l[4]]
    small = _pack_rows(dmod0 + dmod1 + dng0 + dng1 + [sums_lat[0], sums_lat[1], dconvw], lane_mult=LANES)
    gathered, total = _small_allgather("gather_small_grads", small, with_sum=True)
    n_dm = 2 * n_mod * D
    dmod_all = gathered.reshape(8, -1)[:, :n_dm].reshape(8, 2, n_mod * D)
    total = total.reshape(-1)
    g_b_mod = total[:n_dm].reshape(2, n_mod * D)
    g_norm = lax.dynamic_slice(total[n_dm:n_dm + 8 * D].reshape(2, 4, D), (0, 0, chip * Dq), (2, 4, Dq))
    off = n_dm + 8 * D
    g_gq = total[off:off + rank].reshape(1, rank)
    g_gkv = total[off + rank:off + 2 * rank].reshape(1, rank)
    off += 2 * rank
    g_convw = lax.dynamic_slice(total[off:off + 3 * D].reshape(1, 3, D), (0, 0, chip * Dq), (1, 3, Dq))

    dmod_cols = jnp.transpose(lax.dynamic_slice(dmod_all.reshape(8, 2, N_CHIPS, ncol), (0, 0, chip, 0), (8, 2, 1, ncol))
                              .reshape(8, 2, ncol), (1, 0, 2))
    g_w_mod, d_w_mod, nm_w_mod, nv_w_mod, carried = _adamw_mod(
        w_mod, cond_all.T, dmod_cols, m_w_mod, v_w_mod, _both_riders(scatter_of(ps_mla, mla_idx), _share_rider(fs_rest)))
    rbs_mla, finals_rest = carried[:len(mla_idx)], carried[len(mla_idx):]
    fs_mla = [_chip_sum("chip_sum_" + weights[i][0], p, rb, kc_idx, kinds[i]) for i, p, rb in zip(mla_idx, ps_mla, rbs_mla)]
    finals = list(_run_rider("grad_pair_share_mla", _share_rider(fs_mla))) + list(finals_rest)
    orig = [mla_w_in, mla_w_uq, mla_w_ukv, mla_w_o, conv_w_in, conv_w_out, mlp_w_up, mlp_w_down]
    big_grads = [f.reshape(w.shape) for f, w in zip(finals, orig)]

    names = ["b_mod", "norm_g", "mla_w_in", "mla_g_q", "mla_g_kv", "mla_w_uq", "mla_w_ukv", "mla_w_o",
             "conv_w_in", "conv_w", "conv_w_out", "mlp_w_up", "mlp_w_down"]
    ws = [b_mod, norm_g, mla_w_in, mla_g_q, mla_g_kv, mla_w_uq, mla_w_ukv, mla_w_o, conv_w_in, conv_w, conv_w_out,
          mlp_w_up, mlp_w_down]
    ms = [m_b_mod, m_norm_g, m_mla_w_in, m_mla_g_q, m_mla_g_kv, m_mla_w_uq, m_mla_w_ukv, m_mla_w_o, m_conv_w_in,
          m_conv_w, m_conv_w_out, m_mlp_w_up, m_mlp_w_down]
    vs = [v_b_mod, v_norm_g, v_mla_w_in, v_mla_g_q, v_mla_g_kv, v_mla_w_uq, v_mla_w_ukv, v_mla_w_o, v_conv_w_in,
          v_conv_w, v_conv_w_out, v_mlp_w_up, v_mlp_w_down]
    gs = [g_b_mod, g_norm, big_grads[0], g_gq, g_gkv, big_grads[1], big_grads[2], big_grads[3], big_grads[4],
          g_convw, big_grads[5], big_grads[6], big_grads[7]]
    grads, deltas, new_ms, new_vs = [g_w_mod], [d_w_mod], [nm_w_mod], [nv_w_mod]
    for nm, w, g, m, v in zip(names, ws, gs, ms, vs):
        d, nm_, nv_ = _adamw("adamw_" + nm, w, g, m, v)
        grads.append(g)
        deltas.append(d)
        new_ms.append(nm_)
        new_vs.append(nv_)
    return (loss, grad_x[None], *grads, *deltas, *new_ms, *new_vs)
```

```python
from typing import NamedTuple

import jax
import jax.numpy as jnp
from jax import lax
from jax.experimental import pallas as pl
from jax.experimental.pallas import tpu as pltpu

F32 = jnp.float32
BF16 = jnp.bfloat16
NORM_EPS = 1e-6
ROPE_THETA = 10000.0
QK_NOPE = 128
QK_ROPE = 64
V_HEAD = 128
LANES = 128
QK_PAD = QK_NOPE + LANES
ADAM_LR, ADAM_B1, ADAM_B2, ADAM_EPS, ADAM_WD, ADAM_STEP = 0.001, 0.9, 0.999, 1e-08, 0.01, 10
VMEM_LIMIT_BYTES = 48 * 1024 * 1024
N_CHIPS = 4
MESH_ID = pl.DeviceIdType.MESH
ANY = pl.BlockSpec(memory_space=pl.ANY)
NEG_INF = float("-inf")

DIMS_NN = (((1,), (0,)), ((), ()))
DIMS_NT = (((1,), (1,)), ((), ()))
DIMS_TN = (((0,), (0,)), ((), ()))


def _cparams(*sem):
    return pltpu.CompilerParams(dimension_semantics=sem, vmem_limit_bytes=VMEM_LIMIT_BYTES)


def _row_tile(rows, row_bytes, limit=2 * 1024 * 1024, mult=16):
    if rows * row_bytes <= limit or rows % mult:
        return rows
    best = mult
    t = mult
    while t <= rows:
        if rows % t == 0 and t * row_bytes <= limit:
            best = t
        t += mult
    return best


def _rms(v):
    return lax.rsqrt(jnp.mean(v * v, axis=-1, keepdims=True) + NORM_EPS)


class Rider(NamedTuple):
    operands: tuple
    out_shape: tuple
    aliases: dict
    sems: tuple
    start: object
    finish: object


NO_RIDER = Rider((), (), {}, (), None, None)


def _mm(name, a, b, mode, M, N, K, outs, *, a_spec=None, b_spec=None, out_specs=None, epilogue=None,
        extras=(), rider=NO_RIDER, tm=1024, tn=1024, tk=2048):
    tm, tn, tk = min(tm, M), min(tn, N), min(tk, K)
    assert M % tm == 0 and N % tn == 0 and K % tk == 0, (name, M, N, K)
    nk = K // tk
    if a_spec is None:
        a_spec = {"nn": pl.BlockSpec((tm, tk), lambda i, j, k: (i, k)),
                  "nt": pl.BlockSpec((tm, tk), lambda i, j, k: (i, k)),
                  "tn": pl.BlockSpec((tk, tm), lambda i, j, k: (k, i))}[mode]
    else:
        a_spec = a_spec(tm, tn, tk)
    if b_spec is None:
        b_spec = {"nn": pl.BlockSpec((tk, tn), lambda i, j, k: (k, j)),
                  "nt": pl.BlockSpec((tn, tk), lambda i, j, k: (j, k)),
                  "tn": pl.BlockSpec((tk, tn), lambda i, j, k: (k, j))}[mode]
    else:
        b_spec = b_spec(tm, tn, tk)
    if out_specs is None:
        out_specs = [pl.BlockSpec((tm, tn), lambda i, j, k: (i, j)) for _ in outs]
    else:
        out_specs = [s(tm, tn, tk) for s in out_specs]
    dims = {"nn": DIMS_NN, "nt": DIMS_NT, "tn": DIMS_TN}[mode]
    ne, no = len(extras), len(outs)
    n_ri, n_ro = len(rider.operands), len(rider.out_shape)
    grid = (M // tm, N // tn, nk)

    def body(*refs):
        a_ref, b_ref = refs[0], refs[1]
        ex = refs[2:2 + ne]
        r_in = refs[2 + ne:2 + ne + n_ri]
        o = refs[2 + ne + n_ri:2 + ne + n_ri + no]
        r_out = refs[2 + ne + n_ri + no:2 + ne + n_ri + no + n_ro]
        scratch = refs[2 + ne + n_ri + no + n_ro:]
        r_sems = scratch[1:] if nk > 1 else scratch
        ii, jj, kk = pl.program_id(0), pl.program_id(1), pl.program_id(2)

        if rider.start is not None:
            @pl.when((ii == 0) & (jj == 0) & (kk == 0))
            def _():
                rider.start(r_in, r_out, r_sems)

        part = lax.dot_general(a_ref[...].astype(BF16), b_ref[...].astype(BF16), dims,
                               preferred_element_type=F32)

        def finish(total):
            vals = epilogue(total, *[e[...] for e in ex]) if epilogue is not None else (total,)
            for r, v in zip(o, vals):
                r[...] = v.astype(r.dtype)

        if nk == 1:
            finish(part)
        else:
            acc = scratch[0]

            @pl.when(kk == 0)
            def _():
                acc[...] = part

            @pl.when(kk > 0)
            def _():
                acc[...] += part

            @pl.when(kk == nk - 1)
            def _():
                finish(acc[...])

        if rider.finish is not None:
            @pl.when((ii == grid[0] - 1) & (jj == grid[1] - 1) & (kk == nk - 1))
            def _():
                rider.finish(r_in, r_out, r_sems)

    operands = [a, b] + [e[0] for e in extras] + list(rider.operands)
    in_specs = [a_spec, b_spec] + [e[1](tm, tn, tk) for e in extras] + [ANY] * n_ri
    hosted = rider.start is not None
    res = pl.pallas_call(
        body, name=name, grid=grid,
        in_specs=in_specs, out_specs=out_specs + [ANY] * n_ro, out_shape=list(outs) + list(rider.out_shape),
        scratch_shapes=([pltpu.VMEM((tm, tn), F32)] if nk > 1 else []) + list(rider.sems),
        input_output_aliases={2 + ne + i: no + r for i, r in rider.aliases.items()},
        compiler_params=_cparams(*(("arbitrary",) * 3 if hosted else ("parallel", "parallel", "arbitrary"))),
    )(*operands)
    return (res[:no], res[no:]) if hosted else res


def _sds(shape, dtype):
    return jax.ShapeDtypeStruct(tuple(shape), dtype)


def _rope(t, cos_p, sin_lo, sin_hi):
    return t * cos_p + pltpu.roll(t, LANES - QK_ROPE // 2, 1) * sin_lo + pltpu.roll(t, QK_ROPE // 2, 1) * sin_hi


def _rope_t(d, cos_p, sin_lo, sin_hi):
    return d * cos_p + pltpu.roll(d * sin_lo, QK_ROPE // 2, 1) + pltpu.roll(d * sin_hi, LANES - QK_ROPE // 2, 1)


def _vec_spec(d):
    return pl.BlockSpec((1, d), lambda i: (0, 0))


def _fwd_boundary(name, x_prev, y, gate, ng_post, ng_pre, sc, sh):
    S, D = x_prev.shape
    ts = min(256, S)
    has_y = y is not None
    row = pl.BlockSpec((ts, D), lambda i: (i, 0))

    def body(*refs):
        if has_y:
            x_ref, y_ref, g_ref, ngp_ref, ngn_ref, sc_ref, sh_ref, xo_ref, h_ref = refs
            yv = y_ref[...]
            xn = x_ref[...] + g_ref[...] * (yv * _rms(yv) * ngp_ref[...])
            xo_ref[...] = xn
        else:
            x_ref, ngn_ref, sc_ref, sh_ref, h_ref = refs
            xn = x_ref[...]
        hn = xn * _rms(xn) * ngn_ref[...]
        h_ref[...] = (hn * (1.0 + sc_ref[...]) + sh_ref[...]).astype(BF16)

    vec = _vec_spec(D)
    if has_y:
        operands = (x_prev, y, gate, ng_post, ng_pre, sc, sh)
        in_specs = [row, row, vec, vec, vec, vec, vec]
        out_shape = [_sds((S, D), F32), _sds((S, D), BF16)]
        out_specs = [row, row]
    else:
        operands = (x_prev, ng_pre, sc, sh)
        in_specs = [row, vec, vec, vec]
        out_shape = [_sds((S, D), BF16)]
        out_specs = [row]
    return pl.pallas_call(body, name=name, grid=(S // ts,), in_specs=in_specs, out_specs=out_specs,
                          out_shape=out_shape, compiler_params=_cparams("parallel"))(*operands)


def _acc_rows(sums_ref, rows):
    for r, v in rows:
        sums_ref[r:r + 1, :] += jnp.sum(v, axis=0, keepdims=True)


def _post_norm_bwd(dxt, yv, gate, ng_post, sums_ref, dy_ref):
    r1 = _rms(yv)
    yhat = yv * r1
    dn = dxt * gate
    u = dn * ng_post
    dy = r1 * (u - yhat * jnp.mean(u * yhat, axis=-1, keepdims=True))
    dy_ref[...] = dy.astype(dy_ref.dtype)
    _acc_rows(sums_ref, [(3, dxt * (yhat * ng_post)), (4, dn * yhat)])


def _loss_boundary(name, x_prev, y, gate, ng_post, target):
    S, D = x_prev.shape
    ts = min(256, S)
    row = pl.BlockSpec((ts, D), lambda i: (i, 0))
    vec = _vec_spec(D)

    def body(x_ref, y_ref, g_ref, ngp_ref, t_ref, dx_ref, dy_ref, sums_ref, loss_ref):
        @pl.when(pl.program_id(0) == 0)
        def _():
            sums_ref[...] = jnp.zeros_like(sums_ref)
            loss_ref[...] = jnp.zeros_like(loss_ref)

        yv = y_ref[...]
        xf = x_ref[...] + g_ref[...] * (yv * _rms(yv) * ngp_ref[...])
        err = xf - t_ref[...]
        loss_ref[...] += 0.5 * jnp.sum(jnp.mean(err * err, axis=-1, keepdims=True))
        dxt = err / D
        dx_ref[...] = dxt
        _post_norm_bwd(dxt, yv, g_ref[...], ngp_ref[...], sums_ref, dy_ref)

    return pl.pallas_call(
        body, name=name, grid=(S // ts,),
        in_specs=[row, row, vec, vec, row],
        out_specs=[row, row, pl.BlockSpec((8, D), lambda i: (0, 0)), pl.BlockSpec((8, LANES), lambda i: (0, 0))],
        out_shape=[_sds((S, D), F32), _sds((S, D), BF16), _sds((8, D), F32), _sds((8, LANES), F32)],
        compiler_params=_cparams("arbitrary"))(x_prev, y, gate, ng_post, target)


def _bwd_boundary(name, dx_new, dh, x_new, y, gate, ng_post, ng_pre, sc, rider=NO_RIDER):
    S, D = x_new.shape
    ts = min(256, S)
    has_y = y is not None
    row = pl.BlockSpec((ts, D), lambda i: (i, 0))
    vec = _vec_spec(D)
    n_in, n_out = (8, 3) if has_y else (5, 2)
    n_ri, n_ro = len(rider.operands), len(rider.out_shape)

    def body(*refs):
        r_in = refs[n_in:n_in + n_ri]
        r_out = refs[n_in + n_ri + n_out:n_in + n_ri + n_out + n_ro]
        r_sems = refs[n_in + n_ri + n_out + n_ro:]
        own = refs[:n_in] + refs[n_in + n_ri:n_in + n_ri + n_out]
        if has_y:
            dxn_ref, dh_ref, x_ref, y_ref, g_ref, ngp_ref, ngn_ref, sc_ref, dxo_ref, dy_ref, sums_ref = own
        else:
            dxn_ref, dh_ref, x_ref, ngn_ref, sc_ref, dxo_ref, sums_ref = own

        @pl.when(pl.program_id(0) == 0)
        def _():
            sums_ref[...] = jnp.zeros_like(sums_ref)
            if rider.start is not None:
                rider.start(r_in, r_out, r_sems)

        xv = x_ref[...]
        dhv = dh_ref[...]
        ngn = ngn_ref[...]
        r2 = _rms(xv)
        xhat = xv * r2
        dn_pre = dhv * (1.0 + sc_ref[...])
        u2 = dn_pre * ngn
        dxt = dxn_ref[...] + r2 * (u2 - xhat * jnp.mean(u2 * xhat, axis=-1, keepdims=True))
        dxo_ref[...] = dxt
        _acc_rows(sums_ref, [(0, dhv), (1, dhv * (xhat * ngn)), (2, dn_pre * xhat)])
        if has_y:
            _post_norm_bwd(dxt, y_ref[...], g_ref[...], ngp_ref[...], sums_ref, dy_ref)

        if rider.finish is not None:
            @pl.when(pl.program_id(0) == S // ts - 1)
            def _():
                rider.finish(r_in, r_out, r_sems)

    sums_spec = pl.BlockSpec((8, D), lambda i: (0, 0))
    if has_y:
        operands = (dx_new, dh, x_new, y, gate, ng_post, ng_pre, sc)
        in_specs = [row, row, row, row, vec, vec, vec, vec]
        out_shape = [_sds((S, D), F32), _sds((S, D), BF16), _sds((8, D), F32)]
        out_specs = [row, row, sums_spec]
    else:
        operands = (dx_new, dh, x_new, ng_pre, sc)
        in_specs = [row, row, row, vec, vec]
        out_shape = [_sds((S, D), F32), _sds((8, D), F32)]
        out_specs = [row, sums_spec]
    res = pl.pallas_call(
        body, name=name, grid=(S // ts,), in_specs=in_specs + [ANY] * n_ri, out_specs=out_specs + [ANY] * n_ro,
        out_shape=out_shape + list(rider.out_shape), scratch_shapes=list(rider.sems),
        input_output_aliases={n_in + i: n_out + o for i, o in rider.aliases.items()},
        compiler_params=_cparams("arbitrary"))(*operands, *rider.operands)
    return (*res[:n_out], res[n_out:])


def _latent_fwd(lat, g_q, g_kv, rope_tabs, rank):
    S, W = lat.shape
    ts = min(256, S)
    tab = pl.BlockSpec((ts, LANES), lambda i: (i, 0))

    def body(lat_ref, gq_ref, gkv_ref, cos_ref, slo_ref, shi_ref, cq_ref, ckv_ref, kr_ref):
        lq = lat_ref[:, 0:rank]
        lkv = lat_ref[:, rank:2 * rank]
        cq_ref[...] = (lq * _rms(lq) * gq_ref[...]).astype(BF16)
        ckv_ref[...] = (lkv * _rms(lkv) * gkv_ref[...]).astype(BF16)
        kr_ref[...] = _rope(lat_ref[:, 2 * rank:W], cos_ref[...], slo_ref[...], shi_ref[...]).astype(BF16)

    return pl.pallas_call(
        body, name="mla_latent_fwd", grid=(S // ts,),
        in_specs=[pl.BlockSpec((ts, W), lambda i: (i, 0)), _vec_spec(rank), _vec_spec(rank), tab, tab, tab],
        out_specs=[pl.BlockSpec((ts, rank), lambda i: (i, 0)), pl.BlockSpec((ts, rank), lambda i: (i, 0)), tab],
        out_shape=[_sds((S, rank), BF16), _sds((S, rank), BF16), _sds((S, LANES), BF16)],
        compiler_params=_cparams("parallel"))(lat, g_q, g_kv, *rope_tabs)


def _latent_bwd(lat, dcq, dckv, dkr, g_q, g_kv, rope_tabs, rank):
    S, W = lat.shape
    ts = min(256, S)
    tab = pl.BlockSpec((ts, LANES), lambda i: (i, 0))
    half = pl.BlockSpec((ts, rank), lambda i: (i, 0))

    def body(lat_ref, dcq_ref, dckv_ref, dkr_ref, gq_ref, gkv_ref, cos_ref, slo_ref, shi_ref, dlat_ref, sums_ref):
        @pl.when(pl.program_id(0) == 0)
        def _():
            sums_ref[...] = jnp.zeros_like(sums_ref)

        def norm_bwd(v, dn, g, r):
            rr = _rms(v)
            vhat = v * rr
            u = dn * g
            sums_ref[r:r + 1, :] += jnp.sum(dn * vhat, axis=0, keepdims=True)
            return rr * (u - vhat * jnp.mean(u * vhat, axis=-1, keepdims=True))

        dlat_ref[:, 0:rank] = norm_bwd(lat_ref[:, 0:rank], dcq_ref[...], gq_ref[...], 0).astype(BF16)
        dlat_ref[:, rank:2 * rank] = norm_bwd(lat_ref[:, rank:2 * rank], dckv_ref[...], gkv_ref[...], 1).astype(BF16)
        dlat_ref[:, 2 * rank:W] = _rope_t(dkr_ref[...], cos_ref[...], slo_ref[...], shi_ref[...]).astype(BF16)

    return pl.pallas_call(
        body, name="mla_latent_bwd", grid=(S // ts,),
        in_specs=[pl.BlockSpec((ts, W), lambda i: (i, 0)), half, half, tab, _vec_spec(rank), _vec_spec(rank),
                  tab, tab, tab],
        out_specs=[pl.BlockSpec((ts, W), lambda i: (i, 0)), pl.BlockSpec((8, rank), lambda i: (0, 0))],
        out_shape=[_sds((S, W), BF16), _sds((8, rank), F32)],
        compiler_params=_cparams("arbitrary"))(lat, dcq, dckv, dkr, g_q, g_kv, *rope_tabs)


def _attn_tiles(S):
    t = min(512, S)
    return t, S // t


def _causal_mask(t):
    return lax.broadcasted_iota(jnp.int32, (t, t), 1) <= lax.broadcasted_iota(jnp.int32, (t, t), 0)


def _attn_fwd(q, kv, kr, heads, scale, rider=NO_RIDER):
    S = q.shape[0]
    t, nb = _attn_tiles(S)
    G = 2 if heads % 2 == 0 else 1
    n_ri, n_ro = len(rider.operands), len(rider.out_shape)

    def body(*refs):
        q_ref, kv_ref, kr_ref = refs[:3]
        r_in = refs[3:3 + n_ri]
        o_ref, lse_ref = refs[3 + n_ri:5 + n_ri]
        r_out = refs[5 + n_ri:5 + n_ri + n_ro]
        m_scr, acc_scr = refs[5 + n_ri + n_ro:7 + n_ri + n_ro]
        r_sems = refs[7 + n_ri + n_ro:]
        h, qi, ki = pl.program_id(0), pl.program_id(1), pl.program_id(2)

        if rider.start is not None:
            @pl.when((h == 0) & (qi == 0) & (ki == 0))
            def _():
                rider.start(r_in, r_out, r_sems)

        @pl.when(ki == 0)
        def _():
            m_scr[...] = jnp.full_like(m_scr, NEG_INF)
            acc_scr[...] = jnp.zeros_like(acc_scr)

        def step(diagonal):
            ones = jnp.ones((t, LANES), BF16)
            for g in range(G):
                kcat = jnp.concatenate([kv_ref[:, g * QK_PAD:g * QK_PAD + QK_NOPE], kr_ref[...]], axis=1)
                vext = jnp.concatenate([kv_ref[:, g * QK_PAD + QK_NOPE:(g + 1) * QK_PAD], ones], axis=1)
                s = lax.dot_general(q_ref[:, g * QK_PAD:(g + 1) * QK_PAD], kcat, DIMS_NT,
                                    preferred_element_type=F32) * scale
                if diagonal:
                    s = jnp.where(_causal_mask(t), s, NEG_INF)
                m_prev = m_scr[g]
                m_new = jnp.maximum(m_prev, jnp.max(s, axis=-1, keepdims=True))
                alpha = jnp.exp(m_prev - m_new)
                p = jnp.exp(s - jnp.tile(m_new, (1, t // LANES)))
                acc_scr[g] = jnp.tile(alpha, (1, 2)) * acc_scr[g] + lax.dot_general(
                    p.astype(BF16), vext, DIMS_NN, preferred_element_type=F32)
                m_scr[g] = m_new

        @pl.when(ki < qi)
        def _():
            step(False)

        @pl.when(ki == qi)
        def _():
            step(True)

        @pl.when(ki == nb - 1)
        def _():
            for g in range(G):
                acc = acc_scr[g]
                o_ref[:, g * V_HEAD:(g + 1) * V_HEAD] = (acc[:, 0:V_HEAD] / acc[:, V_HEAD:2 * V_HEAD]).astype(BF16)
                lse_ref[g] = m_scr[g] + jnp.log(acc[:, V_HEAD:2 * V_HEAD])

        if rider.finish is not None:
            @pl.when((h == heads // G - 1) & (qi == nb - 1) & (ki == nb - 1))
            def _():
                rider.finish(r_in, r_out, r_sems)

    res = pl.pallas_call(
        body, name="mla_attn_fwd", grid=(heads // G, nb, nb),
        in_specs=[pl.BlockSpec((t, G * QK_PAD), lambda h, qi, ki: (qi, h)),
                  pl.BlockSpec((t, G * QK_PAD), lambda h, qi, ki: (jnp.minimum(ki, qi), h)),
                  pl.BlockSpec((t, LANES), lambda h, qi, ki: (jnp.minimum(ki, qi), 0))] + [ANY] * n_ri,
        out_specs=[pl.BlockSpec((t, G * V_HEAD), lambda h, qi, ki: (qi, h)),
                   pl.BlockSpec((G, t, LANES), lambda h, qi, ki: (h, qi, 0))] + [ANY] * n_ro,
        out_shape=[_sds((S, heads * V_HEAD), BF16), _sds((heads, S, LANES), F32)] + list(rider.out_shape),
        scratch_shapes=[pltpu.VMEM((G, t, LANES), F32), pltpu.VMEM((G, t, 2 * V_HEAD), F32)] + list(rider.sems),
        input_output_aliases={3 + i: 2 + o for i, o in rider.aliases.items()},
        compiler_params=_cparams("arbitrary", "arbitrary", "arbitrary"))(q, kv, kr, *rider.operands)
    return res[0], res[1], res[2:]


def _attn_delta(o, do, heads):
    S = o.shape[0]
    t, nb = _attn_tiles(S)

    def body(o_ref, do_ref, out_ref):
        d = jnp.sum(do_ref[...].astype(F32) * o_ref[...].astype(F32), axis=-1, keepdims=True)
        out_ref[...] = jnp.broadcast_to(d, (t, LANES))

    blk = pl.BlockSpec((t, V_HEAD), lambda h, i: (i, h))
    return pl.pallas_call(
        body, name="mla_attn_delta", grid=(heads, nb), in_specs=[blk, blk],
        out_specs=pl.BlockSpec((None, t, LANES), lambda h, i: (h, i, 0)),
        out_shape=_sds((heads, S, LANES), F32), compiler_params=_cparams("parallel", "parallel"))(o, do)


def _attn_bwd(q, kv, kr, delta, do, lse, rope_tabs, heads, scale, rider=NO_RIDER):
    S = q.shape[0]
    t, nb = _attn_tiles(S)
    n_ri, n_ro = len(rider.operands), len(rider.out_shape)
    rep = t // LANES

    def body(*refs):
        q_ref, kv_ref, kr_ref, delta_ref, do_ref, lse_ref, cos_ref, slo_ref, shi_ref = refs[:9]
        r_in = refs[9:9 + n_ri]
        dq_ref, dkv_ref, dkr_ref = refs[9 + n_ri:12 + n_ri]
        r_out = refs[12 + n_ri:12 + n_ri + n_ro]
        dq_scr, dk_scr, dv_scr, dkr_scr = refs[12 + n_ri + n_ro:16 + n_ri + n_ro]
        r_sems = refs[16 + n_ri + n_ro:]
        h, ki, qi = pl.program_id(0), pl.program_id(1), pl.program_id(2)
        q_rows = pl.ds(pl.multiple_of(qi * t, t), t)
        k_rows = pl.ds(pl.multiple_of(ki * t, t), t)

        if rider.start is not None:
            @pl.when((h == 0) & (ki == 0) & (qi == 0))
            def _():
                rider.start(r_in, r_out, r_sems)

        @pl.when((ki == 0) & (qi == 0))
        def _():
            dq_scr[...] = jnp.zeros_like(dq_scr)

        @pl.when((h == 0) & (ki == 0) & (qi == 0))
        def _():
            dkr_scr[...] = jnp.zeros_like(dkr_scr)

        @pl.when(qi == 0)
        def _():
            dk_scr[...] = jnp.zeros_like(dk_scr)
            dv_scr[...] = jnp.zeros_like(dv_scr)

        def step(diagonal):
            qv = q_ref[...]
            kcat = jnp.concatenate([kv_ref[:, 0:QK_NOPE], kr_ref[...]], axis=1)
            s = lax.dot_general(qv, kcat, DIMS_NT, preferred_element_type=F32) * scale
            p = jnp.exp(s - jnp.tile(lse_ref[...], (1, rep)))
            if diagonal:
                p = jnp.where(_causal_mask(t), p, 0.0)
            dov = do_ref[...]
            dv_scr[...] += lax.dot_general(p.astype(BF16), dov, DIMS_TN, preferred_element_type=F32)
            dp = lax.dot_general(dov, kv_ref[:, QK_NOPE:QK_NOPE + V_HEAD], DIMS_NT, preferred_element_type=F32)
            ds = (p * (dp - jnp.tile(delta_ref[...], (1, rep))) * scale).astype(BF16)
            dk_scr[...] += lax.dot_general(ds, qv, DIMS_TN, preferred_element_type=F32)
            dq_scr[q_rows, :] += lax.dot_general(ds, kcat, DIMS_NN, preferred_element_type=F32)

        @pl.when(qi > ki)
        def _():
            step(False)

        @pl.when(qi == ki)
        def _():
            step(True)

        @pl.when(qi == nb - 1)
        def _():
            dkv_ref[...] = jnp.concatenate([dk_scr[:, 0:QK_NOPE], dv_scr[...]], axis=1).astype(BF16)
            dkr_scr[k_rows, :] += dk_scr[:, QK_NOPE:QK_PAD]

        @pl.when(ki == nb - 1)
        def _():
            dqv = dq_scr[q_rows, :]
            dq_ref[q_rows, :] = jnp.concatenate(
                [dqv[:, 0:QK_NOPE], _rope_t(dqv[:, QK_NOPE:QK_PAD], cos_ref[...], slo_ref[...], shi_ref[...])],
                axis=1).astype(BF16)

        @pl.when((h == heads - 1) & (ki == nb - 1) & (qi == nb - 1))
        def _():
            dkr_ref[...] = dkr_scr[...]
            if rider.finish is not None:
                rider.finish(r_in, r_out, r_sems)

    qmap = lambda h, ki, qi: (jnp.maximum(qi, ki), h)
    stat = pl.BlockSpec((None, t, LANES), lambda h, ki, qi: (h, jnp.maximum(qi, ki), 0))
    tab = pl.BlockSpec((t, LANES), lambda h, ki, qi: (qi, 0))
    res = pl.pallas_call(
        body, name="mla_attn_bwd", grid=(heads, nb, nb),
        in_specs=[pl.BlockSpec((t, QK_PAD), qmap),
                  pl.BlockSpec((t, QK_PAD), lambda h, ki, qi: (ki, h)),
                  pl.BlockSpec((t, LANES), lambda h, ki, qi: (ki, 0)),
                  stat,
                  pl.BlockSpec((t, V_HEAD), qmap),
                  stat,
                  tab, tab, tab] + [ANY] * n_ri,
        out_specs=[pl.BlockSpec((S, QK_PAD), lambda h, ki, qi: (0, h)),
                   pl.BlockSpec((t, QK_PAD), lambda h, ki, qi: (ki, h)),
                   pl.BlockSpec((S, LANES), lambda h, ki, qi: (0, 0))] + [ANY] * n_ro,
        out_shape=[_sds((S, heads * QK_PAD), BF16), _sds((S, heads * QK_PAD), BF16), _sds((S, LANES), F32)]
        + list(rider.out_shape),
        scratch_shapes=[pltpu.VMEM((S, QK_PAD), F32), pltpu.VMEM((t, QK_PAD), F32), pltpu.VMEM((t, V_HEAD), F32),
                        pltpu.VMEM((S, LANES), F32)] + list(rider.sems),
        input_output_aliases={9 + i: 3 + o for i, o in rider.aliases.items()},
        compiler_params=_cparams("arbitrary", "arbitrary", "arbitrary"))(q, kv, kr, delta, do, lse, *rope_tabs, *rider.operands)
    return res[0], res[1], res[2], res[3:]


def _causal_pairs(nb, q_major):
    if q_major:
        pairs = [(qi, ki) for qi in range(nb) for ki in range(qi + 1)]
    else:
        pairs = [(qi, ki) for ki in range(nb) for qi in range(ki, nb)]
    return jnp.array([p[0] for p in pairs], jnp.int32), jnp.array([p[1] for p in pairs], jnp.int32), len(pairs)


def _heads_per_step(heads):
    return 2 if heads % 2 == 0 else 1


def _attn_fwd_tri(q, kv, kr, heads, scale, rider=NO_RIDER):
    S = q.shape[0]
    t, nb = _attn_tiles(S)
    G = _heads_per_step(heads)
    q_tab, k_tab, n_pairs = _causal_pairs(nb, True)
    n_ri, n_ro = len(rider.operands), len(rider.out_shape)

    def body(qt_ref, kt_ref, *refs):
        q_ref, kv_ref, kr_ref = refs[:3]
        r_in = refs[3:3 + n_ri]
        o_ref, lse_ref = refs[3 + n_ri:5 + n_ri]
        r_out = refs[5 + n_ri:5 + n_ri + n_ro]
        m_scr, acc_scr = refs[5 + n_ri + n_ro:7 + n_ri + n_ro]
        r_sems = refs[7 + n_ri + n_ro:]
        h, p = pl.program_id(0), pl.program_id(1)
        qi, ki = qt_ref[p], kt_ref[p]

        if rider.start is not None:
            @pl.when((h == 0) & (p == 0))
            def _():
                rider.start(r_in, r_out, r_sems)

        @pl.when(ki == 0)
        def _():
            m_scr[...] = jnp.full_like(m_scr, NEG_INF)
            acc_scr[...] = jnp.zeros_like(acc_scr)

        def step(diagonal):
            ones = jnp.ones((t, LANES), BF16)
            for g in range(G):
                kcat = jnp.concatenate([kv_ref[:, g * QK_PAD:g * QK_PAD + QK_NOPE], kr_ref[...]], axis=1)
                vext = jnp.concatenate([kv_ref[:, g * QK_PAD + QK_NOPE:(g + 1) * QK_PAD], ones], axis=1)
                s = lax.dot_general(q_ref[:, g * QK_PAD:(g + 1) * QK_PAD], kcat, DIMS_NT,
                                    preferred_element_type=F32) * scale
                if diagonal:
                    s = jnp.where(_causal_mask(t), s, NEG_INF)
                m_prev = m_scr[g]
                m_new = jnp.maximum(m_prev, jnp.max(s, axis=-1, keepdims=True))
                alpha = jnp.exp(m_prev - m_new)
                pr = jnp.exp(s - jnp.tile(m_new, (1, t // LANES)))
                acc_scr[g] = jnp.tile(alpha, (1, 2)) * acc_scr[g] + lax.dot_general(
                    pr.astype(BF16), vext, DIMS_NN, preferred_element_type=F32)
                m_scr[g] = m_new

        @pl.when(ki < qi)
        def _():
            step(False)

        @pl.when(ki == qi)
        def _():
            step(True)
            for g in range(G):
                acc = acc_scr[g]
                o_ref[:, g * V_HEAD:(g + 1) * V_HEAD] = (acc[:, 0:V_HEAD] / acc[:, V_HEAD:2 * V_HEAD]).astype(BF16)
                lse_ref[g] = m_scr[g] + jnp.log(acc[:, V_HEAD:2 * V_HEAD])

        if rider.finish is not None:
            @pl.when((h == heads // G - 1) & (p == n_pairs - 1))
            def _():
                rider.finish(r_in, r_out, r_sems)

    res = pl.pallas_call(
        body, name="mla_attn_fwd",
        grid_spec=pltpu.PrefetchScalarGridSpec(
            num_scalar_prefetch=2, grid=(heads // G, n_pairs),
            in_specs=[pl.BlockSpec((t, G * QK_PAD), lambda h, p, qt, kt: (qt[p], h)),
                      pl.BlockSpec((t, G * QK_PAD), lambda h, p, qt, kt: (kt[p], h)),
                      pl.BlockSpec((t, LANES), lambda h, p, qt, kt: (kt[p], 0))] + [ANY] * n_ri,
            out_specs=[pl.BlockSpec((t, G * V_HEAD), lambda h, p, qt, kt: (qt[p], h)),
                       pl.BlockSpec((G, t, LANES), lambda h, p, qt, kt: (h, qt[p], 0))] + [ANY] * n_ro,
            scratch_shapes=[pltpu.VMEM((G, t, LANES), F32), pltpu.VMEM((G, t, 2 * V_HEAD), F32)] + list(rider.sems)),
        out_shape=[_sds((S, heads * V_HEAD), BF16), _sds((heads, S, LANES), F32)] + list(rider.out_shape),
        input_output_aliases={5 + i: 2 + o for i, o in rider.aliases.items()},
        compiler_params=_cparams("arbitrary", "arbitrary"))(q_tab, k_tab, q, kv, kr, *rider.operands)
    return res[0], res[1], res[2:]


def _attn_bwd_tri(q, kv, kr, o, do, lse, rope_tabs, heads, scale, rider=NO_RIDER):
    S = q.shape[0]
    t, nb = _attn_tiles(S)
    G = _heads_per_step(heads)
    q_tab, k_tab, n_pairs = _causal_pairs(nb, False)
    n_ri, n_ro = len(rider.operands), len(rider.out_shape)
    rep = t // LANES

    def body(qt_ref, kt_ref, *refs):
        q_ref, kv_ref, kr_ref, o_ref, do_ref, lse_ref, cos_ref, slo_ref, shi_ref = refs[:9]
        r_in = refs[9:9 + n_ri]
        dq_ref, dkv_ref, dkr_ref = refs[9 + n_ri:12 + n_ri]
        r_out = refs[12 + n_ri:12 + n_ri + n_ro]
        dq_scr, dk_scr, dv_scr, dkr_scr, delta_scr = refs[12 + n_ri + n_ro:17 + n_ri + n_ro]
        r_sems = refs[17 + n_ri + n_ro:]
        h, p = pl.program_id(0), pl.program_id(1)
        qi, ki = qt_ref[p], kt_ref[p]
        q_rows = pl.ds(pl.multiple_of(qi * t, t), t)
        k_rows = pl.ds(pl.multiple_of(ki * t, t), t)

        @pl.when(ki == 0)
        def _():
            for g in range(G):
                cols = slice(g * V_HEAD, (g + 1) * V_HEAD)
                d = jnp.sum(do_ref[:, cols].astype(F32) * o_ref[:, cols].astype(F32), axis=-1, keepdims=True)
                delta_scr[g, q_rows, :] = jnp.broadcast_to(d, (t, LANES))

        if rider.start is not None:
            @pl.when((h == 0) & (p == 0))
            def _():
                rider.start(r_in, r_out, r_sems)

        @pl.when(p == 0)
        def _():
            dq_scr[...] = jnp.zeros_like(dq_scr)

        @pl.when((h == 0) & (p == 0))
        def _():
            dkr_scr[...] = jnp.zeros_like(dkr_scr)

        @pl.when(qi == ki)
        def _():
            dk_scr[...] = jnp.zeros_like(dk_scr)
            dv_scr[...] = jnp.zeros_like(dv_scr)

        def step(diagonal):
            for g in range(G):
                qv = q_ref[:, g * QK_PAD:(g + 1) * QK_PAD]
                kcat = jnp.concatenate([kv_ref[:, g * QK_PAD:g * QK_PAD + QK_NOPE], kr_ref[...]], axis=1)
                s = lax.dot_general(qv, kcat, DIMS_NT, preferred_element_type=F32) * scale
                pr = jnp.exp(s - jnp.tile(lse_ref[g], (1, rep)))
                if diagonal:
                    pr = jnp.where(_causal_mask(t), pr, 0.0)
                dov = do_ref[:, g * V_HEAD:(g + 1) * V_HEAD]
                dv_scr[g] += lax.dot_general(pr.astype(BF16), dov, DIMS_TN, preferred_element_type=F32)
                dp = lax.dot_general(dov, kv_ref[:, g * QK_PAD + QK_NOPE:(g + 1) * QK_PAD], DIMS_NT,
                                     preferred_element_type=F32)
                ds = (pr * (dp - jnp.tile(delta_scr[g, q_rows, :], (1, rep))) * scale).astype(BF16)
                dk_scr[g] += lax.dot_general(ds, qv, DIMS_TN, preferred_element_type=F32)
                dq_scr[q_rows, g * QK_PAD:(g + 1) * QK_PAD] += lax.dot_general(ds, kcat, DIMS_NN,
                                                                               preferred_element_type=F32)

        @pl.when(qi > ki)
        def _():
            step(False)

        @pl.when(qi == ki)
        def _():
            step(True)
            for g in range(G):
                dqv = dq_scr[q_rows, g * QK_PAD:(g + 1) * QK_PAD]
                dq_ref[q_rows, g * QK_PAD:(g + 1) * QK_PAD] = jnp.concatenate(
                    [dqv[:, 0:QK_NOPE], _rope_t(dqv[:, QK_NOPE:QK_PAD], cos_ref[...], slo_ref[...], shi_ref[...])],
                    axis=1).astype(BF16)

        @pl.when(qi == nb - 1)
        def _():
            for g in range(G):
                dkv_ref[:, g * QK_PAD:(g + 1) * QK_PAD] = jnp.concatenate(
                    [dk_scr[g][:, 0:QK_NOPE], dv_scr[g]], axis=1).astype(BF16)
                dkr_scr[k_rows, :] += dk_scr[g][:, QK_NOPE:QK_PAD]

        @pl.when((h == heads // G - 1) & (p == n_pairs - 1))
        def _():
            dkr_ref[...] = dkr_scr[...]
            if rider.finish is not None:
                rider.finish(r_in, r_out, r_sems)

    q_blk = lambda w: pl.BlockSpec((t, G * w), lambda h, p, qt, kt: (qt[p], h))
    stat = pl.BlockSpec((G, t, LANES), lambda h, p, qt, kt: (h, qt[p], 0))
    tab = pl.BlockSpec((t, LANES), lambda h, p, qt, kt: (kt[p], 0))
    res = pl.pallas_call(
        body, name="mla_attn_bwd",
        grid_spec=pltpu.PrefetchScalarGridSpec(
            num_scalar_prefetch=2, grid=(heads // G, n_pairs),
            in_specs=[q_blk(QK_PAD),
                      pl.BlockSpec((t, G * QK_PAD), lambda h, p, qt, kt: (kt[p], h)),
                      tab, q_blk(V_HEAD), q_blk(V_HEAD), stat, tab, tab, tab] + [ANY] * n_ri,
            out_specs=[pl.BlockSpec((S, G * QK_PAD), lambda h, p, qt, kt: (0, h)),
                       pl.BlockSpec((t, G * QK_PAD), lambda h, p, qt, kt: (kt[p], h)),
                       pl.BlockSpec((S, LANES), lambda h, p, qt, kt: (0, 0))] + [ANY] * n_ro,
            scratch_shapes=[pltpu.VMEM((S, G * QK_PAD), F32), pltpu.VMEM((G, t, QK_PAD), F32),
                            pltpu.VMEM((G, t, V_HEAD), F32), pltpu.VMEM((S, LANES), F32),
                            pltpu.VMEM((G, S, LANES), F32)] + list(rider.sems)),
        out_shape=[_sds((S, heads * QK_PAD), BF16), _sds((S, heads * QK_PAD), BF16), _sds((S, LANES), F32)]
        + list(rider.out_shape),
        input_output_aliases={11 + i: 3 + o for i, o in rider.aliases.items()},
        compiler_params=_cparams("arbitrary", "arbitrary"))(q_tab, k_tab, q, kv, kr, o, do, lse, *rope_tabs,
                                                            *rider.operands)
    return res[0], res[1], res[2], res[3:]


def _shift_down(z, n, rows):
    return jnp.where(rows >= n, pltpu.roll(z, n, 0), 0.0)


def _shift_up(z, n, rows, S):
    return jnp.where(rows < S - n, pltpu.roll(z, S - n, 0), 0.0)


def _conv_specs(S, tc):
    strip = lambda p: pl.BlockSpec((None, S, tc), lambda j: (p, 0, j))
    return strip(0), strip(1), strip(2), pl.BlockSpec((3, tc), lambda j: (0, j))


def _conv_fwd(proj3, w):
    _, S, D = proj3.shape
    tc = LANES

    def body(b_ref, c_ref, u_ref, w_ref, out_ref):
        z = c_ref[...] * u_ref[...]
        rows = lax.broadcasted_iota(jnp.int32, (S, tc), 0)
        zc = w_ref[0:1, :] * _shift_down(z, 2, rows) + w_ref[1:2, :] * _shift_down(z, 1, rows) + w_ref[2:3, :] * z
        out_ref[...] = (b_ref[...] * zc).astype(BF16)

    return pl.pallas_call(
        body, name="conv_fwd", grid=(D // tc,), in_specs=list(_conv_specs(S, tc)),
        out_specs=pl.BlockSpec((S, tc), lambda j: (0, j)), out_shape=_sds((S, D), BF16),
        compiler_params=_cparams("parallel"))(proj3, proj3, proj3, w)


def _conv_bwd(dbz, proj3, w):
    _, S, D = proj3.shape
    tc = LANES

    def body(d_ref, b_ref, c_ref, u_ref, w_ref, dp_ref, dw_ref):
        cv, uv, dv = c_ref[...], u_ref[...], d_ref[...]
        z = cv * uv
        rows = lax.broadcasted_iota(jnp.int32, (S, tc), 0)
        z1, z2 = _shift_down(z, 1, rows), _shift_down(z, 2, rows)
        zc = w_ref[0:1, :] * z2 + w_ref[1:2, :] * z1 + w_ref[2:3, :] * z
        dp_ref[0] = (dv * zc).astype(BF16)
        dzc = dv * b_ref[...]
        dz = w_ref[2:3, :] * dzc + w_ref[1:2, :] * _shift_up(dzc, 1, rows, S) + w_ref[0:1, :] * _shift_up(dzc, 2, rows, S)
        dp_ref[1] = (dz * uv).astype(BF16)
        dp_ref[2] = (dz * cv).astype(BF16)
        dw_ref[0:1, :] = jnp.sum(dzc * z2, axis=0, keepdims=True)
        dw_ref[1:2, :] = jnp.sum(dzc * z1, axis=0, keepdims=True)
        dw_ref[2:3, :] = jnp.sum(dzc * z, axis=0, keepdims=True)

    sb, sc_, su, sw = _conv_specs(S, tc)
    return pl.pallas_call(
        body, name="conv_bwd", grid=(D // tc,),
        in_specs=[pl.BlockSpec((S, tc), lambda j: (0, j)), sb, sc_, su, sw],
        out_specs=[pl.BlockSpec((3, S, tc), lambda j: (0, 0, j)), pl.BlockSpec((3, tc), lambda j: (0, j))],
        out_shape=[_sds((3, S, D), BF16), _sds((3, D), F32)],
        compiler_params=_cparams("parallel"))(dbz, proj3, proj3, proj3, w)


def _silu(c_all):
    def body(c_ref, o_ref):
        cv = c_ref[...]
        o_ref[...] = cv * (1.0 / (1.0 + jnp.exp(-cv)))

    vm = pl.BlockSpec(memory_space=pltpu.VMEM)
    return pl.pallas_call(body, name="cond_silu", in_specs=[vm], out_specs=vm, out_shape=_sds(c_all.shape, F32))(c_all)


def _mod_fwd(cond, w_mod, b_cols):
    L, D, ncol = w_mod.shape
    B = cond.shape[0]
    tk, tn = min(512, D), min(1024, ncol)
    nk = D // tk

    def body(c_ref, w_ref, b_ref, out_ref, acc):
        kk = pl.program_id(2)
        part = lax.dot_general(c_ref[...].astype(BF16), w_ref[...].astype(BF16), DIMS_NN, preferred_element_type=F32)

        @pl.when(kk == 0)
        def _():
            acc[...] = part

        @pl.when(kk > 0)
        def _():
            acc[...] += part

        @pl.when(kk == nk - 1)
        def _():
            out_ref[...] = acc[...] + b_ref[...]

    return pl.pallas_call(
        body, name="mod_fwd", grid=(L, ncol // tn, nk),
        in_specs=[pl.BlockSpec((B, tk), lambda l, j, k: (0, k)),
                  pl.BlockSpec((None, tk, tn), lambda l, j, k: (l, k, j)),
                  pl.BlockSpec((None, 1, tn), lambda l, j, k: (l, 0, j))],
        out_specs=pl.BlockSpec((None, B, tn), lambda l, j, k: (l, 0, j)),
        out_shape=_sds((L, B, ncol), F32),
        scratch_shapes=[pltpu.VMEM((B, tn), F32)],
        compiler_params=_cparams("parallel", "parallel", "arbitrary"))(cond, w_mod, b_cols)


def _adamw_math(w, g, m, v):
    m = ADAM_B1 * m + (1.0 - ADAM_B1) * g
    v = ADAM_B2 * v + (1.0 - ADAM_B2) * (g * g)
    m_hat = m / (1.0 - ADAM_B1 ** ADAM_STEP)
    v_hat = v / (1.0 - ADAM_B2 ** ADAM_STEP)
    delta = -ADAM_LR * (m_hat / (jnp.sqrt(v_hat) + ADAM_EPS) + ADAM_WD * w)
    return delta, m, v


def _adamw(name, w, g, m, v):
    shape = w.shape
    cols = shape[-1] if w.ndim <= 3 else shape[-2] * shape[-1]
    rows = w.size // cols
    w2, g2, m2, v2 = (t.reshape(rows, cols) for t in (w, g, m, v))
    tr = _row_tile(rows, cols * 4, limit=1024 * 1024, mult=8)
    spec = pl.BlockSpec((tr, cols), lambda i: (i, 0))

    def body(w_ref, g_ref, m_ref, v_ref, d_ref, nm_ref, nv_ref):
        d, nm, nv = _adamw_math(w_ref[...], g_ref[...], m_ref[...], v_ref[...])
        d_ref[...] = d
        nm_ref[...] = nm
        nv_ref[...] = nv

    outs = pl.pallas_call(body, name=name, grid=(rows // tr,), in_specs=[spec] * 4, out_specs=[spec] * 3,
                          out_shape=[_sds((rows, cols), F32)] * 3, compiler_params=_cparams("parallel"))(w2, g2, m2, v2)
    return tuple(t.reshape(shape) for t in outs)


def _adamw_mod(w, cond_t, dmod_cols, m, v, rider=NO_RIDER):
    L, D, ncol = w.shape
    B = cond_t.shape[1]
    tr, tc = min(256, D), min(1024, ncol)
    blk = pl.BlockSpec((None, tr, tc), lambda l, i, j: (l, i, j))
    grid = (L, D // tr, ncol // tc)
    n_ri, n_ro = len(rider.operands), len(rider.out_shape)

    def body(*refs):
        w_ref, ct_ref, dm_ref, m_ref, v_ref = refs[:5]
        r_in = refs[5:5 + n_ri]
        g_ref, d_ref, nm_ref, nv_ref = refs[5 + n_ri:9 + n_ri]
        r_out = refs[9 + n_ri:9 + n_ri + n_ro]
        r_sems = refs[9 + n_ri + n_ro:]
        ids = [pl.program_id(a) for a in range(3)]

        if rider.start is not None:
            @pl.when((ids[0] == 0) & (ids[1] == 0) & (ids[2] == 0))
            def _():
                rider.start(r_in, r_out, r_sems)

        g = lax.dot_general(ct_ref[...], dm_ref[...], DIMS_NN, precision=lax.Precision.HIGHEST,
                            preferred_element_type=F32)
        d, nm, nv = _adamw_math(w_ref[...], g, m_ref[...], v_ref[...])
        g_ref[...] = g
        d_ref[...] = d
        nm_ref[...] = nm
        nv_ref[...] = nv

        if rider.finish is not None:
            @pl.when((ids[0] == grid[0] - 1) & (ids[1] == grid[1] - 1) & (ids[2] == grid[2] - 1))
            def _():
                rider.finish(r_in, r_out, r_sems)

    hosted = rider.start is not None
    res = pl.pallas_call(
        body, name="adamw_w_mod", grid=grid,
        in_specs=[blk, pl.BlockSpec((tr, B), lambda l, i, j: (i, 0)),
                  pl.BlockSpec((None, B, tc), lambda l, i, j: (l, 0, j)), blk, blk] + [ANY] * n_ri,
        out_specs=[blk] * 4 + [ANY] * n_ro, out_shape=[_sds((L, D, ncol), F32)] * 4 + list(rider.out_shape),
        scratch_shapes=list(rider.sems), input_output_aliases={5 + i: 4 + o for i, o in rider.aliases.items()},
        compiler_params=_cparams(*(("arbitrary",) * 3 if hosted else ("parallel",) * 3)))(
            w, cond_t, dmod_cols, m, v, *rider.operands)
    return (*res[:4], res[4:])


def _cast_into_full(name, w, kind, k_idx):
    L, R, C = w.shape
    Rh = R // 2
    tr = _row_tile(Rh, C * 4)
    if kind == "row":
        out_shape = (L, N_CHIPS, 2, Rh, C)
        out_spec = pl.BlockSpec((None, None, None, tr, C), lambda l, h, i, k_ref: (l, k_ref[0], h, i, 0))
    else:
        out_shape = (L, 2, Rh, N_CHIPS * C)
        out_spec = pl.BlockSpec((None, None, tr, C), lambda l, h, i, k_ref: (l, h, i, k_ref[0]))

    def body(k_ref, w_ref, o_ref):
        o_ref[...] = w_ref[...].astype(BF16)

    return pl.pallas_call(
        body, name=name,
        grid_spec=pltpu.PrefetchScalarGridSpec(
            num_scalar_prefetch=1, grid=(L, 2, Rh // tr),
            in_specs=[pl.BlockSpec((None, None, tr, C), lambda l, h, i, k_ref: (l, h, i, 0))],
            out_specs=out_spec),
        out_shape=_sds(out_shape, BF16),
        compiler_params=_cparams("parallel", "parallel", "parallel"))(k_idx, w.reshape(L, 2, Rh, C))


def _pair_sum(name, g5, ra, c_idx):
    L, A, _, Rh, Cc = g5.shape
    tr = _row_tile(Rh, Cc * 4)

    def body(c_ref, g_ref, r_ref, o_ref):
        o_ref[...] = (g_ref[...].astype(F32) + r_ref[...].astype(F32)).astype(BF16)

    blk = pl.BlockSpec((None, None, tr, Cc), lambda l, a, i, c_ref: (l, a, i, 0))
    return pl.pallas_call(
        body, name=name,
        grid_spec=pltpu.PrefetchScalarGridSpec(
            num_scalar_prefetch=1, grid=(L, A, Rh // tr),
            in_specs=[pl.BlockSpec((None, None, None, tr, Cc), lambda l, a, i, c_ref: (l, a, c_ref[0], i, 0)), blk],
            out_specs=blk),
        out_shape=_sds((L, A, Rh, Cc), BF16),
        compiler_params=_cparams("parallel", "parallel", "parallel"))(c_idx, g5, ra)


def _chip_sum(name, p, rb, kc_idx, kind, layer=0, n_layers=1, prev=None):
    _, A, Rh, Cc = p.shape
    C = rb.shape[-1]
    tr = _row_tile(Rh, C * 4)
    if kind == "row":
        own = pl.BlockSpec((None, None, tr, C), lambda i, kc: (0, kc[0], i, 0))
    else:
        own = pl.BlockSpec((None, None, tr, C), lambda i, kc: (0, 0, i, kc[0]))
    peer = lambda j: pl.BlockSpec((None, None, tr, C), lambda i, kc: (j, 0, i, 0))

    def body(kc_ref, p_ref, r0_ref, r1_ref, r2_ref, *rest):
        o_ref = rest[-1]
        o_ref[...] = ((p_ref[...].astype(F32) + r0_ref[...].astype(F32)) + r1_ref[...].astype(F32)) + r2_ref[...].astype(F32)

    operands = [kc_idx, p, rb, rb, rb] + ([prev] if prev is not None else [])
    return pl.pallas_call(
        body, name=name,
        grid_spec=pltpu.PrefetchScalarGridSpec(
            num_scalar_prefetch=1, grid=(Rh // tr,),
            in_specs=[own, peer(0), peer(1), peer(2)] + ([ANY] if prev is not None else []),
            out_specs=pl.BlockSpec((None, None, tr, C), lambda i, kc: (layer, kc[1], i, 0))),
        out_shape=_sds((n_layers, 2, Rh, C), F32),
        input_output_aliases={5: 0} if prev is not None else {},
        compiler_params=_cparams("parallel"))(*operands)


def _mesh_place():
    x, y, c = lax.axis_index("x"), lax.axis_index("y"), lax.axis_index("c")
    chips = [(1 - x, y), (x, 1 - y), (1 - x, 1 - y)]
    return x, y, c, chips


def _remote(src, dst, send_sem, recv_sem, to):
    return pltpu.make_async_remote_copy(src_ref=src, dst_ref=dst, send_sem=send_sem, recv_sem=recv_sem,
                                        device_id=to, device_id_type=MESH_ID)


def _small_allgather(name, v, with_sum=False, rider=NO_RIDER):
    R, N = v.shape
    n_ri, n_ro, n_own = len(rider.operands), len(rider.out_shape), 2 if with_sum else 1

    def body(*refs):
        r_in = refs[1:1 + n_ri]
        r_out = refs[1 + n_ri + n_own:1 + n_ri + n_own + n_ro]
        r_sems = refs[1 + n_ri + n_own + n_ro + 3:]
        own = (refs[0],) + refs[1 + n_ri:1 + n_ri + n_own] + refs[1 + n_ri + n_own + n_ro:1 + n_ri + n_own + n_ro + 3]
        if with_sum:
            x_ref, out_ref, sum_ref, send_sems, recv_sems, local_sem = own
        else:
            x_ref, out_ref, send_sems, recv_sems, local_sem = own
        if rider.start is not None:
            rider.start(r_in, r_out, r_sems)
        x, y, c, chips = _mesh_place()
        me, sibling = (x, y, c), (x, y, 1 - c)

        def rows(px, py, pc):
            return out_ref.at[pl.ds((4 * px + 2 * py + pc) * R, R), :]

        def copy(k, block, to, src=None):
            return _remote(rows(*block) if src is None else src, rows(*block), send_sems.at[k], recv_sems.at[k], to)

        mine = pltpu.make_async_copy(x_ref, rows(*me), local_sem)
        mine.start()
        first = [copy(0, me, sibling, src=x_ref)]
        first += [copy(1 + j, me, (*chip, c), src=x_ref) for j, chip in enumerate(chips)]
        for cp in first:
            cp.start()
        passed = [copy(4 + j, (*chip, c), sibling) for j, chip in enumerate(chips)]
        for j, chip in enumerate(chips):
            copy(1 + j, (*chip, c), me).wait_recv()
            passed[j].start()
        copy(0, sibling, me).wait_recv()
        for j, chip in enumerate(chips):
            copy(4 + j, (*chip, 1 - c), me).wait_recv()
        for cp in first + passed:
            cp.wait_send()
        mine.wait()
        if with_sum:
            total = out_ref[0:R, :]
            for p in range(1, 8):
                total = total + out_ref[p * R:(p + 1) * R, :]
            sum_ref[...] = total
        if rider.finish is not None:
            rider.finish(r_in, r_out, r_sems)

    vm = pl.BlockSpec(memory_space=pltpu.VMEM)
    out_shape = [_sds((8 * R, N), F32)] + ([_sds((R, N), F32)] if with_sum else [])
    res = pl.pallas_call(
        body, name=name, out_shape=out_shape + list(rider.out_shape), in_specs=[vm] + [ANY] * n_ri,
        out_specs=[vm] * n_own + [ANY] * n_ro,
        scratch_shapes=[pltpu.SemaphoreType.DMA((7,)), pltpu.SemaphoreType.DMA((7,)), pltpu.SemaphoreType.DMA]
        + list(rider.sems),
        input_output_aliases={1 + i: n_own + o for i, o in rider.aliases.items()},
        compiler_params=pltpu.CompilerParams(vmem_limit_bytes=VMEM_LIMIT_BYTES))(v, *rider.operands)
    if rider.start is not None:
        return (*res[:n_own], res[n_own:])
    return res if with_sum else res[0]


def _full_place(ref, kind, C, kk, half, layer=None):
    lead = slice(None) if layer is None else pl.ds(layer, 1)
    if kind == "row":
        return ref.at[lead, kk, half]
    return ref.at[lead, half, :, pl.ds(pl.multiple_of(kk * C, LANES), C)]


def _gather_rider(fulls, kinds, shard_cols, layers=None):
    n = len(fulls)
    layers = layers or [None] * n

    def copies(outs, sems):
        x, y, c, chips = _mesh_place()
        k = 2 * x + y
        place = lambda a, kk, half: _full_place(outs[a], kinds[a], shard_cols[a], kk, half, layers[a])
        copy = lambda a, j, ref, to: _remote(ref, ref, sems[0].at[6 * a + j], sems[1].at[6 * a + j], to)
        return (x, y, c), chips, k, place, copy

    def start(_, outs, sems):
        (x, y, c), chips, k, place, copy = copies(outs, sems)
        for j, chip in enumerate(chips):
            for a in range(n):
                copy(a, j, place(a, k, c), (*chip, c)).start()

    def finish(_, outs, sems):
        (x, y, c), chips, k, place, copy = copies(outs, sems)
        me, sibling = (x, y, c), (x, y, 1 - c)
        for j, chip in enumerate(chips):
            kj = 2 * chip[0] + chip[1]
            for a in range(n):
                copy(a, j, place(a, kj, c), me).wait_recv()
                copy(a, 3 + j, place(a, kj, c), sibling).start()
        for j, chip in enumerate(chips):
            kj = 2 * chip[0] + chip[1]
            for a in range(n):
                copy(a, 3 + j, place(a, kj, 1 - c), me).wait_recv()
        for j, chip in enumerate(chips):
            kj = 2 * chip[0] + chip[1]
            for a in range(n):
                copy(a, j, place(a, k, c), (*chip, c)).wait_send()
                copy(a, 3 + j, place(a, kj, c), sibling).wait_send()

    return Rider(tuple(fulls), tuple(_sds(f.shape, BF16) for f in fulls), {a: a for a in range(n)},
                 (pltpu.SemaphoreType.DMA((6 * n,)), pltpu.SemaphoreType.DMA((6 * n,))), start, finish)


def _scatter_rider(ps, kinds, shard_cols):
    n = len(ps)

    def copies(ins, outs, sems):
        x, y, c, chips = _mesh_place()
        cps = []
        for j, chip in enumerate(chips):
            kj = 2 * chip[0] + chip[1]
            for a in range(n):
                C = shard_cols[a]
                src = ins[a].at[:, kj] if kinds[a] == "row" else ins[a].at[:, 0, :, pl.ds(pl.multiple_of(kj * C, LANES), C)]
                cps.append(_remote(src, outs[a].at[j], sems[0].at[3 * a + j], sems[1].at[3 * a + j], (*chip, c)))
        return cps

    def start(ins, outs, sems):
        for cp in copies(ins, outs, sems):
            cp.start()

    def finish(ins, outs, sems):
        cps = copies(ins, outs, sems)
        for cp in cps:
            cp.wait_recv()
        for cp in cps:
            cp.wait_send()

    out_shape = tuple(_sds((3, p.shape[0], p.shape[2], C), BF16) for p, C in zip(ps, shard_cols))
    return Rider(tuple(ps), out_shape, {}, (pltpu.SemaphoreType.DMA((3 * n,)), pltpu.SemaphoreType.DMA((3 * n,))),
                 start, finish)


def _run_rider(name, rider):
    n_in, n_out = len(rider.operands), len(rider.out_shape)

    def body(*refs):
        ins, outs, sems = refs[:n_in], refs[n_in:n_in + n_out], refs[n_in + n_out:]
        rider.start(ins, outs, sems)
        rider.finish(ins, outs, sems)

    return pl.pallas_call(
        body, name=name, out_shape=list(rider.out_shape), in_specs=[ANY] * n_in, out_specs=[ANY] * n_out,
        input_output_aliases=dict(rider.aliases), scratch_shapes=list(rider.sems),
        compiler_params=pltpu.CompilerParams(vmem_limit_bytes=VMEM_LIMIT_BYTES))(*rider.operands)


def _exchange_rider(g5s):
    n = len(g5s)

    def copies(ins, outs, sems):
        x, y, c, _ = _mesh_place()
        return [_remote(ins[a].at[:, :, 1 - c], outs[a], sems[0].at[a], sems[1].at[a], (x, y, 1 - c)) for a in range(n)]

    def start(ins, outs, sems):
        for cp in copies(ins, outs, sems):
            cp.start()

    def finish(ins, outs, sems):
        cps = copies(ins, outs, sems)
        for cp in cps:
            cp.wait_recv()
        for cp in cps:
            cp.wait_send()

    out_shape = tuple(_sds((g.shape[0], g.shape[1], g.shape[3], g.shape[4]), BF16) for g in g5s)
    return Rider(tuple(g5s), out_shape, {}, (pltpu.SemaphoreType.DMA((n,)), pltpu.SemaphoreType.DMA((n,))), start, finish)


def _share_rider(fs):
    n = len(fs)

    def start(_, outs, sems):
        x, y, c, _p = _mesh_place()
        for a in range(n):
            mine = outs[a].at[:, c]
            _remote(mine, mine, sems[0].at[a], sems[1].at[a], (x, y, 1 - c)).start()

    def finish(_, outs, sems):
        x, y, c, _p = _mesh_place()
        for a in range(n):
            theirs = outs[a].at[:, 1 - c]
            _remote(theirs, theirs, sems[0].at[a], sems[1].at[a], (x, y, c)).wait_recv()
        for a in range(n):
            mine = outs[a].at[:, c]
            _remote(mine, mine, sems[0].at[a], sems[1].at[a], (x, y, 1 - c)).wait_send()

    return Rider(tuple(fs), tuple(_sds(f.shape, F32) for f in fs), {a: a for a in range(n)},
                 (pltpu.SemaphoreType.DMA((n,)), pltpu.SemaphoreType.DMA((n,))), start, finish)


def _both_riders(r1, r2):
    ni, no, ns = len(r1.operands), len(r1.out_shape), len(r1.sems)
    aliases = dict(r1.aliases)
    aliases.update({ni + i: no + o for i, o in r2.aliases.items()})

    def start(ins, outs, sems):
        r1.start(ins[:ni], outs[:no], sems[:ns])
        r2.start(ins[ni:], outs[no:], sems[ns:])

    def finish(ins, outs, sems):
        r1.finish(ins[:ni], outs[:no], sems[:ns])
        r2.finish(ins[ni:], outs[no:], sems[ns:])

    return Rider(r1.operands + r2.operands, r1.out_shape + r2.out_shape, aliases, r1.sems + r2.sems, start, finish)


def _pack_rows(parts, lane_mult=1024):
    flat = jnp.concatenate([p.reshape(-1).astype(F32) for p in parts])
    n = -(-flat.shape[0] // (8 * lane_mult)) * lane_mult
    return jnp.pad(flat, (0, 8 * n - flat.shape[0])).reshape(8, n)


def _relu2(acc):
    r = jnp.maximum(acc, 0.0)
    return r, r * r


def _times_2r(acc, r):
    return (acc * (2.0 * r.astype(F32)),)


def kernel(x, c, positions, w_mod, b_mod, norm_g, mla_w_in, mla_g_q, mla_g_kv, mla_w_uq, mla_w_ukv, mla_w_o, conv_w_in, conv_w, conv_w_out, mlp_w_up, mlp_w_down, loss_target, m_w_mod, m_b_mod, m_norm_g, m_mla_w_in, m_mla_g_q, m_mla_g_kv, m_mla_w_uq, m_mla_w_ukv, m_mla_w_o, m_conv_w_in, m_conv_w, m_conv_w_out, m_mlp_w_up, m_mlp_w_down, v_w_mod, v_b_mod, v_norm_g, v_mla_w_in, v_mla_g_q, v_mla_g_kv, v_mla_w_uq, v_mla_w_ukv, v_mla_w_o, v_conv_w_in, v_conv_w, v_conv_w_out, v_mlp_w_up, v_mlp_w_down):
    S, D = x.shape[1], x.shape[2]
    Dq = D // N_CHIPS
    ncol = w_mod.shape[2]
    n_mod = N_CHIPS * ncol // D
    F = mlp_w_up.shape[2] * N_CHIPS
    lat_dim = mla_w_in.shape[2]
    rank = mla_g_q.shape[1]
    H = mla_w_uq.shape[2]
    d_qk = mla_w_uq.shape[3]
    assert mla_g_kv.shape[1] == rank and lat_dim == 2 * rank + QK_ROPE and d_qk == QK_NOPE + QK_ROPE
    assert mla_w_ukv.shape[3] == QK_NOPE + V_HEAD and x.shape[0] == 1 and n_mod == 6
    assert norm_g.shape[0] == 2 and mla_w_in.shape[0] == 1 and conv_w_in.shape[0] == 1
    lat_pad = 2 * rank + LANES
    scale = float(d_qk) ** -0.5

    xi, yi, ci = lax.axis_index("x"), lax.axis_index("y"), lax.axis_index("c")
    chip = 2 * xi + yi
    dev = 2 * chip + ci
    c_idx = jnp.reshape(ci, (1,)).astype(jnp.int32)
    k_idx = jnp.reshape(chip, (1,)).astype(jnp.int32)

    n1 = D + 2 * D + 3 * Dq
    g1 = _small_allgather("gather_small_inputs", _pack_rows([c, norm_g, conv_w])).reshape(8, -1)
    c_all = g1[:, :D]
    by_chip = g1[0::2]
    norm_full = jnp.concatenate([by_chip[kk, D:3 * D].reshape(2, 4, Dq) for kk in range(N_CHIPS)], axis=-1)
    convw_full = jnp.concatenate([by_chip[kk, 3 * D:n1].reshape(3, Dq) for kk in range(N_CHIPS)], axis=-1)

    b_cols = lax.dynamic_slice(b_mod, (0, chip * ncol), (2, ncol)).reshape(2, 1, ncol)
    cond_all = _silu(c_all)
    mod_cols = _mod_fwd(cond_all, w_mod, b_cols)
    g2 = _small_allgather("gather_mod", _pack_rows([mod_cols]))
    g2 = g2.reshape(8, -1)[0::2, :2 * 8 * ncol].reshape(N_CHIPS, 2, 8, ncol)
    mod_all = jnp.transpose(g2, (2, 1, 0, 3)).reshape(8, 2, n_mod * D)
    mod_me = lax.dynamic_index_in_dim(mod_all, dev, axis=0, keepdims=False)
    mods = [[mod_me[l, i * D:(i + 1) * D].reshape(1, D) for i in range(n_mod)] for l in range(2)]
    ng = [[norm_full[l, i].reshape(1, D) for i in range(4)] for l in range(2)]

    pos = positions[0].astype(F32)
    inv_freq = ROPE_THETA ** (-jnp.arange(0, QK_ROPE, 2, dtype=F32) / QK_ROPE)
    ang = pos[:, None] * inv_freq
    cos, sin = jnp.cos(ang), jnp.sin(ang)
    zero = jnp.zeros_like(cos)
    rope_tabs = (jnp.concatenate([cos, cos, zero, zero], axis=1),
                 jnp.concatenate([-sin, zero, zero, zero], axis=1),
                 jnp.concatenate([zero, sin, zero, zero], axis=1))

    weights = [("mla_w_in", mla_w_in, "row"), ("mla_w_uq", mla_w_uq.reshape(1, rank // N_CHIPS, H * d_qk), "row"),
               ("mla_w_ukv", mla_w_ukv.reshape(1, rank // N_CHIPS, H * QK_PAD), "row"), ("mla_w_o", mla_w_o, "row"),
               ("conv_w_in", conv_w_in, "col"), ("conv_w_out", conv_w_out, "row"),
               ("mlp_w_up", mlp_w_up, "col"), ("mlp_w_down", mlp_w_down, "row")]
    kinds = [k for _, _, k in weights]
    shard_shapes = [w.shape for _, w, _ in weights]
    shard_cols = [s[2] for s in shard_shapes]
    casted = [_cast_into_full("cast_" + nm, w, kind, k_idx) for nm, w, kind in weights]
    W_IN, W_UQ, W_UKV, W_O, W_CIN, W_COUT, W_UP, W_DOWN = range(8)
    mla_idx = [W_IN, W_UQ, W_UKV, W_O]

    def view(i, buf):
        L, R, C = shard_shapes[i]
        return buf.reshape((L, N_CHIPS * R, C) if kinds[i] == "row" else (L, R, N_CHIPS * C))

    def gather_of(bufs, idx, layers=None):
        return _gather_rider(bufs, [kinds[i] for i in idx], [shard_cols[i] for i in idx], layers)

    def scatter_of(ps, idx):
        return _scatter_rider(ps, [kinds[i] for i in idx], [shard_cols[i] for i in idx])

    def halves(items):
        g5s = []
        for _, i, g in items:
            _, R, C = shard_shapes[i]
            g5s.append(g.reshape((1, N_CHIPS, 2, R // 2, C) if kinds[i] == "row" else (1, 1, 2, R // 2, N_CHIPS * C)))
        return g5s

    def pair_sums(items, g5s, ras):
        return [_pair_sum("pair_sum_" + nm, g5, ra, c_idx) for (nm, _, _), g5, ra in zip(items, g5s, ras)]

    got = _run_rider("gather_weights_mla", gather_of([casted[i] for i in mla_idx], mla_idx))
    w_in_p = jnp.pad(view(W_IN, got[0])[0], ((0, 0), (0, lat_pad - lat_dim)))
    w_q_p = jnp.pad(view(W_UQ, got[1])[0].reshape(rank, H, d_qk), ((0, 0), (0, 0), (0, QK_PAD - d_qk))).reshape(rank, H * QK_PAD)
    w_ukv, w_o = view(W_UKV, got[2])[0], view(W_O, got[3])[0]
    HV = H * V_HEAD

    def layer_b(l, transposed):
        if transposed:
            return lambda tm, tn, tk: pl.BlockSpec((None, tn, tk), lambda i, j, k: (l, j, k))
        return lambda tm, tn, tk: pl.BlockSpec((None, tk, tn), lambda i, j, k: (l, k, j))

    def mlp_up(tag, l, h, w, rider=NO_RIDER):
        return _mm("mlp_up_" + tag, h, w, "nn", S, F, D, [_sds((S, F), BF16)] * 2, epilogue=_relu2,
                   b_spec=layer_b(l, False), rider=rider)

    def mlp_down(tag, l, a2, w, rider=NO_RIDER):
        return _mm("mlp_down_" + tag, a2, w, "nn", S, D, F, [_sds((S, D), F32)], b_spec=layer_b(l, False), rider=rider)

    def mlp_bwd(tag, l, h, r, a2, dy, rider=NO_RIDER):
        res = _mm("mlp_down_dx_" + tag, dy, w_down, "nt", S, F, D, [_sds((S, F), BF16)], epilogue=_times_2r,
                  b_spec=layer_b(l, True), rider=rider,
                  extras=[(r, lambda tm, tn, tk: pl.BlockSpec((tm, tn), lambda i, j, k: (i, j)))])
        (da,), carried = res if rider.start is not None else (res, ())
        (dw_down,) = _mm("mlp_down_dw_" + tag, a2, dy, "tn", F, D, S, [_sds((F, D), BF16)])
        (dh,) = _mm("mlp_up_dx_" + tag, da, w_up, "nt", S, D, F, [_sds((S, D), F32)], b_spec=layer_b(l, True))
        (dw_up,) = _mm("mlp_up_dw_" + tag, h, da, "tn", D, F, S, [_sds((D, F), BF16)])
        return dh, dw_up, dw_down, carried

    x0 = x[0]
    sh1, sc1, gt1, sh2, sc2, gt2 = mods[0]
    (h1,) = _fwd_boundary("fwd_boundary_0", x0, None, None, None, ng[0][0], sc1, sh1)
    (lat,) = _mm("mla_in", h1, w_in_p, "nn", S, lat_pad, D, [_sds((S, lat_pad), F32)], tn=lat_pad)
    cq, ckv, kr = _latent_fwd(lat, mla_g_q, mla_g_kv, rope_tabs, rank)

    def rope_q(acc, cos_p, sin_lo, sin_hi):
        parts = []
        for hh in range(acc.shape[1] // QK_PAD):
            parts.append(acc[:, hh * QK_PAD:hh * QK_PAD + QK_NOPE])
            parts.append(_rope(acc[:, hh * QK_PAD + QK_NOPE:(hh + 1) * QK_PAD], cos_p, sin_lo, sin_hi))
        return (jnp.concatenate(parts, axis=1),)

    tab_extra = lambda tm, tn, tk: pl.BlockSpec((tm, LANES), lambda i, j, k: (i, 0))
    (q,) = _mm("mla_q", cq, w_q_p, "nn", S, H * QK_PAD, rank, [_sds((S, H * QK_PAD), BF16)], epilogue=rope_q,
               extras=[(t, tab_extra) for t in rope_tabs], tn=2 * QK_PAD)
    (kv,) = _mm("mla_kv", ckv, w_ukv, "nn", S, H * QK_PAD, rank, [_sds((S, H * QK_PAD), BF16)])
    rest_idx = [W_COUT, W_UP, W_DOWN]
    o, lse, (cout_buf, up_buf, down_buf) = _attn_fwd_tri(
        q, kv, kr, H, scale, gather_of([casted[i] for i in rest_idx], rest_idx, [None, 0, 0]))
    (y1,) = _mm("mla_out", o, w_o, "nn", S, D, HV, [_sds((S, D), F32)])
    x1, h2 = _fwd_boundary("fwd_boundary_1", x0, y1, gt1, ng[0][1], ng[0][2], sc2, sh2)
    (r2, a2), (cin_buf,) = mlp_up("0", 0, h2, view(W_UP, up_buf), gather_of([casted[W_CIN]], [W_CIN]))
    (y2,), (up_buf,) = mlp_down("0", 0, a2, view(W_DOWN, down_buf), gather_of([up_buf], [W_UP], [1]))
    w_cin, w_cout, w_up = view(W_CIN, cin_buf)[0], view(W_COUT, cout_buf)[0], view(W_UP, up_buf)

    sh1b, sc1b, gt1b, sh2b, sc2b, gt2b = mods[1]
    x2, h3 = _fwd_boundary("fwd_boundary_2", x1, y2, gt2, ng[0][3], ng[1][0], sc1b, sh1b)
    nD = lambda tn: D // tn
    (proj3,) = _mm("conv_in", h3, w_cin, "nn", S, 3 * D, D, [_sds((3, S, D), F32)], tn=min(1024, D),
                   out_specs=[lambda tm, tn, tk: pl.BlockSpec((None, tm, tn), lambda i, j, k: (j // nD(tn), i, j % nD(tn)))])
    bz = _conv_fwd(proj3, convw_full)
    (y3,) = _mm("conv_out", bz, w_cout, "nn", S, D, D, [_sds((S, D), F32)])
    x3, h4 = _fwd_boundary("fwd_boundary_3", x2, y3, gt1b, ng[1][1], ng[1][2], sc2b, sh2b)
    (r4, a4), (down_buf,) = mlp_up("1", 1, h4, w_up, gather_of([down_buf], [W_DOWN], [1]))
    w_down = view(W_DOWN, down_buf)
    (y4,) = mlp_down("1", 1, a4, w_down)

    dx4, dy4, sums_l, loss_acc = _loss_boundary("loss_boundary", x3, y4, gt2b, ng[1][3], loss_target[0])
    loss = lax.psum(loss_acc[0, 0], ("x", "y", "c"))

    dh4, dw_up1, dw_down1, _ = mlp_bwd("1", 1, h4, r4, a4, dy4)
    items = [("mlp_w_up_1", W_UP, dw_up1), ("mlp_w_down_1", W_DOWN, dw_down1)]
    g5s = halves(items)
    dx3, dy3, sums_3, ras = _bwd_boundary("bwd_boundary_3", dx4, dh4, x3, y3, gt1b, ng[1][1], ng[1][2], sc2b,
                                          _exchange_rider(g5s))
    ps_up1, ps_down1 = pair_sums(items, g5s, ras)

    (dbz,) = _mm("conv_out_dx", dy3, w_cout, "nt", S, D, D, [_sds((S, D), F32)])
    (dw_cout,) = _mm("conv_out_dw", bz, dy3, "tn", D, D, S, [_sds((D, D), BF16)])
    dproj3, dconvw = _conv_bwd(dbz, proj3, convw_full)
    (dh3,), (rb_up1,) = _mm(
        "conv_in_dx", dproj3, w_cin, "nt", S, D, 3 * D, [_sds((S, D), F32)], tk=min(1024, D), rider=scatter_of([ps_up1], [W_UP]),
        a_spec=lambda tm, tn, tk: pl.BlockSpec((None, tm, tk), lambda i, j, k: (k // (D // tk), i, k % (D // tk))))
    (dw_cin,), (rb_down1,) = _mm(
        "conv_in_dw", h3, dproj3, "tn", D, 3 * D, S, [_sds((D, 3 * D), BF16)], tn=min(1024, D),
        rider=scatter_of([ps_down1], [W_DOWN]),
        b_spec=lambda tm, tn, tk: pl.BlockSpec((None, tk, tn), lambda i, j, k: (j // nD(tn), k, j % nD(tn))))
    items = [("conv_w_in", W_CIN, dw_cin), ("conv_w_out", W_COUT, dw_cout)]
    g5s = halves(items)
    dx2, dy2, sums_2, ras = _bwd_boundary("bwd_boundary_2", dx3, dh3, x2, y2, gt2, ng[0][3], ng[1][0], sc1b,
                                          _exchange_rider(g5s))
    ps_cin, ps_cout = pair_sums(items, g5s, ras)

    dh2, dw_up0, dw_down0, _ = mlp_bwd("0", 0, h2, r2, a2, dy2)
    items = [("mlp_w_up_0", W_UP, dw_up0), ("mlp_w_down_0", W_DOWN, dw_down0)]
    g5s = halves(items)
    dx1, dy1, sums_1, ras = _bwd_boundary("bwd_boundary_1", dx2, dh2, x1, y1, gt1, ng[0][1], ng[0][2], sc2,
                                          _exchange_rider(g5s))
    ps_up0, ps_down0 = pair_sums(items, g5s, ras)

    (do,) = _mm("mla_out_dx", dy1, w_o, "nt", S, HV, D, [_sds((S, HV), BF16)])
    (dw_o,) = _mm("mla_out_dw", o, dy1, "tn", HV, D, S, [_sds((HV, D), BF16)])
    dq, dkv, dkr, (rb_up0, rb_down0, rb_cin, rb_cout) = _attn_bwd_tri(
        q, kv, kr, o, do, lse, rope_tabs, H, scale,
        scatter_of([ps_up0, ps_down0, ps_cin, ps_cout], [W_UP, W_DOWN, W_CIN, W_COUT]))
    (dcq,) = _mm("mla_q_dx", dq, w_q_p, "nt", S, rank, H * QK_PAD, [_sds((S, rank), F32)])
    (dw_q_p,) = _mm("mla_q_dw", cq, dq, "tn", rank, H * QK_PAD, S, [_sds((rank, H * QK_PAD), BF16)])
    (dckv,) = _mm("mla_kv_dx", dkv, w_ukv, "nt", S, rank, H * QK_PAD, [_sds((S, rank), F32)])
    (dw_ukv,) = _mm("mla_kv_dw", ckv, dkv, "tn", rank, H * QK_PAD, S, [_sds((rank, H * QK_PAD), BF16)])
    dlat, sums_lat = _latent_bwd(lat, dcq, dckv, dkr, mla_g_q, mla_g_kv, rope_tabs, rank)
    (dw_in_p,) = _mm("mla_in_dw", h1, dlat, "tn", D, lat_pad, S, [_sds((D, lat_pad), BF16)], tn=lat_pad)
    dw_mla = [dw_in_p[:, :lat_dim], dw_q_p.reshape(rank, H, QK_PAD)[:, :, :d_qk].reshape(rank, H * d_qk), dw_ukv, dw_o]
    items = [(weights[i][0], i, g) for i, g in zip(mla_idx, dw_mla)]
    g5s = halves(items)
    (dh1,), ras = _mm("mla_in_dx", dlat, w_in_p, "nt", S, D, lat_pad, [_sds((S, D), F32)], rider=_exchange_rider(g5s))
    ps_mla = pair_sums(items, g5s, ras)
    grad_x, sums_0, _ = _bwd_boundary("bwd_boundary_0", dx1, dh1, x0, None, None, None, ng[0][0], sc1)

    kc_idx = jnp.stack([chip, ci]).astype(jnp.int32)
    fs_rest = [_chip_sum("chip_sum_" + weights[i][0], p, rb, kc_idx, kinds[i])
               for i, p, rb in [(W_CIN, ps_cin, rb_cin), (W_COUT, ps_cout, rb_cout)]]
    for i, (p1, r1), (p0, r0) in [(W_UP, (ps_up1, rb_up1), (ps_up0, rb_up0)), (W_DOWN, (ps_down1, rb_down1), (ps_down0, rb_down0))]:
        f = _chip_sum("chip_sum_" + weights[i][0] + "_1", p1, r1, kc_idx, kinds[i], layer=1, n_layers=2)
        fs_rest.append(_chip_sum("chip_sum_" + weights[i][0] + "_0", p0, r0, kc_idx, kinds[i], layer=0, n_layers=2, prev=f))

    dmod0 = [sums_0[0], sums_0[1], sums_1[3], sums_1[0], sums_1[1], sums_2[3]]
    dmod1 = [sums_2[0], sums_2[1], sums_3[3], sums_3[0], sums_3[1], sums_l[3]]
    dng0 = [sums_0[2], sums_1[4], sums_1[2], sums_2[4]]
    dng1 = [sums_2[2], sums_3[4], sums_3[2], sums_l[4]]
    small = _pack_rows(dmod0 + dmod1 + dng0 + dng1 + [sums_lat[0], sums_lat[1], dconvw], lane_mult=LANES)
    gathered, total, carried = _small_allgather(
        "gather_small_grads", small, with_sum=True, rider=_both_riders(scatter_of(ps_mla, mla_idx), _share_rider(fs_rest)))
    rbs_mla, finals_rest = carried[:len(mla_idx)], carried[len(mla_idx):]
    n_dm = 2 * n_mod * D
    dmod_all = gathered.reshape(8, -1)[:, :n_dm].reshape(8, 2, n_mod * D)
    total = total.reshape(-1)
    g_b_mod = total[:n_dm].reshape(2, n_mod * D)
    g_norm = lax.dynamic_slice(total[n_dm:n_dm + 8 * D].reshape(2, 4, D), (0, 0, chip * Dq), (2, 4, Dq))
    off = n_dm + 8 * D
    g_gq = total[off:off + rank].reshape(1, rank)
    g_gkv = total[off + rank:off + 2 * rank].reshape(1, rank)
    off += 2 * rank
    g_convw = lax.dynamic_slice(total[off:off + 3 * D].reshape(1, 3, D), (0, 0, chip * Dq), (1, 3, Dq))

    dmod_cols = jnp.transpose(lax.dynamic_slice(dmod_all.reshape(8, 2, N_CHIPS, ncol), (0, 0, chip, 0), (8, 2, 1, ncol))
                              .reshape(8, 2, ncol), (1, 0, 2))
    g_w_mod, d_w_mod, nm_w_mod, nv_w_mod, _ = _adamw_mod(w_mod, cond_all.T, dmod_cols, m_w_mod, v_w_mod)
    fs_mla = [_chip_sum("chip_sum_" + weights[i][0], p, rb, kc_idx, kinds[i]) for i, p, rb in zip(mla_idx, ps_mla, rbs_mla)]
    finals = list(_run_rider("grad_pair_share_mla", _share_rider(fs_mla))) + list(finals_rest)
    orig = [mla_w_in, mla_w_uq, mla_w_ukv, mla_w_o, conv_w_in, conv_w_out, mlp_w_up, mlp_w_down]
    big_grads = [f.reshape(w.shape) for f, w in zip(finals, orig)]

    names = ["b_mod", "norm_g", "mla_w_in", "mla_g_q", "mla_g_kv", "mla_w_uq", "mla_w_ukv", "mla_w_o",
             "conv_w_in", "conv_w", "conv_w_out", "mlp_w_up", "mlp_w_down"]
    ws = [b_mod, norm_g, mla_w_in, mla_g_q, mla_g_kv, mla_w_uq, mla_w_ukv, mla_w_o, conv_w_in, conv_w, conv_w_out,
          mlp_w_up, mlp_w_down]
    ms = [m_b_mod, m_norm_g, m_mla_w_in, m_mla_g_q, m_mla_g_kv, m_mla_w_uq, m_mla_w_ukv, m_mla_w_o, m_conv_w_in,
          m_conv_w, m_conv_w_out, m_mlp_w_up, m_mlp_w_down]
    vs = [v_b_mod, v_norm_g, v_mla_w_in, v_mla_g_q, v_mla_g_kv, v_mla_w_uq, v_mla_w_ukv, v_mla_w_o, v_conv_w_in,
          v_conv_w, v_conv_w_out, v_mlp_w_up, v_mlp_w_down]
    gs = [g_b_mod, g_norm, big_grads[0], g_gq, g_gkv, big_grads[1], big_grads[2], big_grads[3], big_grads[4],
          g_convw, big_grads[5], big_grads[6], big_grads[7]]
    grads, deltas, new_ms, new_vs = [g_w_mod], [d_w_mod], [nm_w_mod], [nv_w_mod]
    for nm, w, g, m, v in zip(names, ws, gs, ms, vs):
        d, nm_, nv_ = _adamw("adamw_" + nm, w, g, m, v)
        grads.append(g)
        deltas.append(d)
        new_ms.append(nm_)
        new_vs.append(nv_)
    return (loss, grad_x[None], *grads, *deltas, *new_ms, *new_vs)
```

```python
from typing import NamedTuple

import jax
import jax.numpy as jnp
from jax import lax
from jax.experimental import pallas as pl
from jax.experimental.pallas import tpu as pltpu

F32 = jnp.float32
BF16 = jnp.bfloat16
NORM_EPS = 1e-6
ROPE_THETA = 10000.0
QK_NOPE = 128
QK_ROPE = 64
V_HEAD = 128
LANES = 128
QK_PAD = QK_NOPE + LANES
ADAM_LR, ADAM_B1, ADAM_B2, ADAM_EPS, ADAM_WD, ADAM_STEP = 0.001, 0.9, 0.999, 1e-08, 0.01, 10
VMEM_LIMIT_BYTES = 56 * 1024 * 1024
N_CHIPS = 4
MESH_ID = pl.DeviceIdType.MESH
ANY = pl.BlockSpec(memory_space=pl.ANY)
NEG_INF = float("-inf")

DIMS_NN = (((1,), (0,)), ((), ()))
DIMS_NT = (((1,), (1,)), ((), ()))
DIMS_TN = (((0,), (0,)), ((), ()))


def _cparams(*sem):
    return pltpu.CompilerParams(dimension_semantics=sem, vmem_limit_bytes=VMEM_LIMIT_BYTES)


def _row_tile(rows, row_bytes, limit=2 * 1024 * 1024, mult=16):
    if rows * row_bytes <= limit or rows % mult:
        return rows
    best = mult
    t = mult
    while t <= rows:
        if rows % t == 0 and t * row_bytes <= limit:
            best = t
        t += mult
    return best


def _rms(v):
    return lax.rsqrt(jnp.mean(v * v, axis=-1, keepdims=True) + NORM_EPS)


class Rider(NamedTuple):
    operands: tuple
    out_shape: tuple
    aliases: dict
    sems: tuple
    start: object
    finish: object


NO_RIDER = Rider((), (), {}, (), None, None)


def _mm(name, a, b, mode, M, N, K, outs, *, a_spec=None, b_spec=None, out_specs=None, epilogue=None,
        extras=(), rider=NO_RIDER, tm=1024, tn=1024, tk=4096):
    tm, tn, tk = min(tm, M), min(tn, N), min(tk, K)
    assert M % tm == 0 and N % tn == 0 and K % tk == 0, (name, M, N, K)
    nk = K // tk
    if a_spec is None:
        a_spec = {"nn": pl.BlockSpec((tm, tk), lambda i, j, k: (i, k)),
                  "nt": pl.BlockSpec((tm, tk), lambda i, j, k: (i, k)),
                  "tn": pl.BlockSpec((tk, tm), lambda i, j, k: (k, i))}[mode]
    else:
        a_spec = a_spec(tm, tn, tk)
    if b_spec is None:
        b_spec = {"nn": pl.BlockSpec((tk, tn), lambda i, j, k: (k, j)),
                  "nt": pl.BlockSpec((tn, tk), lambda i, j, k: (j, k)),
                  "tn": pl.BlockSpec((tk, tn), lambda i, j, k: (k, j))}[mode]
    else:
        b_spec = b_spec(tm, tn, tk)
    if out_specs is None:
        out_specs = [pl.BlockSpec((tm, tn), lambda i, j, k: (i, j)) for _ in outs]
    else:
        out_specs = [s(tm, tn, tk) for s in out_specs]
    dims = {"nn": DIMS_NN, "nt": DIMS_NT, "tn": DIMS_TN}[mode]
    ne, no = len(extras), len(outs)
    n_ri, n_ro = len(rider.operands), len(rider.out_shape)
    grid = (M // tm, N // tn, nk)

    def body(*refs):
        a_ref, b_ref = refs[0], refs[1]
        ex = refs[2:2 + ne]
        r_in = refs[2 + ne:2 + ne + n_ri]
        o = refs[2 + ne + n_ri:2 + ne + n_ri + no]
        r_out = refs[2 + ne + n_ri + no:2 + ne + n_ri + no + n_ro]
        scratch = refs[2 + ne + n_ri + no + n_ro:]
        r_sems = scratch[1:] if nk > 1 else scratch
        ii, jj, kk = pl.program_id(0), pl.program_id(1), pl.program_id(2)

        if rider.start is not None:
            @pl.when((ii == 0) & (jj == 0) & (kk == 0))
            def _():
                rider.start(r_in, r_out, r_sems)

        part = lax.dot_general(a_ref[...].astype(BF16), b_ref[...].astype(BF16), dims,
                               preferred_element_type=F32)

        def finish(total):
            vals = epilogue(total, *[e[...] for e in ex]) if epilogue is not None else (total,)
            for r, v in zip(o, vals):
                r[...] = v.astype(r.dtype)

        if nk == 1:
            finish(part)
        else:
            acc = scratch[0]

            @pl.when(kk == 0)
            def _():
                acc[...] = part

            @pl.when(kk > 0)
            def _():
                acc[...] += part

            @pl.when(kk == nk - 1)
            def _():
                finish(acc[...])

        if rider.finish is not None:
            @pl.when((ii == grid[0] - 1) & (jj == grid[1] - 1) & (kk == nk - 1))
            def _():
                rider.finish(r_in, r_out, r_sems)

    operands = [a, b] + [e[0] for e in extras] + list(rider.operands)
    in_specs = [a_spec, b_spec] + [e[1](tm, tn, tk) for e in extras] + [ANY] * n_ri
    hosted = rider.start is not None
    res = pl.pallas_call(
        body, name=name, grid=grid,
        in_specs=in_specs, out_specs=out_specs + [ANY] * n_ro, out_shape=list(outs) + list(rider.out_shape),
        scratch_shapes=([pltpu.VMEM((tm, tn), F32)] if nk > 1 else []) + list(rider.sems),
        input_output_aliases={2 + ne + i: no + r for i, r in rider.aliases.items()},
        compiler_params=_cparams(*(("arbitrary",) * 3 if hosted else ("parallel", "parallel", "arbitrary"))),
    )(*operands)
    return (res[:no], res[no:]) if hosted else res


def _sds(shape, dtype):
    return jax.ShapeDtypeStruct(tuple(shape), dtype)


def _rope(t, cos_p, sin_lo, sin_hi):
    return t * cos_p + pltpu.roll(t, LANES - QK_ROPE // 2, 1) * sin_lo + pltpu.roll(t, QK_ROPE // 2, 1) * sin_hi


def _rope_t(d, cos_p, sin_lo, sin_hi):
    return d * cos_p + pltpu.roll(d * sin_lo, QK_ROPE // 2, 1) + pltpu.roll(d * sin_hi, LANES - QK_ROPE // 2, 1)


def _vec_spec(d):
    return pl.BlockSpec((1, d), lambda i: (0, 0))


def _fwd_boundary(name, x_prev, y, gate, ng_post, ng_pre, sc, sh):
    S, D = x_prev.shape
    ts = min(256, S)
    has_y = y is not None
    row = pl.BlockSpec((ts, D), lambda i: (i, 0))

    def body(*refs):
        if has_y:
            x_ref, y_ref, g_ref, ngp_ref, ngn_ref, sc_ref, sh_ref, xo_ref, h_ref = refs
            yv = y_ref[...]
            xn = x_ref[...] + g_ref[...] * (yv * _rms(yv) * ngp_ref[...])
            xo_ref[...] = xn
        else:
            x_ref, ngn_ref, sc_ref, sh_ref, h_ref = refs
            xn = x_ref[...]
        hn = xn * _rms(xn) * ngn_ref[...]
        h_ref[...] = (hn * (1.0 + sc_ref[...]) + sh_ref[...]).astype(BF16)

    vec = _vec_spec(D)
    if has_y:
        operands = (x_prev, y, gate, ng_post, ng_pre, sc, sh)
        in_specs = [row, row, vec, vec, vec, vec, vec]
        out_shape = [_sds((S, D), F32), _sds((S, D), BF16)]
        out_specs = [row, row]
    else:
        operands = (x_prev, ng_pre, sc, sh)
        in_specs = [row, vec, vec, vec]
        out_shape = [_sds((S, D), BF16)]
        out_specs = [row]
    return pl.pallas_call(body, name=name, grid=(S // ts,), in_specs=in_specs, out_specs=out_specs,
                          out_shape=out_shape, compiler_params=_cparams("parallel"))(*operands)


def _acc_rows(sums_ref, rows):
    for r, v in rows:
        sums_ref[r:r + 1, :] += jnp.sum(v, axis=0, keepdims=True)


def _post_norm_bwd(dxt, yv, gate, ng_post, sums_ref, dy_ref):
    r1 = _rms(yv)
    yhat = yv * r1
    dn = dxt * gate
    u = dn * ng_post
    dy = r1 * (u - yhat * jnp.mean(u * yhat, axis=-1, keepdims=True))
    dy_ref[...] = dy.astype(dy_ref.dtype)
    _acc_rows(sums_ref, [(3, dxt * (yhat * ng_post)), (4, dn * yhat)])


def _loss_boundary(name, x_prev, y, gate, ng_post, target):
    S, D = x_prev.shape
    ts = min(256, S)
    row = pl.BlockSpec((ts, D), lambda i: (i, 0))
    vec = _vec_spec(D)

    def body(x_ref, y_ref, g_ref, ngp_ref, t_ref, dx_ref, dy_ref, sums_ref, loss_ref):
        @pl.when(pl.program_id(0) == 0)
        def _():
            sums_ref[...] = jnp.zeros_like(sums_ref)
            loss_ref[...] = jnp.zeros_like(loss_ref)

        yv = y_ref[...]
        xf = x_ref[...] + g_ref[...] * (yv * _rms(yv) * ngp_ref[...])
        err = xf - t_ref[...]
        loss_ref[...] += 0.5 * jnp.sum(jnp.mean(err * err, axis=-1, keepdims=True))
        dxt = err / D
        dx_ref[...] = dxt
        _post_norm_bwd(dxt, yv, g_ref[...], ngp_ref[...], sums_ref, dy_ref)

    return pl.pallas_call(
        body, name=name, grid=(S // ts,),
        in_specs=[row, row, vec, vec, row],
        out_specs=[row, row, pl.BlockSpec((8, D), lambda i: (0, 0)), pl.BlockSpec((8, LANES), lambda i: (0, 0))],
        out_shape=[_sds((S, D), F32), _sds((S, D), BF16), _sds((8, D), F32), _sds((8, LANES), F32)],
        compiler_params=_cparams("arbitrary"))(x_prev, y, gate, ng_post, target)


def _bwd_boundary(name, dx_new, dh, x_new, y, gate, ng_post, ng_pre, sc, rider=NO_RIDER):
    S, D = x_new.shape
    ts = min(256, S)
    has_y = y is not None
    row = pl.BlockSpec((ts, D), lambda i: (i, 0))
    vec = _vec_spec(D)
    n_in, n_out = (8, 3) if has_y else (5, 2)
    n_ri, n_ro = len(rider.operands), len(rider.out_shape)

    def body(*refs):
        r_in = refs[n_in:n_in + n_ri]
        r_out = refs[n_in + n_ri + n_out:n_in + n_ri + n_out + n_ro]
        r_sems = refs[n_in + n_ri + n_out + n_ro:]
        own = refs[:n_in] + refs[n_in + n_ri:n_in + n_ri + n_out]
        if has_y:
            dxn_ref, dh_ref, x_ref, y_ref, g_ref, ngp_ref, ngn_ref, sc_ref, dxo_ref, dy_ref, sums_ref = own
        else:
            dxn_ref, dh_ref, x_ref, ngn_ref, sc_ref, dxo_ref, sums_ref = own

        @pl.when(pl.program_id(0) == 0)
        def _():
            sums_ref[...] = jnp.zeros_like(sums_ref)
            if rider.start is not None:
                rider.start(r_in, r_out, r_sems)

        xv = x_ref[...]
        dhv = dh_ref[...]
        ngn = ngn_ref[...]
        r2 = _rms(xv)
        xhat = xv * r2
        dn_pre = dhv * (1.0 + sc_ref[...])
        u2 = dn_pre * ngn
        dxt = dxn_ref[...] + r2 * (u2 - xhat * jnp.mean(u2 * xhat, axis=-1, keepdims=True))
        dxo_ref[...] = dxt
        _acc_rows(sums_ref, [(0, dhv), (1, dhv * (xhat * ngn)), (2, dn_pre * xhat)])
        if has_y:
            _post_norm_bwd(dxt, y_ref[...], g_ref[...], ngp_ref[...], sums_ref, dy_ref)

        if rider.finish is not None:
            @pl.when(pl.program_id(0) == S // ts - 1)
            def _():
                rider.finish(r_in, r_out, r_sems)

    sums_spec = pl.BlockSpec((8, D), lambda i: (0, 0))
    if has_y:
        operands = (dx_new, dh, x_new, y, gate, ng_post, ng_pre, sc)
        in_specs = [row, row, row, row, vec, vec, vec, vec]
        out_shape = [_sds((S, D), F32), _sds((S, D), BF16), _sds((8, D), F32)]
        out_specs = [row, row, sums_spec]
    else:
        operands = (dx_new, dh, x_new, ng_pre, sc)
        in_specs = [row, row, row, vec, vec]
        out_shape = [_sds((S, D), F32), _sds((8, D), F32)]
        out_specs = [row, sums_spec]
    res = pl.pallas_call(
        body, name=name, grid=(S // ts,), in_specs=in_specs + [ANY] * n_ri, out_specs=out_specs + [ANY] * n_ro,
        out_shape=out_shape + list(rider.out_shape), scratch_shapes=list(rider.sems),
        input_output_aliases={n_in + i: n_out + o for i, o in rider.aliases.items()},
        compiler_params=_cparams("arbitrary"))(*operands, *rider.operands)
    return (*res[:n_out], res[n_out:])


def _latent_fwd(lat, g_q, g_kv, rope_tabs, rank):
    S, W = lat.shape
    ts = min(256, S)
    tab = pl.BlockSpec((ts, LANES), lambda i: (i, 0))

    def body(lat_ref, gq_ref, gkv_ref, cos_ref, slo_ref, shi_ref, cq_ref, ckv_ref, kr_ref):
        lq = lat_ref[:, 0:rank]
        lkv = lat_ref[:, rank:2 * rank]
        cq_ref[...] = (lq * _rms(lq) * gq_ref[...]).astype(BF16)
        ckv_ref[...] = (lkv * _rms(lkv) * gkv_ref[...]).astype(BF16)
        kr_ref[...] = _rope(lat_ref[:, 2 * rank:W], cos_ref[...], slo_ref[...], shi_ref[...]).astype(BF16)

    return pl.pallas_call(
        body, name="mla_latent_fwd", grid=(S // ts,),
        in_specs=[pl.BlockSpec((ts, W), lambda i: (i, 0)), _vec_spec(rank), _vec_spec(rank), tab, tab, tab],
        out_specs=[pl.BlockSpec((ts, rank), lambda i: (i, 0)), pl.BlockSpec((ts, rank), lambda i: (i, 0)), tab],
        out_shape=[_sds((S, rank), BF16), _sds((S, rank), BF16), _sds((S, LANES), BF16)],
        compiler_params=_cparams("parallel"))(lat, g_q, g_kv, *rope_tabs)


def _latent_bwd(lat, dcq, dckv, dkr, g_q, g_kv, rope_tabs, rank):
    S, W = lat.shape
    ts = min(256, S)
    tab = pl.BlockSpec((ts, LANES), lambda i: (i, 0))
    half = pl.BlockSpec((ts, rank), lambda i: (i, 0))

    def body(lat_ref, dcq_ref, dckv_ref, dkr_ref, gq_ref, gkv_ref, cos_ref, slo_ref, shi_ref, dlat_ref, sums_ref):
        @pl.when(pl.program_id(0) == 0)
        def _():
            sums_ref[...] = jnp.zeros_like(sums_ref)

        def norm_bwd(v, dn, g, r):
            rr = _rms(v)
            vhat = v * rr
            u = dn * g
            sums_ref[r:r + 1, :] += jnp.sum(dn * vhat, axis=0, keepdims=True)
            return rr * (u - vhat * jnp.mean(u * vhat, axis=-1, keepdims=True))

        dlat_ref[:, 0:rank] = norm_bwd(lat_ref[:, 0:rank], dcq_ref[...], gq_ref[...], 0).astype(BF16)
        dlat_ref[:, rank:2 * rank] = norm_bwd(lat_ref[:, rank:2 * rank], dckv_ref[...], gkv_ref[...], 1).astype(BF16)
        dlat_ref[:, 2 * rank:W] = _rope_t(dkr_ref[...], cos_ref[...], slo_ref[...], shi_ref[...]).astype(BF16)

    return pl.pallas_call(
        body, name="mla_latent_bwd", grid=(S // ts,),
        in_specs=[pl.BlockSpec((ts, W), lambda i: (i, 0)), half, half, tab, _vec_spec(rank), _vec_spec(rank),
                  tab, tab, tab],
        out_specs=[pl.BlockSpec((ts, W), lambda i: (i, 0)), pl.BlockSpec((8, rank), lambda i: (0, 0))],
        out_shape=[_sds((S, W), BF16), _sds((8, rank), F32)],
        compiler_params=_cparams("arbitrary"))(lat, dcq, dckv, dkr, g_q, g_kv, *rope_tabs)


def _attn_tiles(S):
    t = min(512, S)
    return t, S // t


def _causal_mask(t):
    return lax.broadcasted_iota(jnp.int32, (t, t), 1) <= lax.broadcasted_iota(jnp.int32, (t, t), 0)


def _attn_fwd(q, kv, kr, heads, scale, rider=NO_RIDER):
    S = q.shape[0]
    t, nb = _attn_tiles(S)
    G = 2 if heads % 2 == 0 else 1
    n_ri, n_ro = len(rider.operands), len(rider.out_shape)

    def body(*refs):
        q_ref, kv_ref, kr_ref = refs[:3]
        r_in = refs[3:3 + n_ri]
        o_ref, lse_ref = refs[3 + n_ri:5 + n_ri]
        r_out = refs[5 + n_ri:5 + n_ri + n_ro]
        m_scr, acc_scr = refs[5 + n_ri + n_ro:7 + n_ri + n_ro]
        r_sems = refs[7 + n_ri + n_ro:]
        h, qi, ki = pl.program_id(0), pl.program_id(1), pl.program_id(2)

        if rider.start is not None:
            @pl.when((h == 0) & (qi == 0) & (ki == 0))
            def _():
                rider.start(r_in, r_out, r_sems)

        @pl.when(ki == 0)
        def _():
            m_scr[...] = jnp.full_like(m_scr, NEG_INF)
            acc_scr[...] = jnp.zeros_like(acc_scr)

        def step(diagonal):
            ones = jnp.ones((t, LANES), BF16)
            for g in range(G):
                kcat = jnp.concatenate([kv_ref[:, g * QK_PAD:g * QK_PAD + QK_NOPE], kr_ref[...]], axis=1)
                vext = jnp.concatenate([kv_ref[:, g * QK_PAD + QK_NOPE:(g + 1) * QK_PAD], ones], axis=1)
                s = lax.dot_general(q_ref[:, g * QK_PAD:(g + 1) * QK_PAD], kcat, DIMS_NT,
                                    preferred_element_type=F32) * scale
                if diagonal:
                    s = jnp.where(_causal_mask(t), s, NEG_INF)
                m_prev = m_scr[g]
                m_new = jnp.maximum(m_prev, jnp.max(s, axis=-1, keepdims=True))
                alpha = jnp.exp(m_prev - m_new)
                p = jnp.exp(s - jnp.tile(m_new, (1, t // LANES)))
                acc_scr[g] = jnp.tile(alpha, (1, 2)) * acc_scr[g] + lax.dot_general(
                    p.astype(BF16), vext, DIMS_NN, preferred_element_type=F32)
                m_scr[g] = m_new

        @pl.when(ki < qi)
        def _():
            step(False)

        @pl.when(ki == qi)
        def _():
            step(True)

        @pl.when(ki == nb - 1)
        def _():
            for g in range(G):
                acc = acc_scr[g]
                o_ref[:, g * V_HEAD:(g + 1) * V_HEAD] = (acc[:, 0:V_HEAD] / acc[:, V_HEAD:2 * V_HEAD]).astype(BF16)
                lse_ref[g] = m_scr[g] + jnp.log(acc[:, V_HEAD:2 * V_HEAD])

        if rider.finish is not None:
            @pl.when((h == heads // G - 1) & (qi == nb - 1) & (ki == nb - 1))
            def _():
                rider.finish(r_in, r_out, r_sems)

    res = pl.pallas_call(
        body, name="mla_attn_fwd", grid=(heads // G, nb, nb),
        in_specs=[pl.BlockSpec((t, G * QK_PAD), lambda h, qi, ki: (qi, h)),
                  pl.BlockSpec((t, G * QK_PAD), lambda h, qi, ki: (jnp.minimum(ki, qi), h)),
                  pl.BlockSpec((t, LANES), lambda h, qi, ki: (jnp.minimum(ki, qi), 0))] + [ANY] * n_ri,
        out_specs=[pl.BlockSpec((t, G * V_HEAD), lambda h, qi, ki: (qi, h)),
                   pl.BlockSpec((G, t, LANES), lambda h, qi, ki: (h, qi, 0))] + [ANY] * n_ro,
        out_shape=[_sds((S, heads * V_HEAD), BF16), _sds((heads, S, LANES), F32)] + list(rider.out_shape),
        scratch_shapes=[pltpu.VMEM((G, t, LANES), F32), pltpu.VMEM((G, t, 2 * V_HEAD), F32)] + list(rider.sems),
        input_output_aliases={3 + i: 2 + o for i, o in rider.aliases.items()},
        compiler_params=_cparams("arbitrary", "arbitrary", "arbitrary"))(q, kv, kr, *rider.operands)
    return res[0], res[1], res[2:]


def _attn_delta(o, do, heads):
    S = o.shape[0]
    t, nb = _attn_tiles(S)

    def body(o_ref, do_ref, out_ref):
        d = jnp.sum(do_ref[...].astype(F32) * o_ref[...].astype(F32), axis=-1, keepdims=True)
        out_ref[...] = jnp.broadcast_to(d, (t, LANES))

    blk = pl.BlockSpec((t, V_HEAD), lambda h, i: (i, h))
    return pl.pallas_call(
        body, name="mla_attn_delta", grid=(heads, nb), in_specs=[blk, blk],
        out_specs=pl.BlockSpec((None, t, LANES), lambda h, i: (h, i, 0)),
        out_shape=_sds((heads, S, LANES), F32), compiler_params=_cparams("parallel", "parallel"))(o, do)


def _attn_bwd(q, kv, kr, delta, do, lse, rope_tabs, heads, scale, rider=NO_RIDER):
    S = q.shape[0]
    t, nb = _attn_tiles(S)
    n_ri, n_ro = len(rider.operands), len(rider.out_shape)
    rep = t // LANES

    def body(*refs):
        q_ref, kv_ref, kr_ref, delta_ref, do_ref, lse_ref, cos_ref, slo_ref, shi_ref = refs[:9]
        r_in = refs[9:9 + n_ri]
        dq_ref, dkv_ref, dkr_ref = refs[9 + n_ri:12 + n_ri]
        r_out = refs[12 + n_ri:12 + n_ri + n_ro]
        dq_scr, dk_scr, dv_scr, dkr_scr = refs[12 + n_ri + n_ro:16 + n_ri + n_ro]
        r_sems = refs[16 + n_ri + n_ro:]
        h, ki, qi = pl.program_id(0), pl.program_id(1), pl.program_id(2)
        q_rows = pl.ds(pl.multiple_of(qi * t, t), t)
        k_rows = pl.ds(pl.multiple_of(ki * t, t), t)

        if rider.start is not None:
            @pl.when((h == 0) & (ki == 0) & (qi == 0))
            def _():
                rider.start(r_in, r_out, r_sems)

        @pl.when((ki == 0) & (qi == 0))
        def _():
            dq_scr[...] = jnp.zeros_like(dq_scr)

        @pl.when((h == 0) & (ki == 0) & (qi == 0))
        def _():
            dkr_scr[...] = jnp.zeros_like(dkr_scr)

        @pl.when(qi == 0)
        def _():
            dk_scr[...] = jnp.zeros_like(dk_scr)
            dv_scr[...] = jnp.zeros_like(dv_scr)

        def step(diagonal):
            qv = q_ref[...]
            kcat = jnp.concatenate([kv_ref[:, 0:QK_NOPE], kr_ref[...]], axis=1)
            s = lax.dot_general(qv, kcat, DIMS_NT, preferred_element_type=F32) * scale
            p = jnp.exp(s - jnp.tile(lse_ref[...], (1, rep)))
            if diagonal:
                p = jnp.where(_causal_mask(t), p, 0.0)
            dov = do_ref[...]
            dv_scr[...] += lax.dot_general(p.astype(BF16), dov, DIMS_TN, preferred_element_type=F32)
            dp = lax.dot_general(dov, kv_ref[:, QK_NOPE:QK_NOPE + V_HEAD], DIMS_NT, preferred_element_type=F32)
            ds = (p * (dp - jnp.tile(delta_ref[...], (1, rep))) * scale).astype(BF16)
            dk_scr[...] += lax.dot_general(ds, qv, DIMS_TN, preferred_element_type=F32)
            dq_scr[q_rows, :] += lax.dot_general(ds, kcat, DIMS_NN, preferred_element_type=F32)

        @pl.when(qi > ki)
        def _():
            step(False)

        @pl.when(qi == ki)
        def _():
            step(True)

        @pl.when(qi == nb - 1)
        def _():
            dkv_ref[...] = jnp.concatenate([dk_scr[:, 0:QK_NOPE], dv_scr[...]], axis=1).astype(BF16)
            dkr_scr[k_rows, :] += dk_scr[:, QK_NOPE:QK_PAD]

        @pl.when(ki == nb - 1)
        def _():
            dqv = dq_scr[q_rows, :]
            dq_ref[q_rows, :] = jnp.concatenate(
                [dqv[:, 0:QK_NOPE], _rope_t(dqv[:, QK_NOPE:QK_PAD], cos_ref[...], slo_ref[...], shi_ref[...])],
                axis=1).astype(BF16)

        @pl.when((h == heads - 1) & (ki == nb - 1) & (qi == nb - 1))
        def _():
            dkr_ref[...] = dkr_scr[...]
            if rider.finish is not None:
                rider.finish(r_in, r_out, r_sems)

    qmap = lambda h, ki, qi: (jnp.maximum(qi, ki), h)
    stat = pl.BlockSpec((None, t, LANES), lambda h, ki, qi: (h, jnp.maximum(qi, ki), 0))
    tab = pl.BlockSpec((t, LANES), lambda h, ki, qi: (qi, 0))
    res = pl.pallas_call(
        body, name="mla_attn_bwd", grid=(heads, nb, nb),
        in_specs=[pl.BlockSpec((t, QK_PAD), qmap),
                  pl.BlockSpec((t, QK_PAD), lambda h, ki, qi: (ki, h)),
                  pl.BlockSpec((t, LANES), lambda h, ki, qi: (ki, 0)),
                  stat,
                  pl.BlockSpec((t, V_HEAD), qmap),
                  stat,
                  tab, tab, tab] + [ANY] * n_ri,
        out_specs=[pl.BlockSpec((S, QK_PAD), lambda h, ki, qi: (0, h)),
                   pl.BlockSpec((t, QK_PAD), lambda h, ki, qi: (ki, h)),
                   pl.BlockSpec((S, LANES), lambda h, ki, qi: (0, 0))] + [ANY] * n_ro,
        out_shape=[_sds((S, heads * QK_PAD), BF16), _sds((S, heads * QK_PAD), BF16), _sds((S, LANES), F32)]
        + list(rider.out_shape),
        scratch_shapes=[pltpu.VMEM((S, QK_PAD), F32), pltpu.VMEM((t, QK_PAD), F32), pltpu.VMEM((t, V_HEAD), F32),
                        pltpu.VMEM((S, LANES), F32)] + list(rider.sems),
        input_output_aliases={9 + i: 3 + o for i, o in rider.aliases.items()},
        compiler_params=_cparams("arbitrary", "arbitrary", "arbitrary"))(q, kv, kr, delta, do, lse, *rope_tabs, *rider.operands)
    return res[0], res[1], res[2], res[3:]


def _causal_pairs(nb, q_major):
    if q_major:
        pairs = [(qi, ki) for qi in range(nb) for ki in range(qi + 1)]
    else:
        pairs = [(qi, ki) for ki in range(nb) for qi in range(ki, nb)]
    return jnp.array([p[0] for p in pairs], jnp.int32), jnp.array([p[1] for p in pairs], jnp.int32), len(pairs)


def _heads_per_step(heads):
    return 2 if heads % 2 == 0 else 1


def _attn_fwd_tri(q, kv, kr, heads, scale, rider=NO_RIDER):
    S = q.shape[0]
    t, nb = _attn_tiles(S)
    G = _heads_per_step(heads)
    q_tab, k_tab, n_pairs = _causal_pairs(nb, True)
    n_ri, n_ro = len(rider.operands), len(rider.out_shape)

    def body(qt_ref, kt_ref, *refs):
        q_ref, kv_ref, kr_ref = refs[:3]
        r_in = refs[3:3 + n_ri]
        o_ref, lse_ref = refs[3 + n_ri:5 + n_ri]
        r_out = refs[5 + n_ri:5 + n_ri + n_ro]
        m_scr, acc_scr = refs[5 + n_ri + n_ro:7 + n_ri + n_ro]
        r_sems = refs[7 + n_ri + n_ro:]
        h, p = pl.program_id(0), pl.program_id(1)
        qi, ki = qt_ref[p], kt_ref[p]

        if rider.start is not None:
            @pl.when((h == 0) & (p == 0))
            def _():
                rider.start(r_in, r_out, r_sems)

        @pl.when(ki == 0)
        def _():
            m_scr[...] = jnp.full_like(m_scr, NEG_INF)
            acc_scr[...] = jnp.zeros_like(acc_scr)

        def step(diagonal):
            ones = jnp.ones((t, LANES), BF16)
            for g in range(G):
                kcat = jnp.concatenate([kv_ref[:, g * QK_PAD:g * QK_PAD + QK_NOPE], kr_ref[...]], axis=1)
                vext = jnp.concatenate([kv_ref[:, g * QK_PAD + QK_NOPE:(g + 1) * QK_PAD], ones], axis=1)
                s = lax.dot_general(q_ref[:, g * QK_PAD:(g + 1) * QK_PAD], kcat, DIMS_NT,
                                    preferred_element_type=F32) * scale
                if diagonal:
                    s = jnp.where(_causal_mask(t), s, NEG_INF)
                m_prev = m_scr[g]
                m_new = jnp.maximum(m_prev, jnp.max(s, axis=-1, keepdims=True))
                alpha = jnp.exp(m_prev - m_new)
                pr = jnp.exp(s - jnp.tile(m_new, (1, t // LANES)))
                acc_scr[g] = jnp.tile(alpha, (1, 2)) * acc_scr[g] + lax.dot_general(
                    pr.astype(BF16), vext, DIMS_NN, preferred_element_type=F32)
                m_scr[g] = m_new

        @pl.when(ki < qi)
        def _():
            step(False)

        @pl.when(ki == qi)
        def _():
            step(True)
            for g in range(G):
                acc = acc_scr[g]
                o_ref[:, g * V_HEAD:(g + 1) * V_HEAD] = (acc[:, 0:V_HEAD] / acc[:, V_HEAD:2 * V_HEAD]).astype(BF16)
                lse_ref[g] = m_scr[g] + jnp.log(acc[:, V_HEAD:2 * V_HEAD])

        if rider.finish is not None:
            @pl.when((h == heads // G - 1) & (p == n_pairs - 1))
            def _():
                rider.finish(r_in, r_out, r_sems)

    res = pl.pallas_call(
        body, name="mla_attn_fwd",
        grid_spec=pltpu.PrefetchScalarGridSpec(
            num_scalar_prefetch=2, grid=(heads // G, n_pairs),
            in_specs=[pl.BlockSpec((t, G * QK_PAD), lambda h, p, qt, kt: (qt[p], h)),
                      pl.BlockSpec((t, G * QK_PAD), lambda h, p, qt, kt: (kt[p], h)),
                      pl.BlockSpec((t, LANES), lambda h, p, qt, kt: (kt[p], 0))] + [ANY] * n_ri,
            out_specs=[pl.BlockSpec((t, G * V_HEAD), lambda h, p, qt, kt: (qt[p], h)),
                       pl.BlockSpec((G, t, LANES), lambda h, p, qt, kt: (h, qt[p], 0))] + [ANY] * n_ro,
            scratch_shapes=[pltpu.VMEM((G, t, LANES), F32), pltpu.VMEM((G, t, 2 * V_HEAD), F32)] + list(rider.sems)),
        out_shape=[_sds((S, heads * V_HEAD), BF16), _sds((heads, S, LANES), F32)] + list(rider.out_shape),
        input_output_aliases={5 + i: 2 + o for i, o in rider.aliases.items()},
        compiler_params=_cparams("arbitrary", "arbitrary"))(q_tab, k_tab, q, kv, kr, *rider.operands)
    return res[0], res[1], res[2:]


def _attn_bwd_tri(q, kv, kr, o, do, lse, rope_tabs, heads, scale, rider=NO_RIDER):
    S = q.shape[0]
    t, nb = _attn_tiles(S)
    G = _heads_per_step(heads)
    q_tab, k_tab, n_pairs = _causal_pairs(nb, False)
    n_ri, n_ro = len(rider.operands), len(rider.out_shape)
    rep = t // LANES

    def body(qt_ref, kt_ref, *refs):
        q_ref, kv_ref, kr_ref, o_ref, do_ref, lse_ref, cos_ref, slo_ref, shi_ref = refs[:9]
        r_in = refs[9:9 + n_ri]
        dq_ref, dkv_ref, dkr_ref = refs[9 + n_ri:12 + n_ri]
        r_out = refs[12 + n_ri:12 + n_ri + n_ro]
        dq_scr, dk_scr, dv_scr, dkr_scr, delta_scr = refs[12 + n_ri + n_ro:17 + n_ri + n_ro]
        r_sems = refs[17 + n_ri + n_ro:]
        h, p = pl.program_id(0), pl.program_id(1)
        qi, ki = qt_ref[p], kt_ref[p]
        q_rows = pl.ds(pl.multiple_of(qi * t, t), t)
        k_rows = pl.ds(pl.multiple_of(ki * t, t), t)

        @pl.when(ki == 0)
        def _():
            for g in range(G):
                cols = slice(g * V_HEAD, (g + 1) * V_HEAD)
                d = jnp.sum(do_ref[:, cols].astype(F32) * o_ref[:, cols].astype(F32), axis=-1, keepdims=True)
                delta_scr[g, q_rows, :] = jnp.broadcast_to(d, (t, LANES))

        if rider.start is not None:
            @pl.when((h == 0) & (p == 0))
            def _():
                rider.start(r_in, r_out, r_sems)

        @pl.when(p == 0)
        def _():
            dq_scr[...] = jnp.zeros_like(dq_scr)

        @pl.when((h == 0) & (p == 0))
        def _():
            dkr_scr[...] = jnp.zeros_like(dkr_scr)

        @pl.when(qi == ki)
        def _():
            dk_scr[...] = jnp.zeros_like(dk_scr)
            dv_scr[...] = jnp.zeros_like(dv_scr)

        def step(diagonal):
            for g in range(G):
                qv = q_ref[:, g * QK_PAD:(g + 1) * QK_PAD]
                kcat = jnp.concatenate([kv_ref[:, g * QK_PAD:g * QK_PAD + QK_NOPE], kr_ref[...]], axis=1)
                s = lax.dot_general(qv, kcat, DIMS_NT, preferred_element_type=F32) * scale
                pr = jnp.exp(s - jnp.tile(lse_ref[g], (1, rep)))
                if diagonal:
                    pr = jnp.where(_causal_mask(t), pr, 0.0)
                dov = do_ref[:, g * V_HEAD:(g + 1) * V_HEAD]
                dv_scr[g] += lax.dot_general(pr.astype(BF16), dov, DIMS_TN, preferred_element_type=F32)
                dp = lax.dot_general(dov, kv_ref[:, g * QK_PAD + QK_NOPE:(g + 1) * QK_PAD], DIMS_NT,
                                     preferred_element_type=F32)
                ds = (pr * (dp - jnp.tile(delta_scr[g, q_rows, :], (1, rep))) * scale).astype(BF16)
                dk_scr[g] += lax.dot_general(ds, qv, DIMS_TN, preferred_element_type=F32)
                dq_scr[q_rows, g * QK_PAD:(g + 1) * QK_PAD] += lax.dot_general(ds, kcat, DIMS_NN,
                                                                               preferred_element_type=F32)

        @pl.when(qi > ki)
        def _():
            step(False)

        @pl.when(qi == ki)
        def _():
            step(True)
            for g in range(G):
                dqv = dq_scr[q_rows, g * QK_PAD:(g + 1) * QK_PAD]
                dq_ref[q_rows, g * QK_PAD:(g + 1) * QK_PAD] = jnp.concatenate(
                    [dqv[:, 0:QK_NOPE], _rope_t(dqv[:, QK_NOPE:QK_PAD], cos_ref[...], slo_ref[...], shi_ref[...])],
                    axis=1).astype(BF16)

        @pl.when(qi == nb - 1)
        def _():
            for g in range(G):
                dkv_ref[:, g * QK_PAD:(g + 1) * QK_PAD] = jnp.concatenate(
                    [dk_scr[g][:, 0:QK_NOPE], dv_scr[g]], axis=1).astype(BF16)
                dkr_scr[k_rows, :] += dk_scr[g][:, QK_NOPE:QK_PAD]

        @pl.when((h == heads // G - 1) & (p == n_pairs - 1))
        def _():
            dkr_ref[...] = dkr_scr[...]
            if rider.finish is not None:
                rider.finish(r_in, r_out, r_sems)

    q_blk = lambda w: pl.BlockSpec((t, G * w), lambda h, p, qt, kt: (qt[p], h))
    stat = pl.BlockSpec((G, t, LANES), lambda h, p, qt, kt: (h, qt[p], 0))
    tab = pl.BlockSpec((t, LANES), lambda h, p, qt, kt: (kt[p], 0))
    res = pl.pallas_call(
        body, name="mla_attn_bwd",
        grid_spec=pltpu.PrefetchScalarGridSpec(
            num_scalar_prefetch=2, grid=(heads // G, n_pairs),
            in_specs=[q_blk(QK_PAD),
                      pl.BlockSpec((t, G * QK_PAD), lambda h, p, qt, kt: (kt[p], h)),
                      tab, q_blk(V_HEAD), q_blk(V_HEAD), stat, tab, tab, tab] + [ANY] * n_ri,
            out_specs=[pl.BlockSpec((S, G * QK_PAD), lambda h, p, qt, kt: (0, h)),
                       pl.BlockSpec((t, G * QK_PAD), lambda h, p, qt, kt: (kt[p], h)),
                       pl.BlockSpec((S, LANES), lambda h, p, qt, kt: (0, 0))] + [ANY] * n_ro,
            scratch_shapes=[pltpu.VMEM((S, G * QK_PAD), F32), pltpu.VMEM((G, t, QK_PAD), F32),
                            pltpu.VMEM((G, t, V_HEAD), F32), pltpu.VMEM((S, LANES), F32),
                            pltpu.VMEM((G, S, LANES), F32)] + list(rider.sems)),
        out_shape=[_sds((S, heads * QK_PAD), BF16), _sds((S, heads * QK_PAD), BF16), _sds((S, LANES), F32)]
        + list(rider.out_shape),
        input_output_aliases={11 + i: 3 + o for i, o in rider.aliases.items()},
        compiler_params=_cparams("arbitrary", "arbitrary"))(q_tab, k_tab, q, kv, kr, o, do, lse, *rope_tabs,
                                                            *rider.operands)
    return res[0], res[1], res[2], res[3:]


def _shift_down(z, n, rows):
    return jnp.where(rows >= n, pltpu.roll(z, n, 0), 0.0)


def _shift_up(z, n, rows, S):
    return jnp.where(rows < S - n, pltpu.roll(z, S - n, 0), 0.0)


def _conv_specs(S, tc):
    strip = lambda p: pl.BlockSpec((None, S, tc), lambda j: (p, 0, j))
    return strip(0), strip(1), strip(2), pl.BlockSpec((3, tc), lambda j: (0, j))


def _conv_fwd(proj3, w):
    _, S, D = proj3.shape
    tc = LANES

    def body(b_ref, c_ref, u_ref, w_ref, out_ref):
        z = c_ref[...] * u_ref[...]
        rows = lax.broadcasted_iota(jnp.int32, (S, tc), 0)
        zc = w_ref[0:1, :] * _shift_down(z, 2, rows) + w_ref[1:2, :] * _shift_down(z, 1, rows) + w_ref[2:3, :] * z
        out_ref[...] = (b_ref[...] * zc).astype(BF16)

    return pl.pallas_call(
        body, name="conv_fwd", grid=(D // tc,), in_specs=list(_conv_specs(S, tc)),
        out_specs=pl.BlockSpec((S, tc), lambda j: (0, j)), out_shape=_sds((S, D), BF16),
        compiler_params=_cparams("parallel"))(proj3, proj3, proj3, w)


def _conv_bwd(dbz, proj3, w):
    _, S, D = proj3.shape
    tc = LANES

    def body(d_ref, b_ref, c_ref, u_ref, w_ref, dp_ref, dw_ref):
        cv, uv, dv = c_ref[...], u_ref[...], d_ref[...]
        z = cv * uv
        rows = lax.broadcasted_iota(jnp.int32, (S, tc), 0)
        z1, z2 = _shift_down(z, 1, rows), _shift_down(z, 2, rows)
        zc = w_ref[0:1, :] * z2 + w_ref[1:2, :] * z1 + w_ref[2:3, :] * z
        dp_ref[0] = (dv * zc).astype(BF16)
        dzc = dv * b_ref[...]
        dz = w_ref[2:3, :] * dzc + w_ref[1:2, :] * _shift_up(dzc, 1, rows, S) + w_ref[0:1, :] * _shift_up(dzc, 2, rows, S)
        dp_ref[1] = (dz * uv).astype(BF16)
        dp_ref[2] = (dz * cv).astype(BF16)
        dw_ref[0:1, :] = jnp.sum(dzc * z2, axis=0, keepdims=True)
        dw_ref[1:2, :] = jnp.sum(dzc * z1, axis=0, keepdims=True)
        dw_ref[2:3, :] = jnp.sum(dzc * z, axis=0, keepdims=True)

    sb, sc_, su, sw = _conv_specs(S, tc)
    return pl.pallas_call(
        body, name="conv_bwd", grid=(D // tc,),
        in_specs=[pl.BlockSpec((S, tc), lambda j: (0, j)), sb, sc_, su, sw],
        out_specs=[pl.BlockSpec((3, S, tc), lambda j: (0, 0, j)), pl.BlockSpec((3, tc), lambda j: (0, j))],
        out_shape=[_sds((3, S, D), BF16), _sds((3, D), F32)],
        compiler_params=_cparams("parallel"))(dbz, proj3, proj3, proj3, w)


def _silu(c_all):
    def body(c_ref, o_ref):
        cv = c_ref[...]
        o_ref[...] = cv * (1.0 / (1.0 + jnp.exp(-cv)))

    vm = pl.BlockSpec(memory_space=pltpu.VMEM)
    return pl.pallas_call(body, name="cond_silu", in_specs=[vm], out_specs=vm, out_shape=_sds(c_all.shape, F32))(c_all)


def _mod_fwd(cond, w_mod, b_cols):
    L, D, ncol = w_mod.shape
    B = cond.shape[0]
    tk, tn = min(512, D), min(1024, ncol)
    nk = D // tk

    def body(c_ref, w_ref, b_ref, out_ref, acc):
        kk = pl.program_id(2)
        part = lax.dot_general(c_ref[...].astype(BF16), w_ref[...].astype(BF16), DIMS_NN, preferred_element_type=F32)

        @pl.when(kk == 0)
        def _():
            acc[...] = part

        @pl.when(kk > 0)
        def _():
            acc[...] += part

        @pl.when(kk == nk - 1)
        def _():
            out_ref[...] = acc[...] + b_ref[...]

    return pl.pallas_call(
        body, name="mod_fwd", grid=(L, ncol // tn, nk),
        in_specs=[pl.BlockSpec((B, tk), lambda l, j, k: (0, k)),
                  pl.BlockSpec((None, tk, tn), lambda l, j, k: (l, k, j)),
                  pl.BlockSpec((None, 1, tn), lambda l, j, k: (l, 0, j))],
        out_specs=pl.BlockSpec((None, B, tn), lambda l, j, k: (l, 0, j)),
        out_shape=_sds((L, B, ncol), F32),
        scratch_shapes=[pltpu.VMEM((B, tn), F32)],
        compiler_params=_cparams("parallel", "parallel", "arbitrary"))(cond, w_mod, b_cols)


def _adamw_math(w, g, m, v):
    m = ADAM_B1 * m + (1.0 - ADAM_B1) * g
    v = ADAM_B2 * v + (1.0 - ADAM_B2) * (g * g)
    m_hat = m / (1.0 - ADAM_B1 ** ADAM_STEP)
    v_hat = v / (1.0 - ADAM_B2 ** ADAM_STEP)
    delta = -ADAM_LR * (m_hat / (jnp.sqrt(v_hat) + ADAM_EPS) + ADAM_WD * w)
    return delta, m, v


def _adamw(name, w, g, m, v):
    shape = w.shape
    cols = shape[-1] if w.ndim <= 3 else shape[-2] * shape[-1]
    rows = w.size // cols
    w2, g2, m2, v2 = (t.reshape(rows, cols) for t in (w, g, m, v))
    tr = _row_tile(rows, cols * 4, limit=1024 * 1024, mult=8)
    spec = pl.BlockSpec((tr, cols), lambda i: (i, 0))

    def body(w_ref, g_ref, m_ref, v_ref, d_ref, nm_ref, nv_ref):
        d, nm, nv = _adamw_math(w_ref[...], g_ref[...], m_ref[...], v_ref[...])
        d_ref[...] = d
        nm_ref[...] = nm
        nv_ref[...] = nv

    outs = pl.pallas_call(body, name=name, grid=(rows // tr,), in_specs=[spec] * 4, out_specs=[spec] * 3,
                          out_shape=[_sds((rows, cols), F32)] * 3, compiler_params=_cparams("parallel"))(w2, g2, m2, v2)
    return tuple(t.reshape(shape) for t in outs)


def _adamw_mod(w, cond_t, dmod_cols, m, v, rider=NO_RIDER):
    L, D, ncol = w.shape
    B = cond_t.shape[1]
    tr, tc = min(256, D), min(1024, ncol)
    blk = pl.BlockSpec((None, tr, tc), lambda l, i, j: (l, i, j))
    grid = (L, D // tr, ncol // tc)
    n_ri, n_ro = len(rider.operands), len(rider.out_shape)

    def body(*refs):
        w_ref, ct_ref, dm_ref, m_ref, v_ref = refs[:5]
        r_in = refs[5:5 + n_ri]
        g_ref, d_ref, nm_ref, nv_ref = refs[5 + n_ri:9 + n_ri]
        r_out = refs[9 + n_ri:9 + n_ri + n_ro]
        r_sems = refs[9 + n_ri + n_ro:]
        ids = [pl.program_id(a) for a in range(3)]

        if rider.start is not None:
            @pl.when((ids[0] == 0) & (ids[1] == 0) & (ids[2] == 0))
            def _():
                rider.start(r_in, r_out, r_sems)

        g = lax.dot_general(ct_ref[...], dm_ref[...], DIMS_NN, precision=lax.Precision.HIGHEST,
                            preferred_element_type=F32)
        d, nm, nv = _adamw_math(w_ref[...], g, m_ref[...], v_ref[...])
        g_ref[...] = g
        d_ref[...] = d
        nm_ref[...] = nm
        nv_ref[...] = nv

        if rider.finish is not None:
            @pl.when((ids[0] == grid[0] - 1) & (ids[1] == grid[1] - 1) & (ids[2] == grid[2] - 1))
            def _():
                rider.finish(r_in, r_out, r_sems)

    hosted = rider.start is not None
    res = pl.pallas_call(
        body, name="adamw_w_mod", grid=grid,
        in_specs=[blk, pl.BlockSpec((tr, B), lambda l, i, j: (i, 0)),
                  pl.BlockSpec((None, B, tc), lambda l, i, j: (l, 0, j)), blk, blk] + [ANY] * n_ri,
        out_specs=[blk] * 4 + [ANY] * n_ro, out_shape=[_sds((L, D, ncol), F32)] * 4 + list(rider.out_shape),
        scratch_shapes=list(rider.sems), input_output_aliases={5 + i: 4 + o for i, o in rider.aliases.items()},
        compiler_params=_cparams(*(("arbitrary",) * 3 if hosted else ("parallel",) * 3)))(
            w, cond_t, dmod_cols, m, v, *rider.operands)
    return (*res[:4], res[4:])


def _cast_into_full(name, w, kind, k_idx):
    L, R, C = w.shape
    Rh = R // 2
    tr = _row_tile(Rh, C * 4)
    if kind == "row":
        out_shape = (L, N_CHIPS, 2, Rh, C)
        out_spec = pl.BlockSpec((None, None, None, tr, C), lambda l, h, i, k_ref: (l, k_ref[0], h, i, 0))
    else:
        out_shape = (L, 2, Rh, N_CHIPS * C)
        out_spec = pl.BlockSpec((None, None, tr, C), lambda l, h, i, k_ref: (l, h, i, k_ref[0]))

    def body(k_ref, w_ref, o_ref):
        o_ref[...] = w_ref[...].astype(BF16)

    return pl.pallas_call(
        body, name=name,
        grid_spec=pltpu.PrefetchScalarGridSpec(
            num_scalar_prefetch=1, grid=(L, 2, Rh // tr),
            in_specs=[pl.BlockSpec((None, None, tr, C), lambda l, h, i, k_ref: (l, h, i, 0))],
            out_specs=out_spec),
        out_shape=_sds(out_shape, BF16),
        compiler_params=_cparams("parallel", "parallel", "parallel"))(k_idx, w.reshape(L, 2, Rh, C))


def _pair_sum(name, g5, ra, c_idx):
    L, A, _, Rh, Cc = g5.shape
    tr = _row_tile(Rh, Cc * 4)

    def body(c_ref, g_ref, r_ref, o_ref):
        o_ref[...] = (g_ref[...].astype(F32) + r_ref[...].astype(F32)).astype(BF16)

    blk = pl.BlockSpec((None, None, tr, Cc), lambda l, a, i, c_ref: (l, a, i, 0))
    return pl.pallas_call(
        body, name=name,
        grid_spec=pltpu.PrefetchScalarGridSpec(
            num_scalar_prefetch=1, grid=(L, A, Rh // tr),
            in_specs=[pl.BlockSpec((None, None, None, tr, Cc), lambda l, a, i, c_ref: (l, a, c_ref[0], i, 0)), blk],
            out_specs=blk),
        out_shape=_sds((L, A, Rh, Cc), BF16),
        compiler_params=_cparams("parallel", "parallel", "parallel"))(c_idx, g5, ra)


def _chip_sum(name, p, rb, kc_idx, kind, layer=0, n_layers=1, prev=None):
    _, A, Rh, Cc = p.shape
    C = rb.shape[-1]
    tr = _row_tile(Rh, C * 4)
    if kind == "row":
        own = pl.BlockSpec((None, None, tr, C), lambda i, kc: (0, kc[0], i, 0))
    else:
        own = pl.BlockSpec((None, None, tr, C), lambda i, kc: (0, 0, i, kc[0]))
    peer = lambda j: pl.BlockSpec((None, None, tr, C), lambda i, kc: (j, 0, i, 0))

    def body(kc_ref, p_ref, r0_ref, r1_ref, r2_ref, *rest):
        o_ref = rest[-1]
        o_ref[...] = ((p_ref[...].astype(F32) + r0_ref[...].astype(F32)) + r1_ref[...].astype(F32)) + r2_ref[...].astype(F32)

    operands = [kc_idx, p, rb, rb, rb] + ([prev] if prev is not None else [])
    return pl.pallas_call(
        body, name=name,
        grid_spec=pltpu.PrefetchScalarGridSpec(
            num_scalar_prefetch=1, grid=(Rh // tr,),
            in_specs=[own, peer(0), peer(1), peer(2)] + ([ANY] if prev is not None else []),
            out_specs=pl.BlockSpec((None, None, tr, C), lambda i, kc: (layer, kc[1], i, 0))),
        out_shape=_sds((n_layers, 2, Rh, C), F32),
        input_output_aliases={5: 0} if prev is not None else {},
        compiler_params=_cparams("parallel"))(*operands)


def _mesh_place():
    x, y, c = lax.axis_index("x"), lax.axis_index("y"), lax.axis_index("c")
    chips = [(1 - x, y), (x, 1 - y), (1 - x, 1 - y)]
    return x, y, c, chips


def _remote(src, dst, send_sem, recv_sem, to):
    return pltpu.make_async_remote_copy(src_ref=src, dst_ref=dst, send_sem=send_sem, recv_sem=recv_sem,
                                        device_id=to, device_id_type=MESH_ID)


def _small_allgather(name, v, with_sum=False, rider=NO_RIDER):
    R, N = v.shape
    n_ri, n_ro, n_own = len(rider.operands), len(rider.out_shape), 2 if with_sum else 1

    def body(*refs):
        r_in = refs[1:1 + n_ri]
        r_out = refs[1 + n_ri + n_own:1 + n_ri + n_own + n_ro]
        r_sems = refs[1 + n_ri + n_own + n_ro + 3:]
        own = (refs[0],) + refs[1 + n_ri:1 + n_ri + n_own] + refs[1 + n_ri + n_own + n_ro:1 + n_ri + n_own + n_ro + 3]
        if with_sum:
            x_ref, out_ref, sum_ref, send_sems, recv_sems, local_sem = own
        else:
            x_ref, out_ref, send_sems, recv_sems, local_sem = own
        if rider.start is not None:
            rider.start(r_in, r_out, r_sems)
        x, y, c, chips = _mesh_place()
        me, sibling = (x, y, c), (x, y, 1 - c)

        def rows(px, py, pc):
            return out_ref.at[pl.ds((4 * px + 2 * py + pc) * R, R), :]

        def copy(k, block, to, src=None):
            return _remote(rows(*block) if src is None else src, rows(*block), send_sems.at[k], recv_sems.at[k], to)

        mine = pltpu.make_async_copy(x_ref, rows(*me), local_sem)
        mine.start()
        first = [copy(0, me, sibling, src=x_ref)]
        first += [copy(1 + j, me, (*chip, c), src=x_ref) for j, chip in enumerate(chips)]
        for cp in first:
            cp.start()
        passed = [copy(4 + j, (*chip, c), sibling) for j, chip in enumerate(chips)]
        for j, chip in enumerate(chips):
            copy(1 + j, (*chip, c), me).wait_recv()
            passed[j].start()
        copy(0, sibling, me).wait_recv()
        for j, chip in enumerate(chips):
            copy(4 + j, (*chip, 1 - c), me).wait_recv()
        for cp in first + passed:
            cp.wait_send()
        mine.wait()
        if with_sum:
            total = out_ref[0:R, :]
            for p in range(1, 8):
                total = total + out_ref[p * R:(p + 1) * R, :]
            sum_ref[...] = total
        if rider.finish is not None:
            rider.finish(r_in, r_out, r_sems)

    vm = pl.BlockSpec(memory_space=pltpu.VMEM)
    out_shape = [_sds((8 * R, N), F32)] + ([_sds((R, N), F32)] if with_sum else [])
    res = pl.pallas_call(
        body, name=name, out_shape=out_shape + list(rider.out_shape), in_specs=[vm] + [ANY] * n_ri,
        out_specs=[vm] * n_own + [ANY] * n_ro,
        scratch_shapes=[pltpu.SemaphoreType.DMA((7,)), pltpu.SemaphoreType.DMA((7,)), pltpu.SemaphoreType.DMA]
        + list(rider.sems),
        input_output_aliases={1 + i: n_own + o for i, o in rider.aliases.items()},
        compiler_params=pltpu.CompilerParams(vmem_limit_bytes=VMEM_LIMIT_BYTES))(v, *rider.operands)
    if rider.start is not None:
        return (*res[:n_own], res[n_own:])
    return res if with_sum else res[0]


def _full_place(ref, kind, C, kk, half, layer=None):
    lead = slice(None) if layer is None else pl.ds(layer, 1)
    if kind == "row":
        return ref.at[lead, kk, half]
    return ref.at[lead, half, :, pl.ds(pl.multiple_of(kk * C, LANES), C)]


def _gather_rider(fulls, kinds, shard_cols, layers=None, peers=(0, 1, 2)):
    n = len(fulls)
    layers = layers or [None] * n

    def copies(outs, sems):
        x, y, c, chips = _mesh_place()
        k = 2 * x + y
        place = lambda a, kk, half: _full_place(outs[a], kinds[a], shard_cols[a], kk, half, layers[a])
        copy = lambda a, j, ref, to: _remote(ref, ref, sems[0].at[6 * a + j], sems[1].at[6 * a + j], to)
        return (x, y, c), [(j, chip) for j, chip in enumerate(chips) if j in peers], k, place, copy

    def start(_, outs, sems):
        (x, y, c), chips, k, place, copy = copies(outs, sems)
        for j, chip in chips:
            for a in range(n):
                copy(a, j, place(a, k, c), (*chip, c)).start()

    def finish(_, outs, sems):
        (x, y, c), chips, k, place, copy = copies(outs, sems)
        me, sibling = (x, y, c), (x, y, 1 - c)
        for j, chip in chips:
            kj = 2 * chip[0] + chip[1]
            for a in range(n):
                copy(a, j, place(a, kj, c), me).wait_recv()
                copy(a, 3 + j, place(a, kj, c), sibling).start()
        for j, chip in chips:
            kj = 2 * chip[0] + chip[1]
            for a in range(n):
                copy(a, 3 + j, place(a, kj, 1 - c), me).wait_recv()
        for j, chip in chips:
            kj = 2 * chip[0] + chip[1]
            for a in range(n):
                copy(a, j, place(a, k, c), (*chip, c)).wait_send()
                copy(a, 3 + j, place(a, kj, c), sibling).wait_send()

    return Rider(tuple(fulls), tuple(_sds(f.shape, BF16) for f in fulls), {a: a for a in range(n)},
                 (pltpu.SemaphoreType.DMA((6 * n,)), pltpu.SemaphoreType.DMA((6 * n,))), start, finish)


def _scatter_rider(ps, kinds, shard_cols, peers=(0, 1, 2), into=None):
    n = len(ps)

    def copies(ins, outs, sems):
        x, y, c, chips = _mesh_place()
        cps = []
        for j, chip in enumerate(chips):
            if j not in peers:
                continue
            kj = 2 * chip[0] + chip[1]
            for a in range(n):
                C = shard_cols[a]
                src = ins[a].at[:, kj] if kinds[a] == "row" else ins[a].at[:, 0, :, pl.ds(pl.multiple_of(kj * C, LANES), C)]
                cps.append(_remote(src, outs[a].at[j], sems[0].at[3 * a + j], sems[1].at[3 * a + j], (*chip, c)))
        return cps

    def start(ins, outs, sems):
        for cp in copies(ins, outs, sems):
            cp.start()

    def finish(ins, outs, sems):
        cps = copies(ins, outs, sems)
        for cp in cps:
            cp.wait_recv()
        for cp in cps:
            cp.wait_send()

    out_shape = tuple(_sds((3, p.shape[0], p.shape[2], C), BF16) for p, C in zip(ps, shard_cols))
    aliases = {n + a: a for a in range(n)} if into is not None else {}
    return Rider(tuple(ps) + tuple(into or ()), out_shape, aliases,
                 (pltpu.SemaphoreType.DMA((3 * n,)), pltpu.SemaphoreType.DMA((3 * n,))), start, finish)


def _run_rider(name, rider):
    n_in, n_out = len(rider.operands), len(rider.out_shape)

    def body(*refs):
        ins, outs, sems = refs[:n_in], refs[n_in:n_in + n_out], refs[n_in + n_out:]
        rider.start(ins, outs, sems)
        rider.finish(ins, outs, sems)

    return pl.pallas_call(
        body, name=name, out_shape=list(rider.out_shape), in_specs=[ANY] * n_in, out_specs=[ANY] * n_out,
        input_output_aliases=dict(rider.aliases), scratch_shapes=list(rider.sems),
        compiler_params=pltpu.CompilerParams(vmem_limit_bytes=VMEM_LIMIT_BYTES))(*rider.operands)


def _exchange_rider(g5s):
    n = len(g5s)

    def copies(ins, outs, sems):
        x, y, c, _ = _mesh_place()
        return [_remote(ins[a].at[:, :, 1 - c], outs[a], sems[0].at[a], sems[1].at[a], (x, y, 1 - c)) for a in range(n)]

    def start(ins, outs, sems):
        for cp in copies(ins, outs, sems):
            cp.start()

    def finish(ins, outs, sems):
        cps = copies(ins, outs, sems)
        for cp in cps:
            cp.wait_recv()
        for cp in cps:
            cp.wait_send()

    out_shape = tuple(_sds((g.shape[0], g.shape[1], g.shape[3], g.shape[4]), BF16) for g in g5s)
    return Rider(tuple(g5s), out_shape, {}, (pltpu.SemaphoreType.DMA((n,)), pltpu.SemaphoreType.DMA((n,))), start, finish)


def _share_rider(fs):
    n = len(fs)

    def start(_, outs, sems):
        x, y, c, _p = _mesh_place()
        for a in range(n):
            mine = outs[a].at[:, c]
            _remote(mine, mine, sems[0].at[a], sems[1].at[a], (x, y, 1 - c)).start()

    def finish(_, outs, sems):
        x, y, c, _p = _mesh_place()
        for a in range(n):
            theirs = outs[a].at[:, 1 - c]
            _remote(theirs, theirs, sems[0].at[a], sems[1].at[a], (x, y, c)).wait_recv()
        for a in range(n):
            mine = outs[a].at[:, c]
            _remote(mine, mine, sems[0].at[a], sems[1].at[a], (x, y, 1 - c)).wait_send()

    return Rider(tuple(fs), tuple(_sds(f.shape, F32) for f in fs), {a: a for a in range(n)},
                 (pltpu.SemaphoreType.DMA((n,)), pltpu.SemaphoreType.DMA((n,))), start, finish)


def _both_riders(r1, r2):
    ni, no, ns = len(r1.operands), len(r1.out_shape), len(r1.sems)
    aliases = dict(r1.aliases)
    aliases.update({ni + i: no + o for i, o in r2.aliases.items()})

    def start(ins, outs, sems):
        r1.start(ins[:ni], outs[:no], sems[:ns])
        r2.start(ins[ni:], outs[no:], sems[ns:])

    def finish(ins, outs, sems):
        r1.finish(ins[:ni], outs[:no], sems[:ns])
        r2.finish(ins[ni:], outs[no:], sems[ns:])

    return Rider(r1.operands + r2.operands, r1.out_shape + r2.out_shape, aliases, r1.sems + r2.sems, start, finish)


def _pack_rows(parts, lane_mult=1024):
    flat = jnp.concatenate([p.reshape(-1).astype(F32) for p in parts])
    n = -(-flat.shape[0] // (8 * lane_mult)) * lane_mult
    return jnp.pad(flat, (0, 8 * n - flat.shape[0])).reshape(8, n)


def _relu2(acc):
    r = jnp.maximum(acc, 0.0)
    return r, r * r


def _times_2r(acc, r):
    return (acc * (2.0 * r.astype(F32)),)


def kernel(x, c, positions, w_mod, b_mod, norm_g, mla_w_in, mla_g_q, mla_g_kv, mla_w_uq, mla_w_ukv, mla_w_o, conv_w_in, conv_w, conv_w_out, mlp_w_up, mlp_w_down, loss_target, m_w_mod, m_b_mod, m_norm_g, m_mla_w_in, m_mla_g_q, m_mla_g_kv, m_mla_w_uq, m_mla_w_ukv, m_mla_w_o, m_conv_w_in, m_conv_w, m_conv_w_out, m_mlp_w_up, m_mlp_w_down, v_w_mod, v_b_mod, v_norm_g, v_mla_w_in, v_mla_g_q, v_mla_g_kv, v_mla_w_uq, v_mla_w_ukv, v_mla_w_o, v_conv_w_in, v_conv_w, v_conv_w_out, v_mlp_w_up, v_mlp_w_down):
    S, D = x.shape[1], x.shape[2]
    Dq = D // N_CHIPS
    ncol = w_mod.shape[2]
    n_mod = N_CHIPS * ncol // D
    F = mlp_w_up.shape[2] * N_CHIPS
    lat_dim = mla_w_in.shape[2]
    rank = mla_g_q.shape[1]
    H = mla_w_uq.shape[2]
    d_qk = mla_w_uq.shape[3]
    assert mla_g_kv.shape[1] == rank and lat_dim == 2 * rank + QK_ROPE and d_qk == QK_NOPE + QK_ROPE
    assert mla_w_ukv.shape[3] == QK_NOPE + V_HEAD and x.shape[0] == 1 and n_mod == 6
    assert norm_g.shape[0] == 2 and mla_w_in.shape[0] == 1 and conv_w_in.shape[0] == 1
    lat_pad = 2 * rank + LANES
    scale = float(d_qk) ** -0.5

    xi, yi, ci = lax.axis_index("x"), lax.axis_index("y"), lax.axis_index("c")
    chip = 2 * xi + yi
    dev = 2 * chip + ci
    c_idx = jnp.reshape(ci, (1,)).astype(jnp.int32)
    k_idx = jnp.reshape(chip, (1,)).astype(jnp.int32)

    n1 = D + 2 * D + 3 * Dq
    g1 = _small_allgather("gather_small_inputs", _pack_rows([c, norm_g, conv_w])).reshape(8, -1)
    c_all = g1[:, :D]
    by_chip = g1[0::2]
    norm_full = jnp.concatenate([by_chip[kk, D:3 * D].reshape(2, 4, Dq) for kk in range(N_CHIPS)], axis=-1)
    convw_full = jnp.concatenate([by_chip[kk, 3 * D:n1].reshape(3, Dq) for kk in range(N_CHIPS)], axis=-1)

    b_cols = lax.dynamic_slice(b_mod, (0, chip * ncol), (2, ncol)).reshape(2, 1, ncol)
    cond_all = _silu(c_all)
    mod_cols = _mod_fwd(cond_all, w_mod, b_cols)
    g2 = _small_allgather("gather_mod", _pack_rows([mod_cols]))
    g2 = g2.reshape(8, -1)[0::2, :2 * 8 * ncol].reshape(N_CHIPS, 2, 8, ncol)
    mod_all = jnp.transpose(g2, (2, 1, 0, 3)).reshape(8, 2, n_mod * D)
    mod_me = lax.dynamic_index_in_dim(mod_all, dev, axis=0, keepdims=False)
    mods = [[mod_me[l, i * D:(i + 1) * D].reshape(1, D) for i in range(n_mod)] for l in range(2)]
    ng = [[norm_full[l, i].reshape(1, D) for i in range(4)] for l in range(2)]

    pos = positions[0].astype(F32)
    inv_freq = ROPE_THETA ** (-jnp.arange(0, QK_ROPE, 2, dtype=F32) / QK_ROPE)
    ang = pos[:, None] * inv_freq
    cos, sin = jnp.cos(ang), jnp.sin(ang)
    zero = jnp.zeros_like(cos)
    rope_tabs = (jnp.concatenate([cos, cos, zero, zero], axis=1),
                 jnp.concatenate([-sin, zero, zero, zero], axis=1),
                 jnp.concatenate([zero, sin, zero, zero], axis=1))

    weights = [("mla_w_in", mla_w_in, "row"), ("mla_w_uq", mla_w_uq.reshape(1, rank // N_CHIPS, H * d_qk), "row"),
               ("mla_w_ukv", mla_w_ukv.reshape(1, rank // N_CHIPS, H * QK_PAD), "row"), ("mla_w_o", mla_w_o, "row"),
               ("conv_w_in", conv_w_in, "col"), ("conv_w_out", conv_w_out, "row"),
               ("mlp_w_up", mlp_w_up, "col"), ("mlp_w_down", mlp_w_down, "row")]
    kinds = [k for _, _, k in weights]
    shard_shapes = [w.shape for _, w, _ in weights]
    shard_cols = [s[2] for s in shard_shapes]
    casted = [_cast_into_full("cast_" + nm, w, kind, k_idx) for nm, w, kind in weights]
    W_IN, W_UQ, W_UKV, W_O, W_CIN, W_COUT, W_UP, W_DOWN = range(8)
    mla_idx = [W_IN, W_UQ, W_UKV, W_O]

    def view(i, buf):
        L, R, C = shard_shapes[i]
        return buf.reshape((L, N_CHIPS * R, C) if kinds[i] == "row" else (L, R, N_CHIPS * C))

    NEIGHBOURS, DIAGONAL = (0, 1), (2,)

    def gather_of(bufs, idx, layers=None, peers=(0, 1, 2)):
        return _gather_rider(bufs, [kinds[i] for i in idx], [shard_cols[i] for i in idx], layers, peers)

    def scatter_of(ps, idx, peers=(0, 1, 2), into=None):
        return _scatter_rider(ps, [kinds[i] for i in idx], [shard_cols[i] for i in idx], peers, into)

    def halves(items):
        g5s = []
        for _, i, g in items:
            _, R, C = shard_shapes[i]
            g5s.append(g.reshape((1, N_CHIPS, 2, R // 2, C) if kinds[i] == "row" else (1, 1, 2, R // 2, N_CHIPS * C)))
        return g5s

    def pair_sums(items, g5s, ras):
        return [_pair_sum("pair_sum_" + nm, g5, ra, c_idx) for (nm, _, _), g5, ra in zip(items, g5s, ras)]

    got = _run_rider("gather_weights_mla", gather_of([casted[i] for i in mla_idx], mla_idx))
    w_in_p = jnp.pad(view(W_IN, got[0])[0], ((0, 0), (0, lat_pad - lat_dim)))
    w_q_p = jnp.pad(view(W_UQ, got[1])[0].reshape(rank, H, d_qk), ((0, 0), (0, 0), (0, QK_PAD - d_qk))).reshape(rank, H * QK_PAD)
    w_ukv, w_o = view(W_UKV, got[2])[0], view(W_O, got[3])[0]
    HV = H * V_HEAD

    def layer_b(l, transposed):
        if transposed:
            return lambda tm, tn, tk: pl.BlockSpec((None, tn, tk), lambda i, j, k: (l, j, k))
        return lambda tm, tn, tk: pl.BlockSpec((None, tk, tn), lambda i, j, k: (l, k, j))

    def mlp_up(tag, l, h, w, rider=NO_RIDER):
        return _mm("mlp_up_" + tag, h, w, "nn", S, F, D, [_sds((S, F), BF16)] * 2, epilogue=_relu2,
                   b_spec=layer_b(l, False), rider=rider)

    def mlp_down(tag, l, a2, w, rider=NO_RIDER):
        return _mm("mlp_down_" + tag, a2, w, "nn", S, D, F, [_sds((S, D), F32)], b_spec=layer_b(l, False), rider=rider)

    def mlp_bwd(tag, l, h, r, a2, dy, rider_of=None):
        carried = ()
        first = rider_of(carried) if rider_of else NO_RIDER
        res = _mm("mlp_down_dx_" + tag, dy, w_down, "nt", S, F, D, [_sds((S, F), BF16)], epilogue=_times_2r,
                  b_spec=layer_b(l, True), rider=first,
                  extras=[(r, lambda tm, tn, tk: pl.BlockSpec((tm, tn), lambda i, j, k: (i, j)))])
        (da,), carried = res if rider_of else (res, ())
        second = rider_of(carried) if rider_of else NO_RIDER
        res = _mm("mlp_down_dw_" + tag, a2, dy, "tn", F, D, S, [_sds((F, D), BF16)], rider=second)
        (dw_down,), carried = res if rider_of else (res, ())
        (dh,) = _mm("mlp_up_dx_" + tag, da, w_up, "nt", S, D, F, [_sds((S, D), F32)], b_spec=layer_b(l, True))
        (dw_up,) = _mm("mlp_up_dw_" + tag, h, da, "tn", D, F, S, [_sds((D, F), BF16)])
        return dh, dw_up, dw_down, carried

    x0 = x[0]
    sh1, sc1, gt1, sh2, sc2, gt2 = mods[0]
    (h1,) = _fwd_boundary("fwd_boundary_0", x0, None, None, None, ng[0][0], sc1, sh1)
    (lat,) = _mm("mla_in", h1, w_in_p, "nn", S, lat_pad, D, [_sds((S, lat_pad), F32)], tn=lat_pad)
    cq, ckv, kr = _latent_fwd(lat, mla_g_q, mla_g_kv, rope_tabs, rank)

    def rope_q(acc, cos_p, sin_lo, sin_hi):
        parts = []
        for hh in range(acc.shape[1] // QK_PAD):
            parts.append(acc[:, hh * QK_PAD:hh * QK_PAD + QK_NOPE])
            parts.append(_rope(acc[:, hh * QK_PAD + QK_NOPE:(hh + 1) * QK_PAD], cos_p, sin_lo, sin_hi))
        return (jnp.concatenate(parts, axis=1),)

    tab_extra = lambda tm, tn, tk: pl.BlockSpec((tm, LANES), lambda i, j, k: (i, 0))
    (q,) = _mm("mla_q", cq, w_q_p, "nn", S, H * QK_PAD, rank, [_sds((S, H * QK_PAD), BF16)], epilogue=rope_q,
               extras=[(t, tab_extra) for t in rope_tabs], tn=2 * QK_PAD)
    (kv,) = _mm("mla_kv", ckv, w_ukv, "nn", S, H * QK_PAD, rank, [_sds((S, H * QK_PAD), BF16)])
    rest_idx = [W_COUT, W_UP, W_DOWN]
    o, lse, (cout_buf, up_buf, down_buf) = _attn_fwd_tri(
        q, kv, kr, H, scale, gather_of([casted[i] for i in rest_idx], rest_idx, [None, 0, 0]))
    (y1,) = _mm("mla_out", o, w_o, "nn", S, D, HV, [_sds((S, D), F32)])
    x1, h2 = _fwd_boundary("fwd_boundary_1", x0, y1, gt1, ng[0][1], ng[0][2], sc2, sh2)
    (r2, a2), (cin_buf,) = mlp_up("0", 0, h2, view(W_UP, up_buf), gather_of([casted[W_CIN]], [W_CIN]))
    (y2,), (up_buf,) = mlp_down("0", 0, a2, view(W_DOWN, down_buf), gather_of([up_buf], [W_UP], [1]))
    w_cin, w_cout, w_up = view(W_CIN, cin_buf)[0], view(W_COUT, cout_buf)[0], view(W_UP, up_buf)

    sh1b, sc1b, gt1b, sh2b, sc2b, gt2b = mods[1]
    x2, h3 = _fwd_boundary("fwd_boundary_2", x1, y2, gt2, ng[0][3], ng[1][0], sc1b, sh1b)
    nD = lambda tn: D // tn
    (proj3,), (down_buf,) = _mm(
        "conv_in", h3, w_cin, "nn", S, 3 * D, D, [_sds((3, S, D), F32)], tn=min(1024, D),
        rider=gather_of([down_buf], [W_DOWN], [1], NEIGHBOURS),
        out_specs=[lambda tm, tn, tk: pl.BlockSpec((None, tm, tn), lambda i, j, k: (j // nD(tn), i, j % nD(tn)))])
    bz = _conv_fwd(proj3, convw_full)
    (y3,) = _mm("conv_out", bz, w_cout, "nn", S, D, D, [_sds((S, D), F32)])
    x3, h4 = _fwd_boundary("fwd_boundary_3", x2, y3, gt1b, ng[1][1], ng[1][2], sc2b, sh2b)
    (r4, a4), (down_buf,) = mlp_up("1", 1, h4, w_up, gather_of([down_buf], [W_DOWN], [1], DIAGONAL))
    w_down = view(W_DOWN, down_buf)
    (y4,) = mlp_down("1", 1, a4, w_down)

    dx4, dy4, sums_l, loss_acc = _loss_boundary("loss_boundary", x3, y4, gt2b, ng[1][3], loss_target[0])
    loss = lax.psum(loss_acc[0, 0], ("x", "y", "c"))

    dh4, dw_up1, dw_down1, _ = mlp_bwd("1", 1, h4, r4, a4, dy4)
    items = [("mlp_w_up_1", W_UP, dw_up1), ("mlp_w_down_1", W_DOWN, dw_down1)]
    g5s = halves(items)
    dx3, dy3, sums_3, ras = _bwd_boundary("bwd_boundary_3", dx4, dh4, x3, y3, gt1b, ng[1][1], ng[1][2], sc2b,
                                          _exchange_rider(g5s))
    ps_up1, ps_down1 = pair_sums(items, g5s, ras)

    (dbz,) = _mm("conv_out_dx", dy3, w_cout, "nt", S, D, D, [_sds((S, D), F32)])
    (dw_cout,) = _mm("conv_out_dw", bz, dy3, "tn", D, D, S, [_sds((D, D), BF16)])
    dproj3, dconvw = _conv_bwd(dbz, proj3, convw_full)
    (dh3,), (rb_up1,) = _mm(
        "conv_in_dx", dproj3, w_cin, "nt", S, D, 3 * D, [_sds((S, D), F32)], tk=D,
        rider=scatter_of([ps_up1], [W_UP], NEIGHBOURS),
        a_spec=lambda tm, tn, tk: pl.BlockSpec((None, tm, tk), lambda i, j, k: (k // (D // tk), i, k % (D // tk))))
    (dw_cin,), (rb_up1,) = _mm(
        "conv_in_dw", h3, dproj3, "tn", D, 3 * D, S, [_sds((D, 3 * D), BF16)], tn=min(1024, D),
        rider=scatter_of([ps_up1], [W_UP], DIAGONAL, [rb_up1]),
        b_spec=lambda tm, tn, tk: pl.BlockSpec((None, tk, tn), lambda i, j, k: (j // nD(tn), k, j % nD(tn))))
    items = [("conv_w_in", W_CIN, dw_cin), ("conv_w_out", W_COUT, dw_cout)]
    g5s = halves(items)
    dx2, dy2, sums_2, ras = _bwd_boundary("bwd_boundary_2", dx3, dh3, x2, y2, gt2, ng[0][3], ng[1][0], sc1b,
                                          _exchange_rider(g5s))
    ps_cin, ps_cout = pair_sums(items, g5s, ras)

    dh2, dw_up0, dw_down0, (rb_down1,) = mlp_bwd(
        "0", 0, h2, r2, a2, dy2,
        lambda got: scatter_of([ps_down1], [W_DOWN], DIAGONAL, list(got)) if got else scatter_of([ps_down1], [W_DOWN], NEIGHBOURS))
    items = [("mlp_w_up_0", W_UP, dw_up0), ("mlp_w_down_0", W_DOWN, dw_down0)]
    g5s = halves(items)
    dx1, dy1, sums_1, ras = _bwd_boundary("bwd_boundary_1", dx2, dh2, x1, y1, gt1, ng[0][1], ng[0][2], sc2,
                                          _exchange_rider(g5s))
    ps_up0, ps_down0 = pair_sums(items, g5s, ras)

    (do,) = _mm("mla_out_dx", dy1, w_o, "nt", S, HV, D, [_sds((S, HV), BF16)])
    (dw_o,) = _mm("mla_out_dw", o, dy1, "tn", HV, D, S, [_sds((HV, D), BF16)])
    dq, dkv, dkr, (rb_up0, rb_down0, rb_cin, rb_cout) = _attn_bwd_tri(
        q, kv, kr, o, do, lse, rope_tabs, H, scale,
        scatter_of([ps_up0, ps_down0, ps_cin, ps_cout], [W_UP, W_DOWN, W_CIN, W_COUT]))
    (dcq,) = _mm("mla_q_dx", dq, w_q_p, "nt", S, rank, H * QK_PAD, [_sds((S, rank), F32)])
    (dw_q_p,) = _mm("mla_q_dw", cq, dq, "tn", rank, H * QK_PAD, S, [_sds((rank, H * QK_PAD), BF16)])
    (dckv,) = _mm("mla_kv_dx", dkv, w_ukv, "nt", S, rank, H * QK_PAD, [_sds((S, rank), F32)])
    (dw_ukv,) = _mm("mla_kv_dw", ckv, dkv, "tn", rank, H * QK_PAD, S, [_sds((rank, H * QK_PAD), BF16)])
    dlat, sums_lat = _latent_bwd(lat, dcq, dckv, dkr, mla_g_q, mla_g_kv, rope_tabs, rank)
    (dw_in_p,) = _mm("mla_in_dw", h1, dlat, "tn", D, lat_pad, S, [_sds((D, lat_pad), BF16)], tn=lat_pad)
    dw_mla = [dw_in_p[:, :lat_dim], dw_q_p.reshape(rank, H, QK_PAD)[:, :, :d_qk].reshape(rank, H * d_qk), dw_ukv, dw_o]
    items = [(weights[i][0], i, g) for i, g in zip(mla_idx, dw_mla)]
    g5s = halves(items)
    (dh1,), ras = _mm("mla_in_dx", dlat, w_in_p, "nt", S, D, lat_pad, [_sds((S, D), F32)], rider=_exchange_rider(g5s))
    ps_mla = pair_sums(items, g5s, ras)
    grad_x, sums_0, _ = _bwd_boundary("bwd_boundary_0", dx1, dh1, x0, None, None, None, ng[0][0], sc1)

    kc_idx = jnp.stack([chip, ci]).astype(jnp.int32)
    fs_rest = [_chip_sum("chip_sum_" + weights[i][0], p, rb, kc_idx, kinds[i])
               for i, p, rb in [(W_CIN, ps_cin, rb_cin), (W_COUT, ps_cout, rb_cout)]]
    for i, (p1, r1), (p0, r0) in [(W_UP, (ps_up1, rb_up1), (ps_up0, rb_up0)), (W_DOWN, (ps_down1, rb_down1), (ps_down0, rb_down0))]:
        f = _chip_sum("chip_sum_" + weights[i][0] + "_1", p1, r1, kc_idx, kinds[i], layer=1, n_layers=2)
        fs_rest.append(_chip_sum("chip_sum_" + weights[i][0] + "_0", p0, r0, kc_idx, kinds[i], layer=0, n_layers=2, prev=f))

    dmod0 = [sums_0[0], sums_0[1], sums_1[3], sums_1[0], sums_1[1], sums_2[3]]
    dmod1 = [sums_2[0], sums_2[1], sums_3[3], sums_3[0], sums_3[1], sums_l[3]]
    dng0 = [sums_0[2], sums_1[4], sums_1[2], sums_2[4]]
    dng1 = [sums_2[2], sums_3[4], sums_3[2], sums_l[4]]
    small = _pack_rows(dmod0 + dmod1 + dng0 + dng1 + [sums_lat[0], sums_lat[1], dconvw], lane_mult=LANES)
    gathered, total, carried = _small_allgather(
        "gather_small_grads", small, with_sum=True, rider=_both_riders(scatter_of(ps_mla, mla_idx), _share_rider(fs_rest)))
    rbs_mla, finals_rest = carried[:len(mla_idx)], carried[len(mla_idx):]
    n_dm = 2 * n_mod * D
    dmod_all = gathered.reshape(8, -1)[:, :n_dm].reshape(8, 2, n_mod * D)
    total = total.reshape(-1)
    g_b_mod = total[:n_dm].reshape(2, n_mod * D)
    g_norm = lax.dynamic_slice(total[n_dm:n_dm + 8 * D].reshape(2, 4, D), (0, 0, chip * Dq), (2, 4, Dq))
    off = n_dm + 8 * D
    g_gq = total[off:off + rank].reshape(1, rank)
    g_gkv = total[off + rank:off + 2 * rank].reshape(1, rank)
    off += 2 * rank
    g_convw = lax.dynamic_slice(total[off:off + 3 * D].reshape(1, 3, D), (0, 0, chip * Dq), (1, 3, Dq))

    dmod_cols = jnp.transpose(lax.dynamic_slice(dmod_all.reshape(8, 2, N_CHIPS, ncol), (0, 0, chip, 0), (8, 2, 1, ncol))
                              .reshape(8, 2, ncol), (1, 0, 2))
    g_w_mod, d_w_mod, nm_w_mod, nv_w_mod, _ = _adamw_mod(w_mod, cond_all.T, dmod_cols, m_w_mod, v_w_mod)
    fs_mla = [_chip_sum("chip_sum_" + weights[i][0], p, rb, kc_idx, kinds[i]) for i, p, rb in zip(mla_idx, ps_mla, rbs_mla)]
    finals = list(_run_rider("grad_pair_share_mla", _share_rider(fs_mla))) + list(finals_rest)
    orig = [mla_w_in, mla_w_uq, mla_w_ukv, mla_w_o, conv_w_in, conv_w_out, mlp_w_up, mlp_w_down]
    big_grads = [f.reshape(w.shape) for f, w in zip(finals, orig)]

    names = ["b_mod", "norm_g", "mla_w_in", "mla_g_q", "mla_g_kv", "mla_w_uq", "mla_w_ukv", "mla_w_o",
             "conv_w_in", "conv_w", "conv_w_out", "mlp_w_up", "mlp_w_down"]
    ws = [b_mod, norm_g, mla_w_in, mla_g_q, mla_g_kv, mla_w_uq, mla_w_ukv, mla_w_o, conv_w_in, conv_w, conv_w_out,
          mlp_w_up, mlp_w_down]
    ms = [m_b_mod, m_norm_g, m_mla_w_in, m_mla_g_q, m_mla_g_kv, m_mla_w_uq, m_mla_w_ukv, m_mla_w_o, m_conv_w_in,
          m_conv_w, m_conv_w_out, m_mlp_w_up, m_mlp_w_down]
    vs = [v_b_mod, v_norm_g, v_mla_w_in, v_mla_g_q, v_mla_g_kv, v_mla_w_uq, v_mla_w_ukv, v_mla_w_o, v_conv_w_in,
          v_conv_w, v_conv_w_out, v_mlp_w_up, v_mlp_w_down]
    gs = [g_b_mod, g_norm, big_grads[0], g_gq, g_gkv, big_grads[1], big_grads[2], big_grads[3], big_grads[4],
          g_convw, big_grads[5], big_grads[6], big_grads[7]]
    grads, deltas, new_ms, new_vs = [g_w_mod], [d_w_mod], [nm_w_mod], [nv_w_mod]
    for nm, w, g, m, v in zip(names, ws, gs, ms, vs):
        d, nm_, nv_ = _adamw("adamw_" + nm, w, g, m, v)
        grads.append(g)
        deltas.append(d)
        new_ms.append(nm_)
        new_vs.append(nv_)
    return (loss, grad_x[None], *grads, *deltas, *new_ms, *new_vs)
```

```python
from typing import NamedTuple

import jax
import jax.numpy as jnp
from jax import lax
from jax.experimental import pallas as pl
from jax.experimental.pallas import tpu as pltpu

F32 = jnp.float32
BF16 = jnp.bfloat16
NORM_EPS = 1e-6
ROPE_THETA = 10000.0
QK_NOPE = 128
QK_ROPE = 64
V_HEAD = 128
LANES = 128
QK_PAD = QK_NOPE + LANES
ADAM_LR, ADAM_B1, ADAM_B2, ADAM_EPS, ADAM_WD, ADAM_STEP = 0.001, 0.9, 0.999, 1e-08, 0.01, 10
VMEM_LIMIT_BYTES = 56 * 1024 * 1024
N_CHIPS = 4
MESH_ID = pl.DeviceIdType.MESH
ANY = pl.BlockSpec(memory_space=pl.ANY)
NEG_INF = float("-inf")

DIMS_NN = (((1,), (0,)), ((), ()))
DIMS_NT = (((1,), (1,)), ((), ()))
DIMS_TN = (((0,), (0,)), ((), ()))


def _cparams(*sem):
    return pltpu.CompilerParams(dimension_semantics=sem, vmem_limit_bytes=VMEM_LIMIT_BYTES)


def _row_tile(rows, row_bytes, limit=2 * 1024 * 1024, mult=16):
    if rows * row_bytes <= limit or rows % mult:
        return rows
    best = mult
    t = mult
    while t <= rows:
        if rows % t == 0 and t * row_bytes <= limit:
            best = t
        t += mult
    return best


def _rms(v):
    return lax.rsqrt(jnp.mean(v * v, axis=-1, keepdims=True) + NORM_EPS)


class Rider(NamedTuple):
    operands: tuple
    out_shape: tuple
    aliases: dict
    sems: tuple
    start: object
    finish: object


NO_RIDER = Rider((), (), {}, (), None, None)


def _mm(name, a, b, mode, M, N, K, outs, *, a_spec=None, b_spec=None, out_specs=None, epilogue=None,
        extras=(), rider=NO_RIDER, tm=1024, tn=1024, tk=4096):
    tm, tn, tk = min(tm, M), min(tn, N), min(tk, K)
    assert M % tm == 0 and N % tn == 0 and K % tk == 0, (name, M, N, K)
    nk = K // tk
    if a_spec is None:
        a_spec = {"nn": pl.BlockSpec((tm, tk), lambda i, j, k: (i, k)),
                  "nt": pl.BlockSpec((tm, tk), lambda i, j, k: (i, k)),
                  "tn": pl.BlockSpec((tk, tm), lambda i, j, k: (k, i))}[mode]
    else:
        a_spec = a_spec(tm, tn, tk)
    if b_spec is None:
        b_spec = {"nn": pl.BlockSpec((tk, tn), lambda i, j, k: (k, j)),
                  "nt": pl.BlockSpec((tn, tk), lambda i, j, k: (j, k)),
                  "tn": pl.BlockSpec((tk, tn), lambda i, j, k: (k, j))}[mode]
    else:
        b_spec = b_spec(tm, tn, tk)
    if out_specs is None:
        out_specs = [pl.BlockSpec((tm, tn), lambda i, j, k: (i, j)) for _ in outs]
    else:
        out_specs = [s(tm, tn, tk) for s in out_specs]
    dims = {"nn": DIMS_NN, "nt": DIMS_NT, "tn": DIMS_TN}[mode]
    ne, no = len(extras), len(outs)
    n_ri, n_ro = len(rider.operands), len(rider.out_shape)
    grid = (M // tm, N // tn, nk)

    def body(*refs):
        a_ref, b_ref = refs[0], refs[1]
        ex = refs[2:2 + ne]
        r_in = refs[2 + ne:2 + ne + n_ri]
        o = refs[2 + ne + n_ri:2 + ne + n_ri + no]
        r_out = refs[2 + ne + n_ri + no:2 + ne + n_ri + no + n_ro]
        scratch = refs[2 + ne + n_ri + no + n_ro:]
        r_sems = scratch[1:] if nk > 1 else scratch
        ii, jj, kk = pl.program_id(0), pl.program_id(1), pl.program_id(2)

        if rider.start is not None:
            @pl.when((ii == 0) & (jj == 0) & (kk == 0))
            def _():
                rider.start(r_in, r_out, r_sems)

        part = lax.dot_general(a_ref[...].astype(BF16), b_ref[...].astype(BF16), dims,
                               preferred_element_type=F32)

        def finish(total):
            vals = epilogue(total, *[e[...] for e in ex]) if epilogue is not None else (total,)
            for r, v in zip(o, vals):
                r[...] = v.astype(r.dtype)

        if nk == 1:
            finish(part)
        else:
            acc = scratch[0]

            @pl.when(kk == 0)
            def _():
                acc[...] = part

            @pl.when(kk > 0)
            def _():
                acc[...] += part

            @pl.when(kk == nk - 1)
            def _():
                finish(acc[...])

        if rider.finish is not None:
            @pl.when((ii == grid[0] - 1) & (jj == grid[1] - 1) & (kk == nk - 1))
            def _():
                rider.finish(r_in, r_out, r_sems)

    operands = [a, b] + [e[0] for e in extras] + list(rider.operands)
    in_specs = [a_spec, b_spec] + [e[1](tm, tn, tk) for e in extras] + [ANY] * n_ri
    hosted = rider.start is not None
    res = pl.pallas_call(
        body, name=name, grid=grid,
        in_specs=in_specs, out_specs=out_specs + [ANY] * n_ro, out_shape=list(outs) + list(rider.out_shape),
        scratch_shapes=([pltpu.VMEM((tm, tn), F32)] if nk > 1 else []) + list(rider.sems),
        input_output_aliases={2 + ne + i: no + r for i, r in rider.aliases.items()},
        compiler_params=_cparams(*(("arbitrary",) * 3 if hosted else ("parallel", "parallel", "arbitrary"))),
    )(*operands)
    return (res[:no], res[no:]) if hosted else res


def _sds(shape, dtype):
    return jax.ShapeDtypeStruct(tuple(shape), dtype)


def _rope(t, cos_p, sin_lo, sin_hi):
    return t * cos_p + pltpu.roll(t, LANES - QK_ROPE // 2, 1) * sin_lo + pltpu.roll(t, QK_ROPE // 2, 1) * sin_hi


def _rope_t(d, cos_p, sin_lo, sin_hi):
    return d * cos_p + pltpu.roll(d * sin_lo, QK_ROPE // 2, 1) + pltpu.roll(d * sin_hi, LANES - QK_ROPE // 2, 1)


def _vec_spec(d):
    return pl.BlockSpec((1, d), lambda i: (0, 0))


def _fwd_boundary(name, x_prev, y, gate, ng_post, ng_pre, sc, sh):
    S, D = x_prev.shape
    ts = min(256, S)
    has_y = y is not None
    row = pl.BlockSpec((ts, D), lambda i: (i, 0))

    def body(*refs):
        if has_y:
            x_ref, y_ref, g_ref, ngp_ref, ngn_ref, sc_ref, sh_ref, xo_ref, h_ref = refs
            yv = y_ref[...].astype(F32)
            xn = x_ref[...] + g_ref[...] * (yv * _rms(yv) * ngp_ref[...])
            xo_ref[...] = xn
        else:
            x_ref, ngn_ref, sc_ref, sh_ref, h_ref = refs
            xn = x_ref[...]
        hn = xn * _rms(xn) * ngn_ref[...]
        h_ref[...] = (hn * (1.0 + sc_ref[...]) + sh_ref[...]).astype(BF16)

    vec = _vec_spec(D)
    if has_y:
        operands = (x_prev, y, gate, ng_post, ng_pre, sc, sh)
        in_specs = [row, row, vec, vec, vec, vec, vec]
        out_shape = [_sds((S, D), F32), _sds((S, D), BF16)]
        out_specs = [row, row]
    else:
        operands = (x_prev, ng_pre, sc, sh)
        in_specs = [row, vec, vec, vec]
        out_shape = [_sds((S, D), BF16)]
        out_specs = [row]
    return pl.pallas_call(body, name=name, grid=(S // ts,), in_specs=in_specs, out_specs=out_specs,
                          out_shape=out_shape, compiler_params=_cparams("parallel"))(*operands)


def _acc_rows(sums_ref, rows):
    for r, v in rows:
        sums_ref[r:r + 1, :] += jnp.sum(v, axis=0, keepdims=True)


def _post_norm_bwd(dxt, yv, gate, ng_post, sums_ref, dy_ref):
    r1 = _rms(yv)
    yhat = yv * r1
    dn = dxt * gate
    u = dn * ng_post
    dy = r1 * (u - yhat * jnp.mean(u * yhat, axis=-1, keepdims=True))
    dy_ref[...] = dy.astype(dy_ref.dtype)
    _acc_rows(sums_ref, [(3, dxt * (yhat * ng_post)), (4, dn * yhat)])


def _loss_boundary(name, x_prev, y, gate, ng_post, target):
    S, D = x_prev.shape
    ts = min(256, S)
    row = pl.BlockSpec((ts, D), lambda i: (i, 0))
    vec = _vec_spec(D)

    def body(x_ref, y_ref, g_ref, ngp_ref, t_ref, dx_ref, dy_ref, sums_ref, loss_ref):
        @pl.when(pl.program_id(0) == 0)
        def _():
            sums_ref[...] = jnp.zeros_like(sums_ref)
            loss_ref[...] = jnp.zeros_like(loss_ref)

        yv = y_ref[...].astype(F32)
        xf = x_ref[...] + g_ref[...] * (yv * _rms(yv) * ngp_ref[...])
        err = xf - t_ref[...]
        loss_ref[...] += 0.5 * jnp.sum(jnp.mean(err * err, axis=-1, keepdims=True))
        dxt = err / D
        dx_ref[...] = dxt
        _post_norm_bwd(dxt, yv, g_ref[...], ngp_ref[...], sums_ref, dy_ref)

    return pl.pallas_call(
        body, name=name, grid=(S // ts,),
        in_specs=[row, row, vec, vec, row],
        out_specs=[row, row, pl.BlockSpec((8, D), lambda i: (0, 0)), pl.BlockSpec((8, LANES), lambda i: (0, 0))],
        out_shape=[_sds((S, D), F32), _sds((S, D), BF16), _sds((8, D), F32), _sds((8, LANES), F32)],
        compiler_params=_cparams("arbitrary"))(x_prev, y, gate, ng_post, target)


def _bwd_boundary(name, dx_new, dh, x_new, y, gate, ng_post, ng_pre, sc, rider=NO_RIDER):
    S, D = x_new.shape
    ts = min(256, S)
    has_y = y is not None
    row = pl.BlockSpec((ts, D), lambda i: (i, 0))
    vec = _vec_spec(D)
    n_in, n_out = (8, 3) if has_y else (5, 2)
    n_ri, n_ro = len(rider.operands), len(rider.out_shape)

    def body(*refs):
        r_in = refs[n_in:n_in + n_ri]
        r_out = refs[n_in + n_ri + n_out:n_in + n_ri + n_out + n_ro]
        r_sems = refs[n_in + n_ri + n_out + n_ro:]
        own = refs[:n_in] + refs[n_in + n_ri:n_in + n_ri + n_out]
        if has_y:
            dxn_ref, dh_ref, x_ref, y_ref, g_ref, ngp_ref, ngn_ref, sc_ref, dxo_ref, dy_ref, sums_ref = own
        else:
            dxn_ref, dh_ref, x_ref, ngn_ref, sc_ref, dxo_ref, sums_ref = own

        @pl.when(pl.program_id(0) == 0)
        def _():
            sums_ref[...] = jnp.zeros_like(sums_ref)
            if rider.start is not None:
                rider.start(r_in, r_out, r_sems)

        xv = x_ref[...]
        dhv = dh_ref[...].astype(F32)
        ngn = ngn_ref[...]
        r2 = _rms(xv)
        xhat = xv * r2
        dn_pre = dhv * (1.0 + sc_ref[...])
        u2 = dn_pre * ngn
        dxt = dxn_ref[...] + r2 * (u2 - xhat * jnp.mean(u2 * xhat, axis=-1, keepdims=True))
        dxo_ref[...] = dxt
        _acc_rows(sums_ref, [(0, dhv), (1, dhv * (xhat * ngn)), (2, dn_pre * xhat)])
        if has_y:
            _post_norm_bwd(dxt, y_ref[...].astype(F32), g_ref[...], ngp_ref[...], sums_ref, dy_ref)

        if rider.finish is not None:
            @pl.when(pl.program_id(0) == S // ts - 1)
            def _():
                rider.finish(r_in, r_out, r_sems)

    sums_spec = pl.BlockSpec((8, D), lambda i: (0, 0))
    if has_y:
        operands = (dx_new, dh, x_new, y, gate, ng_post, ng_pre, sc)
        in_specs = [row, row, row, row, vec, vec, vec, vec]
        out_shape = [_sds((S, D), F32), _sds((S, D), BF16), _sds((8, D), F32)]
        out_specs = [row, row, sums_spec]
    else:
        operands = (dx_new, dh, x_new, ng_pre, sc)
        in_specs = [row, row, row, vec, vec]
        out_shape = [_sds((S, D), F32), _sds((8, D), F32)]
        out_specs = [row, sums_spec]
    res = pl.pallas_call(
        body, name=name, grid=(S // ts,), in_specs=in_specs + [ANY] * n_ri, out_specs=out_specs + [ANY] * n_ro,
        out_shape=out_shape + list(rider.out_shape), scratch_shapes=list(rider.sems),
        input_output_aliases={n_in + i: n_out + o for i, o in rider.aliases.items()},
        compiler_params=_cparams("arbitrary"))(*operands, *rider.operands)
    return (*res[:n_out], res[n_out:])


def _latent_fwd(lat, g_q, g_kv, rope_tabs, rank):
    S, W = lat.shape
    ts = min(256, S)
    tab = pl.BlockSpec((ts, LANES), lambda i: (i, 0))

    def body(lat_ref, gq_ref, gkv_ref, cos_ref, slo_ref, shi_ref, cq_ref, ckv_ref, kr_ref):
        lq = lat_ref[:, 0:rank]
        lkv = lat_ref[:, rank:2 * rank]
        cq_ref[...] = (lq * _rms(lq) * gq_ref[...]).astype(BF16)
        ckv_ref[...] = (lkv * _rms(lkv) * gkv_ref[...]).astype(BF16)
        kr_ref[...] = _rope(lat_ref[:, 2 * rank:W], cos_ref[...], slo_ref[...], shi_ref[...]).astype(BF16)

    return pl.pallas_call(
        body, name="mla_latent_fwd", grid=(S // ts,),
        in_specs=[pl.BlockSpec((ts, W), lambda i: (i, 0)), _vec_spec(rank), _vec_spec(rank), tab, tab, tab],
        out_specs=[pl.BlockSpec((ts, rank), lambda i: (i, 0)), pl.BlockSpec((ts, rank), lambda i: (i, 0)), tab],
        out_shape=[_sds((S, rank), BF16), _sds((S, rank), BF16), _sds((S, LANES), BF16)],
        compiler_params=_cparams("parallel"))(lat, g_q, g_kv, *rope_tabs)


def _latent_bwd(lat, dcq, dckv, dkr, g_q, g_kv, rope_tabs, rank):
    S, W = lat.shape
    ts = min(256, S)
    tab = pl.BlockSpec((ts, LANES), lambda i: (i, 0))
    half = pl.BlockSpec((ts, rank), lambda i: (i, 0))

    def body(lat_ref, dcq_ref, dckv_ref, dkr_ref, gq_ref, gkv_ref, cos_ref, slo_ref, shi_ref, dlat_ref, sums_ref):
        @pl.when(pl.program_id(0) == 0)
        def _():
            sums_ref[...] = jnp.zeros_like(sums_ref)

        def norm_bwd(v, dn, g, r):
            rr = _rms(v)
            vhat = v * rr
            u = dn * g
            sums_ref[r:r + 1, :] += jnp.sum(dn * vhat, axis=0, keepdims=True)
            return rr * (u - vhat * jnp.mean(u * vhat, axis=-1, keepdims=True))

        dlat_ref[:, 0:rank] = norm_bwd(lat_ref[:, 0:rank], dcq_ref[...], gq_ref[...], 0).astype(BF16)
        dlat_ref[:, rank:2 * rank] = norm_bwd(lat_ref[:, rank:2 * rank], dckv_ref[...], gkv_ref[...], 1).astype(BF16)
        dlat_ref[:, 2 * rank:W] = _rope_t(dkr_ref[...], cos_ref[...], slo_ref[...], shi_ref[...]).astype(BF16)

    return pl.pallas_call(
        body, name="mla_latent_bwd", grid=(S // ts,),
        in_specs=[pl.BlockSpec((ts, W), lambda i: (i, 0)), half, half, tab, _vec_spec(rank), _vec_spec(rank),
                  tab, tab, tab],
        out_specs=[pl.BlockSpec((ts, W), lambda i: (i, 0)), pl.BlockSpec((8, rank), lambda i: (0, 0))],
        out_shape=[_sds((S, W), BF16), _sds((8, rank), F32)],
        compiler_params=_cparams("arbitrary"))(lat, dcq, dckv, dkr, g_q, g_kv, *rope_tabs)


def _attn_tiles(S):
    t = min(512, S)
    return t, S // t


def _causal_mask(t):
    return lax.broadcasted_iota(jnp.int32, (t, t), 1) <= lax.broadcasted_iota(jnp.int32, (t, t), 0)


def _attn_fwd(q, kv, kr, heads, scale, rider=NO_RIDER):
    S = q.shape[0]
    t, nb = _attn_tiles(S)
    G = 2 if heads % 2 == 0 else 1
    n_ri, n_ro = len(rider.operands), len(rider.out_shape)

    def body(*refs):
        q_ref, kv_ref, kr_ref = refs[:3]
        r_in = refs[3:3 + n_ri]
        o_ref, lse_ref = refs[3 + n_ri:5 + n_ri]
        r_out = refs[5 + n_ri:5 + n_ri + n_ro]
        m_scr, acc_scr = refs[5 + n_ri + n_ro:7 + n_ri + n_ro]
        r_sems = refs[7 + n_ri + n_ro:]
        h, qi, ki = pl.program_id(0), pl.program_id(1), pl.program_id(2)

        if rider.start is not None:
            @pl.when((h == 0) & (qi == 0) & (ki == 0))
            def _():
                rider.start(r_in, r_out, r_sems)

        @pl.when(ki == 0)
        def _():
            m_scr[...] = jnp.full_like(m_scr, NEG_INF)
            acc_scr[...] = jnp.zeros_like(acc_scr)

        def step(diagonal):
            ones = jnp.ones((t, LANES), BF16)
            for g in range(G):
                kcat = jnp.concatenate([kv_ref[:, g * QK_PAD:g * QK_PAD + QK_NOPE], kr_ref[...]], axis=1)
                vext = jnp.concatenate([kv_ref[:, g * QK_PAD + QK_NOPE:(g + 1) * QK_PAD], ones], axis=1)
                s = lax.dot_general(q_ref[:, g * QK_PAD:(g + 1) * QK_PAD], kcat, DIMS_NT,
                                    preferred_element_type=F32) * scale
                if diagonal:
                    s = jnp.where(_causal_mask(t), s, NEG_INF)
                m_prev = m_scr[g]
                m_new = jnp.maximum(m_prev, jnp.max(s, axis=-1, keepdims=True))
                alpha = jnp.exp(m_prev - m_new)
                p = jnp.exp(s - jnp.tile(m_new, (1, t // LANES)))
                acc_scr[g] = jnp.tile(alpha, (1, 2)) * acc_scr[g] + lax.dot_general(
                    p.astype(BF16), vext, DIMS_NN, preferred_element_type=F32)
                m_scr[g] = m_new

        @pl.when(ki < qi)
        def _():
            step(False)

        @pl.when(ki == qi)
        def _():
            step(True)

        @pl.when(ki == nb - 1)
        def _():
            for g in range(G):
                acc = acc_scr[g]
                o_ref[:, g * V_HEAD:(g + 1) * V_HEAD] = (acc[:, 0:V_HEAD] / acc[:, V_HEAD:2 * V_HEAD]).astype(BF16)
                lse_ref[g] = m_scr[g] + jnp.log(acc[:, V_HEAD:2 * V_HEAD])

        if rider.finish is not None:
            @pl.when((h == heads // G - 1) & (qi == nb - 1) & (ki == nb - 1))
            def _():
                rider.finish(r_in, r_out, r_sems)

    res = pl.pallas_call(
        body, name="mla_attn_fwd", grid=(heads // G, nb, nb),
        in_specs=[pl.BlockSpec((t, G * QK_PAD), lambda h, qi, ki: (qi, h)),
                  pl.BlockSpec((t, G * QK_PAD), lambda h, qi, ki: (jnp.minimum(ki, qi), h)),
                  pl.BlockSpec((t, LANES), lambda h, qi, ki: (jnp.minimum(ki, qi), 0))] + [ANY] * n_ri,
        out_specs=[pl.BlockSpec((t, G * V_HEAD), lambda h, qi, ki: (qi, h)),
                   pl.BlockSpec((G, t, LANES), lambda h, qi, ki: (h, qi, 0))] + [ANY] * n_ro,
        out_shape=[_sds((S, heads * V_HEAD), BF16), _sds((heads, S, LANES), F32)] + list(rider.out_shape),
        scratch_shapes=[pltpu.VMEM((G, t, LANES), F32), pltpu.VMEM((G, t, 2 * V_HEAD), F32)] + list(rider.sems),
        input_output_aliases={3 + i: 2 + o for i, o in rider.aliases.items()},
        compiler_params=_cparams("arbitrary", "arbitrary", "arbitrary"))(q, kv, kr, *rider.operands)
    return res[0], res[1], res[2:]


def _attn_delta(o, do, heads):
    S = o.shape[0]
    t, nb = _attn_tiles(S)

    def body(o_ref, do_ref, out_ref):
        d = jnp.sum(do_ref[...].astype(F32) * o_ref[...].astype(F32), axis=-1, keepdims=True)
        out_ref[...] = jnp.broadcast_to(d, (t, LANES))

    blk = pl.BlockSpec((t, V_HEAD), lambda h, i: (i, h))
    return pl.pallas_call(
        body, name="mla_attn_delta", grid=(heads, nb), in_specs=[blk, blk],
        out_specs=pl.BlockSpec((None, t, LANES), lambda h, i: (h, i, 0)),
        out_shape=_sds((heads, S, LANES), F32), compiler_params=_cparams("parallel", "parallel"))(o, do)


def _attn_bwd(q, kv, kr, delta, do, lse, rope_tabs, heads, scale, rider=NO_RIDER):
    S = q.shape[0]
    t, nb = _attn_tiles(S)
    n_ri, n_ro = len(rider.operands), len(rider.out_shape)
    rep = t // LANES

    def body(*refs):
        q_ref, kv_ref, kr_ref, delta_ref, do_ref, lse_ref, cos_ref, slo_ref, shi_ref = refs[:9]
        r_in = refs[9:9 + n_ri]
        dq_ref, dkv_ref, dkr_ref = refs[9 + n_ri:12 + n_ri]
        r_out = refs[12 + n_ri:12 + n_ri + n_ro]
        dq_scr, dk_scr, dv_scr, dkr_scr = refs[12 + n_ri + n_ro:16 + n_ri + n_ro]
        r_sems = refs[16 + n_ri + n_ro:]
        h, ki, qi = pl.program_id(0), pl.program_id(1), pl.program_id(2)
        q_rows = pl.ds(pl.multiple_of(qi * t, t), t)
        k_rows = pl.ds(pl.multiple_of(ki * t, t), t)

        if rider.start is not None:
            @pl.when((h == 0) & (ki == 0) & (qi == 0))
            def _():
                rider.start(r_in, r_out, r_sems)

        @pl.when((ki == 0) & (qi == 0))
        def _():
            dq_scr[...] = jnp.zeros_like(dq_scr)

        @pl.when((h == 0) & (ki == 0) & (qi == 0))
        def _():
            dkr_scr[...] = jnp.zeros_like(dkr_scr)

        @pl.when(qi == 0)
        def _():
            dk_scr[...] = jnp.zeros_like(dk_scr)
            dv_scr[...] = jnp.zeros_like(dv_scr)

        def step(diagonal):
            qv = q_ref[...]
            kcat = jnp.concatenate([kv_ref[:, 0:QK_NOPE], kr_ref[...]], axis=1)
            s = lax.dot_general(qv, kcat, DIMS_NT, preferred_element_type=F32) * scale
            p = jnp.exp(s - jnp.tile(lse_ref[...], (1, rep)))
            if diagonal:
                p = jnp.where(_causal_mask(t), p, 0.0)
            dov = do_ref[...]
            dv_scr[...] += lax.dot_general(p.astype(BF16), dov, DIMS_TN, preferred_element_type=F32)
            dp = lax.dot_general(dov, kv_ref[:, QK_NOPE:QK_NOPE + V_HEAD], DIMS_NT, preferred_element_type=F32)
            ds = (p * (dp - jnp.tile(delta_ref[...], (1, rep))) * scale).astype(BF16)
            dk_scr[...] += lax.dot_general(ds, qv, DIMS_TN, preferred_element_type=F32)
            dq_scr[q_rows, :] += lax.dot_general(ds, kcat, DIMS_NN, preferred_element_type=F32)

        @pl.when(qi > ki)
        def _():
            step(False)

        @pl.when(qi == ki)
        def _():
            step(True)

        @pl.when(qi == nb - 1)
        def _():
            dkv_ref[...] = jnp.concatenate([dk_scr[:, 0:QK_NOPE], dv_scr[...]], axis=1).astype(BF16)
            dkr_scr[k_rows, :] += dk_scr[:, QK_NOPE:QK_PAD]

        @pl.when(ki == nb - 1)
        def _():
            dqv = dq_scr[q_rows, :]
            dq_ref[q_rows, :] = jnp.concatenate(
                [dqv[:, 0:QK_NOPE], _rope_t(dqv[:, QK_NOPE:QK_PAD], cos_ref[...], slo_ref[...], shi_ref[...])],
                axis=1).astype(BF16)

        @pl.when((h == heads - 1) & (ki == nb - 1) & (qi == nb - 1))
        def _():
            dkr_ref[...] = dkr_scr[...]
            if rider.finish is not None:
                rider.finish(r_in, r_out, r_sems)

    qmap = lambda h, ki, qi: (jnp.maximum(qi, ki), h)
    stat = pl.BlockSpec((None, t, LANES), lambda h, ki, qi: (h, jnp.maximum(qi, ki), 0))
    tab = pl.BlockSpec((t, LANES), lambda h, ki, qi: (qi, 0))
    res = pl.pallas_call(
        body, name="mla_attn_bwd", grid=(heads, nb, nb),
        in_specs=[pl.BlockSpec((t, QK_PAD), qmap),
                  pl.BlockSpec((t, QK_PAD), lambda h, ki, qi: (ki, h)),
                  pl.BlockSpec((t, LANES), lambda h, ki, qi: (ki, 0)),
                  stat,
                  pl.BlockSpec((t, V_HEAD), qmap),
                  stat,
                  tab, tab, tab] + [ANY] * n_ri,
        out_specs=[pl.BlockSpec((S, QK_PAD), lambda h, ki, qi: (0, h)),
                   pl.BlockSpec((t, QK_PAD), lambda h, ki, qi: (ki, h)),
                   pl.BlockSpec((S, LANES), lambda h, ki, qi: (0, 0))] + [ANY] * n_ro,
        out_shape=[_sds((S, heads * QK_PAD), BF16), _sds((S, heads * QK_PAD), BF16), _sds((S, LANES), F32)]
        + list(rider.out_shape),
        scratch_shapes=[pltpu.VMEM((S, QK_PAD), F32), pltpu.VMEM((t, QK_PAD), F32), pltpu.VMEM((t, V_HEAD), F32),
                        pltpu.VMEM((S, LANES), F32)] + list(rider.sems),
        input_output_aliases={9 + i: 3 + o for i, o in rider.aliases.items()},
        compiler_params=_cparams("arbitrary", "arbitrary", "arbitrary"))(q, kv, kr, delta, do, lse, *rope_tabs, *rider.operands)
    return res[0], res[1], res[2], res[3:]


def _causal_pairs(nb, q_major):
    if q_major:
        pairs = [(qi, ki) for qi in range(nb) for ki in range(qi + 1)]
    else:
        pairs = [(qi, ki) for ki in range(nb) for qi in range(ki, nb)]
    return jnp.array([p[0] for p in pairs], jnp.int32), jnp.array([p[1] for p in pairs], jnp.int32), len(pairs)


def _heads_per_step(heads):
    return 2 if heads % 2 == 0 else 1


def _attn_fwd_tri(q, kv, kr, heads, scale, rider=NO_RIDER):
    S = q.shape[0]
    t, nb = _attn_tiles(S)
    G = _heads_per_step(heads)
    q_tab, k_tab, n_pairs = _causal_pairs(nb, True)
    n_ri, n_ro = len(rider.operands), len(rider.out_shape)

    def body(qt_ref, kt_ref, *refs):
        q_ref, kv_ref, kr_ref = refs[:3]
        r_in = refs[3:3 + n_ri]
        o_ref, lse_ref = refs[3 + n_ri:5 + n_ri]
        r_out = refs[5 + n_ri:5 + n_ri + n_ro]
        m_scr, acc_scr = refs[5 + n_ri + n_ro:7 + n_ri + n_ro]
        r_sems = refs[7 + n_ri + n_ro:]
        h, p = pl.program_id(0), pl.program_id(1)
        qi, ki = qt_ref[p], kt_ref[p]

        if rider.start is not None:
            @pl.when((h == 0) & (p == 0))
            def _():
                rider.start(r_in, r_out, r_sems)

        @pl.when(ki == 0)
        def _():
            m_scr[...] = jnp.full_like(m_scr, NEG_INF)
            acc_scr[...] = jnp.zeros_like(acc_scr)

        def step(diagonal):
            ones = jnp.ones((t, LANES), BF16)
            for g in range(G):
                kcat = jnp.concatenate([kv_ref[:, g * QK_PAD:g * QK_PAD + QK_NOPE], kr_ref[...]], axis=1)
                vext = jnp.concatenate([kv_ref[:, g * QK_PAD + QK_NOPE:(g + 1) * QK_PAD], ones], axis=1)
                s = lax.dot_general(q_ref[:, g * QK_PAD:(g + 1) * QK_PAD], kcat, DIMS_NT,
                                    preferred_element_type=F32) * scale
                if diagonal:
                    s = jnp.where(_causal_mask(t), s, NEG_INF)
                m_prev = m_scr[g]
                m_new = jnp.maximum(m_prev, jnp.max(s, axis=-1, keepdims=True))
                alpha = jnp.exp(m_prev - m_new)
                pr = jnp.exp(s - jnp.tile(m_new, (1, t // LANES)))
                acc_scr[g] = jnp.tile(alpha, (1, 2)) * acc_scr[g] + lax.dot_general(
                    pr.astype(BF16), vext, DIMS_NN, preferred_element_type=F32)
                m_scr[g] = m_new

        @pl.when(ki < qi)
        def _():
            step(False)

        @pl.when(ki == qi)
        def _():
            step(True)
            for g in range(G):
                acc = acc_scr[g]
                o_ref[:, g * V_HEAD:(g + 1) * V_HEAD] = (acc[:, 0:V_HEAD] / acc[:, V_HEAD:2 * V_HEAD]).astype(BF16)
                lse_ref[g] = m_scr[g] + jnp.log(acc[:, V_HEAD:2 * V_HEAD])

        if rider.finish is not None:
            @pl.when((h == heads // G - 1) & (p == n_pairs - 1))
            def _():
                rider.finish(r_in, r_out, r_sems)

    res = pl.pallas_call(
        body, name="mla_attn_fwd",
        grid_spec=pltpu.PrefetchScalarGridSpec(
            num_scalar_prefetch=2, grid=(heads // G, n_pairs),
            in_specs=[pl.BlockSpec((t, G * QK_PAD), lambda h, p, qt, kt: (qt[p], h)),
                      pl.BlockSpec((t, G * QK_PAD), lambda h, p, qt, kt: (kt[p], h)),
                      pl.BlockSpec((t, LANES), lambda h, p, qt, kt: (kt[p], 0))] + [ANY] * n_ri,
            out_specs=[pl.BlockSpec((t, G * V_HEAD), lambda h, p, qt, kt: (qt[p], h)),
                       pl.BlockSpec((G, t, LANES), lambda h, p, qt, kt: (h, qt[p], 0))] + [ANY] * n_ro,
            scratch_shapes=[pltpu.VMEM((G, t, LANES), F32), pltpu.VMEM((G, t, 2 * V_HEAD), F32)] + list(rider.sems)),
        out_shape=[_sds((S, heads * V_HEAD), BF16), _sds((heads, S, LANES), F32)] + list(rider.out_shape),
        input_output_aliases={5 + i: 2 + o for i, o in rider.aliases.items()},
        compiler_params=_cparams("arbitrary", "arbitrary"))(q_tab, k_tab, q, kv, kr, *rider.operands)
    return res[0], res[1], res[2:]


def _attn_bwd_tri(q, kv, kr, o, do, lse, rope_tabs, heads, scale, rider=NO_RIDER):
    S = q.shape[0]
    t, nb = _attn_tiles(S)
    G = _heads_per_step(heads)
    q_tab, k_tab, n_pairs = _causal_pairs(nb, False)
    n_ri, n_ro = len(rider.operands), len(rider.out_shape)
    rep = t // LANES

    def body(qt_ref, kt_ref, *refs):
        q_ref, kv_ref, kr_ref, o_ref, do_ref, lse_ref, cos_ref, slo_ref, shi_ref = refs[:9]
        r_in = refs[9:9 + n_ri]
        dq_ref, dkv_ref, dkr_ref = refs[9 + n_ri:12 + n_ri]
        r_out = refs[12 + n_ri:12 + n_ri + n_ro]
        dq_scr, dk_scr, dv_scr, dkr_scr, delta_scr = refs[12 + n_ri + n_ro:17 + n_ri + n_ro]
        r_sems = refs[17 + n_ri + n_ro:]
        h, p = pl.program_id(0), pl.program_id(1)
        qi, ki = qt_ref[p], kt_ref[p]
        q_rows = pl.ds(pl.multiple_of(qi * t, t), t)
        k_rows = pl.ds(pl.multiple_of(ki * t, t), t)

        @pl.when(ki == 0)
        def _():
            for g in range(G):
                cols = slice(g * V_HEAD, (g + 1) * V_HEAD)
                d = jnp.sum(do_ref[:, cols].astype(F32) * o_ref[:, cols].astype(F32), axis=-1, keepdims=True)
                delta_scr[g, q_rows, :] = jnp.broadcast_to(d, (t, LANES))

        if rider.start is not None:
            @pl.when((h == 0) & (p == 0))
            def _():
                rider.start(r_in, r_out, r_sems)

        @pl.when(p == 0)
        def _():
            dq_scr[...] = jnp.zeros_like(dq_scr)

        @pl.when((h == 0) & (p == 0))
        def _():
            dkr_scr[...] = jnp.zeros_like(dkr_scr)

        @pl.when(qi == ki)
        def _():
            dk_scr[...] = jnp.zeros_like(dk_scr)
            dv_scr[...] = jnp.zeros_like(dv_scr)

        def step(diagonal):
            for g in range(G):
                qv = q_ref[:, g * QK_PAD:(g + 1) * QK_PAD]
                kcat = jnp.concatenate([kv_ref[:, g * QK_PAD:g * QK_PAD + QK_NOPE], kr_ref[...]], axis=1)
                s = lax.dot_general(qv, kcat, DIMS_NT, preferred_element_type=F32) * scale
                pr = jnp.exp(s - jnp.tile(lse_ref[g], (1, rep)))
                if diagonal:
                    pr = jnp.where(_causal_mask(t), pr, 0.0)
                dov = do_ref[:, g * V_HEAD:(g + 1) * V_HEAD]
                dv_scr[g] += lax.dot_general(pr.astype(BF16), dov, DIMS_TN, preferred_element_type=F32)
                dp = lax.dot_general(dov, kv_ref[:, g * QK_PAD + QK_NOPE:(g + 1) * QK_PAD], DIMS_NT,
                                     preferred_element_type=F32)
                ds = (pr * (dp - jnp.tile(delta_scr[g, q_rows, :], (1, rep))) * scale).astype(BF16)
                dk_scr[g] += lax.dot_general(ds, qv, DIMS_TN, preferred_element_type=F32)
                dq_scr[q_rows, g * QK_PAD:(g + 1) * QK_PAD] += lax.dot_general(ds, kcat, DIMS_NN,
                                                                               preferred_element_type=F32)

        @pl.when(qi > ki)
        def _():
            step(False)

        @pl.when(qi == ki)
        def _():
            step(True)
            for g in range(G):
                dqv = dq_scr[q_rows, g * QK_PAD:(g + 1) * QK_PAD]
                dq_ref[q_rows, g * QK_PAD:(g + 1) * QK_PAD] = jnp.concatenate(
                    [dqv[:, 0:QK_NOPE], _rope_t(dqv[:, QK_NOPE:QK_PAD], cos_ref[...], slo_ref[...], shi_ref[...])],
                    axis=1).astype(BF16)

        @pl.when(qi == nb - 1)
        def _():
            for g in range(G):
                dkv_ref[:, g * QK_PAD:(g + 1) * QK_PAD] = jnp.concatenate(
                    [dk_scr[g][:, 0:QK_NOPE], dv_scr[g]], axis=1).astype(BF16)
                dkr_scr[k_rows, :] += dk_scr[g][:, QK_NOPE:QK_PAD]

        @pl.when((h == heads // G - 1) & (p == n_pairs - 1))
        def _():
            dkr_ref[...] = dkr_scr[...]
            if rider.finish is not None:
                rider.finish(r_in, r_out, r_sems)

    q_blk = lambda w: pl.BlockSpec((t, G * w), lambda h, p, qt, kt: (qt[p], h))
    stat = pl.BlockSpec((G, t, LANES), lambda h, p, qt, kt: (h, qt[p], 0))
    tab = pl.BlockSpec((t, LANES), lambda h, p, qt, kt: (kt[p], 0))
    res = pl.pallas_call(
        body, name="mla_attn_bwd",
        grid_spec=pltpu.PrefetchScalarGridSpec(
            num_scalar_prefetch=2, grid=(heads // G, n_pairs),
            in_specs=[q_blk(QK_PAD),
                      pl.BlockSpec((t, G * QK_PAD), lambda h, p, qt, kt: (kt[p], h)),
                      tab, q_blk(V_HEAD), q_blk(V_HEAD), stat, tab, tab, tab] + [ANY] * n_ri,
            out_specs=[pl.BlockSpec((S, G * QK_PAD), lambda h, p, qt, kt: (0, h)),
                       pl.BlockSpec((t, G * QK_PAD), lambda h, p, qt, kt: (kt[p], h)),
                       pl.BlockSpec((S, LANES), lambda h, p, qt, kt: (0, 0))] + [ANY] * n_ro,
            scratch_shapes=[pltpu.VMEM((S, G * QK_PAD), F32), pltpu.VMEM((G, t, QK_PAD), F32),
                            pltpu.VMEM((G, t, V_HEAD), F32), pltpu.VMEM((S, LANES), F32),
                            pltpu.VMEM((G, S, LANES), F32)] + list(rider.sems)),
        out_shape=[_sds((S, heads * QK_PAD), BF16), _sds((S, heads * QK_PAD), BF16), _sds((S, LANES), F32)]
        + list(rider.out_shape),
        input_output_aliases={11 + i: 3 + o for i, o in rider.aliases.items()},
        compiler_params=_cparams("arbitrary", "arbitrary"))(q_tab, k_tab, q, kv, kr, o, do, lse, *rope_tabs,
                                                            *rider.operands)
    return res[0], res[1], res[2], res[3:]


def _shift_down(z, n, rows):
    return jnp.where(rows >= n, pltpu.roll(z, n, 0), 0.0)


def _shift_up(z, n, rows, S):
    return jnp.where(rows < S - n, pltpu.roll(z, S - n, 0), 0.0)


def _conv_specs(S, tc):
    strip = lambda p: pl.BlockSpec((None, S, tc), lambda j: (p, 0, j))
    return strip(0), strip(1), strip(2), pl.BlockSpec((3, tc), lambda j: (0, j))


def _conv_fwd(proj3, w):
    _, S, D = proj3.shape
    tc = LANES

    def body(b_ref, c_ref, u_ref, w_ref, out_ref):
        z = c_ref[...] * u_ref[...]
        rows = lax.broadcasted_iota(jnp.int32, (S, tc), 0)
        zc = w_ref[0:1, :] * _shift_down(z, 2, rows) + w_ref[1:2, :] * _shift_down(z, 1, rows) + w_ref[2:3, :] * z
        out_ref[...] = (b_ref[...] * zc).astype(BF16)

    return pl.pallas_call(
        body, name="conv_fwd", grid=(D // tc,), in_specs=list(_conv_specs(S, tc)),
        out_specs=pl.BlockSpec((S, tc), lambda j: (0, j)), out_shape=_sds((S, D), BF16),
        compiler_params=_cparams("parallel"))(proj3, proj3, proj3, w)


def _conv_bwd(dbz, proj3, w):
    _, S, D = proj3.shape
    tc = LANES

    def body(d_ref, b_ref, c_ref, u_ref, w_ref, dp_ref, dw_ref):
        cv, uv, dv = c_ref[...], u_ref[...], d_ref[...].astype(F32)
        z = cv * uv
        rows = lax.broadcasted_iota(jnp.int32, (S, tc), 0)
        z1, z2 = _shift_down(z, 1, rows), _shift_down(z, 2, rows)
        zc = w_ref[0:1, :] * z2 + w_ref[1:2, :] * z1 + w_ref[2:3, :] * z
        dp_ref[0] = (dv * zc).astype(BF16)
        dzc = dv * b_ref[...]
        dz = w_ref[2:3, :] * dzc + w_ref[1:2, :] * _shift_up(dzc, 1, rows, S) + w_ref[0:1, :] * _shift_up(dzc, 2, rows, S)
        dp_ref[1] = (dz * uv).astype(BF16)
        dp_ref[2] = (dz * cv).astype(BF16)
        dw_ref[0:1, :] = jnp.sum(dzc * z2, axis=0, keepdims=True)
        dw_ref[1:2, :] = jnp.sum(dzc * z1, axis=0, keepdims=True)
        dw_ref[2:3, :] = jnp.sum(dzc * z, axis=0, keepdims=True)

    sb, sc_, su, sw = _conv_specs(S, tc)
    return pl.pallas_call(
        body, name="conv_bwd", grid=(D // tc,),
        in_specs=[pl.BlockSpec((S, tc), lambda j: (0, j)), sb, sc_, su, sw],
        out_specs=[pl.BlockSpec((3, S, tc), lambda j: (0, 0, j)), pl.BlockSpec((3, tc), lambda j: (0, j))],
        out_shape=[_sds((3, S, D), BF16), _sds((3, D), F32)],
        compiler_params=_cparams("parallel"))(dbz, proj3, proj3, proj3, w)


def _silu(c_all):
    def body(c_ref, o_ref):
        cv = c_ref[...]
        o_ref[...] = cv * (1.0 / (1.0 + jnp.exp(-cv)))

    vm = pl.BlockSpec(memory_space=pltpu.VMEM)
    return pl.pallas_call(body, name="cond_silu", in_specs=[vm], out_specs=vm, out_shape=_sds(c_all.shape, F32))(c_all)


def _mod_fwd(cond, w_mod, b_cols):
    L, D, ncol = w_mod.shape
    B = cond.shape[0]
    tk, tn = min(512, D), min(1024, ncol)
    nk = D // tk

    def body(c_ref, w_ref, b_ref, out_ref, acc):
        kk = pl.program_id(2)
        part = lax.dot_general(c_ref[...].astype(BF16), w_ref[...].astype(BF16), DIMS_NN, preferred_element_type=F32)

        @pl.when(kk == 0)
        def _():
            acc[...] = part

        @pl.when(kk > 0)
        def _():
            acc[...] += part

        @pl.when(kk == nk - 1)
        def _():
            out_ref[...] = acc[...] + b_ref[...]

    return pl.pallas_call(
        body, name="mod_fwd", grid=(L, ncol // tn, nk),
        in_specs=[pl.BlockSpec((B, tk), lambda l, j, k: (0, k)),
                  pl.BlockSpec((None, tk, tn), lambda l, j, k: (l, k, j)),
                  pl.BlockSpec((None, 1, tn), lambda l, j, k: (l, 0, j))],
        out_specs=pl.BlockSpec((None, B, tn), lambda l, j, k: (l, 0, j)),
        out_shape=_sds((L, B, ncol), F32),
        scratch_shapes=[pltpu.VMEM((B, tn), F32)],
        compiler_params=_cparams("parallel", "parallel", "arbitrary"))(cond, w_mod, b_cols)


def _adamw_math(w, g, m, v):
    m = ADAM_B1 * m + (1.0 - ADAM_B1) * g
    v = ADAM_B2 * v + (1.0 - ADAM_B2) * (g * g)
    m_hat = m / (1.0 - ADAM_B1 ** ADAM_STEP)
    v_hat = v / (1.0 - ADAM_B2 ** ADAM_STEP)
    delta = -ADAM_LR * (m_hat / (jnp.sqrt(v_hat) + ADAM_EPS) + ADAM_WD * w)
    return delta, m, v


def _adamw(name, w, g, m, v):
    shape = w.shape
    cols = shape[-1] if w.ndim <= 3 else shape[-2] * shape[-1]
    rows = w.size // cols
    w2, g2, m2, v2 = (t.reshape(rows, cols) for t in (w, g, m, v))
    tr = _row_tile(rows, cols * 4, limit=1024 * 1024, mult=8)
    spec = pl.BlockSpec((tr, cols), lambda i: (i, 0))

    def body(w_ref, g_ref, m_ref, v_ref, d_ref, nm_ref, nv_ref):
        d, nm, nv = _adamw_math(w_ref[...], g_ref[...], m_ref[...], v_ref[...])
        d_ref[...] = d
        nm_ref[...] = nm
        nv_ref[...] = nv

    outs = pl.pallas_call(body, name=name, grid=(rows // tr,), in_specs=[spec] * 4, out_specs=[spec] * 3,
                          out_shape=[_sds((rows, cols), F32)] * 3, compiler_params=_cparams("parallel"))(w2, g2, m2, v2)
    return tuple(t.reshape(shape) for t in outs)


def _adamw_mod(w, cond_t, dmod_cols, m, v, rider=NO_RIDER):
    L, D, ncol = w.shape
    B = cond_t.shape[1]
    tr, tc = min(256, D), min(1024, ncol)
    blk = pl.BlockSpec((None, tr, tc), lambda l, i, j: (l, i, j))
    grid = (L, D // tr, ncol // tc)
    n_ri, n_ro = len(rider.operands), len(rider.out_shape)

    def body(*refs):
        w_ref, ct_ref, dm_ref, m_ref, v_ref = refs[:5]
        r_in = refs[5:5 + n_ri]
        g_ref, d_ref, nm_ref, nv_ref = refs[5 + n_ri:9 + n_ri]
        r_out = refs[9 + n_ri:9 + n_ri + n_ro]
        r_sems = refs[9 + n_ri + n_ro:]
        ids = [pl.program_id(a) for a in range(3)]

        if rider.start is not None:
            @pl.when((ids[0] == 0) & (ids[1] == 0) & (ids[2] == 0))
            def _():
                rider.start(r_in, r_out, r_sems)

        g = lax.dot_general(ct_ref[...], dm_ref[...], DIMS_NN, precision=lax.Precision.HIGHEST,
                            preferred_element_type=F32)
        d, nm, nv = _adamw_math(w_ref[...], g, m_ref[...], v_ref[...])
        g_ref[...] = g
        d_ref[...] = d
        nm_ref[...] = nm
        nv_ref[...] = nv

        if rider.finish is not None:
            @pl.when((ids[0] == grid[0] - 1) & (ids[1] == grid[1] - 1) & (ids[2] == grid[2] - 1))
            def _():
                rider.finish(r_in, r_out, r_sems)

    hosted = rider.start is not None
    res = pl.pallas_call(
        body, name="adamw_w_mod", grid=grid,
        in_specs=[blk, pl.BlockSpec((tr, B), lambda l, i, j: (i, 0)),
                  pl.BlockSpec((None, B, tc), lambda l, i, j: (l, 0, j)), blk, blk] + [ANY] * n_ri,
        out_specs=[blk] * 4 + [ANY] * n_ro, out_shape=[_sds((L, D, ncol), F32)] * 4 + list(rider.out_shape),
        scratch_shapes=list(rider.sems), input_output_aliases={5 + i: 4 + o for i, o in rider.aliases.items()},
        compiler_params=_cparams(*(("arbitrary",) * 3 if hosted else ("parallel",) * 3)))(
            w, cond_t, dmod_cols, m, v, *rider.operands)
    return (*res[:4], res[4:])


def _cast_into_full(name, w, kind, k_idx):
    L, R, C = w.shape
    Rh = R // 2
    tr = _row_tile(Rh, C * 4)
    if kind == "row":
        out_shape = (L, N_CHIPS, 2, Rh, C)
        out_spec = pl.BlockSpec((None, None, None, tr, C), lambda l, h, i, k_ref: (l, k_ref[0], h, i, 0))
    else:
        out_shape = (L, 2, Rh, N_CHIPS * C)
        out_spec = pl.BlockSpec((None, None, tr, C), lambda l, h, i, k_ref: (l, h, i, k_ref[0]))

    def body(k_ref, w_ref, o_ref):
        o_ref[...] = w_ref[...].astype(BF16)

    return pl.pallas_call(
        body, name=name,
        grid_spec=pltpu.PrefetchScalarGridSpec(
            num_scalar_prefetch=1, grid=(L, 2, Rh // tr),
            in_specs=[pl.BlockSpec((None, None, tr, C), lambda l, h, i, k_ref: (l, h, i, 0))],
            out_specs=out_spec),
        out_shape=_sds(out_shape, BF16),
        compiler_params=_cparams("parallel", "parallel", "parallel"))(k_idx, w.reshape(L, 2, Rh, C))


def _pair_sum(name, g5, ra, c_idx):
    L, A, _, Rh, Cc = g5.shape
    tr = _row_tile(Rh, Cc * 4)

    def body(c_ref, g_ref, r_ref, o_ref):
        o_ref[...] = (g_ref[...].astype(F32) + r_ref[...].astype(F32)).astype(BF16)

    blk = pl.BlockSpec((None, None, tr, Cc), lambda l, a, i, c_ref: (l, a, i, 0))
    return pl.pallas_call(
        body, name=name,
        grid_spec=pltpu.PrefetchScalarGridSpec(
            num_scalar_prefetch=1, grid=(L, A, Rh // tr),
            in_specs=[pl.BlockSpec((None, None, None, tr, Cc), lambda l, a, i, c_ref: (l, a, c_ref[0], i, 0)), blk],
            out_specs=blk),
        out_shape=_sds((L, A, Rh, Cc), BF16),
        compiler_params=_cparams("parallel", "parallel", "parallel"))(c_idx, g5, ra)


def _chip_sum(name, p, rb, kc_idx, kind, layer=0, n_layers=1, prev=None):
    _, A, Rh, Cc = p.shape
    C = rb.shape[-1]
    tr = _row_tile(Rh, C * 4)
    if kind == "row":
        own = pl.BlockSpec((None, None, tr, C), lambda i, kc: (0, kc[0], i, 0))
    else:
        own = pl.BlockSpec((None, None, tr, C), lambda i, kc: (0, 0, i, kc[0]))
    peer = lambda j: pl.BlockSpec((None, None, tr, C), lambda i, kc: (j, 0, i, 0))

    def body(kc_ref, p_ref, r0_ref, r1_ref, r2_ref, *rest):
        o_ref = rest[-1]
        o_ref[...] = ((p_ref[...].astype(F32) + r0_ref[...].astype(F32)) + r1_ref[...].astype(F32)) + r2_ref[...].astype(F32)

    operands = [kc_idx, p, rb, rb, rb] + ([prev] if prev is not None else [])
    return pl.pallas_call(
        body, name=name,
        grid_spec=pltpu.PrefetchScalarGridSpec(
            num_scalar_prefetch=1, grid=(Rh // tr,),
            in_specs=[own, peer(0), peer(1), peer(2)] + ([ANY] if prev is not None else []),
            out_specs=pl.BlockSpec((None, None, tr, C), lambda i, kc: (layer, kc[1], i, 0))),
        out_shape=_sds((n_layers, 2, Rh, C), F32),
        input_output_aliases={5: 0} if prev is not None else {},
        compiler_params=_cparams("parallel"))(*operands)


def _mesh_place():
    x, y, c = lax.axis_index("x"), lax.axis_index("y"), lax.axis_index("c")
    chips = [(1 - x, y), (x, 1 - y), (1 - x, 1 - y)]
    return x, y, c, chips


def _remote(src, dst, send_sem, recv_sem, to):
    return pltpu.make_async_remote_copy(src_ref=src, dst_ref=dst, send_sem=send_sem, recv_sem=recv_sem,
                                        device_id=to, device_id_type=MESH_ID)


def _small_allgather(name, v, with_sum=False, rider=NO_RIDER):
    R, N = v.shape
    n_ri, n_ro, n_own = len(rider.operands), len(rider.out_shape), 2 if with_sum else 1

    def body(*refs):
        r_in = refs[1:1 + n_ri]
        r_out = refs[1 + n_ri + n_own:1 + n_ri + n_own + n_ro]
        r_sems = refs[1 + n_ri + n_own + n_ro + 3:]
        own = (refs[0],) + refs[1 + n_ri:1 + n_ri + n_own] + refs[1 + n_ri + n_own + n_ro:1 + n_ri + n_own + n_ro + 3]
        if with_sum:
            x_ref, out_ref, sum_ref, send_sems, recv_sems, local_sem = own
        else:
            x_ref, out_ref, send_sems, recv_sems, local_sem = own
        if rider.start is not None:
            rider.start(r_in, r_out, r_sems)
        x, y, c, chips = _mesh_place()
        me, sibling = (x, y, c), (x, y, 1 - c)

        def rows(px, py, pc):
            return out_ref.at[pl.ds((4 * px + 2 * py + pc) * R, R), :]

        def copy(k, block, to, src=None):
            return _remote(rows(*block) if src is None else src, rows(*block), send_sems.at[k], recv_sems.at[k], to)

        mine = pltpu.make_async_copy(x_ref, rows(*me), local_sem)
        mine.start()
        first = [copy(0, me, sibling, src=x_ref)]
        first += [copy(1 + j, me, (*chip, c), src=x_ref) for j, chip in enumerate(chips)]
        for cp in first:
            cp.start()
        passed = [copy(4 + j, (*chip, c), sibling) for j, chip in enumerate(chips)]
        for j, chip in enumerate(chips):
            copy(1 + j, (*chip, c), me).wait_recv()
            passed[j].start()
        copy(0, sibling, me).wait_recv()
        for j, chip in enumerate(chips):
            copy(4 + j, (*chip, 1 - c), me).wait_recv()
        for cp in first + passed:
            cp.wait_send()
        mine.wait()
        if with_sum:
            total = out_ref[0:R, :]
            for p in range(1, 8):
                total = total + out_ref[p * R:(p + 1) * R, :]
            sum_ref[...] = total
        if rider.finish is not None:
            rider.finish(r_in, r_out, r_sems)

    vm = pl.BlockSpec(memory_space=pltpu.VMEM)
    out_shape = [_sds((8 * R, N), F32)] + ([_sds((R, N), F32)] if with_sum else [])
    res = pl.pallas_call(
        body, name=name, out_shape=out_shape + list(rider.out_shape), in_specs=[vm] + [ANY] * n_ri,
        out_specs=[vm] * n_own + [ANY] * n_ro,
        scratch_shapes=[pltpu.SemaphoreType.DMA((7,)), pltpu.SemaphoreType.DMA((7,)), pltpu.SemaphoreType.DMA]
        + list(rider.sems),
        input_output_aliases={1 + i: n_own + o for i, o in rider.aliases.items()},
        compiler_params=pltpu.CompilerParams(vmem_limit_bytes=VMEM_LIMIT_BYTES))(v, *rider.operands)
    if rider.start is not None:
        return (*res[:n_own], res[n_own:])
    return res if with_sum else res[0]


def _full_place(ref, kind, C, kk, half, layer=None):
    lead = slice(None) if layer is None else pl.ds(layer, 1)
    if kind == "row":
        return ref.at[lead, kk, half]
    return ref.at[lead, half, :, pl.ds(pl.multiple_of(kk * C, LANES), C)]


def _gather_rider(fulls, kinds, shard_cols, layers=None, peers=(0, 1, 2)):
    n = len(fulls)
    layers = layers or [None] * n

    def copies(outs, sems):
        x, y, c, chips = _mesh_place()
        k = 2 * x + y
        place = lambda a, kk, half: _full_place(outs[a], kinds[a], shard_cols[a], kk, half, layers[a])
        copy = lambda a, j, ref, to: _remote(ref, ref, sems[0].at[6 * a + j], sems[1].at[6 * a + j], to)
        return (x, y, c), [(j, chip) for j, chip in enumerate(chips) if j in peers], k, place, copy

    def start(_, outs, sems):
        (x, y, c), chips, k, place, copy = copies(outs, sems)
        for j, chip in chips:
            for a in range(n):
                copy(a, j, place(a, k, c), (*chip, c)).start()

    def finish(_, outs, sems):
        (x, y, c), chips, k, place, copy = copies(outs, sems)
        me, sibling = (x, y, c), (x, y, 1 - c)
        for j, chip in chips:
            kj = 2 * chip[0] + chip[1]
            for a in range(n):
                copy(a, j, place(a, kj, c), me).wait_recv()
                copy(a, 3 + j, place(a, kj, c), sibling).start()
        for j, chip in chips:
            kj = 2 * chip[0] + chip[1]
            for a in range(n):
                copy(a, 3 + j, place(a, kj, 1 - c), me).wait_recv()
        for j, chip in chips:
            kj = 2 * chip[0] + chip[1]
            for a in range(n):
                copy(a, j, place(a, k, c), (*chip, c)).wait_send()
                copy(a, 3 + j, place(a, kj, c), sibling).wait_send()

    return Rider(tuple(fulls), tuple(_sds(f.shape, BF16) for f in fulls), {a: a for a in range(n)},
                 (pltpu.SemaphoreType.DMA((6 * n,)), pltpu.SemaphoreType.DMA((6 * n,))), start, finish)


def _scatter_rider(ps, kinds, shard_cols, peers=(0, 1, 2), into=None):
    n = len(ps)

    def copies(ins, outs, sems):
        x, y, c, chips = _mesh_place()
        cps = []
        for j, chip in enumerate(chips):
            if j not in peers:
                continue
            kj = 2 * chip[0] + chip[1]
            for a in range(n):
                C = shard_cols[a]
                src = ins[a].at[:, kj] if kinds[a] == "row" else ins[a].at[:, 0, :, pl.ds(pl.multiple_of(kj * C, LANES), C)]
                cps.append(_remote(src, outs[a].at[j], sems[0].at[3 * a + j], sems[1].at[3 * a + j], (*chip, c)))
        return cps

    def start(ins, outs, sems):
        for cp in copies(ins, outs, sems):
            cp.start()

    def finish(ins, outs, sems):
        cps = copies(ins, outs, sems)
        for cp in cps:
            cp.wait_recv()
        for cp in cps:
            cp.wait_send()

    out_shape = tuple(_sds((3, p.shape[0], p.shape[2], C), BF16) for p, C in zip(ps, shard_cols))
    aliases = {n + a: a for a in range(n)} if into is not None else {}
    return Rider(tuple(ps) + tuple(into or ()), out_shape, aliases,
                 (pltpu.SemaphoreType.DMA((3 * n,)), pltpu.SemaphoreType.DMA((3 * n,))), start, finish)


def _run_rider(name, rider):
    n_in, n_out = len(rider.operands), len(rider.out_shape)

    def body(*refs):
        ins, outs, sems = refs[:n_in], refs[n_in:n_in + n_out], refs[n_in + n_out:]
        rider.start(ins, outs, sems)
        rider.finish(ins, outs, sems)

    return pl.pallas_call(
        body, name=name, out_shape=list(rider.out_shape), in_specs=[ANY] * n_in, out_specs=[ANY] * n_out,
        input_output_aliases=dict(rider.aliases), scratch_shapes=list(rider.sems),
        compiler_params=pltpu.CompilerParams(vmem_limit_bytes=VMEM_LIMIT_BYTES))(*rider.operands)


def _exchange_rider(g5s):
    n = len(g5s)

    def copies(ins, outs, sems):
        x, y, c, _ = _mesh_place()
        return [_remote(ins[a].at[:, :, 1 - c], outs[a], sems[0].at[a], sems[1].at[a], (x, y, 1 - c)) for a in range(n)]

    def start(ins, outs, sems):
        for cp in copies(ins, outs, sems):
            cp.start()

    def finish(ins, outs, sems):
        cps = copies(ins, outs, sems)
        for cp in cps:
            cp.wait_recv()
        for cp in cps:
            cp.wait_send()

    out_shape = tuple(_sds((g.shape[0], g.shape[1], g.shape[3], g.shape[4]), BF16) for g in g5s)
    return Rider(tuple(g5s), out_shape, {}, (pltpu.SemaphoreType.DMA((n,)), pltpu.SemaphoreType.DMA((n,))), start, finish)


def _share_rider(fs):
    n = len(fs)

    def start(_, outs, sems):
        x, y, c, _p = _mesh_place()
        for a in range(n):
            mine = outs[a].at[:, c]
            _remote(mine, mine, sems[0].at[a], sems[1].at[a], (x, y, 1 - c)).start()

    def finish(_, outs, sems):
        x, y, c, _p = _mesh_place()
        for a in range(n):
            theirs = outs[a].at[:, 1 - c]
            _remote(theirs, theirs, sems[0].at[a], sems[1].at[a], (x, y, c)).wait_recv()
        for a in range(n):
            mine = outs[a].at[:, c]
            _remote(mine, mine, sems[0].at[a], sems[1].at[a], (x, y, 1 - c)).wait_send()

    return Rider(tuple(fs), tuple(_sds(f.shape, F32) for f in fs), {a: a for a in range(n)},
                 (pltpu.SemaphoreType.DMA((n,)), pltpu.SemaphoreType.DMA((n,))), start, finish)


def _both_riders(r1, r2):
    ni, no, ns = len(r1.operands), len(r1.out_shape), len(r1.sems)
    aliases = dict(r1.aliases)
    aliases.update({ni + i: no + o for i, o in r2.aliases.items()})

    def start(ins, outs, sems):
        r1.start(ins[:ni], outs[:no], sems[:ns])
        r2.start(ins[ni:], outs[no:], sems[ns:])

    def finish(ins, outs, sems):
        r1.finish(ins[:ni], outs[:no], sems[:ns])
        r2.finish(ins[ni:], outs[no:], sems[ns:])

    return Rider(r1.operands + r2.operands, r1.out_shape + r2.out_shape, aliases, r1.sems + r2.sems, start, finish)


def _pack_rows(parts, lane_mult=1024):
    flat = jnp.concatenate([p.reshape(-1).astype(F32) for p in parts])
    n = -(-flat.shape[0] // (8 * lane_mult)) * lane_mult
    return jnp.pad(flat, (0, 8 * n - flat.shape[0])).reshape(8, n)


def _relu2(acc):
    r = jnp.maximum(acc, 0.0)
    return r, r * r


def _times_2r(acc, r):
    return (acc * (2.0 * r.astype(F32)),)


def kernel(x, c, positions, w_mod, b_mod, norm_g, mla_w_in, mla_g_q, mla_g_kv, mla_w_uq, mla_w_ukv, mla_w_o, conv_w_in, conv_w, conv_w_out, mlp_w_up, mlp_w_down, loss_target, m_w_mod, m_b_mod, m_norm_g, m_mla_w_in, m_mla_g_q, m_mla_g_kv, m_mla_w_uq, m_mla_w_ukv, m_mla_w_o, m_conv_w_in, m_conv_w, m_conv_w_out, m_mlp_w_up, m_mlp_w_down, v_w_mod, v_b_mod, v_norm_g, v_mla_w_in, v_mla_g_q, v_mla_g_kv, v_mla_w_uq, v_mla_w_ukv, v_mla_w_o, v_conv_w_in, v_conv_w, v_conv_w_out, v_mlp_w_up, v_mlp_w_down):
    S, D = x.shape[1], x.shape[2]
    Dq = D // N_CHIPS
    ncol = w_mod.shape[2]
    n_mod = N_CHIPS * ncol // D
    F = mlp_w_up.shape[2] * N_CHIPS
    lat_dim = mla_w_in.shape[2]
    rank = mla_g_q.shape[1]
    H = mla_w_uq.shape[2]
    d_qk = mla_w_uq.shape[3]
    assert mla_g_kv.shape[1] == rank and lat_dim == 2 * rank + QK_ROPE and d_qk == QK_NOPE + QK_ROPE
    assert mla_w_ukv.shape[3] == QK_NOPE + V_HEAD and x.shape[0] == 1 and n_mod == 6
    assert norm_g.shape[0] == 2 and mla_w_in.shape[0] == 1 and conv_w_in.shape[0] == 1
    lat_pad = 2 * rank + LANES
    scale = float(d_qk) ** -0.5

    xi, yi, ci = lax.axis_index("x"), lax.axis_index("y"), lax.axis_index("c")
    chip = 2 * xi + yi
    dev = 2 * chip + ci
    c_idx = jnp.reshape(ci, (1,)).astype(jnp.int32)
    k_idx = jnp.reshape(chip, (1,)).astype(jnp.int32)

    n1 = D + 2 * D + 3 * Dq
    g1 = _small_allgather("gather_small_inputs", _pack_rows([c, norm_g, conv_w])).reshape(8, -1)
    c_all = g1[:, :D]
    by_chip = g1[0::2]
    norm_full = jnp.concatenate([by_chip[kk, D:3 * D].reshape(2, 4, Dq) for kk in range(N_CHIPS)], axis=-1)
    convw_full = jnp.concatenate([by_chip[kk, 3 * D:n1].reshape(3, Dq) for kk in range(N_CHIPS)], axis=-1)

    b_cols = lax.dynamic_slice(b_mod, (0, chip * ncol), (2, ncol)).reshape(2, 1, ncol)
    cond_all = _silu(c_all)
    mod_cols = _mod_fwd(cond_all, w_mod, b_cols)
    g2 = _small_allgather("gather_mod", _pack_rows([mod_cols]))
    g2 = g2.reshape(8, -1)[0::2, :2 * 8 * ncol].reshape(N_CHIPS, 2, 8, ncol)
    mod_all = jnp.transpose(g2, (2, 1, 0, 3)).reshape(8, 2, n_mod * D)
    mod_me = lax.dynamic_index_in_dim(mod_all, dev, axis=0, keepdims=False)
    mods = [[mod_me[l, i * D:(i + 1) * D].reshape(1, D) for i in range(n_mod)] for l in range(2)]
    ng = [[norm_full[l, i].reshape(1, D) for i in range(4)] for l in range(2)]

    pos = positions[0].astype(F32)
    inv_freq = ROPE_THETA ** (-jnp.arange(0, QK_ROPE, 2, dtype=F32) / QK_ROPE)
    ang = pos[:, None] * inv_freq
    cos, sin = jnp.cos(ang), jnp.sin(ang)
    zero = jnp.zeros_like(cos)
    rope_tabs = (jnp.concatenate([cos, cos, zero, zero], axis=1),
                 jnp.concatenate([-sin, zero, zero, zero], axis=1),
                 jnp.concatenate([zero, sin, zero, zero], axis=1))

    weights = [("mla_w_in", mla_w_in, "row"), ("mla_w_uq", mla_w_uq.reshape(1, rank // N_CHIPS, H * d_qk), "row"),
               ("mla_w_ukv", mla_w_ukv.reshape(1, rank // N_CHIPS, H * QK_PAD), "row"), ("mla_w_o", mla_w_o, "row"),
               ("conv_w_in", conv_w_in, "col"), ("conv_w_out", conv_w_out, "row"),
               ("mlp_w_up", mlp_w_up, "col"), ("mlp_w_down", mlp_w_down, "row")]
    kinds = [k for _, _, k in weights]
    shard_shapes = [w.shape for _, w, _ in weights]
    shard_cols = [s[2] for s in shard_shapes]
    casted = [_cast_into_full("cast_" + nm, w, kind, k_idx) for nm, w, kind in weights]
    W_IN, W_UQ, W_UKV, W_O, W_CIN, W_COUT, W_UP, W_DOWN = range(8)
    mla_idx = [W_IN, W_UQ, W_UKV, W_O]

    def view(i, buf):
        L, R, C = shard_shapes[i]
        return buf.reshape((L, N_CHIPS * R, C) if kinds[i] == "row" else (L, R, N_CHIPS * C))

    NEIGHBOURS, DIAGONAL = (0, 1), (2,)

    def gather_of(bufs, idx, layers=None, peers=(0, 1, 2)):
        return _gather_rider(bufs, [kinds[i] for i in idx], [shard_cols[i] for i in idx], layers, peers)

    def scatter_of(ps, idx, peers=(0, 1, 2), into=None):
        return _scatter_rider(ps, [kinds[i] for i in idx], [shard_cols[i] for i in idx], peers, into)

    def halves(items):
        g5s = []
        for _, i, g in items:
            _, R, C = shard_shapes[i]
            g5s.append(g.reshape((1, N_CHIPS, 2, R // 2, C) if kinds[i] == "row" else (1, 1, 2, R // 2, N_CHIPS * C)))
        return g5s

    def pair_sums(items, g5s, ras):
        return [_pair_sum("pair_sum_" + nm, g5, ra, c_idx) for (nm, _, _), g5, ra in zip(items, g5s, ras)]

    got = _run_rider("gather_weights_mla", gather_of([casted[i] for i in mla_idx], mla_idx))
    w_in_p = jnp.pad(view(W_IN, got[0])[0], ((0, 0), (0, lat_pad - lat_dim)))
    w_q_p = jnp.pad(view(W_UQ, got[1])[0].reshape(rank, H, d_qk), ((0, 0), (0, 0), (0, QK_PAD - d_qk))).reshape(rank, H * QK_PAD)
    w_ukv, w_o = view(W_UKV, got[2])[0], view(W_O, got[3])[0]
    HV = H * V_HEAD

    def layer_b(l, transposed):
        if transposed:
            return lambda tm, tn, tk: pl.BlockSpec((None, tn, tk), lambda i, j, k: (l, j, k))
        return lambda tm, tn, tk: pl.BlockSpec((None, tk, tn), lambda i, j, k: (l, k, j))

    def mlp_up(tag, l, h, w, rider=NO_RIDER):
        return _mm("mlp_up_" + tag, h, w, "nn", S, F, D, [_sds((S, F), BF16)] * 2, epilogue=_relu2,
                   b_spec=layer_b(l, False), rider=rider)

    def mlp_down(tag, l, a2, w, rider=NO_RIDER):
        return _mm("mlp_down_" + tag, a2, w, "nn", S, D, F, [_sds((S, D), BF16)], b_spec=layer_b(l, False), rider=rider)

    def mlp_bwd(tag, l, h, r, a2, dy, rider_of=None):
        carried = ()
        first = rider_of(carried) if rider_of else NO_RIDER
        res = _mm("mlp_down_dx_" + tag, dy, w_down, "nt", S, F, D, [_sds((S, F), BF16)], epilogue=_times_2r,
                  b_spec=layer_b(l, True), rider=first,
                  extras=[(r, lambda tm, tn, tk: pl.BlockSpec((tm, tn), lambda i, j, k: (i, j)))])
        (da,), carried = res if rider_of else (res, ())
        second = rider_of(carried) if rider_of else NO_RIDER
        res = _mm("mlp_down_dw_" + tag, a2, dy, "tn", F, D, S, [_sds((F, D), BF16)], rider=second)
        (dw_down,), carried = res if rider_of else (res, ())
        (dh,) = _mm("mlp_up_dx_" + tag, da, w_up, "nt", S, D, F, [_sds((S, D), BF16)], b_spec=layer_b(l, True))
        (dw_up,) = _mm("mlp_up_dw_" + tag, h, da, "tn", D, F, S, [_sds((D, F), BF16)])
        return dh, dw_up, dw_down, carried

    x0 = x[0]
    sh1, sc1, gt1, sh2, sc2, gt2 = mods[0]
    (h1,) = _fwd_boundary("fwd_boundary_0", x0, None, None, None, ng[0][0], sc1, sh1)
    (lat,) = _mm("mla_in", h1, w_in_p, "nn", S, lat_pad, D, [_sds((S, lat_pad), F32)], tn=lat_pad)
    cq, ckv, kr = _latent_fwd(lat, mla_g_q, mla_g_kv, rope_tabs, rank)

    def rope_q(acc, cos_p, sin_lo, sin_hi):
        parts = []
        for hh in range(acc.shape[1] // QK_PAD):
            parts.append(acc[:, hh * QK_PAD:hh * QK_PAD + QK_NOPE])
            parts.append(_rope(acc[:, hh * QK_PAD + QK_NOPE:(hh + 1) * QK_PAD], cos_p, sin_lo, sin_hi))
        return (jnp.concatenate(parts, axis=1),)

    tab_extra = lambda tm, tn, tk: pl.BlockSpec((tm, LANES), lambda i, j, k: (i, 0))
    (q,) = _mm("mla_q", cq, w_q_p, "nn", S, H * QK_PAD, rank, [_sds((S, H * QK_PAD), BF16)], epilogue=rope_q,
               extras=[(t, tab_extra) for t in rope_tabs], tn=2 * QK_PAD)
    (kv,) = _mm("mla_kv", ckv, w_ukv, "nn", S, H * QK_PAD, rank, [_sds((S, H * QK_PAD), BF16)])
    rest_idx = [W_UP, W_DOWN]
    o, lse, (up_buf, down_buf) = _attn_fwd_tri(
        q, kv, kr, H, scale, gather_of([casted[i] for i in rest_idx], rest_idx, [0, 0]))
    (y1,), (cout_buf,) = _mm("mla_out", o, w_o, "nn", S, D, HV, [_sds((S, D), BF16)],
                             rider=gather_of([casted[W_COUT]], [W_COUT]))
    x1, h2 = _fwd_boundary("fwd_boundary_1", x0, y1, gt1, ng[0][1], ng[0][2], sc2, sh2)
    (r2, a2), (cin_buf,) = mlp_up("0", 0, h2, view(W_UP, up_buf), gather_of([casted[W_CIN]], [W_CIN]))
    (y2,), (up_buf,) = mlp_down("0", 0, a2, view(W_DOWN, down_buf), gather_of([up_buf], [W_UP], [1]))
    w_cin, w_cout, w_up = view(W_CIN, cin_buf)[0], view(W_COUT, cout_buf)[0], view(W_UP, up_buf)

    sh1b, sc1b, gt1b, sh2b, sc2b, gt2b = mods[1]
    x2, h3 = _fwd_boundary("fwd_boundary_2", x1, y2, gt2, ng[0][3], ng[1][0], sc1b, sh1b)
    nD = lambda tn: D // tn
    (proj3,), (down_buf,) = _mm(
        "conv_in", h3, w_cin, "nn", S, 3 * D, D, [_sds((3, S, D), F32)], tn=min(1024, D),
        rider=gather_of([down_buf], [W_DOWN], [1], NEIGHBOURS),
        out_specs=[lambda tm, tn, tk: pl.BlockSpec((None, tm, tn), lambda i, j, k: (j // nD(tn), i, j % nD(tn)))])
    bz = _conv_fwd(proj3, convw_full)
    (y3,) = _mm("conv_out", bz, w_cout, "nn", S, D, D, [_sds((S, D), BF16)])
    x3, h4 = _fwd_boundary("fwd_boundary_3", x2, y3, gt1b, ng[1][1], ng[1][2], sc2b, sh2b)
    (r4, a4), (down_buf,) = mlp_up("1", 1, h4, w_up, gather_of([down_buf], [W_DOWN], [1], DIAGONAL))
    w_down = view(W_DOWN, down_buf)
    (y4,) = mlp_down("1", 1, a4, w_down)

    dx4, dy4, sums_l, loss_acc = _loss_boundary("loss_boundary", x3, y4, gt2b, ng[1][3], loss_target[0])
    loss = lax.psum(loss_acc[0, 0], ("x", "y", "c"))

    dh4, dw_up1, dw_down1, _ = mlp_bwd("1", 1, h4, r4, a4, dy4)
    items = [("mlp_w_up_1", W_UP, dw_up1), ("mlp_w_down_1", W_DOWN, dw_down1)]
    g5s = halves(items)
    dx3, dy3, sums_3, ras = _bwd_boundary("bwd_boundary_3", dx4, dh4, x3, y3, gt1b, ng[1][1], ng[1][2], sc2b,
                                          _exchange_rider(g5s))
    ps_up1, ps_down1 = pair_sums(items, g5s, ras)

    (dbz,) = _mm("conv_out_dx", dy3, w_cout, "nt", S, D, D, [_sds((S, D), BF16)])
    (dw_cout,) = _mm("conv_out_dw", bz, dy3, "tn", D, D, S, [_sds((D, D), BF16)])
    dproj3, dconvw = _conv_bwd(dbz, proj3, convw_full)
    (dh3,), (rb_up1,) = _mm(
        "conv_in_dx", dproj3, w_cin, "nt", S, D, 3 * D, [_sds((S, D), BF16)], tk=D,
        rider=scatter_of([ps_up1], [W_UP], NEIGHBOURS),
        a_spec=lambda tm, tn, tk: pl.BlockSpec((None, tm, tk), lambda i, j, k: (k // (D // tk), i, k % (D // tk))))
    (dw_cin,), (rb_up1,) = _mm(
        "conv_in_dw", h3, dproj3, "tn", D, 3 * D, S, [_sds((D, 3 * D), BF16)], tn=min(1024, D),
        rider=scatter_of([ps_up1], [W_UP], DIAGONAL, [rb_up1]),
        b_spec=lambda tm, tn, tk: pl.BlockSpec((None, tk, tn), lambda i, j, k: (j // nD(tn), k, j % nD(tn))))
    items = [("conv_w_in", W_CIN, dw_cin), ("conv_w_out", W_COUT, dw_cout)]
    g5s = halves(items)
    dx2, dy2, sums_2, ras = _bwd_boundary("bwd_boundary_2", dx3, dh3, x2, y2, gt2, ng[0][3], ng[1][0], sc1b,
                                          _exchange_rider(g5s))
    ps_cin, ps_cout = pair_sums(items, g5s, ras)

    dh2, dw_up0, dw_down0, (rb_down1,) = mlp_bwd(
        "0", 0, h2, r2, a2, dy2,
        lambda got: scatter_of([ps_down1], [W_DOWN], DIAGONAL, list(got)) if got else scatter_of([ps_down1], [W_DOWN], NEIGHBOURS))
    items = [("mlp_w_up_0", W_UP, dw_up0), ("mlp_w_down_0", W_DOWN, dw_down0)]
    g5s = halves(items)
    dx1, dy1, sums_1, ras = _bwd_boundary("bwd_boundary_1", dx2, dh2, x1, y1, gt1, ng[0][1], ng[0][2], sc2,
                                          _exchange_rider(g5s))
    ps_up0, ps_down0 = pair_sums(items, g5s, ras)

    (do,) = _mm("mla_out_dx", dy1, w_o, "nt", S, HV, D, [_sds((S, HV), BF16)])
    (dw_o,) = _mm("mla_out_dw", o, dy1, "tn", HV, D, S, [_sds((HV, D), BF16)])
    dq, dkv, dkr, (rb_up0, rb_down0, rb_cin, rb_cout) = _attn_bwd_tri(
        q, kv, kr, o, do, lse, rope_tabs, H, scale,
        scatter_of([ps_up0, ps_down0, ps_cin, ps_cout], [W_UP, W_DOWN, W_CIN, W_COUT]))
    (dcq,) = _mm("mla_q_dx", dq, w_q_p, "nt", S, rank, H * QK_PAD, [_sds((S, rank), F32)])
    (dw_q_p,) = _mm("mla_q_dw", cq, dq, "tn", rank, H * QK_PAD, S, [_sds((rank, H * QK_PAD), BF16)])
    (dckv,) = _mm("mla_kv_dx", dkv, w_ukv, "nt", S, rank, H * QK_PAD, [_sds((S, rank), F32)])
    (dw_ukv,) = _mm("mla_kv_dw", ckv, dkv, "tn", rank, H * QK_PAD, S, [_sds((rank, H * QK_PAD), BF16)])
    dlat, sums_lat = _latent_bwd(lat, dcq, dckv, dkr, mla_g_q, mla_g_kv, rope_tabs, rank)
    (dw_in_p,) = _mm("mla_in_dw", h1, dlat, "tn", D, lat_pad, S, [_sds((D, lat_pad), BF16)], tn=lat_pad)
    dw_mla = [dw_in_p[:, :lat_dim], dw_q_p.reshape(rank, H, QK_PAD)[:, :, :d_qk].reshape(rank, H * d_qk), dw_ukv, dw_o]
    items = [(weights[i][0], i, g) for i, g in zip(mla_idx, dw_mla)]
    g5s = halves(items)
    (dh1,), ras = _mm("mla_in_dx", dlat, w_in_p, "nt", S, D, lat_pad, [_sds((S, D), BF16)], rider=_exchange_rider(g5s))
    ps_mla = pair_sums(items, g5s, ras)
    grad_x, sums_0, _ = _bwd_boundary("bwd_boundary_0", dx1, dh1, x0, None, None, None, ng[0][0], sc1)

    kc_idx = jnp.stack([chip, ci]).astype(jnp.int32)
    fs_rest = [_chip_sum("chip_sum_" + weights[i][0], p, rb, kc_idx, kinds[i])
               for i, p, rb in [(W_CIN, ps_cin, rb_cin), (W_COUT, ps_cout, rb_cout)]]
    for i, (p1, r1), (p0, r0) in [(W_UP, (ps_up1, rb_up1), (ps_up0, rb_up0)), (W_DOWN, (ps_down1, rb_down1), (ps_down0, rb_down0))]:
        f = _chip_sum("chip_sum_" + weights[i][0] + "_1", p1, r1, kc_idx, kinds[i], layer=1, n_layers=2)
        fs_rest.append(_chip_sum("chip_sum_" + weights[i][0] + "_0", p0, r0, kc_idx, kinds[i], layer=0, n_layers=2, prev=f))

    dmod0 = [sums_0[0], sums_0[1], sums_1[3], sums_1[0], sums_1[1], sums_2[3]]
    dmod1 = [sums_2[0], sums_2[1], sums_3[3], sums_3[0], sums_3[1], sums_l[3]]
    dng0 = [sums_0[2], sums_1[4], sums_1[2], sums_2[4]]
    dng1 = [sums_2[2], sums_3[4], sums_3[2], sums_l[4]]
    small = _pack_rows(dmod0 + dmod1 + dng0 + dng1 + [sums_lat[0], sums_lat[1], dconvw], lane_mult=LANES)
    gathered, total, carried = _small_allgather(
        "gather_small_grads", small, with_sum=True, rider=_both_riders(scatter_of(ps_mla, mla_idx), _share_rider(fs_rest)))
    rbs_mla, finals_rest = carried[:len(mla_idx)], carried[len(mla_idx):]
    n_dm = 2 * n_mod * D
    dmod_all = gathered.reshape(8, -1)[:, :n_dm].reshape(8, 2, n_mod * D)
    total = total.reshape(-1)
    g_b_mod = total[:n_dm].reshape(2, n_mod * D)
    g_norm = lax.dynamic_slice(total[n_dm:n_dm + 8 * D].reshape(2, 4, D), (0, 0, chip * Dq), (2, 4, Dq))
    off = n_dm + 8 * D
    g_gq = total[off:off + rank].reshape(1, rank)
    g_gkv = total[off + rank:off + 2 * rank].reshape(1, rank)
    off += 2 * rank
    g_convw = lax.dynamic_slice(total[off:off + 3 * D].reshape(1, 3, D), (0, 0, chip * Dq), (1, 3, Dq))

    dmod_cols = jnp.transpose(lax.dynamic_slice(dmod_all.reshape(8, 2, N_CHIPS, ncol), (0, 0, chip, 0), (8, 2, 1, ncol))
                              .reshape(8, 2, ncol), (1, 0, 2))
    g_w_mod, d_w_mod, nm_w_mod, nv_w_mod, _ = _adamw_mod(w_mod, cond_all.T, dmod_cols, m_w_mod, v_w_mod)
    fs_mla = [_chip_sum("chip_sum_" + weights[i][0], p, rb, kc_idx, kinds[i]) for i, p, rb in zip(mla_idx, ps_mla, rbs_mla)]
    finals = list(_run_rider("grad_pair_share_mla", _share_rider(fs_mla))) + list(finals_rest)
    orig = [mla_w_in, mla_w_uq, mla_w_ukv, mla_w_o, conv_w_in, conv_w_out, mlp_w_up, mlp_w_down]
    big_grads = [f.reshape(w.shape) for f, w in zip(finals, orig)]

    names = ["b_mod", "norm_g", "mla_w_in", "mla_g_q", "mla_g_kv", "mla_w_uq", "mla_w_ukv", "mla_w_o",
             "conv_w_in", "conv_w", "conv_w_out", "mlp_w_up", "mlp_w_down"]
    ws = [b_mod, norm_g, mla_w_in, mla_g_q, mla_g_kv, mla_w_uq, mla_w_ukv, mla_w_o, conv_w_in, conv_w, conv_w_out,
          mlp_w_up, mlp_w_down]
    ms = [m_b_mod, m_norm_g, m_mla_w_in, m_mla_g_q, m_mla_g_kv, m_mla_w_uq, m_mla_w_ukv, m_mla_w_o, m_conv_w_in,
          m_conv_w, m_conv_w_out, m_mlp_w_up, m_mlp_w_down]
    vs = [v_b_mod, v_norm_g, v_mla_w_in, v_mla_g_q, v_mla_g_kv, v_mla_w_uq, v_mla_w_ukv, v_mla_w_o, v_conv_w_in,
          v_conv_w, v_conv_w_out, v_mlp_w_up, v_mlp_w_down]
    gs = [g_b_mod, g_norm, big_grads[0], g_gq, g_gkv, big_grads[1], big_grads[2], big_grads[3], big_grads[4],
          g_convw, big_grads[5], big_grads[6], big_grads[7]]
    grads, deltas, new_ms, new_vs = [g_w_mod], [d_w_mod], [nm_w_mod], [nv_w_mod]
    for nm, w, g, m, v in zip(names, ws, gs, ms, vs):
        d, nm_, nv_ = _adamw("adamw_" + nm, w, g, m, v)
        grads.append(g)
        deltas.append(d)
        new_ms.append(nm_)
        new_vs.append(nv_)
    return (loss, grad_x[None], *grads, *deltas, *new_ms, *new_vs)
```

```python
from typing import NamedTuple

import jax
import jax.numpy as jnp
from jax import lax
from jax.experimental import pallas as pl
from jax.experimental.pallas import tpu as pltpu

F32 = jnp.float32
BF16 = jnp.bfloat16
NORM_EPS = 1e-6
ROPE_THETA = 10000.0
QK_NOPE = 128
QK_ROPE = 64
V_HEAD = 128
LANES = 128
QK_PAD = QK_NOPE + LANES
ADAM_LR, ADAM_B1, ADAM_B2, ADAM_EPS, ADAM_WD, ADAM_STEP = 0.001, 0.9, 0.999, 1e-08, 0.01, 10
VMEM_LIMIT_BYTES = 56 * 1024 * 1024
N_CHIPS = 4
MESH_ID = pl.DeviceIdType.MESH
ANY = pl.BlockSpec(memory_space=pl.ANY)
NEG_INF = float("-inf")

DIMS_NN = (((1,), (0,)), ((), ()))
DIMS_NT = (((1,), (1,)), ((), ()))
DIMS_TN = (((0,), (0,)), ((), ()))


def _cparams(*sem):
    return pltpu.CompilerParams(dimension_semantics=sem, vmem_limit_bytes=VMEM_LIMIT_BYTES)


def _row_tile(rows, row_bytes, limit=2 * 1024 * 1024, mult=16):
    if rows * row_bytes <= limit or rows % mult:
        return rows
    best = mult
    t = mult
    while t <= rows:
        if rows % t == 0 and t * row_bytes <= limit:
            best = t
        t += mult
    return best


def _rms(v):
    return lax.rsqrt(jnp.mean(v * v, axis=-1, keepdims=True) + NORM_EPS)


class Rider(NamedTuple):
    operands: tuple
    out_shape: tuple
    aliases: dict
    sems: tuple
    start: object
    finish: object


NO_RIDER = Rider((), (), {}, (), None, None)


def _mm(name, a, b, mode, M, N, K, outs, *, a_spec=None, b_spec=None, out_specs=None, epilogue=None,
        extras=(), rider=NO_RIDER, tm=1024, tn=1024, tk=4096):
    tm, tn, tk = min(tm, M), min(tn, N), min(tk, K)
    assert M % tm == 0 and N % tn == 0 and K % tk == 0, (name, M, N, K)
    nk = K // tk
    if a_spec is None:
        a_spec = {"nn": pl.BlockSpec((tm, tk), lambda i, j, k: (i, k)),
                  "nt": pl.BlockSpec((tm, tk), lambda i, j, k: (i, k)),
                  "tn": pl.BlockSpec((tk, tm), lambda i, j, k: (k, i))}[mode]
    else:
        a_spec = a_spec(tm, tn, tk)
    if b_spec is None:
        b_spec = {"nn": pl.BlockSpec((tk, tn), lambda i, j, k: (k, j)),
                  "nt": pl.BlockSpec((tn, tk), lambda i, j, k: (j, k)),
                  "tn": pl.BlockSpec((tk, tn), lambda i, j, k: (k, j))}[mode]
    else:
        b_spec = b_spec(tm, tn, tk)
    if out_specs is None:
        out_specs = [pl.BlockSpec((tm, tn), lambda i, j, k: (i, j)) for _ in outs]
    else:
        out_specs = [s(tm, tn, tk) for s in out_specs]
    dims = {"nn": DIMS_NN, "nt": DIMS_NT, "tn": DIMS_TN}[mode]
    ne, no = len(extras), len(outs)
    n_ri, n_ro = len(rider.operands), len(rider.out_shape)
    grid = (M // tm, N // tn, nk)

    def body(*refs):
        a_ref, b_ref = refs[0], refs[1]
        ex = refs[2:2 + ne]
        r_in = refs[2 + ne:2 + ne + n_ri]
        o = refs[2 + ne + n_ri:2 + ne + n_ri + no]
        r_out = refs[2 + ne + n_ri + no:2 + ne + n_ri + no + n_ro]
        scratch = refs[2 + ne + n_ri + no + n_ro:]
        r_sems = scratch[1:] if nk > 1 else scratch
        ii, jj, kk = pl.program_id(0), pl.program_id(1), pl.program_id(2)

        if rider.start is not None:
            @pl.when((ii == 0) & (jj == 0) & (kk == 0))
            def _():
                rider.start(r_in, r_out, r_sems)

        part = lax.dot_general(a_ref[...].astype(BF16), b_ref[...].astype(BF16), dims,
                               preferred_element_type=F32)

        def finish(total):
            vals = epilogue(total, *[e[...] for e in ex]) if epilogue is not None else (total,)
            for r, v in zip(o, vals):
                r[...] = v.astype(r.dtype)

        if nk == 1:
            finish(part)
        else:
            acc = scratch[0]

            @pl.when(kk == 0)
            def _():
                acc[...] = part

            @pl.when(kk > 0)
            def _():
                acc[...] += part

            @pl.when(kk == nk - 1)
            def _():
                finish(acc[...])

        if rider.finish is not None:
            @pl.when((ii == grid[0] - 1) & (jj == grid[1] - 1) & (kk == nk - 1))
            def _():
                rider.finish(r_in, r_out, r_sems)

    operands = [a, b] + [e[0] for e in extras] + list(rider.operands)
    in_specs = [a_spec, b_spec] + [e[1](tm, tn, tk) for e in extras] + [ANY] * n_ri
    hosted = rider.start is not None
    res = pl.pallas_call(
        body, name=name, grid=grid,
        in_specs=in_specs, out_specs=out_specs + [ANY] * n_ro, out_shape=list(outs) + list(rider.out_shape),
        scratch_shapes=([pltpu.VMEM((tm, tn), F32)] if nk > 1 else []) + list(rider.sems),
        input_output_aliases={2 + ne + i: no + r for i, r in rider.aliases.items()},
        compiler_params=_cparams(*(("arbitrary",) * 3 if hosted else ("parallel", "parallel", "arbitrary"))),
    )(*operands)
    return (res[:no], res[no:]) if hosted else res


def _sds(shape, dtype):
    return jax.ShapeDtypeStruct(tuple(shape), dtype)


def _rope(t, cos_p, sin_lo, sin_hi):
    return t * cos_p + pltpu.roll(t, LANES - QK_ROPE // 2, 1) * sin_lo + pltpu.roll(t, QK_ROPE // 2, 1) * sin_hi


def _rope_t(d, cos_p, sin_lo, sin_hi):
    return d * cos_p + pltpu.roll(d * sin_lo, QK_ROPE // 2, 1) + pltpu.roll(d * sin_hi, LANES - QK_ROPE // 2, 1)


def _vec_spec(d):
    return pl.BlockSpec((1, d), lambda i: (0, 0))


def _fwd_boundary(name, x_prev, y, gate, ng_post, ng_pre, sc, sh):
    S, D = x_prev.shape
    ts = min(256, S)
    has_y = y is not None
    row = pl.BlockSpec((ts, D), lambda i: (i, 0))

    def body(*refs):
        if has_y:
            x_ref, y_ref, g_ref, ngp_ref, ngn_ref, sc_ref, sh_ref, xo_ref, h_ref = refs
            yv = y_ref[...].astype(F32)
            xn = x_ref[...] + g_ref[...] * (yv * _rms(yv) * ngp_ref[...])
            xo_ref[...] = xn
        else:
            x_ref, ngn_ref, sc_ref, sh_ref, h_ref = refs
            xn = x_ref[...]
        hn = xn * _rms(xn) * ngn_ref[...]
        h_ref[...] = (hn * (1.0 + sc_ref[...]) + sh_ref[...]).astype(BF16)

    vec = _vec_spec(D)
    if has_y:
        operands = (x_prev, y, gate, ng_post, ng_pre, sc, sh)
        in_specs = [row, row, vec, vec, vec, vec, vec]
        out_shape = [_sds((S, D), F32), _sds((S, D), BF16)]
        out_specs = [row, row]
    else:
        operands = (x_prev, ng_pre, sc, sh)
        in_specs = [row, vec, vec, vec]
        out_shape = [_sds((S, D), BF16)]
        out_specs = [row]
    return pl.pallas_call(body, name=name, grid=(S // ts,), in_specs=in_specs, out_specs=out_specs,
                          out_shape=out_shape, compiler_params=_cparams("parallel"))(*operands)


def _acc_rows(sums_ref, rows):
    for r, v in rows:
        sums_ref[r:r + 1, :] += jnp.sum(v, axis=0, keepdims=True)


def _post_norm_bwd(dxt, yv, gate, ng_post, sums_ref, dy_ref):
    r1 = _rms(yv)
    yhat = yv * r1
    dn = dxt * gate
    u = dn * ng_post
    dy = r1 * (u - yhat * jnp.mean(u * yhat, axis=-1, keepdims=True))
    dy_ref[...] = dy.astype(dy_ref.dtype)
    _acc_rows(sums_ref, [(3, dxt * (yhat * ng_post)), (4, dn * yhat)])


def _loss_boundary(name, x_prev, y, gate, ng_post, target):
    S, D = x_prev.shape
    ts = min(256, S)
    row = pl.BlockSpec((ts, D), lambda i: (i, 0))
    vec = _vec_spec(D)

    def body(x_ref, y_ref, g_ref, ngp_ref, t_ref, dx_ref, dy_ref, sums_ref, loss_ref):
        @pl.when(pl.program_id(0) == 0)
        def _():
            sums_ref[...] = jnp.zeros_like(sums_ref)
            loss_ref[...] = jnp.zeros_like(loss_ref)

        yv = y_ref[...].astype(F32)
        xf = x_ref[...] + g_ref[...] * (yv * _rms(yv) * ngp_ref[...])
        err = xf - t_ref[...]
        loss_ref[...] += 0.5 * jnp.sum(jnp.mean(err * err, axis=-1, keepdims=True))
        dxt = err / D
        dx_ref[...] = dxt
        _post_norm_bwd(dxt, yv, g_ref[...], ngp_ref[...], sums_ref, dy_ref)

    return pl.pallas_call(
        body, name=name, grid=(S // ts,),
        in_specs=[row, row, vec, vec, row],
        out_specs=[row, row, pl.BlockSpec((8, D), lambda i: (0, 0)), pl.BlockSpec((8, LANES), lambda i: (0, 0))],
        out_shape=[_sds((S, D), F32), _sds((S, D), BF16), _sds((8, D), F32), _sds((8, LANES), F32)],
        compiler_params=_cparams("arbitrary"))(x_prev, y, gate, ng_post, target)


def _bwd_boundary(name, dx_new, dh, x_new, y, gate, ng_post, ng_pre, sc, rider=NO_RIDER):
    S, D = x_new.shape
    ts = min(256, S)
    has_y = y is not None
    row = pl.BlockSpec((ts, D), lambda i: (i, 0))
    vec = _vec_spec(D)
    n_in, n_out = (8, 3) if has_y else (5, 2)
    n_ri, n_ro = len(rider.operands), len(rider.out_shape)

    def body(*refs):
        r_in = refs[n_in:n_in + n_ri]
        r_out = refs[n_in + n_ri + n_out:n_in + n_ri + n_out + n_ro]
        r_sems = refs[n_in + n_ri + n_out + n_ro:]
        own = refs[:n_in] + refs[n_in + n_ri:n_in + n_ri + n_out]
        if has_y:
            dxn_ref, dh_ref, x_ref, y_ref, g_ref, ngp_ref, ngn_ref, sc_ref, dxo_ref, dy_ref, sums_ref = own
        else:
            dxn_ref, dh_ref, x_ref, ngn_ref, sc_ref, dxo_ref, sums_ref = own

        @pl.when(pl.program_id(0) == 0)
        def _():
            sums_ref[...] = jnp.zeros_like(sums_ref)
            if rider.start is not None:
                rider.start(r_in, r_out, r_sems)

        xv = x_ref[...]
        dhv = dh_ref[...].astype(F32)
        ngn = ngn_ref[...]
        r2 = _rms(xv)
        xhat = xv * r2
        dn_pre = dhv * (1.0 + sc_ref[...])
        u2 = dn_pre * ngn
        dxt = dxn_ref[...] + r2 * (u2 - xhat * jnp.mean(u2 * xhat, axis=-1, keepdims=True))
        dxo_ref[...] = dxt
        _acc_rows(sums_ref, [(0, dhv), (1, dhv * (xhat * ngn)), (2, dn_pre * xhat)])
        if has_y:
            _post_norm_bwd(dxt, y_ref[...].astype(F32), g_ref[...], ngp_ref[...], sums_ref, dy_ref)

        if rider.finish is not None:
            @pl.when(pl.program_id(0) == S // ts - 1)
            def _():
                rider.finish(r_in, r_out, r_sems)

    sums_spec = pl.BlockSpec((8, D), lambda i: (0, 0))
    if has_y:
        operands = (dx_new, dh, x_new, y, gate, ng_post, ng_pre, sc)
        in_specs = [row, row, row, row, vec, vec, vec, vec]
        out_shape = [_sds((S, D), F32), _sds((S, D), BF16), _sds((8, D), F32)]
        out_specs = [row, row, sums_spec]
    else:
        operands = (dx_new, dh, x_new, ng_pre, sc)
        in_specs = [row, row, row, vec, vec]
        out_shape = [_sds((S, D), F32), _sds((8, D), F32)]
        out_specs = [row, sums_spec]
    res = pl.pallas_call(
        body, name=name, grid=(S // ts,), in_specs=in_specs + [ANY] * n_ri, out_specs=out_specs + [ANY] * n_ro,
        out_shape=out_shape + list(rider.out_shape), scratch_shapes=list(rider.sems),
        input_output_aliases={n_in + i: n_out + o for i, o in rider.aliases.items()},
        compiler_params=_cparams("arbitrary"))(*operands, *rider.operands)
    return (*res[:n_out], res[n_out:])


def _latent_fwd(lat, g_q, g_kv, rope_tabs, rank):
    S, W = lat.shape
    ts = min(256, S)
    tab = pl.BlockSpec((ts, LANES), lambda i: (i, 0))

    def body(lat_ref, gq_ref, gkv_ref, cos_ref, slo_ref, shi_ref, cq_ref, ckv_ref, kr_ref):
        lq = lat_ref[:, 0:rank]
        lkv = lat_ref[:, rank:2 * rank]
        cq_ref[...] = (lq * _rms(lq) * gq_ref[...]).astype(BF16)
        ckv_ref[...] = (lkv * _rms(lkv) * gkv_ref[...]).astype(BF16)
        kr_ref[...] = _rope(lat_ref[:, 2 * rank:W], cos_ref[...], slo_ref[...], shi_ref[...]).astype(BF16)

    return pl.pallas_call(
        body, name="mla_latent_fwd", grid=(S // ts,),
        in_specs=[pl.BlockSpec((ts, W), lambda i: (i, 0)), _vec_spec(rank), _vec_spec(rank), tab, tab, tab],
        out_specs=[pl.BlockSpec((ts, rank), lambda i: (i, 0)), pl.BlockSpec((ts, rank), lambda i: (i, 0)), tab],
        out_shape=[_sds((S, rank), BF16), _sds((S, rank), BF16), _sds((S, LANES), BF16)],
        compiler_params=_cparams("parallel"))(lat, g_q, g_kv, *rope_tabs)


def _latent_bwd(lat, dcq, dckv, dkr, g_q, g_kv, rope_tabs, rank):
    S, W = lat.shape
    ts = min(256, S)
    tab = pl.BlockSpec((ts, LANES), lambda i: (i, 0))
    half = pl.BlockSpec((ts, rank), lambda i: (i, 0))

    def body(lat_ref, dcq_ref, dckv_ref, dkr_ref, gq_ref, gkv_ref, cos_ref, slo_ref, shi_ref, dlat_ref, sums_ref):
        @pl.when(pl.program_id(0) == 0)
        def _():
            sums_ref[...] = jnp.zeros_like(sums_ref)

        def norm_bwd(v, dn, g, r):
            rr = _rms(v)
            vhat = v * rr
            u = dn * g
            sums_ref[r:r + 1, :] += jnp.sum(dn * vhat, axis=0, keepdims=True)
            return rr * (u - vhat * jnp.mean(u * vhat, axis=-1, keepdims=True))

        dlat_ref[:, 0:rank] = norm_bwd(lat_ref[:, 0:rank], dcq_ref[...], gq_ref[...], 0).astype(BF16)
        dlat_ref[:, rank:2 * rank] = norm_bwd(lat_ref[:, rank:2 * rank], dckv_ref[...], gkv_ref[...], 1).astype(BF16)
        dlat_ref[:, 2 * rank:W] = _rope_t(dkr_ref[...], cos_ref[...], slo_ref[...], shi_ref[...]).astype(BF16)

    return pl.pallas_call(
        body, name="mla_latent_bwd", grid=(S // ts,),
        in_specs=[pl.BlockSpec((ts, W), lambda i: (i, 0)), half, half, tab, _vec_spec(rank), _vec_spec(rank),
                  tab, tab, tab],
        out_specs=[pl.BlockSpec((ts, W), lambda i: (i, 0)), pl.BlockSpec((8, rank), lambda i: (0, 0))],
        out_shape=[_sds((S, W), BF16), _sds((8, rank), F32)],
        compiler_params=_cparams("arbitrary"))(lat, dcq, dckv, dkr, g_q, g_kv, *rope_tabs)


def _attn_tiles(S):
    t = min(512, S)
    return t, S // t


def _causal_mask(t):
    return lax.broadcasted_iota(jnp.int32, (t, t), 1) <= lax.broadcasted_iota(jnp.int32, (t, t), 0)


def _attn_fwd(q, kv, kr, heads, scale, rider=NO_RIDER):
    S = q.shape[0]
    t, nb = _attn_tiles(S)
    G = 2 if heads % 2 == 0 else 1
    n_ri, n_ro = len(rider.operands), len(rider.out_shape)

    def body(*refs):
        q_ref, kv_ref, kr_ref = refs[:3]
        r_in = refs[3:3 + n_ri]
        o_ref, lse_ref = refs[3 + n_ri:5 + n_ri]
        r_out = refs[5 + n_ri:5 + n_ri + n_ro]
        m_scr, acc_scr = refs[5 + n_ri + n_ro:7 + n_ri + n_ro]
        r_sems = refs[7 + n_ri + n_ro:]
        h, qi, ki = pl.program_id(0), pl.program_id(1), pl.program_id(2)

        if rider.start is not None:
            @pl.when((h == 0) & (qi == 0) & (ki == 0))
            def _():
                rider.start(r_in, r_out, r_sems)

        @pl.when(ki == 0)
        def _():
            m_scr[...] = jnp.full_like(m_scr, NEG_INF)
            acc_scr[...] = jnp.zeros_like(acc_scr)

        def step(diagonal):
            ones = jnp.ones((t, LANES), BF16)
            for g in range(G):
                kcat = jnp.concatenate([kv_ref[:, g * QK_PAD:g * QK_PAD + QK_NOPE], kr_ref[...]], axis=1)
                vext = jnp.concatenate([kv_ref[:, g * QK_PAD + QK_NOPE:(g + 1) * QK_PAD], ones], axis=1)
                s = lax.dot_general(q_ref[:, g * QK_PAD:(g + 1) * QK_PAD], kcat, DIMS_NT,
                                    preferred_element_type=F32) * scale
                if diagonal:
                    s = jnp.where(_causal_mask(t), s, NEG_INF)
                m_prev = m_scr[g]
                m_new = jnp.maximum(m_prev, jnp.max(s, axis=-1, keepdims=True))
                alpha = jnp.exp(m_prev - m_new)
                p = jnp.exp(s - jnp.tile(m_new, (1, t // LANES)))
                acc_scr[g] = jnp.tile(alpha, (1, 2)) * acc_scr[g] + lax.dot_general(
                    p.astype(BF16), vext, DIMS_NN, preferred_element_type=F32)
                m_scr[g] = m_new

        @pl.when(ki < qi)
        def _():
            step(False)

        @pl.when(ki == qi)
        def _():
            step(True)

        @pl.when(ki == nb - 1)
        def _():
            for g in range(G):
                acc = acc_scr[g]
                o_ref[:, g * V_HEAD:(g + 1) * V_HEAD] = (acc[:, 0:V_HEAD] / acc[:, V_HEAD:2 * V_HEAD]).astype(BF16)
                lse_ref[g] = m_scr[g] + jnp.log(acc[:, V_HEAD:2 * V_HEAD])

        if rider.finish is not None:
            @pl.when((h == heads // G - 1) & (qi == nb - 1) & (ki == nb - 1))
            def _():
                rider.finish(r_in, r_out, r_sems)

    res = pl.pallas_call(
        body, name="mla_attn_fwd", grid=(heads // G, nb, nb),
        in_specs=[pl.BlockSpec((t, G * QK_PAD), lambda h, qi, ki: (qi, h)),
                  pl.BlockSpec((t, G * QK_PAD), lambda h, qi, ki: (jnp.minimum(ki, qi), h)),
                  pl.BlockSpec((t, LANES), lambda h, qi, ki: (jnp.minimum(ki, qi), 0))] + [ANY] * n_ri,
        out_specs=[pl.BlockSpec((t, G * V_HEAD), lambda h, qi, ki: (qi, h)),
                   pl.BlockSpec((G, t, LANES), lambda h, qi, ki: (h, qi, 0))] + [ANY] * n_ro,
        out_shape=[_sds((S, heads * V_HEAD), BF16), _sds((heads, S, LANES), F32)] + list(rider.out_shape),
        scratch_shapes=[pltpu.VMEM((G, t, LANES), F32), pltpu.VMEM((G, t, 2 * V_HEAD), F32)] + list(rider.sems),
        input_output_aliases={3 + i: 2 + o for i, o in rider.aliases.items()},
        compiler_params=_cparams("arbitrary", "arbitrary", "arbitrary"))(q, kv, kr, *rider.operands)
    return res[0], res[1], res[2:]


def _attn_delta(o, do, heads):
    S = o.shape[0]
    t, nb = _attn_tiles(S)

    def body(o_ref, do_ref, out_ref):
        d = jnp.sum(do_ref[...].astype(F32) * o_ref[...].astype(F32), axis=-1, keepdims=True)
        out_ref[...] = jnp.broadcast_to(d, (t, LANES))

    blk = pl.BlockSpec((t, V_HEAD), lambda h, i: (i, h))
    return pl.pallas_call(
        body, name="mla_attn_delta", grid=(heads, nb), in_specs=[blk, blk],
        out_specs=pl.BlockSpec((None, t, LANES), lambda h, i: (h, i, 0)),
        out_shape=_sds((heads, S, LANES), F32), compiler_params=_cparams("parallel", "parallel"))(o, do)


def _attn_bwd(q, kv, kr, delta, do, lse, rope_tabs, heads, scale, rider=NO_RIDER):
    S = q.shape[0]
    t, nb = _attn_tiles(S)
    n_ri, n_ro = len(rider.operands), len(rider.out_shape)
    rep = t // LANES

    def body(*refs):
        q_ref, kv_ref, kr_ref, delta_ref, do_ref, lse_ref, cos_ref, slo_ref, shi_ref = refs[:9]
        r_in = refs[9:9 + n_ri]
        dq_ref, dkv_ref, dkr_ref = refs[9 + n_ri:12 + n_ri]
        r_out = refs[12 + n_ri:12 + n_ri + n_ro]
        dq_scr, dk_scr, dv_scr, dkr_scr = refs[12 + n_ri + n_ro:16 + n_ri + n_ro]
        r_sems = refs[16 + n_ri + n_ro:]
        h, ki, qi = pl.program_id(0), pl.program_id(1), pl.program_id(2)
        q_rows = pl.ds(pl.multiple_of(qi * t, t), t)
        k_rows = pl.ds(pl.multiple_of(ki * t, t), t)

        if rider.start is not None:
            @pl.when((h == 0) & (ki == 0) & (qi == 0))
            def _():
                rider.start(r_in, r_out, r_sems)

        @pl.when((ki == 0) & (qi == 0))
        def _():
            dq_scr[...] = jnp.zeros_like(dq_scr)

        @pl.when((h == 0) & (ki == 0) & (qi == 0))
        def _():
            dkr_scr[...] = jnp.zeros_like(dkr_scr)

        @pl.when(qi == 0)
        def _():
            dk_scr[...] = jnp.zeros_like(dk_scr)
            dv_scr[...] = jnp.zeros_like(dv_scr)

        def step(diagonal):
            qv = q_ref[...]
            kcat = jnp.concatenate([kv_ref[:, 0:QK_NOPE], kr_ref[...]], axis=1)
            s = lax.dot_general(qv, kcat, DIMS_NT, preferred_element_type=F32) * scale
            p = jnp.exp(s - jnp.tile(lse_ref[...], (1, rep)))
            if diagonal:
                p = jnp.where(_causal_mask(t), p, 0.0)
            dov = do_ref[...]
            dv_scr[...] += lax.dot_general(p.astype(BF16), dov, DIMS_TN, preferred_element_type=F32)
            dp = lax.dot_general(dov, kv_ref[:, QK_NOPE:QK_NOPE + V_HEAD], DIMS_NT, preferred_element_type=F32)
            ds = (p * (dp - jnp.tile(delta_ref[...], (1, rep))) * scale).astype(BF16)
            dk_scr[...] += lax.dot_general(ds, qv, DIMS_TN, preferred_element_type=F32)
            dq_scr[q_rows, :] += lax.dot_general(ds, kcat, DIMS_NN, preferred_element_type=F32)

        @pl.when(qi > ki)
        def _():
            step(False)

        @pl.when(qi == ki)
        def _():
            step(True)

        @pl.when(qi == nb - 1)
        def _():
            dkv_ref[...] = jnp.concatenate([dk_scr[:, 0:QK_NOPE], dv_scr[...]], axis=1).astype(BF16)
            dkr_scr[k_rows, :] += dk_scr[:, QK_NOPE:QK_PAD]

        @pl.when(ki == nb - 1)
        def _():
            dqv = dq_scr[q_rows, :]
            dq_ref[q_rows, :] = jnp.concatenate(
                [dqv[:, 0:QK_NOPE], _rope_t(dqv[:, QK_NOPE:QK_PAD], cos_ref[...], slo_ref[...], shi_ref[...])],
                axis=1).astype(BF16)

        @pl.when((h == heads - 1) & (ki == nb - 1) & (qi == nb - 1))
        def _():
            dkr_ref[...] = dkr_scr[...]
            if rider.finish is not None:
                rider.finish(r_in, r_out, r_sems)

    qmap = lambda h, ki, qi: (jnp.maximum(qi, ki), h)
    stat = pl.BlockSpec((None, t, LANES), lambda h, ki, qi: (h, jnp.maximum(qi, ki), 0))
    tab = pl.BlockSpec((t, LANES), lambda h, ki, qi: (qi, 0))
    res = pl.pallas_call(
        body, name="mla_attn_bwd", grid=(heads, nb, nb),
        in_specs=[pl.BlockSpec((t, QK_PAD), qmap),
                  pl.BlockSpec((t, QK_PAD), lambda h, ki, qi: (ki, h)),
                  pl.BlockSpec((t, LANES), lambda h, ki, qi: (ki, 0)),
                  stat,
                  pl.BlockSpec((t, V_HEAD), qmap),
                  stat,
                  tab, tab, tab] + [ANY] * n_ri,
        out_specs=[pl.BlockSpec((S, QK_PAD), lambda h, ki, qi: (0, h)),
                   pl.BlockSpec((t, QK_PAD), lambda h, ki, qi: (ki, h)),
                   pl.BlockSpec((S, LANES), lambda h, ki, qi: (0, 0))] + [ANY] * n_ro,
        out_shape=[_sds((S, heads * QK_PAD), BF16), _sds((S, heads * QK_PAD), BF16), _sds((S, LANES), F32)]
        + list(rider.out_shape),
        scratch_shapes=[pltpu.VMEM((S, QK_PAD), F32), pltpu.VMEM((t, QK_PAD), F32), pltpu.VMEM((t, V_HEAD), F32),
                        pltpu.VMEM((S, LANES), F32)] + list(rider.sems),
        input_output_aliases={9 + i: 3 + o for i, o in rider.aliases.items()},
        compiler_params=_cparams("arbitrary", "arbitrary", "arbitrary"))(q, kv, kr, delta, do, lse, *rope_tabs, *rider.operands)
    return res[0], res[1], res[2], res[3:]


def _causal_pairs(nb, q_major):
    if q_major:
        pairs = [(qi, ki) for qi in range(nb) for ki in range(qi + 1)]
    else:
        pairs = [(qi, ki) for ki in range(nb) for qi in range(ki, nb)]
    return jnp.array([p[0] for p in pairs], jnp.int32), jnp.array([p[1] for p in pairs], jnp.int32), len(pairs)


def _heads_per_step(heads):
    return 2 if heads % 2 == 0 else 1


def _attn_fwd_tri(q, kv, kr, heads, scale, rider=NO_RIDER):
    S = q.shape[0]
    t, nb = _attn_tiles(S)
    G = _heads_per_step(heads)
    q_tab, k_tab, n_pairs = _causal_pairs(nb, True)
    n_ri, n_ro = len(rider.operands), len(rider.out_shape)

    def body(qt_ref, kt_ref, *refs):
        q_ref, kv_ref, kr_ref = refs[:3]
        r_in = refs[3:3 + n_ri]
        o_ref, lse_ref = refs[3 + n_ri:5 + n_ri]
        r_out = refs[5 + n_ri:5 + n_ri + n_ro]
        m_scr, acc_scr = refs[5 + n_ri + n_ro:7 + n_ri + n_ro]
        r_sems = refs[7 + n_ri + n_ro:]
        h, p = pl.program_id(0), pl.program_id(1)
        qi, ki = qt_ref[p], kt_ref[p]

        if rider.start is not None:
            @pl.when((h == 0) & (p == 0))
            def _():
                rider.start(r_in, r_out, r_sems)

        @pl.when(ki == 0)
        def _():
            m_scr[...] = jnp.full_like(m_scr, NEG_INF)
            acc_scr[...] = jnp.zeros_like(acc_scr)

        def step(diagonal):
            ones = jnp.ones((t, LANES), BF16)
            for g in range(G):
                kcat = jnp.concatenate([kv_ref[:, g * QK_PAD:g * QK_PAD + QK_NOPE], kr_ref[...]], axis=1)
                vext = jnp.concatenate([kv_ref[:, g * QK_PAD + QK_NOPE:(g + 1) * QK_PAD], ones], axis=1)
                s = lax.dot_general(q_ref[:, g * QK_PAD:(g + 1) * QK_PAD], kcat, DIMS_NT,
                                    preferred_element_type=F32) * scale
                if diagonal:
                    s = jnp.where(_causal_mask(t), s, NEG_INF)
                m_prev = m_scr[g]
                m_new = jnp.maximum(m_prev, jnp.max(s, axis=-1, keepdims=True))
                alpha = jnp.exp(m_prev - m_new)
                pr = jnp.exp(s - jnp.tile(m_new, (1, t // LANES)))
                acc_scr[g] = jnp.tile(alpha, (1, 2)) * acc_scr[g] + lax.dot_general(
                    pr.astype(BF16), vext, DIMS_NN, preferred_element_type=F32)
                m_scr[g] = m_new

        @pl.when(ki < qi)
        def _():
            step(False)

        @pl.when(ki == qi)
        def _():
            step(True)
            for g in range(G):
                acc = acc_scr[g]
                o_ref[:, g * V_HEAD:(g + 1) * V_HEAD] = (acc[:, 0:V_HEAD] / acc[:, V_HEAD:2 * V_HEAD]).astype(BF16)
                lse_ref[g] = m_scr[g] + jnp.log(acc[:, V_HEAD:2 * V_HEAD])

        if rider.finish is not None:
            @pl.when((h == heads // G - 1) & (p == n_pairs - 1))
            def _():
                rider.finish(r_in, r_out, r_sems)

    res = pl.pallas_call(
        body, name="mla_attn_fwd",
        grid_spec=pltpu.PrefetchScalarGridSpec(
            num_scalar_prefetch=2, grid=(heads // G, n_pairs),
            in_specs=[pl.BlockSpec((t, G * QK_PAD), lambda h, p, qt, kt: (qt[p], h)),
                      pl.BlockSpec((t, G * QK_PAD), lambda h, p, qt, kt: (kt[p], h)),
                      pl.BlockSpec((t, LANES), lambda h, p, qt, kt: (kt[p], 0))] + [ANY] * n_ri,
            out_specs=[pl.BlockSpec((t, G * V_HEAD), lambda h, p, qt, kt: (qt[p], h)),
                       pl.BlockSpec((G, t, LANES), lambda h, p, qt, kt: (h, qt[p], 0))] + [ANY] * n_ro,
            scratch_shapes=[pltpu.VMEM((G, t, LANES), F32), pltpu.VMEM((G, t, 2 * V_HEAD), F32)] + list(rider.sems)),
        out_shape=[_sds((S, heads * V_HEAD), BF16), _sds((heads, S, LANES), F32)] + list(rider.out_shape),
        input_output_aliases={5 + i: 2 + o for i, o in rider.aliases.items()},
        compiler_params=_cparams("arbitrary", "arbitrary"))(q_tab, k_tab, q, kv, kr, *rider.operands)
    return res[0], res[1], res[2:]


def _attn_bwd_tri(q, kv, kr, o, do, lse, rope_tabs, heads, scale, rider=NO_RIDER):
    S = q.shape[0]
    t, nb = _attn_tiles(S)
    G = _heads_per_step(heads)
    q_tab, k_tab, n_pairs = _causal_pairs(nb, False)
    n_ri, n_ro = len(rider.operands), len(rider.out_shape)
    rep = t // LANES

    def body(qt_ref, kt_ref, *refs):
        q_ref, kv_ref, kr_ref, o_ref, do_ref, lse_ref, cos_ref, slo_ref, shi_ref = refs[:9]
        r_in = refs[9:9 + n_ri]
        dq_ref, dkv_ref, dkr_ref = refs[9 + n_ri:12 + n_ri]
        r_out = refs[12 + n_ri:12 + n_ri + n_ro]
        dq_scr, dk_scr, dv_scr, dkr_scr, delta_scr = refs[12 + n_ri + n_ro:17 + n_ri + n_ro]
        r_sems = refs[17 + n_ri + n_ro:]
        h, p = pl.program_id(0), pl.program_id(1)
        qi, ki = qt_ref[p], kt_ref[p]
        q_rows = pl.ds(pl.multiple_of(qi * t, t), t)
        k_rows = pl.ds(pl.multiple_of(ki * t, t), t)

        @pl.when(ki == 0)
        def _():
            for g in range(G):
                cols = slice(g * V_HEAD, (g + 1) * V_HEAD)
                d = jnp.sum(do_ref[:, cols].astype(F32) * o_ref[:, cols].astype(F32), axis=-1, keepdims=True)
                delta_scr[g, q_rows, :] = jnp.broadcast_to(d, (t, LANES))

        if rider.start is not None:
            @pl.when((h == 0) & (p == 0))
            def _():
                rider.start(r_in, r_out, r_sems)

        @pl.when(p == 0)
        def _():
            dq_scr[...] = jnp.zeros_like(dq_scr)

        @pl.when((h == 0) & (p == 0))
        def _():
            dkr_scr[...] = jnp.zeros_like(dkr_scr)

        @pl.when(qi == ki)
        def _():
            dk_scr[...] = jnp.zeros_like(dk_scr)
            dv_scr[...] = jnp.zeros_like(dv_scr)

        def step(diagonal):
            for g in range(G):
                qv = q_ref[:, g * QK_PAD:(g + 1) * QK_PAD]
                kcat = jnp.concatenate([kv_ref[:, g * QK_PAD:g * QK_PAD + QK_NOPE], kr_ref[...]], axis=1)
                s = lax.dot_general(qv, kcat, DIMS_NT, preferred_element_type=F32) * scale
                pr = jnp.exp(s - jnp.tile(lse_ref[g], (1, rep)))
                if diagonal:
                    pr = jnp.where(_causal_mask(t), pr, 0.0)
                dov = do_ref[:, g * V_HEAD:(g + 1) * V_HEAD]
                dv_scr[g] += lax.dot_general(pr.astype(BF16), dov, DIMS_TN, preferred_element_type=F32)
                dp = lax.dot_general(dov, kv_ref[:, g * QK_PAD + QK_NOPE:(g + 1) * QK_PAD], DIMS_NT,
                                     preferred_element_type=F32)
                ds = (pr * (dp - jnp.tile(delta_scr[g, q_rows, :], (1, rep))) * scale).astype(BF16)
                dk_scr[g] += lax.dot_general(ds, qv, DIMS_TN, preferred_element_type=F32)
                dq_scr[q_rows, g * QK_PAD:(g + 1) * QK_PAD] += lax.dot_general(ds, kcat, DIMS_NN,
                                                                               preferred_element_type=F32)

        @pl.when(qi > ki)
        def _():
            step(False)

        @pl.when(qi == ki)
        def _():
            step(True)
            for g in range(G):
                dqv = dq_scr[q_rows, g * QK_PAD:(g + 1) * QK_PAD]
                dq_ref[q_rows, g * QK_PAD:(g + 1) * QK_PAD] = jnp.concatenate(
                    [dqv[:, 0:QK_NOPE], _rope_t(dqv[:, QK_NOPE:QK_PAD], cos_ref[...], slo_ref[...], shi_ref[...])],
                    axis=1).astype(BF16)

        @pl.when(qi == nb - 1)
        def _():
            for g in range(G):
                dkv_ref[:, g * QK_PAD:(g + 1) * QK_PAD] = jnp.concatenate(
                    [dk_scr[g][:, 0:QK_NOPE], dv_scr[g]], axis=1).astype(BF16)
                dkr_scr[k_rows, :] += dk_scr[g][:, QK_NOPE:QK_PAD]

        @pl.when((h == heads // G - 1) & (p == n_pairs - 1))
        def _():
            dkr_ref[...] = dkr_scr[...]
            if rider.finish is not None:
                rider.finish(r_in, r_out, r_sems)

    q_blk = lambda w: pl.BlockSpec((t, G * w), lambda h, p, qt, kt: (qt[p], h))
    stat = pl.BlockSpec((G, t, LANES), lambda h, p, qt, kt: (h, qt[p], 0))
    tab = pl.BlockSpec((t, LANES), lambda h, p, qt, kt: (kt[p], 0))
    res = pl.pallas_call(
        body, name="mla_attn_bwd",
        grid_spec=pltpu.PrefetchScalarGridSpec(
            num_scalar_prefetch=2, grid=(heads // G, n_pairs),
            in_specs=[q_blk(QK_PAD),
                      pl.BlockSpec((t, G * QK_PAD), lambda h, p, qt, kt: (kt[p], h)),
                      tab, q_blk(V_HEAD), q_blk(V_HEAD), stat, tab, tab, tab] + [ANY] * n_ri,
            out_specs=[pl.BlockSpec((S, G * QK_PAD), lambda h, p, qt, kt: (0, h)),
                       pl.BlockSpec((t, G * QK_PAD), lambda h, p, qt, kt: (kt[p], h)),
                       pl.BlockSpec((S, LANES), lambda h, p, qt, kt: (0, 0))] + [ANY] * n_ro,
            scratch_shapes=[pltpu.VMEM((S, G * QK_PAD), F32), pltpu.VMEM((G, t, QK_PAD), F32),
                            pltpu.VMEM((G, t, V_HEAD), F32), pltpu.VMEM((S, LANES), F32),
                            pltpu.VMEM((G, S, LANES), F32)] + list(rider.sems)),
        out_shape=[_sds((S, heads * QK_PAD), BF16), _sds((S, heads * QK_PAD), BF16), _sds((S, LANES), F32)]
        + list(rider.out_shape),
        input_output_aliases={11 + i: 3 + o for i, o in rider.aliases.items()},
        compiler_params=_cparams("arbitrary", "arbitrary"))(q_tab, k_tab, q, kv, kr, o, do, lse, *rope_tabs,
                                                            *rider.operands)
    return res[0], res[1], res[2], res[3:]


def _shift_down(z, n, rows):
    return jnp.where(rows >= n, pltpu.roll(z, n, 0), 0.0)


def _shift_up(z, n, rows, S):
    return jnp.where(rows < S - n, pltpu.roll(z, S - n, 0), 0.0)


def _conv_specs(S, tc):
    strip = lambda p: pl.BlockSpec((None, S, tc), lambda j: (p, 0, j))
    return strip(0), strip(1), strip(2), pl.BlockSpec((3, tc), lambda j: (0, j))


def _conv_fwd(proj3, w):
    _, S, D = proj3.shape
    tc = LANES

    def body(b_ref, c_ref, u_ref, w_ref, out_ref):
        z = c_ref[...].astype(F32) * u_ref[...].astype(F32)
        rows = lax.broadcasted_iota(jnp.int32, (S, tc), 0)
        zc = w_ref[0:1, :] * _shift_down(z, 2, rows) + w_ref[1:2, :] * _shift_down(z, 1, rows) + w_ref[2:3, :] * z
        out_ref[...] = (b_ref[...].astype(F32) * zc).astype(BF16)

    return pl.pallas_call(
        body, name="conv_fwd", grid=(D // tc,), in_specs=list(_conv_specs(S, tc)),
        out_specs=pl.BlockSpec((S, tc), lambda j: (0, j)), out_shape=_sds((S, D), BF16),
        compiler_params=_cparams("parallel"))(proj3, proj3, proj3, w)


def _conv_bwd(dbz, proj3, w):
    _, S, D = proj3.shape
    tc = LANES

    def body(d_ref, b_ref, c_ref, u_ref, w_ref, dp_ref, dw_ref):
        cv, uv, dv = c_ref[...].astype(F32), u_ref[...].astype(F32), d_ref[...].astype(F32)
        z = cv * uv
        rows = lax.broadcasted_iota(jnp.int32, (S, tc), 0)
        z1, z2 = _shift_down(z, 1, rows), _shift_down(z, 2, rows)
        zc = w_ref[0:1, :] * z2 + w_ref[1:2, :] * z1 + w_ref[2:3, :] * z
        dp_ref[0] = (dv * zc).astype(BF16)
        dzc = dv * b_ref[...].astype(F32)
        dz = w_ref[2:3, :] * dzc + w_ref[1:2, :] * _shift_up(dzc, 1, rows, S) + w_ref[0:1, :] * _shift_up(dzc, 2, rows, S)
        dp_ref[1] = (dz * uv).astype(BF16)
        dp_ref[2] = (dz * cv).astype(BF16)
        dw_ref[0:1, :] = jnp.sum(dzc * z2, axis=0, keepdims=True)
        dw_ref[1:2, :] = jnp.sum(dzc * z1, axis=0, keepdims=True)
        dw_ref[2:3, :] = jnp.sum(dzc * z, axis=0, keepdims=True)

    sb, sc_, su, sw = _conv_specs(S, tc)
    return pl.pallas_call(
        body, name="conv_bwd", grid=(D // tc,),
        in_specs=[pl.BlockSpec((S, tc), lambda j: (0, j)), sb, sc_, su, sw],
        out_specs=[pl.BlockSpec((3, S, tc), lambda j: (0, 0, j)), pl.BlockSpec((3, tc), lambda j: (0, j))],
        out_shape=[_sds((3, S, D), BF16), _sds((3, D), F32)],
        compiler_params=_cparams("parallel"))(dbz, proj3, proj3, proj3, w)


def _silu(c_all):
    def body(c_ref, o_ref):
        cv = c_ref[...]
        o_ref[...] = cv * (1.0 / (1.0 + jnp.exp(-cv)))

    vm = pl.BlockSpec(memory_space=pltpu.VMEM)
    return pl.pallas_call(body, name="cond_silu", in_specs=[vm], out_specs=vm, out_shape=_sds(c_all.shape, F32))(c_all)


def _mod_fwd(cond, w_mod, b_cols):
    L, D, ncol = w_mod.shape
    B = cond.shape[0]
    tk, tn = min(512, D), min(1024, ncol)
    nk = D // tk

    def body(c_ref, w_ref, b_ref, out_ref, acc):
        kk = pl.program_id(2)
        part = lax.dot_general(c_ref[...].astype(BF16), w_ref[...].astype(BF16), DIMS_NN, preferred_element_type=F32)

        @pl.when(kk == 0)
        def _():
            acc[...] = part

        @pl.when(kk > 0)
        def _():
            acc[...] += part

        @pl.when(kk == nk - 1)
        def _():
            out_ref[...] = acc[...] + b_ref[...]

    return pl.pallas_call(
        body, name="mod_fwd", grid=(L, ncol // tn, nk),
        in_specs=[pl.BlockSpec((B, tk), lambda l, j, k: (0, k)),
                  pl.BlockSpec((None, tk, tn), lambda l, j, k: (l, k, j)),
                  pl.BlockSpec((None, 1, tn), lambda l, j, k: (l, 0, j))],
        out_specs=pl.BlockSpec((None, B, tn), lambda l, j, k: (l, 0, j)),
        out_shape=_sds((L, B, ncol), F32),
        scratch_shapes=[pltpu.VMEM((B, tn), F32)],
        compiler_params=_cparams("parallel", "parallel", "arbitrary"))(cond, w_mod, b_cols)


def _adamw_math(w, g, m, v):
    m = ADAM_B1 * m + (1.0 - ADAM_B1) * g
    v = ADAM_B2 * v + (1.0 - ADAM_B2) * (g * g)
    m_hat = m / (1.0 - ADAM_B1 ** ADAM_STEP)
    v_hat = v / (1.0 - ADAM_B2 ** ADAM_STEP)
    delta = -ADAM_LR * (m_hat / (jnp.sqrt(v_hat) + ADAM_EPS) + ADAM_WD * w)
    return delta, m, v


def _adamw(name, w, g, m, v):
    shape = w.shape
    cols = shape[-1] if w.ndim <= 3 else shape[-2] * shape[-1]
    rows = w.size // cols
    w2, g2, m2, v2 = (t.reshape(rows, cols) for t in (w, g, m, v))
    tr = _row_tile(rows, cols * 4, limit=1024 * 1024, mult=8)
    spec = pl.BlockSpec((tr, cols), lambda i: (i, 0))

    def body(w_ref, g_ref, m_ref, v_ref, d_ref, nm_ref, nv_ref):
        d, nm, nv = _adamw_math(w_ref[...], g_ref[...], m_ref[...], v_ref[...])
        d_ref[...] = d
        nm_ref[...] = nm
        nv_ref[...] = nv

    outs = pl.pallas_call(body, name=name, grid=(rows // tr,), in_specs=[spec] * 4, out_specs=[spec] * 3,
                          out_shape=[_sds((rows, cols), F32)] * 3, compiler_params=_cparams("parallel"))(w2, g2, m2, v2)
    return tuple(t.reshape(shape) for t in outs)


def _adamw_mod(w, cond_t, dmod_cols, m, v, rider=NO_RIDER):
    L, D, ncol = w.shape
    B = cond_t.shape[1]
    tr, tc = min(256, D), min(1024, ncol)
    blk = pl.BlockSpec((None, tr, tc), lambda l, i, j: (l, i, j))
    grid = (L, D // tr, ncol // tc)
    n_ri, n_ro = len(rider.operands), len(rider.out_shape)

    def body(*refs):
        w_ref, ct_ref, dm_ref, m_ref, v_ref = refs[:5]
        r_in = refs[5:5 + n_ri]
        g_ref, d_ref, nm_ref, nv_ref = refs[5 + n_ri:9 + n_ri]
        r_out = refs[9 + n_ri:9 + n_ri + n_ro]
        r_sems = refs[9 + n_ri + n_ro:]
        ids = [pl.program_id(a) for a in range(3)]

        if rider.start is not None:
            @pl.when((ids[0] == 0) & (ids[1] == 0) & (ids[2] == 0))
            def _():
                rider.start(r_in, r_out, r_sems)

        g = lax.dot_general(ct_ref[...], dm_ref[...], DIMS_NN, precision=lax.Precision.HIGHEST,
                            preferred_element_type=F32)
        d, nm, nv = _adamw_math(w_ref[...], g, m_ref[...], v_ref[...])
        g_ref[...] = g
        d_ref[...] = d
        nm_ref[...] = nm
        nv_ref[...] = nv

        if rider.finish is not None:
            @pl.when((ids[0] == grid[0] - 1) & (ids[1] == grid[1] - 1) & (ids[2] == grid[2] - 1))
            def _():
                rider.finish(r_in, r_out, r_sems)

    hosted = rider.start is not None
    res = pl.pallas_call(
        body, name="adamw_w_mod", grid=grid,
        in_specs=[blk, pl.BlockSpec((tr, B), lambda l, i, j: (i, 0)),
                  pl.BlockSpec((None, B, tc), lambda l, i, j: (l, 0, j)), blk, blk] + [ANY] * n_ri,
        out_specs=[blk] * 4 + [ANY] * n_ro, out_shape=[_sds((L, D, ncol), F32)] * 4 + list(rider.out_shape),
        scratch_shapes=list(rider.sems), input_output_aliases={5 + i: 4 + o for i, o in rider.aliases.items()},
        compiler_params=_cparams(*(("arbitrary",) * 3 if hosted else ("parallel",) * 3)))(
            w, cond_t, dmod_cols, m, v, *rider.operands)
    return (*res[:4], res[4:])


def _cast_into_full(name, ws, kinds, k_idx, rider=NO_RIDER):
    L, R, C = ws[0].shape
    assert all(w.shape == (L, R, C) for w in ws)
    n = len(ws)
    Rh = R // 2
    tr = _row_tile(Rh, C * 4)
    grid = (L, 2, Rh // tr)
    out_shape, out_specs = [], []
    for kind in kinds:
        if kind == "row":
            out_shape.append(_sds((L, N_CHIPS, 2, Rh, C), BF16))
            out_specs.append(pl.BlockSpec((None, None, None, tr, C), lambda l, h, i, k_ref: (l, k_ref[0], h, i, 0)))
        else:
            out_shape.append(_sds((L, 2, Rh, N_CHIPS * C), BF16))
            out_specs.append(pl.BlockSpec((None, None, tr, C), lambda l, h, i, k_ref: (l, h, i, k_ref[0])))
    n_ri, n_ro = len(rider.operands), len(rider.out_shape)

    def body(k_ref, *refs):
        r_in = refs[n:n + n_ri]
        r_out = refs[2 * n + n_ri:2 * n + n_ri + n_ro]
        r_sems = refs[2 * n + n_ri + n_ro:]
        ids = [pl.program_id(a) for a in range(3)]
        if rider.start is not None:
            @pl.when((ids[0] == 0) & (ids[1] == 0) & (ids[2] == 0))
            def _():
                rider.start(r_in, r_out, r_sems)
        for a in range(n):
            refs[n + n_ri + a][...] = refs[a][...].astype(BF16)
        if rider.finish is not None:
            @pl.when((ids[0] == grid[0] - 1) & (ids[1] == grid[1] - 1) & (ids[2] == grid[2] - 1))
            def _():
                rider.finish(r_in, r_out, r_sems)

    hosted = rider.start is not None
    res = pl.pallas_call(
        body, name=name,
        grid_spec=pltpu.PrefetchScalarGridSpec(
            num_scalar_prefetch=1, grid=grid,
            in_specs=[pl.BlockSpec((None, None, tr, C), lambda l, h, i, k_ref: (l, h, i, 0))] * n + [ANY] * n_ri,
            out_specs=out_specs + [ANY] * n_ro, scratch_shapes=list(rider.sems)),
        out_shape=out_shape + list(rider.out_shape),
        input_output_aliases={1 + n + i: n + o for i, o in rider.aliases.items()},
        compiler_params=_cparams(*(("arbitrary",) * 3 if hosted else ("parallel",) * 3)))(
            k_idx, *[w.reshape(L, 2, Rh, C) for w in ws], *rider.operands)
    return res[:n], res[n:]


def _pair_sum(name, g5, ra, c_idx):
    L, A, _, Rh, Cc = g5.shape
    tr = _row_tile(Rh, Cc * 4)

    def body(c_ref, g_ref, r_ref, o_ref):
        o_ref[...] = (g_ref[...].astype(F32) + r_ref[...].astype(F32)).astype(BF16)

    blk = pl.BlockSpec((None, None, tr, Cc), lambda l, a, i, c_ref: (l, a, i, 0))
    return pl.pallas_call(
        body, name=name,
        grid_spec=pltpu.PrefetchScalarGridSpec(
            num_scalar_prefetch=1, grid=(L, A, Rh // tr),
            in_specs=[pl.BlockSpec((None, None, None, tr, Cc), lambda l, a, i, c_ref: (l, a, c_ref[0], i, 0)), blk],
            out_specs=blk),
        out_shape=_sds((L, A, Rh, Cc), BF16),
        compiler_params=_cparams("parallel", "parallel", "parallel"))(c_idx, g5, ra)


def _chip_sum(name, p, rb, kc_idx, kind, layer=0, n_layers=1, prev=None):
    _, A, Rh, Cc = p.shape
    C = rb.shape[-1]
    tr = _row_tile(Rh, C * 4)
    if kind == "row":
        own = pl.BlockSpec((None, None, tr, C), lambda i, kc: (0, kc[0], i, 0))
    else:
        own = pl.BlockSpec((None, None, tr, C), lambda i, kc: (0, 0, i, kc[0]))
    peer = lambda j: pl.BlockSpec((None, None, tr, C), lambda i, kc: (j, 0, i, 0))

    def body(kc_ref, p_ref, r0_ref, r1_ref, r2_ref, *rest):
        o_ref = rest[-1]
        o_ref[...] = ((p_ref[...].astype(F32) + r0_ref[...].astype(F32)) + r1_ref[...].astype(F32)) + r2_ref[...].astype(F32)

    operands = [kc_idx, p, rb, rb, rb] + ([prev] if prev is not None else [])
    return pl.pallas_call(
        body, name=name,
        grid_spec=pltpu.PrefetchScalarGridSpec(
            num_scalar_prefetch=1, grid=(Rh // tr,),
            in_specs=[own, peer(0), peer(1), peer(2)] + ([ANY] if prev is not None else []),
            out_specs=pl.BlockSpec((None, None, tr, C), lambda i, kc: (layer, kc[1], i, 0))),
        out_shape=_sds((n_layers, 2, Rh, C), F32),
        input_output_aliases={5: 0} if prev is not None else {},
        compiler_params=_cparams("parallel"))(*operands)


def _mesh_place():
    x, y, c = lax.axis_index("x"), lax.axis_index("y"), lax.axis_index("c")
    chips = [(1 - x, y), (x, 1 - y), (1 - x, 1 - y)]
    return x, y, c, chips


def _remote(src, dst, send_sem, recv_sem, to):
    return pltpu.make_async_remote_copy(src_ref=src, dst_ref=dst, send_sem=send_sem, recv_sem=recv_sem,
                                        device_id=to, device_id_type=MESH_ID)


def _small_allgather(name, v, with_sum=False, rider=NO_RIDER):
    R, N = v.shape
    n_ri, n_ro, n_own = len(rider.operands), len(rider.out_shape), 2 if with_sum else 1

    def body(*refs):
        r_in = refs[1:1 + n_ri]
        r_out = refs[1 + n_ri + n_own:1 + n_ri + n_own + n_ro]
        r_sems = refs[1 + n_ri + n_own + n_ro + 3:]
        own = (refs[0],) + refs[1 + n_ri:1 + n_ri + n_own] + refs[1 + n_ri + n_own + n_ro:1 + n_ri + n_own + n_ro + 3]
        if with_sum:
            x_ref, out_ref, sum_ref, send_sems, recv_sems, local_sem = own
        else:
            x_ref, out_ref, send_sems, recv_sems, local_sem = own
        if rider.start is not None:
            rider.start(r_in, r_out, r_sems)
        x, y, c, chips = _mesh_place()
        me, sibling = (x, y, c), (x, y, 1 - c)

        def rows(px, py, pc):
            return out_ref.at[pl.ds((4 * px + 2 * py + pc) * R, R), :]

        def copy(k, block, to, src=None):
            return _remote(rows(*block) if src is None else src, rows(*block), send_sems.at[k], recv_sems.at[k], to)

        mine = pltpu.make_async_copy(x_ref, rows(*me), local_sem)
        mine.start()
        first = [copy(0, me, sibling, src=x_ref)]
        first += [copy(1 + j, me, (*chip, c), src=x_ref) for j, chip in enumerate(chips)]
        for cp in first:
            cp.start()
        passed = [copy(4 + j, (*chip, c), sibling) for j, chip in enumerate(chips)]
        for j, chip in enumerate(chips):
            copy(1 + j, (*chip, c), me).wait_recv()
            passed[j].start()
        copy(0, sibling, me).wait_recv()
        for j, chip in enumerate(chips):
            copy(4 + j, (*chip, 1 - c), me).wait_recv()
        for cp in first + passed:
            cp.wait_send()
        mine.wait()
        if with_sum:
            total = out_ref[0:R, :]
            for p in range(1, 8):
                total = total + out_ref[p * R:(p + 1) * R, :]
            sum_ref[...] = total
        if rider.finish is not None:
            rider.finish(r_in, r_out, r_sems)

    vm = pl.BlockSpec(memory_space=pltpu.VMEM)
    out_shape = [_sds((8 * R, N), F32)] + ([_sds((R, N), F32)] if with_sum else [])
    res = pl.pallas_call(
        body, name=name, out_shape=out_shape + list(rider.out_shape), in_specs=[vm] + [ANY] * n_ri,
        out_specs=[vm] * n_own + [ANY] * n_ro,
        scratch_shapes=[pltpu.SemaphoreType.DMA((7,)), pltpu.SemaphoreType.DMA((7,)), pltpu.SemaphoreType.DMA]
        + list(rider.sems),
        input_output_aliases={1 + i: n_own + o for i, o in rider.aliases.items()},
        compiler_params=pltpu.CompilerParams(vmem_limit_bytes=VMEM_LIMIT_BYTES))(v, *rider.operands)
    if rider.start is not None:
        return (*res[:n_own], res[n_own:])
    return res if with_sum else res[0]


def _full_place(ref, kind, C, kk, half, layer=None):
    lead = slice(None) if layer is None else pl.ds(layer, 1)
    if kind == "row":
        return ref.at[lead, kk, half]
    return ref.at[lead, half, :, pl.ds(pl.multiple_of(kk * C, LANES), C)]


def _gather_rider(fulls, kinds, shard_cols, layers=None, peers=(0, 1, 2)):
    n = len(fulls)
    layers = layers or [None] * n

    def copies(outs, sems):
        x, y, c, chips = _mesh_place()
        k = 2 * x + y
        place = lambda a, kk, half: _full_place(outs[a], kinds[a], shard_cols[a], kk, half, layers[a])
        copy = lambda a, j, ref, to: _remote(ref, ref, sems[0].at[6 * a + j], sems[1].at[6 * a + j], to)
        return (x, y, c), [(j, chip) for j, chip in enumerate(chips) if j in peers], k, place, copy

    def start(_, outs, sems):
        (x, y, c), chips, k, place, copy = copies(outs, sems)
        for j, chip in chips:
            for a in range(n):
                copy(a, j, place(a, k, c), (*chip, c)).start()

    def finish(_, outs, sems):
        (x, y, c), chips, k, place, copy = copies(outs, sems)
        me, sibling = (x, y, c), (x, y, 1 - c)
        for j, chip in chips:
            kj = 2 * chip[0] + chip[1]
            for a in range(n):
                copy(a, j, place(a, kj, c), me).wait_recv()
                copy(a, 3 + j, place(a, kj, c), sibling).start()
        for j, chip in chips:
            kj = 2 * chip[0] + chip[1]
            for a in range(n):
                copy(a, 3 + j, place(a, kj, 1 - c), me).wait_recv()
        for j, chip in chips:
            kj = 2 * chip[0] + chip[1]
            for a in range(n):
                copy(a, j, place(a, k, c), (*chip, c)).wait_send()
                copy(a, 3 + j, place(a, kj, c), sibling).wait_send()

    return Rider(tuple(fulls), tuple(_sds(f.shape, BF16) for f in fulls), {a: a for a in range(n)},
                 (pltpu.SemaphoreType.DMA((6 * n,)), pltpu.SemaphoreType.DMA((6 * n,))), start, finish)


def _scatter_rider(ps, kinds, shard_cols, peers=(0, 1, 2), into=None):
    n = len(ps)

    def copies(ins, outs, sems):
        x, y, c, chips = _mesh_place()
        cps = []
        for j, chip in enumerate(chips):
            if j not in peers:
                continue
            kj = 2 * chip[0] + chip[1]
            for a in range(n):
                C = shard_cols[a]
                src = ins[a].at[:, kj] if kinds[a] == "row" else ins[a].at[:, 0, :, pl.ds(pl.multiple_of(kj * C, LANES), C)]
                cps.append(_remote(src, outs[a].at[j], sems[0].at[3 * a + j], sems[1].at[3 * a + j], (*chip, c)))
        return cps

    def start(ins, outs, sems):
        for cp in copies(ins, outs, sems):
            cp.start()

    def finish(ins, outs, sems):
        cps = copies(ins, outs, sems)
        for cp in cps:
            cp.wait_recv()
        for cp in cps:
            cp.wait_send()

    out_shape = tuple(_sds((3, p.shape[0], p.shape[2], C), BF16) for p, C in zip(ps, shard_cols))
    aliases = {n + a: a for a in range(n)} if into is not None else {}
    return Rider(tuple(ps) + tuple(into or ()), out_shape, aliases,
                 (pltpu.SemaphoreType.DMA((3 * n,)), pltpu.SemaphoreType.DMA((3 * n,))), start, finish)


def _run_rider(name, rider):
    n_in, n_out = len(rider.operands), len(rider.out_shape)

    def body(*refs):
        ins, outs, sems = refs[:n_in], refs[n_in:n_in + n_out], refs[n_in + n_out:]
        rider.start(ins, outs, sems)
        rider.finish(ins, outs, sems)

    return pl.pallas_call(
        body, name=name, out_shape=list(rider.out_shape), in_specs=[ANY] * n_in, out_specs=[ANY] * n_out,
        input_output_aliases=dict(rider.aliases), scratch_shapes=list(rider.sems),
        compiler_params=pltpu.CompilerParams(vmem_limit_bytes=VMEM_LIMIT_BYTES))(*rider.operands)


def _exchange_rider(g5s):
    n = len(g5s)

    def copies(ins, outs, sems):
        x, y, c, _ = _mesh_place()
        return [_remote(ins[a].at[:, :, 1 - c], outs[a], sems[0].at[a], sems[1].at[a], (x, y, 1 - c)) for a in range(n)]

    def start(ins, outs, sems):
        for cp in copies(ins, outs, sems):
            cp.start()

    def finish(ins, outs, sems):
        cps = copies(ins, outs, sems)
        for cp in cps:
            cp.wait_recv()
        for cp in cps:
            cp.wait_send()

    out_shape = tuple(_sds((g.shape[0], g.shape[1], g.shape[3], g.shape[4]), BF16) for g in g5s)
    return Rider(tuple(g5s), out_shape, {}, (pltpu.SemaphoreType.DMA((n,)), pltpu.SemaphoreType.DMA((n,))), start, finish)


def _share_rider(fs):
    n = len(fs)

    def start(_, outs, sems):
        x, y, c, _p = _mesh_place()
        for a in range(n):
            mine = outs[a].at[:, c]
            _remote(mine, mine, sems[0].at[a], sems[1].at[a], (x, y, 1 - c)).start()

    def finish(_, outs, sems):
        x, y, c, _p = _mesh_place()
        for a in range(n):
            theirs = outs[a].at[:, 1 - c]
            _remote(theirs, theirs, sems[0].at[a], sems[1].at[a], (x, y, c)).wait_recv()
        for a in range(n):
            mine = outs[a].at[:, c]
            _remote(mine, mine, sems[0].at[a], sems[1].at[a], (x, y, 1 - c)).wait_send()

    return Rider(tuple(fs), tuple(_sds(f.shape, F32) for f in fs), {a: a for a in range(n)},
                 (pltpu.SemaphoreType.DMA((n,)), pltpu.SemaphoreType.DMA((n,))), start, finish)


def _both_riders(r1, r2):
    ni, no, ns = len(r1.operands), len(r1.out_shape), len(r1.sems)
    aliases = dict(r1.aliases)
    aliases.update({ni + i: no + o for i, o in r2.aliases.items()})

    def start(ins, outs, sems):
        r1.start(ins[:ni], outs[:no], sems[:ns])
        r2.start(ins[ni:], outs[no:], sems[ns:])

    def finish(ins, outs, sems):
        r1.finish(ins[:ni], outs[:no], sems[:ns])
        r2.finish(ins[ni:], outs[no:], sems[ns:])

    return Rider(r1.operands + r2.operands, r1.out_shape + r2.out_shape, aliases, r1.sems + r2.sems, start, finish)


def _pack_rows(parts, lane_mult=1024):
    flat = jnp.concatenate([p.reshape(-1).astype(F32) for p in parts])
    n = -(-flat.shape[0] // (8 * lane_mult)) * lane_mult
    return jnp.pad(flat, (0, 8 * n - flat.shape[0])).reshape(8, n)


def _relu2(acc):
    r = jnp.maximum(acc, 0.0)
    return r, r * r


def _times_2r(acc, r):
    return (acc * (2.0 * r.astype(F32)),)


def kernel(x, c, positions, w_mod, b_mod, norm_g, mla_w_in, mla_g_q, mla_g_kv, mla_w_uq, mla_w_ukv, mla_w_o, conv_w_in, conv_w, conv_w_out, mlp_w_up, mlp_w_down, loss_target, m_w_mod, m_b_mod, m_norm_g, m_mla_w_in, m_mla_g_q, m_mla_g_kv, m_mla_w_uq, m_mla_w_ukv, m_mla_w_o, m_conv_w_in, m_conv_w, m_conv_w_out, m_mlp_w_up, m_mlp_w_down, v_w_mod, v_b_mod, v_norm_g, v_mla_w_in, v_mla_g_q, v_mla_g_kv, v_mla_w_uq, v_mla_w_ukv, v_mla_w_o, v_conv_w_in, v_conv_w, v_conv_w_out, v_mlp_w_up, v_mlp_w_down):
    S, D = x.shape[1], x.shape[2]
    Dq = D // N_CHIPS
    ncol = w_mod.shape[2]
    n_mod = N_CHIPS * ncol // D
    F = mlp_w_up.shape[2] * N_CHIPS
    lat_dim = mla_w_in.shape[2]
    rank = mla_g_q.shape[1]
    H = mla_w_uq.shape[2]
    d_qk = mla_w_uq.shape[3]
    assert mla_g_kv.shape[1] == rank and lat_dim == 2 * rank + QK_ROPE and d_qk == QK_NOPE + QK_ROPE
    assert mla_w_ukv.shape[3] == QK_NOPE + V_HEAD and x.shape[0] == 1 and n_mod == 6
    assert norm_g.shape[0] == 2 and mla_w_in.shape[0] == 1 and conv_w_in.shape[0] == 1
    lat_pad = 2 * rank + LANES
    scale = float(d_qk) ** -0.5

    xi, yi, ci = lax.axis_index("x"), lax.axis_index("y"), lax.axis_index("c")
    chip = 2 * xi + yi
    dev = 2 * chip + ci
    c_idx = jnp.reshape(ci, (1,)).astype(jnp.int32)
    k_idx = jnp.reshape(chip, (1,)).astype(jnp.int32)

    n1 = D + 2 * D + 3 * Dq
    g1 = _small_allgather("gather_small_inputs", _pack_rows([c, norm_g, conv_w])).reshape(8, -1)
    c_all = g1[:, :D]
    by_chip = g1[0::2]
    norm_full = jnp.concatenate([by_chip[kk, D:3 * D].reshape(2, 4, Dq) for kk in range(N_CHIPS)], axis=-1)
    convw_full = jnp.concatenate([by_chip[kk, 3 * D:n1].reshape(3, Dq) for kk in range(N_CHIPS)], axis=-1)

    b_cols = lax.dynamic_slice(b_mod, (0, chip * ncol), (2, ncol)).reshape(2, 1, ncol)
    cond_all = _silu(c_all)
    mod_cols = _mod_fwd(cond_all, w_mod, b_cols)
    g2 = _small_allgather("gather_mod", _pack_rows([mod_cols]))
    g2 = g2.reshape(8, -1)[0::2, :2 * 8 * ncol].reshape(N_CHIPS, 2, 8, ncol)
    mod_all = jnp.transpose(g2, (2, 1, 0, 3)).reshape(8, 2, n_mod * D)
    mod_me = lax.dynamic_index_in_dim(mod_all, dev, axis=0, keepdims=False)
    mods = [[mod_me[l, i * D:(i + 1) * D].reshape(1, D) for i in range(n_mod)] for l in range(2)]
    ng = [[norm_full[l, i].reshape(1, D) for i in range(4)] for l in range(2)]

    pos = positions[0].astype(F32)
    inv_freq = ROPE_THETA ** (-jnp.arange(0, QK_ROPE, 2, dtype=F32) / QK_ROPE)
    ang = pos[:, None] * inv_freq
    cos, sin = jnp.cos(ang), jnp.sin(ang)
    zero = jnp.zeros_like(cos)
    rope_tabs = (jnp.concatenate([cos, cos, zero, zero], axis=1),
                 jnp.concatenate([-sin, zero, zero, zero], axis=1),
                 jnp.concatenate([zero, sin, zero, zero], axis=1))

    weights = [("mla_w_in", mla_w_in, "row"), ("mla_w_uq", mla_w_uq.reshape(1, rank // N_CHIPS, H * d_qk), "row"),
               ("mla_w_ukv", mla_w_ukv.reshape(1, rank // N_CHIPS, H * QK_PAD), "row"), ("mla_w_o", mla_w_o, "row"),
               ("conv_w_in", conv_w_in, "col"), ("conv_w_out", conv_w_out, "row"),
               ("mlp_w_up", mlp_w_up, "col"), ("mlp_w_down", mlp_w_down, "row")]
    kinds = [k for _, _, k in weights]
    shard_shapes = [w.shape for _, w, _ in weights]
    shard_cols = [s[2] for s in shard_shapes]
    W_IN, W_UQ, W_UKV, W_O, W_CIN, W_COUT, W_UP, W_DOWN = range(8)
    mla_idx = [W_IN, W_UQ, W_UKV, W_O]
    casted = [_cast_into_full("cast_" + nm, [w], [kind], k_idx)[0][0] for nm, w, kind in weights[:W_UP]]

    def view(i, buf):
        L, R, C = shard_shapes[i]
        return buf.reshape((L, N_CHIPS * R, C) if kinds[i] == "row" else (L, R, N_CHIPS * C))

    NEIGHBOURS, DIAGONAL = (0, 1), (2,)

    def gather_of(bufs, idx, layers=None, peers=(0, 1, 2)):
        return _gather_rider(bufs, [kinds[i] for i in idx], [shard_cols[i] for i in idx], layers, peers)

    def scatter_of(ps, idx, peers=(0, 1, 2), into=None):
        return _scatter_rider(ps, [kinds[i] for i in idx], [shard_cols[i] for i in idx], peers, into)

    def halves(items):
        g5s = []
        for _, i, g in items:
            _, R, C = shard_shapes[i]
            g5s.append(g.reshape((1, N_CHIPS, 2, R // 2, C) if kinds[i] == "row" else (1, 1, 2, R // 2, N_CHIPS * C)))
        return g5s

    def pair_sums(items, g5s, ras):
        return [_pair_sum("pair_sum_" + nm, g5, ra, c_idx) for (nm, _, _), g5, ra in zip(items, g5s, ras)]

    mlp_casted, got = _cast_into_full("cast_mlp_w", [mlp_w_up, mlp_w_down], [kinds[W_UP], kinds[W_DOWN]], k_idx,
                                      gather_of([casted[i] for i in mla_idx], mla_idx))
    casted += list(mlp_casted)
    w_in_p = jnp.pad(view(W_IN, got[0])[0], ((0, 0), (0, lat_pad - lat_dim)))
    w_q_p = jnp.pad(view(W_UQ, got[1])[0].reshape(rank, H, d_qk), ((0, 0), (0, 0), (0, QK_PAD - d_qk))).reshape(rank, H * QK_PAD)
    w_ukv, w_o = view(W_UKV, got[2])[0], view(W_O, got[3])[0]
    HV = H * V_HEAD

    def layer_b(l, transposed):
        if transposed:
            return lambda tm, tn, tk: pl.BlockSpec((None, tn, tk), lambda i, j, k: (l, j, k))
        return lambda tm, tn, tk: pl.BlockSpec((None, tk, tn), lambda i, j, k: (l, k, j))

    def mlp_up(tag, l, h, w, rider=NO_RIDER):
        return _mm("mlp_up_" + tag, h, w, "nn", S, F, D, [_sds((S, F), BF16)] * 2, epilogue=_relu2,
                   b_spec=layer_b(l, False), rider=rider)

    def mlp_down(tag, l, a2, w, rider=NO_RIDER):
        return _mm("mlp_down_" + tag, a2, w, "nn", S, D, F, [_sds((S, D), BF16)], b_spec=layer_b(l, False), rider=rider)

    def mlp_bwd(tag, l, h, r, a2, dy, rider_of=None):
        carried = ()
        first = rider_of(carried) if rider_of else NO_RIDER
        res = _mm("mlp_down_dx_" + tag, dy, w_down, "nt", S, F, D, [_sds((S, F), BF16)], epilogue=_times_2r,
                  b_spec=layer_b(l, True), rider=first,
                  extras=[(r, lambda tm, tn, tk: pl.BlockSpec((tm, tn), lambda i, j, k: (i, j)))])
        (da,), carried = res if rider_of else (res, ())
        second = rider_of(carried) if rider_of else NO_RIDER
        res = _mm("mlp_down_dw_" + tag, a2, dy, "tn", F, D, S, [_sds((F, D), BF16)], rider=second)
        (dw_down,), carried = res if rider_of else (res, ())
        (dh,) = _mm("mlp_up_dx_" + tag, da, w_up, "nt", S, D, F, [_sds((S, D), BF16)], b_spec=layer_b(l, True))
        (dw_up,) = _mm("mlp_up_dw_" + tag, h, da, "tn", D, F, S, [_sds((D, F), BF16)])
        return dh, dw_up, dw_down, carried

    x0 = x[0]
    sh1, sc1, gt1, sh2, sc2, gt2 = mods[0]
    (h1,) = _fwd_boundary("fwd_boundary_0", x0, None, None, None, ng[0][0], sc1, sh1)
    (lat,) = _mm("mla_in", h1, w_in_p, "nn", S, lat_pad, D, [_sds((S, lat_pad), F32)], tn=lat_pad)
    cq, ckv, kr = _latent_fwd(lat, mla_g_q, mla_g_kv, rope_tabs, rank)

    def rope_q(acc, cos_p, sin_lo, sin_hi):
        parts = []
        for hh in range(acc.shape[1] // QK_PAD):
            parts.append(acc[:, hh * QK_PAD:hh * QK_PAD + QK_NOPE])
            parts.append(_rope(acc[:, hh * QK_PAD + QK_NOPE:(hh + 1) * QK_PAD], cos_p, sin_lo, sin_hi))
        return (jnp.concatenate(parts, axis=1),)

    tab_extra = lambda tm, tn, tk: pl.BlockSpec((tm, LANES), lambda i, j, k: (i, 0))
    (q,) = _mm("mla_q", cq, w_q_p, "nn", S, H * QK_PAD, rank, [_sds((S, H * QK_PAD), BF16)], epilogue=rope_q,
               extras=[(t, tab_extra) for t in rope_tabs], tn=2 * QK_PAD)
    (kv,) = _mm("mla_kv", ckv, w_ukv, "nn", S, H * QK_PAD, rank, [_sds((S, H * QK_PAD), BF16)])
    rest_idx = [W_UP, W_DOWN]
    o, lse, (up_buf, down_buf) = _attn_fwd_tri(
        q, kv, kr, H, scale, gather_of([casted[i] for i in rest_idx], rest_idx, [0, 0]))
    (y1,), (cout_buf,) = _mm("mla_out", o, w_o, "nn", S, D, HV, [_sds((S, D), BF16)],
                             rider=gather_of([casted[W_COUT]], [W_COUT]))
    x1, h2 = _fwd_boundary("fwd_boundary_1", x0, y1, gt1, ng[0][1], ng[0][2], sc2, sh2)
    (r2, a2), (cin_buf,) = mlp_up("0", 0, h2, view(W_UP, up_buf), gather_of([casted[W_CIN]], [W_CIN]))
    (y2,), (up_buf,) = mlp_down("0", 0, a2, view(W_DOWN, down_buf), gather_of([up_buf], [W_UP], [1]))
    w_cin, w_cout, w_up = view(W_CIN, cin_buf)[0], view(W_COUT, cout_buf)[0], view(W_UP, up_buf)

    sh1b, sc1b, gt1b, sh2b, sc2b, gt2b = mods[1]
    x2, h3 = _fwd_boundary("fwd_boundary_2", x1, y2, gt2, ng[0][3], ng[1][0], sc1b, sh1b)
    nD = lambda tn: D // tn
    (proj3,), (down_buf,) = _mm(
        "conv_in", h3, w_cin, "nn", S, 3 * D, D, [_sds((3, S, D), BF16)], tn=min(1024, D),
        rider=gather_of([down_buf], [W_DOWN], [1], NEIGHBOURS),
        out_specs=[lambda tm, tn, tk: pl.BlockSpec((None, tm, tn), lambda i, j, k: (j // nD(tn), i, j % nD(tn)))])
    bz = _conv_fwd(proj3, convw_full)
    (y3,) = _mm("conv_out", bz, w_cout, "nn", S, D, D, [_sds((S, D), BF16)])
    x3, h4 = _fwd_boundary("fwd_boundary_3", x2, y3, gt1b, ng[1][1], ng[1][2], sc2b, sh2b)
    (r4, a4), (down_buf,) = mlp_up("1", 1, h4, w_up, gather_of([down_buf], [W_DOWN], [1], DIAGONAL))
    w_down = view(W_DOWN, down_buf)
    (y4,) = mlp_down("1", 1, a4, w_down)

    dx4, dy4, sums_l, loss_acc = _loss_boundary("loss_boundary", x3, y4, gt2b, ng[1][3], loss_target[0])
    loss = lax.psum(loss_acc[0, 0], ("x", "y", "c"))

    dh4, dw_up1, dw_down1, _ = mlp_bwd("1", 1, h4, r4, a4, dy4)
    items = [("mlp_w_up_1", W_UP, dw_up1), ("mlp_w_down_1", W_DOWN, dw_down1)]
    g5s = halves(items)
    dx3, dy3, sums_3, ras = _bwd_boundary("bwd_boundary_3", dx4, dh4, x3, y3, gt1b, ng[1][1], ng[1][2], sc2b,
                                          _exchange_rider(g5s))
    ps_up1, ps_down1 = pair_sums(items, g5s, ras)

    (dbz,) = _mm("conv_out_dx", dy3, w_cout, "nt", S, D, D, [_sds((S, D), BF16)])
    (dw_cout,) = _mm("conv_out_dw", bz, dy3, "tn", D, D, S, [_sds((D, D), BF16)])
    dproj3, dconvw = _conv_bwd(dbz, proj3, convw_full)
    (dh3,), (rb_up1,) = _mm(
        "conv_in_dx", dproj3, w_cin, "nt", S, D, 3 * D, [_sds((S, D), BF16)], tk=D,
        rider=scatter_of([ps_up1], [W_UP], NEIGHBOURS),
        a_spec=lambda tm, tn, tk: pl.BlockSpec((None, tm, tk), lambda i, j, k: (k // (D // tk), i, k % (D // tk))))
    (dw_cin,), (rb_up1,) = _mm(
        "conv_in_dw", h3, dproj3, "tn", D, 3 * D, S, [_sds((D, 3 * D), BF16)], tn=min(1024, D),
        rider=scatter_of([ps_up1], [W_UP], DIAGONAL, [rb_up1]),
        b_spec=lambda tm, tn, tk: pl.BlockSpec((None, tk, tn), lambda i, j, k: (j // nD(tn), k, j % nD(tn))))
    items = [("conv_w_in", W_CIN, dw_cin), ("conv_w_out", W_COUT, dw_cout)]
    g5s = halves(items)
    dx2, dy2, sums_2, ras = _bwd_boundary("bwd_boundary_2", dx3, dh3, x2, y2, gt2, ng[0][3], ng[1][0], sc1b,
                                          _exchange_rider(g5s))
    ps_cin, ps_cout = pair_sums(items, g5s, ras)

    dh2, dw_up0, dw_down0, (rb_down1,) = mlp_bwd(
        "0", 0, h2, r2, a2, dy2,
        lambda got: scatter_of([ps_down1], [W_DOWN], DIAGONAL, list(got)) if got else scatter_of([ps_down1], [W_DOWN], NEIGHBOURS))
    items = [("mlp_w_up_0", W_UP, dw_up0), ("mlp_w_down_0", W_DOWN, dw_down0)]
    g5s = halves(items)
    dx1, dy1, sums_1, ras = _bwd_boundary("bwd_boundary_1", dx2, dh2, x1, y1, gt1, ng[0][1], ng[0][2], sc2,
                                          _exchange_rider(g5s))
    ps_up0, ps_down0 = pair_sums(items, g5s, ras)

    (do,) = _mm("mla_out_dx", dy1, w_o, "nt", S, HV, D, [_sds((S, HV), BF16)])
    (dw_o,) = _mm("mla_out_dw", o, dy1, "tn", HV, D, S, [_sds((HV, D), BF16)])
    dq, dkv, dkr, (rb_up0, rb_down0, rb_cin, rb_cout) = _attn_bwd_tri(
        q, kv, kr, o, do, lse, rope_tabs, H, scale,
        scatter_of([ps_up0, ps_down0, ps_cin, ps_cout], [W_UP, W_DOWN, W_CIN, W_COUT]))
    (dcq,) = _mm("mla_q_dx", dq, w_q_p, "nt", S, rank, H * QK_PAD, [_sds((S, rank), F32)])
    (dw_q_p,) = _mm("mla_q_dw", cq, dq, "tn", rank, H * QK_PAD, S, [_sds((rank, H * QK_PAD), BF16)])
    (dckv,) = _mm("mla_kv_dx", dkv, w_ukv, "nt", S, rank, H * QK_PAD, [_sds((S, rank), F32)])
    (dw_ukv,) = _mm("mla_kv_dw", ckv, dkv, "tn", rank, H * QK_PAD, S, [_sds((rank, H * QK_PAD), BF16)])
    dlat, sums_lat = _latent_bwd(lat, dcq, dckv, dkr, mla_g_q, mla_g_kv, rope_tabs, rank)
    (dw_in_p,) = _mm("mla_in_dw", h1, dlat, "tn", D, lat_pad, S, [_sds((D, lat_pad), BF16)], tn=lat_pad)
    dw_mla = [dw_in_p[:, :lat_dim], dw_q_p.reshape(rank, H, QK_PAD)[:, :, :d_qk].reshape(rank, H * d_qk), dw_ukv, dw_o]
    items = [(weights[i][0], i, g) for i, g in zip(mla_idx, dw_mla)]
    g5s = halves(items)
    (dh1,), ras = _mm("mla_in_dx", dlat, w_in_p, "nt", S, D, lat_pad, [_sds((S, D), BF16)], rider=_exchange_rider(g5s))
    ps_mla = pair_sums(items, g5s, ras)
    grad_x, sums_0, _ = _bwd_boundary("bwd_boundary_0", dx1, dh1, x0, None, None, None, ng[0][0], sc1)

    kc_idx = jnp.stack([chip, ci]).astype(jnp.int32)
    fs_rest = [_chip_sum("chip_sum_" + weights[i][0], p, rb, kc_idx, kinds[i])
               for i, p, rb in [(W_CIN, ps_cin, rb_cin), (W_COUT, ps_cout, rb_cout)]]
    for i, (p1, r1), (p0, r0) in [(W_UP, (ps_up1, rb_up1), (ps_up0, rb_up0)), (W_DOWN, (ps_down1, rb_down1), (ps_down0, rb_down0))]:
        f = _chip_sum("chip_sum_" + weights[i][0] + "_1", p1, r1, kc_idx, kinds[i], layer=1, n_layers=2)
        fs_rest.append(_chip_sum("chip_sum_" + weights[i][0] + "_0", p0, r0, kc_idx, kinds[i], layer=0, n_layers=2, prev=f))

    dmod0 = [sums_0[0], sums_0[1], sums_1[3], sums_1[0], sums_1[1], sums_2[3]]
    dmod1 = [sums_2[0], sums_2[1], sums_3[3], sums_3[0], sums_3[1], sums_l[3]]
    dng0 = [sums_0[2], sums_1[4], sums_1[2], sums_2[4]]
    dng1 = [sums_2[2], sums_3[4], sums_3[2], sums_l[4]]
    small = _pack_rows(dmod0 + dmod1 + dng0 + dng1 + [sums_lat[0], sums_lat[1], dconvw], lane_mult=LANES)
    gathered, total, carried = _small_allgather(
        "gather_small_grads", small, with_sum=True, rider=_both_riders(scatter_of(ps_mla, mla_idx), _share_rider(fs_rest)))
    rbs_mla, finals_rest = carried[:len(mla_idx)], carried[len(mla_idx):]
    n_dm = 2 * n_mod * D
    dmod_all = gathered.reshape(8, -1)[:, :n_dm].reshape(8, 2, n_mod * D)
    total = total.reshape(-1)
    g_b_mod = total[:n_dm].reshape(2, n_mod * D)
    g_norm = lax.dynamic_slice(total[n_dm:n_dm + 8 * D].reshape(2, 4, D), (0, 0, chip * Dq), (2, 4, Dq))
    off = n_dm + 8 * D
    g_gq = total[off:off + rank].reshape(1, rank)
    g_gkv = total[off + rank:off + 2 * rank].reshape(1, rank)
    off += 2 * rank
    g_convw = lax.dynamic_slice(total[off:off + 3 * D].reshape(1, 3, D), (0, 0, chip * Dq), (1, 3, Dq))

    dmod_cols = jnp.transpose(lax.dynamic_slice(dmod_all.reshape(8, 2, N_CHIPS, ncol), (0, 0, chip, 0), (8, 2, 1, ncol))
                              .reshape(8, 2, ncol), (1, 0, 2))
    g_w_mod, d_w_mod, nm_w_mod, nv_w_mod, _ = _adamw_mod(w_mod, cond_all.T, dmod_cols, m_w_mod, v_w_mod)
    fs_mla = [_chip_sum("chip_sum_" + weights[i][0], p, rb, kc_idx, kinds[i]) for i, p, rb in zip(mla_idx, ps_mla, rbs_mla)]
    finals = list(_run_rider("grad_pair_share_mla", _share_rider(fs_mla))) + list(finals_rest)
    orig = [mla_w_in, mla_w_uq, mla_w_ukv, mla_w_o, conv_w_in, conv_w_out, mlp_w_up, mlp_w_down]
    big_grads = [f.reshape(w.shape) for f, w in zip(finals, orig)]

    names = ["b_mod", "norm_g", "mla_w_in", "mla_g_q", "mla_g_kv", "mla_w_uq", "mla_w_ukv", "mla_w_o",
             "conv_w_in", "conv_w", "conv_w_out", "mlp_w_up", "mlp_w_down"]
    ws = [b_mod, norm_g, mla_w_in, mla_g_q, mla_g_kv, mla_w_uq, mla_w_ukv, mla_w_o, conv_w_in, conv_w, conv_w_out,
          mlp_w_up, mlp_w_down]
    ms = [m_b_mod, m_norm_g, m_mla_w_in, m_mla_g_q, m_mla_g_kv, m_mla_w_uq, m_mla_w_ukv, m_mla_w_o, m_conv_w_in,
          m_conv_w, m_conv_w_out, m_mlp_w_up, m_mlp_w_down]
    vs = [v_b_mod, v_norm_g, v_mla_w_in, v_mla_g_q, v_mla_g_kv, v_mla_w_uq, v_mla_w_ukv, v_mla_w_o, v_conv_w_in,
          v_conv_w, v_conv_w_out, v_mlp_w_up, v_mlp_w_down]
    gs = [g_b_mod, g_norm, big_grads[0], g_gq, g_gkv, big_grads[1], big_grads[2], big_grads[3], big_grads[4],
          g_convw, big_grads[5], big_grads[6], big_grads[7]]
    grads, deltas, new_ms, new_vs = [g_w_mod], [d_w_mod], [nm_w_mod], [nv_w_mod]
    for nm, w, g, m, v in zip(names, ws, gs, ms, vs):
        d, nm_, nv_ = _adamw("adamw_" + nm, w, g, m, v)
        grads.append(g)
        deltas.append(d)
        new_ms.append(nm_)
        new_vs.append(nv_)
    return (loss, grad_x[None], *grads, *deltas, *new_ms, *new_vs)
```

```python
from typing import NamedTuple

import jax
import jax.numpy as jnp
from jax import lax
from jax.experimental import pallas as pl
from jax.experimental.pallas import tpu as pltpu

F32 = jnp.float32
BF16 = jnp.bfloat16
NORM_EPS = 1e-6
ROPE_THETA = 10000.0
QK_NOPE = 128
QK_ROPE = 64
V_HEAD = 128
LANES = 128
QK_PAD = QK_NOPE + LANES
ADAM_LR, ADAM_B1, ADAM_B2, ADAM_EPS, ADAM_WD, ADAM_STEP = 0.001, 0.9, 0.999, 1e-08, 0.01, 10
VMEM_LIMIT_BYTES = 56 * 1024 * 1024
N_CHIPS = 4
MESH_ID = pl.DeviceIdType.MESH
ANY = pl.BlockSpec(memory_space=pl.ANY)
NEG_INF = float("-inf")

DIMS_NN = (((1,), (0,)), ((), ()))
DIMS_NT = (((1,), (1,)), ((), ()))
DIMS_TN = (((0,), (0,)), ((), ()))


def _cparams(*sem):
    return pltpu.CompilerParams(dimension_semantics=sem, vmem_limit_bytes=VMEM_LIMIT_BYTES)


def _row_tile(rows, row_bytes, limit=2 * 1024 * 1024, mult=16):
    if rows * row_bytes <= limit or rows % mult:
        return rows
    best = mult
    t = mult
    while t <= rows:
        if rows % t == 0 and t * row_bytes <= limit:
            best = t
        t += mult
    return best


def _rms(v):
    return lax.rsqrt(jnp.mean(v * v, axis=-1, keepdims=True) + NORM_EPS)


class Rider(NamedTuple):
    operands: tuple
    out_shape: tuple
    aliases: dict
    sems: tuple
    start: object
    finish: object
    mid: object = None


NO_RIDER = Rider((), (), {}, (), None, None)


def _rider_end(rider, r_in, r_out, r_sems):
    if rider.mid is not None:
        rider.mid(r_in, r_out, r_sems)
    rider.finish(r_in, r_out, r_sems)


def _mm(name, a, b, mode, M, N, K, outs, *, a_spec=None, b_spec=None, out_specs=None, epilogue=None,
        extras=(), rider=NO_RIDER, tm=1024, tn=1024, tk=4096):
    tm, tn, tk = min(tm, M), min(tn, N), min(tk, K)
    assert M % tm == 0 and N % tn == 0 and K % tk == 0, (name, M, N, K)
    nk = K // tk
    if a_spec is None:
        a_spec = {"nn": pl.BlockSpec((tm, tk), lambda i, j, k: (i, k)),
                  "nt": pl.BlockSpec((tm, tk), lambda i, j, k: (i, k)),
                  "tn": pl.BlockSpec((tk, tm), lambda i, j, k: (k, i))}[mode]
    else:
        a_spec = a_spec(tm, tn, tk)
    if b_spec is None:
        b_spec = {"nn": pl.BlockSpec((tk, tn), lambda i, j, k: (k, j)),
                  "nt": pl.BlockSpec((tn, tk), lambda i, j, k: (j, k)),
                  "tn": pl.BlockSpec((tk, tn), lambda i, j, k: (k, j))}[mode]
    else:
        b_spec = b_spec(tm, tn, tk)
    if out_specs is None:
        out_specs = [pl.BlockSpec((tm, tn), lambda i, j, k: (i, j)) for _ in outs]
    else:
        out_specs = [s(tm, tn, tk) for s in out_specs]
    dims = {"nn": DIMS_NN, "nt": DIMS_NT, "tn": DIMS_TN}[mode]
    ne, no = len(extras), len(outs)
    n_ri, n_ro = len(rider.operands), len(rider.out_shape)
    grid = (M // tm, N // tn, nk)

    def body(*refs):
        a_ref, b_ref = refs[0], refs[1]
        ex = refs[2:2 + ne]
        r_in = refs[2 + ne:2 + ne + n_ri]
        o = refs[2 + ne + n_ri:2 + ne + n_ri + no]
        r_out = refs[2 + ne + n_ri + no:2 + ne + n_ri + no + n_ro]
        scratch = refs[2 + ne + n_ri + no + n_ro:]
        r_sems = scratch[1:] if nk > 1 else scratch
        ii, jj, kk = pl.program_id(0), pl.program_id(1), pl.program_id(2)

        if rider.start is not None:
            @pl.when((ii == 0) & (jj == 0) & (kk == 0))
            def _():
                rider.start(r_in, r_out, r_sems)

        part = lax.dot_general(a_ref[...].astype(BF16), b_ref[...].astype(BF16), dims,
                               preferred_element_type=F32)

        def finish(total):
            vals = epilogue(total, *[e[...] for e in ex]) if epilogue is not None else (total,)
            for r, v in zip(o, vals):
                r[...] = v.astype(r.dtype)

        if nk == 1:
            finish(part)
        else:
            acc = scratch[0]

            @pl.when(kk == 0)
            def _():
                acc[...] = part

            @pl.when(kk > 0)
            def _():
                acc[...] += part

            @pl.when(kk == nk - 1)
            def _():
                finish(acc[...])

        if rider.finish is not None:
            steps = grid[0] * grid[1] * nk
            if rider.mid is not None and steps >= 4:
                @pl.when((ii * grid[1] + jj) * nk + kk == steps // 2)
                def _():
                    rider.mid(r_in, r_out, r_sems)

            @pl.when((ii == grid[0] - 1) & (jj == grid[1] - 1) & (kk == nk - 1))
            def _():
                if rider.mid is not None and steps < 4:
                    rider.mid(r_in, r_out, r_sems)
                rider.finish(r_in, r_out, r_sems)

    operands = [a, b] + [e[0] for e in extras] + list(rider.operands)
    in_specs = [a_spec, b_spec] + [e[1](tm, tn, tk) for e in extras] + [ANY] * n_ri
    hosted = rider.start is not None
    res = pl.pallas_call(
        body, name=name, grid=grid,
        in_specs=in_specs, out_specs=out_specs + [ANY] * n_ro, out_shape=list(outs) + list(rider.out_shape),
        scratch_shapes=([pltpu.VMEM((tm, tn), F32)] if nk > 1 else []) + list(rider.sems),
        input_output_aliases={2 + ne + i: no + r for i, r in rider.aliases.items()},
        compiler_params=_cparams(*(("arbitrary",) * 3 if hosted else ("parallel", "parallel", "arbitrary"))),
    )(*operands)
    return (res[:no], res[no:]) if hosted else res


def _sds(shape, dtype):
    return jax.ShapeDtypeStruct(tuple(shape), dtype)


def _rope(t, cos_p, sin_lo, sin_hi):
    return t * cos_p + pltpu.roll(t, LANES - QK_ROPE // 2, 1) * sin_lo + pltpu.roll(t, QK_ROPE // 2, 1) * sin_hi


def _rope_t(d, cos_p, sin_lo, sin_hi):
    return d * cos_p + pltpu.roll(d * sin_lo, QK_ROPE // 2, 1) + pltpu.roll(d * sin_hi, LANES - QK_ROPE // 2, 1)


def _vec_spec(d):
    return pl.BlockSpec((1, d), lambda i: (0, 0))


def _fwd_boundary(name, x_prev, y, gate, ng_post, ng_pre, sc, sh):
    S, D = x_prev.shape
    ts = min(256, S)
    has_y = y is not None
    row = pl.BlockSpec((ts, D), lambda i: (i, 0))

    def body(*refs):
        if has_y:
            x_ref, y_ref, g_ref, ngp_ref, ngn_ref, sc_ref, sh_ref, xo_ref, h_ref = refs
            yv = y_ref[...].astype(F32)
            xn = x_ref[...] + g_ref[...] * (yv * _rms(yv) * ngp_ref[...])
            xo_ref[...] = xn
        else:
            x_ref, ngn_ref, sc_ref, sh_ref, h_ref = refs
            xn = x_ref[...]
        hn = xn * _rms(xn) * ngn_ref[...]
        h_ref[...] = (hn * (1.0 + sc_ref[...]) + sh_ref[...]).astype(BF16)

    vec = _vec_spec(D)
    if has_y:
        operands = (x_prev, y, gate, ng_post, ng_pre, sc, sh)
        in_specs = [row, row, vec, vec, vec, vec, vec]
        out_shape = [_sds((S, D), F32), _sds((S, D), BF16)]
        out_specs = [row, row]
    else:
        operands = (x_prev, ng_pre, sc, sh)
        in_specs = [row, vec, vec, vec]
        out_shape = [_sds((S, D), BF16)]
        out_specs = [row]
    return pl.pallas_call(body, name=name, grid=(S // ts,), in_specs=in_specs, out_specs=out_specs,
                          out_shape=out_shape, compiler_params=_cparams("parallel"))(*operands)


def _acc_rows(sums_ref, rows):
    for r, v in rows:
        sums_ref[r:r + 1, :] += jnp.sum(v, axis=0, keepdims=True)


def _post_norm_bwd(dxt, yv, gate, ng_post, sums_ref, dy_ref):
    r1 = _rms(yv)
    yhat = yv * r1
    dn = dxt * gate
    u = dn * ng_post
    dy = r1 * (u - yhat * jnp.mean(u * yhat, axis=-1, keepdims=True))
    dy_ref[...] = dy.astype(dy_ref.dtype)
    _acc_rows(sums_ref, [(3, dxt * (yhat * ng_post)), (4, dn * yhat)])


def _loss_boundary(name, x_prev, y, gate, ng_post, target):
    S, D = x_prev.shape
    ts = min(256, S)
    row = pl.BlockSpec((ts, D), lambda i: (i, 0))
    vec = _vec_spec(D)

    def body(x_ref, y_ref, g_ref, ngp_ref, t_ref, dx_ref, dy_ref, sums_ref, loss_ref):
        @pl.when(pl.program_id(0) == 0)
        def _():
            sums_ref[...] = jnp.zeros_like(sums_ref)
            loss_ref[...] = jnp.zeros_like(loss_ref)

        yv = y_ref[...].astype(F32)
        xf = x_ref[...] + g_ref[...] * (yv * _rms(yv) * ngp_ref[...])
        err = xf - t_ref[...]
        loss_ref[...] += 0.5 * jnp.sum(jnp.mean(err * err, axis=-1, keepdims=True))
        dxt = err / D
        dx_ref[...] = dxt
        _post_norm_bwd(dxt, yv, g_ref[...], ngp_ref[...], sums_ref, dy_ref)

    return pl.pallas_call(
        body, name=name, grid=(S // ts,),
        in_specs=[row, row, vec, vec, row],
        out_specs=[row, row, pl.BlockSpec((8, D), lambda i: (0, 0)), pl.BlockSpec((8, LANES), lambda i: (0, 0))],
        out_shape=[_sds((S, D), F32), _sds((S, D), BF16), _sds((8, D), F32), _sds((8, LANES), F32)],
        compiler_params=_cparams("arbitrary"))(x_prev, y, gate, ng_post, target)


def _bwd_boundary(name, dx_new, dh, x_new, y, gate, ng_post, ng_pre, sc, rider=NO_RIDER):
    S, D = x_new.shape
    ts = min(256, S)
    has_y = y is not None
    row = pl.BlockSpec((ts, D), lambda i: (i, 0))
    vec = _vec_spec(D)
    n_in, n_out = (8, 3) if has_y else (5, 2)
    n_ri, n_ro = len(rider.operands), len(rider.out_shape)

    def body(*refs):
        r_in = refs[n_in:n_in + n_ri]
        r_out = refs[n_in + n_ri + n_out:n_in + n_ri + n_out + n_ro]
        r_sems = refs[n_in + n_ri + n_out + n_ro:]
        own = refs[:n_in] + refs[n_in + n_ri:n_in + n_ri + n_out]
        if has_y:
            dxn_ref, dh_ref, x_ref, y_ref, g_ref, ngp_ref, ngn_ref, sc_ref, dxo_ref, dy_ref, sums_ref = own
        else:
            dxn_ref, dh_ref, x_ref, ngn_ref, sc_ref, dxo_ref, sums_ref = own

        @pl.when(pl.program_id(0) == 0)
        def _():
            sums_ref[...] = jnp.zeros_like(sums_ref)
            if rider.start is not None:
                rider.start(r_in, r_out, r_sems)

        xv = x_ref[...]
        dhv = dh_ref[...].astype(F32)
        ngn = ngn_ref[...]
        r2 = _rms(xv)
        xhat = xv * r2
        dn_pre = dhv * (1.0 + sc_ref[...])
        u2 = dn_pre * ngn
        dxt = dxn_ref[...] + r2 * (u2 - xhat * jnp.mean(u2 * xhat, axis=-1, keepdims=True))
        dxo_ref[...] = dxt
        _acc_rows(sums_ref, [(0, dhv), (1, dhv * (xhat * ngn)), (2, dn_pre * xhat)])
        if has_y:
            _post_norm_bwd(dxt, y_ref[...].astype(F32), g_ref[...], ngp_ref[...], sums_ref, dy_ref)

        if rider.finish is not None:
            @pl.when(pl.program_id(0) == S // ts - 1)
            def _():
                _rider_end(rider, r_in, r_out, r_sems)

    sums_spec = pl.BlockSpec((8, D), lambda i: (0, 0))
    if has_y:
        operands = (dx_new, dh, x_new, y, gate, ng_post, ng_pre, sc)
        in_specs = [row, row, row, row, vec, vec, vec, vec]
        out_shape = [_sds((S, D), F32), _sds((S, D), BF16), _sds((8, D), F32)]
        out_specs = [row, row, sums_spec]
    else:
        operands = (dx_new, dh, x_new, ng_pre, sc)
        in_specs = [row, row, row, vec, vec]
        out_shape = [_sds((S, D), F32), _sds((8, D), F32)]
        out_specs = [row, sums_spec]
    res = pl.pallas_call(
        body, name=name, grid=(S // ts,), in_specs=in_specs + [ANY] * n_ri, out_specs=out_specs + [ANY] * n_ro,
        out_shape=out_shape + list(rider.out_shape), scratch_shapes=list(rider.sems),
        input_output_aliases={n_in + i: n_out + o for i, o in rider.aliases.items()},
        compiler_params=_cparams("arbitrary"))(*operands, *rider.operands)
    return (*res[:n_out], res[n_out:])


def _latent_fwd(lat, g_q, g_kv, rope_tabs, rank):
    S, W = lat.shape
    ts = min(256, S)
    tab = pl.BlockSpec((ts, LANES), lambda i: (i, 0))

    def body(lat_ref, gq_ref, gkv_ref, cos_ref, slo_ref, shi_ref, cq_ref, ckv_ref, kr_ref):
        lq = lat_ref[:, 0:rank]
        lkv = lat_ref[:, rank:2 * rank]
        cq_ref[...] = (lq * _rms(lq) * gq_ref[...]).astype(BF16)
        ckv_ref[...] = (lkv * _rms(lkv) * gkv_ref[...]).astype(BF16)
        kr_ref[...] = _rope(lat_ref[:, 2 * rank:W], cos_ref[...], slo_ref[...], shi_ref[...]).astype(BF16)

    return pl.pallas_call(
        body, name="mla_latent_fwd", grid=(S // ts,),
        in_specs=[pl.BlockSpec((ts, W), lambda i: (i, 0)), _vec_spec(rank), _vec_spec(rank), tab, tab, tab],
        out_specs=[pl.BlockSpec((ts, rank), lambda i: (i, 0)), pl.BlockSpec((ts, rank), lambda i: (i, 0)), tab],
        out_shape=[_sds((S, rank), BF16), _sds((S, rank), BF16), _sds((S, LANES), BF16)],
        compiler_params=_cparams("parallel"))(lat, g_q, g_kv, *rope_tabs)


def _latent_bwd(lat, dcq, dckv, dkr, g_q, g_kv, rope_tabs, rank):
    S, W = lat.shape
    ts = min(256, S)
    tab = pl.BlockSpec((ts, LANES), lambda i: (i, 0))
    half = pl.BlockSpec((ts, rank), lambda i: (i, 0))

    def body(lat_ref, dcq_ref, dckv_ref, dkr_ref, gq_ref, gkv_ref, cos_ref, slo_ref, shi_ref, dlat_ref, sums_ref):
        @pl.when(pl.program_id(0) == 0)
        def _():
            sums_ref[...] = jnp.zeros_like(sums_ref)

        def norm_bwd(v, dn, g, r):
            rr = _rms(v)
            vhat = v * rr
            u = dn * g
            sums_ref[r:r + 1, :] += jnp.sum(dn * vhat, axis=0, keepdims=True)
            return rr * (u - vhat * jnp.mean(u * vhat, axis=-1, keepdims=True))

        dlat_ref[:, 0:rank] = norm_bwd(lat_ref[:, 0:rank], dcq_ref[...], gq_ref[...], 0).astype(BF16)
        dlat_ref[:, rank:2 * rank] = norm_bwd(lat_ref[:, rank:2 * rank], dckv_ref[...], gkv_ref[...], 1).astype(BF16)
        dlat_ref[:, 2 * rank:W] = _rope_t(dkr_ref[...], cos_ref[...], slo_ref[...], shi_ref[...]).astype(BF16)

    return pl.pallas_call(
        body, name="mla_latent_bwd", grid=(S // ts,),
        in_specs=[pl.BlockSpec((ts, W), lambda i: (i, 0)), half, half, tab, _vec_spec(rank), _vec_spec(rank),
                  tab, tab, tab],
        out_specs=[pl.BlockSpec((ts, W), lambda i: (i, 0)), pl.BlockSpec((8, rank), lambda i: (0, 0))],
        out_shape=[_sds((S, W), BF16), _sds((8, rank), F32)],
        compiler_params=_cparams("arbitrary"))(lat, dcq, dckv, dkr, g_q, g_kv, *rope_tabs)


def _attn_tiles(S):
    t = min(512, S)
    return t, S // t


def _causal_mask(t):
    return lax.broadcasted_iota(jnp.int32, (t, t), 1) <= lax.broadcasted_iota(jnp.int32, (t, t), 0)


def _attn_fwd(q, kv, kr, heads, scale, rider=NO_RIDER):
    S = q.shape[0]
    t, nb = _attn_tiles(S)
    G = 2 if heads % 2 == 0 else 1
    n_ri, n_ro = len(rider.operands), len(rider.out_shape)

    def body(*refs):
        q_ref, kv_ref, kr_ref = refs[:3]
        r_in = refs[3:3 + n_ri]
        o_ref, lse_ref = refs[3 + n_ri:5 + n_ri]
        r_out = refs[5 + n_ri:5 + n_ri + n_ro]
        m_scr, acc_scr = refs[5 + n_ri + n_ro:7 + n_ri + n_ro]
        r_sems = refs[7 + n_ri + n_ro:]
        h, qi, ki = pl.program_id(0), pl.program_id(1), pl.program_id(2)

        if rider.start is not None:
            @pl.when((h == 0) & (qi == 0) & (ki == 0))
            def _():
                rider.start(r_in, r_out, r_sems)

        @pl.when(ki == 0)
        def _():
            m_scr[...] = jnp.full_like(m_scr, NEG_INF)
            acc_scr[...] = jnp.zeros_like(acc_scr)

        def step(diagonal):
            ones = jnp.ones((t, LANES), BF16)
            for g in range(G):
                kcat = jnp.concatenate([kv_ref[:, g * QK_PAD:g * QK_PAD + QK_NOPE], kr_ref[...]], axis=1)
                vext = jnp.concatenate([kv_ref[:, g * QK_PAD + QK_NOPE:(g + 1) * QK_PAD], ones], axis=1)
                s = lax.dot_general(q_ref[:, g * QK_PAD:(g + 1) * QK_PAD], kcat, DIMS_NT,
                                    preferred_element_type=F32) * scale
                if diagonal:
                    s = jnp.where(_causal_mask(t), s, NEG_INF)
                m_prev = m_scr[g]
                m_new = jnp.maximum(m_prev, jnp.max(s, axis=-1, keepdims=True))
                alpha = jnp.exp(m_prev - m_new)
                p = jnp.exp(s - jnp.tile(m_new, (1, t // LANES)))
                acc_scr[g] = jnp.tile(alpha, (1, 2)) * acc_scr[g] + lax.dot_general(
                    p.astype(BF16), vext, DIMS_NN, preferred_element_type=F32)
                m_scr[g] = m_new

        @pl.when(ki < qi)
        def _():
            step(False)

        @pl.when(ki == qi)
        def _():
            step(True)

        @pl.when(ki == nb - 1)
        def _():
            for g in range(G):
                acc = acc_scr[g]
                o_ref[:, g * V_HEAD:(g + 1) * V_HEAD] = (acc[:, 0:V_HEAD] / acc[:, V_HEAD:2 * V_HEAD]).astype(BF16)
                lse_ref[g] = m_scr[g] + jnp.log(acc[:, V_HEAD:2 * V_HEAD])

        if rider.finish is not None:
            @pl.when((h == heads // G - 1) & (qi == nb - 1) & (ki == nb - 1))
            def _():
                _rider_end(rider, r_in, r_out, r_sems)

    res = pl.pallas_call(
        body, name="mla_attn_fwd", grid=(heads // G, nb, nb),
        in_specs=[pl.BlockSpec((t, G * QK_PAD), lambda h, qi, ki: (qi, h)),
                  pl.BlockSpec((t, G * QK_PAD), lambda h, qi, ki: (jnp.minimum(ki, qi), h)),
                  pl.BlockSpec((t, LANES), lambda h, qi, ki: (jnp.minimum(ki, qi), 0))] + [ANY] * n_ri,
        out_specs=[pl.BlockSpec((t, G * V_HEAD), lambda h, qi, ki: (qi, h)),
                   pl.BlockSpec((G, t, LANES), lambda h, qi, ki: (h, qi, 0))] + [ANY] * n_ro,
        out_shape=[_sds((S, heads * V_HEAD), BF16), _sds((heads, S, LANES), F32)] + list(rider.out_shape),
        scratch_shapes=[pltpu.VMEM((G, t, LANES), F32), pltpu.VMEM((G, t, 2 * V_HEAD), F32)] + list(rider.sems),
        input_output_aliases={3 + i: 2 + o for i, o in rider.aliases.items()},
        compiler_params=_cparams("arbitrary", "arbitrary", "arbitrary"))(q, kv, kr, *rider.operands)
    return res[0], res[1], res[2:]


def _attn_delta(o, do, heads):
    S = o.shape[0]
    t, nb = _attn_tiles(S)

    def body(o_ref, do_ref, out_ref):
        d = jnp.sum(do_ref[...].astype(F32) * o_ref[...].astype(F32), axis=-1, keepdims=True)
        out_ref[...] = jnp.broadcast_to(d, (t, LANES))

    blk = pl.BlockSpec((t, V_HEAD), lambda h, i: (i, h))
    return pl.pallas_call(
        body, name="mla_attn_delta", grid=(heads, nb), in_specs=[blk, blk],
        out_specs=pl.BlockSpec((None, t, LANES), lambda h, i: (h, i, 0)),
        out_shape=_sds((heads, S, LANES), F32), compiler_params=_cparams("parallel", "parallel"))(o, do)


def _attn_bwd(q, kv, kr, delta, do, lse, rope_tabs, heads, scale, rider=NO_RIDER):
    S = q.shape[0]
    t, nb = _attn_tiles(S)
    n_ri, n_ro = len(rider.operands), len(rider.out_shape)
    rep = t // LANES

    def body(*refs):
        q_ref, kv_ref, kr_ref, delta_ref, do_ref, lse_ref, cos_ref, slo_ref, shi_ref = refs[:9]
        r_in = refs[9:9 + n_ri]
        dq_ref, dkv_ref, dkr_ref = refs[9 + n_ri:12 + n_ri]
        r_out = refs[12 + n_ri:12 + n_ri + n_ro]
        dq_scr, dk_scr, dv_scr, dkr_scr = refs[12 + n_ri + n_ro:16 + n_ri + n_ro]
        r_sems = refs[16 + n_ri + n_ro:]
        h, ki, qi = pl.program_id(0), pl.program_id(1), pl.program_id(2)
        q_rows = pl.ds(pl.multiple_of(qi * t, t), t)
        k_rows = pl.ds(pl.multiple_of(ki * t, t), t)

        if rider.start is not None:
            @pl.when((h == 0) & (ki == 0) & (qi == 0))
            def _():
                rider.start(r_in, r_out, r_sems)

        @pl.when((ki == 0) & (qi == 0))
        def _():
            dq_scr[...] = jnp.zeros_like(dq_scr)

        @pl.when((h == 0) & (ki == 0) & (qi == 0))
        def _():
            dkr_scr[...] = jnp.zeros_like(dkr_scr)

        @pl.when(qi == 0)
        def _():
            dk_scr[...] = jnp.zeros_like(dk_scr)
            dv_scr[...] = jnp.zeros_like(dv_scr)

        def step(diagonal):
            qv = q_ref[...]
            kcat = jnp.concatenate([kv_ref[:, 0:QK_NOPE], kr_ref[...]], axis=1)
            s = lax.dot_general(qv, kcat, DIMS_NT, preferred_element_type=F32) * scale
            p = jnp.exp(s - jnp.tile(lse_ref[...], (1, rep)))
            if diagonal:
                p = jnp.where(_causal_mask(t), p, 0.0)
            dov = do_ref[...]
            dv_scr[...] += lax.dot_general(p.astype(BF16), dov, DIMS_TN, preferred_element_type=F32)
            dp = lax.dot_general(dov, kv_ref[:, QK_NOPE:QK_NOPE + V_HEAD], DIMS_NT, preferred_element_type=F32)
            ds = (p * (dp - jnp.tile(delta_ref[...], (1, rep))) * scale).astype(BF16)
            dk_scr[...] += lax.dot_general(ds, qv, DIMS_TN, preferred_element_type=F32)
            dq_scr[q_rows, :] += lax.dot_general(ds, kcat, DIMS_NN, preferred_element_type=F32)

        @pl.when(qi > ki)
        def _():
            step(False)

        @pl.when(qi == ki)
        def _():
            step(True)

        @pl.when(qi == nb - 1)
        def _():
            dkv_ref[...] = jnp.concatenate([dk_scr[:, 0:QK_NOPE], dv_scr[...]], axis=1).astype(BF16)
            dkr_scr[k_rows, :] += dk_scr[:, QK_NOPE:QK_PAD]

        @pl.when(ki == nb - 1)
        def _():
            dqv = dq_scr[q_rows, :]
            dq_ref[q_rows, :] = jnp.concatenate(
                [dqv[:, 0:QK_NOPE], _rope_t(dqv[:, QK_NOPE:QK_PAD], cos_ref[...], slo_ref[...], shi_ref[...])],
                axis=1).astype(BF16)

        @pl.when((h == heads - 1) & (ki == nb - 1) & (qi == nb - 1))
        def _():
            dkr_ref[...] = dkr_scr[...]
            if rider.finish is not None:
                _rider_end(rider, r_in, r_out, r_sems)

    qmap = lambda h, ki, qi: (jnp.maximum(qi, ki), h)
    stat = pl.BlockSpec((None, t, LANES), lambda h, ki, qi: (h, jnp.maximum(qi, ki), 0))
    tab = pl.BlockSpec((t, LANES), lambda h, ki, qi: (qi, 0))
    res = pl.pallas_call(
        body, name="mla_attn_bwd", grid=(heads, nb, nb),
        in_specs=[pl.BlockSpec((t, QK_PAD), qmap),
                  pl.BlockSpec((t, QK_PAD), lambda h, ki, qi: (ki, h)),
                  pl.BlockSpec((t, LANES), lambda h, ki, qi: (ki, 0)),
                  stat,
                  pl.BlockSpec((t, V_HEAD), qmap),
                  stat,
                  tab, tab, tab] + [ANY] * n_ri,
        out_specs=[pl.BlockSpec((S, QK_PAD), lambda h, ki, qi: (0, h)),
                   pl.BlockSpec((t, QK_PAD), lambda h, ki, qi: (ki, h)),
                   pl.BlockSpec((S, LANES), lambda h, ki, qi: (0, 0))] + [ANY] * n_ro,
        out_shape=[_sds((S, heads * QK_PAD), BF16), _sds((S, heads * QK_PAD), BF16), _sds((S, LANES), F32)]
        + list(rider.out_shape),
        scratch_shapes=[pltpu.VMEM((S, QK_PAD), F32), pltpu.VMEM((t, QK_PAD), F32), pltpu.VMEM((t, V_HEAD), F32),
                        pltpu.VMEM((S, LANES), F32)] + list(rider.sems),
        input_output_aliases={9 + i: 3 + o for i, o in rider.aliases.items()},
        compiler_params=_cparams("arbitrary", "arbitrary", "arbitrary"))(q, kv, kr, delta, do, lse, *rope_tabs, *rider.operands)
    return res[0], res[1], res[2], res[3:]


def _causal_pairs(nb, q_major):
    if q_major:
        pairs = [(qi, ki) for qi in range(nb) for ki in range(qi + 1)]
    else:
        pairs = [(qi, ki) for ki in range(nb) for qi in range(ki, nb)]
    return jnp.array([p[0] for p in pairs], jnp.int32), jnp.array([p[1] for p in pairs], jnp.int32), len(pairs)


def _heads_per_step(heads):
    return 2 if heads % 2 == 0 else 1


def _attn_fwd_tri(q, kv, kr, heads, scale, rider=NO_RIDER):
    S = q.shape[0]
    t, nb = _attn_tiles(S)
    G = _heads_per_step(heads)
    q_tab, k_tab, n_pairs = _causal_pairs(nb, True)
    n_ri, n_ro = len(rider.operands), len(rider.out_shape)

    def body(qt_ref, kt_ref, *refs):
        q_ref, kv_ref, kr_ref = refs[:3]
        r_in = refs[3:3 + n_ri]
        o_ref, lse_ref = refs[3 + n_ri:5 + n_ri]
        r_out = refs[5 + n_ri:5 + n_ri + n_ro]
        m_scr, acc_scr = refs[5 + n_ri + n_ro:7 + n_ri + n_ro]
        r_sems = refs[7 + n_ri + n_ro:]
        h, p = pl.program_id(0), pl.program_id(1)
        qi, ki = qt_ref[p], kt_ref[p]

        if rider.start is not None:
            @pl.when((h == 0) & (p == 0))
            def _():
                rider.start(r_in, r_out, r_sems)

        @pl.when(ki == 0)
        def _():
            m_scr[...] = jnp.full_like(m_scr, NEG_INF)
            acc_scr[...] = jnp.zeros_like(acc_scr)

        def step(diagonal):
            ones = jnp.ones((t, LANES), BF16)
            for g in range(G):
                kcat = jnp.concatenate([kv_ref[:, g * QK_PAD:g * QK_PAD + QK_NOPE], kr_ref[...]], axis=1)
                vext = jnp.concatenate([kv_ref[:, g * QK_PAD + QK_NOPE:(g + 1) * QK_PAD], ones], axis=1)
                s = lax.dot_general(q_ref[:, g * QK_PAD:(g + 1) * QK_PAD], kcat, DIMS_NT,
                                    preferred_element_type=F32) * scale
                if diagonal:
                    s = jnp.where(_causal_mask(t), s, NEG_INF)
                m_prev = m_scr[g]
                m_new = jnp.maximum(m_prev, jnp.max(s, axis=-1, keepdims=True))
                alpha = jnp.exp(m_prev - m_new)
                pr = jnp.exp(s - jnp.tile(m_new, (1, t // LANES)))
                acc_scr[g] = jnp.tile(alpha, (1, 2)) * acc_scr[g] + lax.dot_general(
                    pr.astype(BF16), vext, DIMS_NN, preferred_element_type=F32)
                m_scr[g] = m_new

        @pl.when(ki < qi)
        def _():
            step(False)

        @pl.when(ki == qi)
        def _():
            step(True)
            for g in range(G):
                acc = acc_scr[g]
                o_ref[:, g * V_HEAD:(g + 1) * V_HEAD] = (acc[:, 0:V_HEAD] / acc[:, V_HEAD:2 * V_HEAD]).astype(BF16)
                lse_ref[g] = m_scr[g] + jnp.log(acc[:, V_HEAD:2 * V_HEAD])

        if rider.finish is not None:
            halfway = rider.mid is not None and heads // G >= 2
            if halfway:
                @pl.when((h == heads // G // 2) & (p == 0))
                def _():
                    rider.mid(r_in, r_out, r_sems)

            @pl.when((h == heads // G - 1) & (p == n_pairs - 1))
            def _():
                if halfway:
                    rider.finish(r_in, r_out, r_sems)
                else:
                    _rider_end(rider, r_in, r_out, r_sems)

    res = pl.pallas_call(
        body, name="mla_attn_fwd",
        grid_spec=pltpu.PrefetchScalarGridSpec(
            num_scalar_prefetch=2, grid=(heads // G, n_pairs),
            in_specs=[pl.BlockSpec((t, G * QK_PAD), lambda h, p, qt, kt: (qt[p], h)),
                      pl.BlockSpec((t, G * QK_PAD), lambda h, p, qt, kt: (kt[p], h)),
                      pl.BlockSpec((t, LANES), lambda h, p, qt, kt: (kt[p], 0))] + [ANY] * n_ri,
            out_specs=[pl.BlockSpec((t, G * V_HEAD), lambda h, p, qt, kt: (qt[p], h)),
                       pl.BlockSpec((G, t, LANES), lambda h, p, qt, kt: (h, qt[p], 0))] + [ANY] * n_ro,
            scratch_shapes=[pltpu.VMEM((G, t, LANES), F32), pltpu.VMEM((G, t, 2 * V_HEAD), F32)] + list(rider.sems)),
        out_shape=[_sds((S, heads * V_HEAD), BF16), _sds((heads, S, LANES), F32)] + list(rider.out_shape),
        input_output_aliases={5 + i: 2 + o for i, o in rider.aliases.items()},
        compiler_params=_cparams("arbitrary", "arbitrary"))(q_tab, k_tab, q, kv, kr, *rider.operands)
    return res[0], res[1], res[2:]


def _attn_bwd_tri(q, kv, kr, o, do, lse, rope_tabs, heads, scale, rider=NO_RIDER):
    S = q.shape[0]
    t, nb = _attn_tiles(S)
    G = _heads_per_step(heads)
    q_tab, k_tab, n_pairs = _causal_pairs(nb, False)
    n_ri, n_ro = len(rider.operands), len(rider.out_shape)
    rep = t // LANES

    def body(qt_ref, kt_ref, *refs):
        q_ref, kv_ref, kr_ref, o_ref, do_ref, lse_ref, cos_ref, slo_ref, shi_ref = refs[:9]
        r_in = refs[9:9 + n_ri]
        dq_ref, dkv_ref, dkr_ref = refs[9 + n_ri:12 + n_ri]
        r_out = refs[12 + n_ri:12 + n_ri + n_ro]
        dq_scr, dk_scr, dv_scr, dkr_scr, delta_scr = refs[12 + n_ri + n_ro:17 + n_ri + n_ro]
        r_sems = refs[17 + n_ri + n_ro:]
        h, p = pl.program_id(0), pl.program_id(1)
        qi, ki = qt_ref[p], kt_ref[p]
        q_rows = pl.ds(pl.multiple_of(qi * t, t), t)
        k_rows = pl.ds(pl.multiple_of(ki * t, t), t)

        @pl.when(ki == 0)
        def _():
            for g in range(G):
                cols = slice(g * V_HEAD, (g + 1) * V_HEAD)
                d = jnp.sum(do_ref[:, cols].astype(F32) * o_ref[:, cols].astype(F32), axis=-1, keepdims=True)
                delta_scr[g, q_rows, :] = jnp.broadcast_to(d, (t, LANES))

        if rider.start is not None:
            @pl.when((h == 0) & (p == 0))
            def _():
                rider.start(r_in, r_out, r_sems)

        @pl.when(p == 0)
        def _():
            dq_scr[...] = jnp.zeros_like(dq_scr)

        @pl.when((h == 0) & (p == 0))
        def _():
            dkr_scr[...] = jnp.zeros_like(dkr_scr)

        @pl.when(qi == ki)
        def _():
            dk_scr[...] = jnp.zeros_like(dk_scr)
            dv_scr[...] = jnp.zeros_like(dv_scr)

        def step(diagonal):
            for g in range(G):
                qv = q_ref[:, g * QK_PAD:(g + 1) * QK_PAD]
                kcat = jnp.concatenate([kv_ref[:, g * QK_PAD:g * QK_PAD + QK_NOPE], kr_ref[...]], axis=1)
                s = lax.dot_general(qv, kcat, DIMS_NT, preferred_element_type=F32) * scale
                pr = jnp.exp(s - jnp.tile(lse_ref[g], (1, rep)))
                if diagonal:
                    pr = jnp.where(_causal_mask(t), pr, 0.0)
                dov = do_ref[:, g * V_HEAD:(g + 1) * V_HEAD]
                dv_scr[g] += lax.dot_general(pr.astype(BF16), dov, DIMS_TN, preferred_element_type=F32)
                dp = lax.dot_general(dov, kv_ref[:, g * QK_PAD + QK_NOPE:(g + 1) * QK_PAD], DIMS_NT,
                                     preferred_element_type=F32)
                ds = (pr * (dp - jnp.tile(delta_scr[g, q_rows, :], (1, rep))) * scale).astype(BF16)
                dk_scr[g] += lax.dot_general(ds, qv, DIMS_TN, preferred_element_type=F32)
                dq_scr[q_rows, g * QK_PAD:(g + 1) * QK_PAD] += lax.dot_general(ds, kcat, DIMS_NN,
                                                                               preferred_element_type=F32)

        @pl.when(qi > ki)
        def _():
            step(False)

        @pl.when(qi == ki)
        def _():
            step(True)
            for g in range(G):
                dqv = dq_scr[q_rows, g * QK_PAD:(g + 1) * QK_PAD]
                dq_ref[q_rows, g * QK_PAD:(g + 1) * QK_PAD] = jnp.concatenate(
                    [dqv[:, 0:QK_NOPE], _rope_t(dqv[:, QK_NOPE:QK_PAD], cos_ref[...], slo_ref[...], shi_ref[...])],
                    axis=1).astype(BF16)

        @pl.when(qi == nb - 1)
        def _():
            for g in range(G):
                dkv_ref[:, g * QK_PAD:(g + 1) * QK_PAD] = jnp.concatenate(
                    [dk_scr[g][:, 0:QK_NOPE], dv_scr[g]], axis=1).astype(BF16)
                dkr_scr[k_rows, :] += dk_scr[g][:, QK_NOPE:QK_PAD]

        @pl.when((h == heads // G - 1) & (p == n_pairs - 1))
        def _():
            dkr_ref[...] = dkr_scr[...]
            if rider.finish is not None:
                _rider_end(rider, r_in, r_out, r_sems)

    q_blk = lambda w: pl.BlockSpec((t, G * w), lambda h, p, qt, kt: (qt[p], h))
    stat = pl.BlockSpec((G, t, LANES), lambda h, p, qt, kt: (h, qt[p], 0))
    tab = pl.BlockSpec((t, LANES), lambda h, p, qt, kt: (kt[p], 0))
    res = pl.pallas_call(
        body, name="mla_attn_bwd",
        grid_spec=pltpu.PrefetchScalarGridSpec(
            num_scalar_prefetch=2, grid=(heads // G, n_pairs),
            in_specs=[q_blk(QK_PAD),
                      pl.BlockSpec((t, G * QK_PAD), lambda h, p, qt, kt: (kt[p], h)),
                      tab, q_blk(V_HEAD), q_blk(V_HEAD), stat, tab, tab, tab] + [ANY] * n_ri,
            out_specs=[pl.BlockSpec((S, G * QK_PAD), lambda h, p, qt, kt: (0, h)),
                       pl.BlockSpec((t, G * QK_PAD), lambda h, p, qt, kt: (kt[p], h)),
                       pl.BlockSpec((S, LANES), lambda h, p, qt, kt: (0, 0))] + [ANY] * n_ro,
            scratch_shapes=[pltpu.VMEM((S, G * QK_PAD), F32), pltpu.VMEM((G, t, QK_PAD), F32),
                            pltpu.VMEM((G, t, V_HEAD), F32), pltpu.VMEM((S, LANES), F32),
                            pltpu.VMEM((G, S, LANES), F32)] + list(rider.sems)),
        out_shape=[_sds((S, heads * QK_PAD), BF16), _sds((S, heads * QK_PAD), BF16), _sds((S, LANES), F32)]
        + list(rider.out_shape),
        input_output_aliases={11 + i: 3 + o for i, o in rider.aliases.items()},
        compiler_params=_cparams("arbitrary", "arbitrary"))(q_tab, k_tab, q, kv, kr, o, do, lse, *rope_tabs,
                                                            *rider.operands)
    return res[0], res[1], res[2], res[3:]


def _shift_down(z, n, rows):
    return jnp.where(rows >= n, pltpu.roll(z, n, 0), 0.0)


def _shift_up(z, n, rows, S):
    return jnp.where(rows < S - n, pltpu.roll(z, S - n, 0), 0.0)


def _conv_specs(S, tc):
    strip = lambda p: pl.BlockSpec((None, S, tc), lambda j: (p, 0, j))
    return strip(0), strip(1), strip(2), pl.BlockSpec((3, tc), lambda j: (0, j))


def _conv_fwd(proj3, w):
    _, S, D = proj3.shape
    tc = LANES

    def body(b_ref, c_ref, u_ref, w_ref, out_ref):
        z = c_ref[...].astype(F32) * u_ref[...].astype(F32)
        rows = lax.broadcasted_iota(jnp.int32, (S, tc), 0)
        zc = w_ref[0:1, :] * _shift_down(z, 2, rows) + w_ref[1:2, :] * _shift_down(z, 1, rows) + w_ref[2:3, :] * z
        out_ref[...] = (b_ref[...].astype(F32) * zc).astype(BF16)

    return pl.pallas_call(
        body, name="conv_fwd", grid=(D // tc,), in_specs=list(_conv_specs(S, tc)),
        out_specs=pl.BlockSpec((S, tc), lambda j: (0, j)), out_shape=_sds((S, D), BF16),
        compiler_params=_cparams("parallel"))(proj3, proj3, proj3, w)


def _conv_bwd(dbz, proj3, w):
    _, S, D = proj3.shape
    tc = LANES

    def body(d_ref, b_ref, c_ref, u_ref, w_ref, dp_ref, dw_ref):
        cv, uv, dv = c_ref[...].astype(F32), u_ref[...].astype(F32), d_ref[...].astype(F32)
        z = cv * uv
        rows = lax.broadcasted_iota(jnp.int32, (S, tc), 0)
        z1, z2 = _shift_down(z, 1, rows), _shift_down(z, 2, rows)
        zc = w_ref[0:1, :] * z2 + w_ref[1:2, :] * z1 + w_ref[2:3, :] * z
        dp_ref[0] = (dv * zc).astype(BF16)
        dzc = dv * b_ref[...].astype(F32)
        dz = w_ref[2:3, :] * dzc + w_ref[1:2, :] * _shift_up(dzc, 1, rows, S) + w_ref[0:1, :] * _shift_up(dzc, 2, rows, S)
        dp_ref[1] = (dz * uv).astype(BF16)
        dp_ref[2] = (dz * cv).astype(BF16)
        dw_ref[0:1, :] = jnp.sum(dzc * z2, axis=0, keepdims=True)
        dw_ref[1:2, :] = jnp.sum(dzc * z1, axis=0, keepdims=True)
        dw_ref[2:3, :] = jnp.sum(dzc * z, axis=0, keepdims=True)

    sb, sc_, su, sw = _conv_specs(S, tc)
    return pl.pallas_call(
        body, name="conv_bwd", grid=(D // tc,),
        in_specs=[pl.BlockSpec((S, tc), lambda j: (0, j)), sb, sc_, su, sw],
        out_specs=[pl.BlockSpec((3, S, tc), lambda j: (0, 0, j)), pl.BlockSpec((3, tc), lambda j: (0, j))],
        out_shape=[_sds((3, S, D), BF16), _sds((3, D), F32)],
        compiler_params=_cparams("parallel"))(dbz, proj3, proj3, proj3, w)


def _silu(c_all):
    def body(c_ref, o_ref):
        cv = c_ref[...]
        o_ref[...] = cv * (1.0 / (1.0 + jnp.exp(-cv)))

    vm = pl.BlockSpec(memory_space=pltpu.VMEM)
    return pl.pallas_call(body, name="cond_silu", in_specs=[vm], out_specs=vm, out_shape=_sds(c_all.shape, F32))(c_all)


def _mod_fwd(cond, w_mod, b_cols):
    L, D, ncol = w_mod.shape
    B = cond.shape[0]
    tk, tn = min(512, D), min(1024, ncol)
    nk = D // tk

    def body(c_ref, w_ref, b_ref, out_ref, acc):
        kk = pl.program_id(2)
        part = lax.dot_general(c_ref[...].astype(BF16), w_ref[...].astype(BF16), DIMS_NN, preferred_element_type=F32)

        @pl.when(kk == 0)
        def _():
            acc[...] = part

        @pl.when(kk > 0)
        def _():
            acc[...] += part

        @pl.when(kk == nk - 1)
        def _():
            out_ref[...] = acc[...] + b_ref[...]

    return pl.pallas_call(
        body, name="mod_fwd", grid=(L, ncol // tn, nk),
        in_specs=[pl.BlockSpec((B, tk), lambda l, j, k: (0, k)),
                  pl.BlockSpec((None, tk, tn), lambda l, j, k: (l, k, j)),
                  pl.BlockSpec((None, 1, tn), lambda l, j, k: (l, 0, j))],
        out_specs=pl.BlockSpec((None, B, tn), lambda l, j, k: (l, 0, j)),
        out_shape=_sds((L, B, ncol), F32),
        scratch_shapes=[pltpu.VMEM((B, tn), F32)],
        compiler_params=_cparams("parallel", "parallel", "arbitrary"))(cond, w_mod, b_cols)


def _adamw_math(w, g, m, v):
    m = ADAM_B1 * m + (1.0 - ADAM_B1) * g
    v = ADAM_B2 * v + (1.0 - ADAM_B2) * (g * g)
    m_hat = m / (1.0 - ADAM_B1 ** ADAM_STEP)
    v_hat = v / (1.0 - ADAM_B2 ** ADAM_STEP)
    delta = -ADAM_LR * (m_hat / (jnp.sqrt(v_hat) + ADAM_EPS) + ADAM_WD * w)
    return delta, m, v


def _adamw(name, w, g, m, v):
    shape = w.shape
    cols = shape[-1] if w.ndim <= 3 else shape[-2] * shape[-1]
    rows = w.size // cols
    w2, g2, m2, v2 = (t.reshape(rows, cols) for t in (w, g, m, v))
    tr = _row_tile(rows, cols * 4, limit=1024 * 1024, mult=8)
    spec = pl.BlockSpec((tr, cols), lambda i: (i, 0))

    def body(w_ref, g_ref, m_ref, v_ref, d_ref, nm_ref, nv_ref):
        d, nm, nv = _adamw_math(w_ref[...], g_ref[...], m_ref[...], v_ref[...])
        d_ref[...] = d
        nm_ref[...] = nm
        nv_ref[...] = nv

    outs = pl.pallas_call(body, name=name, grid=(rows // tr,), in_specs=[spec] * 4, out_specs=[spec] * 3,
                          out_shape=[_sds((rows, cols), F32)] * 3, compiler_params=_cparams("parallel"))(w2, g2, m2, v2)
    return tuple(t.reshape(shape) for t in outs)


def _adamw_mod(w, cond_t, dmod_cols, m, v, rider=NO_RIDER):
    L, D, ncol = w.shape
    B = cond_t.shape[1]
    tr, tc = min(256, D), min(1024, ncol)
    blk = pl.BlockSpec((None, tr, tc), lambda l, i, j: (l, i, j))
    grid = (L, D // tr, ncol // tc)
    n_ri, n_ro = len(rider.operands), len(rider.out_shape)

    def body(*refs):
        w_ref, ct_ref, dm_ref, m_ref, v_ref = refs[:5]
        r_in = refs[5:5 + n_ri]
        g_ref, d_ref, nm_ref, nv_ref = refs[5 + n_ri:9 + n_ri]
        r_out = refs[9 + n_ri:9 + n_ri + n_ro]
        r_sems = refs[9 + n_ri + n_ro:]
        ids = [pl.program_id(a) for a in range(3)]

        if rider.start is not None:
            @pl.when((ids[0] == 0) & (ids[1] == 0) & (ids[2] == 0))
            def _():
                rider.start(r_in, r_out, r_sems)

        g = lax.dot_general(ct_ref[...], dm_ref[...], DIMS_NN, precision=lax.Precision.HIGHEST,
                            preferred_element_type=F32)
        d, nm, nv = _adamw_math(w_ref[...], g, m_ref[...], v_ref[...])
        g_ref[...] = g
        d_ref[...] = d
        nm_ref[...] = nm
        nv_ref[...] = nv

        if rider.finish is not None:
            @pl.when((ids[0] == grid[0] - 1) & (ids[1] == grid[1] - 1) & (ids[2] == grid[2] - 1))
            def _():
                _rider_end(rider, r_in, r_out, r_sems)

    hosted = rider.start is not None
    res = pl.pallas_call(
        body, name="adamw_w_mod", grid=grid,
        in_specs=[blk, pl.BlockSpec((tr, B), lambda l, i, j: (i, 0)),
                  pl.BlockSpec((None, B, tc), lambda l, i, j: (l, 0, j)), blk, blk] + [ANY] * n_ri,
        out_specs=[blk] * 4 + [ANY] * n_ro, out_shape=[_sds((L, D, ncol), F32)] * 4 + list(rider.out_shape),
        scratch_shapes=list(rider.sems), input_output_aliases={5 + i: 4 + o for i, o in rider.aliases.items()},
        compiler_params=_cparams(*(("arbitrary",) * 3 if hosted else ("parallel",) * 3)))(
            w, cond_t, dmod_cols, m, v, *rider.operands)
    return (*res[:4], res[4:])


def _cast_into_full(name, ws, kinds, k_idx, rider=NO_RIDER):
    L, R, C = ws[0].shape
    assert all(w.shape == (L, R, C) for w in ws)
    n = len(ws)
    Rh = R // 2
    tr = _row_tile(Rh, C * 4)
    grid = (L, 2, Rh // tr)
    out_shape, out_specs = [], []
    for kind in kinds:
        if kind == "row":
            out_shape.append(_sds((L, N_CHIPS, 2, Rh, C), BF16))
            out_specs.append(pl.BlockSpec((None, None, None, tr, C), lambda l, h, i, k_ref: (l, k_ref[0], h, i, 0)))
        else:
            out_shape.append(_sds((L, 2, Rh, N_CHIPS * C), BF16))
            out_specs.append(pl.BlockSpec((None, None, tr, C), lambda l, h, i, k_ref: (l, h, i, k_ref[0])))
    n_ri, n_ro = len(rider.operands), len(rider.out_shape)

    def body(k_ref, *refs):
        r_in = refs[n:n + n_ri]
        r_out = refs[2 * n + n_ri:2 * n + n_ri + n_ro]
        r_sems = refs[2 * n + n_ri + n_ro:]
        ids = [pl.program_id(a) for a in range(3)]
        if rider.start is not None:
            @pl.when((ids[0] == 0) & (ids[1] == 0) & (ids[2] == 0))
            def _():
                rider.start(r_in, r_out, r_sems)
        for a in range(n):
            refs[n + n_ri + a][...] = refs[a][...].astype(BF16)
        if rider.finish is not None:
            @pl.when((ids[0] == grid[0] - 1) & (ids[1] == grid[1] - 1) & (ids[2] == grid[2] - 1))
            def _():
                _rider_end(rider, r_in, r_out, r_sems)

    hosted = rider.start is not None
    res = pl.pallas_call(
        body, name=name,
        grid_spec=pltpu.PrefetchScalarGridSpec(
            num_scalar_prefetch=1, grid=grid,
            in_specs=[pl.BlockSpec((None, None, tr, C), lambda l, h, i, k_ref: (l, h, i, 0))] * n + [ANY] * n_ri,
            out_specs=out_specs + [ANY] * n_ro, scratch_shapes=list(rider.sems)),
        out_shape=out_shape + list(rider.out_shape),
        input_output_aliases={1 + n + i: n + o for i, o in rider.aliases.items()},
        compiler_params=_cparams(*(("arbitrary",) * 3 if hosted else ("parallel",) * 3)))(
            k_idx, *[w.reshape(L, 2, Rh, C) for w in ws], *rider.operands)
    return res[:n], res[n:]


def _pair_sum(name, g5, ra, c_idx):
    L, A, _, Rh, Cc = g5.shape
    tr = _row_tile(Rh, Cc * 4)

    def body(c_ref, g_ref, r_ref, o_ref):
        o_ref[...] = (g_ref[...].astype(F32) + r_ref[...].astype(F32)).astype(BF16)

    blk = pl.BlockSpec((None, None, tr, Cc), lambda l, a, i, c_ref: (l, a, i, 0))
    return pl.pallas_call(
        body, name=name,
        grid_spec=pltpu.PrefetchScalarGridSpec(
            num_scalar_prefetch=1, grid=(L, A, Rh // tr),
            in_specs=[pl.BlockSpec((None, None, None, tr, Cc), lambda l, a, i, c_ref: (l, a, c_ref[0], i, 0)), blk],
            out_specs=blk),
        out_shape=_sds((L, A, Rh, Cc), BF16),
        compiler_params=_cparams("parallel", "parallel", "parallel"))(c_idx, g5, ra)


def _chip_sum(name, p, rb, kc_idx, kind, layer=0, n_layers=1, prev=None):
    _, A, Rh, Cc = p.shape
    C = rb.shape[-1]
    tr = _row_tile(Rh, C * 4)
    if kind == "row":
        own = pl.BlockSpec((None, None, tr, C), lambda i, kc: (0, kc[0], i, 0))
    else:
        own = pl.BlockSpec((None, None, tr, C), lambda i, kc: (0, 0, i, kc[0]))
    peer = lambda j: pl.BlockSpec((None, None, tr, C), lambda i, kc: (j, 0, i, 0))

    def body(kc_ref, p_ref, r0_ref, r1_ref, r2_ref, *rest):
        o_ref = rest[-1]
        o_ref[...] = ((p_ref[...].astype(F32) + r0_ref[...].astype(F32)) + r1_ref[...].astype(F32)) + r2_ref[...].astype(F32)

    operands = [kc_idx, p, rb, rb, rb] + ([prev] if prev is not None else [])
    return pl.pallas_call(
        body, name=name,
        grid_spec=pltpu.PrefetchScalarGridSpec(
            num_scalar_prefetch=1, grid=(Rh // tr,),
            in_specs=[own, peer(0), peer(1), peer(2)] + ([ANY] if prev is not None else []),
            out_specs=pl.BlockSpec((None, None, tr, C), lambda i, kc: (layer, kc[1], i, 0))),
        out_shape=_sds((n_layers, 2, Rh, C), F32),
        input_output_aliases={5: 0} if prev is not None else {},
        compiler_params=_cparams("parallel"))(*operands)


def _mesh_place():
    x, y, c = lax.axis_index("x"), lax.axis_index("y"), lax.axis_index("c")
    chips = [(1 - x, y), (x, 1 - y), (1 - x, 1 - y)]
    return x, y, c, chips


def _remote(src, dst, send_sem, recv_sem, to):
    return pltpu.make_async_remote_copy(src_ref=src, dst_ref=dst, send_sem=send_sem, recv_sem=recv_sem,
                                        device_id=to, device_id_type=MESH_ID)


def _small_allgather(name, v, with_sum=False, rider=NO_RIDER):
    R, N = v.shape
    n_ri, n_ro, n_own = len(rider.operands), len(rider.out_shape), 2 if with_sum else 1

    def body(*refs):
        r_in = refs[1:1 + n_ri]
        r_out = refs[1 + n_ri + n_own:1 + n_ri + n_own + n_ro]
        r_sems = refs[1 + n_ri + n_own + n_ro + 3:]
        own = (refs[0],) + refs[1 + n_ri:1 + n_ri + n_own] + refs[1 + n_ri + n_own + n_ro:1 + n_ri + n_own + n_ro + 3]
        if with_sum:
            x_ref, out_ref, sum_ref, send_sems, recv_sems, local_sem = own
        else:
            x_ref, out_ref, send_sems, recv_sems, local_sem = own
        if rider.start is not None:
            rider.start(r_in, r_out, r_sems)
        x, y, c, chips = _mesh_place()
        me, sibling = (x, y, c), (x, y, 1 - c)

        def rows(px, py, pc):
            return out_ref.at[pl.ds((4 * px + 2 * py + pc) * R, R), :]

        def copy(k, block, to, src=None):
            return _remote(rows(*block) if src is None else src, rows(*block), send_sems.at[k], recv_sems.at[k], to)

        mine = pltpu.make_async_copy(x_ref, rows(*me), local_sem)
        mine.start()
        first = [copy(0, me, sibling, src=x_ref)]
        first += [copy(1 + j, me, (*chip, c), src=x_ref) for j, chip in enumerate(chips)]
        for cp in first:
            cp.start()
        passed = [copy(4 + j, (*chip, c), sibling) for j, chip in enumerate(chips)]
        for j, chip in enumerate(chips):
            copy(1 + j, (*chip, c), me).wait_recv()
            passed[j].start()
        copy(0, sibling, me).wait_recv()
        for j, chip in enumerate(chips):
            copy(4 + j, (*chip, 1 - c), me).wait_recv()
        for cp in first + passed:
            cp.wait_send()
        mine.wait()
        if with_sum:
            total = out_ref[0:R, :]
            for p in range(1, 8):
                total = total + out_ref[p * R:(p + 1) * R, :]
            sum_ref[...] = total
        if rider.finish is not None:
            _rider_end(rider, r_in, r_out, r_sems)

    vm = pl.BlockSpec(memory_space=pltpu.VMEM)
    out_shape = [_sds((8 * R, N), F32)] + ([_sds((R, N), F32)] if with_sum else [])
    res = pl.pallas_call(
        body, name=name, out_shape=out_shape + list(rider.out_shape), in_specs=[vm] + [ANY] * n_ri,
        out_specs=[vm] * n_own + [ANY] * n_ro,
        scratch_shapes=[pltpu.SemaphoreType.DMA((7,)), pltpu.SemaphoreType.DMA((7,)), pltpu.SemaphoreType.DMA]
        + list(rider.sems),
        input_output_aliases={1 + i: n_own + o for i, o in rider.aliases.items()},
        compiler_params=pltpu.CompilerParams(vmem_limit_bytes=VMEM_LIMIT_BYTES))(v, *rider.operands)
    if rider.start is not None:
        return (*res[:n_own], res[n_own:])
    return res if with_sum else res[0]


def _full_place(ref, kind, C, kk, half, layer=None):
    lead = slice(None) if layer is None else pl.ds(layer, 1)
    if kind == "row":
        return ref.at[lead, kk, half]
    return ref.at[lead, half, :, pl.ds(pl.multiple_of(kk * C, LANES), C)]


def _gather_rider(fulls, kinds, shard_cols, layers=None, peers=(0, 1, 2)):
    n = len(fulls)
    layers = layers or [None] * n
    rows = [f.shape[3] if kind == "row" else f.shape[2] for f, kind in zip(fulls, kinds)]
    n_chunks = 2 if all(r % 32 == 0 for r in rows) else 1

    def copies(outs, sems):
        x, y, c, chips = _mesh_place()
        k = 2 * x + y

        def place(a, kk, half, ch):
            rc = rows[a] // n_chunks
            return _full_place(outs[a], kinds[a], shard_cols[a], kk, half, layers[a]).at[:, pl.ds(ch * rc, rc), :]

        def copy(a, j, ch, ref, to):
            s = 6 * (n_chunks * a + ch) + j
            return _remote(ref, ref, sems[0].at[s], sems[1].at[s], to)

        return (x, y, c), [(j, chip) for j, chip in enumerate(chips) if j in peers], k, place, copy

    def start(_, outs, sems):
        (x, y, c), chips, k, place, copy = copies(outs, sems)
        for ch in range(n_chunks):
            for j, chip in chips:
                for a in range(n):
                    copy(a, j, ch, place(a, k, c, ch), (*chip, c)).start()

    def pass_on(outs, sems, ch):
        (x, y, c), chips, k, place, copy = copies(outs, sems)
        for j, chip in chips:
            kj = 2 * chip[0] + chip[1]
            for a in range(n):
                copy(a, j, ch, place(a, kj, c, ch), (x, y, c)).wait_recv()
                copy(a, 3 + j, ch, place(a, kj, c, ch), (x, y, 1 - c)).start()

    def mid(_, outs, sems):
        pass_on(outs, sems, 0)

    def finish(_, outs, sems):
        pass_on(outs, sems, n_chunks - 1)
        (x, y, c), chips, k, place, copy = copies(outs, sems)
        for ch in range(n_chunks):
            for j, chip in chips:
                kj = 2 * chip[0] + chip[1]
                for a in range(n):
                    copy(a, 3 + j, ch, place(a, kj, 1 - c, ch), (x, y, c)).wait_recv()
        for ch in range(n_chunks):
            for j, chip in chips:
                kj = 2 * chip[0] + chip[1]
                for a in range(n):
                    copy(a, j, ch, place(a, k, c, ch), (*chip, c)).wait_send()
                    copy(a, 3 + j, ch, place(a, kj, c, ch), (x, y, 1 - c)).wait_send()

    n_sems = 6 * n * n_chunks
    return Rider(tuple(fulls), tuple(_sds(f.shape, BF16) for f in fulls), {a: a for a in range(n)},
                 (pltpu.SemaphoreType.DMA((n_sems,)), pltpu.SemaphoreType.DMA((n_sems,))), start, finish,
                 mid if n_chunks == 2 else None)


def _scatter_rider(ps, kinds, shard_cols, peers=(0, 1, 2), into=None):
    n = len(ps)

    def copies(ins, outs, sems):
        x, y, c, chips = _mesh_place()
        cps = []
        for j, chip in enumerate(chips):
            if j not in peers:
                continue
            kj = 2 * chip[0] + chip[1]
            for a in range(n):
                C = shard_cols[a]
                src = ins[a].at[:, kj] if kinds[a] == "row" else ins[a].at[:, 0, :, pl.ds(pl.multiple_of(kj * C, LANES), C)]
                cps.append(_remote(src, outs[a].at[j], sems[0].at[3 * a + j], sems[1].at[3 * a + j], (*chip, c)))
        return cps

    def start(ins, outs, sems):
        for cp in copies(ins, outs, sems):
            cp.start()

    def finish(ins, outs, sems):
        cps = copies(ins, outs, sems)
        for cp in cps:
            cp.wait_recv()
        for cp in cps:
            cp.wait_send()

    out_shape = tuple(_sds((3, p.shape[0], p.shape[2], C), BF16) for p, C in zip(ps, shard_cols))
    aliases = {n + a: a for a in range(n)} if into is not None else {}
    return Rider(tuple(ps) + tuple(into or ()), out_shape, aliases,
                 (pltpu.SemaphoreType.DMA((3 * n,)), pltpu.SemaphoreType.DMA((3 * n,))), start, finish)


def _run_rider(name, rider):
    n_in, n_out = len(rider.operands), len(rider.out_shape)

    def body(*refs):
        ins, outs, sems = refs[:n_in], refs[n_in:n_in + n_out], refs[n_in + n_out:]
        rider.start(ins, outs, sems)
        _rider_end(rider, ins, outs, sems)

    return pl.pallas_call(
        body, name=name, out_shape=list(rider.out_shape), in_specs=[ANY] * n_in, out_specs=[ANY] * n_out,
        input_output_aliases=dict(rider.aliases), scratch_shapes=list(rider.sems),
        compiler_params=pltpu.CompilerParams(vmem_limit_bytes=VMEM_LIMIT_BYTES))(*rider.operands)


def _exchange_rider(g5s):
    n = len(g5s)

    def copies(ins, outs, sems):
        x, y, c, _ = _mesh_place()
        return [_remote(ins[a].at[:, :, 1 - c], outs[a], sems[0].at[a], sems[1].at[a], (x, y, 1 - c)) for a in range(n)]

    def start(ins, outs, sems):
        for cp in copies(ins, outs, sems):
            cp.start()

    def finish(ins, outs, sems):
        cps = copies(ins, outs, sems)
        for cp in cps:
            cp.wait_recv()
        for cp in cps:
            cp.wait_send()

    out_shape = tuple(_sds((g.shape[0], g.shape[1], g.shape[3], g.shape[4]), BF16) for g in g5s)
    return Rider(tuple(g5s), out_shape, {}, (pltpu.SemaphoreType.DMA((n,)), pltpu.SemaphoreType.DMA((n,))), start, finish)


def _share_rider(fs):
    n = len(fs)

    def start(_, outs, sems):
        x, y, c, _p = _mesh_place()
        for a in range(n):
            mine = outs[a].at[:, c]
            _remote(mine, mine, sems[0].at[a], sems[1].at[a], (x, y, 1 - c)).start()

    def finish(_, outs, sems):
        x, y, c, _p = _mesh_place()
        for a in range(n):
            theirs = outs[a].at[:, 1 - c]
            _remote(theirs, theirs, sems[0].at[a], sems[1].at[a], (x, y, c)).wait_recv()
        for a in range(n):
            mine = outs[a].at[:, c]
            _remote(mine, mine, sems[0].at[a], sems[1].at[a], (x, y, 1 - c)).wait_send()

    return Rider(tuple(fs), tuple(_sds(f.shape, F32) for f in fs), {a: a for a in range(n)},
                 (pltpu.SemaphoreType.DMA((n,)), pltpu.SemaphoreType.DMA((n,))), start, finish)


def _both_riders(r1, r2):
    ni, no, ns = len(r1.operands), len(r1.out_shape), len(r1.sems)
    aliases = dict(r1.aliases)
    aliases.update({ni + i: no + o for i, o in r2.aliases.items()})

    def start(ins, outs, sems):
        r1.start(ins[:ni], outs[:no], sems[:ns])
        r2.start(ins[ni:], outs[no:], sems[ns:])

    def finish(ins, outs, sems):
        _rider_end(r1, ins[:ni], outs[:no], sems[:ns])
        _rider_end(r2, ins[ni:], outs[no:], sems[ns:])

    return Rider(r1.operands + r2.operands, r1.out_shape + r2.out_shape, aliases, r1.sems + r2.sems, start, finish)


def _pack_rows(parts, lane_mult=1024):
    flat = jnp.concatenate([p.reshape(-1).astype(F32) for p in parts])
    n = -(-flat.shape[0] // (8 * lane_mult)) * lane_mult
    return jnp.pad(flat, (0, 8 * n - flat.shape[0])).reshape(8, n)


def _relu2(acc):
    r = jnp.maximum(acc, 0.0)
    return r, r * r


def _times_2r(acc, r):
    return (acc * (2.0 * r.astype(F32)),)


def kernel(x, c, positions, w_mod, b_mod, norm_g, mla_w_in, mla_g_q, mla_g_kv, mla_w_uq, mla_w_ukv, mla_w_o, conv_w_in, conv_w, conv_w_out, mlp_w_up, mlp_w_down, loss_target, m_w_mod, m_b_mod, m_norm_g, m_mla_w_in, m_mla_g_q, m_mla_g_kv, m_mla_w_uq, m_mla_w_ukv, m_mla_w_o, m_conv_w_in, m_conv_w, m_conv_w_out, m_mlp_w_up, m_mlp_w_down, v_w_mod, v_b_mod, v_norm_g, v_mla_w_in, v_mla_g_q, v_mla_g_kv, v_mla_w_uq, v_mla_w_ukv, v_mla_w_o, v_conv_w_in, v_conv_w, v_conv_w_out, v_mlp_w_up, v_mlp_w_down):
    S, D = x.shape[1], x.shape[2]
    Dq = D // N_CHIPS
    ncol = w_mod.shape[2]
    n_mod = N_CHIPS * ncol // D
    F = mlp_w_up.shape[2] * N_CHIPS
    lat_dim = mla_w_in.shape[2]
    rank = mla_g_q.shape[1]
    H = mla_w_uq.shape[2]
    d_qk = mla_w_uq.shape[3]
    assert mla_g_kv.shape[1] == rank and lat_dim == 2 * rank + QK_ROPE and d_qk == QK_NOPE + QK_ROPE
    assert mla_w_ukv.shape[3] == QK_NOPE + V_HEAD and x.shape[0] == 1 and n_mod == 6
    assert norm_g.shape[0] == 2 and mla_w_in.shape[0] == 1 and conv_w_in.shape[0] == 1
    lat_pad = 2 * rank + LANES
    scale = float(d_qk) ** -0.5

    xi, yi, ci = lax.axis_index("x"), lax.axis_index("y"), lax.axis_index("c")
    chip = 2 * xi + yi
    dev = 2 * chip + ci
    c_idx = jnp.reshape(ci, (1,)).astype(jnp.int32)
    k_idx = jnp.reshape(chip, (1,)).astype(jnp.int32)

    n1 = D + 2 * D + 3 * Dq
    g1 = _small_allgather("gather_small_inputs", _pack_rows([c, norm_g, conv_w])).reshape(8, -1)
    c_all = g1[:, :D]
    by_chip = g1[0::2]
    norm_full = jnp.concatenate([by_chip[kk, D:3 * D].reshape(2, 4, Dq) for kk in range(N_CHIPS)], axis=-1)
    convw_full = jnp.concatenate([by_chip[kk, 3 * D:n1].reshape(3, Dq) for kk in range(N_CHIPS)], axis=-1)

    b_cols = lax.dynamic_slice(b_mod, (0, chip * ncol), (2, ncol)).reshape(2, 1, ncol)
    cond_all = _silu(c_all)
    mod_cols = _mod_fwd(cond_all, w_mod, b_cols)
    g2 = _small_allgather("gather_mod", _pack_rows([mod_cols]))
    g2 = g2.reshape(8, -1)[0::2, :2 * 8 * ncol].reshape(N_CHIPS, 2, 8, ncol)
    mod_all = jnp.transpose(g2, (2, 1, 0, 3)).reshape(8, 2, n_mod * D)
    mod_me = lax.dynamic_index_in_dim(mod_all, dev, axis=0, keepdims=False)
    mods = [[mod_me[l, i * D:(i + 1) * D].reshape(1, D) for i in range(n_mod)] for l in range(2)]
    ng = [[norm_full[l, i].reshape(1, D) for i in range(4)] for l in range(2)]

    pos = positions[0].astype(F32)
    inv_freq = ROPE_THETA ** (-jnp.arange(0, QK_ROPE, 2, dtype=F32) / QK_ROPE)
    ang = pos[:, None] * inv_freq
    cos, sin = jnp.cos(ang), jnp.sin(ang)
    zero = jnp.zeros_like(cos)
    rope_tabs = (jnp.concatenate([cos, cos, zero, zero], axis=1),
                 jnp.concatenate([-sin, zero, zero, zero], axis=1),
                 jnp.concatenate([zero, sin, zero, zero], axis=1))

    weights = [("mla_w_in", mla_w_in, "row"), ("mla_w_uq", mla_w_uq.reshape(1, rank // N_CHIPS, H * d_qk), "row"),
               ("mla_w_ukv", mla_w_ukv.reshape(1, rank // N_CHIPS, H * QK_PAD), "row"), ("mla_w_o", mla_w_o, "row"),
               ("conv_w_in", conv_w_in, "col"), ("conv_w_out", conv_w_out, "row"),
               ("mlp_w_up", mlp_w_up, "col"), ("mlp_w_down", mlp_w_down, "row")]
    kinds = [k for _, _, k in weights]
    shard_shapes = [w.shape for _, w, _ in weights]
    shard_cols = [s[2] for s in shard_shapes]
    W_IN, W_UQ, W_UKV, W_O, W_CIN, W_COUT, W_UP, W_DOWN = range(8)
    mla_idx = [W_IN, W_UQ, W_UKV, W_O]
    casted = [_cast_into_full("cast_" + nm, [w], [kind], k_idx)[0][0] for nm, w, kind in weights[:W_UP]]

    def view(i, buf):
        L, R, C = shard_shapes[i]
        return buf.reshape((L, N_CHIPS * R, C) if kinds[i] == "row" else (L, R, N_CHIPS * C))

    NEIGHBOURS, DIAGONAL = (0, 1), (2,)

    def gather_of(bufs, idx, layers=None, peers=(0, 1, 2)):
        return _gather_rider(bufs, [kinds[i] for i in idx], [shard_cols[i] for i in idx], layers, peers)

    def scatter_of(ps, idx, peers=(0, 1, 2), into=None):
        return _scatter_rider(ps, [kinds[i] for i in idx], [shard_cols[i] for i in idx], peers, into)

    def halves(items):
        g5s = []
        for _, i, g in items:
            _, R, C = shard_shapes[i]
            g5s.append(g.reshape((1, N_CHIPS, 2, R // 2, C) if kinds[i] == "row" else (1, 1, 2, R // 2, N_CHIPS * C)))
        return g5s

    def pair_sums(items, g5s, ras):
        return [_pair_sum("pair_sum_" + nm, g5, ra, c_idx) for (nm, _, _), g5, ra in zip(items, g5s, ras)]

    mlp_casted, got = _cast_into_full("cast_mlp_w", [mlp_w_up, mlp_w_down], [kinds[W_UP], kinds[W_DOWN]], k_idx,
                                      gather_of([casted[i] for i in mla_idx], mla_idx))
    casted += list(mlp_casted)
    w_in_p = jnp.pad(view(W_IN, got[0])[0], ((0, 0), (0, lat_pad - lat_dim)))
    w_q_p = jnp.pad(view(W_UQ, got[1])[0].reshape(rank, H, d_qk), ((0, 0), (0, 0), (0, QK_PAD - d_qk))).reshape(rank, H * QK_PAD)
    w_ukv, w_o = view(W_UKV, got[2])[0], view(W_O, got[3])[0]
    HV = H * V_HEAD

    def layer_b(l, transposed):
        if transposed:
            return lambda tm, tn, tk: pl.BlockSpec((None, tn, tk), lambda i, j, k: (l, j, k))
        return lambda tm, tn, tk: pl.BlockSpec((None, tk, tn), lambda i, j, k: (l, k, j))

    def mlp_up(tag, l, h, w, rider=NO_RIDER):
        return _mm("mlp_up_" + tag, h, w, "nn", S, F, D, [_sds((S, F), BF16)] * 2, epilogue=_relu2,
                   b_spec=layer_b(l, False), rider=rider)

    def mlp_down(tag, l, a2, w, rider=NO_RIDER):
        return _mm("mlp_down_" + tag, a2, w, "nn", S, D, F, [_sds((S, D), BF16)], b_spec=layer_b(l, False), rider=rider)

    def mlp_bwd(tag, l, h, r, a2, dy, rider_of=None):
        carried = ()
        first = rider_of(carried) if rider_of else NO_RIDER
        res = _mm("mlp_down_dx_" + tag, dy, w_down, "nt", S, F, D, [_sds((S, F), BF16)], epilogue=_times_2r,
                  b_spec=layer_b(l, True), rider=first,
                  extras=[(r, lambda tm, tn, tk: pl.BlockSpec((tm, tn), lambda i, j, k: (i, j)))])
        (da,), carried = res if rider_of else (res, ())
        second = rider_of(carried) if rider_of else NO_RIDER
        res = _mm("mlp_down_dw_" + tag, a2, dy, "tn", F, D, S, [_sds((F, D), BF16)], rider=second)
        (dw_down,), carried = res if rider_of else (res, ())
        (dh,) = _mm("mlp_up_dx_" + tag, da, w_up, "nt", S, D, F, [_sds((S, D), BF16)], b_spec=layer_b(l, True))
        (dw_up,) = _mm("mlp_up_dw_" + tag, h, da, "tn", D, F, S, [_sds((D, F), BF16)])
        return dh, dw_up, dw_down, carried

    x0 = x[0]
    sh1, sc1, gt1, sh2, sc2, gt2 = mods[0]
    (h1,) = _fwd_boundary("fwd_boundary_0", x0, None, None, None, ng[0][0], sc1, sh1)
    (lat,) = _mm("mla_in", h1, w_in_p, "nn", S, lat_pad, D, [_sds((S, lat_pad), F32)], tn=lat_pad)
    cq, ckv, kr = _latent_fwd(lat, mla_g_q, mla_g_kv, rope_tabs, rank)

    def rope_q(acc, cos_p, sin_lo, sin_hi):
        parts = []
        for hh in range(acc.shape[1] // QK_PAD):
            parts.append(acc[:, hh * QK_PAD:hh * QK_PAD + QK_NOPE])
            parts.append(_rope(acc[:, hh * QK_PAD + QK_NOPE:(hh + 1) * QK_PAD], cos_p, sin_lo, sin_hi))
        return (jnp.concatenate(parts, axis=1),)

    tab_extra = lambda tm, tn, tk: pl.BlockSpec((tm, LANES), lambda i, j, k: (i, 0))
    (q,) = _mm("mla_q", cq, w_q_p, "nn", S, H * QK_PAD, rank, [_sds((S, H * QK_PAD), BF16)], epilogue=rope_q,
               extras=[(t, tab_extra) for t in rope_tabs], tn=2 * QK_PAD)
    (kv,) = _mm("mla_kv", ckv, w_ukv, "nn", S, H * QK_PAD, rank, [_sds((S, H * QK_PAD), BF16)])
    rest_idx = [W_UP, W_DOWN]
    o, lse, (up_buf, down_buf) = _attn_fwd_tri(
        q, kv, kr, H, scale, gather_of([casted[i] for i in rest_idx], rest_idx, [0, 0]))
    (y1,), (cout_buf,) = _mm("mla_out", o, w_o, "nn", S, D, HV, [_sds((S, D), BF16)],
                             rider=gather_of([casted[W_COUT]], [W_COUT]))
    x1, h2 = _fwd_boundary("fwd_boundary_1", x0, y1, gt1, ng[0][1], ng[0][2], sc2, sh2)
    (r2, a2), (cin_buf,) = mlp_up("0", 0, h2, view(W_UP, up_buf), gather_of([casted[W_CIN]], [W_CIN]))
    (y2,), (up_buf,) = mlp_down("0", 0, a2, view(W_DOWN, down_buf), gather_of([up_buf], [W_UP], [1]))
    w_cin, w_cout, w_up = view(W_CIN, cin_buf)[0], view(W_COUT, cout_buf)[0], view(W_UP, up_buf)

    sh1b, sc1b, gt1b, sh2b, sc2b, gt2b = mods[1]
    x2, h3 = _fwd_boundary("fwd_boundary_2", x1, y2, gt2, ng[0][3], ng[1][0], sc1b, sh1b)
    nD = lambda tn: D // tn
    (proj3,), (down_buf,) = _mm(
        "conv_in", h3, w_cin, "nn", S, 3 * D, D, [_sds((3, S, D), BF16)], tn=min(1024, D),
        rider=gather_of([down_buf], [W_DOWN], [1], NEIGHBOURS),
        out_specs=[lambda tm, tn, tk: pl.BlockSpec((None, tm, tn), lambda i, j, k: (j // nD(tn), i, j % nD(tn)))])
    bz = _conv_fwd(proj3, convw_full)
    (y3,) = _mm("conv_out", bz, w_cout, "nn", S, D, D, [_sds((S, D), BF16)])
    x3, h4 = _fwd_boundary("fwd_boundary_3", x2, y3, gt1b, ng[1][1], ng[1][2], sc2b, sh2b)
    (r4, a4), (down_buf,) = mlp_up("1", 1, h4, w_up, gather_of([down_buf], [W_DOWN], [1], DIAGONAL))
    w_down = view(W_DOWN, down_buf)
    (y4,) = mlp_down("1", 1, a4, w_down)

    dx4, dy4, sums_l, loss_acc = _loss_boundary("loss_boundary", x3, y4, gt2b, ng[1][3], loss_target[0])
    loss = lax.psum(loss_acc[0, 0], ("x", "y", "c"))

    dh4, dw_up1, dw_down1, _ = mlp_bwd("1", 1, h4, r4, a4, dy4)
    items = [("mlp_w_up_1", W_UP, dw_up1), ("mlp_w_down_1", W_DOWN, dw_down1)]
    g5s = halves(items)
    dx3, dy3, sums_3, ras = _bwd_boundary("bwd_boundary_3", dx4, dh4, x3, y3, gt1b, ng[1][1], ng[1][2], sc2b,
                                          _exchange_rider(g5s))
    ps_up1, ps_down1 = pair_sums(items, g5s, ras)

    (dbz,) = _mm("conv_out_dx", dy3, w_cout, "nt", S, D, D, [_sds((S, D), BF16)])
    (dw_cout,) = _mm("conv_out_dw", bz, dy3, "tn", D, D, S, [_sds((D, D), BF16)])
    dproj3, dconvw = _conv_bwd(dbz, proj3, convw_full)
    (dh3,), (rb_up1,) = _mm(
        "conv_in_dx", dproj3, w_cin, "nt", S, D, 3 * D, [_sds((S, D), BF16)], tk=D,
        rider=scatter_of([ps_up1], [W_UP], NEIGHBOURS),
        a_spec=lambda tm, tn, tk: pl.BlockSpec((None, tm, tk), lambda i, j, k: (k // (D // tk), i, k % (D // tk))))
    (dw_cin,), (rb_up1,) = _mm(
        "conv_in_dw", h3, dproj3, "tn", D, 3 * D, S, [_sds((D, 3 * D), BF16)], tn=min(1024, D),
        rider=scatter_of([ps_up1], [W_UP], DIAGONAL, [rb_up1]),
        b_spec=lambda tm, tn, tk: pl.BlockSpec((None, tk, tn), lambda i, j, k: (j // nD(tn), k, j % nD(tn))))
    items = [("conv_w_in", W_CIN, dw_cin), ("conv_w_out", W_COUT, dw_cout)]
    g5s = halves(items)
    dx2, dy2, sums_2, ras = _bwd_boundary("bwd_boundary_2", dx3, dh3, x2, y2, gt2, ng[0][3], ng[1][0], sc1b,
                                          _exchange_rider(g5s))
    ps_cin, ps_cout = pair_sums(items, g5s, ras)

    dh2, dw_up0, dw_down0, (rb_down1,) = mlp_bwd(
        "0", 0, h2, r2, a2, dy2,
        lambda got: scatter_of([ps_down1], [W_DOWN], DIAGONAL, list(got)) if got else scatter_of([ps_down1], [W_DOWN], NEIGHBOURS))
    items = [("mlp_w_up_0", W_UP, dw_up0), ("mlp_w_down_0", W_DOWN, dw_down0)]
    g5s = halves(items)
    dx1, dy1, sums_1, ras = _bwd_boundary("bwd_boundary_1", dx2, dh2, x1, y1, gt1, ng[0][1], ng[0][2], sc2,
                                          _exchange_rider(g5s))
    ps_up0, ps_down0 = pair_sums(items, g5s, ras)

    (do,) = _mm("mla_out_dx", dy1, w_o, "nt", S, HV, D, [_sds((S, HV), BF16)])
    (dw_o,) = _mm("mla_out_dw", o, dy1, "tn", HV, D, S, [_sds((HV, D), BF16)])
    dq, dkv, dkr, (rb_up0, rb_down0, rb_cin, rb_cout) = _attn_bwd_tri(
        q, kv, kr, o, do, lse, rope_tabs, H, scale,
        scatter_of([ps_up0, ps_down0, ps_cin, ps_cout], [W_UP, W_DOWN, W_CIN, W_COUT]))
    (dcq,) = _mm("mla_q_dx", dq, w_q_p, "nt", S, rank, H * QK_PAD, [_sds((S, rank), F32)])
    (dw_q_p,) = _mm("mla_q_dw", cq, dq, "tn", rank, H * QK_PAD, S, [_sds((rank, H * QK_PAD), BF16)])
    (dckv,) = _mm("mla_kv_dx", dkv, w_ukv, "nt", S, rank, H * QK_PAD, [_sds((S, rank), F32)])
    (dw_ukv,) = _mm("mla_kv_dw", ckv, dkv, "tn", rank, H * QK_PAD, S, [_sds((rank, H * QK_PAD), BF16)])
    dlat, sums_lat = _latent_bwd(lat, dcq, dckv, dkr, mla_g_q, mla_g_kv, rope_tabs, rank)
    (dw_in_p,) = _mm("mla_in_dw", h1, dlat, "tn", D, lat_pad, S, [_sds((D, lat_pad), BF16)], tn=lat_pad)
    dw_mla = [dw_in_p[:, :lat_dim], dw_q_p.reshape(rank, H, QK_PAD)[:, :, :d_qk].reshape(rank, H * d_qk), dw_ukv, dw_o]
    items = [(weights[i][0], i, g) for i, g in zip(mla_idx, dw_mla)]
    g5s = halves(items)
    (dh1,), ras = _mm("mla_in_dx", dlat, w_in_p, "nt", S, D, lat_pad, [_sds((S, D), BF16)], rider=_exchange_rider(g5s))
    ps_mla = pair_sums(items, g5s, ras)
    grad_x, sums_0, _ = _bwd_boundary("bwd_boundary_0", dx1, dh1, x0, None, None, None, ng[0][0], sc1)

    kc_idx = jnp.stack([chip, ci]).astype(jnp.int32)
    fs_rest = [_chip_sum("chip_sum_" + weights[i][0], p, rb, kc_idx, kinds[i])
               for i, p, rb in [(W_CIN, ps_cin, rb_cin), (W_COUT, ps_cout, rb_cout)]]
    for i, (p1, r1), (p0, r0) in [(W_UP, (ps_up1, rb_up1), (ps_up0, rb_up0)), (W_DOWN, (ps_down1, rb_down1), (ps_down0, rb_down0))]:
        f = _chip_sum("chip_sum_" + weights[i][0] + "_1", p1, r1, kc_idx, kinds[i], layer=1, n_layers=2)
        fs_rest.append(_chip_sum("chip_sum_" + weights[i][0] + "_0", p0, r0, kc_idx, kinds[i], layer=0, n_layers=2, prev=f))

    dmod0 = [sums_0[0], sums_0[1], sums_1[3], sums_1[0], sums_1[1], sums_2[3]]
    dmod1 = [sums_2[0], sums_2[1], sums_3[3], sums_3[0], sums_3[1], sums_l[3]]
    dng0 = [sums_0[2], sums_1[4], sums_1[2], sums_2[4]]
    dng1 = [sums_2[2], sums_3[4], sums_3[2], sums_l[4]]
    small = _pack_rows(dmod0 + dmod1 + dng0 + dng1 + [sums_lat[0], sums_lat[1], dconvw], lane_mult=LANES)
    gathered, total, carried = _small_allgather(
        "gather_small_grads", small, with_sum=True, rider=_both_riders(scatter_of(ps_mla, mla_idx), _share_rider(fs_rest)))
    rbs_mla, finals_rest = carried[:len(mla_idx)], carried[len(mla_idx):]
    n_dm = 2 * n_mod * D
    dmod_all = gathered.reshape(8, -1)[:, :n_dm].reshape(8, 2, n_mod * D)
    total = total.reshape(-1)
    g_b_mod = total[:n_dm].reshape(2, n_mod * D)
    g_norm = lax.dynamic_slice(total[n_dm:n_dm + 8 * D].reshape(2, 4, D), (0, 0, chip * Dq), (2, 4, Dq))
    off = n_dm + 8 * D
    g_gq = total[off:off + rank].reshape(1, rank)
    g_gkv = total[off + rank:off + 2 * rank].reshape(1, rank)
    off += 2 * rank
    g_convw = lax.dynamic_slice(total[off:off + 3 * D].reshape(1, 3, D), (0, 0, chip * Dq), (1, 3, Dq))

    dmod_cols = jnp.transpose(lax.dynamic_slice(dmod_all.reshape(8, 2, N_CHIPS, ncol), (0, 0, chip, 0), (8, 2, 1, ncol))
                              .reshape(8, 2, ncol), (1, 0, 2))
    g_w_mod, d_w_mod, nm_w_mod, nv_w_mod, _ = _adamw_mod(w_mod, cond_all.T, dmod_cols, m_w_mod, v_w_mod)
    fs_mla = [_chip_sum("chip_sum_" + weights[i][0], p, rb, kc_idx, kinds[i]) for i, p, rb in zip(mla_idx, ps_mla, rbs_mla)]
    finals = list(_run_rider("grad_pair_share_mla", _share_rider(fs_mla))) + list(finals_rest)
    orig = [mla_w_in, mla_w_uq, mla_w_ukv, mla_w_o, conv_w_in, conv_w_out, mlp_w_up, mlp_w_down]
    big_grads = [f.reshape(w.shape) for f, w in zip(finals, orig)]

    names = ["b_mod", "norm_g", "mla_w_in", "mla_g_q", "mla_g_kv", "mla_w_uq", "mla_w_ukv", "mla_w_o",
             "conv_w_in", "conv_w", "conv_w_out", "mlp_w_up", "mlp_w_down"]
    ws = [b_mod, norm_g, mla_w_in, mla_g_q, mla_g_kv, mla_w_uq, mla_w_ukv, mla_w_o, conv_w_in, conv_w, conv_w_out,
          mlp_w_up, mlp_w_down]
    ms = [m_b_mod, m_norm_g, m_mla_w_in, m_mla_g_q, m_mla_g_kv, m_mla_w_uq, m_mla_w_ukv, m_mla_w_o, m_conv_w_in,
          m_conv_w, m_conv_w_out, m_mlp_w_up, m_mlp_w_down]
    vs = [v_b_mod, v_norm_g, v_mla_w_in, v_mla_g_q, v_mla_g_kv, v_mla_w_uq, v_mla_w_ukv, v_mla_w_o, v_conv_w_in,
          v_conv_w, v_conv_w_out, v_mlp_w_up, v_mlp_w_down]
    gs = [g_b_mod, g_norm, big_grads[0], g_gq, g_gkv, big_grads[1], big_grads[2], big_grads[3], big_grads[4],
          g_convw, big_grads[5], big_grads[6], big_grads[7]]
    grads, deltas, new_ms, new_vs = [g_w_mod], [d_w_mod], [nm_w_mod], [nv_w_mod]
    for nm, w, g, m, v in zip(names, ws, gs, ms, vs):
        d, nm_, nv_ = _adamw("adamw_" + nm, w, g, m, v)
        grads.append(g)
        deltas.append(d)
        new_ms.append(nm_)
        new_vs.append(nv_)
    return (loss, grad_x[None], *grads, *deltas, *new_ms, *new_vs)
```

```python
from typing import NamedTuple

import jax
import jax.numpy as jnp
from jax import lax
from jax.experimental import pallas as pl
from jax.experimental.pallas import tpu as pltpu

F32 = jnp.float32
BF16 = jnp.bfloat16
NORM_EPS = 1e-6
ROPE_THETA = 10000.0
QK_NOPE = 128
QK_ROPE = 64
V_HEAD = 128
LANES = 128
QK_PAD = QK_NOPE + LANES
ADAM_LR, ADAM_B1, ADAM_B2, ADAM_EPS, ADAM_WD, ADAM_STEP = 0.001, 0.9, 0.999, 1e-08, 0.01, 10
VMEM_LIMIT_BYTES = 56 * 1024 * 1024
N_CHIPS = 4
MESH_ID = pl.DeviceIdType.MESH
ANY = pl.BlockSpec(memory_space=pl.ANY)
NEG_INF = float("-inf")

DIMS_NN = (((1,), (0,)), ((), ()))
DIMS_NT = (((1,), (1,)), ((), ()))
DIMS_TN = (((0,), (0,)), ((), ()))


def _cparams(*sem):
    return pltpu.CompilerParams(dimension_semantics=sem, vmem_limit_bytes=VMEM_LIMIT_BYTES)


def _row_tile(rows, row_bytes, limit=2 * 1024 * 1024, mult=16):
    if rows * row_bytes <= limit or rows % mult:
        return rows
    best = mult
    t = mult
    while t <= rows:
        if rows % t == 0 and t * row_bytes <= limit:
            best = t
        t += mult
    return best


def _rms(v):
    return lax.rsqrt(jnp.mean(v * v, axis=-1, keepdims=True) + NORM_EPS)


class Rider(NamedTuple):
    operands: tuple
    out_shape: tuple
    aliases: dict
    sems: tuple
    start: object
    finish: object
    mid: object = None


NO_RIDER = Rider((), (), {}, (), None, None)


def _rider_end(rider, r_in, r_out, r_sems):
    if rider.mid is not None:
        rider.mid(r_in, r_out, r_sems)
    rider.finish(r_in, r_out, r_sems)


def _mm(name, a, b, mode, M, N, K, outs, *, a_spec=None, b_spec=None, out_specs=None, epilogue=None,
        extras=(), rider=NO_RIDER, tm=1024, tn=1024, tk=4096):
    tm, tn, tk = min(tm, M), min(tn, N), min(tk, K)
    assert M % tm == 0 and N % tn == 0 and K % tk == 0, (name, M, N, K)
    nk = K // tk
    if a_spec is None:
        a_spec = {"nn": pl.BlockSpec((tm, tk), lambda i, j, k: (i, k)),
                  "nt": pl.BlockSpec((tm, tk), lambda i, j, k: (i, k)),
                  "tn": pl.BlockSpec((tk, tm), lambda i, j, k: (k, i))}[mode]
    else:
        a_spec = a_spec(tm, tn, tk)
    if b_spec is None:
        b_spec = {"nn": pl.BlockSpec((tk, tn), lambda i, j, k: (k, j)),
                  "nt": pl.BlockSpec((tn, tk), lambda i, j, k: (j, k)),
                  "tn": pl.BlockSpec((tk, tn), lambda i, j, k: (k, j))}[mode]
    else:
        b_spec = b_spec(tm, tn, tk)
    if out_specs is None:
        out_specs = [pl.BlockSpec((tm, tn), lambda i, j, k: (i, j)) for _ in outs]
    else:
        out_specs = [s(tm, tn, tk) for s in out_specs]
    dims = {"nn": DIMS_NN, "nt": DIMS_NT, "tn": DIMS_TN}[mode]
    ne, no = len(extras), len(outs)
    n_ri, n_ro = len(rider.operands), len(rider.out_shape)
    grid = (M // tm, N // tn, nk)

    def body(*refs):
        a_ref, b_ref = refs[0], refs[1]
        ex = refs[2:2 + ne]
        r_in = refs[2 + ne:2 + ne + n_ri]
        o = refs[2 + ne + n_ri:2 + ne + n_ri + no]
        r_out = refs[2 + ne + n_ri + no:2 + ne + n_ri + no + n_ro]
        scratch = refs[2 + ne + n_ri + no + n_ro:]
        r_sems = scratch[1:] if nk > 1 else scratch
        ii, jj, kk = pl.program_id(0), pl.program_id(1), pl.program_id(2)

        if rider.start is not None:
            @pl.when((ii == 0) & (jj == 0) & (kk == 0))
            def _():
                rider.start(r_in, r_out, r_sems)

        part = lax.dot_general(a_ref[...].astype(BF16), b_ref[...].astype(BF16), dims,
                               preferred_element_type=F32)

        def finish(total):
            vals = epilogue(total, *[e[...] for e in ex]) if epilogue is not None else (total,)
            for r, v in zip(o, vals):
                r[...] = v.astype(r.dtype)

        if nk == 1:
            finish(part)
        else:
            acc = scratch[0]

            @pl.when(kk == 0)
            def _():
                acc[...] = part

            @pl.when(kk > 0)
            def _():
                acc[...] += part

            @pl.when(kk == nk - 1)
            def _():
                finish(acc[...])

        if rider.finish is not None:
            steps = grid[0] * grid[1] * nk
            if rider.mid is not None and steps >= 4:
                @pl.when((ii * grid[1] + jj) * nk + kk == steps // 2)
                def _():
                    rider.mid(r_in, r_out, r_sems)

            @pl.when((ii == grid[0] - 1) & (jj == grid[1] - 1) & (kk == nk - 1))
            def _():
                if rider.mid is not None and steps < 4:
                    rider.mid(r_in, r_out, r_sems)
                rider.finish(r_in, r_out, r_sems)

    operands = [a, b] + [e[0] for e in extras] + list(rider.operands)
    in_specs = [a_spec, b_spec] + [e[1](tm, tn, tk) for e in extras] + [ANY] * n_ri
    hosted = rider.start is not None
    res = pl.pallas_call(
        body, name=name, grid=grid,
        in_specs=in_specs, out_specs=out_specs + [ANY] * n_ro, out_shape=list(outs) + list(rider.out_shape),
        scratch_shapes=([pltpu.VMEM((tm, tn), F32)] if nk > 1 else []) + list(rider.sems),
        input_output_aliases={2 + ne + i: no + r for i, r in rider.aliases.items()},
        compiler_params=_cparams(*(("arbitrary",) * 3 if hosted else ("parallel", "parallel", "arbitrary"))),
    )(*operands)
    return (res[:no], res[no:]) if hosted else res


def _sds(shape, dtype):
    return jax.ShapeDtypeStruct(tuple(shape), dtype)


def _rope(t, cos_p, sin_lo, sin_hi):
    return t * cos_p + pltpu.roll(t, LANES - QK_ROPE // 2, 1) * sin_lo + pltpu.roll(t, QK_ROPE // 2, 1) * sin_hi


def _rope_t(d, cos_p, sin_lo, sin_hi):
    return d * cos_p + pltpu.roll(d * sin_lo, QK_ROPE // 2, 1) + pltpu.roll(d * sin_hi, LANES - QK_ROPE // 2, 1)


def _vec_spec(d):
    return pl.BlockSpec((1, d), lambda i: (0, 0))


def _fwd_boundary(name, x_prev, y, gate, ng_post, ng_pre, sc, sh):
    S, D = x_prev.shape
    ts = min(256, S)
    has_y = y is not None
    row = pl.BlockSpec((ts, D), lambda i: (i, 0))

    def body(*refs):
        if has_y:
            x_ref, y_ref, g_ref, ngp_ref, ngn_ref, sc_ref, sh_ref, xo_ref, h_ref = refs
            yv = y_ref[...].astype(F32)
            xn = x_ref[...] + g_ref[...] * (yv * _rms(yv) * ngp_ref[...])
            xo_ref[...] = xn
        else:
            x_ref, ngn_ref, sc_ref, sh_ref, h_ref = refs
            xn = x_ref[...]
        hn = xn * _rms(xn) * ngn_ref[...]
        h_ref[...] = (hn * (1.0 + sc_ref[...]) + sh_ref[...]).astype(BF16)

    vec = _vec_spec(D)
    if has_y:
        operands = (x_prev, y, gate, ng_post, ng_pre, sc, sh)
        in_specs = [row, row, vec, vec, vec, vec, vec]
        out_shape = [_sds((S, D), F32), _sds((S, D), BF16)]
        out_specs = [row, row]
    else:
        operands = (x_prev, ng_pre, sc, sh)
        in_specs = [row, vec, vec, vec]
        out_shape = [_sds((S, D), BF16)]
        out_specs = [row]
    return pl.pallas_call(body, name=name, grid=(S // ts,), in_specs=in_specs, out_specs=out_specs,
                          out_shape=out_shape, compiler_params=_cparams("parallel"))(*operands)


def _acc_rows(sums_ref, rows):
    for r, v in rows:
        sums_ref[r:r + 1, :] += jnp.sum(v, axis=0, keepdims=True)


def _post_norm_bwd(dxt, yv, gate, ng_post, sums_ref, dy_ref):
    r1 = _rms(yv)
    yhat = yv * r1
    dn = dxt * gate
    u = dn * ng_post
    dy = r1 * (u - yhat * jnp.mean(u * yhat, axis=-1, keepdims=True))
    dy_ref[...] = dy.astype(dy_ref.dtype)
    _acc_rows(sums_ref, [(3, dxt * (yhat * ng_post)), (4, dn * yhat)])


def _loss_boundary(name, x_prev, y, gate, ng_post, target):
    S, D = x_prev.shape
    ts = min(256, S)
    row = pl.BlockSpec((ts, D), lambda i: (i, 0))
    vec = _vec_spec(D)

    def body(x_ref, y_ref, g_ref, ngp_ref, t_ref, dx_ref, dy_ref, sums_ref, loss_ref):
        @pl.when(pl.program_id(0) == 0)
        def _():
            sums_ref[...] = jnp.zeros_like(sums_ref)
            loss_ref[...] = jnp.zeros_like(loss_ref)

        yv = y_ref[...].astype(F32)
        xf = x_ref[...] + g_ref[...] * (yv * _rms(yv) * ngp_ref[...])
        err = xf - t_ref[...]
        loss_ref[...] += 0.5 * jnp.sum(jnp.mean(err * err, axis=-1, keepdims=True))
        dxt = err / D
        dx_ref[...] = dxt
        _post_norm_bwd(dxt, yv, g_ref[...], ngp_ref[...], sums_ref, dy_ref)

    return pl.pallas_call(
        body, name=name, grid=(S // ts,),
        in_specs=[row, row, vec, vec, row],
        out_specs=[row, row, pl.BlockSpec((8, D), lambda i: (0, 0)), pl.BlockSpec((8, LANES), lambda i: (0, 0))],
        out_shape=[_sds((S, D), F32), _sds((S, D), BF16), _sds((8, D), F32), _sds((8, LANES), F32)],
        compiler_params=_cparams("arbitrary"))(x_prev, y, gate, ng_post, target)


def _bwd_boundary(name, dx_new, dh, x_new, y, gate, ng_post, ng_pre, sc, rider=NO_RIDER):
    S, D = x_new.shape
    ts = min(256, S)
    has_y = y is not None
    row = pl.BlockSpec((ts, D), lambda i: (i, 0))
    vec = _vec_spec(D)
    n_in, n_out = (8, 3) if has_y else (5, 2)
    n_ri, n_ro = len(rider.operands), len(rider.out_shape)

    def body(*refs):
        r_in = refs[n_in:n_in + n_ri]
        r_out = refs[n_in + n_ri + n_out:n_in + n_ri + n_out + n_ro]
        r_sems = refs[n_in + n_ri + n_out + n_ro:]
        own = refs[:n_in] + refs[n_in + n_ri:n_in + n_ri + n_out]
        if has_y:
            dxn_ref, dh_ref, x_ref, y_ref, g_ref, ngp_ref, ngn_ref, sc_ref, dxo_ref, dy_ref, sums_ref = own
        else:
            dxn_ref, dh_ref, x_ref, ngn_ref, sc_ref, dxo_ref, sums_ref = own

        @pl.when(pl.program_id(0) == 0)
        def _():
            sums_ref[...] = jnp.zeros_like(sums_ref)
            if rider.start is not None:
                rider.start(r_in, r_out, r_sems)

        xv = x_ref[...]
        dhv = dh_ref[...].astype(F32)
        ngn = ngn_ref[...]
        r2 = _rms(xv)
        xhat = xv * r2
        dn_pre = dhv * (1.0 + sc_ref[...])
        u2 = dn_pre * ngn
        dxt = dxn_ref[...] + r2 * (u2 - xhat * jnp.mean(u2 * xhat, axis=-1, keepdims=True))
        dxo_ref[...] = dxt
        _acc_rows(sums_ref, [(0, dhv), (1, dhv * (xhat * ngn)), (2, dn_pre * xhat)])
        if has_y:
            _post_norm_bwd(dxt, y_ref[...].astype(F32), g_ref[...], ngp_ref[...], sums_ref, dy_ref)

        if rider.finish is not None:
            @pl.when(pl.program_id(0) == S // ts - 1)
            def _():
                _rider_end(rider, r_in, r_out, r_sems)

    sums_spec = pl.BlockSpec((8, D), lambda i: (0, 0))
    if has_y:
        operands = (dx_new, dh, x_new, y, gate, ng_post, ng_pre, sc)
        in_specs = [row, row, row, row, vec, vec, vec, vec]
        out_shape = [_sds((S, D), F32), _sds((S, D), BF16), _sds((8, D), F32)]
        out_specs = [row, row, sums_spec]
    else:
        operands = (dx_new, dh, x_new, ng_pre, sc)
        in_specs = [row, row, row, vec, vec]
        out_shape = [_sds((S, D), F32), _sds((8, D), F32)]
        out_specs = [row, sums_spec]
    res = pl.pallas_call(
        body, name=name, grid=(S // ts,), in_specs=in_specs + [ANY] * n_ri, out_specs=out_specs + [ANY] * n_ro,
        out_shape=out_shape + list(rider.out_shape), scratch_shapes=list(rider.sems),
        input_output_aliases={n_in + i: n_out + o for i, o in rider.aliases.items()},
        compiler_params=_cparams("arbitrary"))(*operands, *rider.operands)
    return (*res[:n_out], res[n_out:])


def _latent_fwd(lat, g_q, g_kv, rope_tabs, rank):
    S, W = lat.shape
    ts = min(256, S)
    tab = pl.BlockSpec((ts, LANES), lambda i: (i, 0))

    def body(lat_ref, gq_ref, gkv_ref, cos_ref, slo_ref, shi_ref, cq_ref, ckv_ref, kr_ref):
        lq = lat_ref[:, 0:rank]
        lkv = lat_ref[:, rank:2 * rank]
        cq_ref[...] = (lq * _rms(lq) * gq_ref[...]).astype(BF16)
        ckv_ref[...] = (lkv * _rms(lkv) * gkv_ref[...]).astype(BF16)
        kr_ref[...] = _rope(lat_ref[:, 2 * rank:W], cos_ref[...], slo_ref[...], shi_ref[...]).astype(BF16)

    return pl.pallas_call(
        body, name="mla_latent_fwd", grid=(S // ts,),
        in_specs=[pl.BlockSpec((ts, W), lambda i: (i, 0)), _vec_spec(rank), _vec_spec(rank), tab, tab, tab],
        out_specs=[pl.BlockSpec((ts, rank), lambda i: (i, 0)), pl.BlockSpec((ts, rank), lambda i: (i, 0)), tab],
        out_shape=[_sds((S, rank), BF16), _sds((S, rank), BF16), _sds((S, LANES), BF16)],
        compiler_params=_cparams("parallel"))(lat, g_q, g_kv, *rope_tabs)


def _latent_bwd(lat, dcq, dckv, dkr, g_q, g_kv, rope_tabs, rank):
    S, W = lat.shape
    ts = min(256, S)
    tab = pl.BlockSpec((ts, LANES), lambda i: (i, 0))
    half = pl.BlockSpec((ts, rank), lambda i: (i, 0))

    def body(lat_ref, dcq_ref, dckv_ref, dkr_ref, gq_ref, gkv_ref, cos_ref, slo_ref, shi_ref, dlat_ref, sums_ref):
        @pl.when(pl.program_id(0) == 0)
        def _():
            sums_ref[...] = jnp.zeros_like(sums_ref)

        def norm_bwd(v, dn, g, r):
            rr = _rms(v)
            vhat = v * rr
            u = dn * g
            sums_ref[r:r + 1, :] += jnp.sum(dn * vhat, axis=0, keepdims=True)
            return rr * (u - vhat * jnp.mean(u * vhat, axis=-1, keepdims=True))

        dlat_ref[:, 0:rank] = norm_bwd(lat_ref[:, 0:rank], dcq_ref[...], gq_ref[...], 0).astype(BF16)
        dlat_ref[:, rank:2 * rank] = norm_bwd(lat_ref[:, rank:2 * rank], dckv_ref[...], gkv_ref[...], 1).astype(BF16)
        dlat_ref[:, 2 * rank:W] = _rope_t(dkr_ref[...], cos_ref[...], slo_ref[...], shi_ref[...]).astype(BF16)

    return pl.pallas_call(
        body, name="mla_latent_bwd", grid=(S // ts,),
        in_specs=[pl.BlockSpec((ts, W), lambda i: (i, 0)), half, half, tab, _vec_spec(rank), _vec_spec(rank),
                  tab, tab, tab],
        out_specs=[pl.BlockSpec((ts, W), lambda i: (i, 0)), pl.BlockSpec((8, rank), lambda i: (0, 0))],
        out_shape=[_sds((S, W), BF16), _sds((8, rank), F32)],
        compiler_params=_cparams("arbitrary"))(lat, dcq, dckv, dkr, g_q, g_kv, *rope_tabs)


def _attn_tiles(S):
    t = min(512, S)
    return t, S // t


def _causal_mask(t):
    return lax.broadcasted_iota(jnp.int32, (t, t), 1) <= lax.broadcasted_iota(jnp.int32, (t, t), 0)


def _causal_pairs(nb, q_major):
    if q_major:
        pairs = [(qi, ki) for qi in range(nb) for ki in range(qi + 1)]
    else:
        pairs = [(qi, ki) for ki in range(nb) for qi in range(ki, nb)]
    return jnp.array([p[0] for p in pairs], jnp.int32), jnp.array([p[1] for p in pairs], jnp.int32), len(pairs)


def _heads_per_step(heads):
    return 2 if heads % 2 == 0 else 1


def _attn_fwd_tri(q, kv, kr, heads, scale, rider=NO_RIDER):
    S = q.shape[0]
    t, nb = _attn_tiles(S)
    G = _heads_per_step(heads)
    q_tab, k_tab, n_pairs = _causal_pairs(nb, True)
    n_ri, n_ro = len(rider.operands), len(rider.out_shape)

    def body(qt_ref, kt_ref, *refs):
        q_ref, kv_ref, kr_ref = refs[:3]
        r_in = refs[3:3 + n_ri]
        o_ref, lse_ref = refs[3 + n_ri:5 + n_ri]
        r_out = refs[5 + n_ri:5 + n_ri + n_ro]
        m_scr, acc_scr = refs[5 + n_ri + n_ro:7 + n_ri + n_ro]
        r_sems = refs[7 + n_ri + n_ro:]
        h, p = pl.program_id(0), pl.program_id(1)
        qi, ki = qt_ref[p], kt_ref[p]

        if rider.start is not None:
            @pl.when((h == 0) & (p == 0))
            def _():
                rider.start(r_in, r_out, r_sems)

        @pl.when(ki == 0)
        def _():
            m_scr[...] = jnp.full_like(m_scr, NEG_INF)
            acc_scr[...] = jnp.zeros_like(acc_scr)

        def step(diagonal):
            ones = jnp.ones((t, LANES), BF16)
            for g in range(G):
                kcat = jnp.concatenate([kv_ref[:, g * QK_PAD:g * QK_PAD + QK_NOPE], kr_ref[...]], axis=1)
                vext = jnp.concatenate([kv_ref[:, g * QK_PAD + QK_NOPE:(g + 1) * QK_PAD], ones], axis=1)
                s = lax.dot_general(q_ref[:, g * QK_PAD:(g + 1) * QK_PAD], kcat, DIMS_NT,
                                    preferred_element_type=F32) * scale
                if diagonal:
                    s = jnp.where(_causal_mask(t), s, NEG_INF)
                m_prev = m_scr[g]
                m_new = jnp.maximum(m_prev, jnp.max(s, axis=-1, keepdims=True))
                alpha = jnp.exp(m_prev - m_new)
                pr = jnp.exp(s - jnp.tile(m_new, (1, t // LANES)))
                acc_scr[g] = jnp.tile(alpha, (1, 2)) * acc_scr[g] + lax.dot_general(
                    pr.astype(BF16), vext, DIMS_NN, preferred_element_type=F32)
                m_scr[g] = m_new

        @pl.when(ki < qi)
        def _():
            step(False)

        @pl.when(ki == qi)
        def _():
            step(True)
            for g in range(G):
                acc = acc_scr[g]
                o_ref[:, g * V_HEAD:(g + 1) * V_HEAD] = (acc[:, 0:V_HEAD] / acc[:, V_HEAD:2 * V_HEAD]).astype(BF16)
                lse_ref[g] = m_scr[g] + jnp.log(acc[:, V_HEAD:2 * V_HEAD])

        if rider.finish is not None:
            halfway = rider.mid is not None and heads // G >= 2
            if halfway:
                @pl.when((h == heads // G // 2) & (p == 0))
                def _():
                    rider.mid(r_in, r_out, r_sems)

            @pl.when((h == heads // G - 1) & (p == n_pairs - 1))
            def _():
                if halfway:
                    rider.finish(r_in, r_out, r_sems)
                else:
                    _rider_end(rider, r_in, r_out, r_sems)

    res = pl.pallas_call(
        body, name="mla_attn_fwd",
        grid_spec=pltpu.PrefetchScalarGridSpec(
            num_scalar_prefetch=2, grid=(heads // G, n_pairs),
            in_specs=[pl.BlockSpec((t, G * QK_PAD), lambda h, p, qt, kt: (qt[p], h)),
                      pl.BlockSpec((t, G * QK_PAD), lambda h, p, qt, kt: (kt[p], h)),
                      pl.BlockSpec((t, LANES), lambda h, p, qt, kt: (kt[p], 0))] + [ANY] * n_ri,
            out_specs=[pl.BlockSpec((t, G * V_HEAD), lambda h, p, qt, kt: (qt[p], h)),
                       pl.BlockSpec((G, t, LANES), lambda h, p, qt, kt: (h, qt[p], 0))] + [ANY] * n_ro,
            scratch_shapes=[pltpu.VMEM((G, t, LANES), F32), pltpu.VMEM((G, t, 2 * V_HEAD), F32)] + list(rider.sems)),
        out_shape=[_sds((S, heads * V_HEAD), BF16), _sds((heads, S, LANES), F32)] + list(rider.out_shape),
        input_output_aliases={5 + i: 2 + o for i, o in rider.aliases.items()},
        compiler_params=_cparams("arbitrary", "arbitrary"))(q_tab, k_tab, q, kv, kr, *rider.operands)
    return res[0], res[1], res[2:]


def _attn_bwd_tri(q, kv, kr, o, do, lse, rope_tabs, heads, scale, rider=NO_RIDER):
    S = q.shape[0]
    t, nb = _attn_tiles(S)
    G = _heads_per_step(heads)
    q_tab, k_tab, n_pairs = _causal_pairs(nb, False)
    n_ri, n_ro = len(rider.operands), len(rider.out_shape)
    rep = t // LANES

    tabs = jnp.concatenate(rope_tabs, axis=1)

    def body(qt_ref, kt_ref, *refs):
        q_ref, kv_ref, kr_ref, o_ref, do_ref, lse_ref, tabs_ref = refs[:7]
        cos_ref, slo_ref, shi_ref = (tabs_ref.at[:, pl.ds(i * LANES, LANES)] for i in range(3))
        r_in = refs[7:7 + n_ri]
        dq_ref, dkv_ref, dkr_ref = refs[7 + n_ri:10 + n_ri]
        r_out = refs[10 + n_ri:10 + n_ri + n_ro]
        dq_scr, dk_scr, dv_scr, dkr_scr, delta_scr = refs[10 + n_ri + n_ro:15 + n_ri + n_ro]
        r_sems = refs[15 + n_ri + n_ro:]
        h, p = pl.program_id(0), pl.program_id(1)
        qi, ki = qt_ref[p], kt_ref[p]
        q_rows = pl.ds(pl.multiple_of(qi * t, t), t)
        k_rows = pl.ds(pl.multiple_of(ki * t, t), t)

        @pl.when(ki == 0)
        def _():
            for g in range(G):
                cols = slice(g * V_HEAD, (g + 1) * V_HEAD)
                d = jnp.sum(do_ref[:, cols].astype(F32) * o_ref[:, cols].astype(F32), axis=-1, keepdims=True)
                delta_scr[g, q_rows, :] = jnp.broadcast_to(d, (t, LANES))

        if rider.start is not None:
            @pl.when((h == 0) & (p == 0))
            def _():
                rider.start(r_in, r_out, r_sems)

        @pl.when(p == 0)
        def _():
            dq_scr[...] = jnp.zeros_like(dq_scr)

        @pl.when((h == 0) & (p == 0))
        def _():
            dkr_scr[...] = jnp.zeros_like(dkr_scr)

        @pl.when(qi == ki)
        def _():
            dk_scr[...] = jnp.zeros_like(dk_scr)
            dv_scr[...] = jnp.zeros_like(dv_scr)

        def step(diagonal):
            for g in range(G):
                qv = q_ref[:, g * QK_PAD:(g + 1) * QK_PAD]
                kcat = jnp.concatenate([kv_ref[:, g * QK_PAD:g * QK_PAD + QK_NOPE], kr_ref[...]], axis=1)
                s = lax.dot_general(qv, kcat, DIMS_NT, preferred_element_type=F32) * scale
                pr = jnp.exp(s - jnp.tile(lse_ref[g], (1, rep)))
                if diagonal:
                    pr = jnp.where(_causal_mask(t), pr, 0.0)
                dov = do_ref[:, g * V_HEAD:(g + 1) * V_HEAD]
                dv_scr[g] += lax.dot_general(pr.astype(BF16), dov, DIMS_TN, preferred_element_type=F32)
                dp = lax.dot_general(dov, kv_ref[:, g * QK_PAD + QK_NOPE:(g + 1) * QK_PAD], DIMS_NT,
                                     preferred_element_type=F32)
                ds = (pr * (dp - jnp.tile(delta_scr[g, q_rows, :], (1, rep))) * scale).astype(BF16)
                dk_scr[g] += lax.dot_general(ds, qv, DIMS_TN, preferred_element_type=F32)
                dq_scr[q_rows, g * QK_PAD:(g + 1) * QK_PAD] += lax.dot_general(ds, kcat, DIMS_NN,
                                                                               preferred_element_type=F32)

        @pl.when(qi > ki)
        def _():
            step(False)

        @pl.when(qi == ki)
        def _():
            step(True)
            for g in range(G):
                dqv = dq_scr[q_rows, g * QK_PAD:(g + 1) * QK_PAD]
                dq_ref[q_rows, g * QK_PAD:(g + 1) * QK_PAD] = jnp.concatenate(
                    [dqv[:, 0:QK_NOPE], _rope_t(dqv[:, QK_NOPE:QK_PAD], cos_ref[...], slo_ref[...], shi_ref[...])],
                    axis=1).astype(BF16)

        @pl.when(qi == nb - 1)
        def _():
            for g in range(G):
                dkv_ref[:, g * QK_PAD:(g + 1) * QK_PAD] = jnp.concatenate(
                    [dk_scr[g][:, 0:QK_NOPE], dv_scr[g]], axis=1).astype(BF16)
                dkr_scr[k_rows, :] += dk_scr[g][:, QK_NOPE:QK_PAD]

        @pl.when((h == heads // G - 1) & (p == n_pairs - 1))
        def _():
            dkr_ref[...] = dkr_scr[...]
            if rider.finish is not None:
                _rider_end(rider, r_in, r_out, r_sems)

    q_blk = lambda w: pl.BlockSpec((t, G * w), lambda h, p, qt, kt: (qt[p], h))
    stat = pl.BlockSpec((G, t, LANES), lambda h, p, qt, kt: (h, qt[p], 0))
    tab = pl.BlockSpec((t, LANES), lambda h, p, qt, kt: (kt[p], 0))
    res = pl.pallas_call(
        body, name="mla_attn_bwd",
        grid_spec=pltpu.PrefetchScalarGridSpec(
            num_scalar_prefetch=2, grid=(heads // G, n_pairs),
            in_specs=[q_blk(QK_PAD),
                      pl.BlockSpec((t, G * QK_PAD), lambda h, p, qt, kt: (kt[p], h)),
                      tab, q_blk(V_HEAD), q_blk(V_HEAD), stat,
                      pl.BlockSpec((t, 3 * LANES), lambda h, p, qt, kt: (kt[p], 0))] + [ANY] * n_ri,
            out_specs=[pl.BlockSpec((S, G * QK_PAD), lambda h, p, qt, kt: (0, h)),
                       pl.BlockSpec((t, G * QK_PAD), lambda h, p, qt, kt: (kt[p], h)),
                       pl.BlockSpec((S, LANES), lambda h, p, qt, kt: (0, 0))] + [ANY] * n_ro,
            scratch_shapes=[pltpu.VMEM((S, G * QK_PAD), F32), pltpu.VMEM((G, t, QK_PAD), F32),
                            pltpu.VMEM((G, t, V_HEAD), F32), pltpu.VMEM((S, LANES), F32),
                            pltpu.VMEM((G, S, LANES), F32)] + list(rider.sems)),
        out_shape=[_sds((S, heads * QK_PAD), BF16), _sds((S, heads * QK_PAD), BF16), _sds((S, LANES), F32)]
        + list(rider.out_shape),
        input_output_aliases={9 + i: 3 + o for i, o in rider.aliases.items()},
        compiler_params=_cparams("arbitrary", "arbitrary"))(q_tab, k_tab, q, kv, kr, o, do, lse, tabs,
                                                            *rider.operands)
    return res[0], res[1], res[2], res[3:]


def _shift_down(z, n, rows):
    return jnp.where(rows >= n, pltpu.roll(z, n, 0), 0.0)


def _shift_up(z, n, rows, S):
    return jnp.where(rows < S - n, pltpu.roll(z, S - n, 0), 0.0)


def _conv_specs(S, tc):
    strip = lambda p: pl.BlockSpec((None, S, tc), lambda j: (p, 0, j))
    return strip(0), strip(1), strip(2), pl.BlockSpec((3, tc), lambda j: (0, j))


def _conv_fwd(proj3, w):
    _, S, D = proj3.shape
    tc = LANES

    def body(b_ref, c_ref, u_ref, w_ref, out_ref):
        z = c_ref[...].astype(F32) * u_ref[...].astype(F32)
        rows = lax.broadcasted_iota(jnp.int32, (S, tc), 0)
        zc = w_ref[0:1, :] * _shift_down(z, 2, rows) + w_ref[1:2, :] * _shift_down(z, 1, rows) + w_ref[2:3, :] * z
        out_ref[...] = (b_ref[...].astype(F32) * zc).astype(BF16)

    return pl.pallas_call(
        body, name="conv_fwd", grid=(D // tc,), in_specs=list(_conv_specs(S, tc)),
        out_specs=pl.BlockSpec((S, tc), lambda j: (0, j)), out_shape=_sds((S, D), BF16),
        compiler_params=_cparams("parallel"))(proj3, proj3, proj3, w)


def _conv_bwd(dbz, proj3, w):
    _, S, D = proj3.shape
    tc = LANES

    def body(d_ref, b_ref, c_ref, u_ref, w_ref, dp_ref, dw_ref):
        cv, uv, dv = c_ref[...].astype(F32), u_ref[...].astype(F32), d_ref[...].astype(F32)
        z = cv * uv
        rows = lax.broadcasted_iota(jnp.int32, (S, tc), 0)
        z1, z2 = _shift_down(z, 1, rows), _shift_down(z, 2, rows)
        zc = w_ref[0:1, :] * z2 + w_ref[1:2, :] * z1 + w_ref[2:3, :] * z
        dp_ref[0] = (dv * zc).astype(BF16)
        dzc = dv * b_ref[...].astype(F32)
        dz = w_ref[2:3, :] * dzc + w_ref[1:2, :] * _shift_up(dzc, 1, rows, S) + w_ref[0:1, :] * _shift_up(dzc, 2, rows, S)
        dp_ref[1] = (dz * uv).astype(BF16)
        dp_ref[2] = (dz * cv).astype(BF16)
        dw_ref[0:1, :] = jnp.sum(dzc * z2, axis=0, keepdims=True)
        dw_ref[1:2, :] = jnp.sum(dzc * z1, axis=0, keepdims=True)
        dw_ref[2:3, :] = jnp.sum(dzc * z, axis=0, keepdims=True)

    sb, sc_, su, sw = _conv_specs(S, tc)
    return pl.pallas_call(
        body, name="conv_bwd", grid=(D // tc,),
        in_specs=[pl.BlockSpec((S, tc), lambda j: (0, j)), sb, sc_, su, sw],
        out_specs=[pl.BlockSpec((3, S, tc), lambda j: (0, 0, j)), pl.BlockSpec((3, tc), lambda j: (0, j))],
        out_shape=[_sds((3, S, D), BF16), _sds((3, D), F32)],
        compiler_params=_cparams("parallel"))(dbz, proj3, proj3, proj3, w)


def _silu(c_all):
    def body(c_ref, o_ref):
        cv = c_ref[...]
        o_ref[...] = cv * (1.0 / (1.0 + jnp.exp(-cv)))

    vm = pl.BlockSpec(memory_space=pltpu.VMEM)
    return pl.pallas_call(body, name="cond_silu", in_specs=[vm], out_specs=vm, out_shape=_sds(c_all.shape, F32))(c_all)


def _mod_fwd(cond, w_mod, b_cols):
    L, D, ncol = w_mod.shape
    B = cond.shape[0]
    tk, tn = min(512, D), min(1024, ncol)
    nk = D // tk

    def body(c_ref, w_ref, b_ref, out_ref, acc):
        kk = pl.program_id(2)
        part = lax.dot_general(c_ref[...].astype(BF16), w_ref[...].astype(BF16), DIMS_NN, preferred_element_type=F32)

        @pl.when(kk == 0)
        def _():
            acc[...] = part

        @pl.when(kk > 0)
        def _():
            acc[...] += part

        @pl.when(kk == nk - 1)
        def _():
            out_ref[...] = acc[...] + b_ref[...]

    return pl.pallas_call(
        body, name="mod_fwd", grid=(L, ncol // tn, nk),
        in_specs=[pl.BlockSpec((B, tk), lambda l, j, k: (0, k)),
                  pl.BlockSpec((None, tk, tn), lambda l, j, k: (l, k, j)),
                  pl.BlockSpec((None, 1, tn), lambda l, j, k: (l, 0, j))],
        out_specs=pl.BlockSpec((None, B, tn), lambda l, j, k: (l, 0, j)),
        out_shape=_sds((L, B, ncol), F32),
        scratch_shapes=[pltpu.VMEM((B, tn), F32)],
        compiler_params=_cparams("parallel", "parallel", "arbitrary"))(cond, w_mod, b_cols)


def _adamw_math(w, g, m, v):
    m = ADAM_B1 * m + (1.0 - ADAM_B1) * g
    v = ADAM_B2 * v + (1.0 - ADAM_B2) * (g * g)
    m_hat = m / (1.0 - ADAM_B1 ** ADAM_STEP)
    v_hat = v / (1.0 - ADAM_B2 ** ADAM_STEP)
    delta = -ADAM_LR * (m_hat / (jnp.sqrt(v_hat) + ADAM_EPS) + ADAM_WD * w)
    return delta, m, v


def _adamw(name, w, g, m, v):
    shape = w.shape
    cols = shape[-1] if w.ndim <= 3 else shape[-2] * shape[-1]
    rows = w.size // cols
    w2, g2, m2, v2 = (t.reshape(rows, cols) for t in (w, g, m, v))
    tr = _row_tile(rows, cols * 4, limit=1024 * 1024, mult=8)
    spec = pl.BlockSpec((tr, cols), lambda i: (i, 0))

    def body(w_ref, g_ref, m_ref, v_ref, d_ref, nm_ref, nv_ref):
        d, nm, nv = _adamw_math(w_ref[...], g_ref[...], m_ref[...], v_ref[...])
        d_ref[...] = d
        nm_ref[...] = nm
        nv_ref[...] = nv

    outs = pl.pallas_call(body, name=name, grid=(rows // tr,), in_specs=[spec] * 4, out_specs=[spec] * 3,
                          out_shape=[_sds((rows, cols), F32)] * 3, compiler_params=_cparams("parallel"))(w2, g2, m2, v2)
    return tuple(t.reshape(shape) for t in outs)


def _adamw_mod(w, cond_t, dmod_cols, m, v, rider=NO_RIDER):
    L, D, ncol = w.shape
    B = cond_t.shape[1]
    tr, tc = min(256, D), min(1024, ncol)
    blk = pl.BlockSpec((None, tr, tc), lambda l, i, j: (l, i, j))
    grid = (L, D // tr, ncol // tc)
    n_ri, n_ro = len(rider.operands), len(rider.out_shape)

    def body(*refs):
        w_ref, ct_ref, dm_ref, m_ref, v_ref = refs[:5]
        r_in = refs[5:5 + n_ri]
        g_ref, d_ref, nm_ref, nv_ref = refs[5 + n_ri:9 + n_ri]
        r_out = refs[9 + n_ri:9 + n_ri + n_ro]
        r_sems = refs[9 + n_ri + n_ro:]
        ids = [pl.program_id(a) for a in range(3)]

        if rider.start is not None:
            @pl.when((ids[0] == 0) & (ids[1] == 0) & (ids[2] == 0))
            def _():
                rider.start(r_in, r_out, r_sems)

        g = lax.dot_general(ct_ref[...], dm_ref[...], DIMS_NN, precision=lax.Precision.HIGHEST,
                            preferred_element_type=F32)
        d, nm, nv = _adamw_math(w_ref[...], g, m_ref[...], v_ref[...])
        g_ref[...] = g
        d_ref[...] = d
        nm_ref[...] = nm
        nv_ref[...] = nv

        if rider.finish is not None:
            @pl.when((ids[0] == grid[0] - 1) & (ids[1] == grid[1] - 1) & (ids[2] == grid[2] - 1))
            def _():
                _rider_end(rider, r_in, r_out, r_sems)

    hosted = rider.start is not None
    res = pl.pallas_call(
        body, name="adamw_w_mod", grid=grid,
        in_specs=[blk, pl.BlockSpec((tr, B), lambda l, i, j: (i, 0)),
                  pl.BlockSpec((None, B, tc), lambda l, i, j: (l, 0, j)), blk, blk] + [ANY] * n_ri,
        out_specs=[blk] * 4 + [ANY] * n_ro, out_shape=[_sds((L, D, ncol), F32)] * 4 + list(rider.out_shape),
        scratch_shapes=list(rider.sems), input_output_aliases={5 + i: 4 + o for i, o in rider.aliases.items()},
        compiler_params=_cparams(*(("arbitrary",) * 3 if hosted else ("parallel",) * 3)))(
            w, cond_t, dmod_cols, m, v, *rider.operands)
    return (*res[:4], res[4:])


def _cast_into_full(name, ws, kinds, k_idx, rider=NO_RIDER):
    L, R, C = ws[0].shape
    assert all(w.shape == (L, R, C) for w in ws)
    n = len(ws)
    Rh = R // 2
    tr = _row_tile(Rh, C * 4)
    grid = (L, 2, Rh // tr)
    out_shape, out_specs = [], []
    for kind in kinds:
        if kind == "row":
            out_shape.append(_sds((L, N_CHIPS, 2, Rh, C), BF16))
            out_specs.append(pl.BlockSpec((None, None, None, tr, C), lambda l, h, i, k_ref: (l, k_ref[0], h, i, 0)))
        else:
            out_shape.append(_sds((L, 2, Rh, N_CHIPS * C), BF16))
            out_specs.append(pl.BlockSpec((None, None, tr, C), lambda l, h, i, k_ref: (l, h, i, k_ref[0])))
    n_ri, n_ro = len(rider.operands), len(rider.out_shape)

    def body(k_ref, *refs):
        r_in = refs[n:n + n_ri]
        r_out = refs[2 * n + n_ri:2 * n + n_ri + n_ro]
        r_sems = refs[2 * n + n_ri + n_ro:]
        ids = [pl.program_id(a) for a in range(3)]
        if rider.start is not None:
            @pl.when((ids[0] == 0) & (ids[1] == 0) & (ids[2] == 0))
            def _():
                rider.start(r_in, r_out, r_sems)
        for a in range(n):
            refs[n + n_ri + a][...] = refs[a][...].astype(BF16)
        if rider.finish is not None:
            @pl.when((ids[0] == grid[0] - 1) & (ids[1] == grid[1] - 1) & (ids[2] == grid[2] - 1))
            def _():
                _rider_end(rider, r_in, r_out, r_sems)

    hosted = rider.start is not None
    res = pl.pallas_call(
        body, name=name,
        grid_spec=pltpu.PrefetchScalarGridSpec(
            num_scalar_prefetch=1, grid=grid,
            in_specs=[pl.BlockSpec((None, None, tr, C), lambda l, h, i, k_ref: (l, h, i, 0))] * n + [ANY] * n_ri,
            out_specs=out_specs + [ANY] * n_ro, scratch_shapes=list(rider.sems)),
        out_shape=out_shape + list(rider.out_shape),
        input_output_aliases={1 + n + i: n + o for i, o in rider.aliases.items()},
        compiler_params=_cparams(*(("arbitrary",) * 3 if hosted else ("parallel",) * 3)))(
            k_idx, *[w.reshape(L, 2, Rh, C) for w in ws], *rider.operands)
    return res[:n], res[n:]


def _pair_sum(name, g5, ra, c_idx):
    L, A, _, Rh, Cc = g5.shape
    tr = _row_tile(Rh, Cc * 4)

    def body(c_ref, g_ref, r_ref, o_ref):
        o_ref[...] = (g_ref[...].astype(F32) + r_ref[...].astype(F32)).astype(BF16)

    blk = pl.BlockSpec((None, None, tr, Cc), lambda l, a, i, c_ref: (l, a, i, 0))
    return pl.pallas_call(
        body, name=name,
        grid_spec=pltpu.PrefetchScalarGridSpec(
            num_scalar_prefetch=1, grid=(L, A, Rh // tr),
            in_specs=[pl.BlockSpec((None, None, None, tr, Cc), lambda l, a, i, c_ref: (l, a, c_ref[0], i, 0)), blk],
            out_specs=blk),
        out_shape=_sds((L, A, Rh, Cc), BF16),
        compiler_params=_cparams("parallel", "parallel", "parallel"))(c_idx, g5, ra)


def _chip_sum(name, p, rb, kc_idx, kind, layer=0, n_layers=1, prev=None):
    _, A, Rh, Cc = p.shape
    C = rb.shape[-1]
    tr = _row_tile(Rh, C * 4)
    if kind == "row":
        own = pl.BlockSpec((None, None, tr, C), lambda i, kc: (0, kc[0], i, 0))
    else:
        own = pl.BlockSpec((None, None, tr, C), lambda i, kc: (0, 0, i, kc[0]))
    peer = lambda j: pl.BlockSpec((None, None, tr, C), lambda i, kc: (j, 0, i, 0))

    def body(kc_ref, p_ref, r0_ref, r1_ref, r2_ref, *rest):
        o_ref = rest[-1]
        o_ref[...] = ((p_ref[...].astype(F32) + r0_ref[...].astype(F32)) + r1_ref[...].astype(F32)) + r2_ref[...].astype(F32)

    operands = [kc_idx, p, rb, rb, rb] + ([prev] if prev is not None else [])
    return pl.pallas_call(
        body, name=name,
        grid_spec=pltpu.PrefetchScalarGridSpec(
            num_scalar_prefetch=1, grid=(Rh // tr,),
            in_specs=[own, peer(0), peer(1), peer(2)] + ([ANY] if prev is not None else []),
            out_specs=pl.BlockSpec((None, None, tr, C), lambda i, kc: (layer, kc[1], i, 0))),
        out_shape=_sds((n_layers, 2, Rh, C), F32),
        input_output_aliases={5: 0} if prev is not None else {},
        compiler_params=_cparams("parallel"))(*operands)


def _mesh_place():
    x, y, c = lax.axis_index("x"), lax.axis_index("y"), lax.axis_index("c")
    chips = [(1 - x, y), (x, 1 - y), (1 - x, 1 - y)]
    return x, y, c, chips


def _remote(src, dst, send_sem, recv_sem, to):
    return pltpu.make_async_remote_copy(src_ref=src, dst_ref=dst, send_sem=send_sem, recv_sem=recv_sem,
                                        device_id=to, device_id_type=MESH_ID)


def _small_allgather(name, v, with_sum=False, rider=NO_RIDER):
    R, N = v.shape
    n_ri, n_ro, n_own = len(rider.operands), len(rider.out_shape), 2 if with_sum else 1

    def body(*refs):
        r_in = refs[1:1 + n_ri]
        r_out = refs[1 + n_ri + n_own:1 + n_ri + n_own + n_ro]
        r_sems = refs[1 + n_ri + n_own + n_ro + 3:]
        own = (refs[0],) + refs[1 + n_ri:1 + n_ri + n_own] + refs[1 + n_ri + n_own + n_ro:1 + n_ri + n_own + n_ro + 3]
        if with_sum:
            x_ref, out_ref, sum_ref, send_sems, recv_sems, local_sem = own
        else:
            x_ref, out_ref, send_sems, recv_sems, local_sem = own
        if rider.start is not None:
            rider.start(r_in, r_out, r_sems)
        x, y, c, chips = _mesh_place()
        me, sibling = (x, y, c), (x, y, 1 - c)

        def rows(px, py, pc):
            return out_ref.at[pl.ds((4 * px + 2 * py + pc) * R, R), :]

        def copy(k, block, to, src=None):
            return _remote(rows(*block) if src is None else src, rows(*block), send_sems.at[k], recv_sems.at[k], to)

        mine = pltpu.make_async_copy(x_ref, rows(*me), local_sem)
        mine.start()
        first = [copy(0, me, sibling, src=x_ref)]
        first += [copy(1 + j, me, (*chip, c), src=x_ref) for j, chip in enumerate(chips)]
        for cp in first:
            cp.start()
        passed = [copy(4 + j, (*chip, c), sibling) for j, chip in enumerate(chips)]
        for j, chip in enumerate(chips):
            copy(1 + j, (*chip, c), me).wait_recv()
            passed[j].start()
        copy(0, sibling, me).wait_recv()
        for j, chip in enumerate(chips):
            copy(4 + j, (*chip, 1 - c), me).wait_recv()
        for cp in first + passed:
            cp.wait_send()
        mine.wait()
        if with_sum:
            total = out_ref[0:R, :]
            for p in range(1, 8):
                total = total + out_ref[p * R:(p + 1) * R, :]
            sum_ref[...] = total
        if rider.finish is not None:
            _rider_end(rider, r_in, r_out, r_sems)

    vm = pl.BlockSpec(memory_space=pltpu.VMEM)
    out_shape = [_sds((8 * R, N), F32)] + ([_sds((R, N), F32)] if with_sum else [])
    res = pl.pallas_call(
        body, name=name, out_shape=out_shape + list(rider.out_shape), in_specs=[vm] + [ANY] * n_ri,
        out_specs=[vm] * n_own + [ANY] * n_ro,
        scratch_shapes=[pltpu.SemaphoreType.DMA((7,)), pltpu.SemaphoreType.DMA((7,)), pltpu.SemaphoreType.DMA]
        + list(rider.sems),
        input_output_aliases={1 + i: n_own + o for i, o in rider.aliases.items()},
        compiler_params=pltpu.CompilerParams(vmem_limit_bytes=VMEM_LIMIT_BYTES))(v, *rider.operands)
    if rider.start is not None:
        return (*res[:n_own], res[n_own:])
    return res if with_sum else res[0]


def _full_place(ref, kind, C, kk, half, layer=None):
    lead = slice(None) if layer is None else pl.ds(layer, 1)
    if kind == "row":
        return ref.at[lead, kk, half]
    return ref.at[lead, half, :, pl.ds(pl.multiple_of(kk * C, LANES), C)]


def _gather_rider(fulls, kinds, shard_cols, layers=None, peers=(0, 1, 2)):
    n = len(fulls)
    layers = layers or [None] * n
    rows = [f.shape[3] if kind == "row" else f.shape[2] for f, kind in zip(fulls, kinds)]
    n_chunks = 2 if all(r % 32 == 0 for r in rows) else 1

    def copies(outs, sems):
        x, y, c, chips = _mesh_place()
        k = 2 * x + y

        def place(a, kk, half, ch):
            rc = rows[a] // n_chunks
            return _full_place(outs[a], kinds[a], shard_cols[a], kk, half, layers[a]).at[:, pl.ds(ch * rc, rc), :]

        def copy(a, j, ch, ref, to):
            s = 6 * (n_chunks * a + ch) + j
            return _remote(ref, ref, sems[0].at[s], sems[1].at[s], to)

        return (x, y, c), [(j, chip) for j, chip in enumerate(chips) if j in peers], k, place, copy

    def start(_, outs, sems):
        (x, y, c), chips, k, place, copy = copies(outs, sems)
        for ch in range(n_chunks):
            for j, chip in chips:
                for a in range(n):
                    copy(a, j, ch, place(a, k, c, ch), (*chip, c)).start()

    def pass_on(outs, sems, ch):
        (x, y, c), chips, k, place, copy = copies(outs, sems)
        for j, chip in chips:
            kj = 2 * chip[0] + chip[1]
            for a in range(n):
                copy(a, j, ch, place(a, kj, c, ch), (x, y, c)).wait_recv()
                copy(a, 3 + j, ch, place(a, kj, c, ch), (x, y, 1 - c)).start()

    def mid(_, outs, sems):
        pass_on(outs, sems, 0)

    def finish(_, outs, sems):
        pass_on(outs, sems, n_chunks - 1)
        (x, y, c), chips, k, place, copy = copies(outs, sems)
        for ch in range(n_chunks):
            for j, chip in chips:
                kj = 2 * chip[0] + chip[1]
                for a in range(n):
                    copy(a, 3 + j, ch, place(a, kj, 1 - c, ch), (x, y, c)).wait_recv()
        for ch in range(n_chunks):
            for j, chip in chips:
                kj = 2 * chip[0] + chip[1]
                for a in range(n):
                    copy(a, j, ch, place(a, k, c, ch), (*chip, c)).wait_send()
                    copy(a, 3 + j, ch, place(a, kj, c, ch), (x, y, 1 - c)).wait_send()

    n_sems = 6 * n * n_chunks
    return Rider(tuple(fulls), tuple(_sds(f.shape, BF16) for f in fulls), {a: a for a in range(n)},
                 (pltpu.SemaphoreType.DMA((n_sems,)), pltpu.SemaphoreType.DMA((n_sems,))), start, finish,
                 mid if n_chunks == 2 else None)


def _scatter_rider(ps, kinds, shard_cols, peers=(0, 1, 2), into=None):
    n = len(ps)

    def copies(ins, outs, sems):
        x, y, c, chips = _mesh_place()
        cps = []
        for j, chip in enumerate(chips):
            if j not in peers:
                continue
            kj = 2 * chip[0] + chip[1]
            for a in range(n):
                C = shard_cols[a]
                src = ins[a].at[:, kj] if kinds[a] == "row" else ins[a].at[:, 0, :, pl.ds(pl.multiple_of(kj * C, LANES), C)]
                cps.append(_remote(src, outs[a].at[j], sems[0].at[3 * a + j], sems[1].at[3 * a + j], (*chip, c)))
        return cps

    def start(ins, outs, sems):
        for cp in copies(ins, outs, sems):
            cp.start()

    def finish(ins, outs, sems):
        cps = copies(ins, outs, sems)
        for cp in cps:
            cp.wait_recv()
        for cp in cps:
            cp.wait_send()

    out_shape = tuple(_sds((3, p.shape[0], p.shape[2], C), BF16) for p, C in zip(ps, shard_cols))
    aliases = {n + a: a for a in range(n)} if into is not None else {}
    return Rider(tuple(ps) + tuple(into or ()), out_shape, aliases,
                 (pltpu.SemaphoreType.DMA((3 * n,)), pltpu.SemaphoreType.DMA((3 * n,))), start, finish)


def _run_rider(name, rider):
    n_in, n_out = len(rider.operands), len(rider.out_shape)

    def body(*refs):
        ins, outs, sems = refs[:n_in], refs[n_in:n_in + n_out], refs[n_in + n_out:]
        rider.start(ins, outs, sems)
        _rider_end(rider, ins, outs, sems)

    return pl.pallas_call(
        body, name=name, out_shape=list(rider.out_shape), in_specs=[ANY] * n_in, out_specs=[ANY] * n_out,
        input_output_aliases=dict(rider.aliases), scratch_shapes=list(rider.sems),
        compiler_params=pltpu.CompilerParams(vmem_limit_bytes=VMEM_LIMIT_BYTES))(*rider.operands)


def _exchange_rider(g5s):
    n = len(g5s)

    def copies(ins, outs, sems):
        x, y, c, _ = _mesh_place()
        return [_remote(ins[a].at[:, :, 1 - c], outs[a], sems[0].at[a], sems[1].at[a], (x, y, 1 - c)) for a in range(n)]

    def start(ins, outs, sems):
        for cp in copies(ins, outs, sems):
            cp.start()

    def finish(ins, outs, sems):
        cps = copies(ins, outs, sems)
        for cp in cps:
            cp.wait_recv()
        for cp in cps:
            cp.wait_send()

    out_shape = tuple(_sds((g.shape[0], g.shape[1], g.shape[3], g.shape[4]), BF16) for g in g5s)
    return Rider(tuple(g5s), out_shape, {}, (pltpu.SemaphoreType.DMA((n,)), pltpu.SemaphoreType.DMA((n,))), start, finish)


def _share_rider(fs):
    n = len(fs)

    def start(_, outs, sems):
        x, y, c, _p = _mesh_place()
        for a in range(n):
            mine = outs[a].at[:, c]
            _remote(mine, mine, sems[0].at[a], sems[1].at[a], (x, y, 1 - c)).start()

    def finish(_, outs, sems):
        x, y, c, _p = _mesh_place()
        for a in range(n):
            theirs = outs[a].at[:, 1 - c]
            _remote(theirs, theirs, sems[0].at[a], sems[1].at[a], (x, y, c)).wait_recv()
        for a in range(n):
            mine = outs[a].at[:, c]
            _remote(mine, mine, sems[0].at[a], sems[1].at[a], (x, y, 1 - c)).wait_send()

    return Rider(tuple(fs), tuple(_sds(f.shape, F32) for f in fs), {a: a for a in range(n)},
                 (pltpu.SemaphoreType.DMA((n,)), pltpu.SemaphoreType.DMA((n,))), start, finish)


def _both_riders(r1, r2):
    ni, no, ns = len(r1.operands), len(r1.out_shape), len(r1.sems)
    aliases = dict(r1.aliases)
    aliases.update({ni + i: no + o for i, o in r2.aliases.items()})

    def start(ins, outs, sems):
        r1.start(ins[:ni], outs[:no], sems[:ns])
        r2.start(ins[ni:], outs[no:], sems[ns:])

    def finish(ins, outs, sems):
        _rider_end(r1, ins[:ni], outs[:no], sems[:ns])
        _rider_end(r2, ins[ni:], outs[no:], sems[ns:])

    return Rider(r1.operands + r2.operands, r1.out_shape + r2.out_shape, aliases, r1.sems + r2.sems, start, finish)


def _pack_rows(parts, lane_mult=1024):
    flat = jnp.concatenate([p.reshape(-1).astype(F32) for p in parts])
    n = -(-flat.shape[0] // (8 * lane_mult)) * lane_mult
    return jnp.pad(flat, (0, 8 * n - flat.shape[0])).reshape(8, n)


def _relu2(acc):
    r = jnp.maximum(acc, 0.0)
    return r, r * r


def _times_2r(acc, r):
    return (acc * (2.0 * r.astype(F32)),)


def kernel(x, c, positions, w_mod, b_mod, norm_g, mla_w_in, mla_g_q, mla_g_kv, mla_w_uq, mla_w_ukv, mla_w_o, conv_w_in, conv_w, conv_w_out, mlp_w_up, mlp_w_down, loss_target, m_w_mod, m_b_mod, m_norm_g, m_mla_w_in, m_mla_g_q, m_mla_g_kv, m_mla_w_uq, m_mla_w_ukv, m_mla_w_o, m_conv_w_in, m_conv_w, m_conv_w_out, m_mlp_w_up, m_mlp_w_down, v_w_mod, v_b_mod, v_norm_g, v_mla_w_in, v_mla_g_q, v_mla_g_kv, v_mla_w_uq, v_mla_w_ukv, v_mla_w_o, v_conv_w_in, v_conv_w, v_conv_w_out, v_mlp_w_up, v_mlp_w_down):
    S, D = x.shape[1], x.shape[2]
    Dq = D // N_CHIPS
    ncol = w_mod.shape[2]
    n_mod = N_CHIPS * ncol // D
    F = mlp_w_up.shape[2] * N_CHIPS
    lat_dim = mla_w_in.shape[2]
    rank = mla_g_q.shape[1]
    H = mla_w_uq.shape[2]
    d_qk = mla_w_uq.shape[3]
    assert mla_g_kv.shape[1] == rank and lat_dim == 2 * rank + QK_ROPE and d_qk == QK_NOPE + QK_ROPE
    assert mla_w_ukv.shape[3] == QK_NOPE + V_HEAD and x.shape[0] == 1 and n_mod == 6
    assert norm_g.shape[0] == 2 and mla_w_in.shape[0] == 1 and conv_w_in.shape[0] == 1
    lat_pad = 2 * rank + LANES
    scale = float(d_qk) ** -0.5

    xi, yi, ci = lax.axis_index("x"), lax.axis_index("y"), lax.axis_index("c")
    chip = 2 * xi + yi
    dev = 2 * chip + ci
    c_idx = jnp.reshape(ci, (1,)).astype(jnp.int32)
    k_idx = jnp.reshape(chip, (1,)).astype(jnp.int32)

    n1 = D + 2 * D + 3 * Dq
    g1 = _small_allgather("gather_small_inputs", _pack_rows([c, norm_g, conv_w])).reshape(8, -1)
    c_all = g1[:, :D]
    by_chip = g1[0::2]
    norm_full = jnp.concatenate([by_chip[kk, D:3 * D].reshape(2, 4, Dq) for kk in range(N_CHIPS)], axis=-1)
    convw_full = jnp.concatenate([by_chip[kk, 3 * D:n1].reshape(3, Dq) for kk in range(N_CHIPS)], axis=-1)

    b_cols = lax.dynamic_slice(b_mod, (0, chip * ncol), (2, ncol)).reshape(2, 1, ncol)
    cond_all = _silu(c_all)
    mod_cols = _mod_fwd(cond_all, w_mod, b_cols)
    g2 = _small_allgather("gather_mod", _pack_rows([mod_cols]))
    g2 = g2.reshape(8, -1)[0::2, :2 * 8 * ncol].reshape(N_CHIPS, 2, 8, ncol)
    mod_all = jnp.transpose(g2, (2, 1, 0, 3)).reshape(8, 2, n_mod * D)
    mod_me = lax.dynamic_index_in_dim(mod_all, dev, axis=0, keepdims=False)
    mods = [[mod_me[l, i * D:(i + 1) * D].reshape(1, D) for i in range(n_mod)] for l in range(2)]
    ng = [[norm_full[l, i].reshape(1, D) for i in range(4)] for l in range(2)]

    pos = positions[0].astype(F32)
    inv_freq = ROPE_THETA ** (-jnp.arange(0, QK_ROPE, 2, dtype=F32) / QK_ROPE)
    ang = pos[:, None] * inv_freq
    cos, sin = jnp.cos(ang), jnp.sin(ang)
    zero = jnp.zeros_like(cos)
    rope_tabs = (jnp.concatenate([cos, cos, zero, zero], axis=1),
                 jnp.concatenate([-sin, zero, zero, zero], axis=1),
                 jnp.concatenate([zero, sin, zero, zero], axis=1))

    weights = [("mla_w_in", mla_w_in, "row"), ("mla_w_uq", mla_w_uq.reshape(1, rank // N_CHIPS, H * d_qk), "row"),
               ("mla_w_ukv", mla_w_ukv.reshape(1, rank // N_CHIPS, H * QK_PAD), "row"), ("mla_w_o", mla_w_o, "row"),
               ("conv_w_in", conv_w_in, "col"), ("conv_w_out", conv_w_out, "row"),
               ("mlp_w_up", mlp_w_up, "col"), ("mlp_w_down", mlp_w_down, "row")]
    kinds = [k for _, _, k in weights]
    shard_shapes = [w.shape for _, w, _ in weights]
    shard_cols = [s[2] for s in shard_shapes]
    W_IN, W_UQ, W_UKV, W_O, W_CIN, W_COUT, W_UP, W_DOWN = range(8)
    mla_idx = [W_IN, W_UQ, W_UKV, W_O]
    casted = [_cast_into_full("cast_" + nm, [w], [kind], k_idx)[0][0] for nm, w, kind in weights[:W_UP]]

    def view(i, buf):
        L, R, C = shard_shapes[i]
        return buf.reshape((L, N_CHIPS * R, C) if kinds[i] == "row" else (L, R, N_CHIPS * C))

    NEIGHBOURS, DIAGONAL = (0, 1), (2,)

    def gather_of(bufs, idx, layers=None, peers=(0, 1, 2)):
        return _gather_rider(bufs, [kinds[i] for i in idx], [shard_cols[i] for i in idx], layers, peers)

    def scatter_of(ps, idx, peers=(0, 1, 2), into=None):
        return _scatter_rider(ps, [kinds[i] for i in idx], [shard_cols[i] for i in idx], peers, into)

    def halves(items):
        g5s = []
        for _, i, g in items:
            _, R, C = shard_shapes[i]
            g5s.append(g.reshape((1, N_CHIPS, 2, R // 2, C) if kinds[i] == "row" else (1, 1, 2, R // 2, N_CHIPS * C)))
        return g5s

    def pair_sums(items, g5s, ras):
        return [_pair_sum("pair_sum_" + nm, g5, ra, c_idx) for (nm, _, _), g5, ra in zip(items, g5s, ras)]

    mlp_casted, got = _cast_into_full("cast_mlp_w", [mlp_w_up, mlp_w_down], [kinds[W_UP], kinds[W_DOWN]], k_idx,
                                      gather_of([casted[i] for i in mla_idx], mla_idx))
    casted += list(mlp_casted)
    w_in_p = jnp.pad(view(W_IN, got[0])[0], ((0, 0), (0, lat_pad - lat_dim)))
    w_q_p = jnp.pad(view(W_UQ, got[1])[0].reshape(rank, H, d_qk), ((0, 0), (0, 0), (0, QK_PAD - d_qk))).reshape(rank, H * QK_PAD)
    w_ukv, w_o = view(W_UKV, got[2])[0], view(W_O, got[3])[0]
    HV = H * V_HEAD

    def layer_b(l, transposed):
        if transposed:
            return lambda tm, tn, tk: pl.BlockSpec((None, tn, tk), lambda i, j, k: (l, j, k))
        return lambda tm, tn, tk: pl.BlockSpec((None, tk, tn), lambda i, j, k: (l, k, j))

    def mlp_up(tag, l, h, w, rider=NO_RIDER):
        return _mm("mlp_up_" + tag, h, w, "nn", S, F, D, [_sds((S, F), BF16)] * 2, epilogue=_relu2,
                   b_spec=layer_b(l, False), rider=rider)

    def mlp_down(tag, l, a2, w, rider=NO_RIDER):
        return _mm("mlp_down_" + tag, a2, w, "nn", S, D, F, [_sds((S, D), BF16)], b_spec=layer_b(l, False), rider=rider)

    def mlp_bwd(tag, l, h, r, a2, dy, rider_of=None):
        carried = ()
        first = rider_of(carried) if rider_of else NO_RIDER
        res = _mm("mlp_down_dx_" + tag, dy, w_down, "nt", S, F, D, [_sds((S, F), BF16)], epilogue=_times_2r,
                  b_spec=layer_b(l, True), rider=first,
                  extras=[(r, lambda tm, tn, tk: pl.BlockSpec((tm, tn), lambda i, j, k: (i, j)))])
        (da,), carried = res if rider_of else (res, ())
        second = rider_of(carried) if rider_of else NO_RIDER
        res = _mm("mlp_down_dw_" + tag, a2, dy, "tn", F, D, S, [_sds((F, D), BF16)], rider=second)
        (dw_down,), carried = res if rider_of else (res, ())
        (dh,) = _mm("mlp_up_dx_" + tag, da, w_up, "nt", S, D, F, [_sds((S, D), BF16)], b_spec=layer_b(l, True))
        (dw_up,) = _mm("mlp_up_dw_" + tag, h, da, "tn", D, F, S, [_sds((D, F), BF16)])
        return dh, dw_up, dw_down, carried

    x0 = x[0]
    sh1, sc1, gt1, sh2, sc2, gt2 = mods[0]
    (h1,) = _fwd_boundary("fwd_boundary_0", x0, None, None, None, ng[0][0], sc1, sh1)
    (lat,) = _mm("mla_in", h1, w_in_p, "nn", S, lat_pad, D, [_sds((S, lat_pad), F32)], tn=lat_pad)
    cq, ckv, kr = _latent_fwd(lat, mla_g_q, mla_g_kv, rope_tabs, rank)

    def rope_q(acc, cos_p, sin_lo, sin_hi):
        parts = []
        for hh in range(acc.shape[1] // QK_PAD):
            parts.append(acc[:, hh * QK_PAD:hh * QK_PAD + QK_NOPE])
            parts.append(_rope(acc[:, hh * QK_PAD + QK_NOPE:(hh + 1) * QK_PAD], cos_p, sin_lo, sin_hi))
        return (jnp.concatenate(parts, axis=1),)

    tab_extra = lambda tm, tn, tk: pl.BlockSpec((tm, LANES), lambda i, j, k: (i, 0))
    (q,) = _mm("mla_q", cq, w_q_p, "nn", S, H * QK_PAD, rank, [_sds((S, H * QK_PAD), BF16)], epilogue=rope_q,
               extras=[(t, tab_extra) for t in rope_tabs], tn=2 * QK_PAD)
    (kv,) = _mm("mla_kv", ckv, w_ukv, "nn", S, H * QK_PAD, rank, [_sds((S, H * QK_PAD), BF16)])
    rest_idx = [W_UP, W_DOWN]
    o, lse, (up_buf, down_buf) = _attn_fwd_tri(
        q, kv, kr, H, scale, gather_of([casted[i] for i in rest_idx], rest_idx, [0, 0]))
    (y1,), (cout_buf,) = _mm("mla_out", o, w_o, "nn", S, D, HV, [_sds((S, D), BF16)],
                             rider=gather_of([casted[W_COUT]], [W_COUT]))
    x1, h2 = _fwd_boundary("fwd_boundary_1", x0, y1, gt1, ng[0][1], ng[0][2], sc2, sh2)
    (r2, a2), (cin_buf,) = mlp_up("0", 0, h2, view(W_UP, up_buf), gather_of([casted[W_CIN]], [W_CIN]))
    (y2,), (up_buf,) = mlp_down("0", 0, a2, view(W_DOWN, down_buf), gather_of([up_buf], [W_UP], [1]))
    w_cin, w_cout, w_up = view(W_CIN, cin_buf)[0], view(W_COUT, cout_buf)[0], view(W_UP, up_buf)

    sh1b, sc1b, gt1b, sh2b, sc2b, gt2b = mods[1]
    x2, h3 = _fwd_boundary("fwd_boundary_2", x1, y2, gt2, ng[0][3], ng[1][0], sc1b, sh1b)
    nD = lambda tn: D // tn
    (proj3,), (down_buf,) = _mm(
        "conv_in", h3, w_cin, "nn", S, 3 * D, D, [_sds((3, S, D), BF16)], tn=min(1024, D),
        rider=gather_of([down_buf], [W_DOWN], [1], NEIGHBOURS),
        out_specs=[lambda tm, tn, tk: pl.BlockSpec((None, tm, tn), lambda i, j, k: (j // nD(tn), i, j % nD(tn)))])
    bz = _conv_fwd(proj3, convw_full)
    (y3,) = _mm("conv_out", bz, w_cout, "nn", S, D, D, [_sds((S, D), BF16)])
    x3, h4 = _fwd_boundary("fwd_boundary_3", x2, y3, gt1b, ng[1][1], ng[1][2], sc2b, sh2b)
    (r4, a4), (down_buf,) = mlp_up("1", 1, h4, w_up, gather_of([down_buf], [W_DOWN], [1], DIAGONAL))
    w_down = view(W_DOWN, down_buf)
    (y4,) = mlp_down("1", 1, a4, w_down)

    dx4, dy4, sums_l, loss_acc = _loss_boundary("loss_boundary", x3, y4, gt2b, ng[1][3], loss_target[0])
    loss = lax.psum(loss_acc[0, 0], ("x", "y", "c"))

    dh4, dw_up1, dw_down1, _ = mlp_bwd("1", 1, h4, r4, a4, dy4)
    items = [("mlp_w_up_1", W_UP, dw_up1), ("mlp_w_down_1", W_DOWN, dw_down1)]
    g5s = halves(items)
    dx3, dy3, sums_3, ras = _bwd_boundary("bwd_boundary_3", dx4, dh4, x3, y3, gt1b, ng[1][1], ng[1][2], sc2b,
                                          _exchange_rider(g5s))
    ps_up1, ps_down1 = pair_sums(items, g5s, ras)

    (dbz,) = _mm("conv_out_dx", dy3, w_cout, "nt", S, D, D, [_sds((S, D), BF16)])
    (dw_cout,) = _mm("conv_out_dw", bz, dy3, "tn", D, D, S, [_sds((D, D), BF16)])
    dproj3, dconvw = _conv_bwd(dbz, proj3, convw_full)
    (dh3,), (rb_up1,) = _mm(
        "conv_in_dx", dproj3, w_cin, "nt", S, D, 3 * D, [_sds((S, D), BF16)], tk=D,
        rider=scatter_of([ps_up1], [W_UP], NEIGHBOURS),
        a_spec=lambda tm, tn, tk: pl.BlockSpec((None, tm, tk), lambda i, j, k: (k // (D // tk), i, k % (D // tk))))
    (dw_cin,), (rb_up1,) = _mm(
        "conv_in_dw", h3, dproj3, "tn", D, 3 * D, S, [_sds((D, 3 * D), BF16)], tn=min(1024, D),
        rider=scatter_of([ps_up1], [W_UP], DIAGONAL, [rb_up1]),
        b_spec=lambda tm, tn, tk: pl.BlockSpec((None, tk, tn), lambda i, j, k: (j // nD(tn), k, j % nD(tn))))
    items = [("conv_w_in", W_CIN, dw_cin), ("conv_w_out", W_COUT, dw_cout)]
    g5s = halves(items)
    dx2, dy2, sums_2, ras = _bwd_boundary("bwd_boundary_2", dx3, dh3, x2, y2, gt2, ng[0][3], ng[1][0], sc1b,
                                          _exchange_rider(g5s))
    ps_cin, ps_cout = pair_sums(items, g5s, ras)

    dh2, dw_up0, dw_down0, (rb_down1,) = mlp_bwd(
        "0", 0, h2, r2, a2, dy2,
        lambda got: scatter_of([ps_down1], [W_DOWN], DIAGONAL, list(got)) if got else scatter_of([ps_down1], [W_DOWN], NEIGHBOURS))
    items = [("mlp_w_up_0", W_UP, dw_up0), ("mlp_w_down_0", W_DOWN, dw_down0)]
    g5s = halves(items)
    dx1, dy1, sums_1, ras = _bwd_boundary("bwd_boundary_1", dx2, dh2, x1, y1, gt1, ng[0][1], ng[0][2], sc2,
                                          _exchange_rider(g5s))
    ps_up0, ps_down0 = pair_sums(items, g5s, ras)

    (dw_o,) = _mm("mla_out_dw", o, dy1, "tn", HV, D, S, [_sds((HV, D), BF16)])
    items = [(weights[W_O][0], W_O, dw_o)]
    g5s = halves(items)
    (do,), ras = _mm("mla_out_dx", dy1, w_o, "nt", S, HV, D, [_sds((S, HV), BF16)], rider=_exchange_rider(g5s))
    (ps_o,) = pair_sums(items, g5s, ras)
    dq, dkv, dkr, (rb_up0, rb_down0, rb_cin, rb_cout, rb_o) = _attn_bwd_tri(
        q, kv, kr, o, do, lse, rope_tabs, H, scale,
        scatter_of([ps_up0, ps_down0, ps_cin, ps_cout, ps_o], [W_UP, W_DOWN, W_CIN, W_COUT, W_O]))
    (dcq,) = _mm("mla_q_dx", dq, w_q_p, "nt", S, rank, H * QK_PAD, [_sds((S, rank), F32)])
    (dw_q_p,) = _mm("mla_q_dw", cq, dq, "tn", rank, H * QK_PAD, S, [_sds((rank, H * QK_PAD), BF16)])
    (dckv,) = _mm("mla_kv_dx", dkv, w_ukv, "nt", S, rank, H * QK_PAD, [_sds((S, rank), F32)])
    (dw_ukv,) = _mm("mla_kv_dw", ckv, dkv, "tn", rank, H * QK_PAD, S, [_sds((rank, H * QK_PAD), BF16)])
    dlat, sums_lat = _latent_bwd(lat, dcq, dckv, dkr, mla_g_q, mla_g_kv, rope_tabs, rank)
    (dw_in_p,) = _mm("mla_in_dw", h1, dlat, "tn", D, lat_pad, S, [_sds((D, lat_pad), BF16)], tn=lat_pad)
    late_idx = [W_IN, W_UQ, W_UKV]
    dw_mla = [dw_in_p[:, :lat_dim], dw_q_p.reshape(rank, H, QK_PAD)[:, :, :d_qk].reshape(rank, H * d_qk), dw_ukv]
    items = [(weights[i][0], i, g) for i, g in zip(late_idx, dw_mla)]
    g5s = halves(items)
    (dh1,), ras = _mm("mla_in_dx", dlat, w_in_p, "nt", S, D, lat_pad, [_sds((S, D), BF16)], rider=_exchange_rider(g5s))
    ps_mla = pair_sums(items, g5s, ras)
    grad_x, sums_0, _ = _bwd_boundary("bwd_boundary_0", dx1, dh1, x0, None, None, None, ng[0][0], sc1)

    kc_idx = jnp.stack([chip, ci]).astype(jnp.int32)
    fs_rest = [_chip_sum("chip_sum_" + weights[i][0], p, rb, kc_idx, kinds[i])
               for i, p, rb in [(W_O, ps_o, rb_o), (W_CIN, ps_cin, rb_cin), (W_COUT, ps_cout, rb_cout)]]
    for i, (p1, r1), (p0, r0) in [(W_UP, (ps_up1, rb_up1), (ps_up0, rb_up0)), (W_DOWN, (ps_down1, rb_down1), (ps_down0, rb_down0))]:
        f = _chip_sum("chip_sum_" + weights[i][0] + "_1", p1, r1, kc_idx, kinds[i], layer=1, n_layers=2)
        fs_rest.append(_chip_sum("chip_sum_" + weights[i][0] + "_0", p0, r0, kc_idx, kinds[i], layer=0, n_layers=2, prev=f))

    dmod0 = [sums_0[0], sums_0[1], sums_1[3], sums_1[0], sums_1[1], sums_2[3]]
    dmod1 = [sums_2[0], sums_2[1], sums_3[3], sums_3[0], sums_3[1], sums_l[3]]
    dng0 = [sums_0[2], sums_1[4], sums_1[2], sums_2[4]]
    dng1 = [sums_2[2], sums_3[4], sums_3[2], sums_l[4]]
    small = _pack_rows(dmod0 + dmod1 + dng0 + dng1 + [sums_lat[0], sums_lat[1], dconvw], lane_mult=LANES)
    gathered, total, carried = _small_allgather(
        "gather_small_grads", small, with_sum=True, rider=_both_riders(scatter_of(ps_mla, late_idx), _share_rider(fs_rest)))
    rbs_mla, finals_rest = carried[:len(late_idx)], carried[len(late_idx):]
    n_dm = 2 * n_mod * D
    dmod_all = gathered.reshape(8, -1)[:, :n_dm].reshape(8, 2, n_mod * D)
    total = total.reshape(-1)
    g_b_mod = total[:n_dm].reshape(2, n_mod * D)
    g_norm = lax.dynamic_slice(total[n_dm:n_dm + 8 * D].reshape(2, 4, D), (0, 0, chip * Dq), (2, 4, Dq))
    off = n_dm + 8 * D
    g_gq = total[off:off + rank].reshape(1, rank)
    g_gkv = total[off + rank:off + 2 * rank].reshape(1, rank)
    off += 2 * rank
    g_convw = lax.dynamic_slice(total[off:off + 3 * D].reshape(1, 3, D), (0, 0, chip * Dq), (1, 3, Dq))

    dmod_cols = jnp.transpose(lax.dynamic_slice(dmod_all.reshape(8, 2, N_CHIPS, ncol), (0, 0, chip, 0), (8, 2, 1, ncol))
                              .reshape(8, 2, ncol), (1, 0, 2))
    g_w_mod, d_w_mod, nm_w_mod, nv_w_mod, _ = _adamw_mod(w_mod, cond_all.T, dmod_cols, m_w_mod, v_w_mod)
    fs_mla = [_chip_sum("chip_sum_" + weights[i][0], p, rb, kc_idx, kinds[i]) for i, p, rb in zip(late_idx, ps_mla, rbs_mla)]
    finals = list(_run_rider("grad_pair_share_mla", _share_rider(fs_mla))) + list(finals_rest)
    orig = [mla_w_in, mla_w_uq, mla_w_ukv, mla_w_o, conv_w_in, conv_w_out, mlp_w_up, mlp_w_down]
    big_grads = [f.reshape(w.shape) for f, w in zip(finals, orig)]

    names = ["b_mod", "norm_g", "mla_w_in", "mla_g_q", "mla_g_kv", "mla_w_uq", "mla_w_ukv", "mla_w_o",
             "conv_w_in", "conv_w", "conv_w_out", "mlp_w_up", "mlp_w_down"]
    ws = [b_mod, norm_g, mla_w_in, mla_g_q, mla_g_kv, mla_w_uq, mla_w_ukv, mla_w_o, conv_w_in, conv_w, conv_w_out,
          mlp_w_up, mlp_w_down]
    ms = [m_b_mod, m_norm_g, m_mla_w_in, m_mla_g_q, m_mla_g_kv, m_mla_w_uq, m_mla_w_ukv, m_mla_w_o, m_conv_w_in,
          m_conv_w, m_conv_w_out, m_mlp_w_up, m_mlp_w_down]
    vs = [v_b_mod, v_norm_g, v_mla_w_in, v_mla_g_q, v_mla_g_kv, v_mla_w_uq, v_mla_w_ukv, v_mla_w_o, v_conv_w_in,
          v_conv_w, v_conv_w_out, v_mlp_w_up, v_mlp_w_down]
    gs = [g_b_mod, g_norm, big_grads[0], g_gq, g_gkv, big_grads[1], big_grads[2], big_grads[3], big_grads[4],
          g_convw, big_grads[5], big_grads[6], big_grads[7]]
    grads, deltas, new_ms, new_vs = [g_w_mod], [d_w_mod], [nm_w_mod], [nv_w_mod]
    for nm, w, g, m, v in zip(names, ws, gs, ms, vs):
        d, nm_, nv_ = _adamw("adamw_" + nm, w, g, m, v)
        grads.append(g)
        deltas.append(d)
        new_ms.append(nm_)
        new_vs.append(nv_)
    return (loss, grad_x[None], *grads, *deltas, *new_ms, *new_vs)
```

```python
from typing import NamedTuple

import jax
import jax.numpy as jnp
from jax import lax
from jax.experimental import pallas as pl
from jax.experimental.pallas import tpu as pltpu

F32 = jnp.float32
BF16 = jnp.bfloat16
NORM_EPS = 1e-6
ROPE_THETA = 10000.0
QK_NOPE = 128
QK_ROPE = 64
V_HEAD = 128
LANES = 128
QK_PAD = QK_NOPE + LANES
ADAM_LR, ADAM_B1, ADAM_B2, ADAM_EPS, ADAM_WD, ADAM_STEP = 0.001, 0.9, 0.999, 1e-08, 0.01, 10
VMEM_LIMIT_BYTES = 56 * 1024 * 1024
N_CHIPS = 4
MESH_ID = pl.DeviceIdType.MESH
ANY = pl.BlockSpec(memory_space=pl.ANY)
NEG_INF = float("-inf")

DIMS_NN = (((1,), (0,)), ((), ()))
DIMS_NT = (((1,), (1,)), ((), ()))
DIMS_TN = (((0,), (0,)), ((), ()))


def _cparams(*sem):
    return pltpu.CompilerParams(dimension_semantics=sem, vmem_limit_bytes=VMEM_LIMIT_BYTES)


def _row_tile(rows, row_bytes, limit=2 * 1024 * 1024, mult=16):
    if rows * row_bytes <= limit or rows % mult:
        return rows
    best = mult
    t = mult
    while t <= rows:
        if rows % t == 0 and t * row_bytes <= limit:
            best = t
        t += mult
    return best


def _rms(v):
    return lax.rsqrt(jnp.mean(v * v, axis=-1, keepdims=True) + NORM_EPS)


class Rider(NamedTuple):
    operands: tuple
    out_shape: tuple
    aliases: dict
    sems: tuple
    start: object
    finish: object
    mid: object = None


NO_RIDER = Rider((), (), {}, (), None, None)


def _rider_end(rider, r_in, r_out, r_sems):
    if rider.mid is not None:
        rider.mid(r_in, r_out, r_sems)
    rider.finish(r_in, r_out, r_sems)


def _mm(name, a, b, mode, M, N, K, outs, *, a_spec=None, b_spec=None, out_specs=None, epilogue=None,
        extras=(), rider=NO_RIDER, tm=1024, tn=1024, tk=4096):
    tm, tn, tk = min(tm, M), min(tn, N), min(tk, K)
    assert M % tm == 0 and N % tn == 0 and K % tk == 0, (name, M, N, K)
    nk = K // tk
    if a_spec is None:
        a_spec = {"nn": pl.BlockSpec((tm, tk), lambda i, j, k: (i, k)),
                  "nt": pl.BlockSpec((tm, tk), lambda i, j, k: (i, k)),
                  "tn": pl.BlockSpec((tk, tm), lambda i, j, k: (k, i))}[mode]
    else:
        a_spec = a_spec(tm, tn, tk)
    if b_spec is None:
        b_spec = {"nn": pl.BlockSpec((tk, tn), lambda i, j, k: (k, j)),
                  "nt": pl.BlockSpec((tn, tk), lambda i, j, k: (j, k)),
                  "tn": pl.BlockSpec((tk, tn), lambda i, j, k: (k, j))}[mode]
    else:
        b_spec = b_spec(tm, tn, tk)
    if out_specs is None:
        out_specs = [pl.BlockSpec((tm, tn), lambda i, j, k: (i, j)) for _ in outs]
    else:
        out_specs = [s(tm, tn, tk) for s in out_specs]
    dims = {"nn": DIMS_NN, "nt": DIMS_NT, "tn": DIMS_TN}[mode]
    ne, no = len(extras), len(outs)
    n_ri, n_ro = len(rider.operands), len(rider.out_shape)
    grid = (M // tm, N // tn, nk)

    def body(*refs):
        a_ref, b_ref = refs[0], refs[1]
        ex = refs[2:2 + ne]
        r_in = refs[2 + ne:2 + ne + n_ri]
        o = refs[2 + ne + n_ri:2 + ne + n_ri + no]
        r_out = refs[2 + ne + n_ri + no:2 + ne + n_ri + no + n_ro]
        scratch = refs[2 + ne + n_ri + no + n_ro:]
        r_sems = scratch[1:] if nk > 1 else scratch
        ii, jj, kk = pl.program_id(0), pl.program_id(1), pl.program_id(2)

        if rider.start is not None:
            @pl.when((ii == 0) & (jj == 0) & (kk == 0))
            def _():
                rider.start(r_in, r_out, r_sems)

        part = lax.dot_general(a_ref[...].astype(BF16), b_ref[...].astype(BF16), dims,
                               preferred_element_type=F32)

        def finish(total):
            vals = epilogue(total, *[e[...] for e in ex]) if epilogue is not None else (total,)
            for r, v in zip(o, vals):
                r[...] = v.astype(r.dtype)

        if nk == 1:
            finish(part)
        else:
            acc = scratch[0]

            @pl.when(kk == 0)
            def _():
                acc[...] = part

            @pl.when(kk > 0)
            def _():
                acc[...] += part

            @pl.when(kk == nk - 1)
            def _():
                finish(acc[...])

        if rider.finish is not None:
            steps = grid[0] * grid[1] * nk
            if rider.mid is not None and steps >= 4:
                @pl.when((ii * grid[1] + jj) * nk + kk == steps // 2)
                def _():
                    rider.mid(r_in, r_out, r_sems)

            @pl.when((ii == grid[0] - 1) & (jj == grid[1] - 1) & (kk == nk - 1))
            def _():
                if rider.mid is not None and steps < 4:
                    rider.mid(r_in, r_out, r_sems)
                rider.finish(r_in, r_out, r_sems)

    operands = [a, b] + [e[0] for e in extras] + list(rider.operands)
    in_specs = [a_spec, b_spec] + [e[1](tm, tn, tk) for e in extras] + [ANY] * n_ri
    hosted = rider.start is not None
    res = pl.pallas_call(
        body, name=name, grid=grid,
        in_specs=in_specs, out_specs=out_specs + [ANY] * n_ro, out_shape=list(outs) + list(rider.out_shape),
        scratch_shapes=([pltpu.VMEM((tm, tn), F32)] if nk > 1 else []) + list(rider.sems),
        input_output_aliases={2 + ne + i: no + r for i, r in rider.aliases.items()},
        compiler_params=_cparams(*(("arbitrary",) * 3 if hosted else ("parallel", "parallel", "arbitrary"))),
    )(*operands)
    return (res[:no], res[no:]) if hosted else res


def _sds(shape, dtype):
    return jax.ShapeDtypeStruct(tuple(shape), dtype)


def _rope(t, cos_p, sin_lo, sin_hi):
    return t * cos_p + pltpu.roll(t, LANES - QK_ROPE // 2, 1) * sin_lo + pltpu.roll(t, QK_ROPE // 2, 1) * sin_hi


def _rope_t(d, cos_p, sin_lo, sin_hi):
    return d * cos_p + pltpu.roll(d * sin_lo, QK_ROPE // 2, 1) + pltpu.roll(d * sin_hi, LANES - QK_ROPE // 2, 1)


def _vec_spec(d):
    return pl.BlockSpec((1, d), lambda i: (0, 0))


def _fwd_boundary(name, x_prev, y, gate, ng_post, ng_pre, sc, sh):
    S, D = x_prev.shape
    ts = min(256, S)
    has_y = y is not None
    row = pl.BlockSpec((ts, D), lambda i: (i, 0))

    def body(*refs):
        if has_y:
            x_ref, y_ref, g_ref, ngp_ref, ngn_ref, sc_ref, sh_ref, xo_ref, h_ref = refs
            yv = y_ref[...].astype(F32)
            xn = x_ref[...] + g_ref[...] * (yv * _rms(yv) * ngp_ref[...])
            xo_ref[...] = xn
        else:
            x_ref, ngn_ref, sc_ref, sh_ref, h_ref = refs
            xn = x_ref[...]
        hn = xn * _rms(xn) * ngn_ref[...]
        h_ref[...] = (hn * (1.0 + sc_ref[...]) + sh_ref[...]).astype(BF16)

    vec = _vec_spec(D)
    if has_y:
        operands = (x_prev, y, gate, ng_post, ng_pre, sc, sh)
        in_specs = [row, row, vec, vec, vec, vec, vec]
        out_shape = [_sds((S, D), F32), _sds((S, D), BF16)]
        out_specs = [row, row]
    else:
        operands = (x_prev, ng_pre, sc, sh)
        in_specs = [row, vec, vec, vec]
        out_shape = [_sds((S, D), BF16)]
        out_specs = [row]
    return pl.pallas_call(body, name=name, grid=(S // ts,), in_specs=in_specs, out_specs=out_specs,
                          out_shape=out_shape, compiler_params=_cparams("parallel"))(*operands)


def _acc_rows(sums_ref, rows):
    for r, v in rows:
        sums_ref[r:r + 1, :] += jnp.sum(v, axis=0, keepdims=True)


def _post_norm_bwd(dxt, yv, gate, ng_post, sums_ref, dy_ref):
    r1 = _rms(yv)
    yhat = yv * r1
    dn = dxt * gate
    u = dn * ng_post
    dy = r1 * (u - yhat * jnp.mean(u * yhat, axis=-1, keepdims=True))
    dy_ref[...] = dy.astype(dy_ref.dtype)
    _acc_rows(sums_ref, [(3, dxt * (yhat * ng_post)), (4, dn * yhat)])


def _loss_boundary(name, x_prev, y, gate, ng_post, target):
    S, D = x_prev.shape
    ts = min(256, S)
    row = pl.BlockSpec((ts, D), lambda i: (i, 0))
    vec = _vec_spec(D)

    def body(x_ref, y_ref, g_ref, ngp_ref, t_ref, dx_ref, dy_ref, sums_ref, loss_ref):
        @pl.when(pl.program_id(0) == 0)
        def _():
            sums_ref[...] = jnp.zeros_like(sums_ref)
            loss_ref[...] = jnp.zeros_like(loss_ref)

        yv = y_ref[...].astype(F32)
        xf = x_ref[...] + g_ref[...] * (yv * _rms(yv) * ngp_ref[...])
        err = xf - t_ref[...]
        loss_ref[...] += 0.5 * jnp.sum(jnp.mean(err * err, axis=-1, keepdims=True))
        dxt = err / D
        dx_ref[...] = dxt
        _post_norm_bwd(dxt, yv, g_ref[...], ngp_ref[...], sums_ref, dy_ref)

    return pl.pallas_call(
        body, name=name, grid=(S // ts,),
        in_specs=[row, row, vec, vec, row],
        out_specs=[row, row, pl.BlockSpec((8, D), lambda i: (0, 0)), pl.BlockSpec((8, LANES), lambda i: (0, 0))],
        out_shape=[_sds((S, D), F32), _sds((S, D), BF16), _sds((8, D), F32), _sds((8, LANES), F32)],
        compiler_params=_cparams("arbitrary"))(x_prev, y, gate, ng_post, target)


def _bwd_boundary(name, dx_new, dh, x_new, y, gate, ng_post, ng_pre, sc, rider=NO_RIDER):
    S, D = x_new.shape
    ts = min(256, S)
    has_y = y is not None
    row = pl.BlockSpec((ts, D), lambda i: (i, 0))
    vec = _vec_spec(D)
    n_in, n_out = (8, 3) if has_y else (5, 2)
    n_ri, n_ro = len(rider.operands), len(rider.out_shape)

    def body(*refs):
        r_in = refs[n_in:n_in + n_ri]
        r_out = refs[n_in + n_ri + n_out:n_in + n_ri + n_out + n_ro]
        r_sems = refs[n_in + n_ri + n_out + n_ro:]
        own = refs[:n_in] + refs[n_in + n_ri:n_in + n_ri + n_out]
        if has_y:
            dxn_ref, dh_ref, x_ref, y_ref, g_ref, ngp_ref, ngn_ref, sc_ref, dxo_ref, dy_ref, sums_ref = own
        else:
            dxn_ref, dh_ref, x_ref, ngn_ref, sc_ref, dxo_ref, sums_ref = own

        @pl.when(pl.program_id(0) == 0)
        def _():
            sums_ref[...] = jnp.zeros_like(sums_ref)
            if rider.start is not None:
                rider.start(r_in, r_out, r_sems)

        xv = x_ref[...]
        dhv = dh_ref[...].astype(F32)
        ngn = ngn_ref[...]
        r2 = _rms(xv)
        xhat = xv * r2
        dn_pre = dhv * (1.0 + sc_ref[...])
        u2 = dn_pre * ngn
        dxt = dxn_ref[...] + r2 * (u2 - xhat * jnp.mean(u2 * xhat, axis=-1, keepdims=True))
        dxo_ref[...] = dxt
        _acc_rows(sums_ref, [(0, dhv), (1, dhv * (xhat * ngn)), (2, dn_pre * xhat)])
        if has_y:
            _post_norm_bwd(dxt, y_ref[...].astype(F32), g_ref[...], ngp_ref[...], sums_ref, dy_ref)

        if rider.finish is not None:
            @pl.when(pl.program_id(0) == S // ts - 1)
            def _():
                _rider_end(rider, r_in, r_out, r_sems)

    sums_spec = pl.BlockSpec((8, D), lambda i: (0, 0))
    if has_y:
        operands = (dx_new, dh, x_new, y, gate, ng_post, ng_pre, sc)
        in_specs = [row, row, row, row, vec, vec, vec, vec]
        out_shape = [_sds((S, D), F32), _sds((S, D), BF16), _sds((8, D), F32)]
        out_specs = [row, row, sums_spec]
    else:
        operands = (dx_new, dh, x_new, ng_pre, sc)
        in_specs = [row, row, row, vec, vec]
        out_shape = [_sds((S, D), F32), _sds((8, D), F32)]
        out_specs = [row, sums_spec]
    res = pl.pallas_call(
        body, name=name, grid=(S // ts,), in_specs=in_specs + [ANY] * n_ri, out_specs=out_specs + [ANY] * n_ro,
        out_shape=out_shape + list(rider.out_shape), scratch_shapes=list(rider.sems),
        input_output_aliases={n_in + i: n_out + o for i, o in rider.aliases.items()},
        compiler_params=_cparams("arbitrary"))(*operands, *rider.operands)
    return (*res[:n_out], res[n_out:])


def _latent_fwd(lat, g_q, g_kv, rope_tabs, rank):
    S, W = lat.shape
    ts = min(256, S)
    tab = pl.BlockSpec((ts, LANES), lambda i: (i, 0))

    def body(lat_ref, gq_ref, gkv_ref, cos_ref, slo_ref, shi_ref, cq_ref, ckv_ref, kr_ref):
        lq = lat_ref[:, 0:rank]
        lkv = lat_ref[:, rank:2 * rank]
        cq_ref[...] = (lq * _rms(lq) * gq_ref[...]).astype(BF16)
        ckv_ref[...] = (lkv * _rms(lkv) * gkv_ref[...]).astype(BF16)
        kr_ref[...] = _rope(lat_ref[:, 2 * rank:W], cos_ref[...], slo_ref[...], shi_ref[...]).astype(BF16)

    return pl.pallas_call(
        body, name="mla_latent_fwd", grid=(S // ts,),
        in_specs=[pl.BlockSpec((ts, W), lambda i: (i, 0)), _vec_spec(rank), _vec_spec(rank), tab, tab, tab],
        out_specs=[pl.BlockSpec((ts, rank), lambda i: (i, 0)), pl.BlockSpec((ts, rank), lambda i: (i, 0)), tab],
        out_shape=[_sds((S, rank), BF16), _sds((S, rank), BF16), _sds((S, LANES), BF16)],
        compiler_params=_cparams("parallel"))(lat, g_q, g_kv, *rope_tabs)


def _latent_bwd(lat, dcq, dckv, dkr, g_q, g_kv, rope_tabs, rank):
    S, W = lat.shape
    ts = min(256, S)
    tab = pl.BlockSpec((ts, LANES), lambda i: (i, 0))
    half = pl.BlockSpec((ts, rank), lambda i: (i, 0))

    def body(lat_ref, dcq_ref, dckv_ref, dkr_ref, gq_ref, gkv_ref, cos_ref, slo_ref, shi_ref, dlat_ref, sums_ref):
        @pl.when(pl.program_id(0) == 0)
        def _():
            sums_ref[...] = jnp.zeros_like(sums_ref)

        def norm_bwd(v, dn, g, r):
            rr = _rms(v)
            vhat = v * rr
            u = dn * g
            sums_ref[r:r + 1, :] += jnp.sum(dn * vhat, axis=0, keepdims=True)
            return rr * (u - vhat * jnp.mean(u * vhat, axis=-1, keepdims=True))

        dlat_ref[:, 0:rank] = norm_bwd(lat_ref[:, 0:rank], dcq_ref[...], gq_ref[...], 0).astype(BF16)
        dlat_ref[:, rank:2 * rank] = norm_bwd(lat_ref[:, rank:2 * rank], dckv_ref[...], gkv_ref[...], 1).astype(BF16)
        dlat_ref[:, 2 * rank:W] = _rope_t(dkr_ref[...], cos_ref[...], slo_ref[...], shi_ref[...]).astype(BF16)

    return pl.pallas_call(
        body, name="mla_latent_bwd", grid=(S // ts,),
        in_specs=[pl.BlockSpec((ts, W), lambda i: (i, 0)), half, half, tab, _vec_spec(rank), _vec_spec(rank),
                  tab, tab, tab],
        out_specs=[pl.BlockSpec((ts, W), lambda i: (i, 0)), pl.BlockSpec((8, rank), lambda i: (0, 0))],
        out_shape=[_sds((S, W), BF16), _sds((8, rank), F32)],
        compiler_params=_cparams("arbitrary"))(lat, dcq, dckv, dkr, g_q, g_kv, *rope_tabs)


def _attn_tiles(S):
    t = min(512, S)
    return t, S // t


def _causal_mask(t):
    return lax.broadcasted_iota(jnp.int32, (t, t), 1) <= lax.broadcasted_iota(jnp.int32, (t, t), 0)


def _causal_pairs(nb, q_major):
    if q_major:
        pairs = [(qi, ki) for qi in range(nb) for ki in range(qi + 1)]
    else:
        pairs = [(qi, ki) for ki in range(nb) for qi in range(ki, nb)]
    return jnp.array([p[0] for p in pairs], jnp.int32), jnp.array([p[1] for p in pairs], jnp.int32), len(pairs)


def _heads_per_step(heads):
    return 2 if heads % 2 == 0 else 1


def _attn_fwd_tri(q, kv, kr, heads, scale, rider=NO_RIDER):
    S = q.shape[0]
    t, nb = _attn_tiles(S)
    G = _heads_per_step(heads)
    q_tab, k_tab, n_pairs = _causal_pairs(nb, True)
    n_ri, n_ro = len(rider.operands), len(rider.out_shape)

    def body(qt_ref, kt_ref, *refs):
        q_ref, kv_ref, kr_ref = refs[:3]
        r_in = refs[3:3 + n_ri]
        o_ref, lse_ref = refs[3 + n_ri:5 + n_ri]
        r_out = refs[5 + n_ri:5 + n_ri + n_ro]
        m_scr, acc_scr = refs[5 + n_ri + n_ro:7 + n_ri + n_ro]
        r_sems = refs[7 + n_ri + n_ro:]
        h, p = pl.program_id(0), pl.program_id(1)
        qi, ki = qt_ref[p], kt_ref[p]

        if rider.start is not None:
            @pl.when((h == 0) & (p == 0))
            def _():
                rider.start(r_in, r_out, r_sems)

        @pl.when(ki == 0)
        def _():
            m_scr[...] = jnp.full_like(m_scr, NEG_INF)
            acc_scr[...] = jnp.zeros_like(acc_scr)

        def step(diagonal):
            ones = jnp.ones((t, LANES), BF16)
            for g in range(G):
                kcat = jnp.concatenate([kv_ref[:, g * QK_PAD:g * QK_PAD + QK_NOPE], kr_ref[...]], axis=1)
                vext = jnp.concatenate([kv_ref[:, g * QK_PAD + QK_NOPE:(g + 1) * QK_PAD], ones], axis=1)
                s = lax.dot_general(q_ref[:, g * QK_PAD:(g + 1) * QK_PAD], kcat, DIMS_NT,
                                    preferred_element_type=F32) * scale
                if diagonal:
                    s = jnp.where(_causal_mask(t), s, NEG_INF)
                m_prev = m_scr[g]
                m_new = jnp.maximum(m_prev, jnp.max(s, axis=-1, keepdims=True))
                alpha = jnp.exp(m_prev - m_new)
                pr = jnp.exp(s - jnp.tile(m_new, (1, t // LANES)))
                acc_scr[g] = jnp.tile(alpha, (1, 2)) * acc_scr[g] + lax.dot_general(
                    pr.astype(BF16), vext, DIMS_NN, preferred_element_type=F32)
                m_scr[g] = m_new

        @pl.when(ki < qi)
        def _():
            step(False)

        @pl.when(ki == qi)
        def _():
            step(True)
            for g in range(G):
                acc = acc_scr[g]
                o_ref[:, g * V_HEAD:(g + 1) * V_HEAD] = (acc[:, 0:V_HEAD] / acc[:, V_HEAD:2 * V_HEAD]).astype(BF16)
                lse_ref[g] = m_scr[g] + jnp.log(acc[:, V_HEAD:2 * V_HEAD])

        if rider.finish is not None:
            halfway = rider.mid is not None and heads // G >= 2
            if halfway:
                @pl.when((h == heads // G // 2) & (p == 0))
                def _():
                    rider.mid(r_in, r_out, r_sems)

            @pl.when((h == heads // G - 1) & (p == n_pairs - 1))
            def _():
                if halfway:
                    rider.finish(r_in, r_out, r_sems)
                else:
                    _rider_end(rider, r_in, r_out, r_sems)

    res = pl.pallas_call(
        body, name="mla_attn_fwd",
        grid_spec=pltpu.PrefetchScalarGridSpec(
            num_scalar_prefetch=2, grid=(heads // G, n_pairs),
            in_specs=[pl.BlockSpec((t, G * QK_PAD), lambda h, p, qt, kt: (qt[p], h)),
                      pl.BlockSpec((t, G * QK_PAD), lambda h, p, qt, kt: (kt[p], h)),
                      pl.BlockSpec((t, LANES), lambda h, p, qt, kt: (kt[p], 0))] + [ANY] * n_ri,
            out_specs=[pl.BlockSpec((t, G * V_HEAD), lambda h, p, qt, kt: (qt[p], h)),
                       pl.BlockSpec((G, t, LANES), lambda h, p, qt, kt: (h, qt[p], 0))] + [ANY] * n_ro,
            scratch_shapes=[pltpu.VMEM((G, t, LANES), F32), pltpu.VMEM((G, t, 2 * V_HEAD), F32)] + list(rider.sems)),
        out_shape=[_sds((S, heads * V_HEAD), BF16), _sds((heads, S, LANES), F32)] + list(rider.out_shape),
        input_output_aliases={5 + i: 2 + o for i, o in rider.aliases.items()},
        compiler_params=_cparams("arbitrary", "arbitrary"))(q_tab, k_tab, q, kv, kr, *rider.operands)
    return res[0], res[1], res[2:]


def _attn_bwd_tri(q, kv, kr, o, do, lse, rope_tabs, heads, scale, rider=NO_RIDER):
    S = q.shape[0]
    t, nb = _attn_tiles(S)
    G = _heads_per_step(heads)
    q_tab, k_tab, n_pairs = _causal_pairs(nb, False)
    n_ri, n_ro = len(rider.operands), len(rider.out_shape)
    rep = t // LANES

    tabs = jnp.concatenate(rope_tabs, axis=1)

    def body(qt_ref, kt_ref, *refs):
        q_ref, kv_ref, kr_ref, o_ref, do_ref, lse_ref, tabs_ref = refs[:7]
        cos_ref, slo_ref, shi_ref = (tabs_ref.at[:, pl.ds(i * LANES, LANES)] for i in range(3))
        r_in = refs[7:7 + n_ri]
        dq_ref, dkv_ref, dkr_ref = refs[7 + n_ri:10 + n_ri]
        r_out = refs[10 + n_ri:10 + n_ri + n_ro]
        dq_scr, dk_scr, dv_scr, dkr_scr, delta_scr = refs[10 + n_ri + n_ro:15 + n_ri + n_ro]
        r_sems = refs[15 + n_ri + n_ro:]
        h, p = pl.program_id(0), pl.program_id(1)
        qi, ki = qt_ref[p], kt_ref[p]
        q_rows = pl.ds(pl.multiple_of(qi * t, t), t)
        k_rows = pl.ds(pl.multiple_of(ki * t, t), t)

        @pl.when(ki == 0)
        def _():
            for g in range(G):
                cols = slice(g * V_HEAD, (g + 1) * V_HEAD)
                d = jnp.sum(do_ref[:, cols].astype(F32) * o_ref[:, cols].astype(F32), axis=-1, keepdims=True)
                delta_scr[g, q_rows, :] = jnp.broadcast_to(d, (t, LANES))

        if rider.start is not None:
            @pl.when((h == 0) & (p == 0))
            def _():
                rider.start(r_in, r_out, r_sems)

        @pl.when(p == 0)
        def _():
            dq_scr[...] = jnp.zeros_like(dq_scr)

        @pl.when((h == 0) & (p == 0))
        def _():
            dkr_scr[...] = jnp.zeros_like(dkr_scr)

        @pl.when(qi == ki)
        def _():
            dk_scr[...] = jnp.zeros_like(dk_scr)
            dv_scr[...] = jnp.zeros_like(dv_scr)

        def step(diagonal):
            for g in range(G):
                qv = q_ref[:, g * QK_PAD:(g + 1) * QK_PAD]
                kcat = jnp.concatenate([kv_ref[:, g * QK_PAD:g * QK_PAD + QK_NOPE], kr_ref[...]], axis=1)
                s = lax.dot_general(qv, kcat, DIMS_NT, preferred_element_type=F32) * scale
                pr = jnp.exp(s - jnp.tile(lse_ref[g], (1, rep)))
                if diagonal:
                    pr = jnp.where(_causal_mask(t), pr, 0.0)
                dov = do_ref[:, g * V_HEAD:(g + 1) * V_HEAD]
                dv_scr[g] += lax.dot_general(pr.astype(BF16), dov, DIMS_TN, preferred_element_type=F32)
                dp = lax.dot_general(dov, kv_ref[:, g * QK_PAD + QK_NOPE:(g + 1) * QK_PAD], DIMS_NT,
                                     preferred_element_type=F32)
                ds = (pr * (dp - jnp.tile(delta_scr[g, q_rows, :], (1, rep))) * scale).astype(BF16)
                dk_scr[g] += lax.dot_general(ds, qv, DIMS_TN, preferred_element_type=F32)
                dq_scr[q_rows, g * QK_PAD:(g + 1) * QK_PAD] += lax.dot_general(ds, kcat, DIMS_NN,
                                                                               preferred_element_type=F32)

        @pl.when(qi > ki)
        def _():
            step(False)

        @pl.when(qi == ki)
        def _():
            step(True)
            for g in range(G):
                dqv = dq_scr[q_rows, g * QK_PAD:(g + 1) * QK_PAD]
                dq_ref[q_rows, g * QK_PAD:(g + 1) * QK_PAD] = jnp.concatenate(
                    [dqv[:, 0:QK_NOPE], _rope_t(dqv[:, QK_NOPE:QK_PAD], cos_ref[...], slo_ref[...], shi_ref[...])],
                    axis=1).astype(BF16)

        @pl.when(qi == nb - 1)
        def _():
            for g in range(G):
                dkv_ref[:, g * QK_PAD:(g + 1) * QK_PAD] = jnp.concatenate(
                    [dk_scr[g][:, 0:QK_NOPE], dv_scr[g]], axis=1).astype(BF16)
                dkr_scr[k_rows, :] += dk_scr[g][:, QK_NOPE:QK_PAD]

        @pl.when((h == heads // G - 1) & (p == n_pairs - 1))
        def _():
            dkr_ref[...] = dkr_scr[...]
            if rider.finish is not None:
                _rider_end(rider, r_in, r_out, r_sems)

    q_blk = lambda w: pl.BlockSpec((t, G * w), lambda h, p, qt, kt: (qt[p], h))
    stat = pl.BlockSpec((G, t, LANES), lambda h, p, qt, kt: (h, qt[p], 0))
    tab = pl.BlockSpec((t, LANES), lambda h, p, qt, kt: (kt[p], 0))
    res = pl.pallas_call(
        body, name="mla_attn_bwd",
        grid_spec=pltpu.PrefetchScalarGridSpec(
            num_scalar_prefetch=2, grid=(heads // G, n_pairs),
            in_specs=[q_blk(QK_PAD),
                      pl.BlockSpec((t, G * QK_PAD), lambda h, p, qt, kt: (kt[p], h)),
                      tab, q_blk(V_HEAD), q_blk(V_HEAD), stat,
                      pl.BlockSpec((t, 3 * LANES), lambda h, p, qt, kt: (kt[p], 0))] + [ANY] * n_ri,
            out_specs=[pl.BlockSpec((S, G * QK_PAD), lambda h, p, qt, kt: (0, h)),
                       pl.BlockSpec((t, G * QK_PAD), lambda h, p, qt, kt: (kt[p], h)),
                       pl.BlockSpec((S, LANES), lambda h, p, qt, kt: (0, 0))] + [ANY] * n_ro,
            scratch_shapes=[pltpu.VMEM((S, G * QK_PAD), F32), pltpu.VMEM((G, t, QK_PAD), F32),
                            pltpu.VMEM((G, t, V_HEAD), F32), pltpu.VMEM((S, LANES), F32),
                            pltpu.VMEM((G, S, LANES), F32)] + list(rider.sems)),
        out_shape=[_sds((S, heads * QK_PAD), BF16), _sds((S, heads * QK_PAD), BF16), _sds((S, LANES), F32)]
        + list(rider.out_shape),
        input_output_aliases={9 + i: 3 + o for i, o in rider.aliases.items()},
        compiler_params=_cparams("arbitrary", "arbitrary"))(q_tab, k_tab, q, kv, kr, o, do, lse, tabs,
                                                            *rider.operands)
    return res[0], res[1], res[2], res[3:]


def _shift_down(z, n, rows):
    return jnp.where(rows >= n, pltpu.roll(z, n, 0), 0.0)


def _shift_up(z, n, rows, S):
    return jnp.where(rows < S - n, pltpu.roll(z, S - n, 0), 0.0)


def _conv_specs(S, tc):
    strip = lambda p: pl.BlockSpec((None, S, tc), lambda j: (p, 0, j))
    return strip(0), strip(1), strip(2), pl.BlockSpec((3, tc), lambda j: (0, j))


def _conv_fwd(proj3, w):
    _, S, D = proj3.shape
    tc = LANES

    def body(b_ref, c_ref, u_ref, w_ref, out_ref):
        z = c_ref[...].astype(F32) * u_ref[...].astype(F32)
        rows = lax.broadcasted_iota(jnp.int32, (S, tc), 0)
        zc = w_ref[0:1, :] * _shift_down(z, 2, rows) + w_ref[1:2, :] * _shift_down(z, 1, rows) + w_ref[2:3, :] * z
        out_ref[...] = (b_ref[...].astype(F32) * zc).astype(BF16)

    return pl.pallas_call(
        body, name="conv_fwd", grid=(D // tc,), in_specs=list(_conv_specs(S, tc)),
        out_specs=pl.BlockSpec((S, tc), lambda j: (0, j)), out_shape=_sds((S, D), BF16),
        compiler_params=_cparams("parallel"))(proj3, proj3, proj3, w)


def _conv_bwd(dbz, proj3, w):
    _, S, D = proj3.shape
    tc = LANES

    def body(d_ref, b_ref, c_ref, u_ref, w_ref, dp_ref, dw_ref):
        cv, uv, dv = c_ref[...].astype(F32), u_ref[...].astype(F32), d_ref[...].astype(F32)
        z = cv * uv
        rows = lax.broadcasted_iota(jnp.int32, (S, tc), 0)
        z1, z2 = _shift_down(z, 1, rows), _shift_down(z, 2, rows)
        zc = w_ref[0:1, :] * z2 + w_ref[1:2, :] * z1 + w_ref[2:3, :] * z
        dp_ref[0] = (dv * zc).astype(BF16)
        dzc = dv * b_ref[...].astype(F32)
        dz = w_ref[2:3, :] * dzc + w_ref[1:2, :] * _shift_up(dzc, 1, rows, S) + w_ref[0:1, :] * _shift_up(dzc, 2, rows, S)
        dp_ref[1] = (dz * uv).astype(BF16)
        dp_ref[2] = (dz * cv).astype(BF16)
        dw_ref[0:1, :] = jnp.sum(dzc * z2, axis=0, keepdims=True)
        dw_ref[1:2, :] = jnp.sum(dzc * z1, axis=0, keepdims=True)
        dw_ref[2:3, :] = jnp.sum(dzc * z, axis=0, keepdims=True)

    sb, sc_, su, sw = _conv_specs(S, tc)
    return pl.pallas_call(
        body, name="conv_bwd", grid=(D // tc,),
        in_specs=[pl.BlockSpec((S, tc), lambda j: (0, j)), sb, sc_, su, sw],
        out_specs=[pl.BlockSpec((3, S, tc), lambda j: (0, 0, j)), pl.BlockSpec((3, tc), lambda j: (0, j))],
        out_shape=[_sds((3, S, D), BF16), _sds((3, D), F32)],
        compiler_params=_cparams("parallel"))(dbz, proj3, proj3, proj3, w)


def _silu(c_all):
    def body(c_ref, o_ref):
        cv = c_ref[...]
        o_ref[...] = cv * (1.0 / (1.0 + jnp.exp(-cv)))

    vm = pl.BlockSpec(memory_space=pltpu.VMEM)
    return pl.pallas_call(body, name="cond_silu", in_specs=[vm], out_specs=vm, out_shape=_sds(c_all.shape, F32))(c_all)


def _mod_fwd(cond, w_mod, b_cols):
    L, D, ncol = w_mod.shape
    B = cond.shape[0]
    tk, tn = min(512, D), min(1024, ncol)
    nk = D // tk

    def body(c_ref, w_ref, b_ref, out_ref, acc):
        kk = pl.program_id(2)
        part = lax.dot_general(c_ref[...].astype(BF16), w_ref[...].astype(BF16), DIMS_NN, preferred_element_type=F32)

        @pl.when(kk == 0)
        def _():
            acc[...] = part

        @pl.when(kk > 0)
        def _():
            acc[...] += part

        @pl.when(kk == nk - 1)
        def _():
            out_ref[...] = acc[...] + b_ref[...]

    return pl.pallas_call(
        body, name="mod_fwd", grid=(L, ncol // tn, nk),
        in_specs=[pl.BlockSpec((B, tk), lambda l, j, k: (0, k)),
                  pl.BlockSpec((None, tk, tn), lambda l, j, k: (l, k, j)),
                  pl.BlockSpec((None, 1, tn), lambda l, j, k: (l, 0, j))],
        out_specs=pl.BlockSpec((None, B, tn), lambda l, j, k: (l, 0, j)),
        out_shape=_sds((L, B, ncol), F32),
        scratch_shapes=[pltpu.VMEM((B, tn), F32)],
        compiler_params=_cparams("parallel", "parallel", "arbitrary"))(cond, w_mod, b_cols)


def _adamw_math(w, g, m, v):
    m = ADAM_B1 * m + (1.0 - ADAM_B1) * g
    v = ADAM_B2 * v + (1.0 - ADAM_B2) * (g * g)
    m_hat = m / (1.0 - ADAM_B1 ** ADAM_STEP)
    v_hat = v / (1.0 - ADAM_B2 ** ADAM_STEP)
    delta = -ADAM_LR * (m_hat / (jnp.sqrt(v_hat) + ADAM_EPS) + ADAM_WD * w)
    return delta, m, v


def _adamw(name, w, g, m, v):
    shape = w.shape
    cols = shape[-1] if w.ndim <= 3 else shape[-2] * shape[-1]
    rows = w.size // cols
    w2, g2, m2, v2 = (t.reshape(rows, cols) for t in (w, g, m, v))
    tr = _row_tile(rows, cols * 4, limit=2 * 1024 * 1024, mult=8)
    spec = pl.BlockSpec((tr, cols), lambda i: (i, 0))

    def body(w_ref, g_ref, m_ref, v_ref, d_ref, nm_ref, nv_ref):
        d, nm, nv = _adamw_math(w_ref[...], g_ref[...], m_ref[...], v_ref[...])
        d_ref[...] = d
        nm_ref[...] = nm
        nv_ref[...] = nv

    outs = pl.pallas_call(body, name=name, grid=(rows // tr,), in_specs=[spec] * 4, out_specs=[spec] * 3,
                          out_shape=[_sds((rows, cols), F32)] * 3, compiler_params=_cparams("parallel"))(w2, g2, m2, v2)
    return tuple(t.reshape(shape) for t in outs)


def _adamw_mod(w, cond_t, dmod_cols, m, v, rider=NO_RIDER):
    L, D, ncol = w.shape
    B = cond_t.shape[1]
    tr, tc = min(256, D), min(1024, ncol)
    blk = pl.BlockSpec((None, tr, tc), lambda l, i, j: (l, i, j))
    grid = (L, D // tr, ncol // tc)
    n_ri, n_ro = len(rider.operands), len(rider.out_shape)

    def body(*refs):
        w_ref, ct_ref, dm_ref, m_ref, v_ref = refs[:5]
        r_in = refs[5:5 + n_ri]
        g_ref, d_ref, nm_ref, nv_ref = refs[5 + n_ri:9 + n_ri]
        r_out = refs[9 + n_ri:9 + n_ri + n_ro]
        r_sems = refs[9 + n_ri + n_ro:]
        ids = [pl.program_id(a) for a in range(3)]

        if rider.start is not None:
            @pl.when((ids[0] == 0) & (ids[1] == 0) & (ids[2] == 0))
            def _():
                rider.start(r_in, r_out, r_sems)

        g = lax.dot_general(ct_ref[...], dm_ref[...], DIMS_NN, precision=lax.Precision.HIGHEST,
                            preferred_element_type=F32)
        d, nm, nv = _adamw_math(w_ref[...], g, m_ref[...], v_ref[...])
        g_ref[...] = g
        d_ref[...] = d
        nm_ref[...] = nm
        nv_ref[...] = nv

        if rider.finish is not None:
            @pl.when((ids[0] == grid[0] - 1) & (ids[1] == grid[1] - 1) & (ids[2] == grid[2] - 1))
            def _():
                _rider_end(rider, r_in, r_out, r_sems)

    hosted = rider.start is not None
    res = pl.pallas_call(
        body, name="adamw_w_mod", grid=grid,
        in_specs=[blk, pl.BlockSpec((tr, B), lambda l, i, j: (i, 0)),
                  pl.BlockSpec((None, B, tc), lambda l, i, j: (l, 0, j)), blk, blk] + [ANY] * n_ri,
        out_specs=[blk] * 4 + [ANY] * n_ro, out_shape=[_sds((L, D, ncol), F32)] * 4 + list(rider.out_shape),
        scratch_shapes=list(rider.sems), input_output_aliases={5 + i: 4 + o for i, o in rider.aliases.items()},
        compiler_params=_cparams(*(("arbitrary",) * 3 if hosted else ("parallel",) * 3)))(
            w, cond_t, dmod_cols, m, v, *rider.operands)
    return (*res[:4], res[4:])


def _cast_into_full(name, ws, kinds, k_idx, rider=NO_RIDER):
    L, R, C = ws[0].shape
    assert all(w.shape == (L, R, C) for w in ws)
    n = len(ws)
    Rh = R // 2
    tr = _row_tile(Rh, C * 4)
    grid = (L, 2, Rh // tr)
    out_shape, out_specs = [], []
    for kind in kinds:
        if kind == "row":
            out_shape.append(_sds((L, N_CHIPS, 2, Rh, C), BF16))
            out_specs.append(pl.BlockSpec((None, None, None, tr, C), lambda l, h, i, k_ref: (l, k_ref[0], h, i, 0)))
        else:
            out_shape.append(_sds((L, 2, Rh, N_CHIPS * C), BF16))
            out_specs.append(pl.BlockSpec((None, None, tr, C), lambda l, h, i, k_ref: (l, h, i, k_ref[0])))
    n_ri, n_ro = len(rider.operands), len(rider.out_shape)

    def body(k_ref, *refs):
        r_in = refs[n:n + n_ri]
        r_out = refs[2 * n + n_ri:2 * n + n_ri + n_ro]
        r_sems = refs[2 * n + n_ri + n_ro:]
        ids = [pl.program_id(a) for a in range(3)]
        if rider.start is not None:
            @pl.when((ids[0] == 0) & (ids[1] == 0) & (ids[2] == 0))
            def _():
                rider.start(r_in, r_out, r_sems)
        for a in range(n):
            refs[n + n_ri + a][...] = refs[a][...].astype(BF16)
        if rider.finish is not None:
            @pl.when((ids[0] == grid[0] - 1) & (ids[1] == grid[1] - 1) & (ids[2] == grid[2] - 1))
            def _():
                _rider_end(rider, r_in, r_out, r_sems)

    hosted = rider.start is not None
    res = pl.pallas_call(
        body, name=name,
        grid_spec=pltpu.PrefetchScalarGridSpec(
            num_scalar_prefetch=1, grid=grid,
            in_specs=[pl.BlockSpec((None, None, tr, C), lambda l, h, i, k_ref: (l, h, i, 0))] * n + [ANY] * n_ri,
            out_specs=out_specs + [ANY] * n_ro, scratch_shapes=list(rider.sems)),
        out_shape=out_shape + list(rider.out_shape),
        input_output_aliases={1 + n + i: n + o for i, o in rider.aliases.items()},
        compiler_params=_cparams(*(("arbitrary",) * 3 if hosted else ("parallel",) * 3)))(
            k_idx, *[w.reshape(L, 2, Rh, C) for w in ws], *rider.operands)
    return res[:n], res[n:]


def _pair_sum(name, g5, ra, c_idx):
    L, A, _, Rh, Cc = g5.shape
    tr = _row_tile(Rh, Cc * 4)

    def body(c_ref, g_ref, r_ref, o_ref):
        o_ref[...] = (g_ref[...].astype(F32) + r_ref[...].astype(F32)).astype(BF16)

    blk = pl.BlockSpec((None, None, tr, Cc), lambda l, a, i, c_ref: (l, a, i, 0))
    return pl.pallas_call(
        body, name=name,
        grid_spec=pltpu.PrefetchScalarGridSpec(
            num_scalar_prefetch=1, grid=(L, A, Rh // tr),
            in_specs=[pl.BlockSpec((None, None, None, tr, Cc), lambda l, a, i, c_ref: (l, a, c_ref[0], i, 0)), blk],
            out_specs=blk),
        out_shape=_sds((L, A, Rh, Cc), BF16),
        compiler_params=_cparams("parallel", "parallel", "parallel"))(c_idx, g5, ra)


def _chip_sum(name, p, rb, kc_idx, kind, layer=0, n_layers=1, prev=None):
    _, A, Rh, Cc = p.shape
    C = rb.shape[-1]
    tr = _row_tile(Rh, C * 4)
    if kind == "row":
        own = pl.BlockSpec((None, None, tr, C), lambda i, kc: (0, kc[0], i, 0))
    else:
        own = pl.BlockSpec((None, None, tr, C), lambda i, kc: (0, 0, i, kc[0]))
    peer = lambda j: pl.BlockSpec((None, None, tr, C), lambda i, kc: (j, 0, i, 0))

    def body(kc_ref, p_ref, r0_ref, r1_ref, r2_ref, *rest):
        o_ref = rest[-1]
        o_ref[...] = ((p_ref[...].astype(F32) + r0_ref[...].astype(F32)) + r1_ref[...].astype(F32)) + r2_ref[...].astype(F32)

    operands = [kc_idx, p, rb, rb, rb] + ([prev] if prev is not None else [])
    return pl.pallas_call(
        body, name=name,
        grid_spec=pltpu.PrefetchScalarGridSpec(
            num_scalar_prefetch=1, grid=(Rh // tr,),
            in_specs=[own, peer(0), peer(1), peer(2)] + ([ANY] if prev is not None else []),
            out_specs=pl.BlockSpec((None, None, tr, C), lambda i, kc: (layer, kc[1], i, 0))),
        out_shape=_sds((n_layers, 2, Rh, C), F32),
        input_output_aliases={5: 0} if prev is not None else {},
        compiler_params=_cparams("parallel"))(*operands)


def _mesh_place():
    x, y, c = lax.axis_index("x"), lax.axis_index("y"), lax.axis_index("c")
    chips = [(1 - x, y), (x, 1 - y), (1 - x, 1 - y)]
    return x, y, c, chips


def _remote(src, dst, send_sem, recv_sem, to):
    return pltpu.make_async_remote_copy(src_ref=src, dst_ref=dst, send_sem=send_sem, recv_sem=recv_sem,
                                        device_id=to, device_id_type=MESH_ID)


def _small_allgather(name, v, with_sum=False, rider=NO_RIDER):
    R, N = v.shape
    n_ri, n_ro, n_own = len(rider.operands), len(rider.out_shape), 2 if with_sum else 1

    def body(*refs):
        r_in = refs[1:1 + n_ri]
        r_out = refs[1 + n_ri + n_own:1 + n_ri + n_own + n_ro]
        r_sems = refs[1 + n_ri + n_own + n_ro + 3:]
        own = (refs[0],) + refs[1 + n_ri:1 + n_ri + n_own] + refs[1 + n_ri + n_own + n_ro:1 + n_ri + n_own + n_ro + 3]
        if with_sum:
            x_ref, out_ref, sum_ref, send_sems, recv_sems, local_sem = own
        else:
            x_ref, out_ref, send_sems, recv_sems, local_sem = own
        if rider.start is not None:
            rider.start(r_in, r_out, r_sems)
        x, y, c, chips = _mesh_place()
        me, sibling = (x, y, c), (x, y, 1 - c)

        def rows(px, py, pc):
            return out_ref.at[pl.ds((4 * px + 2 * py + pc) * R, R), :]

        def copy(k, block, to, src=None):
            return _remote(rows(*block) if src is None else src, rows(*block), send_sems.at[k], recv_sems.at[k], to)

        mine = pltpu.make_async_copy(x_ref, rows(*me), local_sem)
        mine.start()
        first = [copy(0, me, sibling, src=x_ref)]
        first += [copy(1 + j, me, (*chip, c), src=x_ref) for j, chip in enumerate(chips)]
        for cp in first:
            cp.start()
        passed = [copy(4 + j, (*chip, c), sibling) for j, chip in enumerate(chips)]
        for j, chip in enumerate(chips):
            copy(1 + j, (*chip, c), me).wait_recv()
            passed[j].start()
        copy(0, sibling, me).wait_recv()
        for j, chip in enumerate(chips):
            copy(4 + j, (*chip, 1 - c), me).wait_recv()
        for cp in first + passed:
            cp.wait_send()
        mine.wait()
        if with_sum:
            total = out_ref[0:R, :]
            for p in range(1, 8):
                total = total + out_ref[p * R:(p + 1) * R, :]
            sum_ref[...] = total
        if rider.finish is not None:
            _rider_end(rider, r_in, r_out, r_sems)

    vm = pl.BlockSpec(memory_space=pltpu.VMEM)
    out_shape = [_sds((8 * R, N), F32)] + ([_sds((R, N), F32)] if with_sum else [])
    res = pl.pallas_call(
        body, name=name, out_shape=out_shape + list(rider.out_shape), in_specs=[vm] + [ANY] * n_ri,
        out_specs=[vm] * n_own + [ANY] * n_ro,
        scratch_shapes=[pltpu.SemaphoreType.DMA((7,)), pltpu.SemaphoreType.DMA((7,)), pltpu.SemaphoreType.DMA]
        + list(rider.sems),
        input_output_aliases={1 + i: n_own + o for i, o in rider.aliases.items()},
        compiler_params=pltpu.CompilerParams(vmem_limit_bytes=VMEM_LIMIT_BYTES))(v, *rider.operands)
    if rider.start is not None:
        return (*res[:n_own], res[n_own:])
    return res if with_sum else res[0]


def _full_place(ref, kind, C, kk, half, layer=None):
    lead = slice(None) if layer is None else pl.ds(layer, 1)
    if kind == "row":
        return ref.at[lead, kk, half]
    return ref.at[lead, half, :, pl.ds(pl.multiple_of(kk * C, LANES), C)]


def _gather_rider(fulls, kinds, shard_cols, layers=None, peers=(0, 1, 2)):
    n = len(fulls)
    layers = layers or [None] * n
    rows = [f.shape[3] if kind == "row" else f.shape[2] for f, kind in zip(fulls, kinds)]
    n_chunks = 2 if all(r % 32 == 0 for r in rows) else 1

    def copies(outs, sems):
        x, y, c, chips = _mesh_place()
        k = 2 * x + y

        def place(a, kk, half, ch):
            rc = rows[a] // n_chunks
            return _full_place(outs[a], kinds[a], shard_cols[a], kk, half, layers[a]).at[:, pl.ds(ch * rc, rc), :]

        def copy(a, j, ch, ref, to):
            s = 6 * (n_chunks * a + ch) + j
            return _remote(ref, ref, sems[0].at[s], sems[1].at[s], to)

        return (x, y, c), [(j, chip) for j, chip in enumerate(chips) if j in peers], k, place, copy

    def start(_, outs, sems):
        (x, y, c), chips, k, place, copy = copies(outs, sems)
        for ch in range(n_chunks):
            for j, chip in chips:
                for a in range(n):
                    copy(a, j, ch, place(a, k, c, ch), (*chip, c)).start()

    def pass_on(outs, sems, ch):
        (x, y, c), chips, k, place, copy = copies(outs, sems)
        for j, chip in chips:
            kj = 2 * chip[0] + chip[1]
            for a in range(n):
                copy(a, j, ch, place(a, kj, c, ch), (x, y, c)).wait_recv()
                copy(a, 3 + j, ch, place(a, kj, c, ch), (x, y, 1 - c)).start()

    def mid(_, outs, sems):
        pass_on(outs, sems, 0)

    def finish(_, outs, sems):
        pass_on(outs, sems, n_chunks - 1)
        (x, y, c), chips, k, place, copy = copies(outs, sems)
        for ch in range(n_chunks):
            for j, chip in chips:
                kj = 2 * chip[0] + chip[1]
                for a in range(n):
                    copy(a, 3 + j, ch, place(a, kj, 1 - c, ch), (x, y, c)).wait_recv()
        for ch in range(n_chunks):
            for j, chip in chips:
                kj = 2 * chip[0] + chip[1]
                for a in range(n):
                    copy(a, j, ch, place(a, k, c, ch), (*chip, c)).wait_send()
                    copy(a, 3 + j, ch, place(a, kj, c, ch), (x, y, 1 - c)).wait_send()

    n_sems = 6 * n * n_chunks
    return Rider(tuple(fulls), tuple(_sds(f.shape, BF16) for f in fulls), {a: a for a in range(n)},
                 (pltpu.SemaphoreType.DMA((n_sems,)), pltpu.SemaphoreType.DMA((n_sems,))), start, finish,
                 mid if n_chunks == 2 else None)


def _scatter_rider(ps, kinds, shard_cols, peers=(0, 1, 2), into=None):
    n = len(ps)

    def copies(ins, outs, sems):
        x, y, c, chips = _mesh_place()
        cps = []
        for j, chip in enumerate(chips):
            if j not in peers:
                continue
            kj = 2 * chip[0] + chip[1]
            for a in range(n):
                C = shard_cols[a]
                src = ins[a].at[:, kj] if kinds[a] == "row" else ins[a].at[:, 0, :, pl.ds(pl.multiple_of(kj * C, LANES), C)]
                cps.append(_remote(src, outs[a].at[j], sems[0].at[3 * a + j], sems[1].at[3 * a + j], (*chip, c)))
        return cps

    def start(ins, outs, sems):
        for cp in copies(ins, outs, sems):
            cp.start()

    def finish(ins, outs, sems):
        cps = copies(ins, outs, sems)
        for cp in cps:
            cp.wait_recv()
        for cp in cps:
            cp.wait_send()

    out_shape = tuple(_sds((3, p.shape[0], p.shape[2], C), BF16) for p, C in zip(ps, shard_cols))
    aliases = {n + a: a for a in range(n)} if into is not None else {}
    return Rider(tuple(ps) + tuple(into or ()), out_shape, aliases,
                 (pltpu.SemaphoreType.DMA((3 * n,)), pltpu.SemaphoreType.DMA((3 * n,))), start, finish)


def _run_rider(name, rider):
    n_in, n_out = len(rider.operands), len(rider.out_shape)

    def body(*refs):
        ins, outs, sems = refs[:n_in], refs[n_in:n_in + n_out], refs[n_in + n_out:]
        rider.start(ins, outs, sems)
        _rider_end(rider, ins, outs, sems)

    return pl.pallas_call(
        body, name=name, out_shape=list(rider.out_shape), in_specs=[ANY] * n_in, out_specs=[ANY] * n_out,
        input_output_aliases=dict(rider.aliases), scratch_shapes=list(rider.sems),
        compiler_params=pltpu.CompilerParams(vmem_limit_bytes=VMEM_LIMIT_BYTES))(*rider.operands)


def _exchange_rider(g5s):
    n = len(g5s)

    def copies(ins, outs, sems):
        x, y, c, _ = _mesh_place()
        return [_remote(ins[a].at[:, :, 1 - c], outs[a], sems[0].at[a], sems[1].at[a], (x, y, 1 - c)) for a in range(n)]

    def start(ins, outs, sems):
        for cp in copies(ins, outs, sems):
            cp.start()

    def finish(ins, outs, sems):
        cps = copies(ins, outs, sems)
        for cp in cps:
            cp.wait_recv()
        for cp in cps:
            cp.wait_send()

    out_shape = tuple(_sds((g.shape[0], g.shape[1], g.shape[3], g.shape[4]), BF16) for g in g5s)
    return Rider(tuple(g5s), out_shape, {}, (pltpu.SemaphoreType.DMA((n,)), pltpu.SemaphoreType.DMA((n,))), start, finish)


def _share_rider(fs):
    n = len(fs)

    def start(_, outs, sems):
        x, y, c, _p = _mesh_place()
        for a in range(n):
            mine = outs[a].at[:, c]
            _remote(mine, mine, sems[0].at[a], sems[1].at[a], (x, y, 1 - c)).start()

    def finish(_, outs, sems):
        x, y, c, _p = _mesh_place()
        for a in range(n):
            theirs = outs[a].at[:, 1 - c]
            _remote(theirs, theirs, sems[0].at[a], sems[1].at[a], (x, y, c)).wait_recv()
        for a in range(n):
            mine = outs[a].at[:, c]
            _remote(mine, mine, sems[0].at[a], sems[1].at[a], (x, y, 1 - c)).wait_send()

    return Rider(tuple(fs), tuple(_sds(f.shape, F32) for f in fs), {a: a for a in range(n)},
                 (pltpu.SemaphoreType.DMA((n,)), pltpu.SemaphoreType.DMA((n,))), start, finish)


def _both_riders(r1, r2):
    ni, no, ns = len(r1.operands), len(r1.out_shape), len(r1.sems)
    aliases = dict(r1.aliases)
    aliases.update({ni + i: no + o for i, o in r2.aliases.items()})

    def start(ins, outs, sems):
        r1.start(ins[:ni], outs[:no], sems[:ns])
        r2.start(ins[ni:], outs[no:], sems[ns:])

    def finish(ins, outs, sems):
        _rider_end(r1, ins[:ni], outs[:no], sems[:ns])
        _rider_end(r2, ins[ni:], outs[no:], sems[ns:])

    return Rider(r1.operands + r2.operands, r1.out_shape + r2.out_shape, aliases, r1.sems + r2.sems, start, finish)


def _pack_rows(parts, lane_mult=1024):
    flat = jnp.concatenate([p.reshape(-1).astype(F32) for p in parts])
    n = -(-flat.shape[0] // (8 * lane_mult)) * lane_mult
    return jnp.pad(flat, (0, 8 * n - flat.shape[0])).reshape(8, n)


def _relu2(acc):
    r = jnp.maximum(acc, 0.0)
    return r, r * r


def _times_2r(acc, r):
    return (acc * (2.0 * r.astype(F32)),)


def kernel(x, c, positions, w_mod, b_mod, norm_g, mla_w_in, mla_g_q, mla_g_kv, mla_w_uq, mla_w_ukv, mla_w_o, conv_w_in, conv_w, conv_w_out, mlp_w_up, mlp_w_down, loss_target, m_w_mod, m_b_mod, m_norm_g, m_mla_w_in, m_mla_g_q, m_mla_g_kv, m_mla_w_uq, m_mla_w_ukv, m_mla_w_o, m_conv_w_in, m_conv_w, m_conv_w_out, m_mlp_w_up, m_mlp_w_down, v_w_mod, v_b_mod, v_norm_g, v_mla_w_in, v_mla_g_q, v_mla_g_kv, v_mla_w_uq, v_mla_w_ukv, v_mla_w_o, v_conv_w_in, v_conv_w, v_conv_w_out, v_mlp_w_up, v_mlp_w_down):
    S, D = x.shape[1], x.shape[2]
    Dq = D // N_CHIPS
    ncol = w_mod.shape[2]
    n_mod = N_CHIPS * ncol // D
    F = mlp_w_up.shape[2] * N_CHIPS
    lat_dim = mla_w_in.shape[2]
    rank = mla_g_q.shape[1]
    H = mla_w_uq.shape[2]
    d_qk = mla_w_uq.shape[3]
    assert mla_g_kv.shape[1] == rank and lat_dim == 2 * rank + QK_ROPE and d_qk == QK_NOPE + QK_ROPE
    assert mla_w_ukv.shape[3] == QK_NOPE + V_HEAD and x.shape[0] == 1 and n_mod == 6
    assert norm_g.shape[0] == 2 and mla_w_in.shape[0] == 1 and conv_w_in.shape[0] == 1
    lat_pad = 2 * rank + LANES
    scale = float(d_qk) ** -0.5

    xi, yi, ci = lax.axis_index("x"), lax.axis_index("y"), lax.axis_index("c")
    chip = 2 * xi + yi
    dev = 2 * chip + ci
    c_idx = jnp.reshape(ci, (1,)).astype(jnp.int32)
    k_idx = jnp.reshape(chip, (1,)).astype(jnp.int32)

    n1 = D + 2 * D + 3 * Dq
    g1 = _small_allgather("gather_small_inputs", _pack_rows([c, norm_g, conv_w])).reshape(8, -1)
    c_all = g1[:, :D]
    by_chip = g1[0::2]
    norm_full = jnp.concatenate([by_chip[kk, D:3 * D].reshape(2, 4, Dq) for kk in range(N_CHIPS)], axis=-1)
    convw_full = jnp.concatenate([by_chip[kk, 3 * D:n1].reshape(3, Dq) for kk in range(N_CHIPS)], axis=-1)

    b_cols = lax.dynamic_slice(b_mod, (0, chip * ncol), (2, ncol)).reshape(2, 1, ncol)
    cond_all = _silu(c_all)
    mod_cols = _mod_fwd(cond_all, w_mod, b_cols)
    g2 = _small_allgather("gather_mod", _pack_rows([mod_cols]))
    g2 = g2.reshape(8, -1)[0::2, :2 * 8 * ncol].reshape(N_CHIPS, 2, 8, ncol)
    mod_all = jnp.transpose(g2, (2, 1, 0, 3)).reshape(8, 2, n_mod * D)
    mod_me = lax.dynamic_index_in_dim(mod_all, dev, axis=0, keepdims=False)
    mods = [[mod_me[l, i * D:(i + 1) * D].reshape(1, D) for i in range(n_mod)] for l in range(2)]
    ng = [[norm_full[l, i].reshape(1, D) for i in range(4)] for l in range(2)]

    pos = positions[0].astype(F32)
    inv_freq = ROPE_THETA ** (-jnp.arange(0, QK_ROPE, 2, dtype=F32) / QK_ROPE)
    ang = pos[:, None] * inv_freq
    cos, sin = jnp.cos(ang), jnp.sin(ang)
    zero = jnp.zeros_like(cos)
    rope_tabs = (jnp.concatenate([cos, cos, zero, zero], axis=1),
                 jnp.concatenate([-sin, zero, zero, zero], axis=1),
                 jnp.concatenate([zero, sin, zero, zero], axis=1))

    weights = [("mla_w_in", mla_w_in, "row"), ("mla_w_uq", mla_w_uq.reshape(1, rank // N_CHIPS, H * d_qk), "row"),
               ("mla_w_ukv", mla_w_ukv.reshape(1, rank // N_CHIPS, H * QK_PAD), "row"), ("mla_w_o", mla_w_o, "row"),
               ("conv_w_in", conv_w_in, "col"), ("conv_w_out", conv_w_out, "row"),
               ("mlp_w_up", mlp_w_up, "col"), ("mlp_w_down", mlp_w_down, "row")]
    kinds = [k for _, _, k in weights]
    shard_shapes = [w.shape for _, w, _ in weights]
    shard_cols = [s[2] for s in shard_shapes]
    W_IN, W_UQ, W_UKV, W_O, W_CIN, W_COUT, W_UP, W_DOWN = range(8)
    mla_idx = [W_IN, W_UQ, W_UKV, W_O]
    casted = [_cast_into_full("cast_" + nm, [w], [kind], k_idx)[0][0] for nm, w, kind in weights[:W_UP]]

    def view(i, buf):
        L, R, C = shard_shapes[i]
        return buf.reshape((L, N_CHIPS * R, C) if kinds[i] == "row" else (L, R, N_CHIPS * C))

    NEIGHBOURS, DIAGONAL = (0, 1), (2,)

    def gather_of(bufs, idx, layers=None, peers=(0, 1, 2)):
        return _gather_rider(bufs, [kinds[i] for i in idx], [shard_cols[i] for i in idx], layers, peers)

    def scatter_of(ps, idx, peers=(0, 1, 2), into=None):
        return _scatter_rider(ps, [kinds[i] for i in idx], [shard_cols[i] for i in idx], peers, into)

    def halves(items):
        g5s = []
        for _, i, g in items:
            _, R, C = shard_shapes[i]
            g5s.append(g.reshape((1, N_CHIPS, 2, R // 2, C) if kinds[i] == "row" else (1, 1, 2, R // 2, N_CHIPS * C)))
        return g5s

    def pair_sums(items, g5s, ras):
        return [_pair_sum("pair_sum_" + nm, g5, ra, c_idx) for (nm, _, _), g5, ra in zip(items, g5s, ras)]

    mlp_casted, got = _cast_into_full("cast_mlp_w", [mlp_w_up, mlp_w_down], [kinds[W_UP], kinds[W_DOWN]], k_idx,
                                      gather_of([casted[i] for i in mla_idx], mla_idx))
    casted += list(mlp_casted)
    w_in_p = jnp.pad(view(W_IN, got[0])[0], ((0, 0), (0, lat_pad - lat_dim)))
    w_q_p = jnp.pad(view(W_UQ, got[1])[0].reshape(rank, H, d_qk), ((0, 0), (0, 0), (0, QK_PAD - d_qk))).reshape(rank, H * QK_PAD)
    w_ukv, w_o = view(W_UKV, got[2])[0], view(W_O, got[3])[0]
    HV = H * V_HEAD

    def layer_b(l, transposed):
        if transposed:
            return lambda tm, tn, tk: pl.BlockSpec((None, tn, tk), lambda i, j, k: (l, j, k))
        return lambda tm, tn, tk: pl.BlockSpec((None, tk, tn), lambda i, j, k: (l, k, j))

    def mlp_up(tag, l, h, w, rider=NO_RIDER):
        return _mm("mlp_up_" + tag, h, w, "nn", S, F, D, [_sds((S, F), BF16)] * 2, epilogue=_relu2,
                   b_spec=layer_b(l, False), rider=rider)

    def mlp_down(tag, l, a2, w, rider=NO_RIDER):
        return _mm("mlp_down_" + tag, a2, w, "nn", S, D, F, [_sds((S, D), BF16)], b_spec=layer_b(l, False), rider=rider)

    def mlp_bwd(tag, l, h, r, a2, dy, first=NO_RIDER, second_of=None):
        res = _mm("mlp_down_dx_" + tag, dy, w_down, "nt", S, F, D, [_sds((S, F), BF16)], epilogue=_times_2r,
                  b_spec=layer_b(l, True), rider=first,
                  extras=[(r, lambda tm, tn, tk: pl.BlockSpec((tm, tn), lambda i, j, k: (i, j)))])
        (da,), got_first = res if first.start is not None else (res, ())
        second = second_of(got_first) if second_of else NO_RIDER
        res = _mm("mlp_down_dw_" + tag, a2, dy, "tn", F, D, S, [_sds((F, D), BF16)], rider=second)
        (dw_down,), got_second = res if second_of else (res, ())
        (dh,) = _mm("mlp_up_dx_" + tag, da, w_up, "nt", S, D, F, [_sds((S, D), BF16)], b_spec=layer_b(l, True))
        (dw_up,) = _mm("mlp_up_dw_" + tag, h, da, "tn", D, F, S, [_sds((D, F), BF16)])
        return dh, dw_up, dw_down, got_first, got_second

    x0 = x[0]
    sh1, sc1, gt1, sh2, sc2, gt2 = mods[0]
    (h1,) = _fwd_boundary("fwd_boundary_0", x0, None, None, None, ng[0][0], sc1, sh1)
    (lat,) = _mm("mla_in", h1, w_in_p, "nn", S, lat_pad, D, [_sds((S, lat_pad), F32)], tn=lat_pad)
    cq, ckv, kr = _latent_fwd(lat, mla_g_q, mla_g_kv, rope_tabs, rank)

    def rope_q(acc, cos_p, sin_lo, sin_hi):
        parts = []
        for hh in range(acc.shape[1] // QK_PAD):
            parts.append(acc[:, hh * QK_PAD:hh * QK_PAD + QK_NOPE])
            parts.append(_rope(acc[:, hh * QK_PAD + QK_NOPE:(hh + 1) * QK_PAD], cos_p, sin_lo, sin_hi))
        return (jnp.concatenate(parts, axis=1),)

    tab_extra = lambda tm, tn, tk: pl.BlockSpec((tm, LANES), lambda i, j, k: (i, 0))
    (q,) = _mm("mla_q", cq, w_q_p, "nn", S, H * QK_PAD, rank, [_sds((S, H * QK_PAD), BF16)], epilogue=rope_q,
               extras=[(t, tab_extra) for t in rope_tabs], tn=2 * QK_PAD)
    (kv,) = _mm("mla_kv", ckv, w_ukv, "nn", S, H * QK_PAD, rank, [_sds((S, H * QK_PAD), BF16)])
    rest_idx = [W_UP, W_DOWN]
    o, lse, (up_buf, down_buf) = _attn_fwd_tri(
        q, kv, kr, H, scale, gather_of([casted[i] for i in rest_idx], rest_idx, [0, 0]))
    (y1,), (cout_buf,) = _mm("mla_out", o, w_o, "nn", S, D, HV, [_sds((S, D), BF16)],
                             rider=gather_of([casted[W_COUT]], [W_COUT]))
    x1, h2 = _fwd_boundary("fwd_boundary_1", x0, y1, gt1, ng[0][1], ng[0][2], sc2, sh2)
    (r2, a2), (cin_buf,) = mlp_up("0", 0, h2, view(W_UP, up_buf), gather_of([casted[W_CIN]], [W_CIN]))
    (y2,), (up_buf,) = mlp_down("0", 0, a2, view(W_DOWN, down_buf), gather_of([up_buf], [W_UP], [1]))
    w_cin, w_cout, w_up = view(W_CIN, cin_buf)[0], view(W_COUT, cout_buf)[0], view(W_UP, up_buf)

    sh1b, sc1b, gt1b, sh2b, sc2b, gt2b = mods[1]
    x2, h3 = _fwd_boundary("fwd_boundary_2", x1, y2, gt2, ng[0][3], ng[1][0], sc1b, sh1b)
    nD = lambda tn: D // tn
    (proj3,), (down_buf,) = _mm(
        "conv_in", h3, w_cin, "nn", S, 3 * D, D, [_sds((3, S, D), BF16)], tn=min(1024, D),
        rider=gather_of([down_buf], [W_DOWN], [1], NEIGHBOURS),
        out_specs=[lambda tm, tn, tk: pl.BlockSpec((None, tm, tn), lambda i, j, k: (j // nD(tn), i, j % nD(tn)))])
    bz = _conv_fwd(proj3, convw_full)
    (y3,) = _mm("conv_out", bz, w_cout, "nn", S, D, D, [_sds((S, D), BF16)])
    x3, h4 = _fwd_boundary("fwd_boundary_3", x2, y3, gt1b, ng[1][1], ng[1][2], sc2b, sh2b)
    (r4, a4), (down_buf,) = mlp_up("1", 1, h4, w_up, gather_of([down_buf], [W_DOWN], [1], DIAGONAL))
    w_down = view(W_DOWN, down_buf)
    (y4,) = mlp_down("1", 1, a4, w_down)

    dx4, dy4, sums_l, loss_acc = _loss_boundary("loss_boundary", x3, y4, gt2b, ng[1][3], loss_target[0])

    dh4, dw_up1, dw_down1, _, _ = mlp_bwd("1", 1, h4, r4, a4, dy4)
    dx3, dy3, sums_3, _ = _bwd_boundary("bwd_boundary_3", dx4, dh4, x3, y3, gt1b, ng[1][1], ng[1][2], sc2b)

    items = [("mlp_w_up_1", W_UP, dw_up1), ("mlp_w_down_1", W_DOWN, dw_down1)]
    g5s = halves(items)
    (dbz,), ras = _mm("conv_out_dx", dy3, w_cout, "nt", S, D, D, [_sds((S, D), BF16)], rider=_exchange_rider(g5s))
    ps_up1, ps_down1 = pair_sums(items, g5s, ras)
    (dw_cout,) = _mm("conv_out_dw", bz, dy3, "tn", D, D, S, [_sds((D, D), BF16)])
    dproj3, dconvw = _conv_bwd(dbz, proj3, convw_full)
    (dh3,), (rb_up1,) = _mm(
        "conv_in_dx", dproj3, w_cin, "nt", S, D, 3 * D, [_sds((S, D), BF16)], tk=D,
        rider=scatter_of([ps_up1], [W_UP], NEIGHBOURS),
        a_spec=lambda tm, tn, tk: pl.BlockSpec((None, tm, tk), lambda i, j, k: (k // (D // tk), i, k % (D // tk))))
    (dw_cin,), (rb_up1,) = _mm(
        "conv_in_dw", h3, dproj3, "tn", D, 3 * D, S, [_sds((D, 3 * D), BF16)], tn=min(1024, D),
        rider=scatter_of([ps_up1], [W_UP], DIAGONAL, [rb_up1]),
        b_spec=lambda tm, tn, tk: pl.BlockSpec((None, tk, tn), lambda i, j, k: (j // nD(tn), k, j % nD(tn))))
    dx2, dy2, sums_2, _ = _bwd_boundary("bwd_boundary_2", dx3, dh3, x2, y2, gt2, ng[0][3], ng[1][0], sc1b)

    items = [("conv_w_in", W_CIN, dw_cin), ("conv_w_out", W_COUT, dw_cout)]
    g5s = halves(items)
    dh2, dw_up0, dw_down0, (rb_down1, *ras), (rb_down1,) = mlp_bwd(
        "0", 0, h2, r2, a2, dy2,
        _both_riders(scatter_of([ps_down1], [W_DOWN], NEIGHBOURS), _exchange_rider(g5s)),
        lambda got: scatter_of([ps_down1], [W_DOWN], DIAGONAL, [got[0]]))
    ps_cin, ps_cout = pair_sums(items, g5s, ras)
    dx1, dy1, sums_1, _ = _bwd_boundary("bwd_boundary_1", dx2, dh2, x1, y1, gt1, ng[0][1], ng[0][2], sc2)

    items = [("mlp_w_up_0", W_UP, dw_up0), ("mlp_w_down_0", W_DOWN, dw_down0)]
    g5s = halves(items)
    (dw_o,), ras = _mm("mla_out_dw", o, dy1, "tn", HV, D, S, [_sds((HV, D), BF16)], rider=_exchange_rider(g5s))
    ps_up0, ps_down0 = pair_sums(items, g5s, ras)
    items = [(weights[W_O][0], W_O, dw_o)]
    g5s = halves(items)
    (do,), ras = _mm("mla_out_dx", dy1, w_o, "nt", S, HV, D, [_sds((S, HV), BF16)], rider=_exchange_rider(g5s))
    (ps_o,) = pair_sums(items, g5s, ras)
    dq, dkv, dkr, (rb_up0, rb_down0, rb_cin, rb_cout, rb_o) = _attn_bwd_tri(
        q, kv, kr, o, do, lse, rope_tabs, H, scale,
        scatter_of([ps_up0, ps_down0, ps_cin, ps_cout, ps_o], [W_UP, W_DOWN, W_CIN, W_COUT, W_O]))
    (dcq,) = _mm("mla_q_dx", dq, w_q_p, "nt", S, rank, H * QK_PAD, [_sds((S, rank), F32)])
    (dw_q_p,) = _mm("mla_q_dw", cq, dq, "tn", rank, H * QK_PAD, S, [_sds((rank, H * QK_PAD), BF16)])
    (dckv,) = _mm("mla_kv_dx", dkv, w_ukv, "nt", S, rank, H * QK_PAD, [_sds((S, rank), F32)])
    (dw_ukv,) = _mm("mla_kv_dw", ckv, dkv, "tn", rank, H * QK_PAD, S, [_sds((rank, H * QK_PAD), BF16)])
    dlat, sums_lat = _latent_bwd(lat, dcq, dckv, dkr, mla_g_q, mla_g_kv, rope_tabs, rank)
    (dw_in_p,) = _mm("mla_in_dw", h1, dlat, "tn", D, lat_pad, S, [_sds((D, lat_pad), BF16)], tn=lat_pad)
    late_idx = [W_IN, W_UQ, W_UKV]
    dw_mla = [dw_in_p[:, :lat_dim], dw_q_p.reshape(rank, H, QK_PAD)[:, :, :d_qk].reshape(rank, H * d_qk), dw_ukv]
    items = [(weights[i][0], i, g) for i, g in zip(late_idx, dw_mla)]
    g5s = halves(items)
    (dh1,), ras = _mm("mla_in_dx", dlat, w_in_p, "nt", S, D, lat_pad, [_sds((S, D), BF16)], rider=_exchange_rider(g5s))
    ps_mla = pair_sums(items, g5s, ras)
    grad_x, sums_0, _ = _bwd_boundary("bwd_boundary_0", dx1, dh1, x0, None, None, None, ng[0][0], sc1)

    kc_idx = jnp.stack([chip, ci]).astype(jnp.int32)
    fs_rest = [_chip_sum("chip_sum_" + weights[i][0], p, rb, kc_idx, kinds[i])
               for i, p, rb in [(W_O, ps_o, rb_o), (W_CIN, ps_cin, rb_cin), (W_COUT, ps_cout, rb_cout)]]
    for i, (p1, r1), (p0, r0) in [(W_UP, (ps_up1, rb_up1), (ps_up0, rb_up0)), (W_DOWN, (ps_down1, rb_down1), (ps_down0, rb_down0))]:
        f = _chip_sum("chip_sum_" + weights[i][0] + "_1", p1, r1, kc_idx, kinds[i], layer=1, n_layers=2)
        fs_rest.append(_chip_sum("chip_sum_" + weights[i][0] + "_0", p0, r0, kc_idx, kinds[i], layer=0, n_layers=2, prev=f))

    dmod0 = [sums_0[0], sums_0[1], sums_1[3], sums_1[0], sums_1[1], sums_2[3]]
    dmod1 = [sums_2[0], sums_2[1], sums_3[3], sums_3[0], sums_3[1], sums_l[3]]
    dng0 = [sums_0[2], sums_1[4], sums_1[2], sums_2[4]]
    dng1 = [sums_2[2], sums_3[4], sums_3[2], sums_l[4]]
    small = _pack_rows(dmod0 + dmod1 + dng0 + dng1 + [sums_lat[0], sums_lat[1], dconvw, loss_acc[0, 0:1]],
                       lane_mult=LANES)
    gathered, total, carried = _small_allgather(
        "gather_small_grads", small, with_sum=True, rider=_both_riders(scatter_of(ps_mla, late_idx), _share_rider(fs_rest)))
    rbs_mla, finals_rest = carried[:len(late_idx)], carried[len(late_idx):]
    n_dm = 2 * n_mod * D
    dmod_all = gathered.reshape(8, -1)[:, :n_dm].reshape(8, 2, n_mod * D)
    total = total.reshape(-1)
    g_b_mod = total[:n_dm].reshape(2, n_mod * D)
    g_norm = lax.dynamic_slice(total[n_dm:n_dm + 8 * D].reshape(2, 4, D), (0, 0, chip * Dq), (2, 4, Dq))
    off = n_dm + 8 * D
    g_gq = total[off:off + rank].reshape(1, rank)
    g_gkv = total[off + rank:off + 2 * rank].reshape(1, rank)
    off += 2 * rank
    g_convw = lax.dynamic_slice(total[off:off + 3 * D].reshape(1, 3, D), (0, 0, chip * Dq), (1, 3, Dq))
    loss = total[off + 3 * D]

    dmod_cols = jnp.transpose(lax.dynamic_slice(dmod_all.reshape(8, 2, N_CHIPS, ncol), (0, 0, chip, 0), (8, 2, 1, ncol))
                              .reshape(8, 2, ncol), (1, 0, 2))
    g_w_mod, d_w_mod, nm_w_mod, nv_w_mod, _ = _adamw_mod(w_mod, cond_all.T, dmod_cols, m_w_mod, v_w_mod)
    fs_mla = [_chip_sum("chip_sum_" + weights[i][0], p, rb, kc_idx, kinds[i]) for i, p, rb in zip(late_idx, ps_mla, rbs_mla)]
    finals = list(_run_rider("grad_pair_share_mla", _share_rider(fs_mla))) + list(finals_rest)
    orig = [mla_w_in, mla_w_uq, mla_w_ukv, mla_w_o, conv_w_in, conv_w_out, mlp_w_up, mlp_w_down]
    big_grads = [f.reshape(w.shape) for f, w in zip(finals, orig)]

    names = ["b_mod", "norm_g", "mla_w_in", "mla_g_q", "mla_g_kv", "mla_w_uq", "mla_w_ukv", "mla_w_o",
             "conv_w_in", "conv_w", "conv_w_out", "mlp_w_up", "mlp_w_down"]
    ws = [b_mod, norm_g, mla_w_in, mla_g_q, mla_g_kv, mla_w_uq, mla_w_ukv, mla_w_o, conv_w_in, conv_w, conv_w_out,
          mlp_w_up, mlp_w_down]
    ms = [m_b_mod, m_norm_g, m_mla_w_in, m_mla_g_q, m_mla_g_kv, m_mla_w_uq, m_mla_w_ukv, m_mla_w_o, m_conv_w_in,
          m_conv_w, m_conv_w_out, m_mlp_w_up, m_mlp_w_down]
    vs = [v_b_mod, v_norm_g, v_mla_w_in, v_mla_g_q, v_mla_g_kv, v_mla_w_uq, v_mla_w_ukv, v_mla_w_o, v_conv_w_in,
          v_conv_w, v_conv_w_out, v_mlp_w_up, v_mlp_w_down]
    gs = [g_b_mod, g_norm, big_grads[0], g_gq, g_gkv, big_grads[1], big_grads[2], big_grads[3], big_grads[4],
          g_convw, big_grads[5], big_grads[6], big_grads[7]]
    grads, deltas, new_ms, new_vs = [g_w_mod], [d_w_mod], [nm_w_mod], [nv_w_mod]
    for nm, w, g, m, v in zip(names, ws, gs, ms, vs):
        d, nm_, nv_ = _adamw("adamw_" + nm, w, g, m, v)
        grads.append(g)
        deltas.append(d)
        new_ms.append(nm_)
        new_vs.append(nv_)
    return (loss, grad_x[None], *grads, *deltas, *new_ms, *new_vs)
```

```python
from typing import NamedTuple

import jax
import jax.numpy as jnp
from jax import lax
from jax.experimental import pallas as pl
from jax.experimental.pallas import tpu as pltpu

F32 = jnp.float32
BF16 = jnp.bfloat16
NORM_EPS = 1e-6
ROPE_THETA = 10000.0
QK_NOPE = 128
QK_ROPE = 64
V_HEAD = 128
LANES = 128
QK_PAD = QK_NOPE + LANES
ADAM_LR, ADAM_B1, ADAM_B2, ADAM_EPS, ADAM_WD, ADAM_STEP = 0.001, 0.9, 0.999, 1e-08, 0.01, 10
VMEM_LIMIT_BYTES = 56 * 1024 * 1024
N_CHIPS = 4
MESH_ID = pl.DeviceIdType.MESH
ANY = pl.BlockSpec(memory_space=pl.ANY)
NEG_INF = float("-inf")

DIMS_NN = (((1,), (0,)), ((), ()))
DIMS_NT = (((1,), (1,)), ((), ()))
DIMS_TN = (((0,), (0,)), ((), ()))


def _cparams(*sem):
    return pltpu.CompilerParams(dimension_semantics=sem, vmem_limit_bytes=VMEM_LIMIT_BYTES)


def _row_tile(rows, row_bytes, limit=2 * 1024 * 1024, mult=16):
    if rows * row_bytes <= limit or rows % mult:
        return rows
    best = mult
    t = mult
    while t <= rows:
        if rows % t == 0 and t * row_bytes <= limit:
            best = t
        t += mult
    return best


def _rms(v):
    return lax.rsqrt(jnp.mean(v * v, axis=-1, keepdims=True) + NORM_EPS)


class Rider(NamedTuple):
    operands: tuple
    out_shape: tuple
    aliases: dict
    sems: tuple
    start: object
    finish: object
    mid: object = None


NO_RIDER = Rider((), (), {}, (), None, None)


def _rider_end(rider, r_in, r_out, r_sems):
    if rider.mid is not None:
        rider.mid(r_in, r_out, r_sems)
    rider.finish(r_in, r_out, r_sems)


def _mm(name, a, b, mode, M, N, K, outs, *, a_spec=None, b_spec=None, out_specs=None, epilogue=None,
        extras=(), rider=NO_RIDER, tm=1024, tn=1024, tk=4096):
    tm, tn, tk = min(tm, M), min(tn, N), min(tk, K)
    assert M % tm == 0 and N % tn == 0 and K % tk == 0, (name, M, N, K)
    nk = K // tk
    if a_spec is None:
        a_spec = {"nn": pl.BlockSpec((tm, tk), lambda i, j, k: (i, k)),
                  "nt": pl.BlockSpec((tm, tk), lambda i, j, k: (i, k)),
                  "tn": pl.BlockSpec((tk, tm), lambda i, j, k: (k, i))}[mode]
    else:
        a_spec = a_spec(tm, tn, tk)
    if b_spec is None:
        b_spec = {"nn": pl.BlockSpec((tk, tn), lambda i, j, k: (k, j)),
                  "nt": pl.BlockSpec((tn, tk), lambda i, j, k: (j, k)),
                  "tn": pl.BlockSpec((tk, tn), lambda i, j, k: (k, j))}[mode]
    else:
        b_spec = b_spec(tm, tn, tk)
    if out_specs is None:
        out_specs = [pl.BlockSpec((tm, tn), lambda i, j, k: (i, j)) for _ in outs]
    else:
        out_specs = [s(tm, tn, tk) for s in out_specs]
    dims = {"nn": DIMS_NN, "nt": DIMS_NT, "tn": DIMS_TN}[mode]
    ne, no = len(extras), len(outs)
    n_ri, n_ro = len(rider.operands), len(rider.out_shape)
    grid = (M // tm, N // tn, nk)

    def body(*refs):
        a_ref, b_ref = refs[0], refs[1]
        ex = refs[2:2 + ne]
        r_in = refs[2 + ne:2 + ne + n_ri]
        o = refs[2 + ne + n_ri:2 + ne + n_ri + no]
        r_out = refs[2 + ne + n_ri + no:2 + ne + n_ri + no + n_ro]
        scratch = refs[2 + ne + n_ri + no + n_ro:]
        r_sems = scratch[1:] if nk > 1 else scratch
        ii, jj, kk = pl.program_id(0), pl.program_id(1), pl.program_id(2)

        if rider.start is not None:
            @pl.when((ii == 0) & (jj == 0) & (kk == 0))
            def _():
                rider.start(r_in, r_out, r_sems)

        part = lax.dot_general(a_ref[...].astype(BF16), b_ref[...].astype(BF16), dims,
                               preferred_element_type=F32)

        def finish(total):
            vals = epilogue(total, *[e[...] for e in ex]) if epilogue is not None else (total,)
            for r, v in zip(o, vals):
                r[...] = v.astype(r.dtype)

        if nk == 1:
            finish(part)
        else:
            acc = scratch[0]

            @pl.when(kk == 0)
            def _():
                acc[...] = part

            @pl.when(kk > 0)
            def _():
                acc[...] += part

            @pl.when(kk == nk - 1)
            def _():
                finish(acc[...])

        if rider.finish is not None:
            steps = grid[0] * grid[1] * nk
            if rider.mid is not None and steps >= 4:
                @pl.when((ii * grid[1] + jj) * nk + kk == steps // 2)
                def _():
                    rider.mid(r_in, r_out, r_sems)

            @pl.when((ii == grid[0] - 1) & (jj == grid[1] - 1) & (kk == nk - 1))
            def _():
                if rider.mid is not None and steps < 4:
                    rider.mid(r_in, r_out, r_sems)
                rider.finish(r_in, r_out, r_sems)

    operands = [a, b] + [e[0] for e in extras] + list(rider.operands)
    in_specs = [a_spec, b_spec] + [e[1](tm, tn, tk) for e in extras] + [ANY] * n_ri
    hosted = rider.start is not None
    res = pl.pallas_call(
        body, name=name, grid=grid,
        in_specs=in_specs, out_specs=out_specs + [ANY] * n_ro, out_shape=list(outs) + list(rider.out_shape),
        scratch_shapes=([pltpu.VMEM((tm, tn), F32)] if nk > 1 else []) + list(rider.sems),
        input_output_aliases={2 + ne + i: no + r for i, r in rider.aliases.items()},
        compiler_params=_cparams(*(("arbitrary",) * 3 if hosted else ("parallel", "parallel", "arbitrary"))),
    )(*operands)
    return (res[:no], res[no:]) if hosted else res


def _sds(shape, dtype):
    return jax.ShapeDtypeStruct(tuple(shape), dtype)


def _rope(t, cos_p, sin_lo, sin_hi):
    return t * cos_p + pltpu.roll(t, LANES - QK_ROPE // 2, 1) * sin_lo + pltpu.roll(t, QK_ROPE // 2, 1) * sin_hi


def _rope_t(d, cos_p, sin_lo, sin_hi):
    return d * cos_p + pltpu.roll(d * sin_lo, QK_ROPE // 2, 1) + pltpu.roll(d * sin_hi, LANES - QK_ROPE // 2, 1)


def _vec_spec(d):
    return pl.BlockSpec((1, d), lambda i: (0, 0))


STRIP = 16


def _for_strips(ts, fn, unroll):
    def step(r, carry):
        fn(pl.ds(pl.multiple_of(r * STRIP, STRIP), STRIP))
        return carry

    lax.fori_loop(0, ts // STRIP, step, 0, unroll=unroll)


def _fwd_boundary(name, x_prev, y, gate, ng_post, ng_pre, sc, sh):
    S, D = x_prev.shape
    ts = min(256, S)
    has_y = y is not None
    row = pl.BlockSpec((ts, D), lambda i: (i, 0))

    def body(*refs):
        if has_y:
            x_ref, y_ref, g_ref, ngp_ref, ngn_ref, sc_ref, sh_ref, xo_ref, h_ref = refs
        else:
            x_ref, ngn_ref, sc_ref, sh_ref, h_ref = refs

        def strip(rows):
            if has_y:
                yv = y_ref[rows, :].astype(F32)
                xn = x_ref[rows, :] + g_ref[...] * (yv * _rms(yv) * ngp_ref[...])
                xo_ref[rows, :] = xn
            else:
                xn = x_ref[rows, :]
            hn = xn * _rms(xn) * ngn_ref[...]
            h_ref[rows, :] = (hn * (1.0 + sc_ref[...]) + sh_ref[...]).astype(BF16)

        _for_strips(ts, strip, unroll=4)

    vec = _vec_spec(D)
    if has_y:
        operands = (x_prev, y, gate, ng_post, ng_pre, sc, sh)
        in_specs = [row, row, vec, vec, vec, vec, vec]
        out_shape = [_sds((S, D), F32), _sds((S, D), BF16)]
        out_specs = [row, row]
    else:
        operands = (x_prev, ng_pre, sc, sh)
        in_specs = [row, vec, vec, vec]
        out_shape = [_sds((S, D), BF16)]
        out_specs = [row]
    return pl.pallas_call(body, name=name, grid=(S // ts,), in_specs=in_specs, out_specs=out_specs,
                          out_shape=out_shape, compiler_params=_cparams("parallel"))(*operands)


N_SUMS = 5


def _write_sums(acc_ref, sums_ref):
    sums_ref[...] = jnp.zeros_like(sums_ref)
    for r in range(N_SUMS):
        sums_ref[r:r + 1, :] = jnp.sum(acc_ref[r], axis=0, keepdims=True)


def _post_norm_bwd(dxt, yv, gate, ng_post, acc_ref):
    r1 = _rms(yv)
    yhat = yv * r1
    dn = dxt * gate
    u = dn * ng_post
    acc_ref[3] += dxt * (yhat * ng_post)
    acc_ref[4] += dn * yhat
    return r1 * (u - yhat * jnp.mean(u * yhat, axis=-1, keepdims=True))


def _loss_boundary(name, x_prev, y, gate, ng_post, target):
    S, D = x_prev.shape
    ts = min(256, S)
    row = pl.BlockSpec((ts, D), lambda i: (i, 0))
    vec = _vec_spec(D)

    def body(x_ref, y_ref, g_ref, ngp_ref, t_ref, dx_ref, dy_ref, sums_ref, loss_ref, acc_ref, lacc_ref):
        @pl.when(pl.program_id(0) == 0)
        def _():
            acc_ref[...] = jnp.zeros_like(acc_ref)
            lacc_ref[...] = jnp.zeros_like(lacc_ref)

        def strip(rows):
            yv = y_ref[rows, :].astype(F32)
            xf = x_ref[rows, :] + g_ref[...] * (yv * _rms(yv) * ngp_ref[...])
            err = xf - t_ref[rows, :]
            lacc_ref[...] += jnp.mean(err * err, axis=-1, keepdims=True)
            dxt = err / D
            dx_ref[rows, :] = dxt
            dy_ref[rows, :] = _post_norm_bwd(dxt, yv, g_ref[...], ngp_ref[...], acc_ref).astype(BF16)

        _for_strips(ts, strip, unroll=2)

        @pl.when(pl.program_id(0) == S // ts - 1)
        def _():
            _write_sums(acc_ref, sums_ref)
            loss_ref[...] = jnp.zeros_like(loss_ref) + 0.5 * jnp.sum(lacc_ref[...])

    return pl.pallas_call(
        body, name=name, grid=(S // ts,),
        in_specs=[row, row, vec, vec, row],
        out_specs=[row, row, pl.BlockSpec((8, D), lambda i: (0, 0)), pl.BlockSpec((8, LANES), lambda i: (0, 0))],
        out_shape=[_sds((S, D), F32), _sds((S, D), BF16), _sds((8, D), F32), _sds((8, LANES), F32)],
        scratch_shapes=[pltpu.VMEM((N_SUMS, STRIP, D), F32), pltpu.VMEM((STRIP, 1), F32)],
        compiler_params=_cparams("arbitrary"))(x_prev, y, gate, ng_post, target)


def _bwd_boundary(name, dx_new, dh, x_new, y, gate, ng_post, ng_pre, sc, rider=NO_RIDER):
    S, D = x_new.shape
    ts = min(256, S)
    has_y = y is not None
    row = pl.BlockSpec((ts, D), lambda i: (i, 0))
    vec = _vec_spec(D)
    n_in, n_out = (8, 3) if has_y else (5, 2)
    n_ri, n_ro = len(rider.operands), len(rider.out_shape)

    def body(*refs):
        r_in = refs[n_in:n_in + n_ri]
        r_out = refs[n_in + n_ri + n_out:n_in + n_ri + n_out + n_ro]
        acc_ref = refs[n_in + n_ri + n_out + n_ro]
        r_sems = refs[n_in + n_ri + n_out + n_ro + 1:]
        own = refs[:n_in] + refs[n_in + n_ri:n_in + n_ri + n_out]
        if has_y:
            dxn_ref, dh_ref, x_ref, y_ref, g_ref, ngp_ref, ngn_ref, sc_ref, dxo_ref, dy_ref, sums_ref = own
        else:
            dxn_ref, dh_ref, x_ref, ngn_ref, sc_ref, dxo_ref, sums_ref = own

        @pl.when(pl.program_id(0) == 0)
        def _():
            acc_ref[...] = jnp.zeros_like(acc_ref)
            if rider.start is not None:
                rider.start(r_in, r_out, r_sems)

        def strip(rows):
            xv = x_ref[rows, :]
            dhv = dh_ref[rows, :].astype(F32)
            ngn = ngn_ref[...]
            r2 = _rms(xv)
            xhat = xv * r2
            dn_pre = dhv * (1.0 + sc_ref[...])
            u2 = dn_pre * ngn
            dxt = dxn_ref[rows, :] + r2 * (u2 - xhat * jnp.mean(u2 * xhat, axis=-1, keepdims=True))
            dxo_ref[rows, :] = dxt
            acc_ref[0] += dhv
            acc_ref[1] += dhv * (xhat * ngn)
            acc_ref[2] += dn_pre * xhat
            if has_y:
                dy_ref[rows, :] = _post_norm_bwd(dxt, y_ref[rows, :].astype(F32), g_ref[...], ngp_ref[...],
                                                 acc_ref).astype(BF16)

        _for_strips(ts, strip, unroll=2)

        @pl.when(pl.program_id(0) == S // ts - 1)
        def _():
            _write_sums(acc_ref, sums_ref)
            if rider.finish is not None:
                _rider_end(rider, r_in, r_out, r_sems)

    sums_spec = pl.BlockSpec((8, D), lambda i: (0, 0))
    if has_y:
        operands = (dx_new, dh, x_new, y, gate, ng_post, ng_pre, sc)
        in_specs = [row, row, row, row, vec, vec, vec, vec]
        out_shape = [_sds((S, D), F32), _sds((S, D), BF16), _sds((8, D), F32)]
        out_specs = [row, row, sums_spec]
    else:
        operands = (dx_new, dh, x_new, ng_pre, sc)
        in_specs = [row, row, row, vec, vec]
        out_shape = [_sds((S, D), F32), _sds((8, D), F32)]
        out_specs = [row, sums_spec]
    res = pl.pallas_call(
        body, name=name, grid=(S // ts,), in_specs=in_specs + [ANY] * n_ri, out_specs=out_specs + [ANY] * n_ro,
        out_shape=out_shape + list(rider.out_shape),
        scratch_shapes=[pltpu.VMEM((N_SUMS, STRIP, D), F32)] + list(rider.sems),
        input_output_aliases={n_in + i: n_out + o for i, o in rider.aliases.items()},
        compiler_params=_cparams("arbitrary"))(*operands, *rider.operands)
    return (*res[:n_out], res[n_out:])


def _latent_fwd(lat, g_q, g_kv, rope_tabs, rank):
    S, W = lat.shape
    ts = min(256, S)
    tab = pl.BlockSpec((ts, LANES), lambda i: (i, 0))

    def body(lat_ref, gq_ref, gkv_ref, cos_ref, slo_ref, shi_ref, cq_ref, ckv_ref, kr_ref):
        lq = lat_ref[:, 0:rank]
        lkv = lat_ref[:, rank:2 * rank]
        cq_ref[...] = (lq * _rms(lq) * gq_ref[...]).astype(BF16)
        ckv_ref[...] = (lkv * _rms(lkv) * gkv_ref[...]).astype(BF16)
        kr_ref[...] = _rope(lat_ref[:, 2 * rank:W], cos_ref[...], slo_ref[...], shi_ref[...]).astype(BF16)

    return pl.pallas_call(
        body, name="mla_latent_fwd", grid=(S // ts,),
        in_specs=[pl.BlockSpec((ts, W), lambda i: (i, 0)), _vec_spec(rank), _vec_spec(rank), tab, tab, tab],
        out_specs=[pl.BlockSpec((ts, rank), lambda i: (i, 0)), pl.BlockSpec((ts, rank), lambda i: (i, 0)), tab],
        out_shape=[_sds((S, rank), BF16), _sds((S, rank), BF16), _sds((S, LANES), BF16)],
        compiler_params=_cparams("parallel"))(lat, g_q, g_kv, *rope_tabs)


def _latent_bwd(lat, dcq, dckv, dkr, g_q, g_kv, rope_tabs, rank):
    S, W = lat.shape
    ts = min(256, S)
    tab = pl.BlockSpec((ts, LANES), lambda i: (i, 0))
    half = pl.BlockSpec((ts, rank), lambda i: (i, 0))

    def body(lat_ref, dcq_ref, dckv_ref, dkr_ref, gq_ref, gkv_ref, cos_ref, slo_ref, shi_ref, dlat_ref, sums_ref):
        @pl.when(pl.program_id(0) == 0)
        def _():
            sums_ref[...] = jnp.zeros_like(sums_ref)

        def norm_bwd(v, dn, g, r):
            rr = _rms(v)
            vhat = v * rr
            u = dn * g
            sums_ref[r:r + 1, :] += jnp.sum(dn * vhat, axis=0, keepdims=True)
            return rr * (u - vhat * jnp.mean(u * vhat, axis=-1, keepdims=True))

        dlat_ref[:, 0:rank] = norm_bwd(lat_ref[:, 0:rank], dcq_ref[...], gq_ref[...], 0).astype(BF16)
        dlat_ref[:, rank:2 * rank] = norm_bwd(lat_ref[:, rank:2 * rank], dckv_ref[...], gkv_ref[...], 1).astype(BF16)
        dlat_ref[:, 2 * rank:W] = _rope_t(dkr_ref[...], cos_ref[...], slo_ref[...], shi_ref[...]).astype(BF16)

    return pl.pallas_call(
        body, name="mla_latent_bwd", grid=(S // ts,),
        in_specs=[pl.BlockSpec((ts, W), lambda i: (i, 0)), half, half, tab, _vec_spec(rank), _vec_spec(rank),
                  tab, tab, tab],
        out_specs=[pl.BlockSpec((ts, W), lambda i: (i, 0)), pl.BlockSpec((8, rank), lambda i: (0, 0))],
        out_shape=[_sds((S, W), BF16), _sds((8, rank), F32)],
        compiler_params=_cparams("arbitrary"))(lat, dcq, dckv, dkr, g_q, g_kv, *rope_tabs)


def _attn_tiles(S):
    t = min(512, S)
    return t, S // t


def _causal_mask(t):
    return lax.broadcasted_iota(jnp.int32, (t, t), 1) <= lax.broadcasted_iota(jnp.int32, (t, t), 0)


def _causal_pairs(nb, q_major):
    if q_major:
        pairs = [(qi, ki) for qi in range(nb) for ki in range(qi + 1)]
    else:
        pairs = [(qi, ki) for ki in range(nb) for qi in range(ki, nb)]
    return jnp.array([p[0] for p in pairs], jnp.int32), jnp.array([p[1] for p in pairs], jnp.int32), len(pairs)


def _heads_per_step(heads):
    return 2 if heads % 2 == 0 else 1


def _attn_fwd_tri(q, kv, kr, heads, scale, rider=NO_RIDER):
    S = q.shape[0]
    t, nb = _attn_tiles(S)
    G = _heads_per_step(heads)
    q_tab, k_tab, n_pairs = _causal_pairs(nb, True)
    n_ri, n_ro = len(rider.operands), len(rider.out_shape)

    def body(qt_ref, kt_ref, *refs):
        q_ref, kv_ref, kr_ref = refs[:3]
        r_in = refs[3:3 + n_ri]
        o_ref, lse_ref = refs[3 + n_ri:5 + n_ri]
        r_out = refs[5 + n_ri:5 + n_ri + n_ro]
        m_scr, acc_scr = refs[5 + n_ri + n_ro:7 + n_ri + n_ro]
        r_sems = refs[7 + n_ri + n_ro:]
        h, p = pl.program_id(0), pl.program_id(1)
        qi, ki = qt_ref[p], kt_ref[p]

        if rider.start is not None:
            @pl.when((h == 0) & (p == 0))
            def _():
                rider.start(r_in, r_out, r_sems)

        @pl.when(ki == 0)
        def _():
            m_scr[...] = jnp.full_like(m_scr, NEG_INF)
            acc_scr[...] = jnp.zeros_like(acc_scr)

        def step(diagonal):
            ones = jnp.ones((t, LANES), BF16)
            for g in range(G):
                kcat = jnp.concatenate([kv_ref[:, g * QK_PAD:g * QK_PAD + QK_NOPE], kr_ref[...]], axis=1)
                vext = jnp.concatenate([kv_ref[:, g * QK_PAD + QK_NOPE:(g + 1) * QK_PAD], ones], axis=1)
                s = lax.dot_general(q_ref[:, g * QK_PAD:(g + 1) * QK_PAD], kcat, DIMS_NT,
                                    preferred_element_type=F32) * scale
                if diagonal:
                    s = jnp.where(_causal_mask(t), s, NEG_INF)
                m_prev = m_scr[g]
                m_new = jnp.maximum(m_prev, jnp.max(s, axis=-1, keepdims=True))
                alpha = jnp.exp(m_prev - m_new)
                pr = jnp.exp(s - jnp.tile(m_new, (1, t // LANES)))
                acc_scr[g] = jnp.tile(alpha, (1, 2)) * acc_scr[g] + lax.dot_general(
                    pr.astype(BF16), vext, DIMS_NN, preferred_element_type=F32)
                m_scr[g] = m_new

        @pl.when(ki < qi)
        def _():
            step(False)

        @pl.when(ki == qi)
        def _():
            step(True)
            for g in range(G):
                acc = acc_scr[g]
                o_ref[:, g * V_HEAD:(g + 1) * V_HEAD] = (acc[:, 0:V_HEAD] / acc[:, V_HEAD:2 * V_HEAD]).astype(BF16)
                lse_ref[g] = m_scr[g] + jnp.log(acc[:, V_HEAD:2 * V_HEAD])

        if rider.finish is not None:
            halfway = rider.mid is not None and heads // G >= 2
            if halfway:
                @pl.when((h == heads // G // 2) & (p == 0))
                def _():
                    rider.mid(r_in, r_out, r_sems)

            @pl.when((h == heads // G - 1) & (p == n_pairs - 1))
            def _():
                if halfway:
                    rider.finish(r_in, r_out, r_sems)
                else:
                    _rider_end(rider, r_in, r_out, r_sems)

    res = pl.pallas_call(
        body, name="mla_attn_fwd",
        grid_spec=pltpu.PrefetchScalarGridSpec(
            num_scalar_prefetch=2, grid=(heads // G, n_pairs),
            in_specs=[pl.BlockSpec((t, G * QK_PAD), lambda h, p, qt, kt: (qt[p], h)),
                      pl.BlockSpec((t, G * QK_PAD), lambda h, p, qt, kt: (kt[p], h)),
                      pl.BlockSpec((t, LANES), lambda h, p, qt, kt: (kt[p], 0))] + [ANY] * n_ri,
            out_specs=[pl.BlockSpec((t, G * V_HEAD), lambda h, p, qt, kt: (qt[p], h)),
                       pl.BlockSpec((G, t, LANES), lambda h, p, qt, kt: (h, qt[p], 0))] + [ANY] * n_ro,
            scratch_shapes=[pltpu.VMEM((G, t, LANES), F32), pltpu.VMEM((G, t, 2 * V_HEAD), F32)] + list(rider.sems)),
        out_shape=[_sds((S, heads * V_HEAD), BF16), _sds((heads, S, LANES), F32)] + list(rider.out_shape),
        input_output_aliases={5 + i: 2 + o for i, o in rider.aliases.items()},
        compiler_params=_cparams("arbitrary", "arbitrary"))(q_tab, k_tab, q, kv, kr, *rider.operands)
    return res[0], res[1], res[2:]


def _attn_bwd_tri(q, kv, kr, o, do, lse, rope_tabs, heads, scale, rider=NO_RIDER):
    S = q.shape[0]
    t, nb = _attn_tiles(S)
    G = _heads_per_step(heads)
    q_tab, k_tab, n_pairs = _causal_pairs(nb, False)
    n_ri, n_ro = len(rider.operands), len(rider.out_shape)
    rep = t // LANES

    tabs = jnp.concatenate(rope_tabs, axis=1)

    def body(qt_ref, kt_ref, *refs):
        q_ref, kv_ref, kr_ref, o_ref, do_ref, lse_ref, tabs_ref = refs[:7]
        cos_ref, slo_ref, shi_ref = (tabs_ref.at[:, pl.ds(i * LANES, LANES)] for i in range(3))
        r_in = refs[7:7 + n_ri]
        dq_ref, dkv_ref, dkr_ref = refs[7 + n_ri:10 + n_ri]
        r_out = refs[10 + n_ri:10 + n_ri + n_ro]
        dq_scr, dk_scr, dv_scr, dkr_scr, delta_scr = refs[10 + n_ri + n_ro:15 + n_ri + n_ro]
        r_sems = refs[15 + n_ri + n_ro:]
        h, p = pl.program_id(0), pl.program_id(1)
        qi, ki = qt_ref[p], kt_ref[p]
        q_rows = pl.ds(pl.multiple_of(qi * t, t), t)
        k_rows = pl.ds(pl.multiple_of(ki * t, t), t)

        @pl.when(ki == 0)
        def _():
            for g in range(G):
                cols = slice(g * V_HEAD, (g + 1) * V_HEAD)
                d = jnp.sum(do_ref[:, cols].astype(F32) * o_ref[:, cols].astype(F32), axis=-1, keepdims=True)
                delta_scr[g, q_rows, :] = jnp.broadcast_to(d, (t, LANES))

        if rider.start is not None:
            @pl.when((h == 0) & (p == 0))
            def _():
                rider.start(r_in, r_out, r_sems)

        @pl.when(p == 0)
        def _():
            dq_scr[...] = jnp.zeros_like(dq_scr)

        @pl.when((h == 0) & (p == 0))
        def _():
            dkr_scr[...] = jnp.zeros_like(dkr_scr)

        @pl.when(qi == ki)
        def _():
            dk_scr[...] = jnp.zeros_like(dk_scr)
            dv_scr[...] = jnp.zeros_like(dv_scr)

        def step(diagonal):
            for g in range(G):
                qv = q_ref[:, g * QK_PAD:(g + 1) * QK_PAD]
                kcat = jnp.concatenate([kv_ref[:, g * QK_PAD:g * QK_PAD + QK_NOPE], kr_ref[...]], axis=1)
                s = lax.dot_general(qv, kcat, DIMS_NT, preferred_element_type=F32) * scale
                pr = jnp.exp(s - jnp.tile(lse_ref[g], (1, rep)))
                if diagonal:
                    pr = jnp.where(_causal_mask(t), pr, 0.0)
                dov = do_ref[:, g * V_HEAD:(g + 1) * V_HEAD]
                dv_scr[g] += lax.dot_general(pr.astype(BF16), dov, DIMS_TN, preferred_element_type=F32)
                dp = lax.dot_general(dov, kv_ref[:, g * QK_PAD + QK_NOPE:(g + 1) * QK_PAD], DIMS_NT,
                                     preferred_element_type=F32)
                ds = (pr * (dp - jnp.tile(delta_scr[g, q_rows, :], (1, rep))) * scale).astype(BF16)
                dk_scr[g] += lax.dot_general(ds, qv, DIMS_TN, preferred_element_type=F32)
                dq_scr[q_rows, g * QK_PAD:(g + 1) * QK_PAD] += lax.dot_general(ds, kcat, DIMS_NN,
                                                                               preferred_element_type=F32)

        @pl.when(qi > ki)
        def _():
            step(False)

        @pl.when(qi == ki)
        def _():
            step(True)
            for g in range(G):
                dqv = dq_scr[q_rows, g * QK_PAD:(g + 1) * QK_PAD]
                dq_ref[q_rows, g * QK_PAD:(g + 1) * QK_PAD] = jnp.concatenate(
                    [dqv[:, 0:QK_NOPE], _rope_t(dqv[:, QK_NOPE:QK_PAD], cos_ref[...], slo_ref[...], shi_ref[...])],
                    axis=1).astype(BF16)

        @pl.when(qi == nb - 1)
        def _():
            for g in range(G):
                dkv_ref[:, g * QK_PAD:(g + 1) * QK_PAD] = jnp.concatenate(
                    [dk_scr[g][:, 0:QK_NOPE], dv_scr[g]], axis=1).astype(BF16)
                dkr_scr[k_rows, :] += dk_scr[g][:, QK_NOPE:QK_PAD]

        @pl.when((h == heads // G - 1) & (p == n_pairs - 1))
        def _():
            dkr_ref[...] = dkr_scr[...]
            if rider.finish is not None:
                _rider_end(rider, r_in, r_out, r_sems)

    q_blk = lambda w: pl.BlockSpec((t, G * w), lambda h, p, qt, kt: (qt[p], h))
    stat = pl.BlockSpec((G, t, LANES), lambda h, p, qt, kt: (h, qt[p], 0))
    tab = pl.BlockSpec((t, LANES), lambda h, p, qt, kt: (kt[p], 0))
    res = pl.pallas_call(
        body, name="mla_attn_bwd",
        grid_spec=pltpu.PrefetchScalarGridSpec(
            num_scalar_prefetch=2, grid=(heads // G, n_pairs),
            in_specs=[q_blk(QK_PAD),
                      pl.BlockSpec((t, G * QK_PAD), lambda h, p, qt, kt: (kt[p], h)),
                      tab, q_blk(V_HEAD), q_blk(V_HEAD), stat,
                      pl.BlockSpec((t, 3 * LANES), lambda h, p, qt, kt: (kt[p], 0))] + [ANY] * n_ri,
            out_specs=[pl.BlockSpec((S, G * QK_PAD), lambda h, p, qt, kt: (0, h)),
                       pl.BlockSpec((t, G * QK_PAD), lambda h, p, qt, kt: (kt[p], h)),
                       pl.BlockSpec((S, LANES), lambda h, p, qt, kt: (0, 0))] + [ANY] * n_ro,
            scratch_shapes=[pltpu.VMEM((S, G * QK_PAD), F32), pltpu.VMEM((G, t, QK_PAD), F32),
                            pltpu.VMEM((G, t, V_HEAD), F32), pltpu.VMEM((S, LANES), F32),
                            pltpu.VMEM((G, S, LANES), F32)] + list(rider.sems)),
        out_shape=[_sds((S, heads * QK_PAD), BF16), _sds((S, heads * QK_PAD), BF16), _sds((S, LANES), F32)]
        + list(rider.out_shape),
        input_output_aliases={9 + i: 3 + o for i, o in rider.aliases.items()},
        compiler_params=_cparams("arbitrary", "arbitrary"))(q_tab, k_tab, q, kv, kr, o, do, lse, tabs,
                                                            *rider.operands)
    return res[0], res[1], res[2], res[3:]


def _shift_down(z, n, rows):
    return jnp.where(rows >= n, pltpu.roll(z, n, 0), 0.0)


def _shift_up(z, n, rows, S):
    return jnp.where(rows < S - n, pltpu.roll(z, S - n, 0), 0.0)


def _conv_specs(S, tc):
    strip = lambda p: pl.BlockSpec((None, S, tc), lambda j: (p, 0, j))
    return strip(0), strip(1), strip(2), pl.BlockSpec((3, tc), lambda j: (0, j))


def _conv_fwd(proj3, w):
    _, S, D = proj3.shape
    tc = LANES

    def body(b_ref, c_ref, u_ref, w_ref, out_ref):
        z = c_ref[...].astype(F32) * u_ref[...].astype(F32)
        rows = lax.broadcasted_iota(jnp.int32, (S, tc), 0)
        zc = w_ref[0:1, :] * _shift_down(z, 2, rows) + w_ref[1:2, :] * _shift_down(z, 1, rows) + w_ref[2:3, :] * z
        out_ref[...] = (b_ref[...].astype(F32) * zc).astype(BF16)

    return pl.pallas_call(
        body, name="conv_fwd", grid=(D // tc,), in_specs=list(_conv_specs(S, tc)),
        out_specs=pl.BlockSpec((S, tc), lambda j: (0, j)), out_shape=_sds((S, D), BF16),
        compiler_params=_cparams("parallel"))(proj3, proj3, proj3, w)


def _conv_bwd(dbz, proj3, w):
    _, S, D = proj3.shape
    tc = LANES

    def body(d_ref, b_ref, c_ref, u_ref, w_ref, dp_ref, dw_ref):
        cv, uv, dv = c_ref[...].astype(F32), u_ref[...].astype(F32), d_ref[...].astype(F32)
        z = cv * uv
        rows = lax.broadcasted_iota(jnp.int32, (S, tc), 0)
        z1, z2 = _shift_down(z, 1, rows), _shift_down(z, 2, rows)
        zc = w_ref[0:1, :] * z2 + w_ref[1:2, :] * z1 + w_ref[2:3, :] * z
        dp_ref[0] = (dv * zc).astype(BF16)
        dzc = dv * b_ref[...].astype(F32)
        dz = w_ref[2:3, :] * dzc + w_ref[1:2, :] * _shift_up(dzc, 1, rows, S) + w_ref[0:1, :] * _shift_up(dzc, 2, rows, S)
        dp_ref[1] = (dz * uv).astype(BF16)
        dp_ref[2] = (dz * cv).astype(BF16)
        dw_ref[0:1, :] = jnp.sum(dzc * z2, axis=0, keepdims=True)
        dw_ref[1:2, :] = jnp.sum(dzc * z1, axis=0, keepdims=True)
        dw_ref[2:3, :] = jnp.sum(dzc * z, axis=0, keepdims=True)

    sb, sc_, su, sw = _conv_specs(S, tc)
    return pl.pallas_call(
        body, name="conv_bwd", grid=(D // tc,),
        in_specs=[pl.BlockSpec((S, tc), lambda j: (0, j)), sb, sc_, su, sw],
        out_specs=[pl.BlockSpec((3, S, tc), lambda j: (0, 0, j)), pl.BlockSpec((3, tc), lambda j: (0, j))],
        out_shape=[_sds((3, S, D), BF16), _sds((3, D), F32)],
        compiler_params=_cparams("parallel"))(dbz, proj3, proj3, proj3, w)


def _silu(c_all):
    def body(c_ref, o_ref):
        cv = c_ref[...]
        o_ref[...] = cv * (1.0 / (1.0 + jnp.exp(-cv)))

    vm = pl.BlockSpec(memory_space=pltpu.VMEM)
    return pl.pallas_call(body, name="cond_silu", in_specs=[vm], out_specs=vm, out_shape=_sds(c_all.shape, F32))(c_all)


def _mod_fwd(cond, w_mod, b_cols):
    L, D, ncol = w_mod.shape
    B = cond.shape[0]
    tk, tn = min(512, D), min(1024, ncol)
    nk = D // tk

    def body(c_ref, w_ref, b_ref, out_ref, acc):
        kk = pl.program_id(2)
        part = lax.dot_general(c_ref[...].astype(BF16), w_ref[...].astype(BF16), DIMS_NN, preferred_element_type=F32)

        @pl.when(kk == 0)
        def _():
            acc[...] = part

        @pl.when(kk > 0)
        def _():
            acc[...] += part

        @pl.when(kk == nk - 1)
        def _():
            out_ref[...] = acc[...] + b_ref[...]

    return pl.pallas_call(
        body, name="mod_fwd", grid=(L, ncol // tn, nk),
        in_specs=[pl.BlockSpec((B, tk), lambda l, j, k: (0, k)),
                  pl.BlockSpec((None, tk, tn), lambda l, j, k: (l, k, j)),
                  pl.BlockSpec((None, 1, tn), lambda l, j, k: (l, 0, j))],
        out_specs=pl.BlockSpec((None, B, tn), lambda l, j, k: (l, 0, j)),
        out_shape=_sds((L, B, ncol), F32),
        scratch_shapes=[pltpu.VMEM((B, tn), F32)],
        compiler_params=_cparams("parallel", "parallel", "arbitrary"))(cond, w_mod, b_cols)


def _adamw_math(w, g, m, v):
    m = ADAM_B1 * m + (1.0 - ADAM_B1) * g
    v = ADAM_B2 * v + (1.0 - ADAM_B2) * (g * g)
    m_hat = m / (1.0 - ADAM_B1 ** ADAM_STEP)
    v_hat = v / (1.0 - ADAM_B2 ** ADAM_STEP)
    delta = -ADAM_LR * (m_hat / (jnp.sqrt(v_hat) + ADAM_EPS) + ADAM_WD * w)
    return delta, m, v


def _adamw(name, w, g, m, v):
    shape = w.shape
    cols = shape[-1] if w.ndim <= 3 else shape[-2] * shape[-1]
    rows = w.size // cols
    w2, g2, m2, v2 = (t.reshape(rows, cols) for t in (w, g, m, v))
    tr = _row_tile(rows, cols * 4, limit=2 * 1024 * 1024, mult=8)
    spec = pl.BlockSpec((tr, cols), lambda i: (i, 0))

    def body(w_ref, g_ref, m_ref, v_ref, d_ref, nm_ref, nv_ref):
        d, nm, nv = _adamw_math(w_ref[...], g_ref[...], m_ref[...], v_ref[...])
        d_ref[...] = d
        nm_ref[...] = nm
        nv_ref[...] = nv

    outs = pl.pallas_call(body, name=name, grid=(rows // tr,), in_specs=[spec] * 4, out_specs=[spec] * 3,
                          out_shape=[_sds((rows, cols), F32)] * 3, compiler_params=_cparams("parallel"))(w2, g2, m2, v2)
    return tuple(t.reshape(shape) for t in outs)


def _adamw_mod(w, cond_t, dmod_cols, m, v, rider=NO_RIDER):
    L, D, ncol = w.shape
    B = cond_t.shape[1]
    tr, tc = min(256, D), min(1024, ncol)
    blk = pl.BlockSpec((None, tr, tc), lambda l, i, j: (l, i, j))
    grid = (L, D // tr, ncol // tc)
    n_ri, n_ro = len(rider.operands), len(rider.out_shape)

    def body(*refs):
        w_ref, ct_ref, dm_ref, m_ref, v_ref = refs[:5]
        r_in = refs[5:5 + n_ri]
        g_ref, d_ref, nm_ref, nv_ref = refs[5 + n_ri:9 + n_ri]
        r_out = refs[9 + n_ri:9 + n_ri + n_ro]
        r_sems = refs[9 + n_ri + n_ro:]
        ids = [pl.program_id(a) for a in range(3)]

        if rider.start is not None:
            @pl.when((ids[0] == 0) & (ids[1] == 0) & (ids[2] == 0))
            def _():
                rider.start(r_in, r_out, r_sems)

        g = lax.dot_general(ct_ref[...], dm_ref[...], DIMS_NN, precision=lax.Precision.HIGHEST,
                            preferred_element_type=F32)
        d, nm, nv = _adamw_math(w_ref[...], g, m_ref[...], v_ref[...])
        g_ref[...] = g
        d_ref[...] = d
        nm_ref[...] = nm
        nv_ref[...] = nv

        if rider.finish is not None:
            @pl.when((ids[0] == grid[0] - 1) & (ids[1] == grid[1] - 1) & (ids[2] == grid[2] - 1))
            def _():
                _rider_end(rider, r_in, r_out, r_sems)

    hosted = rider.start is not None
    res = pl.pallas_call(
        body, name="adamw_w_mod", grid=grid,
        in_specs=[blk, pl.BlockSpec((tr, B), lambda l, i, j: (i, 0)),
                  pl.BlockSpec((None, B, tc), lambda l, i, j: (l, 0, j)), blk, blk] + [ANY] * n_ri,
        out_specs=[blk] * 4 + [ANY] * n_ro, out_shape=[_sds((L, D, ncol), F32)] * 4 + list(rider.out_shape),
        scratch_shapes=list(rider.sems), input_output_aliases={5 + i: 4 + o for i, o in rider.aliases.items()},
        compiler_params=_cparams(*(("arbitrary",) * 3 if hosted else ("parallel",) * 3)))(
            w, cond_t, dmod_cols, m, v, *rider.operands)
    return (*res[:4], res[4:])


def _cast_into_full(name, ws, kinds, k_idx, rider=NO_RIDER):
    L, R, C = ws[0].shape
    assert all(w.shape == (L, R, C) for w in ws)
    n = len(ws)
    Rh = R // 2
    tr = _row_tile(Rh, C * 4)
    grid = (L, 2, Rh // tr)
    out_shape, out_specs = [], []
    for kind in kinds:
        if kind == "row":
            out_shape.append(_sds((L, N_CHIPS, 2, Rh, C), BF16))
            out_specs.append(pl.BlockSpec((None, None, None, tr, C), lambda l, h, i, k_ref: (l, k_ref[0], h, i, 0)))
        else:
            out_shape.append(_sds((L, 2, Rh, N_CHIPS * C), BF16))
            out_specs.append(pl.BlockSpec((None, None, tr, C), lambda l, h, i, k_ref: (l, h, i, k_ref[0])))
    n_ri, n_ro = len(rider.operands), len(rider.out_shape)

    def body(k_ref, *refs):
        r_in = refs[n:n + n_ri]
        r_out = refs[2 * n + n_ri:2 * n + n_ri + n_ro]
        r_sems = refs[2 * n + n_ri + n_ro:]
        ids = [pl.program_id(a) for a in range(3)]
        if rider.start is not None:
            @pl.when((ids[0] == 0) & (ids[1] == 0) & (ids[2] == 0))
            def _():
                rider.start(r_in, r_out, r_sems)
        for a in range(n):
            refs[n + n_ri + a][...] = refs[a][...].astype(BF16)
        if rider.finish is not None:
            @pl.when((ids[0] == grid[0] - 1) & (ids[1] == grid[1] - 1) & (ids[2] == grid[2] - 1))
            def _():
                _rider_end(rider, r_in, r_out, r_sems)

    hosted = rider.start is not None
    res = pl.pallas_call(
        body, name=name,
        grid_spec=pltpu.PrefetchScalarGridSpec(
            num_scalar_prefetch=1, grid=grid,
            in_specs=[pl.BlockSpec((None, None, tr, C), lambda l, h, i, k_ref: (l, h, i, 0))] * n + [ANY] * n_ri,
            out_specs=out_specs + [ANY] * n_ro, scratch_shapes=list(rider.sems)),
        out_shape=out_shape + list(rider.out_shape),
        input_output_aliases={1 + n + i: n + o for i, o in rider.aliases.items()},
        compiler_params=_cparams(*(("arbitrary",) * 3 if hosted else ("parallel",) * 3)))(
            k_idx, *[w.reshape(L, 2, Rh, C) for w in ws], *rider.operands)
    return res[:n], res[n:]


def _pair_sum(name, g5, ra, c_idx):
    L, A, _, Rh, Cc = g5.shape
    tr = _row_tile(Rh, Cc * 4)

    def body(c_ref, g_ref, r_ref, o_ref):
        o_ref[...] = (g_ref[...].astype(F32) + r_ref[...].astype(F32)).astype(BF16)

    blk = pl.BlockSpec((None, None, tr, Cc), lambda l, a, i, c_ref: (l, a, i, 0))
    return pl.pallas_call(
        body, name=name,
        grid_spec=pltpu.PrefetchScalarGridSpec(
            num_scalar_prefetch=1, grid=(L, A, Rh // tr),
            in_specs=[pl.BlockSpec((None, None, None, tr, Cc), lambda l, a, i, c_ref: (l, a, c_ref[0], i, 0)), blk],
            out_specs=blk),
        out_shape=_sds((L, A, Rh, Cc), BF16),
        compiler_params=_cparams("parallel", "parallel", "parallel"))(c_idx, g5, ra)


def _chip_sum(name, p, rb, kc_idx, kind, layer=0, n_layers=1, prev=None):
    _, A, Rh, Cc = p.shape
    C = rb.shape[-1]
    tr = _row_tile(Rh, C * 4)
    if kind == "row":
        own = pl.BlockSpec((None, None, tr, C), lambda i, kc: (0, kc[0], i, 0))
    else:
        own = pl.BlockSpec((None, None, tr, C), lambda i, kc: (0, 0, i, kc[0]))
    peer = lambda j: pl.BlockSpec((None, None, tr, C), lambda i, kc: (j, 0, i, 0))

    def body(kc_ref, p_ref, r0_ref, r1_ref, r2_ref, *rest):
        o_ref = rest[-1]
        o_ref[...] = ((p_ref[...].astype(F32) + r0_ref[...].astype(F32)) + r1_ref[...].astype(F32)) + r2_ref[...].astype(F32)

    operands = [kc_idx, p, rb, rb, rb] + ([prev] if prev is not None else [])
    return pl.pallas_call(
        body, name=name,
        grid_spec=pltpu.PrefetchScalarGridSpec(
            num_scalar_prefetch=1, grid=(Rh // tr,),
            in_specs=[own, peer(0), peer(1), peer(2)] + ([ANY] if prev is not None else []),
            out_specs=pl.BlockSpec((None, None, tr, C), lambda i, kc: (layer, kc[1], i, 0))),
        out_shape=_sds((n_layers, 2, Rh, C), F32),
        input_output_aliases={5: 0} if prev is not None else {},
        compiler_params=_cparams("parallel"))(*operands)


def _mesh_place():
    x, y, c = lax.axis_index("x"), lax.axis_index("y"), lax.axis_index("c")
    chips = [(1 - x, y), (x, 1 - y), (1 - x, 1 - y)]
    return x, y, c, chips


def _remote(src, dst, send_sem, recv_sem, to):
    return pltpu.make_async_remote_copy(src_ref=src, dst_ref=dst, send_sem=send_sem, recv_sem=recv_sem,
                                        device_id=to, device_id_type=MESH_ID)


def _small_allgather(name, v, with_sum=False, rider=NO_RIDER):
    R, N = v.shape
    n_ri, n_ro, n_own = len(rider.operands), len(rider.out_shape), 2 if with_sum else 1

    def body(*refs):
        r_in = refs[1:1 + n_ri]
        r_out = refs[1 + n_ri + n_own:1 + n_ri + n_own + n_ro]
        r_sems = refs[1 + n_ri + n_own + n_ro + 3:]
        own = (refs[0],) + refs[1 + n_ri:1 + n_ri + n_own] + refs[1 + n_ri + n_own + n_ro:1 + n_ri + n_own + n_ro + 3]
        if with_sum:
            x_ref, out_ref, sum_ref, send_sems, recv_sems, local_sem = own
        else:
            x_ref, out_ref, send_sems, recv_sems, local_sem = own
        if rider.start is not None:
            rider.start(r_in, r_out, r_sems)
        x, y, c, chips = _mesh_place()
        me, sibling = (x, y, c), (x, y, 1 - c)

        def rows(px, py, pc):
            return out_ref.at[pl.ds((4 * px + 2 * py + pc) * R, R), :]

        def copy(k, block, to, src=None):
            return _remote(rows(*block) if src is None else src, rows(*block), send_sems.at[k], recv_sems.at[k], to)

        mine = pltpu.make_async_copy(x_ref, rows(*me), local_sem)
        mine.start()
        first = [copy(0, me, sibling, src=x_ref)]
        first += [copy(1 + j, me, (*chip, c), src=x_ref) for j, chip in enumerate(chips)]
        for cp in first:
            cp.start()
        passed = [copy(4 + j, (*chip, c), sibling) for j, chip in enumerate(chips)]
        for j, chip in enumerate(chips):
            copy(1 + j, (*chip, c), me).wait_recv()
            passed[j].start()
        copy(0, sibling, me).wait_recv()
        for j, chip in enumerate(chips):
            copy(4 + j, (*chip, 1 - c), me).wait_recv()
        for cp in first + passed:
            cp.wait_send()
        mine.wait()
        if with_sum:
            total = out_ref[0:R, :]
            for p in range(1, 8):
                total = total + out_ref[p * R:(p + 1) * R, :]
            sum_ref[...] = total
        if rider.finish is not None:
            _rider_end(rider, r_in, r_out, r_sems)

    vm = pl.BlockSpec(memory_space=pltpu.VMEM)
    out_shape = [_sds((8 * R, N), F32)] + ([_sds((R, N), F32)] if with_sum else [])
    res = pl.pallas_call(
        body, name=name, out_shape=out_shape + list(rider.out_shape), in_specs=[vm] + [ANY] * n_ri,
        out_specs=[vm] * n_own + [ANY] * n_ro,
        scratch_shapes=[pltpu.SemaphoreType.DMA((7,)), pltpu.SemaphoreType.DMA((7,)), pltpu.SemaphoreType.DMA]
        + list(rider.sems),
        input_output_aliases={1 + i: n_own + o for i, o in rider.aliases.items()},
        compiler_params=pltpu.CompilerParams(vmem_limit_bytes=VMEM_LIMIT_BYTES))(v, *rider.operands)
    if rider.start is not None:
        return (*res[:n_own], res[n_own:])
    return res if with_sum else res[0]


def _full_place(ref, kind, C, kk, half, layer=None):
    lead = slice(None) if layer is None else pl.ds(layer, 1)
    if kind == "row":
        return ref.at[lead, kk, half]
    return ref.at[lead, half, :, pl.ds(pl.multiple_of(kk * C, LANES), C)]


def _gather_rider(fulls, kinds, shard_cols, layers=None, peers=(0, 1, 2)):
    n = len(fulls)
    layers = layers or [None] * n
    rows = [f.shape[3] if kind == "row" else f.shape[2] for f, kind in zip(fulls, kinds)]
    n_chunks = 2 if all(r % 32 == 0 for r in rows) else 1

    def copies(outs, sems):
        x, y, c, chips = _mesh_place()
        k = 2 * x + y

        def place(a, kk, half, ch):
            rc = rows[a] // n_chunks
            return _full_place(outs[a], kinds[a], shard_cols[a], kk, half, layers[a]).at[:, pl.ds(ch * rc, rc), :]

        def copy(a, j, ch, ref, to):
            s = 6 * (n_chunks * a + ch) + j
            return _remote(ref, ref, sems[0].at[s], sems[1].at[s], to)

        return (x, y, c), [(j, chip) for j, chip in enumerate(chips) if j in peers], k, place, copy

    def start(_, outs, sems):
        (x, y, c), chips, k, place, copy = copies(outs, sems)
        for ch in range(n_chunks):
            for j, chip in chips:
                for a in range(n):
                    copy(a, j, ch, place(a, k, c, ch), (*chip, c)).start()

    def pass_on(outs, sems, ch):
        (x, y, c), chips, k, place, copy = copies(outs, sems)
        for j, chip in chips:
            kj = 2 * chip[0] + chip[1]
            for a in range(n):
                copy(a, j, ch, place(a, kj, c, ch), (x, y, c)).wait_recv()
                copy(a, 3 + j, ch, place(a, kj, c, ch), (x, y, 1 - c)).start()

    def mid(_, outs, sems):
        pass_on(outs, sems, 0)

    def finish(_, outs, sems):
        pass_on(outs, sems, n_chunks - 1)
        (x, y, c), chips, k, place, copy = copies(outs, sems)
        for ch in range(n_chunks):
            for j, chip in chips:
                kj = 2 * chip[0] + chip[1]
                for a in range(n):
                    copy(a, 3 + j, ch, place(a, kj, 1 - c, ch), (x, y, c)).wait_recv()
        for ch in range(n_chunks):
            for j, chip in chips:
                kj = 2 * chip[0] + chip[1]
                for a in range(n):
                    copy(a, j, ch, place(a, k, c, ch), (*chip, c)).wait_send()
                    copy(a, 3 + j, ch, place(a, kj, c, ch), (x, y, 1 - c)).wait_send()

    n_sems = 6 * n * n_chunks
    return Rider(tuple(fulls), tuple(_sds(f.shape, BF16) for f in fulls), {a: a for a in range(n)},
                 (pltpu.SemaphoreType.DMA((n_sems,)), pltpu.SemaphoreType.DMA((n_sems,))), start, finish,
                 mid if n_chunks == 2 else None)


def _scatter_rider(ps, kinds, shard_cols, peers=(0, 1, 2), into=None):
    n = len(ps)

    def copies(ins, outs, sems):
        x, y, c, chips = _mesh_place()
        cps = []
        for j, chip in enumerate(chips):
            if j not in peers:
                continue
            kj = 2 * chip[0] + chip[1]
            for a in range(n):
                C = shard_cols[a]
                src = ins[a].at[:, kj] if kinds[a] == "row" else ins[a].at[:, 0, :, pl.ds(pl.multiple_of(kj * C, LANES), C)]
                cps.append(_remote(src, outs[a].at[j], sems[0].at[3 * a + j], sems[1].at[3 * a + j], (*chip, c)))
        return cps

    def start(ins, outs, sems):
        for cp in copies(ins, outs, sems):
            cp.start()

    def finish(ins, outs, sems):
        cps = copies(ins, outs, sems)
        for cp in cps:
            cp.wait_recv()
        for cp in cps:
            cp.wait_send()

    out_shape = tuple(_sds((3, p.shape[0], p.shape[2], C), BF16) for p, C in zip(ps, shard_cols))
    aliases = {n + a: a for a in range(n)} if into is not None else {}
    return Rider(tuple(ps) + tuple(into or ()), out_shape, aliases,
                 (pltpu.SemaphoreType.DMA((3 * n,)), pltpu.SemaphoreType.DMA((3 * n,))), start, finish)


def _run_rider(name, rider):
    n_in, n_out = len(rider.operands), len(rider.out_shape)

    def body(*refs):
        ins, outs, sems = refs[:n_in], refs[n_in:n_in + n_out], refs[n_in + n_out:]
        rider.start(ins, outs, sems)
        _rider_end(rider, ins, outs, sems)

    return pl.pallas_call(
        body, name=name, out_shape=list(rider.out_shape), in_specs=[ANY] * n_in, out_specs=[ANY] * n_out,
        input_output_aliases=dict(rider.aliases), scratch_shapes=list(rider.sems),
        compiler_params=pltpu.CompilerParams(vmem_limit_bytes=VMEM_LIMIT_BYTES))(*rider.operands)


def _exchange_rider(g5s):
    n = len(g5s)

    def copies(ins, outs, sems):
        x, y, c, _ = _mesh_place()
        return [_remote(ins[a].at[:, :, 1 - c], outs[a], sems[0].at[a], sems[1].at[a], (x, y, 1 - c)) for a in range(n)]

    def start(ins, outs, sems):
        for cp in copies(ins, outs, sems):
            cp.start()

    def finish(ins, outs, sems):
        cps = copies(ins, outs, sems)
        for cp in cps:
            cp.wait_recv()
        for cp in cps:
            cp.wait_send()

    out_shape = tuple(_sds((g.shape[0], g.shape[1], g.shape[3], g.shape[4]), BF16) for g in g5s)
    return Rider(tuple(g5s), out_shape, {}, (pltpu.SemaphoreType.DMA((n,)), pltpu.SemaphoreType.DMA((n,))), start, finish)


def _share_rider(fs):
    n = len(fs)

    def start(_, outs, sems):
        x, y, c, _p = _mesh_place()
        for a in range(n):
            mine = outs[a].at[:, c]
            _remote(mine, mine, sems[0].at[a], sems[1].at[a], (x, y, 1 - c)).start()

    def finish(_, outs, sems):
        x, y, c, _p = _mesh_place()
        for a in range(n):
            theirs = outs[a].at[:, 1 - c]
            _remote(theirs, theirs, sems[0].at[a], sems[1].at[a], (x, y, c)).wait_recv()
        for a in range(n):
            mine = outs[a].at[:, c]
            _remote(mine, mine, sems[0].at[a], sems[1].at[a], (x, y, 1 - c)).wait_send()

    return Rider(tuple(fs), tuple(_sds(f.shape, F32) for f in fs), {a: a for a in range(n)},
                 (pltpu.SemaphoreType.DMA((n,)), pltpu.SemaphoreType.DMA((n,))), start, finish)


def _both_riders(r1, r2):
    ni, no, ns = len(r1.operands), len(r1.out_shape), len(r1.sems)
    aliases = dict(r1.aliases)
    aliases.update({ni + i: no + o for i, o in r2.aliases.items()})

    def start(ins, outs, sems):
        r1.start(ins[:ni], outs[:no], sems[:ns])
        r2.start(ins[ni:], outs[no:], sems[ns:])

    def finish(ins, outs, sems):
        _rider_end(r1, ins[:ni], outs[:no], sems[:ns])
        _rider_end(r2, ins[ni:], outs[no:], sems[ns:])

    return Rider(r1.operands + r2.operands, r1.out_shape + r2.out_shape, aliases, r1.sems + r2.sems, start, finish)


def _pack_rows(parts, lane_mult=1024):
    flat = jnp.concatenate([p.reshape(-1).astype(F32) for p in parts])
    n = -(-flat.shape[0] // (8 * lane_mult)) * lane_mult
    return jnp.pad(flat, (0, 8 * n - flat.shape[0])).reshape(8, n)


def _relu2(acc):
    r = jnp.maximum(acc, 0.0)
    return r, r * r


def _times_2r(acc, r):
    return (acc * (2.0 * r.astype(F32)),)


def kernel(x, c, positions, w_mod, b_mod, norm_g, mla_w_in, mla_g_q, mla_g_kv, mla_w_uq, mla_w_ukv, mla_w_o, conv_w_in, conv_w, conv_w_out, mlp_w_up, mlp_w_down, loss_target, m_w_mod, m_b_mod, m_norm_g, m_mla_w_in, m_mla_g_q, m_mla_g_kv, m_mla_w_uq, m_mla_w_ukv, m_mla_w_o, m_conv_w_in, m_conv_w, m_conv_w_out, m_mlp_w_up, m_mlp_w_down, v_w_mod, v_b_mod, v_norm_g, v_mla_w_in, v_mla_g_q, v_mla_g_kv, v_mla_w_uq, v_mla_w_ukv, v_mla_w_o, v_conv_w_in, v_conv_w, v_conv_w_out, v_mlp_w_up, v_mlp_w_down):
    S, D = x.shape[1], x.shape[2]
    Dq = D // N_CHIPS
    ncol = w_mod.shape[2]
    n_mod = N_CHIPS * ncol // D
    F = mlp_w_up.shape[2] * N_CHIPS
    lat_dim = mla_w_in.shape[2]
    rank = mla_g_q.shape[1]
    H = mla_w_uq.shape[2]
    d_qk = mla_w_uq.shape[3]
    assert mla_g_kv.shape[1] == rank and lat_dim == 2 * rank + QK_ROPE and d_qk == QK_NOPE + QK_ROPE
    assert mla_w_ukv.shape[3] == QK_NOPE + V_HEAD and x.shape[0] == 1 and n_mod == 6
    assert norm_g.shape[0] == 2 and mla_w_in.shape[0] == 1 and conv_w_in.shape[0] == 1
    lat_pad = 2 * rank + LANES
    scale = float(d_qk) ** -0.5

    xi, yi, ci = lax.axis_index("x"), lax.axis_index("y"), lax.axis_index("c")
    chip = 2 * xi + yi
    dev = 2 * chip + ci
    c_idx = jnp.reshape(ci, (1,)).astype(jnp.int32)
    k_idx = jnp.reshape(chip, (1,)).astype(jnp.int32)

    n1 = D + 2 * D + 3 * Dq
    g1 = _small_allgather("gather_small_inputs", _pack_rows([c, norm_g, conv_w])).reshape(8, -1)
    c_all = g1[:, :D]
    by_chip = g1[0::2]
    norm_full = jnp.concatenate([by_chip[kk, D:3 * D].reshape(2, 4, Dq) for kk in range(N_CHIPS)], axis=-1)
    convw_full = jnp.concatenate([by_chip[kk, 3 * D:n1].reshape(3, Dq) for kk in range(N_CHIPS)], axis=-1)

    b_cols = lax.dynamic_slice(b_mod, (0, chip * ncol), (2, ncol)).reshape(2, 1, ncol)
    cond_all = _silu(c_all)
    mod_cols = _mod_fwd(cond_all, w_mod, b_cols)
    g2 = _small_allgather("gather_mod", _pack_rows([mod_cols]))
    g2 = g2.reshape(8, -1)[0::2, :2 * 8 * ncol].reshape(N_CHIPS, 2, 8, ncol)
    mod_all = jnp.transpose(g2, (2, 1, 0, 3)).reshape(8, 2, n_mod * D)
    mod_me = lax.dynamic_index_in_dim(mod_all, dev, axis=0, keepdims=False)
    mods = [[mod_me[l, i * D:(i + 1) * D].reshape(1, D) for i in range(n_mod)] for l in range(2)]
    ng = [[norm_full[l, i].reshape(1, D) for i in range(4)] for l in range(2)]

    pos = positions[0].astype(F32)
    inv_freq = ROPE_THETA ** (-jnp.arange(0, QK_ROPE, 2, dtype=F32) / QK_ROPE)
    ang = pos[:, None] * inv_freq
    cos, sin = jnp.cos(ang), jnp.sin(ang)
    zero = jnp.zeros_like(cos)
    rope_tabs = (jnp.concatenate([cos, cos, zero, zero], axis=1),
                 jnp.concatenate([-sin, zero, zero, zero], axis=1),
                 jnp.concatenate([zero, sin, zero, zero], axis=1))

    weights = [("mla_w_in", mla_w_in, "row"), ("mla_w_uq", mla_w_uq.reshape(1, rank // N_CHIPS, H * d_qk), "row"),
               ("mla_w_ukv", mla_w_ukv.reshape(1, rank // N_CHIPS, H * QK_PAD), "row"), ("mla_w_o", mla_w_o, "row"),
               ("conv_w_in", conv_w_in, "col"), ("conv_w_out", conv_w_out, "row"),
               ("mlp_w_up", mlp_w_up, "col"), ("mlp_w_down", mlp_w_down, "row")]
    kinds = [k for _, _, k in weights]
    shard_shapes = [w.shape for _, w, _ in weights]
    shard_cols = [s[2] for s in shard_shapes]
    W_IN, W_UQ, W_UKV, W_O, W_CIN, W_COUT, W_UP, W_DOWN = range(8)
    mla_idx = [W_IN, W_UQ, W_UKV, W_O]
    casted = [_cast_into_full("cast_" + nm, [w], [kind], k_idx)[0][0] for nm, w, kind in weights[:W_UP]]

    def view(i, buf):
        L, R, C = shard_shapes[i]
        return buf.reshape((L, N_CHIPS * R, C) if kinds[i] == "row" else (L, R, N_CHIPS * C))

    NEIGHBOURS, DIAGONAL = (0, 1), (2,)

    def gather_of(bufs, idx, layers=None, peers=(0, 1, 2)):
        return _gather_rider(bufs, [kinds[i] for i in idx], [shard_cols[i] for i in idx], layers, peers)

    def scatter_of(ps, idx, peers=(0, 1, 2), into=None):
        return _scatter_rider(ps, [kinds[i] for i in idx], [shard_cols[i] for i in idx], peers, into)

    def halves(items):
        g5s = []
        for _, i, g in items:
            _, R, C = shard_shapes[i]
            g5s.append(g.reshape((1, N_CHIPS, 2, R // 2, C) if kinds[i] == "row" else (1, 1, 2, R // 2, N_CHIPS * C)))
        return g5s

    def pair_sums(items, g5s, ras):
        return [_pair_sum("pair_sum_" + nm, g5, ra, c_idx) for (nm, _, _), g5, ra in zip(items, g5s, ras)]

    mlp_casted, got = _cast_into_full("cast_mlp_w", [mlp_w_up, mlp_w_down], [kinds[W_UP], kinds[W_DOWN]], k_idx,
                                      gather_of([casted[i] for i in mla_idx], mla_idx))
    casted += list(mlp_casted)
    w_in_p = jnp.pad(view(W_IN, got[0])[0], ((0, 0), (0, lat_pad - lat_dim)))
    w_q_p = jnp.pad(view(W_UQ, got[1])[0].reshape(rank, H, d_qk), ((0, 0), (0, 0), (0, QK_PAD - d_qk))).reshape(rank, H * QK_PAD)
    w_ukv, w_o = view(W_UKV, got[2])[0], view(W_O, got[3])[0]
    HV = H * V_HEAD

    def layer_b(l, transposed):
        if transposed:
            return lambda tm, tn, tk: pl.BlockSpec((None, tn, tk), lambda i, j, k: (l, j, k))
        return lambda tm, tn, tk: pl.BlockSpec((None, tk, tn), lambda i, j, k: (l, k, j))

    def mlp_up(tag, l, h, w, rider=NO_RIDER):
        return _mm("mlp_up_" + tag, h, w, "nn", S, F, D, [_sds((S, F), BF16)] * 2, epilogue=_relu2,
                   b_spec=layer_b(l, False), rider=rider)

    def mlp_down(tag, l, a2, w, rider=NO_RIDER):
        return _mm("mlp_down_" + tag, a2, w, "nn", S, D, F, [_sds((S, D), BF16)], b_spec=layer_b(l, False), rider=rider)

    def mlp_bwd(tag, l, h, r, a2, dy, first=NO_RIDER, second_of=None):
        res = _mm("mlp_down_dx_" + tag, dy, w_down, "nt", S, F, D, [_sds((S, F), BF16)], epilogue=_times_2r,
                  b_spec=layer_b(l, True), rider=first,
                  extras=[(r, lambda tm, tn, tk: pl.BlockSpec((tm, tn), lambda i, j, k: (i, j)))])
        (da,), got_first = res if first.start is not None else (res, ())
        second = second_of(got_first) if second_of else NO_RIDER
        res = _mm("mlp_down_dw_" + tag, a2, dy, "tn", F, D, S, [_sds((F, D), BF16)], rider=second)
        (dw_down,), got_second = res if second_of else (res, ())
        (dh,) = _mm("mlp_up_dx_" + tag, da, w_up, "nt", S, D, F, [_sds((S, D), BF16)], b_spec=layer_b(l, True))
        (dw_up,) = _mm("mlp_up_dw_" + tag, h, da, "tn", D, F, S, [_sds((D, F), BF16)])
        return dh, dw_up, dw_down, got_first, got_second

    x0 = x[0]
    sh1, sc1, gt1, sh2, sc2, gt2 = mods[0]
    (h1,) = _fwd_boundary("fwd_boundary_0", x0, None, None, None, ng[0][0], sc1, sh1)
    (lat,) = _mm("mla_in", h1, w_in_p, "nn", S, lat_pad, D, [_sds((S, lat_pad), F32)], tn=lat_pad)
    cq, ckv, kr = _latent_fwd(lat, mla_g_q, mla_g_kv, rope_tabs, rank)

    def rope_q(acc, cos_p, sin_lo, sin_hi):
        parts = []
        for hh in range(acc.shape[1] // QK_PAD):
            parts.append(acc[:, hh * QK_PAD:hh * QK_PAD + QK_NOPE])
            parts.append(_rope(acc[:, hh * QK_PAD + QK_NOPE:(hh + 1) * QK_PAD], cos_p, sin_lo, sin_hi))
        return (jnp.concatenate(parts, axis=1),)

    tab_extra = lambda tm, tn, tk: pl.BlockSpec((tm, LANES), lambda i, j, k: (i, 0))
    (q,) = _mm("mla_q", cq, w_q_p, "nn", S, H * QK_PAD, rank, [_sds((S, H * QK_PAD), BF16)], epilogue=rope_q,
               extras=[(t, tab_extra) for t in rope_tabs], tn=2 * QK_PAD)
    (kv,) = _mm("mla_kv", ckv, w_ukv, "nn", S, H * QK_PAD, rank, [_sds((S, H * QK_PAD), BF16)])
    rest_idx = [W_UP, W_DOWN]
    o, lse, (up_buf, down_buf) = _attn_fwd_tri(
        q, kv, kr, H, scale, gather_of([casted[i] for i in rest_idx], rest_idx, [0, 0]))
    (y1,), (cout_buf,) = _mm("mla_out", o, w_o, "nn", S, D, HV, [_sds((S, D), BF16)],
                             rider=gather_of([casted[W_COUT]], [W_COUT]))
    x1, h2 = _fwd_boundary("fwd_boundary_1", x0, y1, gt1, ng[0][1], ng[0][2], sc2, sh2)
    (r2, a2), (cin_buf,) = mlp_up("0", 0, h2, view(W_UP, up_buf), gather_of([casted[W_CIN]], [W_CIN]))
    (y2,), (up_buf,) = mlp_down("0", 0, a2, view(W_DOWN, down_buf), gather_of([up_buf], [W_UP], [1]))
    w_cin, w_cout, w_up = view(W_CIN, cin_buf)[0], view(W_COUT, cout_buf)[0], view(W_UP, up_buf)

    sh1b, sc1b, gt1b, sh2b, sc2b, gt2b = mods[1]
    x2, h3 = _fwd_boundary("fwd_boundary_2", x1, y2, gt2, ng[0][3], ng[1][0], sc1b, sh1b)
    nD = lambda tn: D // tn
    (proj3,), (down_buf,) = _mm(
        "conv_in", h3, w_cin, "nn", S, 3 * D, D, [_sds((3, S, D), BF16)], tn=min(1024, D),
        rider=gather_of([down_buf], [W_DOWN], [1], NEIGHBOURS),
        out_specs=[lambda tm, tn, tk: pl.BlockSpec((None, tm, tn), lambda i, j, k: (j // nD(tn), i, j % nD(tn)))])
    bz = _conv_fwd(proj3, convw_full)
    (y3,) = _mm("conv_out", bz, w_cout, "nn", S, D, D, [_sds((S, D), BF16)])
    x3, h4 = _fwd_boundary("fwd_boundary_3", x2, y3, gt1b, ng[1][1], ng[1][2], sc2b, sh2b)
    (r4, a4), (down_buf,) = mlp_up("1", 1, h4, w_up, gather_of([down_buf], [W_DOWN], [1], DIAGONAL))
    w_down = view(W_DOWN, down_buf)
    (y4,) = mlp_down("1", 1, a4, w_down)

    dx4, dy4, sums_l, loss_acc = _loss_boundary("loss_boundary", x3, y4, gt2b, ng[1][3], loss_target[0])

    dh4, dw_up1, dw_down1, _, _ = mlp_bwd("1", 1, h4, r4, a4, dy4)
    dx3, dy3, sums_3, _ = _bwd_boundary("bwd_boundary_3", dx4, dh4, x3, y3, gt1b, ng[1][1], ng[1][2], sc2b)

    items = [("mlp_w_up_1", W_UP, dw_up1), ("mlp_w_down_1", W_DOWN, dw_down1)]
    g5s = halves(items)
    (dbz,), ras = _mm("conv_out_dx", dy3, w_cout, "nt", S, D, D, [_sds((S, D), BF16)], rider=_exchange_rider(g5s))
    ps_up1, ps_down1 = pair_sums(items, g5s, ras)
    (dw_cout,) = _mm("conv_out_dw", bz, dy3, "tn", D, D, S, [_sds((D, D), BF16)])
    dproj3, dconvw = _conv_bwd(dbz, proj3, convw_full)
    (dh3,), (rb_up1,) = _mm(
        "conv_in_dx", dproj3, w_cin, "nt", S, D, 3 * D, [_sds((S, D), BF16)], tk=D,
        rider=scatter_of([ps_up1], [W_UP], NEIGHBOURS),
        a_spec=lambda tm, tn, tk: pl.BlockSpec((None, tm, tk), lambda i, j, k: (k // (D // tk), i, k % (D // tk))))
    (dw_cin,), (rb_up1,) = _mm(
        "conv_in_dw", h3, dproj3, "tn", D, 3 * D, S, [_sds((D, 3 * D), BF16)], tn=min(1024, D),
        rider=scatter_of([ps_up1], [W_UP], DIAGONAL, [rb_up1]),
        b_spec=lambda tm, tn, tk: pl.BlockSpec((None, tk, tn), lambda i, j, k: (j // nD(tn), k, j % nD(tn))))
    dx2, dy2, sums_2, _ = _bwd_boundary("bwd_boundary_2", dx3, dh3, x2, y2, gt2, ng[0][3], ng[1][0], sc1b)

    items = [("conv_w_in", W_CIN, dw_cin), ("conv_w_out", W_COUT, dw_cout)]
    g5s = halves(items)
    dh2, dw_up0, dw_down0, (rb_down1, *ras), (rb_down1,) = mlp_bwd(
        "0", 0, h2, r2, a2, dy2,
        _both_riders(scatter_of([ps_down1], [W_DOWN], NEIGHBOURS), _exchange_rider(g5s)),
        lambda got: scatter_of([ps_down1], [W_DOWN], DIAGONAL, [got[0]]))
    ps_cin, ps_cout = pair_sums(items, g5s, ras)
    dx1, dy1, sums_1, _ = _bwd_boundary("bwd_boundary_1", dx2, dh2, x1, y1, gt1, ng[0][1], ng[0][2], sc2)

    items = [("mlp_w_up_0", W_UP, dw_up0), ("mlp_w_down_0", W_DOWN, dw_down0)]
    g5s = halves(items)
    (dw_o,), ras = _mm("mla_out_dw", o, dy1, "tn", HV, D, S, [_sds((HV, D), BF16)], rider=_exchange_rider(g5s))
    ps_up0, ps_down0 = pair_sums(items, g5s, ras)
    items = [(weights[W_O][0], W_O, dw_o)]
    g5s = halves(items)
    (do,), ras = _mm("mla_out_dx", dy1, w_o, "nt", S, HV, D, [_sds((S, HV), BF16)], rider=_exchange_rider(g5s))
    (ps_o,) = pair_sums(items, g5s, ras)
    dq, dkv, dkr, (rb_up0, rb_down0, rb_cin, rb_cout, rb_o) = _attn_bwd_tri(
        q, kv, kr, o, do, lse, rope_tabs, H, scale,
        scatter_of([ps_up0, ps_down0, ps_cin, ps_cout, ps_o], [W_UP, W_DOWN, W_CIN, W_COUT, W_O]))
    (dcq,) = _mm("mla_q_dx", dq, w_q_p, "nt", S, rank, H * QK_PAD, [_sds((S, rank), F32)])
    (dw_q_p,) = _mm("mla_q_dw", cq, dq, "tn", rank, H * QK_PAD, S, [_sds((rank, H * QK_PAD), BF16)])
    (dckv,) = _mm("mla_kv_dx", dkv, w_ukv, "nt", S, rank, H * QK_PAD, [_sds((S, rank), F32)])
    (dw_ukv,) = _mm("mla_kv_dw", ckv, dkv, "tn", rank, H * QK_PAD, S, [_sds((rank, H * QK_PAD), BF16)])
    dlat, sums_lat = _latent_bwd(lat, dcq, dckv, dkr, mla_g_q, mla_g_kv, rope_tabs, rank)
    (dw_in_p,) = _mm("mla_in_dw", h1, dlat, "tn", D, lat_pad, S, [_sds((D, lat_pad), BF16)], tn=lat_pad)
    late_idx = [W_IN, W_UQ, W_UKV]
    dw_mla = [dw_in_p[:, :lat_dim], dw_q_p.reshape(rank, H, QK_PAD)[:, :, :d_qk].reshape(rank, H * d_qk), dw_ukv]
    items = [(weights[i][0], i, g) for i, g in zip(late_idx, dw_mla)]
    g5s = halves(items)
    (dh1,), ras = _mm("mla_in_dx", dlat, w_in_p, "nt", S, D, lat_pad, [_sds((S, D), BF16)], rider=_exchange_rider(g5s))
    ps_mla = pair_sums(items, g5s, ras)
    grad_x, sums_0, _ = _bwd_boundary("bwd_boundary_0", dx1, dh1, x0, None, None, None, ng[0][0], sc1)

    kc_idx = jnp.stack([chip, ci]).astype(jnp.int32)
    fs_rest = [_chip_sum("chip_sum_" + weights[i][0], p, rb, kc_idx, kinds[i])
               for i, p, rb in [(W_O, ps_o, rb_o), (W_CIN, ps_cin, rb_cin), (W_COUT, ps_cout, rb_cout)]]
    for i, (p1, r1), (p0, r0) in [(W_UP, (ps_up1, rb_up1), (ps_up0, rb_up0)), (W_DOWN, (ps_down1, rb_down1), (ps_down0, rb_down0))]:
        f = _chip_sum("chip_sum_" + weights[i][0] + "_1", p1, r1, kc_idx, kinds[i], layer=1, n_layers=2)
        fs_rest.append(_chip_sum("chip_sum_" + weights[i][0] + "_0", p0, r0, kc_idx, kinds[i], layer=0, n_layers=2, prev=f))

    dmod0 = [sums_0[0], sums_0[1], sums_1[3], sums_1[0], sums_1[1], sums_2[3]]
    dmod1 = [sums_2[0], sums_2[1], sums_3[3], sums_3[0], sums_3[1], sums_l[3]]
    dng0 = [sums_0[2], sums_1[4], sums_1[2], sums_2[4]]
    dng1 = [sums_2[2], sums_3[4], sums_3[2], sums_l[4]]
    small = _pack_rows(dmod0 + dmod1 + dng0 + dng1 + [sums_lat[0], sums_lat[1], dconvw, loss_acc[0, 0:1]],
                       lane_mult=LANES)
    gathered, total, carried = _small_allgather(
        "gather_small_grads", small, with_sum=True, rider=_both_riders(scatter_of(ps_mla, late_idx), _share_rider(fs_rest)))
    rbs_mla, finals_rest = carried[:len(late_idx)], carried[len(late_idx):]
    n_dm = 2 * n_mod * D
    dmod_all = gathered.reshape(8, -1)[:, :n_dm].reshape(8, 2, n_mod * D)
    total = total.reshape(-1)
    g_b_mod = total[:n_dm].reshape(2, n_mod * D)
    g_norm = lax.dynamic_slice(total[n_dm:n_dm + 8 * D].reshape(2, 4, D), (0, 0, chip * Dq), (2, 4, Dq))
    off = n_dm + 8 * D
    g_gq = total[off:off + rank].reshape(1, rank)
    g_gkv = total[off + rank:off + 2 * rank].reshape(1, rank)
    off += 2 * rank
    g_convw = lax.dynamic_slice(total[off:off + 3 * D].reshape(1, 3, D), (0, 0, chip * Dq), (1, 3, Dq))
    loss = total[off + 3 * D]

    dmod_cols = jnp.transpose(lax.dynamic_slice(dmod_all.reshape(8, 2, N_CHIPS, ncol), (0, 0, chip, 0), (8, 2, 1, ncol))
                              .reshape(8, 2, ncol), (1, 0, 2))
    g_w_mod, d_w_mod, nm_w_mod, nv_w_mod, _ = _adamw_mod(w_mod, cond_all.T, dmod_cols, m_w_mod, v_w_mod)
    fs_mla = [_chip_sum("chip_sum_" + weights[i][0], p, rb, kc_idx, kinds[i]) for i, p, rb in zip(late_idx, ps_mla, rbs_mla)]
    finals = list(_run_rider("grad_pair_share_mla", _share_rider(fs_mla))) + list(finals_rest)
    orig = [mla_w_in, mla_w_uq, mla_w_ukv, mla_w_o, conv_w_in, conv_w_out, mlp_w_up, mlp_w_down]
    big_grads = [f.reshape(w.shape) for f, w in zip(finals, orig)]

    names = ["b_mod", "norm_g", "mla_w_in", "mla_g_q", "mla_g_kv", "mla_w_uq", "mla_w_ukv", "mla_w_o",
             "conv_w_in", "conv_w", "conv_w_out", "mlp_w_up", "mlp_w_down"]
    ws = [b_mod, norm_g, mla_w_in, mla_g_q, mla_g_kv, mla_w_uq, mla_w_ukv, mla_w_o, conv_w_in, conv_w, conv_w_out,
          mlp_w_up, mlp_w_down]
    ms = [m_b_mod, m_norm_g, m_mla_w_in, m_mla_g_q, m_mla_g_kv, m_mla_w_uq, m_mla_w_ukv, m_mla_w_o, m_conv_w_in,
          m_conv_w, m_conv_w_out, m_mlp_w_up, m_mlp_w_down]
    vs = [v_b_mod, v_norm_g, v_mla_w_in, v_mla_g_q, v_mla_g_kv, v_mla_w_uq, v_mla_w_ukv, v_mla_w_o, v_conv_w_in,
          v_conv_w, v_conv_w_out, v_mlp_w_up, v_mlp_w_down]
    gs = [g_b_mod, g_norm, big_grads[0], g_gq, g_gkv, big_grads[1], big_grads[2], big_grads[3], big_grads[4],
          g_convw, big_grads[5], big_grads[6], big_grads[7]]
    grads, deltas, new_ms, new_vs = [g_w_mod], [d_w_mod], [nm_w_mod], [nv_w_mod]
    for nm, w, g, m, v in zip(names, ws, gs, ms, vs):
        d, nm_, nv_ = _adamw("adamw_" + nm, w, g, m, v)
        grads.append(g)
        deltas.append(d)
        new_ms.append(nm_)
        new_vs.append(nv_)
    return (loss, grad_x[None], *grads, *deltas, *new_ms, *new_vs)
```

```python
from typing import NamedTuple

import jax
import jax.numpy as jnp
from jax import lax
from jax.experimental import pallas as pl
from jax.experimental.pallas import tpu as pltpu

F32 = jnp.float32
BF16 = jnp.bfloat16
NORM_EPS = 1e-6
ROPE_THETA = 10000.0
QK_NOPE = 128
QK_ROPE = 64
V_HEAD = 128
LANES = 128
QK_PAD = QK_NOPE + LANES
ADAM_LR, ADAM_B1, ADAM_B2, ADAM_EPS, ADAM_WD, ADAM_STEP = 0.001, 0.9, 0.999, 1e-08, 0.01, 10
VMEM_LIMIT_BYTES = 56 * 1024 * 1024
N_CHIPS = 4
MESH_ID = pl.DeviceIdType.MESH
ANY = pl.BlockSpec(memory_space=pl.ANY)
NEG_INF = float("-inf")

DIMS_NN = (((1,), (0,)), ((), ()))
DIMS_NT = (((1,), (1,)), ((), ()))
DIMS_TN = (((0,), (0,)), ((), ()))


def _cparams(*sem):
    return pltpu.CompilerParams(dimension_semantics=sem, vmem_limit_bytes=VMEM_LIMIT_BYTES)


def _row_tile(rows, row_bytes, limit=2 * 1024 * 1024, mult=16):
    if rows * row_bytes <= limit or rows % mult:
        return rows
    best = mult
    t = mult
    while t <= rows:
        if rows % t == 0 and t * row_bytes <= limit:
            best = t
        t += mult
    return best


def _rms(v):
    return lax.rsqrt(jnp.mean(v * v, axis=-1, keepdims=True) + NORM_EPS)


class Rider(NamedTuple):
    operands: tuple
    out_shape: tuple
    aliases: dict
    sems: tuple
    start: object
    finish: object
    mid: object = None


NO_RIDER = Rider((), (), {}, (), None, None)


def _rider_end(rider, r_in, r_out, r_sems):
    if rider.mid is not None:
        rider.mid(r_in, r_out, r_sems)
    rider.finish(r_in, r_out, r_sems)


def _mm(name, a, b, mode, M, N, K, outs, *, a_spec=None, b_spec=None, out_specs=None, epilogue=None,
        extras=(), rider=NO_RIDER, tm=1024, tn=1024, tk=4096):
    tm, tn, tk = min(tm, M), min(tn, N), min(tk, K)
    assert M % tm == 0 and N % tn == 0 and K % tk == 0, (name, M, N, K)
    nk = K // tk
    if a_spec is None:
        a_spec = {"nn": pl.BlockSpec((tm, tk), lambda i, j, k: (i, k)),
                  "nt": pl.BlockSpec((tm, tk), lambda i, j, k: (i, k)),
                  "tn": pl.BlockSpec((tk, tm), lambda i, j, k: (k, i))}[mode]
    else:
        a_spec = a_spec(tm, tn, tk)
    if b_spec is None:
        b_spec = {"nn": pl.BlockSpec((tk, tn), lambda i, j, k: (k, j)),
                  "nt": pl.BlockSpec((tn, tk), lambda i, j, k: (j, k)),
                  "tn": pl.BlockSpec((tk, tn), lambda i, j, k: (k, j))}[mode]
    else:
        b_spec = b_spec(tm, tn, tk)
    if out_specs is None:
        out_specs = [pl.BlockSpec((tm, tn), lambda i, j, k: (i, j)) for _ in outs]
    else:
        out_specs = [s(tm, tn, tk) for s in out_specs]
    dims = {"nn": DIMS_NN, "nt": DIMS_NT, "tn": DIMS_TN}[mode]
    ne, no = len(extras), len(outs)
    n_ri, n_ro = len(rider.operands), len(rider.out_shape)
    grid = (M // tm, N // tn, nk)

    def body(*refs):
        a_ref, b_ref = refs[0], refs[1]
        ex = refs[2:2 + ne]
        r_in = refs[2 + ne:2 + ne + n_ri]
        o = refs[2 + ne + n_ri:2 + ne + n_ri + no]
        r_out = refs[2 + ne + n_ri + no:2 + ne + n_ri + no + n_ro]
        scratch = refs[2 + ne + n_ri + no + n_ro:]
        r_sems = scratch[1:] if nk > 1 else scratch
        ii, jj, kk = pl.program_id(0), pl.program_id(1), pl.program_id(2)

        if rider.start is not None:
            @pl.when((ii == 0) & (jj == 0) & (kk == 0))
            def _():
                rider.start(r_in, r_out, r_sems)

        part = lax.dot_general(a_ref[...].astype(BF16), b_ref[...].astype(BF16), dims,
                               preferred_element_type=F32)

        def finish(total):
            vals = epilogue(total, *[e[...] for e in ex]) if epilogue is not None else (total,)
            for r, v in zip(o, vals):
                r[...] = v.astype(r.dtype)

        if nk == 1:
            finish(part)
        else:
            acc = scratch[0]

            @pl.when(kk == 0)
            def _():
                acc[...] = part

            @pl.when(kk > 0)
            def _():
                acc[...] += part

            @pl.when(kk == nk - 1)
            def _():
                finish(acc[...])

        if rider.finish is not None:
            steps = grid[0] * grid[1] * nk
            if rider.mid is not None and steps >= 4:
                @pl.when((ii * grid[1] + jj) * nk + kk == steps // 2)
                def _():
                    rider.mid(r_in, r_out, r_sems)

            @pl.when((ii == grid[0] - 1) & (jj == grid[1] - 1) & (kk == nk - 1))
            def _():
                if rider.mid is not None and steps < 4:
                    rider.mid(r_in, r_out, r_sems)
                rider.finish(r_in, r_out, r_sems)

    operands = [a, b] + [e[0] for e in extras] + list(rider.operands)
    in_specs = [a_spec, b_spec] + [e[1](tm, tn, tk) for e in extras] + [ANY] * n_ri
    hosted = rider.start is not None
    res = pl.pallas_call(
        body, name=name, grid=grid,
        in_specs=in_specs, out_specs=out_specs + [ANY] * n_ro, out_shape=list(outs) + list(rider.out_shape),
        scratch_shapes=([pltpu.VMEM((tm, tn), F32)] if nk > 1 else []) + list(rider.sems),
        input_output_aliases={2 + ne + i: no + r for i, r in rider.aliases.items()},
        compiler_params=_cparams(*(("arbitrary",) * 3 if hosted else ("parallel", "parallel", "arbitrary"))),
    )(*operands)
    return (res[:no], res[no:]) if hosted else res


def _sds(shape, dtype):
    return jax.ShapeDtypeStruct(tuple(shape), dtype)


def _rope(t, cos_p, sin_lo, sin_hi):
    return t * cos_p + pltpu.roll(t, LANES - QK_ROPE // 2, 1) * sin_lo + pltpu.roll(t, QK_ROPE // 2, 1) * sin_hi


def _rope_t(d, cos_p, sin_lo, sin_hi):
    return d * cos_p + pltpu.roll(d * sin_lo, QK_ROPE // 2, 1) + pltpu.roll(d * sin_hi, LANES - QK_ROPE // 2, 1)


def _vec_spec(d):
    return pl.BlockSpec((1, d), lambda i: (0, 0))


def _fwd_boundary(name, x_prev, y, gate, ng_post, ng_pre, sc, sh):
    S, D = x_prev.shape
    ts = min(256, S)
    has_y = y is not None
    row = pl.BlockSpec((ts, D), lambda i: (i, 0))

    def body(*refs):
        if has_y:
            x_ref, y_ref, g_ref, ngp_ref, ngn_ref, sc_ref, sh_ref, xo_ref, h_ref = refs
            yv = y_ref[...].astype(F32)
            xn = x_ref[...] + g_ref[...] * (yv * _rms(yv) * ngp_ref[...])
            xo_ref[...] = xn
        else:
            x_ref, ngn_ref, sc_ref, sh_ref, h_ref = refs
            xn = x_ref[...]
        hn = xn * _rms(xn) * ngn_ref[...]
        h_ref[...] = (hn * (1.0 + sc_ref[...]) + sh_ref[...]).astype(BF16)

    vec = _vec_spec(D)
    if has_y:
        operands = (x_prev, y, gate, ng_post, ng_pre, sc, sh)
        in_specs = [row, row, vec, vec, vec, vec, vec]
        out_shape = [_sds((S, D), F32), _sds((S, D), BF16)]
        out_specs = [row, row]
    else:
        operands = (x_prev, ng_pre, sc, sh)
        in_specs = [row, vec, vec, vec]
        out_shape = [_sds((S, D), BF16)]
        out_specs = [row]
    return pl.pallas_call(body, name=name, grid=(S // ts,), in_specs=in_specs, out_specs=out_specs,
                          out_shape=out_shape, compiler_params=_cparams("parallel"))(*operands)


def _acc_rows(sums_ref, rows):
    for r, v in rows:
        sums_ref[r:r + 1, :] += jnp.sum(v, axis=0, keepdims=True)


def _post_norm_bwd(dxt, yv, gate, ng_post, sums_ref, dy_ref):
    r1 = _rms(yv)
    yhat = yv * r1
    dn = dxt * gate
    u = dn * ng_post
    dy = r1 * (u - yhat * jnp.mean(u * yhat, axis=-1, keepdims=True))
    dy_ref[...] = dy.astype(dy_ref.dtype)
    _acc_rows(sums_ref, [(3, dxt * (yhat * ng_post)), (4, dn * yhat)])


def _loss_boundary(name, x_prev, y, gate, ng_post, target):
    S, D = x_prev.shape
    ts = min(256, S)
    row = pl.BlockSpec((ts, D), lambda i: (i, 0))
    vec = _vec_spec(D)

    def body(x_ref, y_ref, g_ref, ngp_ref, t_ref, dx_ref, dy_ref, sums_ref, loss_ref):
        @pl.when(pl.program_id(0) == 0)
        def _():
            sums_ref[...] = jnp.zeros_like(sums_ref)
            loss_ref[...] = jnp.zeros_like(loss_ref)

        yv = y_ref[...].astype(F32)
        xf = x_ref[...] + g_ref[...] * (yv * _rms(yv) * ngp_ref[...])
        err = xf - t_ref[...]
        loss_ref[...] += 0.5 * jnp.sum(jnp.mean(err * err, axis=-1, keepdims=True))
        dxt = err / D
        dx_ref[...] = dxt
        _post_norm_bwd(dxt, yv, g_ref[...], ngp_ref[...], sums_ref, dy_ref)

    return pl.pallas_call(
        body, name=name, grid=(S // ts,),
        in_specs=[row, row, vec, vec, row],
        out_specs=[row, row, pl.BlockSpec((8, D), lambda i: (0, 0)), pl.BlockSpec((8, LANES), lambda i: (0, 0))],
        out_shape=[_sds((S, D), F32), _sds((S, D), BF16), _sds((8, D), F32), _sds((8, LANES), F32)],
        compiler_params=_cparams("arbitrary"))(x_prev, y, gate, ng_post, target)


def _bwd_boundary(name, dx_new, dh, x_new, y, gate, ng_post, ng_pre, sc, rider=NO_RIDER):
    S, D = x_new.shape
    ts = min(256, S)
    has_y = y is not None
    row = pl.BlockSpec((ts, D), lambda i: (i, 0))
    vec = _vec_spec(D)
    n_in, n_out = (8, 3) if has_y else (5, 2)
    n_ri, n_ro = len(rider.operands), len(rider.out_shape)

    def body(*refs):
        r_in = refs[n_in:n_in + n_ri]
        r_out = refs[n_in + n_ri + n_out:n_in + n_ri + n_out + n_ro]
        r_sems = refs[n_in + n_ri + n_out + n_ro:]
        own = refs[:n_in] + refs[n_in + n_ri:n_in + n_ri + n_out]
        if has_y:
            dxn_ref, dh_ref, x_ref, y_ref, g_ref, ngp_ref, ngn_ref, sc_ref, dxo_ref, dy_ref, sums_ref = own
        else:
            dxn_ref, dh_ref, x_ref, ngn_ref, sc_ref, dxo_ref, sums_ref = own

        @pl.when(pl.program_id(0) == 0)
        def _():
            sums_ref[...] = jnp.zeros_like(sums_ref)
            if rider.start is not None:
                rider.start(r_in, r_out, r_sems)

        xv = x_ref[...]
        dhv = dh_ref[...].astype(F32)
        ngn = ngn_ref[...]
        r2 = _rms(xv)
        xhat = xv * r2
        dn_pre = dhv * (1.0 + sc_ref[...])
        u2 = dn_pre * ngn
        dxt = dxn_ref[...] + r2 * (u2 - xhat * jnp.mean(u2 * xhat, axis=-1, keepdims=True))
        dxo_ref[...] = dxt
        _acc_rows(sums_ref, [(0, dhv), (1, dhv * (xhat * ngn)), (2, dn_pre * xhat)])
        if has_y:
            _post_norm_bwd(dxt, y_ref[...].astype(F32), g_ref[...], ngp_ref[...], sums_ref, dy_ref)

        if rider.finish is not None:
            @pl.when(pl.program_id(0) == S // ts - 1)
            def _():
                _rider_end(rider, r_in, r_out, r_sems)

    sums_spec = pl.BlockSpec((8, D), lambda i: (0, 0))
    if has_y:
        operands = (dx_new, dh, x_new, y, gate, ng_post, ng_pre, sc)
        in_specs = [row, row, row, row, vec, vec, vec, vec]
        out_shape = [_sds((S, D), F32), _sds((S, D), BF16), _sds((8, D), F32)]
        out_specs = [row, row, sums_spec]
    else:
        operands = (dx_new, dh, x_new, ng_pre, sc)
        in_specs = [row, row, row, vec, vec]
        out_shape = [_sds((S, D), F32), _sds((8, D), F32)]
        out_specs = [row, sums_spec]
    res = pl.pallas_call(
        body, name=name, grid=(S // ts,), in_specs=in_specs + [ANY] * n_ri, out_specs=out_specs + [ANY] * n_ro,
        out_shape=out_shape + list(rider.out_shape), scratch_shapes=list(rider.sems),
        input_output_aliases={n_in + i: n_out + o for i, o in rider.aliases.items()},
        compiler_params=_cparams("arbitrary"))(*operands, *rider.operands)
    return (*res[:n_out], res[n_out:])


def _latent_fwd(lat, g_q, g_kv, rope_tabs, rank):
    S, W = lat.shape
    ts = min(256, S)
    tab = pl.BlockSpec((ts, LANES), lambda i: (i, 0))

    def body(lat_ref, gq_ref, gkv_ref, cos_ref, slo_ref, shi_ref, cq_ref, ckv_ref, kr_ref):
        lq = lat_ref[:, 0:rank]
        lkv = lat_ref[:, rank:2 * rank]
        cq_ref[...] = (lq * _rms(lq) * gq_ref[...]).astype(BF16)
        ckv_ref[...] = (lkv * _rms(lkv) * gkv_ref[...]).astype(BF16)
        kr_ref[...] = _rope(lat_ref[:, 2 * rank:W], cos_ref[...], slo_ref[...], shi_ref[...]).astype(BF16)

    return pl.pallas_call(
        body, name="mla_latent_fwd", grid=(S // ts,),
        in_specs=[pl.BlockSpec((ts, W), lambda i: (i, 0)), _vec_spec(rank), _vec_spec(rank), tab, tab, tab],
        out_specs=[pl.BlockSpec((ts, rank), lambda i: (i, 0)), pl.BlockSpec((ts, rank), lambda i: (i, 0)), tab],
        out_shape=[_sds((S, rank), BF16), _sds((S, rank), BF16), _sds((S, LANES), BF16)],
        compiler_params=_cparams("parallel"))(lat, g_q, g_kv, *rope_tabs)


def _latent_bwd(lat, dcq, dckv, dkr, g_q, g_kv, rope_tabs, rank):
    S, W = lat.shape
    ts = min(256, S)
    tab = pl.BlockSpec((ts, LANES), lambda i: (i, 0))
    half = pl.BlockSpec((ts, rank), lambda i: (i, 0))

    def body(lat_ref, dcq_ref, dckv_ref, dkr_ref, gq_ref, gkv_ref, cos_ref, slo_ref, shi_ref, dlat_ref, sums_ref):
        @pl.when(pl.program_id(0) == 0)
        def _():
            sums_ref[...] = jnp.zeros_like(sums_ref)

        def norm_bwd(v, dn, g, r):
            rr = _rms(v)
            vhat = v * rr
            u = dn * g
            sums_ref[r:r + 1, :] += jnp.sum(dn * vhat, axis=0, keepdims=True)
            return rr * (u - vhat * jnp.mean(u * vhat, axis=-1, keepdims=True))

        dlat_ref[:, 0:rank] = norm_bwd(lat_ref[:, 0:rank], dcq_ref[...], gq_ref[...], 0).astype(BF16)
        dlat_ref[:, rank:2 * rank] = norm_bwd(lat_ref[:, rank:2 * rank], dckv_ref[...], gkv_ref[...], 1).astype(BF16)
        dlat_ref[:, 2 * rank:W] = _rope_t(dkr_ref[...], cos_ref[...], slo_ref[...], shi_ref[...]).astype(BF16)

    return pl.pallas_call(
        body, name="mla_latent_bwd", grid=(S // ts,),
        in_specs=[pl.BlockSpec((ts, W), lambda i: (i, 0)), half, half, tab, _vec_spec(rank), _vec_spec(rank),
                  tab, tab, tab],
        out_specs=[pl.BlockSpec((ts, W), lambda i: (i, 0)), pl.BlockSpec((8, rank), lambda i: (0, 0))],
        out_shape=[_sds((S, W), BF16), _sds((8, rank), F32)],
        compiler_params=_cparams("arbitrary"))(lat, dcq, dckv, dkr, g_q, g_kv, *rope_tabs)


def _attn_tiles(S):
    t = min(512, S)
    return t, S // t


def _causal_mask(t):
    return lax.broadcasted_iota(jnp.int32, (t, t), 1) <= lax.broadcasted_iota(jnp.int32, (t, t), 0)


def _causal_pairs(nb, q_major):
    if q_major:
        pairs = [(qi, ki) for qi in range(nb) for ki in range(qi + 1)]
    else:
        pairs = [(qi, ki) for ki in range(nb) for qi in range(ki, nb)]
    return jnp.array([p[0] for p in pairs], jnp.int32), jnp.array([p[1] for p in pairs], jnp.int32), len(pairs)


def _heads_per_step(heads):
    return 2 if heads % 2 == 0 else 1


def _attn_fwd_tri(q, kv, kr, heads, scale, rider=NO_RIDER):
    S = q.shape[0]
    t, nb = _attn_tiles(S)
    G = _heads_per_step(heads)
    q_tab, k_tab, n_pairs = _causal_pairs(nb, True)
    n_ri, n_ro = len(rider.operands), len(rider.out_shape)

    def body(qt_ref, kt_ref, *refs):
        q_ref, kv_ref, kr_ref = refs[:3]
        r_in = refs[3:3 + n_ri]
        o_ref, lse_ref = refs[3 + n_ri:5 + n_ri]
        r_out = refs[5 + n_ri:5 + n_ri + n_ro]
        m_scr, acc_scr = refs[5 + n_ri + n_ro:7 + n_ri + n_ro]
        r_sems = refs[7 + n_ri + n_ro:]
        h, p = pl.program_id(0), pl.program_id(1)
        qi, ki = qt_ref[p], kt_ref[p]

        if rider.start is not None:
            @pl.when((h == 0) & (p == 0))
            def _():
                rider.start(r_in, r_out, r_sems)

        @pl.when(ki == 0)
        def _():
            m_scr[...] = jnp.full_like(m_scr, NEG_INF)
            acc_scr[...] = jnp.zeros_like(acc_scr)

        def step(diagonal):
            ones = jnp.ones((t, LANES), BF16)
            for g in range(G):
                kcat = jnp.concatenate([kv_ref[:, g * QK_PAD:g * QK_PAD + QK_NOPE], kr_ref[...]], axis=1)
                vext = jnp.concatenate([kv_ref[:, g * QK_PAD + QK_NOPE:(g + 1) * QK_PAD], ones], axis=1)
                s = lax.dot_general(q_ref[:, g * QK_PAD:(g + 1) * QK_PAD], kcat, DIMS_NT,
                                    preferred_element_type=F32) * scale
                if diagonal:
                    s = jnp.where(_causal_mask(t), s, NEG_INF)
                m_prev = m_scr[g]
                m_new = jnp.maximum(m_prev, jnp.max(s, axis=-1, keepdims=True))
                alpha = jnp.exp(m_prev - m_new)
                pr = jnp.exp(s - jnp.tile(m_new, (1, t // LANES)))
                acc_scr[g] = jnp.tile(alpha, (1, 2)) * acc_scr[g] + lax.dot_general(
                    pr.astype(BF16), vext, DIMS_NN, preferred_element_type=F32)
                m_scr[g] = m_new

        @pl.when(ki < qi)
        def _():
            step(False)

        @pl.when(ki == qi)
        def _():
            step(True)
            for g in range(G):
                acc = acc_scr[g]
                o_ref[:, g * V_HEAD:(g + 1) * V_HEAD] = (acc[:, 0:V_HEAD] / acc[:, V_HEAD:2 * V_HEAD]).astype(BF16)
                lse_ref[g] = m_scr[g] + jnp.log(acc[:, V_HEAD:2 * V_HEAD])

        if rider.finish is not None:
            halfway = rider.mid is not None and heads // G >= 2
            if halfway:
                @pl.when((h == heads // G // 2) & (p == 0))
                def _():
                    rider.mid(r_in, r_out, r_sems)

            @pl.when((h == heads // G - 1) & (p == n_pairs - 1))
            def _():
                if halfway:
                    rider.finish(r_in, r_out, r_sems)
                else:
                    _rider_end(rider, r_in, r_out, r_sems)

    res = pl.pallas_call(
        body, name="mla_attn_fwd",
        grid_spec=pltpu.PrefetchScalarGridSpec(
            num_scalar_prefetch=2, grid=(heads // G, n_pairs),
            in_specs=[pl.BlockSpec((t, G * QK_PAD), lambda h, p, qt, kt: (qt[p], h)),
                      pl.BlockSpec((t, G * QK_PAD), lambda h, p, qt, kt: (kt[p], h)),
                      pl.BlockSpec((t, LANES), lambda h, p, qt, kt: (kt[p], 0))] + [ANY] * n_ri,
            out_specs=[pl.BlockSpec((t, G * V_HEAD), lambda h, p, qt, kt: (qt[p], h)),
                       pl.BlockSpec((G, t, LANES), lambda h, p, qt, kt: (h, qt[p], 0))] + [ANY] * n_ro,
            scratch_shapes=[pltpu.VMEM((G, t, LANES), F32), pltpu.VMEM((G, t, 2 * V_HEAD), F32)] + list(rider.sems)),
        out_shape=[_sds((S, heads * V_HEAD), BF16), _sds((heads, S, LANES), F32)] + list(rider.out_shape),
        input_output_aliases={5 + i: 2 + o for i, o in rider.aliases.items()},
        compiler_params=_cparams("arbitrary", "arbitrary"))(q_tab, k_tab, q, kv, kr, *rider.operands)
    return res[0], res[1], res[2:]


def _attn_bwd_tri(q, kv, kr, o, do, lse, rope_tabs, heads, scale, rider=NO_RIDER):
    S = q.shape[0]
    t, nb = _attn_tiles(S)
    G = _heads_per_step(heads)
    q_tab, k_tab, n_pairs = _causal_pairs(nb, False)
    n_ri, n_ro = len(rider.operands), len(rider.out_shape)
    rep = t // LANES

    tabs = jnp.concatenate(rope_tabs, axis=1)

    def body(qt_ref, kt_ref, *refs):
        q_ref, kv_ref, kr_ref, o_ref, do_ref, lse_ref, tabs_ref = refs[:7]
        cos_ref, slo_ref, shi_ref = (tabs_ref.at[:, pl.ds(i * LANES, LANES)] for i in range(3))
        r_in = refs[7:7 + n_ri]
        dq_ref, dkv_ref, dkr_ref = refs[7 + n_ri:10 + n_ri]
        r_out = refs[10 + n_ri:10 + n_ri + n_ro]
        dq_scr, dk_scr, dv_scr, dkr_scr, delta_scr = refs[10 + n_ri + n_ro:15 + n_ri + n_ro]
        r_sems = refs[15 + n_ri + n_ro:]
        h, p = pl.program_id(0), pl.program_id(1)
        qi, ki = qt_ref[p], kt_ref[p]
        q_rows = pl.ds(pl.multiple_of(qi * t, t), t)
        k_rows = pl.ds(pl.multiple_of(ki * t, t), t)

        @pl.when(ki == 0)
        def _():
            for g in range(G):
                cols = slice(g * V_HEAD, (g + 1) * V_HEAD)
                d = jnp.sum(do_ref[:, cols].astype(F32) * o_ref[:, cols].astype(F32), axis=-1, keepdims=True)
                delta_scr[g, q_rows, :] = jnp.broadcast_to(d, (t, LANES))

        if rider.start is not None:
            @pl.when((h == 0) & (p == 0))
            def _():
                rider.start(r_in, r_out, r_sems)

        @pl.when(p == 0)
        def _():
            dq_scr[...] = jnp.zeros_like(dq_scr)

        @pl.when((h == 0) & (p == 0))
        def _():
            dkr_scr[...] = jnp.zeros_like(dkr_scr)

        @pl.when(qi == ki)
        def _():
            dk_scr[...] = jnp.zeros_like(dk_scr)
            dv_scr[...] = jnp.zeros_like(dv_scr)

        def step(diagonal):
            for g in range(G):
                qv = q_ref[:, g * QK_PAD:(g + 1) * QK_PAD]
                kcat = jnp.concatenate([kv_ref[:, g * QK_PAD:g * QK_PAD + QK_NOPE], kr_ref[...]], axis=1)
                s = lax.dot_general(qv, kcat, DIMS_NT, preferred_element_type=F32) * scale
                pr = jnp.exp(s - jnp.tile(lse_ref[g], (1, rep)))
                if diagonal:
                    pr = jnp.where(_causal_mask(t), pr, 0.0)
                dov = do_ref[:, g * V_HEAD:(g + 1) * V_HEAD]
                dv_scr[g] += lax.dot_general(pr.astype(BF16), dov, DIMS_TN, preferred_element_type=F32)
                dp = lax.dot_general(dov, kv_ref[:, g * QK_PAD + QK_NOPE:(g + 1) * QK_PAD], DIMS_NT,
                                     preferred_element_type=F32)
                ds = (pr * (dp - jnp.tile(delta_scr[g, q_rows, :], (1, rep))) * scale).astype(BF16)
                dk_scr[g] += lax.dot_general(ds, qv, DIMS_TN, preferred_element_type=F32)
                dq_scr[q_rows, g * QK_PAD:(g + 1) * QK_PAD] += lax.dot_general(ds, kcat, DIMS_NN,
                                                                               preferred_element_type=F32)

        @pl.when(qi > ki)
        def _():
            step(False)

        @pl.when(qi == ki)
        def _():
            step(True)
            for g in range(G):
                dqv = dq_scr[q_rows, g * QK_PAD:(g + 1) * QK_PAD]
                dq_ref[q_rows, g * QK_PAD:(g + 1) * QK_PAD] = jnp.concatenate(
                    [dqv[:, 0:QK_NOPE], _rope_t(dqv[:, QK_NOPE:QK_PAD], cos_ref[...], slo_ref[...], shi_ref[...])],
                    axis=1).astype(BF16)

        @pl.when(qi == nb - 1)
        def _():
            for g in range(G):
                dkv_ref[:, g * QK_PAD:(g + 1) * QK_PAD] = jnp.concatenate(
                    [dk_scr[g][:, 0:QK_NOPE], dv_scr[g]], axis=1).astype(BF16)
                dkr_scr[k_rows, :] += dk_scr[g][:, QK_NOPE:QK_PAD]

        @pl.when((h == heads // G - 1) & (p == n_pairs - 1))
        def _():
            dkr_ref[...] = dkr_scr[...]
            if rider.finish is not None:
                _rider_end(rider, r_in, r_out, r_sems)

    q_blk = lambda w: pl.BlockSpec((t, G * w), lambda h, p, qt, kt: (qt[p], h))
    stat = pl.BlockSpec((G, t, LANES), lambda h, p, qt, kt: (h, qt[p], 0))
    tab = pl.BlockSpec((t, LANES), lambda h, p, qt, kt: (kt[p], 0))
    res = pl.pallas_call(
        body, name="mla_attn_bwd",
        grid_spec=pltpu.PrefetchScalarGridSpec(
            num_scalar_prefetch=2, grid=(heads // G, n_pairs),
            in_specs=[q_blk(QK_PAD),
                      pl.BlockSpec((t, G * QK_PAD), lambda h, p, qt, kt: (kt[p], h)),
                      tab, q_blk(V_HEAD), q_blk(V_HEAD), stat,
                      pl.BlockSpec((t, 3 * LANES), lambda h, p, qt, kt: (kt[p], 0))] + [ANY] * n_ri,
            out_specs=[pl.BlockSpec((S, G * QK_PAD), lambda h, p, qt, kt: (0, h)),
                       pl.BlockSpec((t, G * QK_PAD), lambda h, p, qt, kt: (kt[p], h)),
                       pl.BlockSpec((S, LANES), lambda h, p, qt, kt: (0, 0))] + [ANY] * n_ro,
            scratch_shapes=[pltpu.VMEM((S, G * QK_PAD), F32), pltpu.VMEM((G, t, QK_PAD), F32),
                            pltpu.VMEM((G, t, V_HEAD), F32), pltpu.VMEM((S, LANES), F32),
                            pltpu.VMEM((G, S, LANES), F32)] + list(rider.sems)),
        out_shape=[_sds((S, heads * QK_PAD), BF16), _sds((S, heads * QK_PAD), BF16), _sds((S, LANES), F32)]
        + list(rider.out_shape),
        input_output_aliases={9 + i: 3 + o for i, o in rider.aliases.items()},
        compiler_params=_cparams("arbitrary", "arbitrary"))(q_tab, k_tab, q, kv, kr, o, do, lse, tabs,
                                                            *rider.operands)
    return res[0], res[1], res[2], res[3:]


def _shift_down(z, n, rows):
    return jnp.where(rows >= n, pltpu.roll(z, n, 0), 0.0)


def _shift_up(z, n, rows, S):
    return jnp.where(rows < S - n, pltpu.roll(z, S - n, 0), 0.0)


def _conv_specs(S, tc):
    strip = lambda p: pl.BlockSpec((None, S, tc), lambda j: (p, 0, j))
    return strip(0), strip(1), strip(2), pl.BlockSpec((3, tc), lambda j: (0, j))


def _conv_fwd(proj3, w):
    _, S, D = proj3.shape
    tc = LANES

    def body(b_ref, c_ref, u_ref, w_ref, out_ref):
        z = c_ref[...].astype(F32) * u_ref[...].astype(F32)
        rows = lax.broadcasted_iota(jnp.int32, (S, tc), 0)
        zc = w_ref[0:1, :] * _shift_down(z, 2, rows) + w_ref[1:2, :] * _shift_down(z, 1, rows) + w_ref[2:3, :] * z
        out_ref[...] = (b_ref[...].astype(F32) * zc).astype(BF16)

    return pl.pallas_call(
        body, name="conv_fwd", grid=(D // tc,), in_specs=list(_conv_specs(S, tc)),
        out_specs=pl.BlockSpec((S, tc), lambda j: (0, j)), out_shape=_sds((S, D), BF16),
        compiler_params=_cparams("parallel"))(proj3, proj3, proj3, w)


def _conv_bwd(dbz, proj3, w):
    _, S, D = proj3.shape
    tc = LANES

    def body(d_ref, b_ref, c_ref, u_ref, w_ref, dp_ref, dw_ref):
        cv, uv, dv = c_ref[...].astype(F32), u_ref[...].astype(F32), d_ref[...].astype(F32)
        z = cv * uv
        rows = lax.broadcasted_iota(jnp.int32, (S, tc), 0)
        z1, z2 = _shift_down(z, 1, rows), _shift_down(z, 2, rows)
        zc = w_ref[0:1, :] * z2 + w_ref[1:2, :] * z1 + w_ref[2:3, :] * z
        dp_ref[0] = (dv * zc).astype(BF16)
        dzc = dv * b_ref[...].astype(F32)
        dz = w_ref[2:3, :] * dzc + w_ref[1:2, :] * _shift_up(dzc, 1, rows, S) + w_ref[0:1, :] * _shift_up(dzc, 2, rows, S)
        dp_ref[1] = (dz * uv).astype(BF16)
        dp_ref[2] = (dz * cv).astype(BF16)
        dw_ref[0:1, :] = jnp.sum(dzc * z2, axis=0, keepdims=True)
        dw_ref[1:2, :] = jnp.sum(dzc * z1, axis=0, keepdims=True)
        dw_ref[2:3, :] = jnp.sum(dzc * z, axis=0, keepdims=True)

    sb, sc_, su, sw = _conv_specs(S, tc)
    return pl.pallas_call(
        body, name="conv_bwd", grid=(D // tc,),
        in_specs=[pl.BlockSpec((S, tc), lambda j: (0, j)), sb, sc_, su, sw],
        out_specs=[pl.BlockSpec((3, S, tc), lambda j: (0, 0, j)), pl.BlockSpec((3, tc), lambda j: (0, j))],
        out_shape=[_sds((3, S, D), BF16), _sds((3, D), F32)],
        compiler_params=_cparams("parallel"))(dbz, proj3, proj3, proj3, w)


def _silu(c_all):
    def body(c_ref, o_ref):
        cv = c_ref[...]
        o_ref[...] = cv * (1.0 / (1.0 + jnp.exp(-cv)))

    vm = pl.BlockSpec(memory_space=pltpu.VMEM)
    return pl.pallas_call(body, name="cond_silu", in_specs=[vm], out_specs=vm, out_shape=_sds(c_all.shape, F32))(c_all)


def _mod_fwd(cond, w_mod, b_cols):
    L, D, ncol = w_mod.shape
    B = cond.shape[0]
    tk, tn = min(512, D), min(1024, ncol)
    nk = D // tk

    def body(c_ref, w_ref, b_ref, out_ref, acc):
        kk = pl.program_id(2)
        part = lax.dot_general(c_ref[...].astype(BF16), w_ref[...].astype(BF16), DIMS_NN, preferred_element_type=F32)

        @pl.when(kk == 0)
        def _():
            acc[...] = part

        @pl.when(kk > 0)
        def _():
            acc[...] += part

        @pl.when(kk == nk - 1)
        def _():
            out_ref[...] = acc[...] + b_ref[...]

    return pl.pallas_call(
        body, name="mod_fwd", grid=(L, ncol // tn, nk),
        in_specs=[pl.BlockSpec((B, tk), lambda l, j, k: (0, k)),
                  pl.BlockSpec((None, tk, tn), lambda l, j, k: (l, k, j)),
                  pl.BlockSpec((None, 1, tn), lambda l, j, k: (l, 0, j))],
        out_specs=pl.BlockSpec((None, B, tn), lambda l, j, k: (l, 0, j)),
        out_shape=_sds((L, B, ncol), F32),
        scratch_shapes=[pltpu.VMEM((B, tn), F32)],
        compiler_params=_cparams("parallel", "parallel", "arbitrary"))(cond, w_mod, b_cols)


def _adamw_math(w, g, m, v):
    m = ADAM_B1 * m + (1.0 - ADAM_B1) * g
    v = ADAM_B2 * v + (1.0 - ADAM_B2) * (g * g)
    m_hat = m / (1.0 - ADAM_B1 ** ADAM_STEP)
    v_hat = v / (1.0 - ADAM_B2 ** ADAM_STEP)
    delta = -ADAM_LR * (m_hat / (jnp.sqrt(v_hat) + ADAM_EPS) + ADAM_WD * w)
    return delta, m, v


def _adamw(name, w, g, m, v, emit_grad=False):
    shape = w.shape
    cols = shape[-1] if w.ndim <= 3 else shape[-2] * shape[-1]
    rows = w.size // cols
    w2, g2, m2, v2 = (t.reshape(rows, cols) for t in (w, g, m, v))
    tr = _row_tile(rows, cols * 4, limit=2 * 1024 * 1024, mult=8)
    spec = pl.BlockSpec((tr, cols), lambda i: (i, 0))
    n_out = 4 if emit_grad else 3

    def body(w_ref, g_ref, m_ref, v_ref, d_ref, nm_ref, nv_ref, *rest):
        gv = g_ref[...]
        d, nm, nv = _adamw_math(w_ref[...], gv, m_ref[...], v_ref[...])
        d_ref[...] = d
        nm_ref[...] = nm
        nv_ref[...] = nv
        if emit_grad:
            rest[0][...] = gv

    outs = pl.pallas_call(body, name=name, grid=(rows // tr,), in_specs=[spec] * 4, out_specs=[spec] * n_out,
                          out_shape=[_sds((rows, cols), F32)] * n_out, compiler_params=_cparams("parallel"))(w2, g2, m2, v2)
    return tuple(t.reshape(shape) for t in outs)


def _adamw_mod(w, cond_t, dmod_cols, m, v, rider=NO_RIDER):
    L, D, ncol = w.shape
    B = cond_t.shape[1]
    tr, tc = min(256, D), min(1024, ncol)
    blk = pl.BlockSpec((None, tr, tc), lambda l, i, j: (l, i, j))
    grid = (L, D // tr, ncol // tc)
    n_ri, n_ro = len(rider.operands), len(rider.out_shape)

    def body(*refs):
        w_ref, ct_ref, dm_ref, m_ref, v_ref = refs[:5]
        r_in = refs[5:5 + n_ri]
        g_ref, d_ref, nm_ref, nv_ref = refs[5 + n_ri:9 + n_ri]
        r_out = refs[9 + n_ri:9 + n_ri + n_ro]
        r_sems = refs[9 + n_ri + n_ro:]
        ids = [pl.program_id(a) for a in range(3)]

        if rider.start is not None:
            @pl.when((ids[0] == 0) & (ids[1] == 0) & (ids[2] == 0))
            def _():
                rider.start(r_in, r_out, r_sems)

        g = lax.dot_general(ct_ref[...], dm_ref[...], DIMS_NN, precision=lax.Precision.HIGHEST,
                            preferred_element_type=F32)
        d, nm, nv = _adamw_math(w_ref[...], g, m_ref[...], v_ref[...])
        g_ref[...] = g
        d_ref[...] = d
        nm_ref[...] = nm
        nv_ref[...] = nv

        if rider.finish is not None:
            @pl.when((ids[0] == grid[0] - 1) & (ids[1] == grid[1] - 1) & (ids[2] == grid[2] - 1))
            def _():
                _rider_end(rider, r_in, r_out, r_sems)

    hosted = rider.start is not None
    res = pl.pallas_call(
        body, name="adamw_w_mod", grid=grid,
        in_specs=[blk, pl.BlockSpec((tr, B), lambda l, i, j: (i, 0)),
                  pl.BlockSpec((None, B, tc), lambda l, i, j: (l, 0, j)), blk, blk] + [ANY] * n_ri,
        out_specs=[blk] * 4 + [ANY] * n_ro, out_shape=[_sds((L, D, ncol), F32)] * 4 + list(rider.out_shape),
        scratch_shapes=list(rider.sems), input_output_aliases={5 + i: 4 + o for i, o in rider.aliases.items()},
        compiler_params=_cparams(*(("arbitrary",) * 3 if hosted else ("parallel",) * 3)))(
            w, cond_t, dmod_cols, m, v, *rider.operands)
    return (*res[:4], res[4:])


def _cast_into_full(name, ws, kinds, k_idx, rider=NO_RIDER):
    L, R, C = ws[0].shape
    assert all(w.shape == (L, R, C) for w in ws)
    n = len(ws)
    Rh = R // 2
    tr = _row_tile(Rh, C * 4)
    grid = (L, 2, Rh // tr)
    out_shape, out_specs = [], []
    for kind in kinds:
        if kind == "row":
            out_shape.append(_sds((L, N_CHIPS, 2, Rh, C), BF16))
            out_specs.append(pl.BlockSpec((None, None, None, tr, C), lambda l, h, i, k_ref: (l, k_ref[0], h, i, 0)))
        else:
            out_shape.append(_sds((L, 2, Rh, N_CHIPS * C), BF16))
            out_specs.append(pl.BlockSpec((None, None, tr, C), lambda l, h, i, k_ref: (l, h, i, k_ref[0])))
    n_ri, n_ro = len(rider.operands), len(rider.out_shape)

    def body(k_ref, *refs):
        r_in = refs[n:n + n_ri]
        r_out = refs[2 * n + n_ri:2 * n + n_ri + n_ro]
        r_sems = refs[2 * n + n_ri + n_ro:]
        ids = [pl.program_id(a) for a in range(3)]
        if rider.start is not None:
            @pl.when((ids[0] == 0) & (ids[1] == 0) & (ids[2] == 0))
            def _():
                rider.start(r_in, r_out, r_sems)
        for a in range(n):
            refs[n + n_ri + a][...] = refs[a][...].astype(BF16)
        if rider.finish is not None:
            @pl.when((ids[0] == grid[0] - 1) & (ids[1] == grid[1] - 1) & (ids[2] == grid[2] - 1))
            def _():
                _rider_end(rider, r_in, r_out, r_sems)

    hosted = rider.start is not None
    res = pl.pallas_call(
        body, name=name,
        grid_spec=pltpu.PrefetchScalarGridSpec(
            num_scalar_prefetch=1, grid=grid,
            in_specs=[pl.BlockSpec((None, None, tr, C), lambda l, h, i, k_ref: (l, h, i, 0))] * n + [ANY] * n_ri,
            out_specs=out_specs + [ANY] * n_ro, scratch_shapes=list(rider.sems)),
        out_shape=out_shape + list(rider.out_shape),
        input_output_aliases={1 + n + i: n + o for i, o in rider.aliases.items()},
        compiler_params=_cparams(*(("arbitrary",) * 3 if hosted else ("parallel",) * 3)))(
            k_idx, *[w.reshape(L, 2, Rh, C) for w in ws], *rider.operands)
    return res[:n], res[n:]


def _pair_sum(name, g5, ra, c_idx):
    L, A, _, Rh, Cc = g5.shape
    tr = _row_tile(Rh, Cc * 4)

    def body(c_ref, g_ref, r_ref, o_ref):
        o_ref[...] = (g_ref[...].astype(F32) + r_ref[...].astype(F32)).astype(BF16)

    blk = pl.BlockSpec((None, None, tr, Cc), lambda l, a, i, c_ref: (l, a, i, 0))
    return pl.pallas_call(
        body, name=name,
        grid_spec=pltpu.PrefetchScalarGridSpec(
            num_scalar_prefetch=1, grid=(L, A, Rh // tr),
            in_specs=[pl.BlockSpec((None, None, None, tr, Cc), lambda l, a, i, c_ref: (l, a, c_ref[0], i, 0)), blk],
            out_specs=blk),
        out_shape=_sds((L, A, Rh, Cc), BF16),
        compiler_params=_cparams("parallel", "parallel", "parallel"))(c_idx, g5, ra)


def _chip_sum(name, p, rb, kc_idx, kind, layer=0, n_layers=1, prev=None):
    _, A, Rh, Cc = p.shape
    C = rb.shape[-1]
    tr = _row_tile(Rh, C * 4)
    if kind == "row":
        own = pl.BlockSpec((None, None, tr, C), lambda i, kc: (0, kc[0], i, 0))
    else:
        own = pl.BlockSpec((None, None, tr, C), lambda i, kc: (0, 0, i, kc[0]))
    peer = lambda j: pl.BlockSpec((None, None, tr, C), lambda i, kc: (j, 0, i, 0))

    def body(kc_ref, p_ref, r0_ref, r1_ref, r2_ref, *rest):
        o_ref = rest[-1]
        o_ref[...] = ((p_ref[...].astype(F32) + r0_ref[...].astype(F32)) + r1_ref[...].astype(F32)) + r2_ref[...].astype(F32)

    operands = [kc_idx, p, rb, rb, rb] + ([prev] if prev is not None else [])
    return pl.pallas_call(
        body, name=name,
        grid_spec=pltpu.PrefetchScalarGridSpec(
            num_scalar_prefetch=1, grid=(Rh // tr,),
            in_specs=[own, peer(0), peer(1), peer(2)] + ([ANY] if prev is not None else []),
            out_specs=pl.BlockSpec((None, None, tr, C), lambda i, kc: (layer, kc[1], i, 0))),
        out_shape=_sds((n_layers, 2, Rh, C), F32),
        input_output_aliases={5: 0} if prev is not None else {},
        compiler_params=_cparams("parallel"))(*operands)


def _mesh_place():
    x, y, c = lax.axis_index("x"), lax.axis_index("y"), lax.axis_index("c")
    chips = [(1 - x, y), (x, 1 - y), (1 - x, 1 - y)]
    return x, y, c, chips


def _remote(src, dst, send_sem, recv_sem, to):
    return pltpu.make_async_remote_copy(src_ref=src, dst_ref=dst, send_sem=send_sem, recv_sem=recv_sem,
                                        device_id=to, device_id_type=MESH_ID)


def _small_allgather(name, v, with_sum=False, rider=NO_RIDER):
    R, N = v.shape
    n_ri, n_ro, n_own = len(rider.operands), len(rider.out_shape), 2 if with_sum else 1

    def body(*refs):
        r_in = refs[1:1 + n_ri]
        r_out = refs[1 + n_ri + n_own:1 + n_ri + n_own + n_ro]
        r_sems = refs[1 + n_ri + n_own + n_ro + 3:]
        own = (refs[0],) + refs[1 + n_ri:1 + n_ri + n_own] + refs[1 + n_ri + n_own + n_ro:1 + n_ri + n_own + n_ro + 3]
        if with_sum:
            x_ref, out_ref, sum_ref, send_sems, recv_sems, local_sem = own
        else:
            x_ref, out_ref, send_sems, recv_sems, local_sem = own
        if rider.start is not None:
            rider.start(r_in, r_out, r_sems)
        x, y, c, chips = _mesh_place()
        me, sibling = (x, y, c), (x, y, 1 - c)

        def rows(px, py, pc):
            return out_ref.at[pl.ds((4 * px + 2 * py + pc) * R, R), :]

        def copy(k, block, to, src=None):
            return _remote(rows(*block) if src is None else src, rows(*block), send_sems.at[k], recv_sems.at[k], to)

        mine = pltpu.make_async_copy(x_ref, rows(*me), local_sem)
        mine.start()
        first = [copy(0, me, sibling, src=x_ref)]
        first += [copy(1 + j, me, (*chip, c), src=x_ref) for j, chip in enumerate(chips)]
        for cp in first:
            cp.start()
        passed = [copy(4 + j, (*chip, c), sibling) for j, chip in enumerate(chips)]
        for j, chip in enumerate(chips):
            copy(1 + j, (*chip, c), me).wait_recv()
            passed[j].start()
        copy(0, sibling, me).wait_recv()
        for j, chip in enumerate(chips):
            copy(4 + j, (*chip, 1 - c), me).wait_recv()
        for cp in first + passed:
            cp.wait_send()
        mine.wait()
        if with_sum:
            total = out_ref[0:R, :]
            for p in range(1, 8):
                total = total + out_ref[p * R:(p + 1) * R, :]
            sum_ref[...] = total
        if rider.finish is not None:
            _rider_end(rider, r_in, r_out, r_sems)

    vm = pl.BlockSpec(memory_space=pltpu.VMEM)
    out_shape = [_sds((8 * R, N), F32)] + ([_sds((R, N), F32)] if with_sum else [])
    res = pl.pallas_call(
        body, name=name, out_shape=out_shape + list(rider.out_shape), in_specs=[vm] + [ANY] * n_ri,
        out_specs=[vm] * n_own + [ANY] * n_ro,
        scratch_shapes=[pltpu.SemaphoreType.DMA((7,)), pltpu.SemaphoreType.DMA((7,)), pltpu.SemaphoreType.DMA]
        + list(rider.sems),
        input_output_aliases={1 + i: n_own + o for i, o in rider.aliases.items()},
        compiler_params=pltpu.CompilerParams(vmem_limit_bytes=VMEM_LIMIT_BYTES))(v, *rider.operands)
    if rider.start is not None:
        return (*res[:n_own], res[n_own:])
    return res if with_sum else res[0]


def _full_place(ref, kind, C, kk, half, layer=None):
    lead = slice(None) if layer is None else pl.ds(layer, 1)
    if kind == "row":
        return ref.at[lead, kk, half]
    return ref.at[lead, half, :, pl.ds(pl.multiple_of(kk * C, LANES), C)]


def _gather_rider(fulls, kinds, shard_cols, layers=None, peers=(0, 1, 2)):
    n = len(fulls)
    layers = layers or [None] * n
    rows = [f.shape[3] if kind == "row" else f.shape[2] for f, kind in zip(fulls, kinds)]
    n_chunks = 2 if all(r % 32 == 0 for r in rows) else 1

    def copies(outs, sems):
        x, y, c, chips = _mesh_place()
        k = 2 * x + y

        def place(a, kk, half, ch):
            rc = rows[a] // n_chunks
            return _full_place(outs[a], kinds[a], shard_cols[a], kk, half, layers[a]).at[:, pl.ds(ch * rc, rc), :]

        def copy(a, j, ch, ref, to):
            s = 6 * (n_chunks * a + ch) + j
            return _remote(ref, ref, sems[0].at[s], sems[1].at[s], to)

        return (x, y, c), [(j, chip) for j, chip in enumerate(chips) if j in peers], k, place, copy

    def start(_, outs, sems):
        (x, y, c), chips, k, place, copy = copies(outs, sems)
        for ch in range(n_chunks):
            for j, chip in chips:
                for a in range(n):
                    copy(a, j, ch, place(a, k, c, ch), (*chip, c)).start()

    def pass_on(outs, sems, ch):
        (x, y, c), chips, k, place, copy = copies(outs, sems)
        for j, chip in chips:
            kj = 2 * chip[0] + chip[1]
            for a in range(n):
                copy(a, j, ch, place(a, kj, c, ch), (x, y, c)).wait_recv()
                copy(a, 3 + j, ch, place(a, kj, c, ch), (x, y, 1 - c)).start()

    def mid(_, outs, sems):
        pass_on(outs, sems, 0)

    def finish(_, outs, sems):
        pass_on(outs, sems, n_chunks - 1)
        (x, y, c), chips, k, place, copy = copies(outs, sems)
        for ch in range(n_chunks):
            for j, chip in chips:
                kj = 2 * chip[0] + chip[1]
                for a in range(n):
                    copy(a, 3 + j, ch, place(a, kj, 1 - c, ch), (x, y, c)).wait_recv()
        for ch in range(n_chunks):
            for j, chip in chips:
                kj = 2 * chip[0] + chip[1]
                for a in range(n):
                    copy(a, j, ch, place(a, k, c, ch), (*chip, c)).wait_send()
                    copy(a, 3 + j, ch, place(a, kj, c, ch), (x, y, 1 - c)).wait_send()

    n_sems = 6 * n * n_chunks
    return Rider(tuple(fulls), tuple(_sds(f.shape, BF16) for f in fulls), {a: a for a in range(n)},
                 (pltpu.SemaphoreType.DMA((n_sems,)), pltpu.SemaphoreType.DMA((n_sems,))), start, finish,
                 mid if n_chunks == 2 else None)


def _scatter_rider(ps, kinds, shard_cols, peers=(0, 1, 2), into=None):
    n = len(ps)

    def copies(ins, outs, sems):
        x, y, c, chips = _mesh_place()
        cps = []
        for j, chip in enumerate(chips):
            if j not in peers:
                continue
            kj = 2 * chip[0] + chip[1]
            for a in range(n):
                C = shard_cols[a]
                src = ins[a].at[:, kj] if kinds[a] == "row" else ins[a].at[:, 0, :, pl.ds(pl.multiple_of(kj * C, LANES), C)]
                cps.append(_remote(src, outs[a].at[j], sems[0].at[3 * a + j], sems[1].at[3 * a + j], (*chip, c)))
        return cps

    def start(ins, outs, sems):
        for cp in copies(ins, outs, sems):
            cp.start()

    def finish(ins, outs, sems):
        cps = copies(ins, outs, sems)
        for cp in cps:
            cp.wait_recv()
        for cp in cps:
            cp.wait_send()

    out_shape = tuple(_sds((3, p.shape[0], p.shape[2], C), BF16) for p, C in zip(ps, shard_cols))
    aliases = {n + a: a for a in range(n)} if into is not None else {}
    return Rider(tuple(ps) + tuple(into or ()), out_shape, aliases,
                 (pltpu.SemaphoreType.DMA((3 * n,)), pltpu.SemaphoreType.DMA((3 * n,))), start, finish)


def _run_rider(name, rider):
    n_in, n_out = len(rider.operands), len(rider.out_shape)

    def body(*refs):
        ins, outs, sems = refs[:n_in], refs[n_in:n_in + n_out], refs[n_in + n_out:]
        rider.start(ins, outs, sems)
        _rider_end(rider, ins, outs, sems)

    return pl.pallas_call(
        body, name=name, out_shape=list(rider.out_shape), in_specs=[ANY] * n_in, out_specs=[ANY] * n_out,
        input_output_aliases=dict(rider.aliases), scratch_shapes=list(rider.sems),
        compiler_params=pltpu.CompilerParams(vmem_limit_bytes=VMEM_LIMIT_BYTES))(*rider.operands)


def _exchange_rider(g5s):
    n = len(g5s)

    def copies(ins, outs, sems):
        x, y, c, _ = _mesh_place()
        return [_remote(ins[a].at[:, :, 1 - c], outs[a], sems[0].at[a], sems[1].at[a], (x, y, 1 - c)) for a in range(n)]

    def start(ins, outs, sems):
        for cp in copies(ins, outs, sems):
            cp.start()

    def finish(ins, outs, sems):
        cps = copies(ins, outs, sems)
        for cp in cps:
            cp.wait_recv()
        for cp in cps:
            cp.wait_send()

    out_shape = tuple(_sds((g.shape[0], g.shape[1], g.shape[3], g.shape[4]), BF16) for g in g5s)
    return Rider(tuple(g5s), out_shape, {}, (pltpu.SemaphoreType.DMA((n,)), pltpu.SemaphoreType.DMA((n,))), start, finish)


def _share_rider(fs):
    n = len(fs)

    def start(_, outs, sems):
        x, y, c, _p = _mesh_place()
        for a in range(n):
            mine = outs[a].at[:, c]
            _remote(mine, mine, sems[0].at[a], sems[1].at[a], (x, y, 1 - c)).start()

    def finish(_, outs, sems):
        x, y, c, _p = _mesh_place()
        for a in range(n):
            theirs = outs[a].at[:, 1 - c]
            _remote(theirs, theirs, sems[0].at[a], sems[1].at[a], (x, y, c)).wait_recv()
        for a in range(n):
            mine = outs[a].at[:, c]
            _remote(mine, mine, sems[0].at[a], sems[1].at[a], (x, y, 1 - c)).wait_send()

    return Rider(tuple(fs), tuple(_sds(f.shape, F32) for f in fs), {a: a for a in range(n)},
                 (pltpu.SemaphoreType.DMA((n,)), pltpu.SemaphoreType.DMA((n,))), start, finish)


def _both_riders(r1, r2):
    ni, no, ns = len(r1.operands), len(r1.out_shape), len(r1.sems)
    aliases = dict(r1.aliases)
    aliases.update({ni + i: no + o for i, o in r2.aliases.items()})

    def start(ins, outs, sems):
        r1.start(ins[:ni], outs[:no], sems[:ns])
        r2.start(ins[ni:], outs[no:], sems[ns:])

    def finish(ins, outs, sems):
        _rider_end(r1, ins[:ni], outs[:no], sems[:ns])
        _rider_end(r2, ins[ni:], outs[no:], sems[ns:])

    return Rider(r1.operands + r2.operands, r1.out_shape + r2.out_shape, aliases, r1.sems + r2.sems, start, finish)


def _pack_rows(parts, lane_mult=1024):
    flat = jnp.concatenate([p.reshape(-1).astype(F32) for p in parts])
    n = -(-flat.shape[0] // (8 * lane_mult)) * lane_mult
    return jnp.pad(flat, (0, 8 * n - flat.shape[0])).reshape(8, n)


def _relu2(acc):
    r = jnp.maximum(acc, 0.0)
    return r, r * r


def _times_2r(acc, r):
    return (acc * (2.0 * r.astype(F32)),)


def kernel(x, c, positions, w_mod, b_mod, norm_g, mla_w_in, mla_g_q, mla_g_kv, mla_w_uq, mla_w_ukv, mla_w_o, conv_w_in, conv_w, conv_w_out, mlp_w_up, mlp_w_down, loss_target, m_w_mod, m_b_mod, m_norm_g, m_mla_w_in, m_mla_g_q, m_mla_g_kv, m_mla_w_uq, m_mla_w_ukv, m_mla_w_o, m_conv_w_in, m_conv_w, m_conv_w_out, m_mlp_w_up, m_mlp_w_down, v_w_mod, v_b_mod, v_norm_g, v_mla_w_in, v_mla_g_q, v_mla_g_kv, v_mla_w_uq, v_mla_w_ukv, v_mla_w_o, v_conv_w_in, v_conv_w, v_conv_w_out, v_mlp_w_up, v_mlp_w_down):
    S, D = x.shape[1], x.shape[2]
    Dq = D // N_CHIPS
    ncol = w_mod.shape[2]
    n_mod = N_CHIPS * ncol // D
    F = mlp_w_up.shape[2] * N_CHIPS
    lat_dim = mla_w_in.shape[2]
    rank = mla_g_q.shape[1]
    H = mla_w_uq.shape[2]
    d_qk = mla_w_uq.shape[3]
    assert mla_g_kv.shape[1] == rank and lat_dim == 2 * rank + QK_ROPE and d_qk == QK_NOPE + QK_ROPE
    assert mla_w_ukv.shape[3] == QK_NOPE + V_HEAD and x.shape[0] == 1 and n_mod == 6
    assert norm_g.shape[0] == 2 and mla_w_in.shape[0] == 1 and conv_w_in.shape[0] == 1
    lat_pad = 2 * rank + LANES
    scale = float(d_qk) ** -0.5

    xi, yi, ci = lax.axis_index("x"), lax.axis_index("y"), lax.axis_index("c")
    chip = 2 * xi + yi
    dev = 2 * chip + ci
    c_idx = jnp.reshape(ci, (1,)).astype(jnp.int32)
    k_idx = jnp.reshape(chip, (1,)).astype(jnp.int32)

    n1 = D + 2 * D + 3 * Dq
    g1 = _small_allgather("gather_small_inputs", _pack_rows([c, norm_g, conv_w])).reshape(8, -1)
    c_all = g1[:, :D]
    by_chip = g1[0::2]
    norm_full = jnp.concatenate([by_chip[kk, D:3 * D].reshape(2, 4, Dq) for kk in range(N_CHIPS)], axis=-1)
    convw_full = jnp.concatenate([by_chip[kk, 3 * D:n1].reshape(3, Dq) for kk in range(N_CHIPS)], axis=-1)

    b_cols = lax.dynamic_slice(b_mod, (0, chip * ncol), (2, ncol)).reshape(2, 1, ncol)
    cond_all = _silu(c_all)
    mod_cols = _mod_fwd(cond_all, w_mod, b_cols)
    g2 = _small_allgather("gather_mod", _pack_rows([mod_cols]))
    g2 = g2.reshape(8, -1)[0::2, :2 * 8 * ncol].reshape(N_CHIPS, 2, 8, ncol)
    mod_all = jnp.transpose(g2, (2, 1, 0, 3)).reshape(8, 2, n_mod * D)
    mod_me = lax.dynamic_index_in_dim(mod_all, dev, axis=0, keepdims=False)
    mods = [[mod_me[l, i * D:(i + 1) * D].reshape(1, D) for i in range(n_mod)] for l in range(2)]
    ng = [[norm_full[l, i].reshape(1, D) for i in range(4)] for l in range(2)]

    pos = positions[0].astype(F32)
    inv_freq = ROPE_THETA ** (-jnp.arange(0, QK_ROPE, 2, dtype=F32) / QK_ROPE)
    ang = pos[:, None] * inv_freq
    cos, sin = jnp.cos(ang), jnp.sin(ang)
    zero = jnp.zeros_like(cos)
    rope_tabs = (jnp.concatenate([cos, cos, zero, zero], axis=1),
                 jnp.concatenate([-sin, zero, zero, zero], axis=1),
                 jnp.concatenate([zero, sin, zero, zero], axis=1))

    weights = [("mla_w_in", mla_w_in, "row"), ("mla_w_uq", mla_w_uq.reshape(1, rank // N_CHIPS, H * d_qk), "row"),
               ("mla_w_ukv", mla_w_ukv.reshape(1, rank // N_CHIPS, H * QK_PAD), "row"), ("mla_w_o", mla_w_o, "row"),
               ("conv_w_in", conv_w_in, "col"), ("conv_w_out", conv_w_out, "row"),
               ("mlp_w_up", mlp_w_up, "col"), ("mlp_w_down", mlp_w_down, "row")]
    kinds = [k for _, _, k in weights]
    shard_shapes = [w.shape for _, w, _ in weights]
    shard_cols = [s[2] for s in shard_shapes]
    W_IN, W_UQ, W_UKV, W_O, W_CIN, W_COUT, W_UP, W_DOWN = range(8)
    mla_idx = [W_IN, W_UQ, W_UKV, W_O]
    casted = [_cast_into_full("cast_" + nm, [w], [kind], k_idx)[0][0] for nm, w, kind in weights[:W_UP]]

    def view(i, buf):
        L, R, C = shard_shapes[i]
        return buf.reshape((L, N_CHIPS * R, C) if kinds[i] == "row" else (L, R, N_CHIPS * C))

    NEIGHBOURS, DIAGONAL = (0, 1), (2,)

    def gather_of(bufs, idx, layers=None, peers=(0, 1, 2)):
        return _gather_rider(bufs, [kinds[i] for i in idx], [shard_cols[i] for i in idx], layers, peers)

    def scatter_of(ps, idx, peers=(0, 1, 2), into=None):
        return _scatter_rider(ps, [kinds[i] for i in idx], [shard_cols[i] for i in idx], peers, into)

    def halves(items):
        g5s = []
        for _, i, g in items:
            _, R, C = shard_shapes[i]
            g5s.append(g.reshape((1, N_CHIPS, 2, R // 2, C) if kinds[i] == "row" else (1, 1, 2, R // 2, N_CHIPS * C)))
        return g5s

    def pair_sums(items, g5s, ras):
        return [_pair_sum("pair_sum_" + nm, g5, ra, c_idx) for (nm, _, _), g5, ra in zip(items, g5s, ras)]

    first_idx = [W_IN, W_UQ, W_UKV]
    mlp_casted, got = _cast_into_full("cast_mlp_w", [mlp_w_up, mlp_w_down], [kinds[W_UP], kinds[W_DOWN]], k_idx,
                                      gather_of([casted[i] for i in first_idx], first_idx))
    casted += list(mlp_casted)
    w_in_p = jnp.pad(view(W_IN, got[0])[0], ((0, 0), (0, lat_pad - lat_dim)))
    w_q_p = jnp.pad(view(W_UQ, got[1])[0].reshape(rank, H, d_qk), ((0, 0), (0, 0), (0, QK_PAD - d_qk))).reshape(rank, H * QK_PAD)
    w_ukv = view(W_UKV, got[2])[0]
    HV = H * V_HEAD

    def layer_b(l, transposed):
        if transposed:
            return lambda tm, tn, tk: pl.BlockSpec((None, tn, tk), lambda i, j, k: (l, j, k))
        return lambda tm, tn, tk: pl.BlockSpec((None, tk, tn), lambda i, j, k: (l, k, j))

    def mlp_up(tag, l, h, w, rider=NO_RIDER):
        return _mm("mlp_up_" + tag, h, w, "nn", S, F, D, [_sds((S, F), BF16)] * 2, epilogue=_relu2,
                   b_spec=layer_b(l, False), rider=rider)

    def mlp_down(tag, l, a2, w, rider=NO_RIDER):
        return _mm("mlp_down_" + tag, a2, w, "nn", S, D, F, [_sds((S, D), BF16)], b_spec=layer_b(l, False), rider=rider)

    def mlp_bwd(tag, l, h, r, a2, dy, first=NO_RIDER, second_of=None):
        res = _mm("mlp_down_dx_" + tag, dy, w_down, "nt", S, F, D, [_sds((S, F), BF16)], epilogue=_times_2r,
                  b_spec=layer_b(l, True), rider=first,
                  extras=[(r, lambda tm, tn, tk: pl.BlockSpec((tm, tn), lambda i, j, k: (i, j)))])
        (da,), got_first = res if first.start is not None else (res, ())
        second = second_of(got_first) if second_of else NO_RIDER
        res = _mm("mlp_down_dw_" + tag, a2, dy, "tn", F, D, S, [_sds((F, D), BF16)], rider=second)
        (dw_down,), got_second = res if second_of else (res, ())
        (dh,) = _mm("mlp_up_dx_" + tag, da, w_up, "nt", S, D, F, [_sds((S, D), BF16)], b_spec=layer_b(l, True))
        (dw_up,) = _mm("mlp_up_dw_" + tag, h, da, "tn", D, F, S, [_sds((D, F), BF16)])
        return dh, dw_up, dw_down, got_first, got_second

    x0 = x[0]
    sh1, sc1, gt1, sh2, sc2, gt2 = mods[0]
    (h1,) = _fwd_boundary("fwd_boundary_0", x0, None, None, None, ng[0][0], sc1, sh1)
    (lat,) = _mm("mla_in", h1, w_in_p, "nn", S, lat_pad, D, [_sds((S, lat_pad), F32)], tn=lat_pad)
    cq, ckv, kr = _latent_fwd(lat, mla_g_q, mla_g_kv, rope_tabs, rank)

    def rope_q(acc, cos_p, sin_lo, sin_hi):
        parts = []
        for hh in range(acc.shape[1] // QK_PAD):
            parts.append(acc[:, hh * QK_PAD:hh * QK_PAD + QK_NOPE])
            parts.append(_rope(acc[:, hh * QK_PAD + QK_NOPE:(hh + 1) * QK_PAD], cos_p, sin_lo, sin_hi))
        return (jnp.concatenate(parts, axis=1),)

    tab_extra = lambda tm, tn, tk: pl.BlockSpec((tm, LANES), lambda i, j, k: (i, 0))
    (q,), (o_buf,) = _mm("mla_q", cq, w_q_p, "nn", S, H * QK_PAD, rank, [_sds((S, H * QK_PAD), BF16)], epilogue=rope_q,
                         extras=[(t, tab_extra) for t in rope_tabs], tn=2 * QK_PAD,
                         rider=gather_of([casted[W_O]], [W_O]))
    w_o = view(W_O, o_buf)[0]
    (kv,) = _mm("mla_kv", ckv, w_ukv, "nn", S, H * QK_PAD, rank, [_sds((S, H * QK_PAD), BF16)])
    rest_idx = [W_UP, W_DOWN]
    o, lse, (up_buf, down_buf) = _attn_fwd_tri(
        q, kv, kr, H, scale, gather_of([casted[i] for i in rest_idx], rest_idx, [0, 0]))
    (y1,), (cout_buf,) = _mm("mla_out", o, w_o, "nn", S, D, HV, [_sds((S, D), BF16)],
                             rider=gather_of([casted[W_COUT]], [W_COUT]))
    x1, h2 = _fwd_boundary("fwd_boundary_1", x0, y1, gt1, ng[0][1], ng[0][2], sc2, sh2)
    (r2, a2), (cin_buf,) = mlp_up("0", 0, h2, view(W_UP, up_buf), gather_of([casted[W_CIN]], [W_CIN]))
    (y2,), (up_buf,) = mlp_down("0", 0, a2, view(W_DOWN, down_buf), gather_of([up_buf], [W_UP], [1]))
    w_cin, w_cout, w_up = view(W_CIN, cin_buf)[0], view(W_COUT, cout_buf)[0], view(W_UP, up_buf)

    sh1b, sc1b, gt1b, sh2b, sc2b, gt2b = mods[1]
    x2, h3 = _fwd_boundary("fwd_boundary_2", x1, y2, gt2, ng[0][3], ng[1][0], sc1b, sh1b)
    nD = lambda tn: D // tn
    (proj3,), (down_buf,) = _mm(
        "conv_in", h3, w_cin, "nn", S, 3 * D, D, [_sds((3, S, D), BF16)], tn=min(1024, D),
        rider=gather_of([down_buf], [W_DOWN], [1], NEIGHBOURS),
        out_specs=[lambda tm, tn, tk: pl.BlockSpec((None, tm, tn), lambda i, j, k: (j // nD(tn), i, j % nD(tn)))])
    bz = _conv_fwd(proj3, convw_full)
    (y3,) = _mm("conv_out", bz, w_cout, "nn", S, D, D, [_sds((S, D), BF16)])
    x3, h4 = _fwd_boundary("fwd_boundary_3", x2, y3, gt1b, ng[1][1], ng[1][2], sc2b, sh2b)
    (r4, a4), (down_buf,) = mlp_up("1", 1, h4, w_up, gather_of([down_buf], [W_DOWN], [1], DIAGONAL))
    w_down = view(W_DOWN, down_buf)
    (y4,) = mlp_down("1", 1, a4, w_down)

    dx4, dy4, sums_l, loss_acc = _loss_boundary("loss_boundary", x3, y4, gt2b, ng[1][3], loss_target[0])

    dh4, dw_up1, dw_down1, _, _ = mlp_bwd("1", 1, h4, r4, a4, dy4)
    dx3, dy3, sums_3, _ = _bwd_boundary("bwd_boundary_3", dx4, dh4, x3, y3, gt1b, ng[1][1], ng[1][2], sc2b)

    items = [("mlp_w_up_1", W_UP, dw_up1), ("mlp_w_down_1", W_DOWN, dw_down1)]
    g5s = halves(items)
    (dbz,), ras = _mm("conv_out_dx", dy3, w_cout, "nt", S, D, D, [_sds((S, D), BF16)], rider=_exchange_rider(g5s))
    ps_up1, ps_down1 = pair_sums(items, g5s, ras)
    (dw_cout,) = _mm("conv_out_dw", bz, dy3, "tn", D, D, S, [_sds((D, D), BF16)])
    dproj3, dconvw = _conv_bwd(dbz, proj3, convw_full)
    (dh3,), (rb_up1,) = _mm(
        "conv_in_dx", dproj3, w_cin, "nt", S, D, 3 * D, [_sds((S, D), BF16)], tk=D,
        rider=scatter_of([ps_up1], [W_UP], NEIGHBOURS),
        a_spec=lambda tm, tn, tk: pl.BlockSpec((None, tm, tk), lambda i, j, k: (k // (D // tk), i, k % (D // tk))))
    (dw_cin,), (rb_up1,) = _mm(
        "conv_in_dw", h3, dproj3, "tn", D, 3 * D, S, [_sds((D, 3 * D), BF16)], tn=min(1024, D),
        rider=scatter_of([ps_up1], [W_UP], DIAGONAL, [rb_up1]),
        b_spec=lambda tm, tn, tk: pl.BlockSpec((None, tk, tn), lambda i, j, k: (j // nD(tn), k, j % nD(tn))))
    dx2, dy2, sums_2, _ = _bwd_boundary("bwd_boundary_2", dx3, dh3, x2, y2, gt2, ng[0][3], ng[1][0], sc1b)

    items = [("conv_w_in", W_CIN, dw_cin), ("conv_w_out", W_COUT, dw_cout)]
    g5s = halves(items)
    dh2, dw_up0, dw_down0, (rb_down1, *ras), (rb_down1,) = mlp_bwd(
        "0", 0, h2, r2, a2, dy2,
        _both_riders(scatter_of([ps_down1], [W_DOWN], NEIGHBOURS), _exchange_rider(g5s)),
        lambda got: scatter_of([ps_down1], [W_DOWN], DIAGONAL, [got[0]]))
    ps_cin, ps_cout = pair_sums(items, g5s, ras)
    dx1, dy1, sums_1, _ = _bwd_boundary("bwd_boundary_1", dx2, dh2, x1, y1, gt1, ng[0][1], ng[0][2], sc2)

    items = [("mlp_w_up_0", W_UP, dw_up0), ("mlp_w_down_0", W_DOWN, dw_down0)]
    g5s = halves(items)
    (dw_o,), ras = _mm("mla_out_dw", o, dy1, "tn", HV, D, S, [_sds((HV, D), BF16)], rider=_exchange_rider(g5s))
    ps_up0, ps_down0 = pair_sums(items, g5s, ras)
    items = [(weights[W_O][0], W_O, dw_o)]
    g5s = halves(items)
    (do,), ras = _mm("mla_out_dx", dy1, w_o, "nt", S, HV, D, [_sds((S, HV), BF16)], rider=_exchange_rider(g5s))
    (ps_o,) = pair_sums(items, g5s, ras)
    dq, dkv, dkr, (rb_up0, rb_down0, rb_cin, rb_cout, rb_o) = _attn_bwd_tri(
        q, kv, kr, o, do, lse, rope_tabs, H, scale,
        scatter_of([ps_up0, ps_down0, ps_cin, ps_cout, ps_o], [W_UP, W_DOWN, W_CIN, W_COUT, W_O]))
    (dcq,) = _mm("mla_q_dx", dq, w_q_p, "nt", S, rank, H * QK_PAD, [_sds((S, rank), F32)])
    (dw_q_p,) = _mm("mla_q_dw", cq, dq, "tn", rank, H * QK_PAD, S, [_sds((rank, H * QK_PAD), BF16)])
    (dckv,) = _mm("mla_kv_dx", dkv, w_ukv, "nt", S, rank, H * QK_PAD, [_sds((S, rank), F32)])
    (dw_ukv,) = _mm("mla_kv_dw", ckv, dkv, "tn", rank, H * QK_PAD, S, [_sds((rank, H * QK_PAD), BF16)])
    dlat, sums_lat = _latent_bwd(lat, dcq, dckv, dkr, mla_g_q, mla_g_kv, rope_tabs, rank)
    late_idx = [W_IN, W_UQ, W_UKV]
    items_qkv = [(weights[W_UQ][0], W_UQ, dw_q_p.reshape(rank, H, QK_PAD)[:, :, :d_qk].reshape(rank, H * d_qk)),
                 (weights[W_UKV][0], W_UKV, dw_ukv)]
    g5s_qkv = halves(items_qkv)
    (dw_in_p,), ras = _mm("mla_in_dw", h1, dlat, "tn", D, lat_pad, S, [_sds((D, lat_pad), BF16)], tn=lat_pad,
                          rider=_exchange_rider(g5s_qkv))
    ps_qkv = pair_sums(items_qkv, g5s_qkv, ras)
    items_in = [(weights[W_IN][0], W_IN, dw_in_p[:, :lat_dim])]
    g5s_in = halves(items_in)
    (dh1,), carried = _mm("mla_in_dx", dlat, w_in_p, "nt", S, D, lat_pad, [_sds((S, D), BF16)],
                          rider=_both_riders(scatter_of(ps_qkv, [W_UQ, W_UKV]), _exchange_rider(g5s_in)))
    rbs_qkv, ras = carried[:2], carried[2:]
    ps_in = pair_sums(items_in, g5s_in, ras)
    ps_mla = ps_in + ps_qkv
    grad_x, sums_0, _ = _bwd_boundary("bwd_boundary_0", dx1, dh1, x0, None, None, None, ng[0][0], sc1)

    kc_idx = jnp.stack([chip, ci]).astype(jnp.int32)
    fs_rest = [_chip_sum("chip_sum_" + weights[i][0], p, rb, kc_idx, kinds[i])
               for i, p, rb in [(W_O, ps_o, rb_o), (W_CIN, ps_cin, rb_cin), (W_COUT, ps_cout, rb_cout)]]
    for i, (p1, r1), (p0, r0) in [(W_UP, (ps_up1, rb_up1), (ps_up0, rb_up0)), (W_DOWN, (ps_down1, rb_down1), (ps_down0, rb_down0))]:
        f = _chip_sum("chip_sum_" + weights[i][0] + "_1", p1, r1, kc_idx, kinds[i], layer=1, n_layers=2)
        fs_rest.append(_chip_sum("chip_sum_" + weights[i][0] + "_0", p0, r0, kc_idx, kinds[i], layer=0, n_layers=2, prev=f))

    dmod0 = [sums_0[0], sums_0[1], sums_1[3], sums_1[0], sums_1[1], sums_2[3]]
    dmod1 = [sums_2[0], sums_2[1], sums_3[3], sums_3[0], sums_3[1], sums_l[3]]
    dng0 = [sums_0[2], sums_1[4], sums_1[2], sums_2[4]]
    dng1 = [sums_2[2], sums_3[4], sums_3[2], sums_l[4]]
    small = _pack_rows(dmod0 + dmod1 + dng0 + dng1 + [sums_lat[0], sums_lat[1], dconvw, loss_acc[0, 0:1]],
                       lane_mult=LANES)
    gathered, total, carried = _small_allgather(
        "gather_small_grads", small, with_sum=True, rider=_both_riders(scatter_of(ps_in, [W_IN]), _share_rider(fs_rest)))
    rbs_mla, finals_rest = list(carried[:1]) + list(rbs_qkv), carried[1:]
    n_dm = 2 * n_mod * D
    dmod_all = gathered.reshape(8, -1)[:, :n_dm].reshape(8, 2, n_mod * D)
    total = total.reshape(-1)
    g_b_mod = total[:n_dm].reshape(2, n_mod * D)
    g_norm = lax.dynamic_slice(total[n_dm:n_dm + 8 * D].reshape(2, 4, D), (0, 0, chip * Dq), (2, 4, Dq))
    off = n_dm + 8 * D
    g_gq = total[off:off + rank].reshape(1, rank)
    g_gkv = total[off + rank:off + 2 * rank].reshape(1, rank)
    off += 2 * rank
    g_convw = lax.dynamic_slice(total[off:off + 3 * D].reshape(1, 3, D), (0, 0, chip * Dq), (1, 3, Dq))
    loss = total[off + 3 * D]

    dmod_cols = jnp.transpose(lax.dynamic_slice(dmod_all.reshape(8, 2, N_CHIPS, ncol), (0, 0, chip, 0), (8, 2, 1, ncol))
                              .reshape(8, 2, ncol), (1, 0, 2))
    g_w_mod, d_w_mod, nm_w_mod, nv_w_mod, _ = _adamw_mod(w_mod, cond_all.T, dmod_cols, m_w_mod, v_w_mod)
    fs_mla = [_chip_sum("chip_sum_" + weights[i][0], p, rb, kc_idx, kinds[i]) for i, p, rb in zip(late_idx, ps_mla, rbs_mla)]
    finals = list(_run_rider("grad_pair_share_mla", _share_rider(fs_mla))) + list(finals_rest)
    orig = [mla_w_in, mla_w_uq, mla_w_ukv, mla_w_o, conv_w_in, conv_w_out, mlp_w_up, mlp_w_down]
    big_grads = [f.reshape(w.shape) for f, w in zip(finals, orig)]

    names = ["b_mod", "norm_g", "mla_w_in", "mla_g_q", "mla_g_kv", "mla_w_uq", "mla_w_ukv", "mla_w_o",
             "conv_w_in", "conv_w", "conv_w_out", "mlp_w_up", "mlp_w_down"]
    ws = [b_mod, norm_g, mla_w_in, mla_g_q, mla_g_kv, mla_w_uq, mla_w_ukv, mla_w_o, conv_w_in, conv_w, conv_w_out,
          mlp_w_up, mlp_w_down]
    ms = [m_b_mod, m_norm_g, m_mla_w_in, m_mla_g_q, m_mla_g_kv, m_mla_w_uq, m_mla_w_ukv, m_mla_w_o, m_conv_w_in,
          m_conv_w, m_conv_w_out, m_mlp_w_up, m_mlp_w_down]
    vs = [v_b_mod, v_norm_g, v_mla_w_in, v_mla_g_q, v_mla_g_kv, v_mla_w_uq, v_mla_w_ukv, v_mla_w_o, v_conv_w_in,
          v_conv_w, v_conv_w_out, v_mlp_w_up, v_mlp_w_down]
    gs = [g_b_mod, g_norm, big_grads[0], g_gq, g_gkv, big_grads[1], big_grads[2], big_grads[3], big_grads[4],
          g_convw, big_grads[5], big_grads[6], big_grads[7]]
    grads, deltas, new_ms, new_vs = [g_w_mod], [d_w_mod], [nm_w_mod], [nv_w_mod]
    for nm, w, g, m, v in zip(names, ws, gs, ms, vs):
        big = any(g is b for b in big_grads)
        d, nm_, nv_, *g_out = _adamw("adamw_" + nm, w, g, m, v, emit_grad=big)
        g = g_out[0] if big else g
        grads.append(g)
        deltas.append(d)
        new_ms.append(nm_)
        new_vs.append(nv_)
    return (loss, grad_x[None], *grads, *deltas, *new_ms, *new_vs)
```

```python
from typing import NamedTuple

import jax
import jax.numpy as jnp
from jax import lax
from jax.experimental import pallas as pl
from jax.experimental.pallas import tpu as pltpu

F32 = jnp.float32
BF16 = jnp.bfloat16
NORM_EPS = 1e-6
ROPE_THETA = 10000.0
QK_NOPE = 128
QK_ROPE = 64
V_HEAD = 128
LANES = 128
QK_PAD = QK_NOPE + LANES
ADAM_LR, ADAM_B1, ADAM_B2, ADAM_EPS, ADAM_WD, ADAM_STEP = 0.001, 0.9, 0.999, 1e-08, 0.01, 10
VMEM_LIMIT_BYTES = 56 * 1024 * 1024
N_CHIPS = 4
MESH_ID = pl.DeviceIdType.MESH
ANY = pl.BlockSpec(memory_space=pl.ANY)
NEG_INF = float("-inf")

DIMS_NN = (((1,), (0,)), ((), ()))
DIMS_NT = (((1,), (1,)), ((), ()))
DIMS_TN = (((0,), (0,)), ((), ()))


def _cparams(*sem):
    return pltpu.CompilerParams(dimension_semantics=sem, vmem_limit_bytes=VMEM_LIMIT_BYTES)


def _row_tile(rows, row_bytes, limit=2 * 1024 * 1024, mult=16):
    if rows * row_bytes <= limit or rows % mult:
        return rows
    best = mult
    t = mult
    while t <= rows:
        if rows % t == 0 and t * row_bytes <= limit:
            best = t
        t += mult
    return best


def _rms(v):
    return lax.rsqrt(jnp.mean(v * v, axis=-1, keepdims=True) + NORM_EPS)


class Rider(NamedTuple):
    operands: tuple
    out_shape: tuple
    aliases: dict
    sems: tuple
    start: object
    finish: object
    mid: object = None


NO_RIDER = Rider((), (), {}, (), None, None)


def _rider_end(rider, r_in, r_out, r_sems):
    if rider.mid is not None:
        rider.mid(r_in, r_out, r_sems)
    rider.finish(r_in, r_out, r_sems)


def _mm(name, a, b, mode, M, N, K, outs, *, a_spec=None, b_spec=None, out_specs=None, epilogue=None,
        extras=(), rider=NO_RIDER, tm=1024, tn=1024, tk=4096):
    tm, tn, tk = min(tm, M), min(tn, N), min(tk, K)
    assert M % tm == 0 and N % tn == 0 and K % tk == 0, (name, M, N, K)
    nk = K // tk
    if a_spec is None:
        a_spec = {"nn": pl.BlockSpec((tm, tk), lambda i, j, k: (i, k)),
                  "nt": pl.BlockSpec((tm, tk), lambda i, j, k: (i, k)),
                  "tn": pl.BlockSpec((tk, tm), lambda i, j, k: (k, i))}[mode]
    else:
        a_spec = a_spec(tm, tn, tk)
    if b_spec is None:
        b_spec = {"nn": pl.BlockSpec((tk, tn), lambda i, j, k: (k, j)),
                  "nt": pl.BlockSpec((tn, tk), lambda i, j, k: (j, k)),
                  "tn": pl.BlockSpec((tk, tn), lambda i, j, k: (k, j))}[mode]
    else:
        b_spec = b_spec(tm, tn, tk)
    if out_specs is None:
        out_specs = [pl.BlockSpec((tm, tn), lambda i, j, k: (i, j)) for _ in outs]
    else:
        out_specs = [s(tm, tn, tk) for s in out_specs]
    dims = {"nn": DIMS_NN, "nt": DIMS_NT, "tn": DIMS_TN}[mode]
    ne, no = len(extras), len(outs)
    n_ri, n_ro = len(rider.operands), len(rider.out_shape)
    grid = (M // tm, N // tn, nk)

    def body(*refs):
        a_ref, b_ref = refs[0], refs[1]
        ex = refs[2:2 + ne]
        r_in = refs[2 + ne:2 + ne + n_ri]
        o = refs[2 + ne + n_ri:2 + ne + n_ri + no]
        r_out = refs[2 + ne + n_ri + no:2 + ne + n_ri + no + n_ro]
        scratch = refs[2 + ne + n_ri + no + n_ro:]
        r_sems = scratch[1:] if nk > 1 else scratch
        ii, jj, kk = pl.program_id(0), pl.program_id(1), pl.program_id(2)

        if rider.start is not None:
            @pl.when((ii == 0) & (jj == 0) & (kk == 0))
            def _():
                rider.start(r_in, r_out, r_sems)

        part = lax.dot_general(a_ref[...].astype(BF16), b_ref[...].astype(BF16), dims,
                               preferred_element_type=F32)

        def finish(total):
            vals = epilogue(total, *[e[...] for e in ex]) if epilogue is not None else (total,)
            for r, v in zip(o, vals):
                r[...] = v.astype(r.dtype)

        if nk == 1:
            finish(part)
        else:
            acc = scratch[0]

            @pl.when(kk == 0)
            def _():
                acc[...] = part

            @pl.when(kk > 0)
            def _():
                acc[...] += part

            @pl.when(kk == nk - 1)
            def _():
                finish(acc[...])

        if rider.finish is not None:
            steps = grid[0] * grid[1] * nk
            if rider.mid is not None and steps >= 4:
                @pl.when((ii * grid[1] + jj) * nk + kk == steps // 2)
                def _():
                    rider.mid(r_in, r_out, r_sems)

            @pl.when((ii == grid[0] - 1) & (jj == grid[1] - 1) & (kk == nk - 1))
            def _():
                if rider.mid is not None and steps < 4:
                    rider.mid(r_in, r_out, r_sems)
                rider.finish(r_in, r_out, r_sems)

    operands = [a, b] + [e[0] for e in extras] + list(rider.operands)
    in_specs = [a_spec, b_spec] + [e[1](tm, tn, tk) for e in extras] + [ANY] * n_ri
    hosted = rider.start is not None
    res = pl.pallas_call(
        body, name=name, grid=grid,
        in_specs=in_specs, out_specs=out_specs + [ANY] * n_ro, out_shape=list(outs) + list(rider.out_shape),
        scratch_shapes=([pltpu.VMEM((tm, tn), F32)] if nk > 1 else []) + list(rider.sems),
        input_output_aliases={2 + ne + i: no + r for i, r in rider.aliases.items()},
        compiler_params=_cparams(*(("arbitrary",) * 3 if hosted else ("parallel", "parallel", "arbitrary"))),
    )(*operands)
    return (res[:no], res[no:]) if hosted else res


def _sds(shape, dtype):
    return jax.ShapeDtypeStruct(tuple(shape), dtype)


def _rope(t, cos_p, sin_lo, sin_hi):
    return t * cos_p + pltpu.roll(t, LANES - QK_ROPE // 2, 1) * sin_lo + pltpu.roll(t, QK_ROPE // 2, 1) * sin_hi


def _rope_t(d, cos_p, sin_lo, sin_hi):
    return d * cos_p + pltpu.roll(d * sin_lo, QK_ROPE // 2, 1) + pltpu.roll(d * sin_hi, LANES - QK_ROPE // 2, 1)


def _vec_spec(d):
    return pl.BlockSpec((1, d), lambda i: (0, 0))


def _fwd_boundary(name, x_prev, y, gate, ng_post, ng_pre, sc, sh):
    S, D = x_prev.shape
    ts = min(256, S)
    has_y = y is not None
    row = pl.BlockSpec((ts, D), lambda i: (i, 0))

    def body(*refs):
        if has_y:
            x_ref, y_ref, g_ref, ngp_ref, ngn_ref, sc_ref, sh_ref, xo_ref, h_ref = refs
            yv = y_ref[...].astype(F32)
            xn = x_ref[...] + g_ref[...] * (yv * _rms(yv) * ngp_ref[...])
            xo_ref[...] = xn
        else:
            x_ref, ngn_ref, sc_ref, sh_ref, h_ref = refs
            xn = x_ref[...]
        hn = xn * _rms(xn) * ngn_ref[...]
        h_ref[...] = (hn * (1.0 + sc_ref[...]) + sh_ref[...]).astype(BF16)

    vec = _vec_spec(D)
    if has_y:
        operands = (x_prev, y, gate, ng_post, ng_pre, sc, sh)
        in_specs = [row, row, vec, vec, vec, vec, vec]
        out_shape = [_sds((S, D), F32), _sds((S, D), BF16)]
        out_specs = [row, row]
    else:
        operands = (x_prev, ng_pre, sc, sh)
        in_specs = [row, vec, vec, vec]
        out_shape = [_sds((S, D), BF16)]
        out_specs = [row]
    return pl.pallas_call(body, name=name, grid=(S // ts,), in_specs=in_specs, out_specs=out_specs,
                          out_shape=out_shape, compiler_params=_cparams("parallel"))(*operands)


def _acc_rows(sums_ref, rows):
    for r, v in rows:
        sums_ref[r:r + 1, :] += jnp.sum(v, axis=0, keepdims=True)


def _post_norm_bwd(dxt, yv, gate, ng_post, sums_ref, dy_ref):
    r1 = _rms(yv)
    yhat = yv * r1
    dn = dxt * gate
    u = dn * ng_post
    dy = r1 * (u - yhat * jnp.mean(u * yhat, axis=-1, keepdims=True))
    dy_ref[...] = dy.astype(dy_ref.dtype)
    _acc_rows(sums_ref, [(3, dxt * (yhat * ng_post)), (4, dn * yhat)])


def _loss_boundary(name, x_prev, y, gate, ng_post, target):
    S, D = x_prev.shape
    ts = min(256, S)
    row = pl.BlockSpec((ts, D), lambda i: (i, 0))
    vec = _vec_spec(D)

    def body(x_ref, y_ref, g_ref, ngp_ref, t_ref, dx_ref, dy_ref, sums_ref, loss_ref):
        @pl.when(pl.program_id(0) == 0)
        def _():
            sums_ref[...] = jnp.zeros_like(sums_ref)
            loss_ref[...] = jnp.zeros_like(loss_ref)

        yv = y_ref[...].astype(F32)
        xf = x_ref[...] + g_ref[...] * (yv * _rms(yv) * ngp_ref[...])
        err = xf - t_ref[...]
        loss_ref[...] += 0.5 * jnp.sum(jnp.mean(err * err, axis=-1, keepdims=True))
        dxt = err / D
        dx_ref[...] = dxt
        _post_norm_bwd(dxt, yv, g_ref[...], ngp_ref[...], sums_ref, dy_ref)

    return pl.pallas_call(
        body, name=name, grid=(S // ts,),
        in_specs=[row, row, vec, vec, row],
        out_specs=[row, row, pl.BlockSpec((8, D), lambda i: (0, 0)), pl.BlockSpec((8, LANES), lambda i: (0, 0))],
        out_shape=[_sds((S, D), F32), _sds((S, D), BF16), _sds((8, D), F32), _sds((8, LANES), F32)],
        compiler_params=_cparams("arbitrary"))(x_prev, y, gate, ng_post, target)


def _bwd_boundary(name, dx_new, dh, x_new, y, gate, ng_post, ng_pre, sc, rider=NO_RIDER):
    S, D = x_new.shape
    ts = min(256, S)
    has_y = y is not None
    row = pl.BlockSpec((ts, D), lambda i: (i, 0))
    vec = _vec_spec(D)
    n_in, n_out = (8, 3) if has_y else (5, 2)
    n_ri, n_ro = len(rider.operands), len(rider.out_shape)

    def body(*refs):
        r_in = refs[n_in:n_in + n_ri]
        r_out = refs[n_in + n_ri + n_out:n_in + n_ri + n_out + n_ro]
        r_sems = refs[n_in + n_ri + n_out + n_ro:]
        own = refs[:n_in] + refs[n_in + n_ri:n_in + n_ri + n_out]
        if has_y:
            dxn_ref, dh_ref, x_ref, y_ref, g_ref, ngp_ref, ngn_ref, sc_ref, dxo_ref, dy_ref, sums_ref = own
        else:
            dxn_ref, dh_ref, x_ref, ngn_ref, sc_ref, dxo_ref, sums_ref = own

        @pl.when(pl.program_id(0) == 0)
        def _():
            sums_ref[...] = jnp.zeros_like(sums_ref)
            if rider.start is not None:
                rider.start(r_in, r_out, r_sems)

        xv = x_ref[...]
        dhv = dh_ref[...].astype(F32)
        ngn = ngn_ref[...]
        r2 = _rms(xv)
        xhat = xv * r2
        dn_pre = dhv * (1.0 + sc_ref[...])
        u2 = dn_pre * ngn
        dxt = dxn_ref[...] + r2 * (u2 - xhat * jnp.mean(u2 * xhat, axis=-1, keepdims=True))
        dxo_ref[...] = dxt
        _acc_rows(sums_ref, [(0, dhv), (1, dhv * (xhat * ngn)), (2, dn_pre * xhat)])
        if has_y:
            _post_norm_bwd(dxt, y_ref[...].astype(F32), g_ref[...], ngp_ref[...], sums_ref, dy_ref)

        if rider.finish is not None:
            @pl.when(pl.program_id(0) == S // ts - 1)
            def _():
                _rider_end(rider, r_in, r_out, r_sems)

    sums_spec = pl.BlockSpec((8, D), lambda i: (0, 0))
    if has_y:
        operands = (dx_new, dh, x_new, y, gate, ng_post, ng_pre, sc)
        in_specs = [row, row, row, row, vec, vec, vec, vec]
        out_shape = [_sds((S, D), F32), _sds((S, D), BF16), _sds((8, D), F32)]
        out_specs = [row, row, sums_spec]
    else:
        operands = (dx_new, dh, x_new, ng_pre, sc)
        in_specs = [row, row, row, vec, vec]
        out_shape = [_sds((S, D), F32), _sds((8, D), F32)]
        out_specs = [row, sums_spec]
    res = pl.pallas_call(
        body, name=name, grid=(S // ts,), in_specs=in_specs + [ANY] * n_ri, out_specs=out_specs + [ANY] * n_ro,
        out_shape=out_shape + list(rider.out_shape), scratch_shapes=list(rider.sems),
        input_output_aliases={n_in + i: n_out + o for i, o in rider.aliases.items()},
        compiler_params=_cparams("arbitrary"))(*operands, *rider.operands)
    return (*res[:n_out], res[n_out:])


def _latent_fwd(lat, g_q, g_kv, rope_tabs, rank):
    S, W = lat.shape
    ts = min(256, S)
    tab = pl.BlockSpec((ts, LANES), lambda i: (i, 0))

    def body(lat_ref, gq_ref, gkv_ref, cos_ref, slo_ref, shi_ref, cq_ref, ckv_ref, kr_ref):
        lq = lat_ref[:, 0:rank]
        lkv = lat_ref[:, rank:2 * rank]
        cq_ref[...] = (lq * _rms(lq) * gq_ref[...]).astype(BF16)
        ckv_ref[...] = (lkv * _rms(lkv) * gkv_ref[...]).astype(BF16)
        kr_ref[...] = _rope(lat_ref[:, 2 * rank:W], cos_ref[...], slo_ref[...], shi_ref[...]).astype(BF16)

    return pl.pallas_call(
        body, name="mla_latent_fwd", grid=(S // ts,),
        in_specs=[pl.BlockSpec((ts, W), lambda i: (i, 0)), _vec_spec(rank), _vec_spec(rank), tab, tab, tab],
        out_specs=[pl.BlockSpec((ts, rank), lambda i: (i, 0)), pl.BlockSpec((ts, rank), lambda i: (i, 0)), tab],
        out_shape=[_sds((S, rank), BF16), _sds((S, rank), BF16), _sds((S, LANES), BF16)],
        compiler_params=_cparams("parallel"))(lat, g_q, g_kv, *rope_tabs)


def _latent_bwd(lat, dcq, dckv, dkr, g_q, g_kv, rope_tabs, rank):
    S, W = lat.shape
    ts = min(256, S)
    tab = pl.BlockSpec((ts, LANES), lambda i: (i, 0))
    half = pl.BlockSpec((ts, rank), lambda i: (i, 0))

    def body(lat_ref, dcq_ref, dckv_ref, dkr_ref, gq_ref, gkv_ref, cos_ref, slo_ref, shi_ref, dlat_ref, sums_ref):
        @pl.when(pl.program_id(0) == 0)
        def _():
            sums_ref[...] = jnp.zeros_like(sums_ref)

        def norm_bwd(v, dn, g, r):
            rr = _rms(v)
            vhat = v * rr
            u = dn * g
            sums_ref[r:r + 1, :] += jnp.sum(dn * vhat, axis=0, keepdims=True)
            return rr * (u - vhat * jnp.mean(u * vhat, axis=-1, keepdims=True))

        dlat_ref[:, 0:rank] = norm_bwd(lat_ref[:, 0:rank], dcq_ref[...], gq_ref[...], 0).astype(BF16)
        dlat_ref[:, rank:2 * rank] = norm_bwd(lat_ref[:, rank:2 * rank], dckv_ref[...], gkv_ref[...], 1).astype(BF16)
        dlat_ref[:, 2 * rank:W] = _rope_t(dkr_ref[...], cos_ref[...], slo_ref[...], shi_ref[...]).astype(BF16)

    return pl.pallas_call(
        body, name="mla_latent_bwd", grid=(S // ts,),
        in_specs=[pl.BlockSpec((ts, W), lambda i: (i, 0)), half, half, tab, _vec_spec(rank), _vec_spec(rank),
                  tab, tab, tab],
        out_specs=[pl.BlockSpec((ts, W), lambda i: (i, 0)), pl.BlockSpec((8, rank), lambda i: (0, 0))],
        out_shape=[_sds((S, W), BF16), _sds((8, rank), F32)],
        compiler_params=_cparams("arbitrary"))(lat, dcq, dckv, dkr, g_q, g_kv, *rope_tabs)


def _attn_tiles(S):
    t = min(512, S)
    return t, S // t


def _causal_mask(t):
    return lax.broadcasted_iota(jnp.int32, (t, t), 1) <= lax.broadcasted_iota(jnp.int32, (t, t), 0)


def _causal_pairs(nb, q_major):
    if q_major:
        pairs = [(qi, ki) for qi in range(nb) for ki in range(qi + 1)]
    else:
        pairs = [(qi, ki) for ki in range(nb) for qi in range(ki, nb)]
    return jnp.array([p[0] for p in pairs], jnp.int32), jnp.array([p[1] for p in pairs], jnp.int32), len(pairs)


def _heads_per_step(heads):
    return 2 if heads % 2 == 0 else 1


def _attn_fwd_tri(q, kv, kr, heads, scale, rider=NO_RIDER):
    S = q.shape[0]
    t, nb = _attn_tiles(S)
    G = _heads_per_step(heads)
    q_tab, k_tab, n_pairs = _causal_pairs(nb, True)
    n_ri, n_ro = len(rider.operands), len(rider.out_shape)

    def body(qt_ref, kt_ref, *refs):
        q_ref, kv_ref, kr_ref = refs[:3]
        r_in = refs[3:3 + n_ri]
        o_ref, lse_ref = refs[3 + n_ri:5 + n_ri]
        r_out = refs[5 + n_ri:5 + n_ri + n_ro]
        m_scr, acc_scr = refs[5 + n_ri + n_ro:7 + n_ri + n_ro]
        r_sems = refs[7 + n_ri + n_ro:]
        h, p = pl.program_id(0), pl.program_id(1)
        qi, ki = qt_ref[p], kt_ref[p]

        if rider.start is not None:
            @pl.when((h == 0) & (p == 0))
            def _():
                rider.start(r_in, r_out, r_sems)

        @pl.when(ki == 0)
        def _():
            m_scr[...] = jnp.full_like(m_scr, NEG_INF)
            acc_scr[...] = jnp.zeros_like(acc_scr)

        def step(diagonal):
            ones = jnp.ones((t, LANES), BF16)
            for g in range(G):
                kcat = jnp.concatenate([kv_ref[:, g * QK_PAD:g * QK_PAD + QK_NOPE], kr_ref[...]], axis=1)
                vext = jnp.concatenate([kv_ref[:, g * QK_PAD + QK_NOPE:(g + 1) * QK_PAD], ones], axis=1)
                s = lax.dot_general(q_ref[:, g * QK_PAD:(g + 1) * QK_PAD], kcat, DIMS_NT,
                                    preferred_element_type=F32) * scale
                if diagonal:
                    s = jnp.where(_causal_mask(t), s, NEG_INF)
                m_prev = m_scr[g]
                m_new = jnp.maximum(m_prev, jnp.max(s, axis=-1, keepdims=True))
                alpha = jnp.exp(m_prev - m_new)
                pr = jnp.exp(s - jnp.tile(m_new, (1, t // LANES)))
                acc_scr[g] = jnp.tile(alpha, (1, 2)) * acc_scr[g] + lax.dot_general(
                    pr.astype(BF16), vext, DIMS_NN, preferred_element_type=F32)
                m_scr[g] = m_new

        @pl.when(ki < qi)
        def _():
            step(False)

        @pl.when(ki == qi)
        def _():
            step(True)
            for g in range(G):
                acc = acc_scr[g]
                o_ref[:, g * V_HEAD:(g + 1) * V_HEAD] = (acc[:, 0:V_HEAD] / acc[:, V_HEAD:2 * V_HEAD]).astype(BF16)
                lse_ref[g] = m_scr[g] + jnp.log(acc[:, V_HEAD:2 * V_HEAD])

        if rider.finish is not None:
            halfway = rider.mid is not None and heads // G >= 2
            if halfway:
                @pl.when((h == heads // G // 2) & (p == 0))
                def _():
                    rider.mid(r_in, r_out, r_sems)

            @pl.when((h == heads // G - 1) & (p == n_pairs - 1))
            def _():
                if halfway:
                    rider.finish(r_in, r_out, r_sems)
                else:
                    _rider_end(rider, r_in, r_out, r_sems)

    res = pl.pallas_call(
        body, name="mla_attn_fwd",
        grid_spec=pltpu.PrefetchScalarGridSpec(
            num_scalar_prefetch=2, grid=(heads // G, n_pairs),
            in_specs=[pl.BlockSpec((t, G * QK_PAD), lambda h, p, qt, kt: (qt[p], h)),
                      pl.BlockSpec((t, G * QK_PAD), lambda h, p, qt, kt: (kt[p], h)),
                      pl.BlockSpec((t, LANES), lambda h, p, qt, kt: (kt[p], 0))] + [ANY] * n_ri,
            out_specs=[pl.BlockSpec((t, G * V_HEAD), lambda h, p, qt, kt: (qt[p], h)),
                       pl.BlockSpec((G, t, LANES), lambda h, p, qt, kt: (h, qt[p], 0))] + [ANY] * n_ro,
            scratch_shapes=[pltpu.VMEM((G, t, LANES), F32), pltpu.VMEM((G, t, 2 * V_HEAD), F32)] + list(rider.sems)),
        out_shape=[_sds((S, heads * V_HEAD), BF16), _sds((heads, S, LANES), F32)] + list(rider.out_shape),
        input_output_aliases={5 + i: 2 + o for i, o in rider.aliases.items()},
        compiler_params=_cparams("arbitrary", "arbitrary"))(q_tab, k_tab, q, kv, kr, *rider.operands)
    return res[0], res[1], res[2:]


def _attn_bwd_tri(q, kv, kr, o, do, lse, rope_tabs, heads, scale, rider=NO_RIDER):
    S = q.shape[0]
    t, nb = _attn_tiles(S)
    G = _heads_per_step(heads)
    q_tab, k_tab, n_pairs = _causal_pairs(nb, False)
    n_ri, n_ro = len(rider.operands), len(rider.out_shape)
    rep = t // LANES

    tabs = jnp.concatenate(rope_tabs, axis=1)

    def body(qt_ref, kt_ref, *refs):
        q_ref, kv_ref, kr_ref, o_ref, do_ref, lse_ref, tabs_ref = refs[:7]
        cos_ref, slo_ref, shi_ref = (tabs_ref.at[:, pl.ds(i * LANES, LANES)] for i in range(3))
        r_in = refs[7:7 + n_ri]
        dq_ref, dkv_ref, dkr_ref = refs[7 + n_ri:10 + n_ri]
        r_out = refs[10 + n_ri:10 + n_ri + n_ro]
        dq_scr, dk_scr, dv_scr, dkr_scr, delta_scr = refs[10 + n_ri + n_ro:15 + n_ri + n_ro]
        r_sems = refs[15 + n_ri + n_ro:]
        h, p = pl.program_id(0), pl.program_id(1)
        qi, ki = qt_ref[p], kt_ref[p]
        q_rows = pl.ds(pl.multiple_of(qi * t, t), t)
        k_rows = pl.ds(pl.multiple_of(ki * t, t), t)

        @pl.when(ki == 0)
        def _():
            for g in range(G):
                cols = slice(g * V_HEAD, (g + 1) * V_HEAD)
                d = jnp.sum(do_ref[:, cols].astype(F32) * o_ref[:, cols].astype(F32), axis=-1, keepdims=True)
                delta_scr[g, q_rows, :] = jnp.broadcast_to(d, (t, LANES))

        if rider.start is not None:
            @pl.when((h == 0) & (p == 0))
            def _():
                rider.start(r_in, r_out, r_sems)

        @pl.when(p == 0)
        def _():
            dq_scr[...] = jnp.zeros_like(dq_scr)

        @pl.when((h == 0) & (p == 0))
        def _():
            dkr_scr[...] = jnp.zeros_like(dkr_scr)

        @pl.when(qi == ki)
        def _():
            dk_scr[...] = jnp.zeros_like(dk_scr)
            dv_scr[...] = jnp.zeros_like(dv_scr)

        def step(diagonal):
            for g in range(G):
                qv = q_ref[:, g * QK_PAD:(g + 1) * QK_PAD]
                kcat = jnp.concatenate([kv_ref[:, g * QK_PAD:g * QK_PAD + QK_NOPE], kr_ref[...]], axis=1)
                s = lax.dot_general(qv, kcat, DIMS_NT, preferred_element_type=F32) * scale
                pr = jnp.exp(s - jnp.tile(lse_ref[g], (1, rep)))
                if diagonal:
                    pr = jnp.where(_causal_mask(t), pr, 0.0)
                dov = do_ref[:, g * V_HEAD:(g + 1) * V_HEAD]
                dv_scr[g] += lax.dot_general(pr.astype(BF16), dov, DIMS_TN, preferred_element_type=F32)
                dp = lax.dot_general(dov, kv_ref[:, g * QK_PAD + QK_NOPE:(g + 1) * QK_PAD], DIMS_NT,
                                     preferred_element_type=F32)
                ds = (pr * (dp - jnp.tile(delta_scr[g, q_rows, :], (1, rep))) * scale).astype(BF16)
                dk_scr[g] += lax.dot_general(ds, qv, DIMS_TN, preferred_element_type=F32)
                dq_scr[q_rows, g * QK_PAD:(g + 1) * QK_PAD] += lax.dot_general(ds, kcat, DIMS_NN,
                                                                               preferred_element_type=F32)

        @pl.when(qi > ki)
        def _():
            step(False)

        @pl.when(qi == ki)
        def _():
            step(True)
            for g in range(G):
                dqv = dq_scr[q_rows, g * QK_PAD:(g + 1) * QK_PAD]
                dq_ref[q_rows, g * QK_PAD:(g + 1) * QK_PAD] = jnp.concatenate(
                    [dqv[:, 0:QK_NOPE], _rope_t(dqv[:, QK_NOPE:QK_PAD], cos_ref[...], slo_ref[...], shi_ref[...])],
                    axis=1).astype(BF16)

        @pl.when(qi == nb - 1)
        def _():
            for g in range(G):
                dkv_ref[:, g * QK_PAD:(g + 1) * QK_PAD] = jnp.concatenate(
                    [dk_scr[g][:, 0:QK_NOPE], dv_scr[g]], axis=1).astype(BF16)
                dkr_scr[k_rows, :] += dk_scr[g][:, QK_NOPE:QK_PAD]

        @pl.when((h == heads // G - 1) & (p == n_pairs - 1))
        def _():
            dkr_ref[...] = dkr_scr[...]
            if rider.finish is not None:
                _rider_end(rider, r_in, r_out, r_sems)

    q_blk = lambda w: pl.BlockSpec((t, G * w), lambda h, p, qt, kt: (qt[p], h))
    stat = pl.BlockSpec((G, t, LANES), lambda h, p, qt, kt: (h, qt[p], 0))
    tab = pl.BlockSpec((t, LANES), lambda h, p, qt, kt: (kt[p], 0))
    res = pl.pallas_call(
        body, name="mla_attn_bwd",
        grid_spec=pltpu.PrefetchScalarGridSpec(
            num_scalar_prefetch=2, grid=(heads // G, n_pairs),
            in_specs=[q_blk(QK_PAD),
                      pl.BlockSpec((t, G * QK_PAD), lambda h, p, qt, kt: (kt[p], h)),
                      tab, q_blk(V_HEAD), q_blk(V_HEAD), stat,
                      pl.BlockSpec((t, 3 * LANES), lambda h, p, qt, kt: (kt[p], 0))] + [ANY] * n_ri,
            out_specs=[pl.BlockSpec((S, G * QK_PAD), lambda h, p, qt, kt: (0, h)),
                       pl.BlockSpec((t, G * QK_PAD), lambda h, p, qt, kt: (kt[p], h)),
                       pl.BlockSpec((S, LANES), lambda h, p, qt, kt: (0, 0))] + [ANY] * n_ro,
            scratch_shapes=[pltpu.VMEM((S, G * QK_PAD), F32), pltpu.VMEM((G, t, QK_PAD), F32),
                            pltpu.VMEM((G, t, V_HEAD), F32), pltpu.VMEM((S, LANES), F32),
                            pltpu.VMEM((G, S, LANES), F32)] + list(rider.sems)),
        out_shape=[_sds((S, heads * QK_PAD), BF16), _sds((S, heads * QK_PAD), BF16), _sds((S, LANES), F32)]
        + list(rider.out_shape),
        input_output_aliases={9 + i: 3 + o for i, o in rider.aliases.items()},
        compiler_params=_cparams("arbitrary", "arbitrary"))(q_tab, k_tab, q, kv, kr, o, do, lse, tabs,
                                                            *rider.operands)
    return res[0], res[1], res[2], res[3:]


def _shift_down(z, n, rows):
    return jnp.where(rows >= n, pltpu.roll(z, n, 0), 0.0)


def _shift_up(z, n, rows, S):
    return jnp.where(rows < S - n, pltpu.roll(z, S - n, 0), 0.0)


def _conv_specs(S, tc):
    strip = lambda p: pl.BlockSpec((None, S, tc), lambda j: (p, 0, j))
    return strip(0), strip(1), strip(2), pl.BlockSpec((3, tc), lambda j: (0, j))


def _conv_fwd(proj3, w):
    _, S, D = proj3.shape
    tc = LANES

    def body(b_ref, c_ref, u_ref, w_ref, out_ref):
        z = c_ref[...].astype(F32) * u_ref[...].astype(F32)
        rows = lax.broadcasted_iota(jnp.int32, (S, tc), 0)
        zc = w_ref[0:1, :] * _shift_down(z, 2, rows) + w_ref[1:2, :] * _shift_down(z, 1, rows) + w_ref[2:3, :] * z
        out_ref[...] = (b_ref[...].astype(F32) * zc).astype(BF16)

    return pl.pallas_call(
        body, name="conv_fwd", grid=(D // tc,), in_specs=list(_conv_specs(S, tc)),
        out_specs=pl.BlockSpec((S, tc), lambda j: (0, j)), out_shape=_sds((S, D), BF16),
        compiler_params=_cparams("parallel"))(proj3, proj3, proj3, w)


def _conv_bwd(dbz, proj3, w):
    _, S, D = proj3.shape
    tc = LANES

    def body(d_ref, b_ref, c_ref, u_ref, w_ref, dp_ref, dw_ref):
        cv, uv, dv = c_ref[...].astype(F32), u_ref[...].astype(F32), d_ref[...].astype(F32)
        z = cv * uv
        rows = lax.broadcasted_iota(jnp.int32, (S, tc), 0)
        z1, z2 = _shift_down(z, 1, rows), _shift_down(z, 2, rows)
        zc = w_ref[0:1, :] * z2 + w_ref[1:2, :] * z1 + w_ref[2:3, :] * z
        dp_ref[0] = (dv * zc).astype(BF16)
        dzc = dv * b_ref[...].astype(F32)
        dz = w_ref[2:3, :] * dzc + w_ref[1:2, :] * _shift_up(dzc, 1, rows, S) + w_ref[0:1, :] * _shift_up(dzc, 2, rows, S)
        dp_ref[1] = (dz * uv).astype(BF16)
        dp_ref[2] = (dz * cv).astype(BF16)
        dw_ref[0:1, :] = jnp.sum(dzc * z2, axis=0, keepdims=True)
        dw_ref[1:2, :] = jnp.sum(dzc * z1, axis=0, keepdims=True)
        dw_ref[2:3, :] = jnp.sum(dzc * z, axis=0, keepdims=True)

    sb, sc_, su, sw = _conv_specs(S, tc)
    return pl.pallas_call(
        body, name="conv_bwd", grid=(D // tc,),
        in_specs=[pl.BlockSpec((S, tc), lambda j: (0, j)), sb, sc_, su, sw],
        out_specs=[pl.BlockSpec((3, S, tc), lambda j: (0, 0, j)), pl.BlockSpec((3, tc), lambda j: (0, j))],
        out_shape=[_sds((3, S, D), BF16), _sds((3, D), F32)],
        compiler_params=_cparams("parallel"))(dbz, proj3, proj3, proj3, w)


def _silu(c_all):
    def body(c_ref, o_ref):
        cv = c_ref[...]
        o_ref[...] = cv * (1.0 / (1.0 + jnp.exp(-cv)))

    vm = pl.BlockSpec(memory_space=pltpu.VMEM)
    return pl.pallas_call(body, name="cond_silu", in_specs=[vm], out_specs=vm, out_shape=_sds(c_all.shape, F32))(c_all)


def _mod_fwd(cond, w_mod, b_cols):
    L, D, ncol = w_mod.shape
    B = cond.shape[0]
    tk, tn = min(512, D), min(1024, ncol)
    nk = D // tk

    def body(c_ref, w_ref, b_ref, out_ref, acc):
        kk = pl.program_id(2)
        part = lax.dot_general(c_ref[...].astype(BF16), w_ref[...].astype(BF16), DIMS_NN, preferred_element_type=F32)

        @pl.when(kk == 0)
        def _():
            acc[...] = part

        @pl.when(kk > 0)
        def _():
            acc[...] += part

        @pl.when(kk == nk - 1)
        def _():
            out_ref[...] = acc[...] + b_ref[...]

    return pl.pallas_call(
        body, name="mod_fwd", grid=(L, ncol // tn, nk),
        in_specs=[pl.BlockSpec((B, tk), lambda l, j, k: (0, k)),
                  pl.BlockSpec((None, tk, tn), lambda l, j, k: (l, k, j)),
                  pl.BlockSpec((None, 1, tn), lambda l, j, k: (l, 0, j))],
        out_specs=pl.BlockSpec((None, B, tn), lambda l, j, k: (l, 0, j)),
        out_shape=_sds((L, B, ncol), F32),
        scratch_shapes=[pltpu.VMEM((B, tn), F32)],
        compiler_params=_cparams("parallel", "parallel", "arbitrary"))(cond, w_mod, b_cols)


def _adamw_math(w, g, m, v):
    m = ADAM_B1 * m + (1.0 - ADAM_B1) * g
    v = ADAM_B2 * v + (1.0 - ADAM_B2) * (g * g)
    m_hat = m / (1.0 - ADAM_B1 ** ADAM_STEP)
    v_hat = v / (1.0 - ADAM_B2 ** ADAM_STEP)
    delta = -ADAM_LR * (m_hat / (jnp.sqrt(v_hat) + ADAM_EPS) + ADAM_WD * w)
    return delta, m, v


def _adamw(name, w, g, m, v, emit_grad=False):
    shape = w.shape
    cols = shape[-1] if w.ndim <= 3 else shape[-2] * shape[-1]
    rows = w.size // cols
    w2, g2, m2, v2 = (t.reshape(rows, cols) for t in (w, g, m, v))
    tr = _row_tile(rows, cols * 4, limit=2 * 1024 * 1024, mult=8)
    spec = pl.BlockSpec((tr, cols), lambda i: (i, 0))
    n_out = 4 if emit_grad else 3

    def body(w_ref, g_ref, m_ref, v_ref, d_ref, nm_ref, nv_ref, *rest):
        gv = g_ref[...]
        d, nm, nv = _adamw_math(w_ref[...], gv, m_ref[...], v_ref[...])
        d_ref[...] = d
        nm_ref[...] = nm
        nv_ref[...] = nv
        if emit_grad:
            rest[0][...] = gv

    outs = pl.pallas_call(body, name=name, grid=(rows // tr,), in_specs=[spec] * 4, out_specs=[spec] * n_out,
                          out_shape=[_sds((rows, cols), F32)] * n_out, compiler_params=_cparams("parallel"))(w2, g2, m2, v2)
    return tuple(t.reshape(shape) for t in outs)


def _adamw_mod(w, cond_t, dmod_cols, m, v, rider=NO_RIDER):
    L, D, ncol = w.shape
    B = cond_t.shape[1]
    tr, tc = min(256, D), min(1024, ncol)
    blk = pl.BlockSpec((None, tr, tc), lambda l, i, j: (l, i, j))
    grid = (L, D // tr, ncol // tc)
    n_ri, n_ro = len(rider.operands), len(rider.out_shape)

    def body(*refs):
        w_ref, ct_ref, dm_ref, m_ref, v_ref = refs[:5]
        r_in = refs[5:5 + n_ri]
        g_ref, d_ref, nm_ref, nv_ref = refs[5 + n_ri:9 + n_ri]
        r_out = refs[9 + n_ri:9 + n_ri + n_ro]
        r_sems = refs[9 + n_ri + n_ro:]
        ids = [pl.program_id(a) for a in range(3)]

        if rider.start is not None:
            @pl.when((ids[0] == 0) & (ids[1] == 0) & (ids[2] == 0))
            def _():
                rider.start(r_in, r_out, r_sems)

        g = lax.dot_general(ct_ref[...], dm_ref[...], DIMS_NN, precision=lax.Precision.HIGHEST,
                            preferred_element_type=F32)
        d, nm, nv = _adamw_math(w_ref[...], g, m_ref[...], v_ref[...])
        g_ref[...] = g
        d_ref[...] = d
        nm_ref[...] = nm
        nv_ref[...] = nv

        if rider.finish is not None:
            @pl.when((ids[0] == grid[0] - 1) & (ids[1] == grid[1] - 1) & (ids[2] == grid[2] - 1))
            def _():
                _rider_end(rider, r_in, r_out, r_sems)

    hosted = rider.start is not None
    res = pl.pallas_call(
        body, name="adamw_w_mod", grid=grid,
        in_specs=[blk, pl.BlockSpec((tr, B), lambda l, i, j: (i, 0)),
                  pl.BlockSpec((None, B, tc), lambda l, i, j: (l, 0, j)), blk, blk] + [ANY] * n_ri,
        out_specs=[blk] * 4 + [ANY] * n_ro, out_shape=[_sds((L, D, ncol), F32)] * 4 + list(rider.out_shape),
        scratch_shapes=list(rider.sems), input_output_aliases={5 + i: 4 + o for i, o in rider.aliases.items()},
        compiler_params=_cparams(*(("arbitrary",) * 3 if hosted else ("parallel",) * 3)))(
            w, cond_t, dmod_cols, m, v, *rider.operands)
    return (*res[:4], res[4:])


def _cast_into_full(name, ws, kinds, k_idx, rider=NO_RIDER):
    L, R, C = ws[0].shape
    assert all(w.shape == (L, R, C) for w in ws)
    n = len(ws)
    Rh = R // 2
    tr = _row_tile(Rh, C * 4)
    grid = (L, 2, Rh // tr)
    out_shape, out_specs = [], []
    for kind in kinds:
        if kind == "row":
            out_shape.append(_sds((L, N_CHIPS, 2, Rh, C), BF16))
            out_specs.append(pl.BlockSpec((None, None, None, tr, C), lambda l, h, i, k_ref: (l, k_ref[0], h, i, 0)))
        else:
            out_shape.append(_sds((L, 2, Rh, N_CHIPS * C), BF16))
            out_specs.append(pl.BlockSpec((None, None, tr, C), lambda l, h, i, k_ref: (l, h, i, k_ref[0])))
    n_ri, n_ro = len(rider.operands), len(rider.out_shape)

    def body(k_ref, *refs):
        r_in = refs[n:n + n_ri]
        r_out = refs[2 * n + n_ri:2 * n + n_ri + n_ro]
        r_sems = refs[2 * n + n_ri + n_ro:]
        ids = [pl.program_id(a) for a in range(3)]
        if rider.start is not None:
            @pl.when((ids[0] == 0) & (ids[1] == 0) & (ids[2] == 0))
            def _():
                rider.start(r_in, r_out, r_sems)
        for a in range(n):
            refs[n + n_ri + a][...] = refs[a][...].astype(BF16)
        if rider.finish is not None:
            @pl.when((ids[0] == grid[0] - 1) & (ids[1] == grid[1] - 1) & (ids[2] == grid[2] - 1))
            def _():
                _rider_end(rider, r_in, r_out, r_sems)

    hosted = rider.start is not None
    res = pl.pallas_call(
        body, name=name,
        grid_spec=pltpu.PrefetchScalarGridSpec(
            num_scalar_prefetch=1, grid=grid,
            in_specs=[pl.BlockSpec((None, None, tr, C), lambda l, h, i, k_ref: (l, h, i, 0))] * n + [ANY] * n_ri,
            out_specs=out_specs + [ANY] * n_ro, scratch_shapes=list(rider.sems)),
        out_shape=out_shape + list(rider.out_shape),
        input_output_aliases={1 + n + i: n + o for i, o in rider.aliases.items()},
        compiler_params=_cparams(*(("arbitrary",) * 3 if hosted else ("parallel",) * 3)))(
            k_idx, *[w.reshape(L, 2, Rh, C) for w in ws], *rider.operands)
    return res[:n], res[n:]


def _pair_sum(name, g5, ra, c_idx):
    L, A, _, Rh, Cc = g5.shape
    tr = _row_tile(Rh, Cc * 4)

    def body(c_ref, g_ref, r_ref, o_ref):
        o_ref[...] = (g_ref[...].astype(F32) + r_ref[...].astype(F32)).astype(BF16)

    blk = pl.BlockSpec((None, None, tr, Cc), lambda l, a, i, c_ref: (l, a, i, 0))
    return pl.pallas_call(
        body, name=name,
        grid_spec=pltpu.PrefetchScalarGridSpec(
            num_scalar_prefetch=1, grid=(L, A, Rh // tr),
            in_specs=[pl.BlockSpec((None, None, None, tr, Cc), lambda l, a, i, c_ref: (l, a, c_ref[0], i, 0)), blk],
            out_specs=blk),
        out_shape=_sds((L, A, Rh, Cc), BF16),
        compiler_params=_cparams("parallel", "parallel", "parallel"))(c_idx, g5, ra)


def _chip_sum(name, p, rb, kc_idx, kind, layer=0, n_layers=1, prev=None):
    _, A, Rh, Cc = p.shape
    C = rb.shape[-1]
    tr = _row_tile(Rh, C * 4)
    if kind == "row":
        own = pl.BlockSpec((None, None, tr, C), lambda i, kc: (0, kc[0], i, 0))
    else:
        own = pl.BlockSpec((None, None, tr, C), lambda i, kc: (0, 0, i, kc[0]))
    peer = lambda j: pl.BlockSpec((None, None, tr, C), lambda i, kc: (j, 0, i, 0))

    def body(kc_ref, p_ref, r0_ref, r1_ref, r2_ref, *rest):
        o_ref = rest[-1]
        o_ref[...] = ((p_ref[...].astype(F32) + r0_ref[...].astype(F32)) + r1_ref[...].astype(F32)) + r2_ref[...].astype(F32)

    operands = [kc_idx, p, rb, rb, rb] + ([prev] if prev is not None else [])
    return pl.pallas_call(
        body, name=name,
        grid_spec=pltpu.PrefetchScalarGridSpec(
            num_scalar_prefetch=1, grid=(Rh // tr,),
            in_specs=[own, peer(0), peer(1), peer(2)] + ([ANY] if prev is not None else []),
            out_specs=pl.BlockSpec((None, None, tr, C), lambda i, kc: (layer, kc[1], i, 0))),
        out_shape=_sds((n_layers, 2, Rh, C), F32),
        input_output_aliases={5: 0} if prev is not None else {},
        compiler_params=_cparams("parallel"))(*operands)


def _mesh_place():
    x, y, c = lax.axis_index("x"), lax.axis_index("y"), lax.axis_index("c")
    chips = [(1 - x, y), (x, 1 - y), (1 - x, 1 - y)]
    return x, y, c, chips


def _remote(src, dst, send_sem, recv_sem, to):
    return pltpu.make_async_remote_copy(src_ref=src, dst_ref=dst, send_sem=send_sem, recv_sem=recv_sem,
                                        device_id=to, device_id_type=MESH_ID)


def _small_allgather(name, v, with_sum=False, rider=NO_RIDER):
    R, N = v.shape
    n_ri, n_ro, n_own = len(rider.operands), len(rider.out_shape), 2 if with_sum else 1

    def body(*refs):
        r_in = refs[1:1 + n_ri]
        r_out = refs[1 + n_ri + n_own:1 + n_ri + n_own + n_ro]
        r_sems = refs[1 + n_ri + n_own + n_ro + 3:]
        own = (refs[0],) + refs[1 + n_ri:1 + n_ri + n_own] + refs[1 + n_ri + n_own + n_ro:1 + n_ri + n_own + n_ro + 3]
        if with_sum:
            x_ref, out_ref, sum_ref, send_sems, recv_sems, local_sem = own
        else:
            x_ref, out_ref, send_sems, recv_sems, local_sem = own
        if rider.start is not None:
            rider.start(r_in, r_out, r_sems)
        x, y, c, chips = _mesh_place()
        me, sibling = (x, y, c), (x, y, 1 - c)

        def rows(px, py, pc):
            return out_ref.at[pl.ds((4 * px + 2 * py + pc) * R, R), :]

        def copy(k, block, to, src=None):
            return _remote(rows(*block) if src is None else src, rows(*block), send_sems.at[k], recv_sems.at[k], to)

        mine = pltpu.make_async_copy(x_ref, rows(*me), local_sem)
        mine.start()
        first = [copy(0, me, sibling, src=x_ref)]
        first += [copy(1 + j, me, (*chip, c), src=x_ref) for j, chip in enumerate(chips)]
        for cp in first:
            cp.start()
        passed = [copy(4 + j, (*chip, c), sibling) for j, chip in enumerate(chips)]
        for j, chip in enumerate(chips):
            copy(1 + j, (*chip, c), me).wait_recv()
            passed[j].start()
        copy(0, sibling, me).wait_recv()
        for j, chip in enumerate(chips):
            copy(4 + j, (*chip, 1 - c), me).wait_recv()
        for cp in first + passed:
            cp.wait_send()
        mine.wait()
        if with_sum:
            total = out_ref[0:R, :]
            for p in range(1, 8):
                total = total + out_ref[p * R:(p + 1) * R, :]
            sum_ref[...] = total
        if rider.finish is not None:
            _rider_end(rider, r_in, r_out, r_sems)

    vm = pl.BlockSpec(memory_space=pltpu.VMEM)
    out_shape = [_sds((8 * R, N), F32)] + ([_sds((R, N), F32)] if with_sum else [])
    res = pl.pallas_call(
        body, name=name, out_shape=out_shape + list(rider.out_shape), in_specs=[vm] + [ANY] * n_ri,
        out_specs=[vm] * n_own + [ANY] * n_ro,
        scratch_shapes=[pltpu.SemaphoreType.DMA((7,)), pltpu.SemaphoreType.DMA((7,)), pltpu.SemaphoreType.DMA]
        + list(rider.sems),
        input_output_aliases={1 + i: n_own + o for i, o in rider.aliases.items()},
        compiler_params=pltpu.CompilerParams(vmem_limit_bytes=VMEM_LIMIT_BYTES))(v, *rider.operands)
    if rider.start is not None:
        return (*res[:n_own], res[n_own:])
    return res if with_sum else res[0]


def _full_place(ref, kind, C, kk, half, layer=None):
    lead = slice(None) if layer is None else pl.ds(layer, 1)
    if kind == "row":
        return ref.at[lead, kk, half]
    return ref.at[lead, half, :, pl.ds(pl.multiple_of(kk * C, LANES), C)]


def _gather_rider(fulls, kinds, shard_cols, layers=None, peers=(0, 1, 2)):
    n = len(fulls)
    layers = layers or [None] * n
    rows = [f.shape[3] if kind == "row" else f.shape[2] for f, kind in zip(fulls, kinds)]
    n_chunks = 2 if all(r % 32 == 0 for r in rows) else 1

    def copies(outs, sems):
        x, y, c, chips = _mesh_place()
        k = 2 * x + y

        def place(a, kk, half, ch):
            rc = rows[a] // n_chunks
            return _full_place(outs[a], kinds[a], shard_cols[a], kk, half, layers[a]).at[:, pl.ds(ch * rc, rc), :]

        def copy(a, j, ch, ref, to):
            s = 6 * (n_chunks * a + ch) + j
            return _remote(ref, ref, sems[0].at[s], sems[1].at[s], to)

        return (x, y, c), [(j, chip) for j, chip in enumerate(chips) if j in peers], k, place, copy

    def start(_, outs, sems):
        (x, y, c), chips, k, place, copy = copies(outs, sems)
        for ch in range(n_chunks):
            for j, chip in chips:
                for a in range(n):
                    copy(a, j, ch, place(a, k, c, ch), (*chip, c)).start()

    def pass_on(outs, sems, ch):
        (x, y, c), chips, k, place, copy = copies(outs, sems)
        for j, chip in chips:
            kj = 2 * chip[0] + chip[1]
            for a in range(n):
                copy(a, j, ch, place(a, kj, c, ch), (x, y, c)).wait_recv()
                copy(a, 3 + j, ch, place(a, kj, c, ch), (x, y, 1 - c)).start()

    def mid(_, outs, sems):
        pass_on(outs, sems, 0)

    def finish(_, outs, sems):
        pass_on(outs, sems, n_chunks - 1)
        (x, y, c), chips, k, place, copy = copies(outs, sems)
        for ch in range(n_chunks):
            for j, chip in chips:
                kj = 2 * chip[0] + chip[1]
                for a in range(n):
                    copy(a, 3 + j, ch, place(a, kj, 1 - c, ch), (x, y, c)).wait_recv()
        for ch in range(n_chunks):
            for j, chip in chips:
                kj = 2 * chip[0] + chip[1]
                for a in range(n):
                    copy(a, j, ch, place(a, k, c, ch), (*chip, c)).wait_send()
                    copy(a, 3 + j, ch, place(a, kj, c, ch), (x, y, 1 - c)).wait_send()

    n_sems = 6 * n * n_chunks
    return Rider(tuple(fulls), tuple(_sds(f.shape, BF16) for f in fulls), {a: a for a in range(n)},
                 (pltpu.SemaphoreType.DMA((n_sems,)), pltpu.SemaphoreType.DMA((n_sems,))), start, finish,
                 mid if n_chunks == 2 else None)


def _scatter_rider(ps, kinds, shard_cols, peers=(0, 1, 2), into=None):
    n = len(ps)

    def copies(ins, outs, sems):
        x, y, c, chips = _mesh_place()
        cps = []
        for j, chip in enumerate(chips):
            if j not in peers:
                continue
            kj = 2 * chip[0] + chip[1]
            for a in range(n):
                C = shard_cols[a]
                src = ins[a].at[:, kj] if kinds[a] == "row" else ins[a].at[:, 0, :, pl.ds(pl.multiple_of(kj * C, LANES), C)]
                cps.append(_remote(src, outs[a].at[j], sems[0].at[3 * a + j], sems[1].at[3 * a + j], (*chip, c)))
        return cps

    def start(ins, outs, sems):
        for cp in copies(ins, outs, sems):
            cp.start()

    def finish(ins, outs, sems):
        cps = copies(ins, outs, sems)
        for cp in cps:
            cp.wait_recv()
        for cp in cps:
            cp.wait_send()

    out_shape = tuple(_sds((3, p.shape[0], p.shape[2], C), BF16) for p, C in zip(ps, shard_cols))
    aliases = {n + a: a for a in range(n)} if into is not None else {}
    return Rider(tuple(ps) + tuple(into or ()), out_shape, aliases,
                 (pltpu.SemaphoreType.DMA((3 * n,)), pltpu.SemaphoreType.DMA((3 * n,))), start, finish)


def _run_rider(name, rider):
    n_in, n_out = len(rider.operands), len(rider.out_shape)

    def body(*refs):
        ins, outs, sems = refs[:n_in], refs[n_in:n_in + n_out], refs[n_in + n_out:]
        rider.start(ins, outs, sems)
        _rider_end(rider, ins, outs, sems)

    return pl.pallas_call(
        body, name=name, out_shape=list(rider.out_shape), in_specs=[ANY] * n_in, out_specs=[ANY] * n_out,
        input_output_aliases=dict(rider.aliases), scratch_shapes=list(rider.sems),
        compiler_params=pltpu.CompilerParams(vmem_limit_bytes=VMEM_LIMIT_BYTES))(*rider.operands)


def _exchange_rider(g5s):
    n = len(g5s)

    def copies(ins, outs, sems):
        x, y, c, _ = _mesh_place()
        return [_remote(ins[a].at[:, :, 1 - c], outs[a], sems[0].at[a], sems[1].at[a], (x, y, 1 - c)) for a in range(n)]

    def start(ins, outs, sems):
        for cp in copies(ins, outs, sems):
            cp.start()

    def finish(ins, outs, sems):
        cps = copies(ins, outs, sems)
        for cp in cps:
            cp.wait_recv()
        for cp in cps:
            cp.wait_send()

    out_shape = tuple(_sds((g.shape[0], g.shape[1], g.shape[3], g.shape[4]), BF16) for g in g5s)
    return Rider(tuple(g5s), out_shape, {}, (pltpu.SemaphoreType.DMA((n,)), pltpu.SemaphoreType.DMA((n,))), start, finish)


def _share_rider(fs):
    n = len(fs)

    def start(_, outs, sems):
        x, y, c, _p = _mesh_place()
        for a in range(n):
            mine = outs[a].at[:, c]
            _remote(mine, mine, sems[0].at[a], sems[1].at[a], (x, y, 1 - c)).start()

    def finish(_, outs, sems):
        x, y, c, _p = _mesh_place()
        for a in range(n):
            theirs = outs[a].at[:, 1 - c]
            _remote(theirs, theirs, sems[0].at[a], sems[1].at[a], (x, y, c)).wait_recv()
        for a in range(n):
            mine = outs[a].at[:, c]
            _remote(mine, mine, sems[0].at[a], sems[1].at[a], (x, y, 1 - c)).wait_send()

    return Rider(tuple(fs), tuple(_sds(f.shape, F32) for f in fs), {a: a for a in range(n)},
                 (pltpu.SemaphoreType.DMA((n,)), pltpu.SemaphoreType.DMA((n,))), start, finish)


def _both_riders(r1, r2):
    ni, no, ns = len(r1.operands), len(r1.out_shape), len(r1.sems)
    aliases = dict(r1.aliases)
    aliases.update({ni + i: no + o for i, o in r2.aliases.items()})

    def start(ins, outs, sems):
        r1.start(ins[:ni], outs[:no], sems[:ns])
        r2.start(ins[ni:], outs[no:], sems[ns:])

    def finish(ins, outs, sems):
        _rider_end(r1, ins[:ni], outs[:no], sems[:ns])
        _rider_end(r2, ins[ni:], outs[no:], sems[ns:])

    return Rider(r1.operands + r2.operands, r1.out_shape + r2.out_shape, aliases, r1.sems + r2.sems, start, finish)


def _pack_rows(parts, lane_mult=1024):
    flat = jnp.concatenate([p.reshape(-1).astype(F32) for p in parts])
    n = -(-flat.shape[0] // (8 * lane_mult)) * lane_mult
    return jnp.pad(flat, (0, 8 * n - flat.shape[0])).reshape(8, n)


def _relu2(acc):
    r = jnp.maximum(acc, 0.0)
    return r, r * r


def _times_2r(acc, r):
    return (acc * (2.0 * r.astype(F32)),)


def kernel(x, c, positions, w_mod, b_mod, norm_g, mla_w_in, mla_g_q, mla_g_kv, mla_w_uq, mla_w_ukv, mla_w_o, conv_w_in, conv_w, conv_w_out, mlp_w_up, mlp_w_down, loss_target, m_w_mod, m_b_mod, m_norm_g, m_mla_w_in, m_mla_g_q, m_mla_g_kv, m_mla_w_uq, m_mla_w_ukv, m_mla_w_o, m_conv_w_in, m_conv_w, m_conv_w_out, m_mlp_w_up, m_mlp_w_down, v_w_mod, v_b_mod, v_norm_g, v_mla_w_in, v_mla_g_q, v_mla_g_kv, v_mla_w_uq, v_mla_w_ukv, v_mla_w_o, v_conv_w_in, v_conv_w, v_conv_w_out, v_mlp_w_up, v_mlp_w_down):
    S, D = x.shape[1], x.shape[2]
    Dq = D // N_CHIPS
    ncol = w_mod.shape[2]
    n_mod = N_CHIPS * ncol // D
    F = mlp_w_up.shape[2] * N_CHIPS
    lat_dim = mla_w_in.shape[2]
    rank = mla_g_q.shape[1]
    H = mla_w_uq.shape[2]
    d_qk = mla_w_uq.shape[3]
    assert mla_g_kv.shape[1] == rank and lat_dim == 2 * rank + QK_ROPE and d_qk == QK_NOPE + QK_ROPE
    assert mla_w_ukv.shape[3] == QK_NOPE + V_HEAD and x.shape[0] == 1 and n_mod == 6
    assert norm_g.shape[0] == 2 and mla_w_in.shape[0] == 1 and conv_w_in.shape[0] == 1
    lat_pad = 2 * rank + LANES
    scale = float(d_qk) ** -0.5

    xi, yi, ci = lax.axis_index("x"), lax.axis_index("y"), lax.axis_index("c")
    chip = 2 * xi + yi
    dev = 2 * chip + ci
    c_idx = jnp.reshape(ci, (1,)).astype(jnp.int32)
    k_idx = jnp.reshape(chip, (1,)).astype(jnp.int32)

    n1 = D + 2 * D + 3 * Dq
    g1 = _small_allgather("gather_small_inputs", _pack_rows([c, norm_g, conv_w])).reshape(8, -1)
    c_all = g1[:, :D]
    by_chip = g1[0::2]
    norm_full = jnp.concatenate([by_chip[kk, D:3 * D].reshape(2, 4, Dq) for kk in range(N_CHIPS)], axis=-1)
    convw_full = jnp.concatenate([by_chip[kk, 3 * D:n1].reshape(3, Dq) for kk in range(N_CHIPS)], axis=-1)

    b_cols = lax.dynamic_slice(b_mod, (0, chip * ncol), (2, ncol)).reshape(2, 1, ncol)
    cond_all = _silu(c_all)
    mod_cols = _mod_fwd(cond_all, w_mod, b_cols)
    g2 = _small_allgather("gather_mod", _pack_rows([mod_cols]))
    g2 = g2.reshape(8, -1)[0::2, :2 * 8 * ncol].reshape(N_CHIPS, 2, 8, ncol)
    mod_all = jnp.transpose(g2, (2, 1, 0, 3)).reshape(8, 2, n_mod * D)
    mod_me = lax.dynamic_index_in_dim(mod_all, dev, axis=0, keepdims=False)
    mods = [[mod_me[l, i * D:(i + 1) * D].reshape(1, D) for i in range(n_mod)] for l in range(2)]
    ng = [[norm_full[l, i].reshape(1, D) for i in range(4)] for l in range(2)]

    pos = positions[0].astype(F32)
    inv_freq = ROPE_THETA ** (-jnp.arange(0, QK_ROPE, 2, dtype=F32) / QK_ROPE)
    ang = pos[:, None] * inv_freq
    cos, sin = jnp.cos(ang), jnp.sin(ang)
    zero = jnp.zeros_like(cos)
    rope_tabs = (jnp.concatenate([cos, cos, zero, zero], axis=1),
                 jnp.concatenate([-sin, zero, zero, zero], axis=1),
                 jnp.concatenate([zero, sin, zero, zero], axis=1))

    weights = [("mla_w_in", mla_w_in, "row"), ("mla_w_uq", mla_w_uq.reshape(1, rank // N_CHIPS, H * d_qk), "row"),
               ("mla_w_ukv", mla_w_ukv.reshape(1, rank // N_CHIPS, H * QK_PAD), "row"), ("mla_w_o", mla_w_o, "row"),
               ("conv_w_in", conv_w_in, "col"), ("conv_w_out", conv_w_out, "row"),
               ("mlp_w_up", mlp_w_up, "col"), ("mlp_w_down", mlp_w_down, "row")]
    kinds = [k for _, _, k in weights]
    shard_shapes = [w.shape for _, w, _ in weights]
    shard_cols = [s[2] for s in shard_shapes]
    W_IN, W_UQ, W_UKV, W_O, W_CIN, W_COUT, W_UP, W_DOWN = range(8)
    mla_idx = [W_IN, W_UQ, W_UKV, W_O]
    casted = [_cast_into_full("cast_" + nm, [w], [kind], k_idx)[0][0] for nm, w, kind in weights[:W_UP]]

    def view(i, buf):
        L, R, C = shard_shapes[i]
        return buf.reshape((L, N_CHIPS * R, C) if kinds[i] == "row" else (L, R, N_CHIPS * C))

    NEIGHBOURS, DIAGONAL = (0, 1), (2,)

    def gather_of(bufs, idx, layers=None, peers=(0, 1, 2)):
        return _gather_rider(bufs, [kinds[i] for i in idx], [shard_cols[i] for i in idx], layers, peers)

    def scatter_of(ps, idx, peers=(0, 1, 2), into=None):
        return _scatter_rider(ps, [kinds[i] for i in idx], [shard_cols[i] for i in idx], peers, into)

    def halves(items):
        g5s = []
        for _, i, g in items:
            _, R, C = shard_shapes[i]
            g5s.append(g.reshape((1, N_CHIPS, 2, R // 2, C) if kinds[i] == "row" else (1, 1, 2, R // 2, N_CHIPS * C)))
        return g5s

    def pair_sums(items, g5s, ras):
        return [_pair_sum("pair_sum_" + nm, g5, ra, c_idx) for (nm, _, _), g5, ra in zip(items, g5s, ras)]

    first_idx = [W_IN, W_UQ, W_UKV]
    mlp_casted, got = _cast_into_full("cast_mlp_w", [mlp_w_up, mlp_w_down], [kinds[W_UP], kinds[W_DOWN]], k_idx,
                                      gather_of([casted[i] for i in first_idx], first_idx))
    casted += list(mlp_casted)
    w_in_p = jnp.pad(view(W_IN, got[0])[0], ((0, 0), (0, lat_pad - lat_dim)))
    w_q_p = jnp.pad(view(W_UQ, got[1])[0].reshape(rank, H, d_qk), ((0, 0), (0, 0), (0, QK_PAD - d_qk))).reshape(rank, H * QK_PAD)
    w_ukv = view(W_UKV, got[2])[0]
    HV = H * V_HEAD

    def layer_b(l, transposed):
        if transposed:
            return lambda tm, tn, tk: pl.BlockSpec((None, tn, tk), lambda i, j, k: (l, j, k))
        return lambda tm, tn, tk: pl.BlockSpec((None, tk, tn), lambda i, j, k: (l, k, j))

    def mlp_up(tag, l, h, w, rider=NO_RIDER):
        return _mm("mlp_up_" + tag, h, w, "nn", S, F, D, [_sds((S, F), BF16)] * 2, epilogue=_relu2,
                   b_spec=layer_b(l, False), rider=rider)

    def mlp_down(tag, l, a2, w, rider=NO_RIDER):
        return _mm("mlp_down_" + tag, a2, w, "nn", S, D, F, [_sds((S, D), BF16)], b_spec=layer_b(l, False), rider=rider)

    def mlp_bwd(tag, l, h, r, a2, dy, first=NO_RIDER, second_of=None):
        res = _mm("mlp_down_dx_" + tag, dy, w_down, "nt", S, F, D, [_sds((S, F), BF16)], epilogue=_times_2r,
                  b_spec=layer_b(l, True), rider=first,
                  extras=[(r, lambda tm, tn, tk: pl.BlockSpec((tm, tn), lambda i, j, k: (i, j)))])
        (da,), got_first = res if first.start is not None else (res, ())
        second = second_of(got_first) if second_of else NO_RIDER
        res = _mm("mlp_down_dw_" + tag, a2, dy, "tn", F, D, S, [_sds((F, D), BF16)], rider=second)
        (dw_down,), got_second = res if second_of else (res, ())
        (dh,) = _mm("mlp_up_dx_" + tag, da, w_up, "nt", S, D, F, [_sds((S, D), BF16)], b_spec=layer_b(l, True))
        (dw_up,) = _mm("mlp_up_dw_" + tag, h, da, "tn", D, F, S, [_sds((D, F), BF16)])
        return dh, dw_up, dw_down, got_first, got_second

    x0 = x[0]
    sh1, sc1, gt1, sh2, sc2, gt2 = mods[0]
    (h1,) = _fwd_boundary("fwd_boundary_0", x0, None, None, None, ng[0][0], sc1, sh1)
    (lat,) = _mm("mla_in", h1, w_in_p, "nn", S, lat_pad, D, [_sds((S, lat_pad), F32)], tn=lat_pad)
    cq, ckv, kr = _latent_fwd(lat, mla_g_q, mla_g_kv, rope_tabs, rank)

    def rope_q(acc, cos_p, sin_lo, sin_hi):
        parts = []
        for hh in range(acc.shape[1] // QK_PAD):
            parts.append(acc[:, hh * QK_PAD:hh * QK_PAD + QK_NOPE])
            parts.append(_rope(acc[:, hh * QK_PAD + QK_NOPE:(hh + 1) * QK_PAD], cos_p, sin_lo, sin_hi))
        return (jnp.concatenate(parts, axis=1),)

    tab_extra = lambda tm, tn, tk: pl.BlockSpec((tm, LANES), lambda i, j, k: (i, 0))
    (q,), (o_buf,) = _mm("mla_q", cq, w_q_p, "nn", S, H * QK_PAD, rank, [_sds((S, H * QK_PAD), BF16)], epilogue=rope_q,
                         extras=[(t, tab_extra) for t in rope_tabs], tn=2 * QK_PAD,
                         rider=gather_of([casted[W_O]], [W_O]))
    w_o = view(W_O, o_buf)[0]
    (kv,) = _mm("mla_kv", ckv, w_ukv, "nn", S, H * QK_PAD, rank, [_sds((S, H * QK_PAD), BF16)])
    rest_idx = [W_UP, W_DOWN]
    o, lse, (up_buf, down_buf) = _attn_fwd_tri(
        q, kv, kr, H, scale, gather_of([casted[i] for i in rest_idx], rest_idx, [0, 0]))
    (y1,), (cout_buf,) = _mm("mla_out", o, w_o, "nn", S, D, HV, [_sds((S, D), BF16)],
                             rider=gather_of([casted[W_COUT]], [W_COUT]))
    x1, h2 = _fwd_boundary("fwd_boundary_1", x0, y1, gt1, ng[0][1], ng[0][2], sc2, sh2)
    (r2, a2), (cin_buf,) = mlp_up("0", 0, h2, view(W_UP, up_buf), gather_of([casted[W_CIN]], [W_CIN]))
    (y2,), (up_buf,) = mlp_down("0", 0, a2, view(W_DOWN, down_buf), gather_of([up_buf], [W_UP], [1]))
    w_cin, w_cout, w_up = view(W_CIN, cin_buf)[0], view(W_COUT, cout_buf)[0], view(W_UP, up_buf)

    sh1b, sc1b, gt1b, sh2b, sc2b, gt2b = mods[1]
    x2, h3 = _fwd_boundary("fwd_boundary_2", x1, y2, gt2, ng[0][3], ng[1][0], sc1b, sh1b)
    nD = lambda tn: D // tn
    (proj3,), (down_buf,) = _mm(
        "conv_in", h3, w_cin, "nn", S, 3 * D, D, [_sds((3, S, D), BF16)], tn=min(1024, D),
        rider=gather_of([down_buf], [W_DOWN], [1], NEIGHBOURS),
        out_specs=[lambda tm, tn, tk: pl.BlockSpec((None, tm, tn), lambda i, j, k: (j // nD(tn), i, j % nD(tn)))])
    bz = _conv_fwd(proj3, convw_full)
    (y3,) = _mm("conv_out", bz, w_cout, "nn", S, D, D, [_sds((S, D), BF16)])
    x3, h4 = _fwd_boundary("fwd_boundary_3", x2, y3, gt1b, ng[1][1], ng[1][2], sc2b, sh2b)
    (r4, a4), (down_buf,) = mlp_up("1", 1, h4, w_up, gather_of([down_buf], [W_DOWN], [1], DIAGONAL))
    w_down = view(W_DOWN, down_buf)
    (y4,) = mlp_down("1", 1, a4, w_down)

    dx4, dy4, sums_l, loss_acc = _loss_boundary("loss_boundary", x3, y4, gt2b, ng[1][3], loss_target[0])

    dh4, dw_up1, dw_down1, _, _ = mlp_bwd("1", 1, h4, r4, a4, dy4)
    dx3, dy3, sums_3, _ = _bwd_boundary("bwd_boundary_3", dx4, dh4, x3, y3, gt1b, ng[1][1], ng[1][2], sc2b)

    items = [("mlp_w_up_1", W_UP, dw_up1), ("mlp_w_down_1", W_DOWN, dw_down1)]
    g5s = halves(items)
    (dbz,), ras = _mm("conv_out_dx", dy3, w_cout, "nt", S, D, D, [_sds((S, D), BF16)], rider=_exchange_rider(g5s))
    ps_up1, ps_down1 = pair_sums(items, g5s, ras)
    (dw_cout,) = _mm("conv_out_dw", bz, dy3, "tn", D, D, S, [_sds((D, D), BF16)])
    dproj3, dconvw = _conv_bwd(dbz, proj3, convw_full)
    (dh3,), (rb_up1,) = _mm(
        "conv_in_dx", dproj3, w_cin, "nt", S, D, 3 * D, [_sds((S, D), BF16)], tk=D,
        rider=scatter_of([ps_up1], [W_UP], NEIGHBOURS),
        a_spec=lambda tm, tn, tk: pl.BlockSpec((None, tm, tk), lambda i, j, k: (k // (D // tk), i, k % (D // tk))))
    (dw_cin,), (rb_up1,) = _mm(
        "conv_in_dw", h3, dproj3, "tn", D, 3 * D, S, [_sds((D, 3 * D), BF16)], tn=min(1024, D),
        rider=scatter_of([ps_up1], [W_UP], DIAGONAL, [rb_up1]),
        b_spec=lambda tm, tn, tk: pl.BlockSpec((None, tk, tn), lambda i, j, k: (j // nD(tn), k, j % nD(tn))))
    dx2, dy2, sums_2, _ = _bwd_boundary("bwd_boundary_2", dx3, dh3, x2, y2, gt2, ng[0][3], ng[1][0], sc1b)

    items = [("conv_w_in", W_CIN, dw_cin), ("conv_w_out", W_COUT, dw_cout)]
    g5s = halves(items)
    dh2, dw_up0, dw_down0, (rb_down1, *ras), (rb_down1,) = mlp_bwd(
        "0", 0, h2, r2, a2, dy2,
        _both_riders(scatter_of([ps_down1], [W_DOWN], NEIGHBOURS), _exchange_rider(g5s)),
        lambda got: scatter_of([ps_down1], [W_DOWN], DIAGONAL, [got[0]]))
    ps_cin, ps_cout = pair_sums(items, g5s, ras)
    dx1, dy1, sums_1, _ = _bwd_boundary("bwd_boundary_1", dx2, dh2, x1, y1, gt1, ng[0][1], ng[0][2], sc2)

    items = [("mlp_w_up_0", W_UP, dw_up0), ("mlp_w_down_0", W_DOWN, dw_down0)]
    g5s = halves(items)
    (dw_o,), ras = _mm("mla_out_dw", o, dy1, "tn", HV, D, S, [_sds((HV, D), BF16)], rider=_exchange_rider(g5s))
    ps_up0, ps_down0 = pair_sums(items, g5s, ras)
    items = [(weights[W_O][0], W_O, dw_o)]
    g5s = halves(items)
    (do,), ras = _mm("mla_out_dx", dy1, w_o, "nt", S, HV, D, [_sds((S, HV), BF16)], rider=_exchange_rider(g5s))
    (ps_o,) = pair_sums(items, g5s, ras)
    dq, dkv, dkr, (rb_up0, rb_down0, rb_cin, rb_cout, rb_o) = _attn_bwd_tri(
        q, kv, kr, o, do, lse, rope_tabs, H, scale,
        scatter_of([ps_up0, ps_down0, ps_cin, ps_cout, ps_o], [W_UP, W_DOWN, W_CIN, W_COUT, W_O]))
    kc_idx = jnp.stack([chip, ci]).astype(jnp.int32)
    f_o, f_cin, f_cout = [_chip_sum("chip_sum_" + weights[i][0], p, rb, kc_idx, kinds[i])
                          for i, p, rb in [(W_O, ps_o, rb_o), (W_CIN, ps_cin, rb_cin), (W_COUT, ps_cout, rb_cout)]]
    f_mlp = []
    for i, (p1, r1), (p0, r0) in [(W_UP, (ps_up1, rb_up1), (ps_up0, rb_up0)), (W_DOWN, (ps_down1, rb_down1), (ps_down0, rb_down0))]:
        f = _chip_sum("chip_sum_" + weights[i][0] + "_1", p1, r1, kc_idx, kinds[i], layer=1, n_layers=2)
        f_mlp.append(_chip_sum("chip_sum_" + weights[i][0] + "_0", p0, r0, kc_idx, kinds[i], layer=0, n_layers=2, prev=f))
    (dcq,), fin_up = _mm("mla_q_dx", dq, w_q_p, "nt", S, rank, H * QK_PAD, [_sds((S, rank), F32)],
                         rider=_share_rider([f_mlp[0]]))
    (dw_q_p,), fin_small = _mm("mla_q_dw", cq, dq, "tn", rank, H * QK_PAD, S, [_sds((rank, H * QK_PAD), BF16)],
                               rider=_share_rider([f_o, f_cin, f_cout]))
    (dckv,), fin_down = _mm("mla_kv_dx", dkv, w_ukv, "nt", S, rank, H * QK_PAD, [_sds((S, rank), F32)],
                            rider=_share_rider([f_mlp[1]]))
    finals_rest = list(fin_small) + list(fin_up) + list(fin_down)
    (dw_ukv,) = _mm("mla_kv_dw", ckv, dkv, "tn", rank, H * QK_PAD, S, [_sds((rank, H * QK_PAD), BF16)])
    dlat, sums_lat = _latent_bwd(lat, dcq, dckv, dkr, mla_g_q, mla_g_kv, rope_tabs, rank)
    late_idx = [W_IN, W_UQ, W_UKV]
    items_qkv = [(weights[W_UQ][0], W_UQ, dw_q_p.reshape(rank, H, QK_PAD)[:, :, :d_qk].reshape(rank, H * d_qk)),
                 (weights[W_UKV][0], W_UKV, dw_ukv)]
    g5s_qkv = halves(items_qkv)
    (dw_in_p,), ras = _mm("mla_in_dw", h1, dlat, "tn", D, lat_pad, S, [_sds((D, lat_pad), BF16)], tn=lat_pad,
                          rider=_exchange_rider(g5s_qkv))
    ps_qkv = pair_sums(items_qkv, g5s_qkv, ras)
    items_in = [(weights[W_IN][0], W_IN, dw_in_p[:, :lat_dim])]
    g5s_in = halves(items_in)
    (dh1,), carried = _mm("mla_in_dx", dlat, w_in_p, "nt", S, D, lat_pad, [_sds((S, D), BF16)],
                          rider=_both_riders(scatter_of(ps_qkv, [W_UQ, W_UKV]), _exchange_rider(g5s_in)))
    rbs_qkv, ras = carried[:2], carried[2:]
    ps_in = pair_sums(items_in, g5s_in, ras)
    ps_mla = ps_in + ps_qkv
    grad_x, sums_0, _ = _bwd_boundary("bwd_boundary_0", dx1, dh1, x0, None, None, None, ng[0][0], sc1)

    dmod0 = [sums_0[0], sums_0[1], sums_1[3], sums_1[0], sums_1[1], sums_2[3]]
    dmod1 = [sums_2[0], sums_2[1], sums_3[3], sums_3[0], sums_3[1], sums_l[3]]
    dng0 = [sums_0[2], sums_1[4], sums_1[2], sums_2[4]]
    dng1 = [sums_2[2], sums_3[4], sums_3[2], sums_l[4]]
    small = _pack_rows(dmod0 + dmod1 + dng0 + dng1 + [sums_lat[0], sums_lat[1], dconvw, loss_acc[0, 0:1]],
                       lane_mult=LANES)
    gathered, total, carried = _small_allgather(
        "gather_small_grads", small, with_sum=True, rider=scatter_of(ps_in, [W_IN]))
    rbs_mla = list(carried) + list(rbs_qkv)
    n_dm = 2 * n_mod * D
    dmod_all = gathered.reshape(8, -1)[:, :n_dm].reshape(8, 2, n_mod * D)
    total = total.reshape(-1)
    g_b_mod = total[:n_dm].reshape(2, n_mod * D)
    g_norm = lax.dynamic_slice(total[n_dm:n_dm + 8 * D].reshape(2, 4, D), (0, 0, chip * Dq), (2, 4, Dq))
    off = n_dm + 8 * D
    g_gq = total[off:off + rank].reshape(1, rank)
    g_gkv = total[off + rank:off + 2 * rank].reshape(1, rank)
    off += 2 * rank
    g_convw = lax.dynamic_slice(total[off:off + 3 * D].reshape(1, 3, D), (0, 0, chip * Dq), (1, 3, Dq))
    loss = total[off + 3 * D]

    dmod_cols = jnp.transpose(lax.dynamic_slice(dmod_all.reshape(8, 2, N_CHIPS, ncol), (0, 0, chip, 0), (8, 2, 1, ncol))
                              .reshape(8, 2, ncol), (1, 0, 2))
    g_w_mod, d_w_mod, nm_w_mod, nv_w_mod, _ = _adamw_mod(w_mod, cond_all.T, dmod_cols, m_w_mod, v_w_mod)
    fs_mla = [_chip_sum("chip_sum_" + weights[i][0], p, rb, kc_idx, kinds[i]) for i, p, rb in zip(late_idx, ps_mla, rbs_mla)]
    finals = list(_run_rider("grad_pair_share_mla", _share_rider(fs_mla))) + list(finals_rest)
    orig = [mla_w_in, mla_w_uq, mla_w_ukv, mla_w_o, conv_w_in, conv_w_out, mlp_w_up, mlp_w_down]
    big_grads = [f.reshape(w.shape) for f, w in zip(finals, orig)]

    names = ["b_mod", "norm_g", "mla_w_in", "mla_g_q", "mla_g_kv", "mla_w_uq", "mla_w_ukv", "mla_w_o",
             "conv_w_in", "conv_w", "conv_w_out", "mlp_w_up", "mlp_w_down"]
    ws = [b_mod, norm_g, mla_w_in, mla_g_q, mla_g_kv, mla_w_uq, mla_w_ukv, mla_w_o, conv_w_in, conv_w, conv_w_out,
          mlp_w_up, mlp_w_down]
    ms = [m_b_mod, m_norm_g, m_mla_w_in, m_mla_g_q, m_mla_g_kv, m_mla_w_uq, m_mla_w_ukv, m_mla_w_o, m_conv_w_in,
          m_conv_w, m_conv_w_out, m_mlp_w_up, m_mlp_w_down]
    vs = [v_b_mod, v_norm_g, v_mla_w_in, v_mla_g_q, v_mla_g_kv, v_mla_w_uq, v_mla_w_ukv, v_mla_w_o, v_conv_w_in,
          v_conv_w, v_conv_w_out, v_mlp_w_up, v_mlp_w_down]
    gs = [g_b_mod, g_norm, big_grads[0], g_gq, g_gkv, big_grads[1], big_grads[2], big_grads[3], big_grads[4],
          g_convw, big_grads[5], big_grads[6], big_grads[7]]
    grads, deltas, new_ms, new_vs = [g_w_mod], [d_w_mod], [nm_w_mod], [nv_w_mod]
    for nm, w, g, m, v in zip(names, ws, gs, ms, vs):
        big = any(g is b for b in big_grads)
        d, nm_, nv_, *g_out = _adamw("adamw_" + nm, w, g, m, v, emit_grad=big)
        g = g_out[0] if big else g
        grads.append(g)
        deltas.append(d)
        new_ms.append(nm_)
        new_vs.append(nv_)
    return (loss, grad_x[None], *grads, *deltas, *new_ms, *new_vs)
```

```python
from typing import NamedTuple

import jax
import jax.numpy as jnp
from jax import lax
from jax.experimental import pallas as pl
from jax.experimental.pallas import tpu as pltpu

F32 = jnp.float32
BF16 = jnp.bfloat16
NORM_EPS = 1e-6
ROPE_THETA = 10000.0
QK_NOPE = 128
QK_ROPE = 64
V_HEAD = 128
LANES = 128
QK_PAD = QK_NOPE + LANES
ADAM_LR, ADAM_B1, ADAM_B2, ADAM_EPS, ADAM_WD, ADAM_STEP = 0.001, 0.9, 0.999, 1e-08, 0.01, 10
VMEM_LIMIT_BYTES = 56 * 1024 * 1024
N_CHIPS = 4
MESH_ID = pl.DeviceIdType.MESH
ANY = pl.BlockSpec(memory_space=pl.ANY)
NEG_INF = float("-inf")

DIMS_NN = (((1,), (0,)), ((), ()))
DIMS_NT = (((1,), (1,)), ((), ()))
DIMS_TN = (((0,), (0,)), ((), ()))


def _cparams(*sem):
    return pltpu.CompilerParams(dimension_semantics=sem, vmem_limit_bytes=VMEM_LIMIT_BYTES)


def _row_tile(rows, row_bytes, limit=2 * 1024 * 1024, mult=16):
    if rows * row_bytes <= limit or rows % mult:
        return rows
    best = mult
    t = mult
    while t <= rows:
        if rows % t == 0 and t * row_bytes <= limit:
            best = t
        t += mult
    return best


def _rms(v):
    return lax.rsqrt(jnp.mean(v * v, axis=-1, keepdims=True) + NORM_EPS)


class Rider(NamedTuple):
    operands: tuple
    out_shape: tuple
    aliases: dict
    sems: tuple
    start: object
    finish: object
    mid: object = None


NO_RIDER = Rider((), (), {}, (), None, None)


def _rider_end(rider, r_in, r_out, r_sems):
    if rider.mid is not None:
        rider.mid(r_in, r_out, r_sems)
    rider.finish(r_in, r_out, r_sems)


def _mm(name, a, b, mode, M, N, K, outs, *, a_spec=None, b_spec=None, out_specs=None, epilogue=None,
        extras=(), rider=NO_RIDER, tm=1024, tn=1024, tk=4096):
    tm, tn, tk = min(tm, M), min(tn, N), min(tk, K)
    assert M % tm == 0 and N % tn == 0 and K % tk == 0, (name, M, N, K)
    nk = K // tk
    if a_spec is None:
        a_spec = {"nn": pl.BlockSpec((tm, tk), lambda i, j, k: (i, k)),
                  "nt": pl.BlockSpec((tm, tk), lambda i, j, k: (i, k)),
                  "tn": pl.BlockSpec((tk, tm), lambda i, j, k: (k, i))}[mode]
    else:
        a_spec = a_spec(tm, tn, tk)
    if b_spec is None:
        b_spec = {"nn": pl.BlockSpec((tk, tn), lambda i, j, k: (k, j)),
                  "nt": pl.BlockSpec((tn, tk), lambda i, j, k: (j, k)),
                  "tn": pl.BlockSpec((tk, tn), lambda i, j, k: (k, j))}[mode]
    else:
        b_spec = b_spec(tm, tn, tk)
    if out_specs is None:
        out_specs = [pl.BlockSpec((tm, tn), lambda i, j, k: (i, j)) for _ in outs]
    else:
        out_specs = [s(tm, tn, tk) for s in out_specs]
    dims = {"nn": DIMS_NN, "nt": DIMS_NT, "tn": DIMS_TN}[mode]
    ne, no = len(extras), len(outs)
    n_ri, n_ro = len(rider.operands), len(rider.out_shape)
    grid = (M // tm, N // tn, nk)

    def body(*refs):
        a_ref, b_ref = refs[0], refs[1]
        ex = refs[2:2 + ne]
        r_in = refs[2 + ne:2 + ne + n_ri]
        o = refs[2 + ne + n_ri:2 + ne + n_ri + no]
        r_out = refs[2 + ne + n_ri + no:2 + ne + n_ri + no + n_ro]
        scratch = refs[2 + ne + n_ri + no + n_ro:]
        r_sems = scratch[1:] if nk > 1 else scratch
        ii, jj, kk = pl.program_id(0), pl.program_id(1), pl.program_id(2)

        if rider.start is not None:
            @pl.when((ii == 0) & (jj == 0) & (kk == 0))
            def _():
                rider.start(r_in, r_out, r_sems)

        part = lax.dot_general(a_ref[...].astype(BF16), b_ref[...].astype(BF16), dims,
                               preferred_element_type=F32)

        def finish(total):
            vals = epilogue(total, *[e[...] for e in ex]) if epilogue is not None else (total,)
            for r, v in zip(o, vals):
                r[...] = v.astype(r.dtype)

        if nk == 1:
            finish(part)
        else:
            acc = scratch[0]

            @pl.when(kk == 0)
            def _():
                acc[...] = part

            @pl.when(kk > 0)
            def _():
                acc[...] += part

            @pl.when(kk == nk - 1)
            def _():
                finish(acc[...])

        if rider.finish is not None:
            steps = grid[0] * grid[1] * nk
            if rider.mid is not None and steps >= 4:
                @pl.when((ii * grid[1] + jj) * nk + kk == steps // 2)
                def _():
                    rider.mid(r_in, r_out, r_sems)

            @pl.when((ii == grid[0] - 1) & (jj == grid[1] - 1) & (kk == nk - 1))
            def _():
                if rider.mid is not None and steps < 4:
                    rider.mid(r_in, r_out, r_sems)
                rider.finish(r_in, r_out, r_sems)

    operands = [a, b] + [e[0] for e in extras] + list(rider.operands)
    in_specs = [a_spec, b_spec] + [e[1](tm, tn, tk) for e in extras] + [ANY] * n_ri
    hosted = rider.start is not None
    res = pl.pallas_call(
        body, name=name, grid=grid,
        in_specs=in_specs, out_specs=out_specs + [ANY] * n_ro, out_shape=list(outs) + list(rider.out_shape),
        scratch_shapes=([pltpu.VMEM((tm, tn), F32)] if nk > 1 else []) + list(rider.sems),
        input_output_aliases={2 + ne + i: no + r for i, r in rider.aliases.items()},
        compiler_params=_cparams(*(("arbitrary",) * 3 if hosted else ("parallel", "parallel", "arbitrary"))),
    )(*operands)
    return (res[:no], res[no:]) if hosted else res


def _sds(shape, dtype):
    return jax.ShapeDtypeStruct(tuple(shape), dtype)


def _rope(t, cos_p, sin_lo, sin_hi):
    return t * cos_p + pltpu.roll(t, LANES - QK_ROPE // 2, 1) * sin_lo + pltpu.roll(t, QK_ROPE // 2, 1) * sin_hi


def _rope_t(d, cos_p, sin_lo, sin_hi):
    return d * cos_p + pltpu.roll(d * sin_lo, QK_ROPE // 2, 1) + pltpu.roll(d * sin_hi, LANES - QK_ROPE // 2, 1)


def _vec_spec(d):
    return pl.BlockSpec((1, d), lambda i: (0, 0))


def _fwd_boundary(name, x_prev, y, gate, ng_post, ng_pre, sc, sh):
    S, D = x_prev.shape
    ts = min(256, S)
    has_y = y is not None
    row = pl.BlockSpec((ts, D), lambda i: (i, 0))

    def body(*refs):
        if has_y:
            x_ref, y_ref, g_ref, ngp_ref, ngn_ref, sc_ref, sh_ref, xo_ref, h_ref = refs
            yv = y_ref[...].astype(F32)
            xn = x_ref[...] + g_ref[...] * (yv * _rms(yv) * ngp_ref[...])
            xo_ref[...] = xn
        else:
            x_ref, ngn_ref, sc_ref, sh_ref, h_ref = refs
            xn = x_ref[...]
        hn = xn * _rms(xn) * ngn_ref[...]
        h_ref[...] = (hn * (1.0 + sc_ref[...]) + sh_ref[...]).astype(BF16)

    vec = _vec_spec(D)
    if has_y:
        operands = (x_prev, y, gate, ng_post, ng_pre, sc, sh)
        in_specs = [row, row, vec, vec, vec, vec, vec]
        out_shape = [_sds((S, D), F32), _sds((S, D), BF16)]
        out_specs = [row, row]
    else:
        operands = (x_prev, ng_pre, sc, sh)
        in_specs = [row, vec, vec, vec]
        out_shape = [_sds((S, D), BF16)]
        out_specs = [row]
    return pl.pallas_call(body, name=name, grid=(S // ts,), in_specs=in_specs, out_specs=out_specs,
                          out_shape=out_shape, compiler_params=_cparams("parallel"))(*operands)


def _acc_rows(sums_ref, rows):
    for r, v in rows:
        sums_ref[r:r + 1, :] += jnp.sum(v, axis=0, keepdims=True)


def _post_norm_bwd(dxt, yv, gate, ng_post, sums_ref, dy_ref):
    r1 = _rms(yv)
    yhat = yv * r1
    dn = dxt * gate
    u = dn * ng_post
    dy = r1 * (u - yhat * jnp.mean(u * yhat, axis=-1, keepdims=True))
    dy_ref[...] = dy.astype(dy_ref.dtype)
    _acc_rows(sums_ref, [(3, dxt * (yhat * ng_post)), (4, dn * yhat)])


def _loss_boundary(name, x_prev, y, gate, ng_post, target):
    S, D = x_prev.shape
    ts = min(256, S)
    row = pl.BlockSpec((ts, D), lambda i: (i, 0))
    vec = _vec_spec(D)

    def body(x_ref, y_ref, g_ref, ngp_ref, t_ref, dx_ref, dy_ref, sums_ref, loss_ref):
        @pl.when(pl.program_id(0) == 0)
        def _():
            sums_ref[...] = jnp.zeros_like(sums_ref)
            loss_ref[...] = jnp.zeros_like(loss_ref)

        yv = y_ref[...].astype(F32)
        xf = x_ref[...] + g_ref[...] * (yv * _rms(yv) * ngp_ref[...])
        err = xf - t_ref[...]
        loss_ref[...] += 0.5 * jnp.sum(jnp.mean(err * err, axis=-1, keepdims=True))
        dxt = err / D
        dx_ref[...] = dxt
        _post_norm_bwd(dxt, yv, g_ref[...], ngp_ref[...], sums_ref, dy_ref)

    return pl.pallas_call(
        body, name=name, grid=(S // ts,),
        in_specs=[row, row, vec, vec, row],
        out_specs=[row, row, pl.BlockSpec((8, D), lambda i: (0, 0)), pl.BlockSpec((8, LANES), lambda i: (0, 0))],
        out_shape=[_sds((S, D), F32), _sds((S, D), BF16), _sds((8, D), F32), _sds((8, LANES), F32)],
        compiler_params=_cparams("arbitrary"))(x_prev, y, gate, ng_post, target)


def _bwd_boundary(name, dx_new, dh, x_new, y, gate, ng_post, ng_pre, sc, rider=NO_RIDER):
    S, D = x_new.shape
    ts = min(256, S)
    has_y = y is not None
    row = pl.BlockSpec((ts, D), lambda i: (i, 0))
    vec = _vec_spec(D)
    n_in, n_out = (8, 3) if has_y else (5, 2)
    n_ri, n_ro = len(rider.operands), len(rider.out_shape)

    def body(*refs):
        r_in = refs[n_in:n_in + n_ri]
        r_out = refs[n_in + n_ri + n_out:n_in + n_ri + n_out + n_ro]
        r_sems = refs[n_in + n_ri + n_out + n_ro:]
        own = refs[:n_in] + refs[n_in + n_ri:n_in + n_ri + n_out]
        if has_y:
            dxn_ref, dh_ref, x_ref, y_ref, g_ref, ngp_ref, ngn_ref, sc_ref, dxo_ref, dy_ref, sums_ref = own
        else:
            dxn_ref, dh_ref, x_ref, ngn_ref, sc_ref, dxo_ref, sums_ref = own

        @pl.when(pl.program_id(0) == 0)
        def _():
            sums_ref[...] = jnp.zeros_like(sums_ref)
            if rider.start is not None:
                rider.start(r_in, r_out, r_sems)

        xv = x_ref[...]
        dhv = dh_ref[...].astype(F32)
        ngn = ngn_ref[...]
        r2 = _rms(xv)
        xhat = xv * r2
        dn_pre = dhv * (1.0 + sc_ref[...])
        u2 = dn_pre * ngn
        dxt = dxn_ref[...] + r2 * (u2 - xhat * jnp.mean(u2 * xhat, axis=-1, keepdims=True))
        dxo_ref[...] = dxt
        _acc_rows(sums_ref, [(0, dhv), (1, dhv * (xhat * ngn)), (2, dn_pre * xhat)])
        if has_y:
            _post_norm_bwd(dxt, y_ref[...].astype(F32), g_ref[...], ngp_ref[...], sums_ref, dy_ref)

        if rider.finish is not None:
            @pl.when(pl.program_id(0) == S // ts - 1)
            def _():
                _rider_end(rider, r_in, r_out, r_sems)

    sums_spec = pl.BlockSpec((8, D), lambda i: (0, 0))
    if has_y:
        operands = (dx_new, dh, x_new, y, gate, ng_post, ng_pre, sc)
        in_specs = [row, row, row, row, vec, vec, vec, vec]
        out_shape = [_sds((S, D), F32), _sds((S, D), BF16), _sds((8, D), F32)]
        out_specs = [row, row, sums_spec]
    else:
        operands = (dx_new, dh, x_new, ng_pre, sc)
        in_specs = [row, row, row, vec, vec]
        out_shape = [_sds((S, D), F32), _sds((8, D), F32)]
        out_specs = [row, sums_spec]
    res = pl.pallas_call(
        body, name=name, grid=(S // ts,), in_specs=in_specs + [ANY] * n_ri, out_specs=out_specs + [ANY] * n_ro,
        out_shape=out_shape + list(rider.out_shape), scratch_shapes=list(rider.sems),
        input_output_aliases={n_in + i: n_out + o for i, o in rider.aliases.items()},
        compiler_params=_cparams("arbitrary"))(*operands, *rider.operands)
    return (*res[:n_out], res[n_out:])


def _latent_fwd(lat, g_q, g_kv, rope_tabs, rank):
    S, W = lat.shape
    ts = min(256, S)
    tab = pl.BlockSpec((ts, LANES), lambda i: (i, 0))

    def body(lat_ref, gq_ref, gkv_ref, cos_ref, slo_ref, shi_ref, cq_ref, ckv_ref, kr_ref):
        lq = lat_ref[:, 0:rank]
        lkv = lat_ref[:, rank:2 * rank]
        cq_ref[...] = (lq * _rms(lq) * gq_ref[...]).astype(BF16)
        ckv_ref[...] = (lkv * _rms(lkv) * gkv_ref[...]).astype(BF16)
        kr_ref[...] = _rope(lat_ref[:, 2 * rank:W], cos_ref[...], slo_ref[...], shi_ref[...]).astype(BF16)

    return pl.pallas_call(
        body, name="mla_latent_fwd", grid=(S // ts,),
        in_specs=[pl.BlockSpec((ts, W), lambda i: (i, 0)), _vec_spec(rank), _vec_spec(rank), tab, tab, tab],
        out_specs=[pl.BlockSpec((ts, rank), lambda i: (i, 0)), pl.BlockSpec((ts, rank), lambda i: (i, 0)), tab],
        out_shape=[_sds((S, rank), BF16), _sds((S, rank), BF16), _sds((S, LANES), BF16)],
        compiler_params=_cparams("parallel"))(lat, g_q, g_kv, *rope_tabs)


def _latent_bwd(lat, dcq, dckv, dkr, g_q, g_kv, rope_tabs, rank):
    S, W = lat.shape
    ts = min(256, S)
    tab = pl.BlockSpec((ts, LANES), lambda i: (i, 0))
    half = pl.BlockSpec((ts, rank), lambda i: (i, 0))

    def body(lat_ref, dcq_ref, dckv_ref, dkr_ref, gq_ref, gkv_ref, cos_ref, slo_ref, shi_ref, dlat_ref, sums_ref):
        @pl.when(pl.program_id(0) == 0)
        def _():
            sums_ref[...] = jnp.zeros_like(sums_ref)

        def norm_bwd(v, dn, g, r):
            rr = _rms(v)
            vhat = v * rr
            u = dn * g
            sums_ref[r:r + 1, :] += jnp.sum(dn * vhat, axis=0, keepdims=True)
            return rr * (u - vhat * jnp.mean(u * vhat, axis=-1, keepdims=True))

        dlat_ref[:, 0:rank] = norm_bwd(lat_ref[:, 0:rank], dcq_ref[...], gq_ref[...], 0).astype(BF16)
        dlat_ref[:, rank:2 * rank] = norm_bwd(lat_ref[:, rank:2 * rank], dckv_ref[...], gkv_ref[...], 1).astype(BF16)
        dlat_ref[:, 2 * rank:W] = _rope_t(dkr_ref[...], cos_ref[...], slo_ref[...], shi_ref[...]).astype(BF16)

    return pl.pallas_call(
        body, name="mla_latent_bwd", grid=(S // ts,),
        in_specs=[pl.BlockSpec((ts, W), lambda i: (i, 0)), half, half, tab, _vec_spec(rank), _vec_spec(rank),
                  tab, tab, tab],
        out_specs=[pl.BlockSpec((ts, W), lambda i: (i, 0)), pl.BlockSpec((8, rank), lambda i: (0, 0))],
        out_shape=[_sds((S, W), BF16), _sds((8, rank), F32)],
        compiler_params=_cparams("arbitrary"))(lat, dcq, dckv, dkr, g_q, g_kv, *rope_tabs)


def _attn_tiles(S):
    t = min(512, S)
    return t, S // t


def _causal_mask(t):
    return lax.broadcasted_iota(jnp.int32, (t, t), 1) <= lax.broadcasted_iota(jnp.int32, (t, t), 0)


def _causal_pairs(nb, q_major):
    if q_major:
        pairs = [(qi, ki) for qi in range(nb) for ki in range(qi + 1)]
    else:
        pairs = [(qi, ki) for ki in range(nb) for qi in range(ki, nb)]
    return jnp.array([p[0] for p in pairs], jnp.int32), jnp.array([p[1] for p in pairs], jnp.int32), len(pairs)


def _heads_per_step(heads):
    return 2 if heads % 2 == 0 else 1


def _attn_fwd_tri(q, kv, kr, heads, scale, rider=NO_RIDER):
    S = q.shape[0]
    t, nb = _attn_tiles(S)
    G = _heads_per_step(heads)
    q_tab, k_tab, n_pairs = _causal_pairs(nb, True)
    n_ri, n_ro = len(rider.operands), len(rider.out_shape)

    def body(qt_ref, kt_ref, *refs):
        q_ref, kv_ref, kr_ref = refs[:3]
        r_in = refs[3:3 + n_ri]
        o_ref, lse_ref = refs[3 + n_ri:5 + n_ri]
        r_out = refs[5 + n_ri:5 + n_ri + n_ro]
        m_scr, acc_scr = refs[5 + n_ri + n_ro:7 + n_ri + n_ro]
        r_sems = refs[7 + n_ri + n_ro:]
        h, p = pl.program_id(0), pl.program_id(1)
        qi, ki = qt_ref[p], kt_ref[p]

        if rider.start is not None:
            @pl.when((h == 0) & (p == 0))
            def _():
                rider.start(r_in, r_out, r_sems)

        @pl.when(ki == 0)
        def _():
            m_scr[...] = jnp.full_like(m_scr, NEG_INF)
            acc_scr[...] = jnp.zeros_like(acc_scr)

        def step(diagonal):
            ones = jnp.ones((t, LANES), BF16)
            for g in range(G):
                kcat = jnp.concatenate([kv_ref[:, g * QK_PAD:g * QK_PAD + QK_NOPE], kr_ref[...]], axis=1)
                vext = jnp.concatenate([kv_ref[:, g * QK_PAD + QK_NOPE:(g + 1) * QK_PAD], ones], axis=1)
                s = lax.dot_general(q_ref[:, g * QK_PAD:(g + 1) * QK_PAD], kcat, DIMS_NT,
                                    preferred_element_type=F32) * scale
                if diagonal:
                    s = jnp.where(_causal_mask(t), s, NEG_INF)
                m_prev = m_scr[g]
                m_new = jnp.maximum(m_prev, jnp.max(s, axis=-1, keepdims=True))
                alpha = jnp.exp(m_prev - m_new)
                pr = jnp.exp(s - jnp.tile(m_new, (1, t // LANES)))
                acc_scr[g] = jnp.tile(alpha, (1, 2)) * acc_scr[g] + lax.dot_general(
                    pr.astype(BF16), vext, DIMS_NN, preferred_element_type=F32)
                m_scr[g] = m_new

        @pl.when(ki < qi)
        def _():
            step(False)

        @pl.when(ki == qi)
        def _():
            step(True)
            for g in range(G):
                acc = acc_scr[g]
                o_ref[:, g * V_HEAD:(g + 1) * V_HEAD] = (acc[:, 0:V_HEAD] / acc[:, V_HEAD:2 * V_HEAD]).astype(BF16)
                lse_ref[g] = m_scr[g] + jnp.log(acc[:, V_HEAD:2 * V_HEAD])

        if rider.finish is not None:
            halfway = rider.mid is not None and heads // G >= 2
            if halfway:
                @pl.when((h == heads // G // 2) & (p == 0))
                def _():
                    rider.mid(r_in, r_out, r_sems)

            @pl.when((h == heads // G - 1) & (p == n_pairs - 1))
            def _():
                if halfway:
                    rider.finish(r_in, r_out, r_sems)
                else:
                    _rider_end(rider, r_in, r_out, r_sems)

    res = pl.pallas_call(
        body, name="mla_attn_fwd",
        grid_spec=pltpu.PrefetchScalarGridSpec(
            num_scalar_prefetch=2, grid=(heads // G, n_pairs),
            in_specs=[pl.BlockSpec((t, G * QK_PAD), lambda h, p, qt, kt: (qt[p], h)),
                      pl.BlockSpec((t, G * QK_PAD), lambda h, p, qt, kt: (kt[p], h)),
                      pl.BlockSpec((t, LANES), lambda h, p, qt, kt: (kt[p], 0))] + [ANY] * n_ri,
            out_specs=[pl.BlockSpec((t, G * V_HEAD), lambda h, p, qt, kt: (qt[p], h)),
                       pl.BlockSpec((G, t, LANES), lambda h, p, qt, kt: (h, qt[p], 0))] + [ANY] * n_ro,
            scratch_shapes=[pltpu.VMEM((G, t, LANES), F32), pltpu.VMEM((G, t, 2 * V_HEAD), F32)] + list(rider.sems)),
        out_shape=[_sds((S, heads * V_HEAD), BF16), _sds((heads, S, LANES), F32)] + list(rider.out_shape),
        input_output_aliases={5 + i: 2 + o for i, o in rider.aliases.items()},
        compiler_params=_cparams("arbitrary", "arbitrary"))(q_tab, k_tab, q, kv, kr, *rider.operands)
    return res[0], res[1], res[2:]


def _attn_bwd_tri(q, kv, kr, o, do, lse, rope_tabs, heads, scale, rider=NO_RIDER):
    S = q.shape[0]
    t, nb = _attn_tiles(S)
    G = _heads_per_step(heads)
    q_tab, k_tab, n_pairs = _causal_pairs(nb, False)
    n_ri, n_ro = len(rider.operands), len(rider.out_shape)
    rep = t // LANES

    tabs = jnp.concatenate(rope_tabs, axis=1)

    def body(qt_ref, kt_ref, *refs):
        q_ref, kv_ref, kr_ref, o_ref, do_ref, lse_ref, tabs_ref = refs[:7]
        cos_ref, slo_ref, shi_ref = (tabs_ref.at[:, pl.ds(i * LANES, LANES)] for i in range(3))
        r_in = refs[7:7 + n_ri]
        dq_ref, dkv_ref, dkr_ref = refs[7 + n_ri:10 + n_ri]
        r_out = refs[10 + n_ri:10 + n_ri + n_ro]
        dq_scr, dk_scr, dv_scr, dkr_scr, delta_scr = refs[10 + n_ri + n_ro:15 + n_ri + n_ro]
        r_sems = refs[15 + n_ri + n_ro:]
        h, p = pl.program_id(0), pl.program_id(1)
        qi, ki = qt_ref[p], kt_ref[p]
        q_rows = pl.ds(pl.multiple_of(qi * t, t), t)
        k_rows = pl.ds(pl.multiple_of(ki * t, t), t)

        @pl.when(ki == 0)
        def _():
            for g in range(G):
                cols = slice(g * V_HEAD, (g + 1) * V_HEAD)
                d = jnp.sum(do_ref[:, cols].astype(F32) * o_ref[:, cols].astype(F32), axis=-1, keepdims=True)
                delta_scr[g, q_rows, :] = jnp.broadcast_to(d, (t, LANES))

        if rider.start is not None:
            @pl.when((h == 0) & (p == 0))
            def _():
                rider.start(r_in, r_out, r_sems)

        @pl.when(p == 0)
        def _():
            dq_scr[...] = jnp.zeros_like(dq_scr)

        @pl.when((h == 0) & (p == 0))
        def _():
            dkr_scr[...] = jnp.zeros_like(dkr_scr)

        @pl.when(qi == ki)
        def _():
            dk_scr[...] = jnp.zeros_like(dk_scr)
            dv_scr[...] = jnp.zeros_like(dv_scr)

        def step(diagonal):
            for g in range(G):
                qv = q_ref[:, g * QK_PAD:(g + 1) * QK_PAD]
                kcat = jnp.concatenate([kv_ref[:, g * QK_PAD:g * QK_PAD + QK_NOPE], kr_ref[...]], axis=1)
                s = lax.dot_general(qv, kcat, DIMS_NT, preferred_element_type=F32) * scale
                pr = jnp.exp(s - jnp.tile(lse_ref[g], (1, rep)))
                if diagonal:
                    pr = jnp.where(_causal_mask(t), pr, 0.0)
                dov = do_ref[:, g * V_HEAD:(g + 1) * V_HEAD]
                dv_scr[g] += lax.dot_general(pr.astype(BF16), dov, DIMS_TN, preferred_element_type=F32)
                dp = lax.dot_general(dov, kv_ref[:, g * QK_PAD + QK_NOPE:(g + 1) * QK_PAD], DIMS_NT,
                                     preferred_element_type=F32)
                ds = (pr * (dp - jnp.tile(delta_scr[g, q_rows, :], (1, rep))) * scale).astype(BF16)
                dk_scr[g] += lax.dot_general(ds, qv, DIMS_TN, preferred_element_type=F32)
                dq_scr[q_rows, g * QK_PAD:(g + 1) * QK_PAD] += lax.dot_general(ds, kcat, DIMS_NN,
                                                                               preferred_element_type=F32)

        @pl.when(qi > ki)
        def _():
            step(False)

        @pl.when(qi == ki)
        def _():
            step(True)
            for g in range(G):
                dqv = dq_scr[q_rows, g * QK_PAD:(g + 1) * QK_PAD]
                dq_ref[q_rows, g * QK_PAD:(g + 1) * QK_PAD] = jnp.concatenate(
                    [dqv[:, 0:QK_NOPE], _rope_t(dqv[:, QK_NOPE:QK_PAD], cos_ref[...], slo_ref[...], shi_ref[...])],
                    axis=1).astype(BF16)

        @pl.when(qi == nb - 1)
        def _():
            for g in range(G):
                dkv_ref[:, g * QK_PAD:(g + 1) * QK_PAD] = jnp.concatenate(
                    [dk_scr[g][:, 0:QK_NOPE], dv_scr[g]], axis=1).astype(BF16)
                dkr_scr[k_rows, :] += dk_scr[g][:, QK_NOPE:QK_PAD]

        @pl.when((h == heads // G - 1) & (p == n_pairs - 1))
        def _():
            dkr_ref[...] = dkr_scr[...]
            if rider.finish is not None:
                _rider_end(rider, r_in, r_out, r_sems)

    q_blk = lambda w: pl.BlockSpec((t, G * w), lambda h, p, qt, kt: (qt[p], h))
    stat = pl.BlockSpec((G, t, LANES), lambda h, p, qt, kt: (h, qt[p], 0))
    tab = pl.BlockSpec((t, LANES), lambda h, p, qt, kt: (kt[p], 0))
    res = pl.pallas_call(
        body, name="mla_attn_bwd",
        grid_spec=pltpu.PrefetchScalarGridSpec(
            num_scalar_prefetch=2, grid=(heads // G, n_pairs),
            in_specs=[q_blk(QK_PAD),
                      pl.BlockSpec((t, G * QK_PAD), lambda h, p, qt, kt: (kt[p], h)),
                      tab, q_blk(V_HEAD), q_blk(V_HEAD), stat,
                      pl.BlockSpec((t, 3 * LANES), lambda h, p, qt, kt: (kt[p], 0))] + [ANY] * n_ri,
            out_specs=[pl.BlockSpec((S, G * QK_PAD), lambda h, p, qt, kt: (0, h)),
                       pl.BlockSpec((t, G * QK_PAD), lambda h, p, qt, kt: (kt[p], h)),
                       pl.BlockSpec((S, LANES), lambda h, p, qt, kt: (0, 0))] + [ANY] * n_ro,
            scratch_shapes=[pltpu.VMEM((S, G * QK_PAD), F32), pltpu.VMEM((G, t, QK_PAD), F32),
                            pltpu.VMEM((G, t, V_HEAD), F32), pltpu.VMEM((S, LANES), F32),
                            pltpu.VMEM((G, S, LANES), F32)] + list(rider.sems)),
        out_shape=[_sds((S, heads * QK_PAD), BF16), _sds((S, heads * QK_PAD), BF16), _sds((S, LANES), F32)]
        + list(rider.out_shape),
        input_output_aliases={9 + i: 3 + o for i, o in rider.aliases.items()},
        compiler_params=_cparams("arbitrary", "arbitrary"))(q_tab, k_tab, q, kv, kr, o, do, lse, tabs,
                                                            *rider.operands)
    return res[0], res[1], res[2], res[3:]


def _shift_down(z, n, rows):
    return jnp.where(rows >= n, pltpu.roll(z, n, 0), 0.0)


def _shift_up(z, n, rows, S):
    return jnp.where(rows < S - n, pltpu.roll(z, S - n, 0), 0.0)


def _conv_specs(S, tc):
    strip = lambda p: pl.BlockSpec((None, S, tc), lambda j: (p, 0, j))
    return strip(0), strip(1), strip(2), pl.BlockSpec((3, tc), lambda j: (0, j))


def _conv_fwd(proj3, w):
    _, S, D = proj3.shape
    tc = LANES

    def body(b_ref, c_ref, u_ref, w_ref, out_ref):
        z = c_ref[...].astype(F32) * u_ref[...].astype(F32)
        rows = lax.broadcasted_iota(jnp.int32, (S, tc), 0)
        zc = w_ref[0:1, :] * _shift_down(z, 2, rows) + w_ref[1:2, :] * _shift_down(z, 1, rows) + w_ref[2:3, :] * z
        out_ref[...] = (b_ref[...].astype(F32) * zc).astype(BF16)

    return pl.pallas_call(
        body, name="conv_fwd", grid=(D // tc,), in_specs=list(_conv_specs(S, tc)),
        out_specs=pl.BlockSpec((S, tc), lambda j: (0, j)), out_shape=_sds((S, D), BF16),
        compiler_params=_cparams("parallel"))(proj3, proj3, proj3, w)


def _conv_bwd(dbz, proj3, w):
    _, S, D = proj3.shape
    tc = LANES

    def body(d_ref, b_ref, c_ref, u_ref, w_ref, dp_ref, dw_ref):
        cv, uv, dv = c_ref[...].astype(F32), u_ref[...].astype(F32), d_ref[...].astype(F32)
        z = cv * uv
        rows = lax.broadcasted_iota(jnp.int32, (S, tc), 0)
        z1, z2 = _shift_down(z, 1, rows), _shift_down(z, 2, rows)
        zc = w_ref[0:1, :] * z2 + w_ref[1:2, :] * z1 + w_ref[2:3, :] * z
        dp_ref[0] = (dv * zc).astype(BF16)
        dzc = dv * b_ref[...].astype(F32)
        dz = w_ref[2:3, :] * dzc + w_ref[1:2, :] * _shift_up(dzc, 1, rows, S) + w_ref[0:1, :] * _shift_up(dzc, 2, rows, S)
        dp_ref[1] = (dz * uv).astype(BF16)
        dp_ref[2] = (dz * cv).astype(BF16)
        dw_ref[0:1, :] = jnp.sum(dzc * z2, axis=0, keepdims=True)
        dw_ref[1:2, :] = jnp.sum(dzc * z1, axis=0, keepdims=True)
        dw_ref[2:3, :] = jnp.sum(dzc * z, axis=0, keepdims=True)

    sb, sc_, su, sw = _conv_specs(S, tc)
    return pl.pallas_call(
        body, name="conv_bwd", grid=(D // tc,),
        in_specs=[pl.BlockSpec((S, tc), lambda j: (0, j)), sb, sc_, su, sw],
        out_specs=[pl.BlockSpec((3, S, tc), lambda j: (0, 0, j)), pl.BlockSpec((3, tc), lambda j: (0, j))],
        out_shape=[_sds((3, S, D), BF16), _sds((3, D), F32)],
        compiler_params=_cparams("parallel"))(dbz, proj3, proj3, proj3, w)


def _silu(c_all):
    def body(c_ref, o_ref):
        cv = c_ref[...]
        o_ref[...] = cv * (1.0 / (1.0 + jnp.exp(-cv)))

    vm = pl.BlockSpec(memory_space=pltpu.VMEM)
    return pl.pallas_call(body, name="cond_silu", in_specs=[vm], out_specs=vm, out_shape=_sds(c_all.shape, F32))(c_all)


def _mod_fwd(cond, w_mod, b_cols):
    L, D, ncol = w_mod.shape
    B = cond.shape[0]
    tk, tn = min(512, D), min(1024, ncol)
    nk = D // tk

    def body(c_ref, w_ref, b_ref, out_ref, acc):
        kk = pl.program_id(2)
        part = lax.dot_general(c_ref[...].astype(BF16), w_ref[...].astype(BF16), DIMS_NN, preferred_element_type=F32)

        @pl.when(kk == 0)
        def _():
            acc[...] = part

        @pl.when(kk > 0)
        def _():
            acc[...] += part

        @pl.when(kk == nk - 1)
        def _():
            out_ref[...] = acc[...] + b_ref[...]

    return pl.pallas_call(
        body, name="mod_fwd", grid=(L, ncol // tn, nk),
        in_specs=[pl.BlockSpec((B, tk), lambda l, j, k: (0, k)),
                  pl.BlockSpec((None, tk, tn), lambda l, j, k: (l, k, j)),
                  pl.BlockSpec((None, 1, tn), lambda l, j, k: (l, 0, j))],
        out_specs=pl.BlockSpec((None, B, tn), lambda l, j, k: (l, 0, j)),
        out_shape=_sds((L, B, ncol), F32),
        scratch_shapes=[pltpu.VMEM((B, tn), F32)],
        compiler_params=_cparams("parallel", "parallel", "arbitrary"))(cond, w_mod, b_cols)


def _adamw_math(w, g, m, v):
    m = ADAM_B1 * m + (1.0 - ADAM_B1) * g
    v = ADAM_B2 * v + (1.0 - ADAM_B2) * (g * g)
    m_hat = m / (1.0 - ADAM_B1 ** ADAM_STEP)
    v_hat = v / (1.0 - ADAM_B2 ** ADAM_STEP)
    delta = -ADAM_LR * (m_hat / (jnp.sqrt(v_hat) + ADAM_EPS) + ADAM_WD * w)
    return delta, m, v


def _adamw(name, w, g, m, v, emit_grad=False):
    shape = w.shape
    cols = shape[-1] if w.ndim <= 3 else shape[-2] * shape[-1]
    rows = w.size // cols
    w2, g2, m2, v2 = (t.reshape(rows, cols) for t in (w, g, m, v))
    tr = _row_tile(rows, cols * 4, limit=2 * 1024 * 1024, mult=8)
    spec = pl.BlockSpec((tr, cols), lambda i: (i, 0))
    n_out = 4 if emit_grad else 3

    def body(w_ref, g_ref, m_ref, v_ref, d_ref, nm_ref, nv_ref, *rest):
        gv = g_ref[...]
        d, nm, nv = _adamw_math(w_ref[...], gv, m_ref[...], v_ref[...])
        d_ref[...] = d
        nm_ref[...] = nm
        nv_ref[...] = nv
        if emit_grad:
            rest[0][...] = gv

    outs = pl.pallas_call(body, name=name, grid=(rows // tr,), in_specs=[spec] * 4, out_specs=[spec] * n_out,
                          out_shape=[_sds((rows, cols), F32)] * n_out, compiler_params=_cparams("parallel"))(w2, g2, m2, v2)
    return tuple(t.reshape(shape) for t in outs)


def _adamw_mod(w, cond_t, dmod_cols, m, v, rider=NO_RIDER):
    L, D, ncol = w.shape
    B = cond_t.shape[1]
    tr, tc = min(256, D), min(1024, ncol)
    blk = pl.BlockSpec((None, tr, tc), lambda l, i, j: (l, i, j))
    grid = (L, D // tr, ncol // tc)
    n_ri, n_ro = len(rider.operands), len(rider.out_shape)

    def body(*refs):
        w_ref, ct_ref, dm_ref, m_ref, v_ref = refs[:5]
        r_in = refs[5:5 + n_ri]
        g_ref, d_ref, nm_ref, nv_ref = refs[5 + n_ri:9 + n_ri]
        r_out = refs[9 + n_ri:9 + n_ri + n_ro]
        r_sems = refs[9 + n_ri + n_ro:]
        ids = [pl.program_id(a) for a in range(3)]

        if rider.start is not None:
            @pl.when((ids[0] == 0) & (ids[1] == 0) & (ids[2] == 0))
            def _():
                rider.start(r_in, r_out, r_sems)

        g = lax.dot_general(ct_ref[...], dm_ref[...], DIMS_NN, precision=lax.Precision.HIGHEST,
                            preferred_element_type=F32)
        d, nm, nv = _adamw_math(w_ref[...], g, m_ref[...], v_ref[...])
        g_ref[...] = g
        d_ref[...] = d
        nm_ref[...] = nm
        nv_ref[...] = nv

        if rider.finish is not None:
            @pl.when((ids[0] == grid[0] - 1) & (ids[1] == grid[1] - 1) & (ids[2] == grid[2] - 1))
            def _():
                _rider_end(rider, r_in, r_out, r_sems)

    hosted = rider.start is not None
    res = pl.pallas_call(
        body, name="adamw_w_mod", grid=grid,
        in_specs=[blk, pl.BlockSpec((tr, B), lambda l, i, j: (i, 0)),
                  pl.BlockSpec((None, B, tc), lambda l, i, j: (l, 0, j)), blk, blk] + [ANY] * n_ri,
        out_specs=[blk] * 4 + [ANY] * n_ro, out_shape=[_sds((L, D, ncol), F32)] * 4 + list(rider.out_shape),
        scratch_shapes=list(rider.sems), input_output_aliases={5 + i: 4 + o for i, o in rider.aliases.items()},
        compiler_params=_cparams(*(("arbitrary",) * 3 if hosted else ("parallel",) * 3)))(
            w, cond_t, dmod_cols, m, v, *rider.operands)
    return (*res[:4], res[4:])


def _cast_into_full(name, ws, kinds, k_idx, rider=NO_RIDER):
    L, R, C = ws[0].shape
    assert all(w.shape == (L, R, C) for w in ws)
    n = len(ws)
    Rh = R // 2
    tr = _row_tile(Rh, C * 4)
    grid = (L, 2, Rh // tr)
    out_shape, out_specs = [], []
    for kind in kinds:
        if kind == "row":
            out_shape.append(_sds((L, N_CHIPS, 2, Rh, C), BF16))
            out_specs.append(pl.BlockSpec((None, None, None, tr, C), lambda l, h, i, k_ref: (l, k_ref[0], h, i, 0)))
        else:
            out_shape.append(_sds((L, 2, Rh, N_CHIPS * C), BF16))
            out_specs.append(pl.BlockSpec((None, None, tr, C), lambda l, h, i, k_ref: (l, h, i, k_ref[0])))
    n_ri, n_ro = len(rider.operands), len(rider.out_shape)

    def body(k_ref, *refs):
        r_in = refs[n:n + n_ri]
        r_out = refs[2 * n + n_ri:2 * n + n_ri + n_ro]
        r_sems = refs[2 * n + n_ri + n_ro:]
        ids = [pl.program_id(a) for a in range(3)]
        if rider.start is not None:
            @pl.when((ids[0] == 0) & (ids[1] == 0) & (ids[2] == 0))
            def _():
                rider.start(r_in, r_out, r_sems)
        for a in range(n):
            refs[n + n_ri + a][...] = refs[a][...].astype(BF16)
        if rider.finish is not None:
            @pl.when((ids[0] == grid[0] - 1) & (ids[1] == grid[1] - 1) & (ids[2] == grid[2] - 1))
            def _():
                _rider_end(rider, r_in, r_out, r_sems)

    hosted = rider.start is not None
    res = pl.pallas_call(
        body, name=name,
        grid_spec=pltpu.PrefetchScalarGridSpec(
            num_scalar_prefetch=1, grid=grid,
            in_specs=[pl.BlockSpec((None, None, tr, C), lambda l, h, i, k_ref: (l, h, i, 0))] * n + [ANY] * n_ri,
            out_specs=out_specs + [ANY] * n_ro, scratch_shapes=list(rider.sems)),
        out_shape=out_shape + list(rider.out_shape),
        input_output_aliases={1 + n + i: n + o for i, o in rider.aliases.items()},
        compiler_params=_cparams(*(("arbitrary",) * 3 if hosted else ("parallel",) * 3)))(
            k_idx, *[w.reshape(L, 2, Rh, C) for w in ws], *rider.operands)
    return res[:n], res[n:]


def _pair_sum(name, g5, ra, c_idx):
    L, A, _, Rh, Cc = g5.shape
    tr = _row_tile(Rh, Cc * 4)

    def body(c_ref, g_ref, r_ref, o_ref):
        o_ref[...] = (g_ref[...].astype(F32) + r_ref[...].astype(F32)).astype(BF16)

    blk = pl.BlockSpec((None, None, tr, Cc), lambda l, a, i, c_ref: (l, a, i, 0))
    return pl.pallas_call(
        body, name=name,
        grid_spec=pltpu.PrefetchScalarGridSpec(
            num_scalar_prefetch=1, grid=(L, A, Rh // tr),
            in_specs=[pl.BlockSpec((None, None, None, tr, Cc), lambda l, a, i, c_ref: (l, a, c_ref[0], i, 0)), blk],
            out_specs=blk),
        out_shape=_sds((L, A, Rh, Cc), BF16),
        compiler_params=_cparams("parallel", "parallel", "parallel"))(c_idx, g5, ra)


def _chip_sum(name, p, rb, kc_idx, kind, layer=0, n_layers=1, prev=None):
    _, A, Rh, Cc = p.shape
    C = rb.shape[-1]
    tr = _row_tile(Rh, C * 4)
    if kind == "row":
        own = pl.BlockSpec((None, None, tr, C), lambda i, kc: (0, kc[0], i, 0))
    else:
        own = pl.BlockSpec((None, None, tr, C), lambda i, kc: (0, 0, i, kc[0]))
    peer = lambda j: pl.BlockSpec((None, None, tr, C), lambda i, kc: (j, 0, i, 0))

    def body(kc_ref, p_ref, r0_ref, r1_ref, r2_ref, *rest):
        o_ref = rest[-1]
        o_ref[...] = ((p_ref[...].astype(F32) + r0_ref[...].astype(F32)) + r1_ref[...].astype(F32)) + r2_ref[...].astype(F32)

    operands = [kc_idx, p, rb, rb, rb] + ([prev] if prev is not None else [])
    return pl.pallas_call(
        body, name=name,
        grid_spec=pltpu.PrefetchScalarGridSpec(
            num_scalar_prefetch=1, grid=(Rh // tr,),
            in_specs=[own, peer(0), peer(1), peer(2)] + ([ANY] if prev is not None else []),
            out_specs=pl.BlockSpec((None, None, tr, C), lambda i, kc: (layer, kc[1], i, 0))),
        out_shape=_sds((n_layers, 2, Rh, C), F32),
        input_output_aliases={5: 0} if prev is not None else {},
        compiler_params=_cparams("parallel"))(*operands)


def _mesh_place():
    x, y, c = lax.axis_index("x"), lax.axis_index("y"), lax.axis_index("c")
    chips = [(1 - x, y), (x, 1 - y), (1 - x, 1 - y)]
    return x, y, c, chips


def _remote(src, dst, send_sem, recv_sem, to):
    return pltpu.make_async_remote_copy(src_ref=src, dst_ref=dst, send_sem=send_sem, recv_sem=recv_sem,
                                        device_id=to, device_id_type=MESH_ID)


def _small_allgather(name, v, with_sum=False, rider=NO_RIDER):
    R, N = v.shape
    n_ri, n_ro, n_own = len(rider.operands), len(rider.out_shape), 2 if with_sum else 1

    def body(*refs):
        r_in = refs[1:1 + n_ri]
        r_out = refs[1 + n_ri + n_own:1 + n_ri + n_own + n_ro]
        r_sems = refs[1 + n_ri + n_own + n_ro + 3:]
        own = (refs[0],) + refs[1 + n_ri:1 + n_ri + n_own] + refs[1 + n_ri + n_own + n_ro:1 + n_ri + n_own + n_ro + 3]
        if with_sum:
            x_ref, out_ref, sum_ref, send_sems, recv_sems, local_sem = own
        else:
            x_ref, out_ref, send_sems, recv_sems, local_sem = own
        if rider.start is not None:
            rider.start(r_in, r_out, r_sems)
        x, y, c, chips = _mesh_place()
        me, sibling = (x, y, c), (x, y, 1 - c)

        def rows(px, py, pc):
            return out_ref.at[pl.ds((4 * px + 2 * py + pc) * R, R), :]

        def copy(k, block, to, src=None):
            return _remote(rows(*block) if src is None else src, rows(*block), send_sems.at[k], recv_sems.at[k], to)

        mine = pltpu.make_async_copy(x_ref, rows(*me), local_sem)
        mine.start()
        first = [copy(0, me, sibling, src=x_ref)]
        first += [copy(1 + j, me, (*chip, c), src=x_ref) for j, chip in enumerate(chips)]
        for cp in first:
            cp.start()
        passed = [copy(4 + j, (*chip, c), sibling) for j, chip in enumerate(chips)]
        for j, chip in enumerate(chips):
            copy(1 + j, (*chip, c), me).wait_recv()
            passed[j].start()
        copy(0, sibling, me).wait_recv()
        for j, chip in enumerate(chips):
            copy(4 + j, (*chip, 1 - c), me).wait_recv()
        for cp in first + passed:
            cp.wait_send()
        mine.wait()
        if with_sum:
            total = out_ref[0:R, :]
            for p in range(1, 8):
                total = total + out_ref[p * R:(p + 1) * R, :]
            sum_ref[...] = total
        if rider.finish is not None:
            _rider_end(rider, r_in, r_out, r_sems)

    vm = pl.BlockSpec(memory_space=pltpu.VMEM)
    out_shape = [_sds((8 * R, N), F32)] + ([_sds((R, N), F32)] if with_sum else [])
    res = pl.pallas_call(
        body, name=name, out_shape=out_shape + list(rider.out_shape), in_specs=[vm] + [ANY] * n_ri,
        out_specs=[vm] * n_own + [ANY] * n_ro,
        scratch_shapes=[pltpu.SemaphoreType.DMA((7,)), pltpu.SemaphoreType.DMA((7,)), pltpu.SemaphoreType.DMA]
        + list(rider.sems),
        input_output_aliases={1 + i: n_own + o for i, o in rider.aliases.items()},
        compiler_params=pltpu.CompilerParams(vmem_limit_bytes=VMEM_LIMIT_BYTES))(v, *rider.operands)
    if rider.start is not None:
        return (*res[:n_own], res[n_own:])
    return res if with_sum else res[0]


def _full_place(ref, kind, C, kk, half, layer=None):
    lead = slice(None) if layer is None else pl.ds(layer, 1)
    if kind == "row":
        return ref.at[lead, kk, half]
    return ref.at[lead, half, :, pl.ds(pl.multiple_of(kk * C, LANES), C)]


def _gather_rider(fulls, kinds, shard_cols, layers=None, peers=(0, 1, 2)):
    n = len(fulls)
    layers = layers or [None] * n
    rows = [f.shape[3] if kind == "row" else f.shape[2] for f, kind in zip(fulls, kinds)]
    n_chunks = 2 if all(r % 32 == 0 for r in rows) else 1

    def copies(outs, sems):
        x, y, c, chips = _mesh_place()
        k = 2 * x + y

        def place(a, kk, half, ch):
            rc = rows[a] // n_chunks
            return _full_place(outs[a], kinds[a], shard_cols[a], kk, half, layers[a]).at[:, pl.ds(ch * rc, rc), :]

        def copy(a, j, ch, ref, to):
            s = 6 * (n_chunks * a + ch) + j
            return _remote(ref, ref, sems[0].at[s], sems[1].at[s], to)

        return (x, y, c), [(j, chip) for j, chip in enumerate(chips) if j in peers], k, place, copy

    def start(_, outs, sems):
        (x, y, c), chips, k, place, copy = copies(outs, sems)
        for ch in range(n_chunks):
            for j, chip in chips:
                for a in range(n):
                    copy(a, j, ch, place(a, k, c, ch), (*chip, c)).start()

    def pass_on(outs, sems, ch):
        (x, y, c), chips, k, place, copy = copies(outs, sems)
        for j, chip in chips:
            kj = 2 * chip[0] + chip[1]
            for a in range(n):
                copy(a, j, ch, place(a, kj, c, ch), (x, y, c)).wait_recv()
                copy(a, 3 + j, ch, place(a, kj, c, ch), (x, y, 1 - c)).start()

    def mid(_, outs, sems):
        pass_on(outs, sems, 0)

    def finish(_, outs, sems):
        pass_on(outs, sems, n_chunks - 1)
        (x, y, c), chips, k, place, copy = copies(outs, sems)
        for ch in range(n_chunks):
            for j, chip in chips:
                kj = 2 * chip[0] + chip[1]
                for a in range(n):
                    copy(a, 3 + j, ch, place(a, kj, 1 - c, ch), (x, y, c)).wait_recv()
        for ch in range(n_chunks):
            for j, chip in chips:
                kj = 2 * chip[0] + chip[1]
                for a in range(n):
                    copy(a, j, ch, place(a, k, c, ch), (*chip, c)).wait_send()
                    copy(a, 3 + j, ch, place(a, kj, c, ch), (x, y, 1 - c)).wait_send()

    n_sems = 6 * n * n_chunks
    return Rider(tuple(fulls), tuple(_sds(f.shape, BF16) for f in fulls), {a: a for a in range(n)},
                 (pltpu.SemaphoreType.DMA((n_sems,)), pltpu.SemaphoreType.DMA((n_sems,))), start, finish,
                 mid if n_chunks == 2 else None)


def _scatter_rider(ps, kinds, shard_cols, peers=(0, 1, 2), into=None):
    n = len(ps)

    def copies(ins, outs, sems):
        x, y, c, chips = _mesh_place()
        cps = []
        for j, chip in enumerate(chips):
            if j not in peers:
                continue
            kj = 2 * chip[0] + chip[1]
            for a in range(n):
                C = shard_cols[a]
                src = ins[a].at[:, kj] if kinds[a] == "row" else ins[a].at[:, 0, :, pl.ds(pl.multiple_of(kj * C, LANES), C)]
                cps.append(_remote(src, outs[a].at[j], sems[0].at[3 * a + j], sems[1].at[3 * a + j], (*chip, c)))
        return cps

    def start(ins, outs, sems):
        for cp in copies(ins, outs, sems):
            cp.start()

    def finish(ins, outs, sems):
        cps = copies(ins, outs, sems)
        for cp in cps:
            cp.wait_recv()
        for cp in cps:
            cp.wait_send()

    out_shape = tuple(_sds((3, p.shape[0], p.shape[2], C), BF16) for p, C in zip(ps, shard_cols))
    aliases = {n + a: a for a in range(n)} if into is not None else {}
    return Rider(tuple(ps) + tuple(into or ()), out_shape, aliases,
                 (pltpu.SemaphoreType.DMA((3 * n,)), pltpu.SemaphoreType.DMA((3 * n,))), start, finish)


def _run_rider(name, rider):
    n_in, n_out = len(rider.operands), len(rider.out_shape)

    def body(*refs):
        ins, outs, sems = refs[:n_in], refs[n_in:n_in + n_out], refs[n_in + n_out:]
        rider.start(ins, outs, sems)
        _rider_end(rider, ins, outs, sems)

    return pl.pallas_call(
        body, name=name, out_shape=list(rider.out_shape), in_specs=[ANY] * n_in, out_specs=[ANY] * n_out,
        input_output_aliases=dict(rider.aliases), scratch_shapes=list(rider.sems),
        compiler_params=pltpu.CompilerParams(vmem_limit_bytes=VMEM_LIMIT_BYTES))(*rider.operands)


def _exchange_rider(g5s):
    n = len(g5s)

    def copies(ins, outs, sems):
        x, y, c, _ = _mesh_place()
        return [_remote(ins[a].at[:, :, 1 - c], outs[a], sems[0].at[a], sems[1].at[a], (x, y, 1 - c)) for a in range(n)]

    def start(ins, outs, sems):
        for cp in copies(ins, outs, sems):
            cp.start()

    def finish(ins, outs, sems):
        cps = copies(ins, outs, sems)
        for cp in cps:
            cp.wait_recv()
        for cp in cps:
            cp.wait_send()

    out_shape = tuple(_sds((g.shape[0], g.shape[1], g.shape[3], g.shape[4]), BF16) for g in g5s)
    return Rider(tuple(g5s), out_shape, {}, (pltpu.SemaphoreType.DMA((n,)), pltpu.SemaphoreType.DMA((n,))), start, finish)


def _share_rider(fs):
    n = len(fs)

    def start(_, outs, sems):
        x, y, c, _p = _mesh_place()
        for a in range(n):
            mine = outs[a].at[:, c]
            _remote(mine, mine, sems[0].at[a], sems[1].at[a], (x, y, 1 - c)).start()

    def finish(_, outs, sems):
        x, y, c, _p = _mesh_place()
        for a in range(n):
            theirs = outs[a].at[:, 1 - c]
            _remote(theirs, theirs, sems[0].at[a], sems[1].at[a], (x, y, c)).wait_recv()
        for a in range(n):
            mine = outs[a].at[:, c]
            _remote(mine, mine, sems[0].at[a], sems[1].at[a], (x, y, 1 - c)).wait_send()

    return Rider(tuple(fs), tuple(_sds(f.shape, F32) for f in fs), {a: a for a in range(n)},
                 (pltpu.SemaphoreType.DMA((n,)), pltpu.SemaphoreType.DMA((n,))), start, finish)


def _both_riders(r1, r2):
    ni, no, ns = len(r1.operands), len(r1.out_shape), len(r1.sems)
    aliases = dict(r1.aliases)
    aliases.update({ni + i: no + o for i, o in r2.aliases.items()})

    def start(ins, outs, sems):
        r1.start(ins[:ni], outs[:no], sems[:ns])
        r2.start(ins[ni:], outs[no:], sems[ns:])

    def finish(ins, outs, sems):
        _rider_end(r1, ins[:ni], outs[:no], sems[:ns])
        _rider_end(r2, ins[ni:], outs[no:], sems[ns:])

    return Rider(r1.operands + r2.operands, r1.out_shape + r2.out_shape, aliases, r1.sems + r2.sems, start, finish)


def _pack_rows(parts, lane_mult=1024):
    flat = jnp.concatenate([p.reshape(-1).astype(F32) for p in parts])
    n = -(-flat.shape[0] // (8 * lane_mult)) * lane_mult
    return jnp.pad(flat, (0, 8 * n - flat.shape[0])).reshape(8, n)


def _relu2(acc):
    r = jnp.maximum(acc, 0.0)
    return r, r * r


def _times_2r(acc, r):
    return (acc * (2.0 * r.astype(F32)),)


def kernel(x, c, positions, w_mod, b_mod, norm_g, mla_w_in, mla_g_q, mla_g_kv, mla_w_uq, mla_w_ukv, mla_w_o, conv_w_in, conv_w, conv_w_out, mlp_w_up, mlp_w_down, loss_target, m_w_mod, m_b_mod, m_norm_g, m_mla_w_in, m_mla_g_q, m_mla_g_kv, m_mla_w_uq, m_mla_w_ukv, m_mla_w_o, m_conv_w_in, m_conv_w, m_conv_w_out, m_mlp_w_up, m_mlp_w_down, v_w_mod, v_b_mod, v_norm_g, v_mla_w_in, v_mla_g_q, v_mla_g_kv, v_mla_w_uq, v_mla_w_ukv, v_mla_w_o, v_conv_w_in, v_conv_w, v_conv_w_out, v_mlp_w_up, v_mlp_w_down):
    S, D = x.shape[1], x.shape[2]
    Dq = D // N_CHIPS
    ncol = w_mod.shape[2]
    n_mod = N_CHIPS * ncol // D
    F = mlp_w_up.shape[2] * N_CHIPS
    lat_dim = mla_w_in.shape[2]
    rank = mla_g_q.shape[1]
    H = mla_w_uq.shape[2]
    d_qk = mla_w_uq.shape[3]
    assert mla_g_kv.shape[1] == rank and lat_dim == 2 * rank + QK_ROPE and d_qk == QK_NOPE + QK_ROPE
    assert mla_w_ukv.shape[3] == QK_NOPE + V_HEAD and x.shape[0] == 1 and n_mod == 6
    assert norm_g.shape[0] == 2 and mla_w_in.shape[0] == 1 and conv_w_in.shape[0] == 1
    lat_pad = 2 * rank + LANES
    scale = float(d_qk) ** -0.5

    xi, yi, ci = lax.axis_index("x"), lax.axis_index("y"), lax.axis_index("c")
    chip = 2 * xi + yi
    dev = 2 * chip + ci
    c_idx = jnp.reshape(ci, (1,)).astype(jnp.int32)
    k_idx = jnp.reshape(chip, (1,)).astype(jnp.int32)

    n1 = D + 2 * D + 3 * Dq
    g1 = _small_allgather("gather_small_inputs", _pack_rows([c, norm_g, conv_w])).reshape(8, -1)
    c_all = g1[:, :D]
    by_chip = g1[0::2]
    norm_full = jnp.concatenate([by_chip[kk, D:3 * D].reshape(2, 4, Dq) for kk in range(N_CHIPS)], axis=-1)
    convw_full = jnp.concatenate([by_chip[kk, 3 * D:n1].reshape(3, Dq) for kk in range(N_CHIPS)], axis=-1)

    b_cols = lax.dynamic_slice(b_mod, (0, chip * ncol), (2, ncol)).reshape(2, 1, ncol)
    cond_all = _silu(c_all)
    mod_cols = _mod_fwd(cond_all, w_mod, b_cols)
    g2 = _small_allgather("gather_mod", _pack_rows([mod_cols]))
    g2 = g2.reshape(8, -1)[0::2, :2 * 8 * ncol].reshape(N_CHIPS, 2, 8, ncol)
    mod_all = jnp.transpose(g2, (2, 1, 0, 3)).reshape(8, 2, n_mod * D)
    mod_me = lax.dynamic_index_in_dim(mod_all, dev, axis=0, keepdims=False)
    mods = [[mod_me[l, i * D:(i + 1) * D].reshape(1, D) for i in range(n_mod)] for l in range(2)]
    ng = [[norm_full[l, i].reshape(1, D) for i in range(4)] for l in range(2)]

    pos = positions[0].astype(F32)
    inv_freq = ROPE_THETA ** (-jnp.arange(0, QK_ROPE, 2, dtype=F32) / QK_ROPE)
    ang = pos[:, None] * inv_freq
    cos, sin = jnp.cos(ang), jnp.sin(ang)
    zero = jnp.zeros_like(cos)
    rope_tabs = (jnp.concatenate([cos, cos, zero, zero], axis=1),
                 jnp.concatenate([-sin, zero, zero, zero], axis=1),
                 jnp.concatenate([zero, sin, zero, zero], axis=1))

    weights = [("mla_w_in", mla_w_in, "row"), ("mla_w_uq", mla_w_uq.reshape(1, rank // N_CHIPS, H * d_qk), "row"),
               ("mla_w_ukv", mla_w_ukv.reshape(1, rank // N_CHIPS, H * QK_PAD), "row"), ("mla_w_o", mla_w_o, "row"),
               ("conv_w_in", conv_w_in, "col"), ("conv_w_out", conv_w_out, "row"),
               ("mlp_w_up", mlp_w_up, "col"), ("mlp_w_down", mlp_w_down, "row")]
    kinds = [k for _, _, k in weights]
    shard_shapes = [w.shape for _, w, _ in weights]
    shard_cols = [s[2] for s in shard_shapes]
    W_IN, W_UQ, W_UKV, W_O, W_CIN, W_COUT, W_UP, W_DOWN = range(8)
    mla_idx = [W_IN, W_UQ, W_UKV, W_O]
    casted = [_cast_into_full("cast_" + nm, [w], [kind], k_idx)[0][0] for nm, w, kind in weights[:W_UP]]

    def view(i, buf):
        L, R, C = shard_shapes[i]
        return buf.reshape((L, N_CHIPS * R, C) if kinds[i] == "row" else (L, R, N_CHIPS * C))

    NEIGHBOURS, DIAGONAL = (0, 1), (2,)

    def gather_of(bufs, idx, layers=None, peers=(0, 1, 2)):
        return _gather_rider(bufs, [kinds[i] for i in idx], [shard_cols[i] for i in idx], layers, peers)

    def scatter_of(ps, idx, peers=(0, 1, 2), into=None):
        return _scatter_rider(ps, [kinds[i] for i in idx], [shard_cols[i] for i in idx], peers, into)

    def halves(items):
        g5s = []
        for _, i, g in items:
            _, R, C = shard_shapes[i]
            g5s.append(g.reshape((1, N_CHIPS, 2, R // 2, C) if kinds[i] == "row" else (1, 1, 2, R // 2, N_CHIPS * C)))
        return g5s

    def pair_sums(items, g5s, ras):
        return [_pair_sum("pair_sum_" + nm, g5, ra, c_idx) for (nm, _, _), g5, ra in zip(items, g5s, ras)]

    first_idx = [W_IN, W_UQ, W_UKV]
    mlp_casted, got = _cast_into_full("cast_mlp_w", [mlp_w_up, mlp_w_down], [kinds[W_UP], kinds[W_DOWN]], k_idx,
                                      gather_of([casted[i] for i in first_idx], first_idx))
    casted += list(mlp_casted)
    w_in_p = jnp.pad(view(W_IN, got[0])[0], ((0, 0), (0, lat_pad - lat_dim)))
    w_q_p = jnp.pad(view(W_UQ, got[1])[0].reshape(rank, H, d_qk), ((0, 0), (0, 0), (0, QK_PAD - d_qk))).reshape(rank, H * QK_PAD)
    w_ukv = view(W_UKV, got[2])[0]
    HV = H * V_HEAD

    def layer_b(l, transposed):
        if transposed:
            return lambda tm, tn, tk: pl.BlockSpec((None, tn, tk), lambda i, j, k: (l, j, k))
        return lambda tm, tn, tk: pl.BlockSpec((None, tk, tn), lambda i, j, k: (l, k, j))

    def mlp_up(tag, l, h, w, rider=NO_RIDER):
        return _mm("mlp_up_" + tag, h, w, "nn", S, F, D, [_sds((S, F), BF16)] * 2, epilogue=_relu2,
                   b_spec=layer_b(l, False), rider=rider)

    def mlp_down(tag, l, a2, w, rider=NO_RIDER):
        return _mm("mlp_down_" + tag, a2, w, "nn", S, D, F, [_sds((S, D), BF16)], b_spec=layer_b(l, False), rider=rider)

    def mlp_bwd(tag, l, h, r, a2, dy, first=NO_RIDER, second_of=None):
        res = _mm("mlp_down_dx_" + tag, dy, w_down, "nt", S, F, D, [_sds((S, F), BF16)], epilogue=_times_2r,
                  b_spec=layer_b(l, True), rider=first,
                  extras=[(r, lambda tm, tn, tk: pl.BlockSpec((tm, tn), lambda i, j, k: (i, j)))])
        (da,), got_first = res if first.start is not None else (res, ())
        second = second_of(got_first) if second_of else NO_RIDER
        res = _mm("mlp_down_dw_" + tag, a2, dy, "tn", F, D, S, [_sds((F, D), BF16)], rider=second)
        (dw_down,), got_second = res if second_of else (res, ())
        (dh,) = _mm("mlp_up_dx_" + tag, da, w_up, "nt", S, D, F, [_sds((S, D), BF16)], b_spec=layer_b(l, True))
        (dw_up,) = _mm("mlp_up_dw_" + tag, h, da, "tn", D, F, S, [_sds((D, F), BF16)])
        return dh, dw_up, dw_down, got_first, got_second

    x0 = x[0]
    sh1, sc1, gt1, sh2, sc2, gt2 = mods[0]
    (h1,) = _fwd_boundary("fwd_boundary_0", x0, None, None, None, ng[0][0], sc1, sh1)
    (lat,) = _mm("mla_in", h1, w_in_p, "nn", S, lat_pad, D, [_sds((S, lat_pad), F32)], tn=lat_pad)
    cq, ckv, kr = _latent_fwd(lat, mla_g_q, mla_g_kv, rope_tabs, rank)

    def rope_q(acc, cos_p, sin_lo, sin_hi):
        parts = []
        for hh in range(acc.shape[1] // QK_PAD):
            parts.append(acc[:, hh * QK_PAD:hh * QK_PAD + QK_NOPE])
            parts.append(_rope(acc[:, hh * QK_PAD + QK_NOPE:(hh + 1) * QK_PAD], cos_p, sin_lo, sin_hi))
        return (jnp.concatenate(parts, axis=1),)

    tab_extra = lambda tm, tn, tk: pl.BlockSpec((tm, LANES), lambda i, j, k: (i, 0))
    (q,), (o_buf,) = _mm("mla_q", cq, w_q_p, "nn", S, H * QK_PAD, rank, [_sds((S, H * QK_PAD), BF16)], epilogue=rope_q,
                         extras=[(t, tab_extra) for t in rope_tabs], tn=2 * QK_PAD,
                         rider=gather_of([casted[W_O]], [W_O]))
    w_o = view(W_O, o_buf)[0]
    (kv,) = _mm("mla_kv", ckv, w_ukv, "nn", S, H * QK_PAD, rank, [_sds((S, H * QK_PAD), BF16)])
    rest_idx = [W_UP, W_DOWN]
    o, lse, (up_buf, down_buf) = _attn_fwd_tri(
        q, kv, kr, H, scale, gather_of([casted[i] for i in rest_idx], rest_idx, [0, 0]))
    (y1,), (cout_buf,) = _mm("mla_out", o, w_o, "nn", S, D, HV, [_sds((S, D), BF16)],
                             rider=gather_of([casted[W_COUT]], [W_COUT]))
    x1, h2 = _fwd_boundary("fwd_boundary_1", x0, y1, gt1, ng[0][1], ng[0][2], sc2, sh2)
    (r2, a2), (cin_buf,) = mlp_up("0", 0, h2, view(W_UP, up_buf), gather_of([casted[W_CIN]], [W_CIN]))
    (y2,), (up_buf,) = mlp_down("0", 0, a2, view(W_DOWN, down_buf), gather_of([up_buf], [W_UP], [1]))
    w_cin, w_cout, w_up = view(W_CIN, cin_buf)[0], view(W_COUT, cout_buf)[0], view(W_UP, up_buf)

    sh1b, sc1b, gt1b, sh2b, sc2b, gt2b = mods[1]
    x2, h3 = _fwd_boundary("fwd_boundary_2", x1, y2, gt2, ng[0][3], ng[1][0], sc1b, sh1b)
    nD = lambda tn: D // tn
    (proj3,), (down_buf,) = _mm(
        "conv_in", h3, w_cin, "nn", S, 3 * D, D, [_sds((3, S, D), BF16)], tn=min(1024, D),
        rider=gather_of([down_buf], [W_DOWN], [1], NEIGHBOURS),
        out_specs=[lambda tm, tn, tk: pl.BlockSpec((None, tm, tn), lambda i, j, k: (j // nD(tn), i, j % nD(tn)))])
    bz = _conv_fwd(proj3, convw_full)
    (y3,) = _mm("conv_out", bz, w_cout, "nn", S, D, D, [_sds((S, D), BF16)])
    x3, h4 = _fwd_boundary("fwd_boundary_3", x2, y3, gt1b, ng[1][1], ng[1][2], sc2b, sh2b)
    (r4, a4), (down_buf,) = mlp_up("1", 1, h4, w_up, gather_of([down_buf], [W_DOWN], [1], DIAGONAL))
    w_down = view(W_DOWN, down_buf)
    (y4,) = mlp_down("1", 1, a4, w_down)

    dx4, dy4, sums_l, loss_acc = _loss_boundary("loss_boundary", x3, y4, gt2b, ng[1][3], loss_target[0])

    dh4, dw_up1, dw_down1, _, _ = mlp_bwd("1", 1, h4, r4, a4, dy4)
    dx3, dy3, sums_3, _ = _bwd_boundary("bwd_boundary_3", dx4, dh4, x3, y3, gt1b, ng[1][1], ng[1][2], sc2b)

    items = [("mlp_w_up_1", W_UP, dw_up1)]
    g5s = halves(items)
    (dbz,), ras = _mm("conv_out_dx", dy3, w_cout, "nt", S, D, D, [_sds((S, D), BF16)], rider=_exchange_rider(g5s))
    (ps_up1,) = pair_sums(items, g5s, ras)
    items = [("mlp_w_down_1", W_DOWN, dw_down1)]
    g5s = halves(items)
    (dw_cout,), ras = _mm("conv_out_dw", bz, dy3, "tn", D, D, S, [_sds((D, D), BF16)], rider=_exchange_rider(g5s))
    (ps_down1,) = pair_sums(items, g5s, ras)
    dproj3, dconvw = _conv_bwd(dbz, proj3, convw_full)
    (dh3,), (rb_up1,) = _mm(
        "conv_in_dx", dproj3, w_cin, "nt", S, D, 3 * D, [_sds((S, D), BF16)], tk=D,
        rider=scatter_of([ps_up1], [W_UP], NEIGHBOURS),
        a_spec=lambda tm, tn, tk: pl.BlockSpec((None, tm, tk), lambda i, j, k: (k // (D // tk), i, k % (D // tk))))
    (dw_cin,), (rb_up1,) = _mm(
        "conv_in_dw", h3, dproj3, "tn", D, 3 * D, S, [_sds((D, 3 * D), BF16)], tn=min(1024, D),
        rider=scatter_of([ps_up1], [W_UP], DIAGONAL, [rb_up1]),
        b_spec=lambda tm, tn, tk: pl.BlockSpec((None, tk, tn), lambda i, j, k: (j // nD(tn), k, j % nD(tn))))
    dx2, dy2, sums_2, _ = _bwd_boundary("bwd_boundary_2", dx3, dh3, x2, y2, gt2, ng[0][3], ng[1][0], sc1b)

    items = [("conv_w_in", W_CIN, dw_cin), ("conv_w_out", W_COUT, dw_cout)]
    g5s = halves(items)
    dh2, dw_up0, dw_down0, (rb_down1, *ras), (rb_down1,) = mlp_bwd(
        "0", 0, h2, r2, a2, dy2,
        _both_riders(scatter_of([ps_down1], [W_DOWN], NEIGHBOURS), _exchange_rider(g5s)),
        lambda got: scatter_of([ps_down1], [W_DOWN], DIAGONAL, [got[0]]))
    ps_cin, ps_cout = pair_sums(items, g5s, ras)
    dx1, dy1, sums_1, _ = _bwd_boundary("bwd_boundary_1", dx2, dh2, x1, y1, gt1, ng[0][1], ng[0][2], sc2)

    items = [("mlp_w_up_0", W_UP, dw_up0)]
    g5s = halves(items)
    (dw_o,), ras = _mm("mla_out_dw", o, dy1, "tn", HV, D, S, [_sds((HV, D), BF16)], rider=_exchange_rider(g5s))
    (ps_up0,) = pair_sums(items, g5s, ras)
    items = [(weights[W_O][0], W_O, dw_o), ("mlp_w_down_0", W_DOWN, dw_down0)]
    g5s = halves(items)
    (do,), ras = _mm("mla_out_dx", dy1, w_o, "nt", S, HV, D, [_sds((S, HV), BF16)], rider=_exchange_rider(g5s))
    ps_o, ps_down0 = pair_sums(items, g5s, ras)
    dq, dkv, dkr, (rb_up0, rb_down0, rb_cin, rb_cout, rb_o) = _attn_bwd_tri(
        q, kv, kr, o, do, lse, rope_tabs, H, scale,
        scatter_of([ps_up0, ps_down0, ps_cin, ps_cout, ps_o], [W_UP, W_DOWN, W_CIN, W_COUT, W_O]))
    kc_idx = jnp.stack([chip, ci]).astype(jnp.int32)
    f_o, f_cin, f_cout = [_chip_sum("chip_sum_" + weights[i][0], p, rb, kc_idx, kinds[i])
                          for i, p, rb in [(W_O, ps_o, rb_o), (W_CIN, ps_cin, rb_cin), (W_COUT, ps_cout, rb_cout)]]
    f_mlp = []
    for i, (p1, r1), (p0, r0) in [(W_UP, (ps_up1, rb_up1), (ps_up0, rb_up0)), (W_DOWN, (ps_down1, rb_down1), (ps_down0, rb_down0))]:
        f = _chip_sum("chip_sum_" + weights[i][0] + "_1", p1, r1, kc_idx, kinds[i], layer=1, n_layers=2)
        f_mlp.append(_chip_sum("chip_sum_" + weights[i][0] + "_0", p0, r0, kc_idx, kinds[i], layer=0, n_layers=2, prev=f))
    (dcq,), fin_up = _mm("mla_q_dx", dq, w_q_p, "nt", S, rank, H * QK_PAD, [_sds((S, rank), F32)],
                         rider=_share_rider([f_mlp[0]]))
    (dw_q_p,), fin_small = _mm("mla_q_dw", cq, dq, "tn", rank, H * QK_PAD, S, [_sds((rank, H * QK_PAD), BF16)],
                               rider=_share_rider([f_o, f_cin, f_cout]))
    (dckv,), fin_down = _mm("mla_kv_dx", dkv, w_ukv, "nt", S, rank, H * QK_PAD, [_sds((S, rank), F32)],
                            rider=_share_rider([f_mlp[1]]))
    finals_rest = list(fin_small) + list(fin_up) + list(fin_down)
    (dw_ukv,) = _mm("mla_kv_dw", ckv, dkv, "tn", rank, H * QK_PAD, S, [_sds((rank, H * QK_PAD), BF16)])
    dlat, sums_lat = _latent_bwd(lat, dcq, dckv, dkr, mla_g_q, mla_g_kv, rope_tabs, rank)
    late_idx = [W_IN, W_UQ, W_UKV]
    items_qkv = [(weights[W_UQ][0], W_UQ, dw_q_p.reshape(rank, H, QK_PAD)[:, :, :d_qk].reshape(rank, H * d_qk)),
                 (weights[W_UKV][0], W_UKV, dw_ukv)]
    g5s_qkv = halves(items_qkv)
    (dw_in_p,), ras = _mm("mla_in_dw", h1, dlat, "tn", D, lat_pad, S, [_sds((D, lat_pad), BF16)], tn=lat_pad,
                          rider=_exchange_rider(g5s_qkv))
    ps_qkv = pair_sums(items_qkv, g5s_qkv, ras)
    items_in = [(weights[W_IN][0], W_IN, dw_in_p[:, :lat_dim])]
    g5s_in = halves(items_in)
    (dh1,), carried = _mm("mla_in_dx", dlat, w_in_p, "nt", S, D, lat_pad, [_sds((S, D), BF16)],
                          rider=_both_riders(scatter_of(ps_qkv, [W_UQ, W_UKV]), _exchange_rider(g5s_in)))
    rbs_qkv, ras = carried[:2], carried[2:]
    ps_in = pair_sums(items_in, g5s_in, ras)
    ps_mla = ps_in + ps_qkv
    grad_x, sums_0, _ = _bwd_boundary("bwd_boundary_0", dx1, dh1, x0, None, None, None, ng[0][0], sc1)

    dmod0 = [sums_0[0], sums_0[1], sums_1[3], sums_1[0], sums_1[1], sums_2[3]]
    dmod1 = [sums_2[0], sums_2[1], sums_3[3], sums_3[0], sums_3[1], sums_l[3]]
    dng0 = [sums_0[2], sums_1[4], sums_1[2], sums_2[4]]
    dng1 = [sums_2[2], sums_3[4], sums_3[2], sums_l[4]]
    small = _pack_rows(dmod0 + dmod1 + dng0 + dng1 + [sums_lat[0], sums_lat[1], dconvw, loss_acc[0, 0:1]],
                       lane_mult=LANES)
    gathered, total, carried = _small_allgather(
        "gather_small_grads", small, with_sum=True, rider=scatter_of(ps_in, [W_IN]))
    rbs_mla = list(carried) + list(rbs_qkv)
    n_dm = 2 * n_mod * D
    dmod_all = gathered.reshape(8, -1)[:, :n_dm].reshape(8, 2, n_mod * D)
    total = total.reshape(-1)
    g_b_mod = total[:n_dm].reshape(2, n_mod * D)
    g_norm = lax.dynamic_slice(total[n_dm:n_dm + 8 * D].reshape(2, 4, D), (0, 0, chip * Dq), (2, 4, Dq))
    off = n_dm + 8 * D
    g_gq = total[off:off + rank].reshape(1, rank)
    g_gkv = total[off + rank:off + 2 * rank].reshape(1, rank)
    off += 2 * rank
    g_convw = lax.dynamic_slice(total[off:off + 3 * D].reshape(1, 3, D), (0, 0, chip * Dq), (1, 3, Dq))
    loss = total[off + 3 * D]

    dmod_cols = jnp.transpose(lax.dynamic_slice(dmod_all.reshape(8, 2, N_CHIPS, ncol), (0, 0, chip, 0), (8, 2, 1, ncol))
                              .reshape(8, 2, ncol), (1, 0, 2))
    g_w_mod, d_w_mod, nm_w_mod, nv_w_mod, _ = _adamw_mod(w_mod, cond_all.T, dmod_cols, m_w_mod, v_w_mod)
    fs_mla = [_chip_sum("chip_sum_" + weights[i][0], p, rb, kc_idx, kinds[i]) for i, p, rb in zip(late_idx, ps_mla, rbs_mla)]
    finals = list(_run_rider("grad_pair_share_mla", _share_rider(fs_mla))) + list(finals_rest)
    orig = [mla_w_in, mla_w_uq, mla_w_ukv, mla_w_o, conv_w_in, conv_w_out, mlp_w_up, mlp_w_down]
    big_grads = [f.reshape(w.shape) for f, w in zip(finals, orig)]

    names = ["b_mod", "norm_g", "mla_w_in", "mla_g_q", "mla_g_kv", "mla_w_uq", "mla_w_ukv", "mla_w_o",
             "conv_w_in", "conv_w", "conv_w_out", "mlp_w_up", "mlp_w_down"]
    ws = [b_mod, norm_g, mla_w_in, mla_g_q, mla_g_kv, mla_w_uq, mla_w_ukv, mla_w_o, conv_w_in, conv_w, conv_w_out,
          mlp_w_up, mlp_w_down]
    ms = [m_b_mod, m_norm_g, m_mla_w_in, m_mla_g_q, m_mla_g_kv, m_mla_w_uq, m_mla_w_ukv, m_mla_w_o, m_conv_w_in,
          m_conv_w, m_conv_w_out, m_mlp_w_up, m_mlp_w_down]
    vs = [v_b_mod, v_norm_g, v_mla_w_in, v_mla_g_q, v_mla_g_kv, v_mla_w_uq, v_mla_w_ukv, v_mla_w_o, v_conv_w_in,
          v_conv_w, v_conv_w_out, v_mlp_w_up, v_mlp_w_down]
    gs = [g_b_mod, g_norm, big_grads[0], g_gq, g_gkv, big_grads[1], big_grads[2], big_grads[3], big_grads[4],
          g_convw, big_grads[5], big_grads[6], big_grads[7]]
    grads, deltas, new_ms, new_vs = [g_w_mod], [d_w_mod], [nm_w_mod], [nv_w_mod]
    for nm, w, g, m, v in zip(names, ws, gs, ms, vs):
        big = any(g is b for b in big_grads)
        d, nm_, nv_, *g_out = _adamw("adamw_" + nm, w, g, m, v, emit_grad=big)
        g = g_out[0] if big else g
        grads.append(g)
        deltas.append(d)
        new_ms.append(nm_)
        new_vs.append(nv_)
    return (loss, grad_x[None], *grads, *deltas, *new_ms, *new_vs)
```

```python
from typing import NamedTuple

import jax
import jax.numpy as jnp
from jax import lax
from jax.experimental import pallas as pl
from jax.experimental.pallas import tpu as pltpu

F32 = jnp.float32
BF16 = jnp.bfloat16
NORM_EPS = 1e-6
ROPE_THETA = 10000.0
QK_NOPE = 128
QK_ROPE = 64
V_HEAD = 128
LANES = 128
QK_PAD = QK_NOPE + LANES
ADAM_LR, ADAM_B1, ADAM_B2, ADAM_EPS, ADAM_WD, ADAM_STEP = 0.001, 0.9, 0.999, 1e-08, 0.01, 10
VMEM_LIMIT_BYTES = 56 * 1024 * 1024
N_CHIPS = 4
MESH_ID = pl.DeviceIdType.MESH
ANY = pl.BlockSpec(memory_space=pl.ANY)
NEG_INF = float("-inf")

DIMS_NN = (((1,), (0,)), ((), ()))
DIMS_NT = (((1,), (1,)), ((), ()))
DIMS_TN = (((0,), (0,)), ((), ()))


def _cparams(*sem):
    return pltpu.CompilerParams(dimension_semantics=sem, vmem_limit_bytes=VMEM_LIMIT_BYTES)


def _row_tile(rows, row_bytes, limit=2 * 1024 * 1024, mult=16):
    if rows * row_bytes <= limit or rows % mult:
        return rows
    best = mult
    t = mult
    while t <= rows:
        if rows % t == 0 and t * row_bytes <= limit:
            best = t
        t += mult
    return best


def _rms(v):
    return lax.rsqrt(jnp.mean(v * v, axis=-1, keepdims=True) + NORM_EPS)


class Rider(NamedTuple):
    operands: tuple
    out_shape: tuple
    aliases: dict
    sems: tuple
    start: object
    finish: object
    mid: object = None


NO_RIDER = Rider((), (), {}, (), None, None)


def _rider_end(rider, r_in, r_out, r_sems):
    if rider.mid is not None:
        rider.mid(r_in, r_out, r_sems)
    rider.finish(r_in, r_out, r_sems)


def _mm(name, a, b, mode, M, N, K, outs, *, a_spec=None, b_spec=None, out_specs=None, epilogue=None,
        extras=(), rider=NO_RIDER, tm=1024, tn=1024, tk=4096):
    tm, tn, tk = min(tm, M), min(tn, N), min(tk, K)
    assert M % tm == 0 and N % tn == 0 and K % tk == 0, (name, M, N, K)
    nk = K // tk
    if a_spec is None:
        a_spec = {"nn": pl.BlockSpec((tm, tk), lambda i, j, k: (i, k)),
                  "nt": pl.BlockSpec((tm, tk), lambda i, j, k: (i, k)),
                  "tn": pl.BlockSpec((tk, tm), lambda i, j, k: (k, i))}[mode]
    else:
        a_spec = a_spec(tm, tn, tk)
    if b_spec is None:
        b_spec = {"nn": pl.BlockSpec((tk, tn), lambda i, j, k: (k, j)),
                  "nt": pl.BlockSpec((tn, tk), lambda i, j, k: (j, k)),
                  "tn": pl.BlockSpec((tk, tn), lambda i, j, k: (k, j))}[mode]
    else:
        b_spec = b_spec(tm, tn, tk)
    if out_specs is None:
        out_specs = [pl.BlockSpec((tm, tn), lambda i, j, k: (i, j)) for _ in outs]
    else:
        out_specs = [s(tm, tn, tk) for s in out_specs]
    dims = {"nn": DIMS_NN, "nt": DIMS_NT, "tn": DIMS_TN}[mode]
    ne, no = len(extras), len(outs)
    n_ri, n_ro = len(rider.operands), len(rider.out_shape)
    grid = (M // tm, N // tn, nk)

    def body(*refs):
        a_ref, b_ref = refs[0], refs[1]
        ex = refs[2:2 + ne]
        r_in = refs[2 + ne:2 + ne + n_ri]
        o = refs[2 + ne + n_ri:2 + ne + n_ri + no]
        r_out = refs[2 + ne + n_ri + no:2 + ne + n_ri + no + n_ro]
        scratch = refs[2 + ne + n_ri + no + n_ro:]
        r_sems = scratch[1:] if nk > 1 else scratch
        ii, jj, kk = pl.program_id(0), pl.program_id(1), pl.program_id(2)

        if rider.start is not None:
            @pl.when((ii == 0) & (jj == 0) & (kk == 0))
            def _():
                rider.start(r_in, r_out, r_sems)

        part = lax.dot_general(a_ref[...].astype(BF16), b_ref[...].astype(BF16), dims,
                               preferred_element_type=F32)

        def finish(total):
            vals = epilogue(total, *[e[...] for e in ex]) if epilogue is not None else (total,)
            for r, v in zip(o, vals):
                r[...] = v.astype(r.dtype)

        if nk == 1:
            finish(part)
        else:
            acc = scratch[0]

            @pl.when(kk == 0)
            def _():
                acc[...] = part

            @pl.when(kk > 0)
            def _():
                acc[...] += part

            @pl.when(kk == nk - 1)
            def _():
                finish(acc[...])

        if rider.finish is not None:
            steps = grid[0] * grid[1] * nk
            if rider.mid is not None and steps >= 4:
                @pl.when((ii * grid[1] + jj) * nk + kk == steps // 2)
                def _():
                    rider.mid(r_in, r_out, r_sems)

            @pl.when((ii == grid[0] - 1) & (jj == grid[1] - 1) & (kk == nk - 1))
            def _():
                if rider.mid is not None and steps < 4:
                    rider.mid(r_in, r_out, r_sems)
                rider.finish(r_in, r_out, r_sems)

    operands = [a, b] + [e[0] for e in extras] + list(rider.operands)
    in_specs = [a_spec, b_spec] + [e[1](tm, tn, tk) for e in extras] + [ANY] * n_ri
    hosted = rider.start is not None
    res = pl.pallas_call(
        body, name=name, grid=grid,
        in_specs=in_specs, out_specs=out_specs + [ANY] * n_ro, out_shape=list(outs) + list(rider.out_shape),
        scratch_shapes=([pltpu.VMEM((tm, tn), F32)] if nk > 1 else []) + list(rider.sems),
        input_output_aliases={2 + ne + i: no + r for i, r in rider.aliases.items()},
        compiler_params=_cparams(*(("arbitrary",) * 3 if hosted else ("parallel", "parallel", "arbitrary"))),
    )(*operands)
    return (res[:no], res[no:]) if hosted else res


def _sds(shape, dtype):
    return jax.ShapeDtypeStruct(tuple(shape), dtype)


def _rope(t, cos_p, sin_lo, sin_hi):
    return t * cos_p + pltpu.roll(t, LANES - QK_ROPE // 2, 1) * sin_lo + pltpu.roll(t, QK_ROPE // 2, 1) * sin_hi


def _rope_t(d, cos_p, sin_lo, sin_hi):
    return d * cos_p + pltpu.roll(d * sin_lo, QK_ROPE // 2, 1) + pltpu.roll(d * sin_hi, LANES - QK_ROPE // 2, 1)


def _vec_spec(d):
    return pl.BlockSpec((1, d), lambda i: (0, 0))


def _fwd_boundary(name, x_prev, y, gate, ng_post, ng_pre, sc, sh):
    S, D = x_prev.shape
    ts = min(256, S)
    has_y = y is not None
    row = pl.BlockSpec((ts, D), lambda i: (i, 0))

    def body(*refs):
        if has_y:
            x_ref, y_ref, g_ref, ngp_ref, ngn_ref, sc_ref, sh_ref, xo_ref, h_ref = refs
            yv = y_ref[...].astype(F32)
            xn = x_ref[...] + g_ref[...] * (yv * _rms(yv) * ngp_ref[...])
            xo_ref[...] = xn
        else:
            x_ref, ngn_ref, sc_ref, sh_ref, h_ref = refs
            xn = x_ref[...]
        hn = xn * _rms(xn) * ngn_ref[...]
        h_ref[...] = (hn * (1.0 + sc_ref[...]) + sh_ref[...]).astype(BF16)

    vec = _vec_spec(D)
    if has_y:
        operands = (x_prev, y, gate, ng_post, ng_pre, sc, sh)
        in_specs = [row, row, vec, vec, vec, vec, vec]
        out_shape = [_sds((S, D), F32), _sds((S, D), BF16)]
        out_specs = [row, row]
    else:
        operands = (x_prev, ng_pre, sc, sh)
        in_specs = [row, vec, vec, vec]
        out_shape = [_sds((S, D), BF16)]
        out_specs = [row]
    return pl.pallas_call(body, name=name, grid=(S // ts,), in_specs=in_specs, out_specs=out_specs,
                          out_shape=out_shape, compiler_params=_cparams("parallel"))(*operands)


def _acc_rows(sums_ref, rows):
    for r, v in rows:
        sums_ref[r:r + 1, :] += jnp.sum(v, axis=0, keepdims=True)


def _post_norm_bwd(dxt, yv, gate, ng_post, sums_ref, dy_ref):
    r1 = _rms(yv)
    yhat = yv * r1
    dn = dxt * gate
    u = dn * ng_post
    dy = r1 * (u - yhat * jnp.mean(u * yhat, axis=-1, keepdims=True))
    dy_ref[...] = dy.astype(dy_ref.dtype)
    _acc_rows(sums_ref, [(3, dxt * (yhat * ng_post)), (4, dn * yhat)])


def _loss_boundary(name, x_prev, y, gate, ng_post, target):
    S, D = x_prev.shape
    ts = min(256, S)
    row = pl.BlockSpec((ts, D), lambda i: (i, 0))
    vec = _vec_spec(D)

    def body(x_ref, y_ref, g_ref, ngp_ref, t_ref, dx_ref, dy_ref, sums_ref, loss_ref):
        @pl.when(pl.program_id(0) == 0)
        def _():
            sums_ref[...] = jnp.zeros_like(sums_ref)
            loss_ref[...] = jnp.zeros_like(loss_ref)

        yv = y_ref[...].astype(F32)
        xf = x_ref[...] + g_ref[...] * (yv * _rms(yv) * ngp_ref[...])
        err = xf - t_ref[...]
        loss_ref[...] += 0.5 * jnp.sum(jnp.mean(err * err, axis=-1, keepdims=True))
        dxt = err / D
        dx_ref[...] = dxt
        _post_norm_bwd(dxt, yv, g_ref[...], ngp_ref[...], sums_ref, dy_ref)

    return pl.pallas_call(
        body, name=name, grid=(S // ts,),
        in_specs=[row, row, vec, vec, row],
        out_specs=[row, row, pl.BlockSpec((8, D), lambda i: (0, 0)), pl.BlockSpec((8, LANES), lambda i: (0, 0))],
        out_shape=[_sds((S, D), F32), _sds((S, D), BF16), _sds((8, D), F32), _sds((8, LANES), F32)],
        compiler_params=_cparams("arbitrary"))(x_prev, y, gate, ng_post, target)


def _bwd_boundary(name, dx_new, dh, x_new, y, gate, ng_post, ng_pre, sc, rider=NO_RIDER):
    S, D = x_new.shape
    ts = min(256, S)
    has_y = y is not None
    row = pl.BlockSpec((ts, D), lambda i: (i, 0))
    vec = _vec_spec(D)
    n_in, n_out = (8, 3) if has_y else (5, 2)
    n_ri, n_ro = len(rider.operands), len(rider.out_shape)

    def body(*refs):
        r_in = refs[n_in:n_in + n_ri]
        r_out = refs[n_in + n_ri + n_out:n_in + n_ri + n_out + n_ro]
        r_sems = refs[n_in + n_ri + n_out + n_ro:]
        own = refs[:n_in] + refs[n_in + n_ri:n_in + n_ri + n_out]
        if has_y:
            dxn_ref, dh_ref, x_ref, y_ref, g_ref, ngp_ref, ngn_ref, sc_ref, dxo_ref, dy_ref, sums_ref = own
        else:
            dxn_ref, dh_ref, x_ref, ngn_ref, sc_ref, dxo_ref, sums_ref = own

        @pl.when(pl.program_id(0) == 0)
        def _():
            sums_ref[...] = jnp.zeros_like(sums_ref)
            if rider.start is not None:
                rider.start(r_in, r_out, r_sems)

        xv = x_ref[...]
        dhv = dh_ref[...].astype(F32)
        ngn = ngn_ref[...]
        r2 = _rms(xv)
        xhat = xv * r2
        dn_pre = dhv * (1.0 + sc_ref[...])
        u2 = dn_pre * ngn
        dxt = dxn_ref[...] + r2 * (u2 - xhat * jnp.mean(u2 * xhat, axis=-1, keepdims=True))
        dxo_ref[...] = dxt
        _acc_rows(sums_ref, [(0, dhv), (1, dhv * (xhat * ngn)), (2, dn_pre * xhat)])
        if has_y:
            _post_norm_bwd(dxt, y_ref[...].astype(F32), g_ref[...], ngp_ref[...], sums_ref, dy_ref)

        if rider.finish is not None:
            @pl.when(pl.program_id(0) == S // ts - 1)
            def _():
                _rider_end(rider, r_in, r_out, r_sems)

    sums_spec = pl.BlockSpec((8, D), lambda i: (0, 0))
    if has_y:
        operands = (dx_new, dh, x_new, y, gate, ng_post, ng_pre, sc)
        in_specs = [row, row, row, row, vec, vec, vec, vec]
        out_shape = [_sds((S, D), F32), _sds((S, D), BF16), _sds((8, D), F32)]
        out_specs = [row, row, sums_spec]
    else:
        operands = (dx_new, dh, x_new, ng_pre, sc)
        in_specs = [row, row, row, vec, vec]
        out_shape = [_sds((S, D), F32), _sds((8, D), F32)]
        out_specs = [row, sums_spec]
    res = pl.pallas_call(
        body, name=name, grid=(S // ts,), in_specs=in_specs + [ANY] * n_ri, out_specs=out_specs + [ANY] * n_ro,
        out_shape=out_shape + list(rider.out_shape), scratch_shapes=list(rider.sems),
        input_output_aliases={n_in + i: n_out + o for i, o in rider.aliases.items()},
        compiler_params=_cparams("arbitrary"))(*operands, *rider.operands)
    return (*res[:n_out], res[n_out:])


def _latent_fwd(lat, g_q, g_kv, rope_tabs, rank):
    S, W = lat.shape
    ts = min(256, S)
    tab = pl.BlockSpec((ts, LANES), lambda i: (i, 0))

    def body(lat_ref, gq_ref, gkv_ref, cos_ref, slo_ref, shi_ref, cq_ref, ckv_ref, kr_ref):
        lq = lat_ref[:, 0:rank]
        lkv = lat_ref[:, rank:2 * rank]
        cq_ref[...] = (lq * _rms(lq) * gq_ref[...]).astype(BF16)
        ckv_ref[...] = (lkv * _rms(lkv) * gkv_ref[...]).astype(BF16)
        kr_ref[...] = _rope(lat_ref[:, 2 * rank:W], cos_ref[...], slo_ref[...], shi_ref[...]).astype(BF16)

    return pl.pallas_call(
        body, name="mla_latent_fwd", grid=(S // ts,),
        in_specs=[pl.BlockSpec((ts, W), lambda i: (i, 0)), _vec_spec(rank), _vec_spec(rank), tab, tab, tab],
        out_specs=[pl.BlockSpec((ts, rank), lambda i: (i, 0)), pl.BlockSpec((ts, rank), lambda i: (i, 0)), tab],
        out_shape=[_sds((S, rank), BF16), _sds((S, rank), BF16), _sds((S, LANES), BF16)],
        compiler_params=_cparams("parallel"))(lat, g_q, g_kv, *rope_tabs)


def _latent_bwd(lat, dcq, dckv, dkr, g_q, g_kv, rope_tabs, rank):
    S, W = lat.shape
    ts = min(256, S)
    tab = pl.BlockSpec((ts, LANES), lambda i: (i, 0))
    half = pl.BlockSpec((ts, rank), lambda i: (i, 0))

    def body(lat_ref, dcq_ref, dckv_ref, dkr_ref, gq_ref, gkv_ref, cos_ref, slo_ref, shi_ref, dlat_ref, sums_ref):
        @pl.when(pl.program_id(0) == 0)
        def _():
            sums_ref[...] = jnp.zeros_like(sums_ref)

        def norm_bwd(v, dn, g, r):
            rr = _rms(v)
            vhat = v * rr
            u = dn * g
            sums_ref[r:r + 1, :] += jnp.sum(dn * vhat, axis=0, keepdims=True)
            return rr * (u - vhat * jnp.mean(u * vhat, axis=-1, keepdims=True))

        dlat_ref[:, 0:rank] = norm_bwd(lat_ref[:, 0:rank], dcq_ref[...], gq_ref[...], 0).astype(BF16)
        dlat_ref[:, rank:2 * rank] = norm_bwd(lat_ref[:, rank:2 * rank], dckv_ref[...], gkv_ref[...], 1).astype(BF16)
        dlat_ref[:, 2 * rank:W] = _rope_t(dkr_ref[...], cos_ref[...], slo_ref[...], shi_ref[...]).astype(BF16)

    return pl.pallas_call(
        body, name="mla_latent_bwd", grid=(S // ts,),
        in_specs=[pl.BlockSpec((ts, W), lambda i: (i, 0)), half, half, tab, _vec_spec(rank), _vec_spec(rank),
                  tab, tab, tab],
        out_specs=[pl.BlockSpec((ts, W), lambda i: (i, 0)), pl.BlockSpec((8, rank), lambda i: (0, 0))],
        out_shape=[_sds((S, W), BF16), _sds((8, rank), F32)],
        compiler_params=_cparams("arbitrary"))(lat, dcq, dckv, dkr, g_q, g_kv, *rope_tabs)


def _attn_tiles(S):
    t = min(512, S)
    return t, S // t


def _causal_mask(t):
    return lax.broadcasted_iota(jnp.int32, (t, t), 1) <= lax.broadcasted_iota(jnp.int32, (t, t), 0)


def _causal_pairs(nb, q_major):
    if q_major:
        pairs = [(qi, ki) for qi in range(nb) for ki in range(qi + 1)]
    else:
        pairs = [(qi, ki) for ki in range(nb) for qi in range(ki, nb)]
    return jnp.array([p[0] for p in pairs], jnp.int32), jnp.array([p[1] for p in pairs], jnp.int32), len(pairs)


def _heads_per_step(heads):
    return 2 if heads % 2 == 0 else 1


def _attn_fwd_tri(q, kv, kr, heads, scale, rider=NO_RIDER):
    S = q.shape[0]
    t, nb = _attn_tiles(S)
    G = _heads_per_step(heads)
    q_tab, k_tab, n_pairs = _causal_pairs(nb, True)
    n_ri, n_ro = len(rider.operands), len(rider.out_shape)

    def body(qt_ref, kt_ref, *refs):
        q_ref, kv_ref, kr_ref = refs[:3]
        r_in = refs[3:3 + n_ri]
        o_ref, lse_ref = refs[3 + n_ri:5 + n_ri]
        r_out = refs[5 + n_ri:5 + n_ri + n_ro]
        m_scr, acc_scr = refs[5 + n_ri + n_ro:7 + n_ri + n_ro]
        r_sems = refs[7 + n_ri + n_ro:]
        h, p = pl.program_id(0), pl.program_id(1)
        qi, ki = qt_ref[p], kt_ref[p]

        if rider.start is not None:
            @pl.when((h == 0) & (p == 0))
            def _():
                rider.start(r_in, r_out, r_sems)

        @pl.when(ki == 0)
        def _():
            m_scr[...] = jnp.full_like(m_scr, NEG_INF)
            acc_scr[...] = jnp.zeros_like(acc_scr)

        def step(diagonal):
            ones = jnp.ones((t, LANES), BF16)
            for g in range(G):
                kcat = jnp.concatenate([kv_ref[:, g * QK_PAD:g * QK_PAD + QK_NOPE], kr_ref[...]], axis=1)
                vext = jnp.concatenate([kv_ref[:, g * QK_PAD + QK_NOPE:(g + 1) * QK_PAD], ones], axis=1)
                s = lax.dot_general(q_ref[:, g * QK_PAD:(g + 1) * QK_PAD], kcat, DIMS_NT,
                                    preferred_element_type=F32) * scale
                if diagonal:
                    s = jnp.where(_causal_mask(t), s, NEG_INF)
                m_prev = m_scr[g]
                m_new = jnp.maximum(m_prev, jnp.max(s, axis=-1, keepdims=True))
                alpha = jnp.exp(m_prev - m_new)
                pr = jnp.exp(s - jnp.tile(m_new, (1, t // LANES)))
                acc_scr[g] = jnp.tile(alpha, (1, 2)) * acc_scr[g] + lax.dot_general(
                    pr.astype(BF16), vext, DIMS_NN, preferred_element_type=F32)
                m_scr[g] = m_new

        @pl.when(ki < qi)
        def _():
            step(False)

        @pl.when(ki == qi)
        def _():
            step(True)
            for g in range(G):
                acc = acc_scr[g]
                o_ref[:, g * V_HEAD:(g + 1) * V_HEAD] = (acc[:, 0:V_HEAD] / acc[:, V_HEAD:2 * V_HEAD]).astype(BF16)
                lse_ref[g] = m_scr[g] + jnp.log(acc[:, V_HEAD:2 * V_HEAD])

        if rider.finish is not None:
            halfway = rider.mid is not None and heads // G >= 2
            if halfway:
                @pl.when((h == heads // G // 2) & (p == 0))
                def _():
                    rider.mid(r_in, r_out, r_sems)

            @pl.when((h == heads // G - 1) & (p == n_pairs - 1))
            def _():
                if halfway:
                    rider.finish(r_in, r_out, r_sems)
                else:
                    _rider_end(rider, r_in, r_out, r_sems)

    res = pl.pallas_call(
        body, name="mla_attn_fwd",
        grid_spec=pltpu.PrefetchScalarGridSpec(
            num_scalar_prefetch=2, grid=(heads // G, n_pairs),
            in_specs=[pl.BlockSpec((t, G * QK_PAD), lambda h, p, qt, kt: (qt[p], h)),
                      pl.BlockSpec((t, G * QK_PAD), lambda h, p, qt, kt: (kt[p], h)),
                      pl.BlockSpec((t, LANES), lambda h, p, qt, kt: (kt[p], 0))] + [ANY] * n_ri,
            out_specs=[pl.BlockSpec((t, G * V_HEAD), lambda h, p, qt, kt: (qt[p], h)),
                       pl.BlockSpec((G, t, LANES), lambda h, p, qt, kt: (h, qt[p], 0))] + [ANY] * n_ro,
            scratch_shapes=[pltpu.VMEM((G, t, LANES), F32), pltpu.VMEM((G, t, 2 * V_HEAD), F32)] + list(rider.sems)),
        out_shape=[_sds((S, heads * V_HEAD), BF16), _sds((heads, S, LANES), F32)] + list(rider.out_shape),
        input_output_aliases={5 + i: 2 + o for i, o in rider.aliases.items()},
        compiler_params=_cparams("arbitrary", "arbitrary"))(q_tab, k_tab, q, kv, kr, *rider.operands)
    return res[0], res[1], res[2:]


def _attn_bwd_tri(q, kv, kr, o, do, lse, rope_tabs, heads, scale, rider=NO_RIDER):
    S = q.shape[0]
    t, nb = _attn_tiles(S)
    G = _heads_per_step(heads)
    q_tab, k_tab, n_pairs = _causal_pairs(nb, False)
    n_ri, n_ro = len(rider.operands), len(rider.out_shape)
    rep = t // LANES

    tabs = jnp.concatenate(rope_tabs, axis=1)

    def body(qt_ref, kt_ref, *refs):
        q_ref, kv_ref, kr_ref, o_ref, do_ref, lse_ref, tabs_ref = refs[:7]
        cos_ref, slo_ref, shi_ref = (tabs_ref.at[:, pl.ds(i * LANES, LANES)] for i in range(3))
        r_in = refs[7:7 + n_ri]
        dq_ref, dkv_ref, dkr_ref = refs[7 + n_ri:10 + n_ri]
        r_out = refs[10 + n_ri:10 + n_ri + n_ro]
        dq_scr, dk_scr, dv_scr, dkr_scr, delta_scr = refs[10 + n_ri + n_ro:15 + n_ri + n_ro]
        r_sems = refs[15 + n_ri + n_ro:]
        h, p = pl.program_id(0), pl.program_id(1)
        qi, ki = qt_ref[p], kt_ref[p]
        q_rows = pl.ds(pl.multiple_of(qi * t, t), t)
        k_rows = pl.ds(pl.multiple_of(ki * t, t), t)

        @pl.when(ki == 0)
        def _():
            for g in range(G):
                cols = slice(g * V_HEAD, (g + 1) * V_HEAD)
                d = jnp.sum(do_ref[:, cols].astype(F32) * o_ref[:, cols].astype(F32), axis=-1, keepdims=True)
                delta_scr[g, q_rows, :] = jnp.broadcast_to(d, (t, LANES))

        if rider.start is not None:
            @pl.when((h == 0) & (p == 0))
            def _():
                rider.start(r_in, r_out, r_sems)

        @pl.when(p == 0)
        def _():
            dq_scr[...] = jnp.zeros_like(dq_scr)

        @pl.when((h == 0) & (p == 0))
        def _():
            dkr_scr[...] = jnp.zeros_like(dkr_scr)

        @pl.when(qi == ki)
        def _():
            dk_scr[...] = jnp.zeros_like(dk_scr)
            dv_scr[...] = jnp.zeros_like(dv_scr)

        def step(diagonal):
            for g in range(G):
                qv = q_ref[:, g * QK_PAD:(g + 1) * QK_PAD]
                kcat = jnp.concatenate([kv_ref[:, g * QK_PAD:g * QK_PAD + QK_NOPE], kr_ref[...]], axis=1)
                s = lax.dot_general(qv, kcat, DIMS_NT, preferred_element_type=F32) * scale
                pr = jnp.exp(s - jnp.tile(lse_ref[g], (1, rep)))
                if diagonal:
                    pr = jnp.where(_causal_mask(t), pr, 0.0)
                dov = do_ref[:, g * V_HEAD:(g + 1) * V_HEAD]
                dv_scr[g] += lax.dot_general(pr.astype(BF16), dov, DIMS_TN, preferred_element_type=F32)
                dp = lax.dot_general(dov, kv_ref[:, g * QK_PAD + QK_NOPE:(g + 1) * QK_PAD], DIMS_NT,
                                     preferred_element_type=F32)
                ds = (pr * (dp - jnp.tile(delta_scr[g, q_rows, :], (1, rep))) * scale).astype(BF16)
                dk_scr[g] += lax.dot_general(ds, qv, DIMS_TN, preferred_element_type=F32)
                dq_scr[q_rows, g * QK_PAD:(g + 1) * QK_PAD] += lax.dot_general(ds, kcat, DIMS_NN,
                                                                               preferred_element_type=F32)

        @pl.when(qi > ki)
        def _():
            step(False)

        @pl.when(qi == ki)
        def _():
            step(True)
            for g in range(G):
                dqv = dq_scr[q_rows, g * QK_PAD:(g + 1) * QK_PAD]
                dq_ref[q_rows, g * QK_PAD:(g + 1) * QK_PAD] = jnp.concatenate(
                    [dqv[:, 0:QK_NOPE], _rope_t(dqv[:, QK_NOPE:QK_PAD], cos_ref[...], slo_ref[...], shi_ref[...])],
                    axis=1).astype(BF16)

        @pl.when(qi == nb - 1)
        def _():
            for g in range(G):
                dkv_ref[:, g * QK_PAD:(g + 1) * QK_PAD] = jnp.concatenate(
                    [dk_scr[g][:, 0:QK_NOPE], dv_scr[g]], axis=1).astype(BF16)
                dkr_scr[k_rows, :] += dk_scr[g][:, QK_NOPE:QK_PAD]

        @pl.when((h == heads // G - 1) & (p == n_pairs - 1))
        def _():
            dkr_ref[...] = dkr_scr[...]
            if rider.finish is not None:
                _rider_end(rider, r_in, r_out, r_sems)

    q_blk = lambda w: pl.BlockSpec((t, G * w), lambda h, p, qt, kt: (qt[p], h))
    stat = pl.BlockSpec((G, t, LANES), lambda h, p, qt, kt: (h, qt[p], 0))
    tab = pl.BlockSpec((t, LANES), lambda h, p, qt, kt: (kt[p], 0))
    res = pl.pallas_call(
        body, name="mla_attn_bwd",
        grid_spec=pltpu.PrefetchScalarGridSpec(
            num_scalar_prefetch=2, grid=(heads // G, n_pairs),
            in_specs=[q_blk(QK_PAD),
                      pl.BlockSpec((t, G * QK_PAD), lambda h, p, qt, kt: (kt[p], h)),
                      tab, q_blk(V_HEAD), q_blk(V_HEAD), stat,
                      pl.BlockSpec((t, 3 * LANES), lambda h, p, qt, kt: (kt[p], 0))] + [ANY] * n_ri,
            out_specs=[pl.BlockSpec((S, G * QK_PAD), lambda h, p, qt, kt: (0, h)),
                       pl.BlockSpec((t, G * QK_PAD), lambda h, p, qt, kt: (kt[p], h)),
                       pl.BlockSpec((S, LANES), lambda h, p, qt, kt: (0, 0))] + [ANY] * n_ro,
            scratch_shapes=[pltpu.VMEM((S, G * QK_PAD), F32), pltpu.VMEM((G, t, QK_PAD), F32),
                            pltpu.VMEM((G, t, V_HEAD), F32), pltpu.VMEM((S, LANES), F32),
                            pltpu.VMEM((G, S, LANES), F32)] + list(rider.sems)),
        out_shape=[_sds((S, heads * QK_PAD), BF16), _sds((S, heads * QK_PAD), BF16), _sds((S, LANES), F32)]
        + list(rider.out_shape),
        input_output_aliases={9 + i: 3 + o for i, o in rider.aliases.items()},
        compiler_params=_cparams("arbitrary", "arbitrary"))(q_tab, k_tab, q, kv, kr, o, do, lse, tabs,
                                                            *rider.operands)
    return res[0], res[1], res[2], res[3:]


def _shift_down(z, n, rows):
    return jnp.where(rows >= n, pltpu.roll(z, n, 0), 0.0)


def _shift_up(z, n, rows, S):
    return jnp.where(rows < S - n, pltpu.roll(z, S - n, 0), 0.0)


def _conv_specs(S, tc):
    strip = lambda p: pl.BlockSpec((None, S, tc), lambda j: (p, 0, j))
    return strip(0), strip(1), strip(2), pl.BlockSpec((3, tc), lambda j: (0, j))


def _conv_fwd(proj3, w):
    _, S, D = proj3.shape
    tc = LANES

    def body(b_ref, c_ref, u_ref, w_ref, out_ref):
        z = c_ref[...].astype(F32) * u_ref[...].astype(F32)
        rows = lax.broadcasted_iota(jnp.int32, (S, tc), 0)
        zc = w_ref[0:1, :] * _shift_down(z, 2, rows) + w_ref[1:2, :] * _shift_down(z, 1, rows) + w_ref[2:3, :] * z
        out_ref[...] = (b_ref[...].astype(F32) * zc).astype(BF16)

    return pl.pallas_call(
        body, name="conv_fwd", grid=(D // tc,), in_specs=list(_conv_specs(S, tc)),
        out_specs=pl.BlockSpec((S, tc), lambda j: (0, j)), out_shape=_sds((S, D), BF16),
        compiler_params=_cparams("parallel"))(proj3, proj3, proj3, w)


def _conv_bwd(dbz, proj3, w):
    _, S, D = proj3.shape
    tc = LANES

    def body(d_ref, b_ref, c_ref, u_ref, w_ref, dp_ref, dw_ref):
        cv, uv, dv = c_ref[...].astype(F32), u_ref[...].astype(F32), d_ref[...].astype(F32)
        z = cv * uv
        rows = lax.broadcasted_iota(jnp.int32, (S, tc), 0)
        z1, z2 = _shift_down(z, 1, rows), _shift_down(z, 2, rows)
        zc = w_ref[0:1, :] * z2 + w_ref[1:2, :] * z1 + w_ref[2:3, :] * z
        dp_ref[0] = (dv * zc).astype(BF16)
        dzc = dv * b_ref[...].astype(F32)
        dz = w_ref[2:3, :] * dzc + w_ref[1:2, :] * _shift_up(dzc, 1, rows, S) + w_ref[0:1, :] * _shift_up(dzc, 2, rows, S)
        dp_ref[1] = (dz * uv).astype(BF16)
        dp_ref[2] = (dz * cv).astype(BF16)
        dw_ref[0:1, :] = jnp.sum(dzc * z2, axis=0, keepdims=True)
        dw_ref[1:2, :] = jnp.sum(dzc * z1, axis=0, keepdims=True)
        dw_ref[2:3, :] = jnp.sum(dzc * z, axis=0, keepdims=True)

    sb, sc_, su, sw = _conv_specs(S, tc)
    return pl.pallas_call(
        body, name="conv_bwd", grid=(D // tc,),
        in_specs=[pl.BlockSpec((S, tc), lambda j: (0, j)), sb, sc_, su, sw],
        out_specs=[pl.BlockSpec((3, S, tc), lambda j: (0, 0, j)), pl.BlockSpec((3, tc), lambda j: (0, j))],
        out_shape=[_sds((3, S, D), BF16), _sds((3, D), F32)],
        compiler_params=_cparams("parallel"))(dbz, proj3, proj3, proj3, w)


def _silu(c_all):
    def body(c_ref, o_ref):
        cv = c_ref[...]
        o_ref[...] = cv * (1.0 / (1.0 + jnp.exp(-cv)))

    vm = pl.BlockSpec(memory_space=pltpu.VMEM)
    return pl.pallas_call(body, name="cond_silu", in_specs=[vm], out_specs=vm, out_shape=_sds(c_all.shape, F32))(c_all)


def _mod_fwd(cond, w_mod, b_cols):
    L, D, ncol = w_mod.shape
    B = cond.shape[0]
    tk, tn = min(512, D), min(1024, ncol)
    nk = D // tk

    def body(c_ref, w_ref, b_ref, out_ref, acc):
        kk = pl.program_id(2)
        part = lax.dot_general(c_ref[...].astype(BF16), w_ref[...].astype(BF16), DIMS_NN, preferred_element_type=F32)

        @pl.when(kk == 0)
        def _():
            acc[...] = part

        @pl.when(kk > 0)
        def _():
            acc[...] += part

        @pl.when(kk == nk - 1)
        def _():
            out_ref[...] = acc[...] + b_ref[...]

    return pl.pallas_call(
        body, name="mod_fwd", grid=(L, ncol // tn, nk),
        in_specs=[pl.BlockSpec((B, tk), lambda l, j, k: (0, k)),
                  pl.BlockSpec((None, tk, tn), lambda l, j, k: (l, k, j)),
                  pl.BlockSpec((None, 1, tn), lambda l, j, k: (l, 0, j))],
        out_specs=pl.BlockSpec((None, B, tn), lambda l, j, k: (l, 0, j)),
        out_shape=_sds((L, B, ncol), F32),
        scratch_shapes=[pltpu.VMEM((B, tn), F32)],
        compiler_params=_cparams("parallel", "parallel", "arbitrary"))(cond, w_mod, b_cols)


def _adamw_math(w, g, m, v):
    m = ADAM_B1 * m + (1.0 - ADAM_B1) * g
    v = ADAM_B2 * v + (1.0 - ADAM_B2) * (g * g)
    m_hat = m / (1.0 - ADAM_B1 ** ADAM_STEP)
    v_hat = v / (1.0 - ADAM_B2 ** ADAM_STEP)
    delta = -ADAM_LR * (m_hat / (jnp.sqrt(v_hat) + ADAM_EPS) + ADAM_WD * w)
    return delta, m, v


def _adamw(name, w, g, m, v, emit_grad=False):
    shape = w.shape
    cols = shape[-1] if w.ndim <= 3 else shape[-2] * shape[-1]
    rows = w.size // cols
    w2, g2, m2, v2 = (t.reshape(rows, cols) for t in (w, g, m, v))
    tr = _row_tile(rows, cols * 4, limit=2 * 1024 * 1024, mult=8)
    spec = pl.BlockSpec((tr, cols), lambda i: (i, 0))
    n_out = 4 if emit_grad else 3

    def body(w_ref, g_ref, m_ref, v_ref, d_ref, nm_ref, nv_ref, *rest):
        gv = g_ref[...]
        d, nm, nv = _adamw_math(w_ref[...], gv, m_ref[...], v_ref[...])
        d_ref[...] = d
        nm_ref[...] = nm
        nv_ref[...] = nv
        if emit_grad:
            rest[0][...] = gv

    outs = pl.pallas_call(body, name=name, grid=(rows // tr,), in_specs=[spec] * 4, out_specs=[spec] * n_out,
                          out_shape=[_sds((rows, cols), F32)] * n_out, compiler_params=_cparams("parallel"))(w2, g2, m2, v2)
    return tuple(t.reshape(shape) for t in outs)


def _adamw_mod(w, cond_t, dmod_cols, m, v, rider=NO_RIDER):
    L, D, ncol = w.shape
    B = cond_t.shape[1]
    tr, tc = min(256, D), min(1024, ncol)
    blk = pl.BlockSpec((None, tr, tc), lambda l, i, j: (l, i, j))
    grid = (L, D // tr, ncol // tc)
    n_ri, n_ro = len(rider.operands), len(rider.out_shape)

    def body(*refs):
        w_ref, ct_ref, dm_ref, m_ref, v_ref = refs[:5]
        r_in = refs[5:5 + n_ri]
        g_ref, d_ref, nm_ref, nv_ref = refs[5 + n_ri:9 + n_ri]
        r_out = refs[9 + n_ri:9 + n_ri + n_ro]
        r_sems = refs[9 + n_ri + n_ro:]
        ids = [pl.program_id(a) for a in range(3)]

        if rider.start is not None:
            @pl.when((ids[0] == 0) & (ids[1] == 0) & (ids[2] == 0))
            def _():
                rider.start(r_in, r_out, r_sems)

        g = lax.dot_general(ct_ref[...], dm_ref[...], DIMS_NN, precision=lax.Precision.HIGHEST,
                            preferred_element_type=F32)
        d, nm, nv = _adamw_math(w_ref[...], g, m_ref[...], v_ref[...])
        g_ref[...] = g
        d_ref[...] = d
        nm_ref[...] = nm
        nv_ref[...] = nv

        if rider.finish is not None:
            @pl.when((ids[0] == grid[0] - 1) & (ids[1] == grid[1] - 1) & (ids[2] == grid[2] - 1))
            def _():
                _rider_end(rider, r_in, r_out, r_sems)

    hosted = rider.start is not None
    res = pl.pallas_call(
        body, name="adamw_w_mod", grid=grid,
        in_specs=[blk, pl.BlockSpec((tr, B), lambda l, i, j: (i, 0)),
                  pl.BlockSpec((None, B, tc), lambda l, i, j: (l, 0, j)), blk, blk] + [ANY] * n_ri,
        out_specs=[blk] * 4 + [ANY] * n_ro, out_shape=[_sds((L, D, ncol), F32)] * 4 + list(rider.out_shape),
        scratch_shapes=list(rider.sems), input_output_aliases={5 + i: 4 + o for i, o in rider.aliases.items()},
        compiler_params=_cparams(*(("arbitrary",) * 3 if hosted else ("parallel",) * 3)))(
            w, cond_t, dmod_cols, m, v, *rider.operands)
    return (*res[:4], res[4:])


def _cast_into_full(name, ws, kinds, k_idx, rider=NO_RIDER):
    L, R, C = ws[0].shape
    assert all(w.shape == (L, R, C) for w in ws)
    n = len(ws)
    Rh = R // 2
    tr = _row_tile(Rh, C * 4)
    grid = (L, 2, Rh // tr)
    out_shape, out_specs = [], []
    for kind in kinds:
        if kind == "row":
            out_shape.append(_sds((L, N_CHIPS, 2, Rh, C), BF16))
            out_specs.append(pl.BlockSpec((None, None, None, tr, C), lambda l, h, i, k_ref: (l, k_ref[0], h, i, 0)))
        else:
            out_shape.append(_sds((L, 2, Rh, N_CHIPS * C), BF16))
            out_specs.append(pl.BlockSpec((None, None, tr, C), lambda l, h, i, k_ref: (l, h, i, k_ref[0])))
    n_ri, n_ro = len(rider.operands), len(rider.out_shape)

    def body(k_ref, *refs):
        r_in = refs[n:n + n_ri]
        r_out = refs[2 * n + n_ri:2 * n + n_ri + n_ro]
        r_sems = refs[2 * n + n_ri + n_ro:]
        ids = [pl.program_id(a) for a in range(3)]
        if rider.start is not None:
            @pl.when((ids[0] == 0) & (ids[1] == 0) & (ids[2] == 0))
            def _():
                rider.start(r_in, r_out, r_sems)
        for a in range(n):
            refs[n + n_ri + a][...] = refs[a][...].astype(BF16)
        if rider.finish is not None:
            @pl.when((ids[0] == grid[0] - 1) & (ids[1] == grid[1] - 1) & (ids[2] == grid[2] - 1))
            def _():
                _rider_end(rider, r_in, r_out, r_sems)

    hosted = rider.start is not None
    res = pl.pallas_call(
        body, name=name,
        grid_spec=pltpu.PrefetchScalarGridSpec(
            num_scalar_prefetch=1, grid=grid,
            in_specs=[pl.BlockSpec((None, None, tr, C), lambda l, h, i, k_ref: (l, h, i, 0))] * n + [ANY] * n_ri,
            out_specs=out_specs + [ANY] * n_ro, scratch_shapes=list(rider.sems)),
        out_shape=out_shape + list(rider.out_shape),
        input_output_aliases={1 + n + i: n + o for i, o in rider.aliases.items()},
        compiler_params=_cparams(*(("arbitrary",) * 3 if hosted else ("parallel",) * 3)))(
            k_idx, *[w.reshape(L, 2, Rh, C) for w in ws], *rider.operands)
    return res[:n], res[n:]


def _pair_sum(name, g5, ra, c_idx):
    L, A, _, Rh, Cc = g5.shape
    tr = _row_tile(Rh, Cc * 4)

    def body(c_ref, g_ref, r_ref, o_ref):
        o_ref[...] = (g_ref[...].astype(F32) + r_ref[...].astype(F32)).astype(BF16)

    blk = pl.BlockSpec((None, None, tr, Cc), lambda l, a, i, c_ref: (l, a, i, 0))
    return pl.pallas_call(
        body, name=name,
        grid_spec=pltpu.PrefetchScalarGridSpec(
            num_scalar_prefetch=1, grid=(L, A, Rh // tr),
            in_specs=[pl.BlockSpec((None, None, None, tr, Cc), lambda l, a, i, c_ref: (l, a, c_ref[0], i, 0)), blk],
            out_specs=blk),
        out_shape=_sds((L, A, Rh, Cc), BF16),
        compiler_params=_cparams("parallel", "parallel", "parallel"))(c_idx, g5, ra)


def _chip_sum(name, p, rb, kc_idx, kind, layer=0, n_layers=1, prev=None):
    _, A, Rh, Cc = p.shape
    C = rb.shape[-1]
    tr = _row_tile(Rh, C * 4)
    if kind == "row":
        own = pl.BlockSpec((None, None, tr, C), lambda i, kc: (0, kc[0], i, 0))
    else:
        own = pl.BlockSpec((None, None, tr, C), lambda i, kc: (0, 0, i, kc[0]))
    peer = lambda j: pl.BlockSpec((None, None, tr, C), lambda i, kc: (j, 0, i, 0))

    def body(kc_ref, p_ref, r0_ref, r1_ref, r2_ref, *rest):
        o_ref = rest[-1]
        o_ref[...] = ((p_ref[...].astype(F32) + r0_ref[...].astype(F32)) + r1_ref[...].astype(F32)) + r2_ref[...].astype(F32)

    operands = [kc_idx, p, rb, rb, rb] + ([prev] if prev is not None else [])
    return pl.pallas_call(
        body, name=name,
        grid_spec=pltpu.PrefetchScalarGridSpec(
            num_scalar_prefetch=1, grid=(Rh // tr,),
            in_specs=[own, peer(0), peer(1), peer(2)] + ([ANY] if prev is not None else []),
            out_specs=pl.BlockSpec((None, None, tr, C), lambda i, kc: (layer, kc[1], i, 0))),
        out_shape=_sds((n_layers, 2, Rh, C), F32),
        input_output_aliases={5: 0} if prev is not None else {},
        compiler_params=_cparams("parallel"))(*operands)


def _mesh_place():
    x, y, c = lax.axis_index("x"), lax.axis_index("y"), lax.axis_index("c")
    chips = [(1 - x, y), (x, 1 - y), (1 - x, 1 - y)]
    return x, y, c, chips


def _remote(src, dst, send_sem, recv_sem, to):
    return pltpu.make_async_remote_copy(src_ref=src, dst_ref=dst, send_sem=send_sem, recv_sem=recv_sem,
                                        device_id=to, device_id_type=MESH_ID)


def _small_allgather(name, v, with_sum=False, rider=NO_RIDER):
    R, N = v.shape
    n_ri, n_ro, n_own = len(rider.operands), len(rider.out_shape), 2 if with_sum else 1

    def body(*refs):
        r_in = refs[1:1 + n_ri]
        r_out = refs[1 + n_ri + n_own:1 + n_ri + n_own + n_ro]
        r_sems = refs[1 + n_ri + n_own + n_ro + 3:]
        own = (refs[0],) + refs[1 + n_ri:1 + n_ri + n_own] + refs[1 + n_ri + n_own + n_ro:1 + n_ri + n_own + n_ro + 3]
        if with_sum:
            x_ref, out_ref, sum_ref, send_sems, recv_sems, local_sem = own
        else:
            x_ref, out_ref, send_sems, recv_sems, local_sem = own
        if rider.start is not None:
            rider.start(r_in, r_out, r_sems)
        x, y, c, chips = _mesh_place()
        me, sibling = (x, y, c), (x, y, 1 - c)

        def rows(px, py, pc):
            return out_ref.at[pl.ds((4 * px + 2 * py + pc) * R, R), :]

        def copy(k, block, to, src=None):
            return _remote(rows(*block) if src is None else src, rows(*block), send_sems.at[k], recv_sems.at[k], to)

        mine = pltpu.make_async_copy(x_ref, rows(*me), local_sem)
        mine.start()
        first = [copy(0, me, sibling, src=x_ref)]
        first += [copy(1 + j, me, (*chip, c), src=x_ref) for j, chip in enumerate(chips)]
        for cp in first:
            cp.start()
        passed = [copy(4 + j, (*chip, c), sibling) for j, chip in enumerate(chips)]
        for j, chip in enumerate(chips):
            copy(1 + j, (*chip, c), me).wait_recv()
            passed[j].start()
        copy(0, sibling, me).wait_recv()
        for j, chip in enumerate(chips):
            copy(4 + j, (*chip, 1 - c), me).wait_recv()
        for cp in first + passed:
            cp.wait_send()
        mine.wait()
        if with_sum:
            total = out_ref[0:R, :]
            for p in range(1, 8):
                total = total + out_ref[p * R:(p + 1) * R, :]
            sum_ref[...] = total
        if rider.finish is not None:
            _rider_end(rider, r_in, r_out, r_sems)

    vm = pl.BlockSpec(memory_space=pltpu.VMEM)
    out_shape = [_sds((8 * R, N), F32)] + ([_sds((R, N), F32)] if with_sum else [])
    res = pl.pallas_call(
        body, name=name, out_shape=out_shape + list(rider.out_shape), in_specs=[vm] + [ANY] * n_ri,
        out_specs=[vm] * n_own + [ANY] * n_ro,
        scratch_shapes=[pltpu.SemaphoreType.DMA((7,)), pltpu.SemaphoreType.DMA((7,)), pltpu.SemaphoreType.DMA]
        + list(rider.sems),
        input_output_aliases={1 + i: n_own + o for i, o in rider.aliases.items()},
        compiler_params=pltpu.CompilerParams(vmem_limit_bytes=VMEM_LIMIT_BYTES))(v, *rider.operands)
    if rider.start is not None:
        return (*res[:n_own], res[n_own:])
    return res if with_sum else res[0]


def _full_place(ref, kind, C, kk, half, layer=None):
    lead = slice(None) if layer is None else pl.ds(layer, 1)
    if kind == "row":
        return ref.at[lead, kk, half]
    return ref.at[lead, half, :, pl.ds(pl.multiple_of(kk * C, LANES), C)]


def _gather_rider(fulls, kinds, shard_cols, layers=None, peers=(0, 1, 2)):
    n = len(fulls)
    layers = layers or [None] * n
    rows = [f.shape[3] if kind == "row" else f.shape[2] for f, kind in zip(fulls, kinds)]
    n_chunks = 2 if all(r % 32 == 0 for r in rows) else 1

    def copies(outs, sems):
        x, y, c, chips = _mesh_place()
        k = 2 * x + y

        def place(a, kk, half, ch):
            rc = rows[a] // n_chunks
            return _full_place(outs[a], kinds[a], shard_cols[a], kk, half, layers[a]).at[:, pl.ds(ch * rc, rc), :]

        def copy(a, j, ch, ref, to):
            s = 6 * (n_chunks * a + ch) + j
            return _remote(ref, ref, sems[0].at[s], sems[1].at[s], to)

        return (x, y, c), [(j, chip) for j, chip in enumerate(chips) if j in peers], k, place, copy

    def start(_, outs, sems):
        (x, y, c), chips, k, place, copy = copies(outs, sems)
        for ch in range(n_chunks):
            for j, chip in chips:
                for a in range(n):
                    copy(a, j, ch, place(a, k, c, ch), (*chip, c)).start()

    def pass_on(outs, sems, ch):
        (x, y, c), chips, k, place, copy = copies(outs, sems)
        for j, chip in chips:
            kj = 2 * chip[0] + chip[1]
            for a in range(n):
                copy(a, j, ch, place(a, kj, c, ch), (x, y, c)).wait_recv()
                copy(a, 3 + j, ch, place(a, kj, c, ch), (x, y, 1 - c)).start()

    def mid(_, outs, sems):
        pass_on(outs, sems, 0)

    def finish(_, outs, sems):
        pass_on(outs, sems, n_chunks - 1)
        (x, y, c), chips, k, place, copy = copies(outs, sems)
        for ch in range(n_chunks):
            for j, chip in chips:
                kj = 2 * chip[0] + chip[1]
                for a in range(n):
                    copy(a, 3 + j, ch, place(a, kj, 1 - c, ch), (x, y, c)).wait_recv()
        for ch in range(n_chunks):
            for j, chip in chips:
                kj = 2 * chip[0] + chip[1]
                for a in range(n):
                    copy(a, j, ch, place(a, k, c, ch), (*chip, c)).wait_send()
                    copy(a, 3 + j, ch, place(a, kj, c, ch), (x, y, 1 - c)).wait_send()

    n_sems = 6 * n * n_chunks
    return Rider(tuple(fulls), tuple(_sds(f.shape, BF16) for f in fulls), {a: a for a in range(n)},
                 (pltpu.SemaphoreType.DMA((n_sems,)), pltpu.SemaphoreType.DMA((n_sems,))), start, finish,
                 mid if n_chunks == 2 else None)


def _scatter_rider(ps, kinds, shard_cols, peers=(0, 1, 2), into=None):
    n = len(ps)

    def copies(ins, outs, sems):
        x, y, c, chips = _mesh_place()
        cps = []
        for j, chip in enumerate(chips):
            if j not in peers:
                continue
            kj = 2 * chip[0] + chip[1]
            for a in range(n):
                C = shard_cols[a]
                src = ins[a].at[:, kj] if kinds[a] == "row" else ins[a].at[:, 0, :, pl.ds(pl.multiple_of(kj * C, LANES), C)]
                cps.append(_remote(src, outs[a].at[j], sems[0].at[3 * a + j], sems[1].at[3 * a + j], (*chip, c)))
        return cps

    def start(ins, outs, sems):
        for cp in copies(ins, outs, sems):
            cp.start()

    def finish(ins, outs, sems):
        cps = copies(ins, outs, sems)
        for cp in cps:
            cp.wait_recv()
        for cp in cps:
            cp.wait_send()

    out_shape = tuple(_sds((3, p.shape[0], p.shape[2], C), BF16) for p, C in zip(ps, shard_cols))
    aliases = {n + a: a for a in range(n)} if into is not None else {}
    return Rider(tuple(ps) + tuple(into or ()), out_shape, aliases,
                 (pltpu.SemaphoreType.DMA((3 * n,)), pltpu.SemaphoreType.DMA((3 * n,))), start, finish)


def _run_rider(name, rider):
    n_in, n_out = len(rider.operands), len(rider.out_shape)

    def body(*refs):
        ins, outs, sems = refs[:n_in], refs[n_in:n_in + n_out], refs[n_in + n_out:]
        rider.start(ins, outs, sems)
        _rider_end(rider, ins, outs, sems)

    return pl.pallas_call(
        body, name=name, out_shape=list(rider.out_shape), in_specs=[ANY] * n_in, out_specs=[ANY] * n_out,
        input_output_aliases=dict(rider.aliases), scratch_shapes=list(rider.sems),
        compiler_params=pltpu.CompilerParams(vmem_limit_bytes=VMEM_LIMIT_BYTES))(*rider.operands)


def _exchange_rider(g5s):
    n = len(g5s)

    def copies(ins, outs, sems):
        x, y, c, _ = _mesh_place()
        return [_remote(ins[a].at[:, :, 1 - c], outs[a], sems[0].at[a], sems[1].at[a], (x, y, 1 - c)) for a in range(n)]

    def start(ins, outs, sems):
        for cp in copies(ins, outs, sems):
            cp.start()

    def finish(ins, outs, sems):
        cps = copies(ins, outs, sems)
        for cp in cps:
            cp.wait_recv()
        for cp in cps:
            cp.wait_send()

    out_shape = tuple(_sds((g.shape[0], g.shape[1], g.shape[3], g.shape[4]), BF16) for g in g5s)
    return Rider(tuple(g5s), out_shape, {}, (pltpu.SemaphoreType.DMA((n,)), pltpu.SemaphoreType.DMA((n,))), start, finish)


def _share_rider(fs):
    n = len(fs)

    def start(_, outs, sems):
        x, y, c, _p = _mesh_place()
        for a in range(n):
            mine = outs[a].at[:, c]
            _remote(mine, mine, sems[0].at[a], sems[1].at[a], (x, y, 1 - c)).start()

    def finish(_, outs, sems):
        x, y, c, _p = _mesh_place()
        for a in range(n):
            theirs = outs[a].at[:, 1 - c]
            _remote(theirs, theirs, sems[0].at[a], sems[1].at[a], (x, y, c)).wait_recv()
        for a in range(n):
            mine = outs[a].at[:, c]
            _remote(mine, mine, sems[0].at[a], sems[1].at[a], (x, y, 1 - c)).wait_send()

    return Rider(tuple(fs), tuple(_sds(f.shape, F32) for f in fs), {a: a for a in range(n)},
                 (pltpu.SemaphoreType.DMA((n,)), pltpu.SemaphoreType.DMA((n,))), start, finish)


def _both_riders(r1, r2):
    ni, no, ns = len(r1.operands), len(r1.out_shape), len(r1.sems)
    aliases = dict(r1.aliases)
    aliases.update({ni + i: no + o for i, o in r2.aliases.items()})

    def start(ins, outs, sems):
        r1.start(ins[:ni], outs[:no], sems[:ns])
        r2.start(ins[ni:], outs[no:], sems[ns:])

    def finish(ins, outs, sems):
        _rider_end(r1, ins[:ni], outs[:no], sems[:ns])
        _rider_end(r2, ins[ni:], outs[no:], sems[ns:])

    return Rider(r1.operands + r2.operands, r1.out_shape + r2.out_shape, aliases, r1.sems + r2.sems, start, finish)


def _pack_rows(parts, lane_mult=1024):
    flat = jnp.concatenate([p.reshape(-1).astype(F32) for p in parts])
    n = -(-flat.shape[0] // (8 * lane_mult)) * lane_mult
    return jnp.pad(flat, (0, 8 * n - flat.shape[0])).reshape(8, n)


def _relu2(acc):
    r = jnp.maximum(acc, 0.0)
    return r, r * r


def _times_2r(acc, r):
    return (acc * (2.0 * r.astype(F32)),)


def kernel(x, c, positions, w_mod, b_mod, norm_g, mla_w_in, mla_g_q, mla_g_kv, mla_w_uq, mla_w_ukv, mla_w_o, conv_w_in, conv_w, conv_w_out, mlp_w_up, mlp_w_down, loss_target, m_w_mod, m_b_mod, m_norm_g, m_mla_w_in, m_mla_g_q, m_mla_g_kv, m_mla_w_uq, m_mla_w_ukv, m_mla_w_o, m_conv_w_in, m_conv_w, m_conv_w_out, m_mlp_w_up, m_mlp_w_down, v_w_mod, v_b_mod, v_norm_g, v_mla_w_in, v_mla_g_q, v_mla_g_kv, v_mla_w_uq, v_mla_w_ukv, v_mla_w_o, v_conv_w_in, v_conv_w, v_conv_w_out, v_mlp_w_up, v_mlp_w_down):
    S, D = x.shape[1], x.shape[2]
    Dq = D // N_CHIPS
    ncol = w_mod.shape[2]
    n_mod = N_CHIPS * ncol // D
    F = mlp_w_up.shape[2] * N_CHIPS
    lat_dim = mla_w_in.shape[2]
    rank = mla_g_q.shape[1]
    H = mla_w_uq.shape[2]
    d_qk = mla_w_uq.shape[3]
    assert mla_g_kv.shape[1] == rank and lat_dim == 2 * rank + QK_ROPE and d_qk == QK_NOPE + QK_ROPE
    assert mla_w_ukv.shape[3] == QK_NOPE + V_HEAD and x.shape[0] == 1 and n_mod == 6
    assert norm_g.shape[0] == 2 and mla_w_in.shape[0] == 1 and conv_w_in.shape[0] == 1
    lat_pad = 2 * rank + LANES
    scale = float(d_qk) ** -0.5

    xi, yi, ci = lax.axis_index("x"), lax.axis_index("y"), lax.axis_index("c")
    chip = 2 * xi + yi
    dev = 2 * chip + ci
    c_idx = jnp.reshape(ci, (1,)).astype(jnp.int32)
    k_idx = jnp.reshape(chip, (1,)).astype(jnp.int32)

    n1 = D + 2 * D + 3 * Dq
    g1 = _small_allgather("gather_small_inputs", _pack_rows([c, norm_g, conv_w])).reshape(8, -1)
    c_all = g1[:, :D]
    by_chip = g1[0::2]
    norm_full = jnp.concatenate([by_chip[kk, D:3 * D].reshape(2, 4, Dq) for kk in range(N_CHIPS)], axis=-1)
    convw_full = jnp.concatenate([by_chip[kk, 3 * D:n1].reshape(3, Dq) for kk in range(N_CHIPS)], axis=-1)

    b_cols = lax.dynamic_slice(b_mod, (0, chip * ncol), (2, ncol)).reshape(2, 1, ncol)
    cond_all = _silu(c_all)
    mod_cols = _mod_fwd(cond_all, w_mod, b_cols)
    g2 = _small_allgather("gather_mod", _pack_rows([mod_cols]))
    g2 = g2.reshape(8, -1)[0::2, :2 * 8 * ncol].reshape(N_CHIPS, 2, 8, ncol)
    mod_all = jnp.transpose(g2, (2, 1, 0, 3)).reshape(8, 2, n_mod * D)
    mod_me = lax.dynamic_index_in_dim(mod_all, dev, axis=0, keepdims=False)
    mods = [[mod_me[l, i * D:(i + 1) * D].reshape(1, D) for i in range(n_mod)] for l in range(2)]
    ng = [[norm_full[l, i].reshape(1, D) for i in range(4)] for l in range(2)]

    pos = positions[0].astype(F32)
    inv_freq = ROPE_THETA ** (-jnp.arange(0, QK_ROPE, 2, dtype=F32) / QK_ROPE)
    ang = pos[:, None] * inv_freq
    cos, sin = jnp.cos(ang), jnp.sin(ang)
    zero = jnp.zeros_like(cos)
    rope_tabs = (jnp.concatenate([cos, cos, zero, zero], axis=1),
                 jnp.concatenate([-sin, zero, zero, zero], axis=1),
                 jnp.concatenate([zero, sin, zero, zero], axis=1))

    weights = [("mla_w_in", mla_w_in, "row"), ("mla_w_uq", mla_w_uq.reshape(1, rank // N_CHIPS, H * d_qk), "row"),
               ("mla_w_ukv", mla_w_ukv.reshape(1, rank // N_CHIPS, H * QK_PAD), "row"), ("mla_w_o", mla_w_o, "row"),
               ("conv_w_in", conv_w_in, "col"), ("conv_w_out", conv_w_out, "row"),
               ("mlp_w_up", mlp_w_up, "col"), ("mlp_w_down", mlp_w_down, "row")]
    kinds = [k for _, _, k in weights]
    shard_shapes = [w.shape for _, w, _ in weights]
    shard_cols = [s[2] for s in shard_shapes]
    W_IN, W_UQ, W_UKV, W_O, W_CIN, W_COUT, W_UP, W_DOWN = range(8)
    mla_idx = [W_IN, W_UQ, W_UKV, W_O]
    casted = [_cast_into_full("cast_" + nm, [w], [kind], k_idx)[0][0] for nm, w, kind in weights[:W_UP]]

    def view(i, buf):
        L, R, C = shard_shapes[i]
        return buf.reshape((L, N_CHIPS * R, C) if kinds[i] == "row" else (L, R, N_CHIPS * C))

    NEIGHBOURS, DIAGONAL = (0, 1), (2,)

    def gather_of(bufs, idx, layers=None, peers=(0, 1, 2)):
        return _gather_rider(bufs, [kinds[i] for i in idx], [shard_cols[i] for i in idx], layers, peers)

    def scatter_of(ps, idx, peers=(0, 1, 2), into=None):
        return _scatter_rider(ps, [kinds[i] for i in idx], [shard_cols[i] for i in idx], peers, into)

    def halves(items):
        g5s = []
        for _, i, g in items:
            _, R, C = shard_shapes[i]
            g5s.append(g.reshape((1, N_CHIPS, 2, R // 2, C) if kinds[i] == "row" else (1, 1, 2, R // 2, N_CHIPS * C)))
        return g5s

    def pair_sums(items, g5s, ras):
        return [_pair_sum("pair_sum_" + nm, g5, ra, c_idx) for (nm, _, _), g5, ra in zip(items, g5s, ras)]

    mlp_casted, got = _cast_into_full("cast_mlp_w", [mlp_w_up, mlp_w_down], [kinds[W_UP], kinds[W_DOWN]], k_idx,
                                      gather_of([casted[W_IN]], [W_IN]))
    casted += list(mlp_casted)
    w_in_p = jnp.pad(view(W_IN, got[0])[0], ((0, 0), (0, lat_pad - lat_dim)))
    HV = H * V_HEAD

    def layer_b(l, transposed):
        if transposed:
            return lambda tm, tn, tk: pl.BlockSpec((None, tn, tk), lambda i, j, k: (l, j, k))
        return lambda tm, tn, tk: pl.BlockSpec((None, tk, tn), lambda i, j, k: (l, k, j))

    def mlp_up(tag, l, h, w, rider=NO_RIDER):
        return _mm("mlp_up_" + tag, h, w, "nn", S, F, D, [_sds((S, F), BF16)] * 2, epilogue=_relu2,
                   b_spec=layer_b(l, False), rider=rider)

    def mlp_down(tag, l, a2, w, rider=NO_RIDER):
        return _mm("mlp_down_" + tag, a2, w, "nn", S, D, F, [_sds((S, D), BF16)], b_spec=layer_b(l, False), rider=rider)

    def mlp_bwd(tag, l, h, r, a2, dy, first=NO_RIDER, plan=None):
        res = _mm("mlp_down_dx_" + tag, dy, w_down, "nt", S, F, D, [_sds((S, F), BF16)], epilogue=_times_2r,
                  b_spec=layer_b(l, True), rider=first,
                  extras=[(r, lambda tm, tn, tk: pl.BlockSpec((tm, tn), lambda i, j, k: (i, j)))])
        (da,), got_first = res if first.start is not None else (res, ())
        second, third, fourth = plan(got_first) if plan else (NO_RIDER,) * 3
        res = _mm("mlp_down_dw_" + tag, a2, dy, "tn", F, D, S, [_sds((F, D), BF16)], rider=second)
        (dw_down,), got_second = res if plan else (res, ())
        res = _mm("mlp_up_dx_" + tag, da, w_up, "nt", S, D, F, [_sds((S, D), BF16)], b_spec=layer_b(l, True), rider=third)
        (dh,), got_third = res if plan else (res, ())
        res = _mm("mlp_up_dw_" + tag, h, da, "tn", D, F, S, [_sds((D, F), BF16)], rider=fourth)
        (dw_up,), got_fourth = res if plan else (res, ())
        return dh, dw_up, dw_down, (got_first, got_second, got_third, got_fourth)

    x0 = x[0]
    sh1, sc1, gt1, sh2, sc2, gt2 = mods[0]
    (h1,) = _fwd_boundary("fwd_boundary_0", x0, None, None, None, ng[0][0], sc1, sh1)
    (lat,), (uq_buf, ukv_buf) = _mm("mla_in", h1, w_in_p, "nn", S, lat_pad, D, [_sds((S, lat_pad), F32)], tn=lat_pad,
                                    rider=gather_of([casted[W_UQ], casted[W_UKV]], [W_UQ, W_UKV]))
    w_q_p = jnp.pad(view(W_UQ, uq_buf)[0].reshape(rank, H, d_qk), ((0, 0), (0, 0), (0, QK_PAD - d_qk))).reshape(rank, H * QK_PAD)
    w_ukv = view(W_UKV, ukv_buf)[0]
    cq, ckv, kr = _latent_fwd(lat, mla_g_q, mla_g_kv, rope_tabs, rank)

    def rope_q(acc, cos_p, sin_lo, sin_hi):
        parts = []
        for hh in range(acc.shape[1] // QK_PAD):
            parts.append(acc[:, hh * QK_PAD:hh * QK_PAD + QK_NOPE])
            parts.append(_rope(acc[:, hh * QK_PAD + QK_NOPE:(hh + 1) * QK_PAD], cos_p, sin_lo, sin_hi))
        return (jnp.concatenate(parts, axis=1),)

    tab_extra = lambda tm, tn, tk: pl.BlockSpec((tm, LANES), lambda i, j, k: (i, 0))
    (q,), (o_buf,) = _mm("mla_q", cq, w_q_p, "nn", S, H * QK_PAD, rank, [_sds((S, H * QK_PAD), BF16)], epilogue=rope_q,
                         extras=[(t, tab_extra) for t in rope_tabs], tn=2 * QK_PAD,
                         rider=gather_of([casted[W_O]], [W_O]))
    w_o = view(W_O, o_buf)[0]
    (kv,) = _mm("mla_kv", ckv, w_ukv, "nn", S, H * QK_PAD, rank, [_sds((S, H * QK_PAD), BF16)])
    rest_idx = [W_UP, W_DOWN]
    o, lse, (up_buf, down_buf) = _attn_fwd_tri(
        q, kv, kr, H, scale, gather_of([casted[i] for i in rest_idx], rest_idx, [0, 0]))
    (y1,), (cout_buf,) = _mm("mla_out", o, w_o, "nn", S, D, HV, [_sds((S, D), BF16)],
                             rider=gather_of([casted[W_COUT]], [W_COUT]))
    x1, h2 = _fwd_boundary("fwd_boundary_1", x0, y1, gt1, ng[0][1], ng[0][2], sc2, sh2)
    (r2, a2), (cin_buf,) = mlp_up("0", 0, h2, view(W_UP, up_buf), gather_of([casted[W_CIN]], [W_CIN]))
    (y2,), (up_buf,) = mlp_down("0", 0, a2, view(W_DOWN, down_buf), gather_of([up_buf], [W_UP], [1]))
    w_cin, w_cout, w_up = view(W_CIN, cin_buf)[0], view(W_COUT, cout_buf)[0], view(W_UP, up_buf)

    sh1b, sc1b, gt1b, sh2b, sc2b, gt2b = mods[1]
    x2, h3 = _fwd_boundary("fwd_boundary_2", x1, y2, gt2, ng[0][3], ng[1][0], sc1b, sh1b)
    nD = lambda tn: D // tn
    (proj3,), (down_buf,) = _mm(
        "conv_in", h3, w_cin, "nn", S, 3 * D, D, [_sds((3, S, D), BF16)], tn=min(1024, D),
        rider=gather_of([down_buf], [W_DOWN], [1], NEIGHBOURS),
        out_specs=[lambda tm, tn, tk: pl.BlockSpec((None, tm, tn), lambda i, j, k: (j // nD(tn), i, j % nD(tn)))])
    bz = _conv_fwd(proj3, convw_full)
    (y3,) = _mm("conv_out", bz, w_cout, "nn", S, D, D, [_sds((S, D), BF16)])
    x3, h4 = _fwd_boundary("fwd_boundary_3", x2, y3, gt1b, ng[1][1], ng[1][2], sc2b, sh2b)
    (r4, a4), (down_buf,) = mlp_up("1", 1, h4, w_up, gather_of([down_buf], [W_DOWN], [1], DIAGONAL))
    w_down = view(W_DOWN, down_buf)
    (y4,) = mlp_down("1", 1, a4, w_down)

    dx4, dy4, sums_l, loss_acc = _loss_boundary("loss_boundary", x3, y4, gt2b, ng[1][3], loss_target[0])

    dh4, dw_up1, dw_down1, _ = mlp_bwd("1", 1, h4, r4, a4, dy4)
    dx3, dy3, sums_3, _ = _bwd_boundary("bwd_boundary_3", dx4, dh4, x3, y3, gt1b, ng[1][1], ng[1][2], sc2b)

    items = [("mlp_w_up_1", W_UP, dw_up1)]
    g5s = halves(items)
    (dbz,), ras = _mm("conv_out_dx", dy3, w_cout, "nt", S, D, D, [_sds((S, D), BF16)], rider=_exchange_rider(g5s))
    (ps_up1,) = pair_sums(items, g5s, ras)
    items = [("mlp_w_down_1", W_DOWN, dw_down1)]
    g5s = halves(items)
    (dw_cout,), ras = _mm("conv_out_dw", bz, dy3, "tn", D, D, S, [_sds((D, D), BF16)], rider=_exchange_rider(g5s))
    (ps_down1,) = pair_sums(items, g5s, ras)
    dproj3, dconvw = _conv_bwd(dbz, proj3, convw_full)
    (dh3,), (rb_up1,) = _mm(
        "conv_in_dx", dproj3, w_cin, "nt", S, D, 3 * D, [_sds((S, D), BF16)], tk=D,
        rider=scatter_of([ps_up1], [W_UP], NEIGHBOURS),
        a_spec=lambda tm, tn, tk: pl.BlockSpec((None, tm, tk), lambda i, j, k: (k // (D // tk), i, k % (D // tk))))
    (dw_cin,), (rb_up1,) = _mm(
        "conv_in_dw", h3, dproj3, "tn", D, 3 * D, S, [_sds((D, 3 * D), BF16)], tn=min(1024, D),
        rider=scatter_of([ps_up1], [W_UP], DIAGONAL, [rb_up1]),
        b_spec=lambda tm, tn, tk: pl.BlockSpec((None, tk, tn), lambda i, j, k: (j // nD(tn), k, j % nD(tn))))
    dx2, dy2, sums_2, _ = _bwd_boundary("bwd_boundary_2", dx3, dh3, x2, y2, gt2, ng[0][3], ng[1][0], sc1b)

    items = [("conv_w_in", W_CIN, dw_cin), ("conv_w_out", W_COUT, dw_cout)]
    g5s = halves(items)
    conv_ps = []

    def plan(got):
        conv_ps.extend(pair_sums(items, g5s, got[1:]))
        return (scatter_of([ps_down1], [W_DOWN], DIAGONAL, [got[0]]),
                scatter_of(conv_ps[:1], [W_CIN]), scatter_of(conv_ps[1:], [W_COUT]))

    dh2, dw_up0, dw_down0, (_, (rb_down1,), (rb_cin,), (rb_cout,)) = mlp_bwd(
        "0", 0, h2, r2, a2, dy2,
        _both_riders(scatter_of([ps_down1], [W_DOWN], NEIGHBOURS), _exchange_rider(g5s)), plan)
    ps_cin, ps_cout = conv_ps
    dx1, dy1, sums_1, _ = _bwd_boundary("bwd_boundary_1", dx2, dh2, x1, y1, gt1, ng[0][1], ng[0][2], sc2)

    items = [("mlp_w_up_0", W_UP, dw_up0)]
    g5s = halves(items)
    (dw_o,), ras = _mm("mla_out_dw", o, dy1, "tn", HV, D, S, [_sds((HV, D), BF16)], rider=_exchange_rider(g5s))
    (ps_up0,) = pair_sums(items, g5s, ras)
    items = [(weights[W_O][0], W_O, dw_o), ("mlp_w_down_0", W_DOWN, dw_down0)]
    g5s = halves(items)
    (do,), ras = _mm("mla_out_dx", dy1, w_o, "nt", S, HV, D, [_sds((S, HV), BF16)], rider=_exchange_rider(g5s))
    ps_o, ps_down0 = pair_sums(items, g5s, ras)
    dq, dkv, dkr, (rb_up0, rb_down0, rb_o) = _attn_bwd_tri(
        q, kv, kr, o, do, lse, rope_tabs, H, scale,
        scatter_of([ps_up0, ps_down0, ps_o], [W_UP, W_DOWN, W_O]))
    kc_idx = jnp.stack([chip, ci]).astype(jnp.int32)
    f_o, f_cin, f_cout = [_chip_sum("chip_sum_" + weights[i][0], p, rb, kc_idx, kinds[i])
                          for i, p, rb in [(W_O, ps_o, rb_o), (W_CIN, ps_cin, rb_cin), (W_COUT, ps_cout, rb_cout)]]
    f_mlp = []
    for i, (p1, r1), (p0, r0) in [(W_UP, (ps_up1, rb_up1), (ps_up0, rb_up0)), (W_DOWN, (ps_down1, rb_down1), (ps_down0, rb_down0))]:
        f = _chip_sum("chip_sum_" + weights[i][0] + "_1", p1, r1, kc_idx, kinds[i], layer=1, n_layers=2)
        f_mlp.append(_chip_sum("chip_sum_" + weights[i][0] + "_0", p0, r0, kc_idx, kinds[i], layer=0, n_layers=2, prev=f))
    (dcq,), fin_up = _mm("mla_q_dx", dq, w_q_p, "nt", S, rank, H * QK_PAD, [_sds((S, rank), F32)],
                         rider=_share_rider([f_mlp[0]]))
    (dw_q_p,), fin_small = _mm("mla_q_dw", cq, dq, "tn", rank, H * QK_PAD, S, [_sds((rank, H * QK_PAD), BF16)],
                               rider=_share_rider([f_o, f_cin, f_cout]))
    (dckv,), fin_down = _mm("mla_kv_dx", dkv, w_ukv, "nt", S, rank, H * QK_PAD, [_sds((S, rank), F32)],
                            rider=_share_rider([f_mlp[1]]))
    finals_rest = list(fin_small) + list(fin_up) + list(fin_down)
    (dw_ukv,) = _mm("mla_kv_dw", ckv, dkv, "tn", rank, H * QK_PAD, S, [_sds((rank, H * QK_PAD), BF16)])
    dlat, sums_lat = _latent_bwd(lat, dcq, dckv, dkr, mla_g_q, mla_g_kv, rope_tabs, rank)
    late_idx = [W_IN, W_UQ, W_UKV]
    items_qkv = [(weights[W_UQ][0], W_UQ, dw_q_p.reshape(rank, H, QK_PAD)[:, :, :d_qk].reshape(rank, H * d_qk)),
                 (weights[W_UKV][0], W_UKV, dw_ukv)]
    g5s_qkv = halves(items_qkv)
    (dw_in_p,), ras = _mm("mla_in_dw", h1, dlat, "tn", D, lat_pad, S, [_sds((D, lat_pad), BF16)], tn=lat_pad,
                          rider=_exchange_rider(g5s_qkv))
    ps_qkv = pair_sums(items_qkv, g5s_qkv, ras)
    items_in = [(weights[W_IN][0], W_IN, dw_in_p[:, :lat_dim])]
    g5s_in = halves(items_in)
    (dh1,), carried = _mm("mla_in_dx", dlat, w_in_p, "nt", S, D, lat_pad, [_sds((S, D), BF16)],
                          rider=_both_riders(scatter_of(ps_qkv, [W_UQ, W_UKV]), _exchange_rider(g5s_in)))
    rbs_qkv, ras = carried[:2], carried[2:]
    ps_in = pair_sums(items_in, g5s_in, ras)
    ps_mla = ps_in + ps_qkv
    grad_x, sums_0, _ = _bwd_boundary("bwd_boundary_0", dx1, dh1, x0, None, None, None, ng[0][0], sc1)

    dmod0 = [sums_0[0], sums_0[1], sums_1[3], sums_1[0], sums_1[1], sums_2[3]]
    dmod1 = [sums_2[0], sums_2[1], sums_3[3], sums_3[0], sums_3[1], sums_l[3]]
    dng0 = [sums_0[2], sums_1[4], sums_1[2], sums_2[4]]
    dng1 = [sums_2[2], sums_3[4], sums_3[2], sums_l[4]]
    small = _pack_rows(dmod0 + dmod1 + dng0 + dng1 + [sums_lat[0], sums_lat[1], dconvw, loss_acc[0, 0:1]],
                       lane_mult=LANES)
    gathered, total, carried = _small_allgather(
        "gather_small_grads", small, with_sum=True, rider=scatter_of(ps_in, [W_IN]))
    rbs_mla = list(carried) + list(rbs_qkv)
    n_dm = 2 * n_mod * D
    dmod_all = gathered.reshape(8, -1)[:, :n_dm].reshape(8, 2, n_mod * D)
    total = total.reshape(-1)
    g_b_mod = total[:n_dm].reshape(2, n_mod * D)
    g_norm = lax.dynamic_slice(total[n_dm:n_dm + 8 * D].reshape(2, 4, D), (0, 0, chip * Dq), (2, 4, Dq))
    off = n_dm + 8 * D
    g_gq = total[off:off + rank].reshape(1, rank)
    g_gkv = total[off + rank:off + 2 * rank].reshape(1, rank)
    off += 2 * rank
    g_convw = lax.dynamic_slice(total[off:off + 3 * D].reshape(1, 3, D), (0, 0, chip * Dq), (1, 3, Dq))
    loss = total[off + 3 * D]

    dmod_cols = jnp.transpose(lax.dynamic_slice(dmod_all.reshape(8, 2, N_CHIPS, ncol), (0, 0, chip, 0), (8, 2, 1, ncol))
                              .reshape(8, 2, ncol), (1, 0, 2))
    g_w_mod, d_w_mod, nm_w_mod, nv_w_mod, _ = _adamw_mod(w_mod, cond_all.T, dmod_cols, m_w_mod, v_w_mod)
    fs_mla = [_chip_sum("chip_sum_" + weights[i][0], p, rb, kc_idx, kinds[i]) for i, p, rb in zip(late_idx, ps_mla, rbs_mla)]
    finals = list(_run_rider("grad_pair_share_mla", _share_rider(fs_mla))) + list(finals_rest)
    orig = [mla_w_in, mla_w_uq, mla_w_ukv, mla_w_o, conv_w_in, conv_w_out, mlp_w_up, mlp_w_down]
    big_grads = [f.reshape(w.shape) for f, w in zip(finals, orig)]

    names = ["b_mod", "norm_g", "mla_w_in", "mla_g_q", "mla_g_kv", "mla_w_uq", "mla_w_ukv", "mla_w_o",
             "conv_w_in", "conv_w", "conv_w_out", "mlp_w_up", "mlp_w_down"]
    ws = [b_mod, norm_g, mla_w_in, mla_g_q, mla_g_kv, mla_w_uq, mla_w_ukv, mla_w_o, conv_w_in, conv_w, conv_w_out,
          mlp_w_up, mlp_w_down]
    ms = [m_b_mod, m_norm_g, m_mla_w_in, m_mla_g_q, m_mla_g_kv, m_mla_w_uq, m_mla_w_ukv, m_mla_w_o, m_conv_w_in,
          m_conv_w, m_conv_w_out, m_mlp_w_up, m_mlp_w_down]
    vs = [v_b_mod, v_norm_g, v_mla_w_in, v_mla_g_q, v_mla_g_kv, v_mla_w_uq, v_mla_w_ukv, v_mla_w_o, v_conv_w_in,
          v_conv_w, v_conv_w_out, v_mlp_w_up, v_mlp_w_down]
    gs = [g_b_mod, g_norm, big_grads[0], g_gq, g_gkv, big_grads[1], big_grads[2], big_grads[3], big_grads[4],
          g_convw, big_grads[5], big_grads[6], big_grads[7]]
    grads, deltas, new_ms, new_vs = [g_w_mod], [d_w_mod], [nm_w_mod], [nv_w_mod]
    for nm, w, g, m, v in zip(names, ws, gs, ms, vs):
        big = any(g is b for b in big_grads)
        d, nm_, nv_, *g_out = _adamw("adamw_" + nm, w, g, m, v, emit_grad=big)
        g = g_out[0] if big else g
        grads.append(g)
        deltas.append(d)
        new_ms.append(nm_)
        new_vs.append(nv_)
    return (loss, grad_x[None], *grads, *deltas, *new_ms, *new_vs)
```

```python
from typing import NamedTuple

import jax
import jax.numpy as jnp
from jax import lax
from jax.experimental import pallas as pl
from jax.experimental.pallas import tpu as pltpu

F32 = jnp.float32
BF16 = jnp.bfloat16
NORM_EPS = 1e-6
ROPE_THETA = 10000.0
QK_NOPE = 128
QK_ROPE = 64
V_HEAD = 128
LANES = 128
QK_PAD = QK_NOPE + LANES
ADAM_LR, ADAM_B1, ADAM_B2, ADAM_EPS, ADAM_WD, ADAM_STEP = 0.001, 0.9, 0.999, 1e-08, 0.01, 10
VMEM_LIMIT_BYTES = 56 * 1024 * 1024
N_CHIPS = 4
MESH_ID = pl.DeviceIdType.MESH
ANY = pl.BlockSpec(memory_space=pl.ANY)
NEG_INF = float("-inf")

DIMS_NN = (((1,), (0,)), ((), ()))
DIMS_NT = (((1,), (1,)), ((), ()))
DIMS_TN = (((0,), (0,)), ((), ()))


def _cparams(*sem):
    return pltpu.CompilerParams(dimension_semantics=sem, vmem_limit_bytes=VMEM_LIMIT_BYTES)


def _row_tile(rows, row_bytes, limit=2 * 1024 * 1024, mult=16):
    if rows * row_bytes <= limit or rows % mult:
        return rows
    best = mult
    t = mult
    while t <= rows:
        if rows % t == 0 and t * row_bytes <= limit:
            best = t
        t += mult
    return best


def _rms(v):
    return lax.rsqrt(jnp.mean(v * v, axis=-1, keepdims=True) + NORM_EPS)


class Rider(NamedTuple):
    operands: tuple
    out_shape: tuple
    aliases: dict
    sems: tuple
    start: object
    finish: object
    mid: object = None


NO_RIDER = Rider((), (), {}, (), None, None)


def _rider_end(rider, r_in, r_out, r_sems):
    if rider.mid is not None:
        rider.mid(r_in, r_out, r_sems)
    rider.finish(r_in, r_out, r_sems)


def _mm(name, a, b, mode, M, N, K, outs, *, a_spec=None, b_spec=None, out_specs=None, epilogue=None,
        extras=(), rider=NO_RIDER, tm=1024, tn=1024, tk=4096):
    tm, tn, tk = min(tm, M), min(tn, N), min(tk, K)
    assert M % tm == 0 and N % tn == 0 and K % tk == 0, (name, M, N, K)
    nk = K // tk
    if a_spec is None:
        a_spec = {"nn": pl.BlockSpec((tm, tk), lambda i, j, k: (i, k)),
                  "nt": pl.BlockSpec((tm, tk), lambda i, j, k: (i, k)),
                  "tn": pl.BlockSpec((tk, tm), lambda i, j, k: (k, i))}[mode]
    else:
        a_spec = a_spec(tm, tn, tk)
    if b_spec is None:
        b_spec = {"nn": pl.BlockSpec((tk, tn), lambda i, j, k: (k, j)),
                  "nt": pl.BlockSpec((tn, tk), lambda i, j, k: (j, k)),
                  "tn": pl.BlockSpec((tk, tn), lambda i, j, k: (k, j))}[mode]
    else:
        b_spec = b_spec(tm, tn, tk)
    if out_specs is None:
        out_specs = [pl.BlockSpec((tm, tn), lambda i, j, k: (i, j)) for _ in outs]
    else:
        out_specs = [s(tm, tn, tk) for s in out_specs]
    dims = {"nn": DIMS_NN, "nt": DIMS_NT, "tn": DIMS_TN}[mode]
    ne, no = len(extras), len(outs)
    n_ri, n_ro = len(rider.operands), len(rider.out_shape)
    grid = (M // tm, N // tn, nk)

    def body(*refs):
        a_ref, b_ref = refs[0], refs[1]
        ex = refs[2:2 + ne]
        r_in = refs[2 + ne:2 + ne + n_ri]
        o = refs[2 + ne + n_ri:2 + ne + n_ri + no]
        r_out = refs[2 + ne + n_ri + no:2 + ne + n_ri + no + n_ro]
        scratch = refs[2 + ne + n_ri + no + n_ro:]
        r_sems = scratch[1:] if nk > 1 else scratch
        ii, jj, kk = pl.program_id(0), pl.program_id(1), pl.program_id(2)

        if rider.start is not None:
            @pl.when((ii == 0) & (jj == 0) & (kk == 0))
            def _():
                rider.start(r_in, r_out, r_sems)

        part = lax.dot_general(a_ref[...].astype(BF16), b_ref[...].astype(BF16), dims,
                               preferred_element_type=F32)

        def finish(total):
            vals = epilogue(total, *[e[...] for e in ex]) if epilogue is not None else (total,)
            for r, v in zip(o, vals):
                r[...] = v.astype(r.dtype)

        if nk == 1:
            finish(part)
        else:
            acc = scratch[0]

            @pl.when(kk == 0)
            def _():
                acc[...] = part

            @pl.when(kk > 0)
            def _():
                acc[...] += part

            @pl.when(kk == nk - 1)
            def _():
                finish(acc[...])

        if rider.finish is not None:
            steps = grid[0] * grid[1] * nk
            if rider.mid is not None and steps >= 4:
                @pl.when((ii * grid[1] + jj) * nk + kk == steps // 2)
                def _():
                    rider.mid(r_in, r_out, r_sems)

            @pl.when((ii == grid[0] - 1) & (jj == grid[1] - 1) & (kk == nk - 1))
            def _():
                if rider.mid is not None and steps < 4:
                    rider.mid(r_in, r_out, r_sems)
                rider.finish(r_in, r_out, r_sems)

    operands = [a, b] + [e[0] for e in extras] + list(rider.operands)
    in_specs = [a_spec, b_spec] + [e[1](tm, tn, tk) for e in extras] + [ANY] * n_ri
    hosted = rider.start is not None
    res = pl.pallas_call(
        body, name=name, grid=grid,
        in_specs=in_specs, out_specs=out_specs + [ANY] * n_ro, out_shape=list(outs) + list(rider.out_shape),
        scratch_shapes=([pltpu.VMEM((tm, tn), F32)] if nk > 1 else []) + list(rider.sems),
        input_output_aliases={2 + ne + i: no + r for i, r in rider.aliases.items()},
        compiler_params=_cparams(*(("arbitrary",) * 3 if hosted else ("parallel", "parallel", "arbitrary"))),
    )(*operands)
    return (res[:no], res[no:]) if hosted else res


def _sds(shape, dtype):
    return jax.ShapeDtypeStruct(tuple(shape), dtype)


def _rope(t, cos_p, sin_lo, sin_hi):
    return t * cos_p + pltpu.roll(t, LANES - QK_ROPE // 2, 1) * sin_lo + pltpu.roll(t, QK_ROPE // 2, 1) * sin_hi


def _rope_t(d, cos_p, sin_lo, sin_hi):
    return d * cos_p + pltpu.roll(d * sin_lo, QK_ROPE // 2, 1) + pltpu.roll(d * sin_hi, LANES - QK_ROPE // 2, 1)


def _vec_spec(d):
    return pl.BlockSpec((1, d), lambda i: (0, 0))


def _fwd_boundary(name, x_prev, y, gate, ng_post, ng_pre, sc, sh):
    S, D = x_prev.shape
    ts = min(256, S)
    has_y = y is not None
    row = pl.BlockSpec((ts, D), lambda i: (i, 0))

    def body(*refs):
        if has_y:
            x_ref, y_ref, g_ref, ngp_ref, ngn_ref, sc_ref, sh_ref, xo_ref, h_ref = refs
            yv = y_ref[...].astype(F32)
            xn = x_ref[...] + g_ref[...] * (yv * _rms(yv) * ngp_ref[...])
            xo_ref[...] = xn
        else:
            x_ref, ngn_ref, sc_ref, sh_ref, h_ref = refs
            xn = x_ref[...]
        hn = xn * _rms(xn) * ngn_ref[...]
        h_ref[...] = (hn * (1.0 + sc_ref[...]) + sh_ref[...]).astype(BF16)

    vec = _vec_spec(D)
    if has_y:
        operands = (x_prev, y, gate, ng_post, ng_pre, sc, sh)
        in_specs = [row, row, vec, vec, vec, vec, vec]
        out_shape = [_sds((S, D), F32), _sds((S, D), BF16)]
        out_specs = [row, row]
    else:
        operands = (x_prev, ng_pre, sc, sh)
        in_specs = [row, vec, vec, vec]
        out_shape = [_sds((S, D), BF16)]
        out_specs = [row]
    return pl.pallas_call(body, name=name, grid=(S // ts,), in_specs=in_specs, out_specs=out_specs,
                          out_shape=out_shape, compiler_params=_cparams("parallel"))(*operands)


def _acc_rows(sums_ref, rows):
    for r, v in rows:
        sums_ref[r:r + 1, :] += jnp.sum(v, axis=0, keepdims=True)


def _post_norm_bwd(dxt, yv, gate, ng_post, sums_ref, dy_ref):
    r1 = _rms(yv)
    yhat = yv * r1
    dn = dxt * gate
    u = dn * ng_post
    dy = r1 * (u - yhat * jnp.mean(u * yhat, axis=-1, keepdims=True))
    dy_ref[...] = dy.astype(dy_ref.dtype)
    _acc_rows(sums_ref, [(3, dxt * (yhat * ng_post)), (4, dn * yhat)])


def _loss_boundary(name, x_prev, y, gate, ng_post, target):
    S, D = x_prev.shape
    ts = min(256, S)
    row = pl.BlockSpec((ts, D), lambda i: (i, 0))
    vec = _vec_spec(D)

    def body(x_ref, y_ref, g_ref, ngp_ref, t_ref, dx_ref, dy_ref, sums_ref, loss_ref):
        @pl.when(pl.program_id(0) == 0)
        def _():
            sums_ref[...] = jnp.zeros_like(sums_ref)
            loss_ref[...] = jnp.zeros_like(loss_ref)

        yv = y_ref[...].astype(F32)
        xf = x_ref[...] + g_ref[...] * (yv * _rms(yv) * ngp_ref[...])
        err = xf - t_ref[...]
        loss_ref[...] += 0.5 * jnp.sum(jnp.mean(err * err, axis=-1, keepdims=True))
        dxt = err / D
        dx_ref[...] = dxt
        _post_norm_bwd(dxt, yv, g_ref[...], ngp_ref[...], sums_ref, dy_ref)

    return pl.pallas_call(
        body, name=name, grid=(S // ts,),
        in_specs=[row, row, vec, vec, row],
        out_specs=[row, row, pl.BlockSpec((8, D), lambda i: (0, 0)), pl.BlockSpec((8, LANES), lambda i: (0, 0))],
        out_shape=[_sds((S, D), F32), _sds((S, D), BF16), _sds((8, D), F32), _sds((8, LANES), F32)],
        compiler_params=_cparams("arbitrary"))(x_prev, y, gate, ng_post, target)


def _bwd_boundary(name, dx_new, dh, x_new, y, gate, ng_post, ng_pre, sc, rider=NO_RIDER):
    S, D = x_new.shape
    ts = min(256, S)
    has_y = y is not None
    row = pl.BlockSpec((ts, D), lambda i: (i, 0))
    vec = _vec_spec(D)
    n_in, n_out = (8, 3) if has_y else (5, 2)
    n_ri, n_ro = len(rider.operands), len(rider.out_shape)

    def body(*refs):
        r_in = refs[n_in:n_in + n_ri]
        r_out = refs[n_in + n_ri + n_out:n_in + n_ri + n_out + n_ro]
        r_sems = refs[n_in + n_ri + n_out + n_ro:]
        own = refs[:n_in] + refs[n_in + n_ri:n_in + n_ri + n_out]
        if has_y:
            dxn_ref, dh_ref, x_ref, y_ref, g_ref, ngp_ref, ngn_ref, sc_ref, dxo_ref, dy_ref, sums_ref = own
        else:
            dxn_ref, dh_ref, x_ref, ngn_ref, sc_ref, dxo_ref, sums_ref = own

        @pl.when(pl.program_id(0) == 0)
        def _():
            sums_ref[...] = jnp.zeros_like(sums_ref)
            if rider.start is not None:
                rider.start(r_in, r_out, r_sems)

        xv = x_ref[...]
        dhv = dh_ref[...].astype(F32)
        ngn = ngn_ref[...]
        r2 = _rms(xv)
        xhat = xv * r2
        dn_pre = dhv * (1.0 + sc_ref[...])
        u2 = dn_pre * ngn
        dxt = dxn_ref[...] + r2 * (u2 - xhat * jnp.mean(u2 * xhat, axis=-1, keepdims=True))
        dxo_ref[...] = dxt
        _acc_rows(sums_ref, [(0, dhv), (1, dhv * (xhat * ngn)), (2, dn_pre * xhat)])
        if has_y:
            _post_norm_bwd(dxt, y_ref[...].astype(F32), g_ref[...], ngp_ref[...], sums_ref, dy_ref)

        if rider.finish is not None:
            @pl.when(pl.program_id(0) == S // ts - 1)
            def _():
                _rider_end(rider, r_in, r_out, r_sems)

    sums_spec = pl.BlockSpec((8, D), lambda i: (0, 0))
    if has_y:
        operands = (dx_new, dh, x_new, y, gate, ng_post, ng_pre, sc)
        in_specs = [row, row, row, row, vec, vec, vec, vec]
        out_shape = [_sds((S, D), F32), _sds((S, D), BF16), _sds((8, D), F32)]
        out_specs = [row, row, sums_spec]
    else:
        operands = (dx_new, dh, x_new, ng_pre, sc)
        in_specs = [row, row, row, vec, vec]
        out_shape = [_sds((S, D), F32), _sds((8, D), F32)]
        out_specs = [row, sums_spec]
    res = pl.pallas_call(
        body, name=name, grid=(S // ts,), in_specs=in_specs + [ANY] * n_ri, out_specs=out_specs + [ANY] * n_ro,
        out_shape=out_shape + list(rider.out_shape), scratch_shapes=list(rider.sems),
        input_output_aliases={n_in + i: n_out + o for i, o in rider.aliases.items()},
        compiler_params=_cparams("arbitrary"))(*operands, *rider.operands)
    return (*res[:n_out], res[n_out:])


def _latent_fwd(lat, g_q, g_kv, rope_tabs, rank):
    S, W = lat.shape
    ts = min(256, S)
    tab = pl.BlockSpec((ts, LANES), lambda i: (i, 0))

    def body(lat_ref, gq_ref, gkv_ref, cos_ref, slo_ref, shi_ref, cq_ref, ckv_ref, kr_ref):
        lq = lat_ref[:, 0:rank]
        lkv = lat_ref[:, rank:2 * rank]
        cq_ref[...] = (lq * _rms(lq) * gq_ref[...]).astype(BF16)
        ckv_ref[...] = (lkv * _rms(lkv) * gkv_ref[...]).astype(BF16)
        kr_ref[...] = _rope(lat_ref[:, 2 * rank:W], cos_ref[...], slo_ref[...], shi_ref[...]).astype(BF16)

    return pl.pallas_call(
        body, name="mla_latent_fwd", grid=(S // ts,),
        in_specs=[pl.BlockSpec((ts, W), lambda i: (i, 0)), _vec_spec(rank), _vec_spec(rank), tab, tab, tab],
        out_specs=[pl.BlockSpec((ts, rank), lambda i: (i, 0)), pl.BlockSpec((ts, rank), lambda i: (i, 0)), tab],
        out_shape=[_sds((S, rank), BF16), _sds((S, rank), BF16), _sds((S, LANES), BF16)],
        compiler_params=_cparams("parallel"))(lat, g_q, g_kv, *rope_tabs)


def _latent_bwd(lat, dcq, dckv, dkr, g_q, g_kv, rope_tabs, rank):
    S, W = lat.shape
    ts = min(256, S)
    tab = pl.BlockSpec((ts, LANES), lambda i: (i, 0))
    half = pl.BlockSpec((ts, rank), lambda i: (i, 0))

    def body(lat_ref, dcq_ref, dckv_ref, dkr_ref, gq_ref, gkv_ref, cos_ref, slo_ref, shi_ref, dlat_ref, sums_ref):
        @pl.when(pl.program_id(0) == 0)
        def _():
            sums_ref[...] = jnp.zeros_like(sums_ref)

        def norm_bwd(v, dn, g, r):
            rr = _rms(v)
            vhat = v * rr
            u = dn * g
            sums_ref[r:r + 1, :] += jnp.sum(dn * vhat, axis=0, keepdims=True)
            return rr * (u - vhat * jnp.mean(u * vhat, axis=-1, keepdims=True))

        dlat_ref[:, 0:rank] = norm_bwd(lat_ref[:, 0:rank], dcq_ref[...], gq_ref[...], 0).astype(BF16)
        dlat_ref[:, rank:2 * rank] = norm_bwd(lat_ref[:, rank:2 * rank], dckv_ref[...], gkv_ref[...], 1).astype(BF16)
        dlat_ref[:, 2 * rank:W] = _rope_t(dkr_ref[...], cos_ref[...], slo_ref[...], shi_ref[...]).astype(BF16)

    return pl.pallas_call(
        body, name="mla_latent_bwd", grid=(S // ts,),
        in_specs=[pl.BlockSpec((ts, W), lambda i: (i, 0)), half, half, tab, _vec_spec(rank), _vec_spec(rank),
                  tab, tab, tab],
        out_specs=[pl.BlockSpec((ts, W), lambda i: (i, 0)), pl.BlockSpec((8, rank), lambda i: (0, 0))],
        out_shape=[_sds((S, W), BF16), _sds((8, rank), F32)],
        compiler_params=_cparams("arbitrary"))(lat, dcq, dckv, dkr, g_q, g_kv, *rope_tabs)


def _attn_tiles(S):
    t = min(512, S)
    return t, S // t


def _causal_mask(t):
    return lax.broadcasted_iota(jnp.int32, (t, t), 1) <= lax.broadcasted_iota(jnp.int32, (t, t), 0)


def _causal_pairs(nb, q_major):
    if q_major:
        pairs = [(qi, ki) for qi in range(nb) for ki in range(qi + 1)]
    else:
        pairs = [(qi, ki) for ki in range(nb) for qi in range(ki, nb)]
    return jnp.array([p[0] for p in pairs], jnp.int32), jnp.array([p[1] for p in pairs], jnp.int32), len(pairs)


def _heads_per_step(heads):
    return 2 if heads % 2 == 0 else 1


def _attn_fwd_tri(q, kv, kr, heads, scale, rider=NO_RIDER):
    S = q.shape[0]
    t, nb = _attn_tiles(S)
    G = _heads_per_step(heads)
    q_tab, k_tab, n_pairs = _causal_pairs(nb, True)
    n_ri, n_ro = len(rider.operands), len(rider.out_shape)

    def body(qt_ref, kt_ref, *refs):
        q_ref, kv_ref, kr_ref = refs[:3]
        r_in = refs[3:3 + n_ri]
        o_ref, lse_ref = refs[3 + n_ri:5 + n_ri]
        r_out = refs[5 + n_ri:5 + n_ri + n_ro]
        m_scr, acc_scr = refs[5 + n_ri + n_ro:7 + n_ri + n_ro]
        r_sems = refs[7 + n_ri + n_ro:]
        h, p = pl.program_id(0), pl.program_id(1)
        qi, ki = qt_ref[p], kt_ref[p]

        if rider.start is not None:
            @pl.when((h == 0) & (p == 0))
            def _():
                rider.start(r_in, r_out, r_sems)

        @pl.when(ki == 0)
        def _():
            m_scr[...] = jnp.full_like(m_scr, NEG_INF)
            acc_scr[...] = jnp.zeros_like(acc_scr)

        def step(diagonal):
            ones = jnp.ones((t, LANES), BF16)
            for g in range(G):
                kcat = jnp.concatenate([kv_ref[:, g * QK_PAD:g * QK_PAD + QK_NOPE], kr_ref[...]], axis=1)
                vext = jnp.concatenate([kv_ref[:, g * QK_PAD + QK_NOPE:(g + 1) * QK_PAD], ones], axis=1)
                s = lax.dot_general(q_ref[:, g * QK_PAD:(g + 1) * QK_PAD], kcat, DIMS_NT,
                                    preferred_element_type=F32) * scale
                if diagonal:
                    s = jnp.where(_causal_mask(t), s, NEG_INF)
                m_prev = m_scr[g]
                m_new = jnp.maximum(m_prev, jnp.max(s, axis=-1, keepdims=True))
                alpha = jnp.exp(m_prev - m_new)
                pr = jnp.exp(s - jnp.tile(m_new, (1, t // LANES)))
                acc_scr[g] = jnp.tile(alpha, (1, 2)) * acc_scr[g] + lax.dot_general(
                    pr.astype(BF16), vext, DIMS_NN, preferred_element_type=F32)
                m_scr[g] = m_new

        @pl.when(ki < qi)
        def _():
            step(False)

        @pl.when(ki == qi)
        def _():
            step(True)
            for g in range(G):
                acc = acc_scr[g]
                o_ref[:, g * V_HEAD:(g + 1) * V_HEAD] = (acc[:, 0:V_HEAD] / acc[:, V_HEAD:2 * V_HEAD]).astype(BF16)
                lse_ref[g] = m_scr[g] + jnp.log(acc[:, V_HEAD:2 * V_HEAD])

        if rider.finish is not None:
            halfway = rider.mid is not None and heads // G >= 2
            if halfway:
                @pl.when((h == heads // G // 2) & (p == 0))
                def _():
                    rider.mid(r_in, r_out, r_sems)

            @pl.when((h == heads // G - 1) & (p == n_pairs - 1))
            def _():
                if halfway:
                    rider.finish(r_in, r_out, r_sems)
                else:
                    _rider_end(rider, r_in, r_out, r_sems)

    res = pl.pallas_call(
        body, name="mla_attn_fwd",
        grid_spec=pltpu.PrefetchScalarGridSpec(
            num_scalar_prefetch=2, grid=(heads // G, n_pairs),
            in_specs=[pl.BlockSpec((t, G * QK_PAD), lambda h, p, qt, kt: (qt[p], h)),
                      pl.BlockSpec((t, G * QK_PAD), lambda h, p, qt, kt: (kt[p], h)),
                      pl.BlockSpec((t, LANES), lambda h, p, qt, kt: (kt[p], 0))] + [ANY] * n_ri,
            out_specs=[pl.BlockSpec((t, G * V_HEAD), lambda h, p, qt, kt: (qt[p], h)),
                       pl.BlockSpec((G, t, LANES), lambda h, p, qt, kt: (h, qt[p], 0))] + [ANY] * n_ro,
            scratch_shapes=[pltpu.VMEM((G, t, LANES), F32), pltpu.VMEM((G, t, 2 * V_HEAD), F32)] + list(rider.sems)),
        out_shape=[_sds((S, heads * V_HEAD), BF16), _sds((heads, S, LANES), F32)] + list(rider.out_shape),
        input_output_aliases={5 + i: 2 + o for i, o in rider.aliases.items()},
        compiler_params=_cparams("arbitrary", "arbitrary"))(q_tab, k_tab, q, kv, kr, *rider.operands)
    return res[0], res[1], res[2:]


def _attn_bwd_tri(q, kv, kr, o, do, lse, rope_tabs, heads, scale, rider=NO_RIDER):
    S = q.shape[0]
    t, nb = _attn_tiles(S)
    G = _heads_per_step(heads)
    q_tab, k_tab, n_pairs = _causal_pairs(nb, False)
    n_ri, n_ro = len(rider.operands), len(rider.out_shape)
    rep = t // LANES

    tabs = jnp.concatenate(rope_tabs, axis=1)

    def body(qt_ref, kt_ref, *refs):
        q_ref, kv_ref, kr_ref, o_ref, do_ref, lse_ref, tabs_ref = refs[:7]
        cos_ref, slo_ref, shi_ref = (tabs_ref.at[:, pl.ds(i * LANES, LANES)] for i in range(3))
        r_in = refs[7:7 + n_ri]
        dq_ref, dkv_ref, dkr_ref = refs[7 + n_ri:10 + n_ri]
        r_out = refs[10 + n_ri:10 + n_ri + n_ro]
        dq_scr, dk_scr, dv_scr, dkr_scr, delta_scr = refs[10 + n_ri + n_ro:15 + n_ri + n_ro]
        r_sems = refs[15 + n_ri + n_ro:]
        h, p = pl.program_id(0), pl.program_id(1)
        qi, ki = qt_ref[p], kt_ref[p]
        q_rows = pl.ds(pl.multiple_of(qi * t, t), t)
        k_rows = pl.ds(pl.multiple_of(ki * t, t), t)

        @pl.when(ki == 0)
        def _():
            for g in range(G):
                cols = slice(g * V_HEAD, (g + 1) * V_HEAD)
                d = jnp.sum(do_ref[:, cols].astype(F32) * o_ref[:, cols].astype(F32), axis=-1, keepdims=True)
                delta_scr[g, q_rows, :] = jnp.broadcast_to(d, (t, LANES))

        if rider.start is not None:
            @pl.when((h == 0) & (p == 0))
            def _():
                rider.start(r_in, r_out, r_sems)

        @pl.when(p == 0)
        def _():
            dq_scr[...] = jnp.zeros_like(dq_scr)

        @pl.when((h == 0) & (p == 0))
        def _():
            dkr_scr[...] = jnp.zeros_like(dkr_scr)

        @pl.when(qi == ki)
        def _():
            dk_scr[...] = jnp.zeros_like(dk_scr)
            dv_scr[...] = jnp.zeros_like(dv_scr)

        def step(diagonal):
            for g in range(G):
                qv = q_ref[:, g * QK_PAD:(g + 1) * QK_PAD]
                kcat = jnp.concatenate([kv_ref[:, g * QK_PAD:g * QK_PAD + QK_NOPE], kr_ref[...]], axis=1)
                s = lax.dot_general(qv, kcat, DIMS_NT, preferred_element_type=F32) * scale
                pr = jnp.exp(s - jnp.tile(lse_ref[g], (1, rep)))
                if diagonal:
                    pr = jnp.where(_causal_mask(t), pr, 0.0)
                dov = do_ref[:, g * V_HEAD:(g + 1) * V_HEAD]
                dv_scr[g] += lax.dot_general(pr.astype(BF16), dov, DIMS_TN, preferred_element_type=F32)
                dp = lax.dot_general(dov, kv_ref[:, g * QK_PAD + QK_NOPE:(g + 1) * QK_PAD], DIMS_NT,
                                     preferred_element_type=F32)
                ds = (pr * (dp - jnp.tile(delta_scr[g, q_rows, :], (1, rep))) * scale).astype(BF16)
                dk_scr[g] += lax.dot_general(ds, qv, DIMS_TN, preferred_element_type=F32)
                dq_scr[q_rows, g * QK_PAD:(g + 1) * QK_PAD] += lax.dot_general(ds, kcat, DIMS_NN,
                                                                               preferred_element_type=F32)

        @pl.when(qi > ki)
        def _():
            step(False)

        @pl.when(qi == ki)
        def _():
            step(True)
            for g in range(G):
                dqv = dq_scr[q_rows, g * QK_PAD:(g + 1) * QK_PAD]
                dq_ref[q_rows, g * QK_PAD:(g + 1) * QK_PAD] = jnp.concatenate(
                    [dqv[:, 0:QK_NOPE], _rope_t(dqv[:, QK_NOPE:QK_PAD], cos_ref[...], slo_ref[...], shi_ref[...])],
                    axis=1).astype(BF16)

        @pl.when(qi == nb - 1)
        def _():
            for g in range(G):
                dkv_ref[:, g * QK_PAD:(g + 1) * QK_PAD] = jnp.concatenate(
                    [dk_scr[g][:, 0:QK_NOPE], dv_scr[g]], axis=1).astype(BF16)
                dkr_scr[k_rows, :] += dk_scr[g][:, QK_NOPE:QK_PAD]

        @pl.when((h == heads // G - 1) & (p == n_pairs - 1))
        def _():
            dkr_ref[...] = dkr_scr[...]
            if rider.finish is not None:
                _rider_end(rider, r_in, r_out, r_sems)

    q_blk = lambda w: pl.BlockSpec((t, G * w), lambda h, p, qt, kt: (qt[p], h))
    stat = pl.BlockSpec((G, t, LANES), lambda h, p, qt, kt: (h, qt[p], 0))
    tab = pl.BlockSpec((t, LANES), lambda h, p, qt, kt: (kt[p], 0))
    res = pl.pallas_call(
        body, name="mla_attn_bwd",
        grid_spec=pltpu.PrefetchScalarGridSpec(
            num_scalar_prefetch=2, grid=(heads // G, n_pairs),
            in_specs=[q_blk(QK_PAD),
                      pl.BlockSpec((t, G * QK_PAD), lambda h, p, qt, kt: (kt[p], h)),
                      tab, q_blk(V_HEAD), q_blk(V_HEAD), stat,
                      pl.BlockSpec((t, 3 * LANES), lambda h, p, qt, kt: (kt[p], 0))] + [ANY] * n_ri,
            out_specs=[pl.BlockSpec((S, G * QK_PAD), lambda h, p, qt, kt: (0, h)),
                       pl.BlockSpec((t, G * QK_PAD), lambda h, p, qt, kt: (kt[p], h)),
                       pl.BlockSpec((S, LANES), lambda h, p, qt, kt: (0, 0))] + [ANY] * n_ro,
            scratch_shapes=[pltpu.VMEM((S, G * QK_PAD), F32), pltpu.VMEM((G, t, QK_PAD), F32),
                            pltpu.VMEM((G, t, V_HEAD), F32), pltpu.VMEM((S, LANES), F32),
                            pltpu.VMEM((G, S, LANES), F32)] + list(rider.sems)),
        out_shape=[_sds((S, heads * QK_PAD), BF16), _sds((S, heads * QK_PAD), BF16), _sds((S, LANES), F32)]
        + list(rider.out_shape),
        input_output_aliases={9 + i: 3 + o for i, o in rider.aliases.items()},
        compiler_params=_cparams("arbitrary", "arbitrary"))(q_tab, k_tab, q, kv, kr, o, do, lse, tabs,
                                                            *rider.operands)
    return res[0], res[1], res[2], res[3:]


def _shift_down(z, n, rows):
    return jnp.where(rows >= n, pltpu.roll(z, n, 0), 0.0)


def _shift_up(z, n, rows, S):
    return jnp.where(rows < S - n, pltpu.roll(z, S - n, 0), 0.0)


def _conv_specs(S, tc):
    strip = lambda p: pl.BlockSpec((None, S, tc), lambda j: (p, 0, j))
    return strip(0), strip(1), strip(2), pl.BlockSpec((3, tc), lambda j: (0, j))


def _conv_fwd(proj3, w):
    _, S, D = proj3.shape
    tc = LANES

    def body(b_ref, c_ref, u_ref, w_ref, out_ref):
        z = c_ref[...].astype(F32) * u_ref[...].astype(F32)
        rows = lax.broadcasted_iota(jnp.int32, (S, tc), 0)
        zc = w_ref[0:1, :] * _shift_down(z, 2, rows) + w_ref[1:2, :] * _shift_down(z, 1, rows) + w_ref[2:3, :] * z
        out_ref[...] = (b_ref[...].astype(F32) * zc).astype(BF16)

    return pl.pallas_call(
        body, name="conv_fwd", grid=(D // tc,), in_specs=list(_conv_specs(S, tc)),
        out_specs=pl.BlockSpec((S, tc), lambda j: (0, j)), out_shape=_sds((S, D), BF16),
        compiler_params=_cparams("parallel"))(proj3, proj3, proj3, w)


def _conv_bwd(dbz, proj3, w):
    _, S, D = proj3.shape
    tc = LANES

    def body(d_ref, b_ref, c_ref, u_ref, w_ref, dp_ref, dw_ref):
        cv, uv, dv = c_ref[...].astype(F32), u_ref[...].astype(F32), d_ref[...].astype(F32)
        z = cv * uv
        rows = lax.broadcasted_iota(jnp.int32, (S, tc), 0)
        z1, z2 = _shift_down(z, 1, rows), _shift_down(z, 2, rows)
        zc = w_ref[0:1, :] * z2 + w_ref[1:2, :] * z1 + w_ref[2:3, :] * z
        dp_ref[0] = (dv * zc).astype(BF16)
        dzc = dv * b_ref[...].astype(F32)
        dz = w_ref[2:3, :] * dzc + w_ref[1:2, :] * _shift_up(dzc, 1, rows, S) + w_ref[0:1, :] * _shift_up(dzc, 2, rows, S)
        dp_ref[1] = (dz * uv).astype(BF16)
        dp_ref[2] = (dz * cv).astype(BF16)
        dw_ref[0:1, :] = jnp.sum(dzc * z2, axis=0, keepdims=True)
        dw_ref[1:2, :] = jnp.sum(dzc * z1, axis=0, keepdims=True)
        dw_ref[2:3, :] = jnp.sum(dzc * z, axis=0, keepdims=True)

    sb, sc_, su, sw = _conv_specs(S, tc)
    return pl.pallas_call(
        body, name="conv_bwd", grid=(D // tc,),
        in_specs=[pl.BlockSpec((S, tc), lambda j: (0, j)), sb, sc_, su, sw],
        out_specs=[pl.BlockSpec((3, S, tc), lambda j: (0, 0, j)), pl.BlockSpec((3, tc), lambda j: (0, j))],
        out_shape=[_sds((3, S, D), BF16), _sds((3, D), F32)],
        compiler_params=_cparams("parallel"))(dbz, proj3, proj3, proj3, w)


def _silu(c_all):
    def body(c_ref, o_ref):
        cv = c_ref[...]
        o_ref[...] = cv * (1.0 / (1.0 + jnp.exp(-cv)))

    vm = pl.BlockSpec(memory_space=pltpu.VMEM)
    return pl.pallas_call(body, name="cond_silu", in_specs=[vm], out_specs=vm, out_shape=_sds(c_all.shape, F32))(c_all)


def _mod_fwd(cond, w_mod, b_cols):
    L, D, ncol = w_mod.shape
    B = cond.shape[0]
    tk, tn = min(512, D), min(1024, ncol)
    nk = D // tk

    def body(c_ref, w_ref, b_ref, out_ref, acc):
        kk = pl.program_id(2)
        part = lax.dot_general(c_ref[...].astype(BF16), w_ref[...].astype(BF16), DIMS_NN, preferred_element_type=F32)

        @pl.when(kk == 0)
        def _():
            acc[...] = part

        @pl.when(kk > 0)
        def _():
            acc[...] += part

        @pl.when(kk == nk - 1)
        def _():
            out_ref[...] = acc[...] + b_ref[...]

    return pl.pallas_call(
        body, name="mod_fwd", grid=(L, ncol // tn, nk),
        in_specs=[pl.BlockSpec((B, tk), lambda l, j, k: (0, k)),
                  pl.BlockSpec((None, tk, tn), lambda l, j, k: (l, k, j)),
                  pl.BlockSpec((None, 1, tn), lambda l, j, k: (l, 0, j))],
        out_specs=pl.BlockSpec((None, B, tn), lambda l, j, k: (l, 0, j)),
        out_shape=_sds((L, B, ncol), F32),
        scratch_shapes=[pltpu.VMEM((B, tn), F32)],
        compiler_params=_cparams("parallel", "parallel", "arbitrary"))(cond, w_mod, b_cols)


def _adamw_math(w, g, m, v):
    m = ADAM_B1 * m + (1.0 - ADAM_B1) * g
    v = ADAM_B2 * v + (1.0 - ADAM_B2) * (g * g)
    m_hat = m / (1.0 - ADAM_B1 ** ADAM_STEP)
    v_hat = v / (1.0 - ADAM_B2 ** ADAM_STEP)
    delta = -ADAM_LR * (m_hat / (jnp.sqrt(v_hat) + ADAM_EPS) + ADAM_WD * w)
    return delta, m, v


def _adamw(name, w, g, m, v, emit_grad=False):
    shape = w.shape
    cols = shape[-1] if w.ndim <= 3 else shape[-2] * shape[-1]
    rows = w.size // cols
    w2, g2, m2, v2 = (t.reshape(rows, cols) for t in (w, g, m, v))
    tr = _row_tile(rows, cols * 4, limit=2 * 1024 * 1024, mult=8)
    spec = pl.BlockSpec((tr, cols), lambda i: (i, 0))
    n_out = 4 if emit_grad else 3

    def body(w_ref, g_ref, m_ref, v_ref, d_ref, nm_ref, nv_ref, *rest):
        gv = g_ref[...]
        d, nm, nv = _adamw_math(w_ref[...], gv, m_ref[...], v_ref[...])
        d_ref[...] = d
        nm_ref[...] = nm
        nv_ref[...] = nv
        if emit_grad:
            rest[0][...] = gv

    outs = pl.pallas_call(body, name=name, grid=(rows // tr,), in_specs=[spec] * 4, out_specs=[spec] * n_out,
                          out_shape=[_sds((rows, cols), F32)] * n_out, compiler_params=_cparams("parallel"))(w2, g2, m2, v2)
    return tuple(t.reshape(shape) for t in outs)


def _adamw_mod(w, cond_t, dmod_cols, m, v, rider=NO_RIDER):
    L, D, ncol = w.shape
    B = cond_t.shape[1]
    tr, tc = min(256, D), (ncol // 2 if ncol % 256 == 0 and ncol > 1024 else min(1024, ncol))
    blk = pl.BlockSpec((None, tr, tc), lambda l, i, j: (l, i, j))
    grid = (L, D // tr, ncol // tc)
    n_ri, n_ro = len(rider.operands), len(rider.out_shape)

    def body(*refs):
        w_ref, ct_ref, dm_ref, m_ref, v_ref = refs[:5]
        r_in = refs[5:5 + n_ri]
        g_ref, d_ref, nm_ref, nv_ref = refs[5 + n_ri:9 + n_ri]
        r_out = refs[9 + n_ri:9 + n_ri + n_ro]
        r_sems = refs[9 + n_ri + n_ro:]
        ids = [pl.program_id(a) for a in range(3)]

        if rider.start is not None:
            @pl.when((ids[0] == 0) & (ids[1] == 0) & (ids[2] == 0))
            def _():
                rider.start(r_in, r_out, r_sems)

        g = lax.dot_general(ct_ref[...], dm_ref[...], DIMS_NN, precision=lax.Precision.HIGHEST,
                            preferred_element_type=F32)
        d, nm, nv = _adamw_math(w_ref[...], g, m_ref[...], v_ref[...])
        g_ref[...] = g
        d_ref[...] = d
        nm_ref[...] = nm
        nv_ref[...] = nv

        if rider.finish is not None:
            @pl.when((ids[0] == grid[0] - 1) & (ids[1] == grid[1] - 1) & (ids[2] == grid[2] - 1))
            def _():
                _rider_end(rider, r_in, r_out, r_sems)

    hosted = rider.start is not None
    res = pl.pallas_call(
        body, name="adamw_w_mod", grid=grid,
        in_specs=[blk, pl.BlockSpec((tr, B), lambda l, i, j: (i, 0)),
                  pl.BlockSpec((None, B, tc), lambda l, i, j: (l, 0, j)), blk, blk] + [ANY] * n_ri,
        out_specs=[blk] * 4 + [ANY] * n_ro, out_shape=[_sds((L, D, ncol), F32)] * 4 + list(rider.out_shape),
        scratch_shapes=list(rider.sems), input_output_aliases={5 + i: 4 + o for i, o in rider.aliases.items()},
        compiler_params=_cparams(*(("arbitrary",) * 3 if hosted else ("parallel",) * 3)))(
            w, cond_t, dmod_cols, m, v, *rider.operands)
    return (*res[:4], res[4:])


def _cast_into_full(name, ws, kinds, k_idx, rider=NO_RIDER):
    L, R, C = ws[0].shape
    assert all(w.shape == (L, R, C) for w in ws)
    n = len(ws)
    Rh = R // 2
    tr = _row_tile(Rh, C * 4)
    grid = (L, 2, Rh // tr)
    out_shape, out_specs = [], []
    for kind in kinds:
        if kind == "row":
            out_shape.append(_sds((L, N_CHIPS, 2, Rh, C), BF16))
            out_specs.append(pl.BlockSpec((None, None, None, tr, C), lambda l, h, i, k_ref: (l, k_ref[0], h, i, 0)))
        else:
            out_shape.append(_sds((L, 2, Rh, N_CHIPS * C), BF16))
            out_specs.append(pl.BlockSpec((None, None, tr, C), lambda l, h, i, k_ref: (l, h, i, k_ref[0])))
    n_ri, n_ro = len(rider.operands), len(rider.out_shape)

    def body(k_ref, *refs):
        r_in = refs[n:n + n_ri]
        r_out = refs[2 * n + n_ri:2 * n + n_ri + n_ro]
        r_sems = refs[2 * n + n_ri + n_ro:]
        ids = [pl.program_id(a) for a in range(3)]
        if rider.start is not None:
            @pl.when((ids[0] == 0) & (ids[1] == 0) & (ids[2] == 0))
            def _():
                rider.start(r_in, r_out, r_sems)
        for a in range(n):
            refs[n + n_ri + a][...] = refs[a][...].astype(BF16)
        if rider.finish is not None:
            @pl.when((ids[0] == grid[0] - 1) & (ids[1] == grid[1] - 1) & (ids[2] == grid[2] - 1))
            def _():
                _rider_end(rider, r_in, r_out, r_sems)

    hosted = rider.start is not None
    res = pl.pallas_call(
        body, name=name,
        grid_spec=pltpu.PrefetchScalarGridSpec(
            num_scalar_prefetch=1, grid=grid,
            in_specs=[pl.BlockSpec((None, None, tr, C), lambda l, h, i, k_ref: (l, h, i, 0))] * n + [ANY] * n_ri,
            out_specs=out_specs + [ANY] * n_ro, scratch_shapes=list(rider.sems)),
        out_shape=out_shape + list(rider.out_shape),
        input_output_aliases={1 + n + i: n + o for i, o in rider.aliases.items()},
        compiler_params=_cparams(*(("arbitrary",) * 3 if hosted else ("parallel",) * 3)))(
            k_idx, *[w.reshape(L, 2, Rh, C) for w in ws], *rider.operands)
    return res[:n], res[n:]


def _pair_sum(name, g5, ra, c_idx):
    L, A, _, Rh, Cc = g5.shape
    tr = _row_tile(Rh, Cc * 4)

    def body(c_ref, g_ref, r_ref, o_ref):
        o_ref[...] = (g_ref[...].astype(F32) + r_ref[...].astype(F32)).astype(BF16)

    blk = pl.BlockSpec((None, None, tr, Cc), lambda l, a, i, c_ref: (l, a, i, 0))
    return pl.pallas_call(
        body, name=name,
        grid_spec=pltpu.PrefetchScalarGridSpec(
            num_scalar_prefetch=1, grid=(L, A, Rh // tr),
            in_specs=[pl.BlockSpec((None, None, None, tr, Cc), lambda l, a, i, c_ref: (l, a, c_ref[0], i, 0)), blk],
            out_specs=blk),
        out_shape=_sds((L, A, Rh, Cc), BF16),
        compiler_params=_cparams("parallel", "parallel", "parallel"))(c_idx, g5, ra)


def _chip_sum(name, p, rb, kc_idx, kind, layer=0, n_layers=1, prev=None):
    _, A, Rh, Cc = p.shape
    C = rb.shape[-1]
    tr = _row_tile(Rh, C * 4)
    if kind == "row":
        own = pl.BlockSpec((None, None, tr, C), lambda i, kc: (0, kc[0], i, 0))
    else:
        own = pl.BlockSpec((None, None, tr, C), lambda i, kc: (0, 0, i, kc[0]))
    peer = lambda j: pl.BlockSpec((None, None, tr, C), lambda i, kc: (j, 0, i, 0))

    def body(kc_ref, p_ref, r0_ref, r1_ref, r2_ref, *rest):
        o_ref = rest[-1]
        o_ref[...] = ((p_ref[...].astype(F32) + r0_ref[...].astype(F32)) + r1_ref[...].astype(F32)) + r2_ref[...].astype(F32)

    operands = [kc_idx, p, rb, rb, rb] + ([prev] if prev is not None else [])
    return pl.pallas_call(
        body, name=name,
        grid_spec=pltpu.PrefetchScalarGridSpec(
            num_scalar_prefetch=1, grid=(Rh // tr,),
            in_specs=[own, peer(0), peer(1), peer(2)] + ([ANY] if prev is not None else []),
            out_specs=pl.BlockSpec((None, None, tr, C), lambda i, kc: (layer, kc[1], i, 0))),
        out_shape=_sds((n_layers, 2, Rh, C), F32),
        input_output_aliases={5: 0} if prev is not None else {},
        compiler_params=_cparams("parallel"))(*operands)


def _mesh_place():
    x, y, c = lax.axis_index("x"), lax.axis_index("y"), lax.axis_index("c")
    chips = [(1 - x, y), (x, 1 - y), (1 - x, 1 - y)]
    return x, y, c, chips


def _remote(src, dst, send_sem, recv_sem, to):
    return pltpu.make_async_remote_copy(src_ref=src, dst_ref=dst, send_sem=send_sem, recv_sem=recv_sem,
                                        device_id=to, device_id_type=MESH_ID)


def _small_allgather(name, v, with_sum=False, rider=NO_RIDER):
    R, N = v.shape
    n_ri, n_ro, n_own = len(rider.operands), len(rider.out_shape), 2 if with_sum else 1

    def body(*refs):
        r_in = refs[1:1 + n_ri]
        r_out = refs[1 + n_ri + n_own:1 + n_ri + n_own + n_ro]
        r_sems = refs[1 + n_ri + n_own + n_ro + 3:]
        own = (refs[0],) + refs[1 + n_ri:1 + n_ri + n_own] + refs[1 + n_ri + n_own + n_ro:1 + n_ri + n_own + n_ro + 3]
        if with_sum:
            x_ref, out_ref, sum_ref, send_sems, recv_sems, local_sem = own
        else:
            x_ref, out_ref, send_sems, recv_sems, local_sem = own
        if rider.start is not None:
            rider.start(r_in, r_out, r_sems)
        x, y, c, chips = _mesh_place()
        me, sibling = (x, y, c), (x, y, 1 - c)

        def rows(px, py, pc):
            return out_ref.at[pl.ds((4 * px + 2 * py + pc) * R, R), :]

        def copy(k, block, to, src=None):
            return _remote(rows(*block) if src is None else src, rows(*block), send_sems.at[k], recv_sems.at[k], to)

        mine = pltpu.make_async_copy(x_ref, rows(*me), local_sem)
        mine.start()
        first = [copy(0, me, sibling, src=x_ref)]
        first += [copy(1 + j, me, (*chip, c), src=x_ref) for j, chip in enumerate(chips)]
        for cp in first:
            cp.start()
        passed = [copy(4 + j, (*chip, c), sibling) for j, chip in enumerate(chips)]
        for j, chip in enumerate(chips):
            copy(1 + j, (*chip, c), me).wait_recv()
            passed[j].start()
        copy(0, sibling, me).wait_recv()
        for j, chip in enumerate(chips):
            copy(4 + j, (*chip, 1 - c), me).wait_recv()
        for cp in first + passed:
            cp.wait_send()
        mine.wait()
        if with_sum:
            total = out_ref[0:R, :]
            for p in range(1, 8):
                total = total + out_ref[p * R:(p + 1) * R, :]
            sum_ref[...] = total
        if rider.finish is not None:
            _rider_end(rider, r_in, r_out, r_sems)

    vm = pl.BlockSpec(memory_space=pltpu.VMEM)
    out_shape = [_sds((8 * R, N), F32)] + ([_sds((R, N), F32)] if with_sum else [])
    res = pl.pallas_call(
        body, name=name, out_shape=out_shape + list(rider.out_shape), in_specs=[vm] + [ANY] * n_ri,
        out_specs=[vm] * n_own + [ANY] * n_ro,
        scratch_shapes=[pltpu.SemaphoreType.DMA((7,)), pltpu.SemaphoreType.DMA((7,)), pltpu.SemaphoreType.DMA]
        + list(rider.sems),
        input_output_aliases={1 + i: n_own + o for i, o in rider.aliases.items()},
        compiler_params=pltpu.CompilerParams(vmem_limit_bytes=VMEM_LIMIT_BYTES))(v, *rider.operands)
    if rider.start is not None:
        return (*res[:n_own], res[n_own:])
    return res if with_sum else res[0]


def _full_place(ref, kind, C, kk, half, layer=None):
    lead = slice(None) if layer is None else pl.ds(layer, 1)
    if kind == "row":
        return ref.at[lead, kk, half]
    return ref.at[lead, half, :, pl.ds(pl.multiple_of(kk * C, LANES), C)]


def _gather_rider(fulls, kinds, shard_cols, layers=None, peers=(0, 1, 2)):
    n = len(fulls)
    layers = layers or [None] * n
    rows = [f.shape[3] if kind == "row" else f.shape[2] for f, kind in zip(fulls, kinds)]
    n_chunks = 2 if all(r % 32 == 0 for r in rows) else 1

    def copies(outs, sems):
        x, y, c, chips = _mesh_place()
        k = 2 * x + y

        def place(a, kk, half, ch):
            rc = rows[a] // n_chunks
            return _full_place(outs[a], kinds[a], shard_cols[a], kk, half, layers[a]).at[:, pl.ds(ch * rc, rc), :]

        def copy(a, j, ch, ref, to):
            s = 6 * (n_chunks * a + ch) + j
            return _remote(ref, ref, sems[0].at[s], sems[1].at[s], to)

        return (x, y, c), [(j, chip) for j, chip in enumerate(chips) if j in peers], k, place, copy

    def start(_, outs, sems):
        (x, y, c), chips, k, place, copy = copies(outs, sems)
        for ch in range(n_chunks):
            for j, chip in chips:
                for a in range(n):
                    copy(a, j, ch, place(a, k, c, ch), (*chip, c)).start()

    def pass_on(outs, sems, ch):
        (x, y, c), chips, k, place, copy = copies(outs, sems)
        for j, chip in chips:
            kj = 2 * chip[0] + chip[1]
            for a in range(n):
                copy(a, j, ch, place(a, kj, c, ch), (x, y, c)).wait_recv()
                copy(a, 3 + j, ch, place(a, kj, c, ch), (x, y, 1 - c)).start()

    def mid(_, outs, sems):
        pass_on(outs, sems, 0)

    def finish(_, outs, sems):
        pass_on(outs, sems, n_chunks - 1)
        (x, y, c), chips, k, place, copy = copies(outs, sems)
        for ch in range(n_chunks):
            for j, chip in chips:
                kj = 2 * chip[0] + chip[1]
                for a in range(n):
                    copy(a, 3 + j, ch, place(a, kj, 1 - c, ch), (x, y, c)).wait_recv()
        for ch in range(n_chunks):
            for j, chip in chips:
                kj = 2 * chip[0] + chip[1]
                for a in range(n):
                    copy(a, j, ch, place(a, k, c, ch), (*chip, c)).wait_send()
                    copy(a, 3 + j, ch, place(a, kj, c, ch), (x, y, 1 - c)).wait_send()

    n_sems = 6 * n * n_chunks
    return Rider(tuple(fulls), tuple(_sds(f.shape, BF16) for f in fulls), {a: a for a in range(n)},
                 (pltpu.SemaphoreType.DMA((n_sems,)), pltpu.SemaphoreType.DMA((n_sems,))), start, finish,
                 mid if n_chunks == 2 else None)


def _scatter_rider(ps, kinds, shard_cols, peers=(0, 1, 2), into=None):
    n = len(ps)

    def copies(ins, outs, sems):
        x, y, c, chips = _mesh_place()
        cps = []
        for j, chip in enumerate(chips):
            if j not in peers:
                continue
            kj = 2 * chip[0] + chip[1]
            for a in range(n):
                C = shard_cols[a]
                src = ins[a].at[:, kj] if kinds[a] == "row" else ins[a].at[:, 0, :, pl.ds(pl.multiple_of(kj * C, LANES), C)]
                cps.append(_remote(src, outs[a].at[j], sems[0].at[3 * a + j], sems[1].at[3 * a + j], (*chip, c)))
        return cps

    def start(ins, outs, sems):
        for cp in copies(ins, outs, sems):
            cp.start()

    def finish(ins, outs, sems):
        cps = copies(ins, outs, sems)
        for cp in cps:
            cp.wait_recv()
        for cp in cps:
            cp.wait_send()

    out_shape = tuple(_sds((3, p.shape[0], p.shape[2], C), BF16) for p, C in zip(ps, shard_cols))
    aliases = {n + a: a for a in range(n)} if into is not None else {}
    return Rider(tuple(ps) + tuple(into or ()), out_shape, aliases,
                 (pltpu.SemaphoreType.DMA((3 * n,)), pltpu.SemaphoreType.DMA((3 * n,))), start, finish)


def _run_rider(name, rider):
    n_in, n_out = len(rider.operands), len(rider.out_shape)

    def body(*refs):
        ins, outs, sems = refs[:n_in], refs[n_in:n_in + n_out], refs[n_in + n_out:]
        rider.start(ins, outs, sems)
        _rider_end(rider, ins, outs, sems)

    return pl.pallas_call(
        body, name=name, out_shape=list(rider.out_shape), in_specs=[ANY] * n_in, out_specs=[ANY] * n_out,
        input_output_aliases=dict(rider.aliases), scratch_shapes=list(rider.sems),
        compiler_params=pltpu.CompilerParams(vmem_limit_bytes=VMEM_LIMIT_BYTES))(*rider.operands)


def _exchange_rider(g5s):
    n = len(g5s)

    def copies(ins, outs, sems):
        x, y, c, _ = _mesh_place()
        return [_remote(ins[a].at[:, :, 1 - c], outs[a], sems[0].at[a], sems[1].at[a], (x, y, 1 - c)) for a in range(n)]

    def start(ins, outs, sems):
        for cp in copies(ins, outs, sems):
            cp.start()

    def finish(ins, outs, sems):
        cps = copies(ins, outs, sems)
        for cp in cps:
            cp.wait_recv()
        for cp in cps:
            cp.wait_send()

    out_shape = tuple(_sds((g.shape[0], g.shape[1], g.shape[3], g.shape[4]), BF16) for g in g5s)
    return Rider(tuple(g5s), out_shape, {}, (pltpu.SemaphoreType.DMA((n,)), pltpu.SemaphoreType.DMA((n,))), start, finish)


def _share_rider(fs):
    n = len(fs)

    def start(_, outs, sems):
        x, y, c, _p = _mesh_place()
        for a in range(n):
            mine = outs[a].at[:, c]
            _remote(mine, mine, sems[0].at[a], sems[1].at[a], (x, y, 1 - c)).start()

    def finish(_, outs, sems):
        x, y, c, _p = _mesh_place()
        for a in range(n):
            theirs = outs[a].at[:, 1 - c]
            _remote(theirs, theirs, sems[0].at[a], sems[1].at[a], (x, y, c)).wait_recv()
        for a in range(n):
            mine = outs[a].at[:, c]
            _remote(mine, mine, sems[0].at[a], sems[1].at[a], (x, y, 1 - c)).wait_send()

    return Rider(tuple(fs), tuple(_sds(f.shape, F32) for f in fs), {a: a for a in range(n)},
                 (pltpu.SemaphoreType.DMA((n,)), pltpu.SemaphoreType.DMA((n,))), start, finish)


def _both_riders(r1, r2):
    ni, no, ns = len(r1.operands), len(r1.out_shape), len(r1.sems)
    aliases = dict(r1.aliases)
    aliases.update({ni + i: no + o for i, o in r2.aliases.items()})

    def start(ins, outs, sems):
        r1.start(ins[:ni], outs[:no], sems[:ns])
        r2.start(ins[ni:], outs[no:], sems[ns:])

    def finish(ins, outs, sems):
        _rider_end(r1, ins[:ni], outs[:no], sems[:ns])
        _rider_end(r2, ins[ni:], outs[no:], sems[ns:])

    return Rider(r1.operands + r2.operands, r1.out_shape + r2.out_shape, aliases, r1.sems + r2.sems, start, finish)


def _pack_rows(parts, lane_mult=1024):
    flat = jnp.concatenate([p.reshape(-1).astype(F32) for p in parts])
    n = -(-flat.shape[0] // (8 * lane_mult)) * lane_mult
    return jnp.pad(flat, (0, 8 * n - flat.shape[0])).reshape(8, n)


def _relu2(acc):
    r = jnp.maximum(acc, 0.0)
    return r, r * r


def _times_2r(acc, r):
    return (acc * (2.0 * r.astype(F32)),)


def kernel(x, c, positions, w_mod, b_mod, norm_g, mla_w_in, mla_g_q, mla_g_kv, mla_w_uq, mla_w_ukv, mla_w_o, conv_w_in, conv_w, conv_w_out, mlp_w_up, mlp_w_down, loss_target, m_w_mod, m_b_mod, m_norm_g, m_mla_w_in, m_mla_g_q, m_mla_g_kv, m_mla_w_uq, m_mla_w_ukv, m_mla_w_o, m_conv_w_in, m_conv_w, m_conv_w_out, m_mlp_w_up, m_mlp_w_down, v_w_mod, v_b_mod, v_norm_g, v_mla_w_in, v_mla_g_q, v_mla_g_kv, v_mla_w_uq, v_mla_w_ukv, v_mla_w_o, v_conv_w_in, v_conv_w, v_conv_w_out, v_mlp_w_up, v_mlp_w_down):
    S, D = x.shape[1], x.shape[2]
    Dq = D // N_CHIPS
    ncol = w_mod.shape[2]
    n_mod = N_CHIPS * ncol // D
    F = mlp_w_up.shape[2] * N_CHIPS
    lat_dim = mla_w_in.shape[2]
    rank = mla_g_q.shape[1]
    H = mla_w_uq.shape[2]
    d_qk = mla_w_uq.shape[3]
    assert mla_g_kv.shape[1] == rank and lat_dim == 2 * rank + QK_ROPE and d_qk == QK_NOPE + QK_ROPE
    assert mla_w_ukv.shape[3] == QK_NOPE + V_HEAD and x.shape[0] == 1 and n_mod == 6
    assert norm_g.shape[0] == 2 and mla_w_in.shape[0] == 1 and conv_w_in.shape[0] == 1
    lat_pad = 2 * rank + LANES
    scale = float(d_qk) ** -0.5

    xi, yi, ci = lax.axis_index("x"), lax.axis_index("y"), lax.axis_index("c")
    chip = 2 * xi + yi
    dev = 2 * chip + ci
    c_idx = jnp.reshape(ci, (1,)).astype(jnp.int32)
    k_idx = jnp.reshape(chip, (1,)).astype(jnp.int32)

    n1 = D + 2 * D + 3 * Dq
    g1 = _small_allgather("gather_small_inputs", _pack_rows([c, norm_g, conv_w])).reshape(8, -1)
    c_all = g1[:, :D]
    by_chip = g1[0::2]
    norm_full = jnp.concatenate([by_chip[kk, D:3 * D].reshape(2, 4, Dq) for kk in range(N_CHIPS)], axis=-1)
    convw_full = jnp.concatenate([by_chip[kk, 3 * D:n1].reshape(3, Dq) for kk in range(N_CHIPS)], axis=-1)

    b_cols = lax.dynamic_slice(b_mod, (0, chip * ncol), (2, ncol)).reshape(2, 1, ncol)
    cond_all = _silu(c_all)
    mod_cols = _mod_fwd(cond_all, w_mod, b_cols)
    g2 = _small_allgather("gather_mod", _pack_rows([mod_cols]))
    g2 = g2.reshape(8, -1)[0::2, :2 * 8 * ncol].reshape(N_CHIPS, 2, 8, ncol)
    mod_all = jnp.transpose(g2, (2, 1, 0, 3)).reshape(8, 2, n_mod * D)
    mod_me = lax.dynamic_index_in_dim(mod_all, dev, axis=0, keepdims=False)
    mods = [[mod_me[l, i * D:(i + 1) * D].reshape(1, D) for i in range(n_mod)] for l in range(2)]
    ng = [[norm_full[l, i].reshape(1, D) for i in range(4)] for l in range(2)]

    pos = positions[0].astype(F32)
    inv_freq = ROPE_THETA ** (-jnp.arange(0, QK_ROPE, 2, dtype=F32) / QK_ROPE)
    ang = pos[:, None] * inv_freq
    cos, sin = jnp.cos(ang), jnp.sin(ang)
    zero = jnp.zeros_like(cos)
    rope_tabs = (jnp.concatenate([cos, cos, zero, zero], axis=1),
                 jnp.concatenate([-sin, zero, zero, zero], axis=1),
                 jnp.concatenate([zero, sin, zero, zero], axis=1))

    weights = [("mla_w_in", mla_w_in, "row"), ("mla_w_uq", mla_w_uq.reshape(1, rank // N_CHIPS, H * d_qk), "row"),
               ("mla_w_ukv", mla_w_ukv.reshape(1, rank // N_CHIPS, H * QK_PAD), "row"), ("mla_w_o", mla_w_o, "row"),
               ("conv_w_in", conv_w_in, "col"), ("conv_w_out", conv_w_out, "row"),
               ("mlp_w_up", mlp_w_up, "col"), ("mlp_w_down", mlp_w_down, "row")]
    kinds = [k for _, _, k in weights]
    shard_shapes = [w.shape for _, w, _ in weights]
    shard_cols = [s[2] for s in shard_shapes]
    W_IN, W_UQ, W_UKV, W_O, W_CIN, W_COUT, W_UP, W_DOWN = range(8)
    mla_idx = [W_IN, W_UQ, W_UKV, W_O]
    casted = [_cast_into_full("cast_" + nm, [w], [kind], k_idx)[0][0] for nm, w, kind in weights[:W_UP]]

    def view(i, buf):
        L, R, C = shard_shapes[i]
        return buf.reshape((L, N_CHIPS * R, C) if kinds[i] == "row" else (L, R, N_CHIPS * C))

    NEIGHBOURS, DIAGONAL = (0, 1), (2,)

    def gather_of(bufs, idx, layers=None, peers=(0, 1, 2)):
        return _gather_rider(bufs, [kinds[i] for i in idx], [shard_cols[i] for i in idx], layers, peers)

    def scatter_of(ps, idx, peers=(0, 1, 2), into=None):
        return _scatter_rider(ps, [kinds[i] for i in idx], [shard_cols[i] for i in idx], peers, into)

    def halves(items):
        g5s = []
        for _, i, g in items:
            _, R, C = shard_shapes[i]
            g5s.append(g.reshape((1, N_CHIPS, 2, R // 2, C) if kinds[i] == "row" else (1, 1, 2, R // 2, N_CHIPS * C)))
        return g5s

    def pair_sums(items, g5s, ras):
        return [_pair_sum("pair_sum_" + nm, g5, ra, c_idx) for (nm, _, _), g5, ra in zip(items, g5s, ras)]

    mlp_casted, got = _cast_into_full("cast_mlp_w", [mlp_w_up, mlp_w_down], [kinds[W_UP], kinds[W_DOWN]], k_idx,
                                      gather_of([casted[W_IN]], [W_IN]))
    casted += list(mlp_casted)
    w_in_p = jnp.pad(view(W_IN, got[0])[0], ((0, 0), (0, lat_pad - lat_dim)))
    HV = H * V_HEAD

    def layer_b(l, transposed):
        if transposed:
            return lambda tm, tn, tk: pl.BlockSpec((None, tn, tk), lambda i, j, k: (l, j, k))
        return lambda tm, tn, tk: pl.BlockSpec((None, tk, tn), lambda i, j, k: (l, k, j))

    def mlp_up(tag, l, h, w, rider=NO_RIDER):
        return _mm("mlp_up_" + tag, h, w, "nn", S, F, D, [_sds((S, F), BF16)] * 2, epilogue=_relu2,
                   b_spec=layer_b(l, False), rider=rider)

    def mlp_down(tag, l, a2, w, rider=NO_RIDER):
        return _mm("mlp_down_" + tag, a2, w, "nn", S, D, F, [_sds((S, D), BF16)], b_spec=layer_b(l, False), rider=rider)

    def mlp_bwd(tag, l, h, r, a2, dy, first=NO_RIDER, plan=None):
        res = _mm("mlp_down_dx_" + tag, dy, w_down, "nt", S, F, D, [_sds((S, F), BF16)], epilogue=_times_2r,
                  b_spec=layer_b(l, True), rider=first,
                  extras=[(r, lambda tm, tn, tk: pl.BlockSpec((tm, tn), lambda i, j, k: (i, j)))])
        (da,), got_first = res if first.start is not None else (res, ())
        second, third, fourth = plan(got_first) if plan else (NO_RIDER,) * 3
        res = _mm("mlp_down_dw_" + tag, a2, dy, "tn", F, D, S, [_sds((F, D), BF16)], rider=second)
        (dw_down,), got_second = res if plan else (res, ())
        res = _mm("mlp_up_dx_" + tag, da, w_up, "nt", S, D, F, [_sds((S, D), BF16)], b_spec=layer_b(l, True), rider=third)
        (dh,), got_third = res if plan else (res, ())
        res = _mm("mlp_up_dw_" + tag, h, da, "tn", D, F, S, [_sds((D, F), BF16)], rider=fourth)
        (dw_up,), got_fourth = res if plan else (res, ())
        return dh, dw_up, dw_down, (got_first, got_second, got_third, got_fourth)

    x0 = x[0]
    sh1, sc1, gt1, sh2, sc2, gt2 = mods[0]
    (h1,) = _fwd_boundary("fwd_boundary_0", x0, None, None, None, ng[0][0], sc1, sh1)
    (lat,), (uq_buf, ukv_buf) = _mm("mla_in", h1, w_in_p, "nn", S, lat_pad, D, [_sds((S, lat_pad), F32)], tn=lat_pad,
                                    rider=gather_of([casted[W_UQ], casted[W_UKV]], [W_UQ, W_UKV]))
    w_q_p = jnp.pad(view(W_UQ, uq_buf)[0].reshape(rank, H, d_qk), ((0, 0), (0, 0), (0, QK_PAD - d_qk))).reshape(rank, H * QK_PAD)
    w_ukv = view(W_UKV, ukv_buf)[0]
    cq, ckv, kr = _latent_fwd(lat, mla_g_q, mla_g_kv, rope_tabs, rank)

    def rope_q(acc, cos_p, sin_lo, sin_hi):
        parts = []
        for hh in range(acc.shape[1] // QK_PAD):
            parts.append(acc[:, hh * QK_PAD:hh * QK_PAD + QK_NOPE])
            parts.append(_rope(acc[:, hh * QK_PAD + QK_NOPE:(hh + 1) * QK_PAD], cos_p, sin_lo, sin_hi))
        return (jnp.concatenate(parts, axis=1),)

    tab_extra = lambda tm, tn, tk: pl.BlockSpec((tm, LANES), lambda i, j, k: (i, 0))
    (q,), (o_buf,) = _mm("mla_q", cq, w_q_p, "nn", S, H * QK_PAD, rank, [_sds((S, H * QK_PAD), BF16)], epilogue=rope_q,
                         extras=[(t, tab_extra) for t in rope_tabs], tn=2 * QK_PAD,
                         rider=gather_of([casted[W_O]], [W_O]))
    w_o = view(W_O, o_buf)[0]
    (kv,) = _mm("mla_kv", ckv, w_ukv, "nn", S, H * QK_PAD, rank, [_sds((S, H * QK_PAD), BF16)])
    rest_idx = [W_UP, W_DOWN]
    o, lse, (up_buf, down_buf) = _attn_fwd_tri(
        q, kv, kr, H, scale, gather_of([casted[i] for i in rest_idx], rest_idx, [0, 0]))
    (y1,), (cout_buf,) = _mm("mla_out", o, w_o, "nn", S, D, HV, [_sds((S, D), BF16)],
                             rider=gather_of([casted[W_COUT]], [W_COUT]))
    x1, h2 = _fwd_boundary("fwd_boundary_1", x0, y1, gt1, ng[0][1], ng[0][2], sc2, sh2)
    (r2, a2), (cin_buf,) = mlp_up("0", 0, h2, view(W_UP, up_buf), gather_of([casted[W_CIN]], [W_CIN]))
    (y2,), (up_buf,) = mlp_down("0", 0, a2, view(W_DOWN, down_buf), gather_of([up_buf], [W_UP], [1]))
    w_cin, w_cout, w_up = view(W_CIN, cin_buf)[0], view(W_COUT, cout_buf)[0], view(W_UP, up_buf)

    sh1b, sc1b, gt1b, sh2b, sc2b, gt2b = mods[1]
    x2, h3 = _fwd_boundary("fwd_boundary_2", x1, y2, gt2, ng[0][3], ng[1][0], sc1b, sh1b)
    nD = lambda tn: D // tn
    (proj3,), (down_buf,) = _mm(
        "conv_in", h3, w_cin, "nn", S, 3 * D, D, [_sds((3, S, D), BF16)], tn=min(1024, D),
        rider=gather_of([down_buf], [W_DOWN], [1], NEIGHBOURS),
        out_specs=[lambda tm, tn, tk: pl.BlockSpec((None, tm, tn), lambda i, j, k: (j // nD(tn), i, j % nD(tn)))])
    bz = _conv_fwd(proj3, convw_full)
    (y3,) = _mm("conv_out", bz, w_cout, "nn", S, D, D, [_sds((S, D), BF16)])
    x3, h4 = _fwd_boundary("fwd_boundary_3", x2, y3, gt1b, ng[1][1], ng[1][2], sc2b, sh2b)
    (r4, a4), (down_buf,) = mlp_up("1", 1, h4, w_up, gather_of([down_buf], [W_DOWN], [1], DIAGONAL))
    w_down = view(W_DOWN, down_buf)
    (y4,) = mlp_down("1", 1, a4, w_down)

    dx4, dy4, sums_l, loss_acc = _loss_boundary("loss_boundary", x3, y4, gt2b, ng[1][3], loss_target[0])

    dh4, dw_up1, dw_down1, _ = mlp_bwd("1", 1, h4, r4, a4, dy4)
    dx3, dy3, sums_3, _ = _bwd_boundary("bwd_boundary_3", dx4, dh4, x3, y3, gt1b, ng[1][1], ng[1][2], sc2b)

    items = [("mlp_w_up_1", W_UP, dw_up1)]
    g5s = halves(items)
    (dbz,), ras = _mm("conv_out_dx", dy3, w_cout, "nt", S, D, D, [_sds((S, D), BF16)], rider=_exchange_rider(g5s))
    (ps_up1,) = pair_sums(items, g5s, ras)
    items = [("mlp_w_down_1", W_DOWN, dw_down1)]
    g5s = halves(items)
    (dw_cout,), ras = _mm("conv_out_dw", bz, dy3, "tn", D, D, S, [_sds((D, D), BF16)], rider=_exchange_rider(g5s))
    (ps_down1,) = pair_sums(items, g5s, ras)
    dproj3, dconvw = _conv_bwd(dbz, proj3, convw_full)
    (dh3,), (rb_up1,) = _mm(
        "conv_in_dx", dproj3, w_cin, "nt", S, D, 3 * D, [_sds((S, D), BF16)], tk=D,
        rider=scatter_of([ps_up1], [W_UP], NEIGHBOURS),
        a_spec=lambda tm, tn, tk: pl.BlockSpec((None, tm, tk), lambda i, j, k: (k // (D // tk), i, k % (D // tk))))
    (dw_cin,), (rb_up1,) = _mm(
        "conv_in_dw", h3, dproj3, "tn", D, 3 * D, S, [_sds((D, 3 * D), BF16)], tn=min(1024, D),
        rider=scatter_of([ps_up1], [W_UP], DIAGONAL, [rb_up1]),
        b_spec=lambda tm, tn, tk: pl.BlockSpec((None, tk, tn), lambda i, j, k: (j // nD(tn), k, j % nD(tn))))
    dx2, dy2, sums_2, _ = _bwd_boundary("bwd_boundary_2", dx3, dh3, x2, y2, gt2, ng[0][3], ng[1][0], sc1b)

    items = [("conv_w_in", W_CIN, dw_cin), ("conv_w_out", W_COUT, dw_cout)]
    g5s = halves(items)
    conv_ps = []

    def plan(got):
        conv_ps.extend(pair_sums(items, g5s, got[1:]))
        return (scatter_of([ps_down1], [W_DOWN], DIAGONAL, [got[0]]),
                scatter_of(conv_ps[:1], [W_CIN]), scatter_of(conv_ps[1:], [W_COUT]))

    dh2, dw_up0, dw_down0, (_, (rb_down1,), (rb_cin,), (rb_cout,)) = mlp_bwd(
        "0", 0, h2, r2, a2, dy2,
        _both_riders(scatter_of([ps_down1], [W_DOWN], NEIGHBOURS), _exchange_rider(g5s)), plan)
    ps_cin, ps_cout = conv_ps
    dx1, dy1, sums_1, _ = _bwd_boundary("bwd_boundary_1", dx2, dh2, x1, y1, gt1, ng[0][1], ng[0][2], sc2)

    items = [("mlp_w_up_0", W_UP, dw_up0)]
    g5s = halves(items)
    (dw_o,), ras = _mm("mla_out_dw", o, dy1, "tn", HV, D, S, [_sds((HV, D), BF16)], rider=_exchange_rider(g5s))
    (ps_up0,) = pair_sums(items, g5s, ras)
    items = [(weights[W_O][0], W_O, dw_o), ("mlp_w_down_0", W_DOWN, dw_down0)]
    g5s = halves(items)
    (do,), ras = _mm("mla_out_dx", dy1, w_o, "nt", S, HV, D, [_sds((S, HV), BF16)], rider=_exchange_rider(g5s))
    ps_o, ps_down0 = pair_sums(items, g5s, ras)
    dq, dkv, dkr, (rb_up0, rb_down0, rb_o) = _attn_bwd_tri(
        q, kv, kr, o, do, lse, rope_tabs, H, scale,
        scatter_of([ps_up0, ps_down0, ps_o], [W_UP, W_DOWN, W_O]))
    kc_idx = jnp.stack([chip, ci]).astype(jnp.int32)
    f_o, f_cin, f_cout = [_chip_sum("chip_sum_" + weights[i][0], p, rb, kc_idx, kinds[i])
                          for i, p, rb in [(W_O, ps_o, rb_o), (W_CIN, ps_cin, rb_cin), (W_COUT, ps_cout, rb_cout)]]
    f_mlp = []
    for i, (p1, r1), (p0, r0) in [(W_UP, (ps_up1, rb_up1), (ps_up0, rb_up0)), (W_DOWN, (ps_down1, rb_down1), (ps_down0, rb_down0))]:
        f = _chip_sum("chip_sum_" + weights[i][0] + "_1", p1, r1, kc_idx, kinds[i], layer=1, n_layers=2)
        f_mlp.append(_chip_sum("chip_sum_" + weights[i][0] + "_0", p0, r0, kc_idx, kinds[i], layer=0, n_layers=2, prev=f))
    (dcq,), fin_up = _mm("mla_q_dx", dq, w_q_p, "nt", S, rank, H * QK_PAD, [_sds((S, rank), F32)],
                         rider=_share_rider([f_mlp[0]]))
    (dw_q_p,), fin_small = _mm("mla_q_dw", cq, dq, "tn", rank, H * QK_PAD, S, [_sds((rank, H * QK_PAD), BF16)],
                               rider=_share_rider([f_o, f_cin, f_cout]))
    (dckv,), fin_down = _mm("mla_kv_dx", dkv, w_ukv, "nt", S, rank, H * QK_PAD, [_sds((S, rank), F32)],
                            rider=_share_rider([f_mlp[1]]))
    finals_rest = list(fin_small) + list(fin_up) + list(fin_down)
    (dw_ukv,) = _mm("mla_kv_dw", ckv, dkv, "tn", rank, H * QK_PAD, S, [_sds((rank, H * QK_PAD), BF16)])
    dlat, sums_lat = _latent_bwd(lat, dcq, dckv, dkr, mla_g_q, mla_g_kv, rope_tabs, rank)
    late_idx = [W_IN, W_UQ, W_UKV]
    items_qkv = [(weights[W_UQ][0], W_UQ, dw_q_p.reshape(rank, H, QK_PAD)[:, :, :d_qk].reshape(rank, H * d_qk)),
                 (weights[W_UKV][0], W_UKV, dw_ukv)]
    g5s_qkv = halves(items_qkv)
    (dw_in_p,), ras = _mm("mla_in_dw", h1, dlat, "tn", D, lat_pad, S, [_sds((D, lat_pad), BF16)], tn=lat_pad,
                          rider=_exchange_rider(g5s_qkv))
    ps_qkv = pair_sums(items_qkv, g5s_qkv, ras)
    items_in = [(weights[W_IN][0], W_IN, dw_in_p[:, :lat_dim])]
    g5s_in = halves(items_in)
    (dh1,), carried = _mm("mla_in_dx", dlat, w_in_p, "nt", S, D, lat_pad, [_sds((S, D), BF16)],
                          rider=_both_riders(scatter_of(ps_qkv, [W_UQ, W_UKV]), _exchange_rider(g5s_in)))
    rbs_qkv, ras = carried[:2], carried[2:]
    ps_in = pair_sums(items_in, g5s_in, ras)
    ps_mla = ps_in + ps_qkv
    grad_x, sums_0, _ = _bwd_boundary("bwd_boundary_0", dx1, dh1, x0, None, None, None, ng[0][0], sc1)

    dmod0 = [sums_0[0], sums_0[1], sums_1[3], sums_1[0], sums_1[1], sums_2[3]]
    dmod1 = [sums_2[0], sums_2[1], sums_3[3], sums_3[0], sums_3[1], sums_l[3]]
    dng0 = [sums_0[2], sums_1[4], sums_1[2], sums_2[4]]
    dng1 = [sums_2[2], sums_3[4], sums_3[2], sums_l[4]]
    small = _pack_rows(dmod0 + dmod1 + dng0 + dng1 + [sums_lat[0], sums_lat[1], dconvw, loss_acc[0, 0:1]],
                       lane_mult=LANES)
    gathered, total, carried = _small_allgather(
        "gather_small_grads", small, with_sum=True, rider=scatter_of(ps_in, [W_IN]))
    rbs_mla = list(carried) + list(rbs_qkv)
    n_dm = 2 * n_mod * D
    dmod_all = gathered.reshape(8, -1)[:, :n_dm].reshape(8, 2, n_mod * D)
    total = total.reshape(-1)
    g_b_mod = total[:n_dm].reshape(2, n_mod * D)
    g_norm = lax.dynamic_slice(total[n_dm:n_dm + 8 * D].reshape(2, 4, D), (0, 0, chip * Dq), (2, 4, Dq))
    off = n_dm + 8 * D
    g_gq = total[off:off + rank].reshape(1, rank)
    g_gkv = total[off + rank:off + 2 * rank].reshape(1, rank)
    off += 2 * rank
    g_convw = lax.dynamic_slice(total[off:off + 3 * D].reshape(1, 3, D), (0, 0, chip * Dq), (1, 3, Dq))
    loss = total[off + 3 * D]

    dmod_cols = jnp.transpose(lax.dynamic_slice(dmod_all.reshape(8, 2, N_CHIPS, ncol), (0, 0, chip, 0), (8, 2, 1, ncol))
                              .reshape(8, 2, ncol), (1, 0, 2))
    g_w_mod, d_w_mod, nm_w_mod, nv_w_mod, _ = _adamw_mod(w_mod, cond_all.T, dmod_cols, m_w_mod, v_w_mod)
    fs_mla = [_chip_sum("chip_sum_" + weights[i][0], p, rb, kc_idx, kinds[i]) for i, p, rb in zip(late_idx, ps_mla, rbs_mla)]
    finals = list(_run_rider("grad_pair_share_mla", _share_rider(fs_mla))) + list(finals_rest)
    orig = [mla_w_in, mla_w_uq, mla_w_ukv, mla_w_o, conv_w_in, conv_w_out, mlp_w_up, mlp_w_down]
    big_grads = [f.reshape(w.shape) for f, w in zip(finals, orig)]

    names = ["b_mod", "norm_g", "mla_w_in", "mla_g_q", "mla_g_kv", "mla_w_uq", "mla_w_ukv", "mla_w_o",
             "conv_w_in", "conv_w", "conv_w_out", "mlp_w_up", "mlp_w_down"]
    ws = [b_mod, norm_g, mla_w_in, mla_g_q, mla_g_kv, mla_w_uq, mla_w_ukv, mla_w_o, conv_w_in, conv_w, conv_w_out,
          mlp_w_up, mlp_w_down]
    ms = [m_b_mod, m_norm_g, m_mla_w_in, m_mla_g_q, m_mla_g_kv, m_mla_w_uq, m_mla_w_ukv, m_mla_w_o, m_conv_w_in,
          m_conv_w, m_conv_w_out, m_mlp_w_up, m_mlp_w_down]
    vs = [v_b_mod, v_norm_g, v_mla_w_in, v_mla_g_q, v_mla_g_kv, v_mla_w_uq, v_mla_w_ukv, v_mla_w_o, v_conv_w_in,
          v_conv_w, v_conv_w_out, v_mlp_w_up, v_mlp_w_down]
    gs = [g_b_mod, g_norm, big_grads[0], g_gq, g_gkv, big_grads[1], big_grads[2], big_grads[3], big_grads[4],
          g_convw, big_grads[5], big_grads[6], big_grads[7]]
    grads, deltas, new_ms, new_vs = [g_w_mod], [d_w_mod], [nm_w_mod], [nv_w_mod]
    for nm, w, g, m, v in zip(names, ws, gs, ms, vs):
        big = any(g is b for b in big_grads)
        d, nm_, nv_, *g_out = _adamw("adamw_" + nm, w, g, m, v, emit_grad=big)
        g = g_out[0] if big else g
        grads.append(g)
        deltas.append(d)
        new_ms.append(nm_)
        new_vs.append(nv_)
    return (loss, grad_x[None], *grads, *deltas, *new_ms, *new_vs)
```

```python
from typing import NamedTuple

import jax
import jax.numpy as jnp
from jax import lax
from jax.experimental import pallas as pl
from jax.experimental.pallas import tpu as pltpu

F32 = jnp.float32
BF16 = jnp.bfloat16
NORM_EPS = 1e-6
ROPE_THETA = 10000.0
QK_NOPE = 128
QK_ROPE = 64
V_HEAD = 128
LANES = 128
QK_PAD = QK_NOPE + LANES
ADAM_LR, ADAM_B1, ADAM_B2, ADAM_EPS, ADAM_WD, ADAM_STEP = 0.001, 0.9, 0.999, 1e-08, 0.01, 10
VMEM_LIMIT_BYTES = 56 * 1024 * 1024
N_CHIPS = 4
MESH_ID = pl.DeviceIdType.MESH
ANY = pl.BlockSpec(memory_space=pl.ANY)
NEG_INF = float("-inf")

DIMS_NN = (((1,), (0,)), ((), ()))
DIMS_NT = (((1,), (1,)), ((), ()))
DIMS_TN = (((0,), (0,)), ((), ()))


def _cparams(*sem):
    return pltpu.CompilerParams(dimension_semantics=sem, vmem_limit_bytes=VMEM_LIMIT_BYTES)


def _row_tile(rows, row_bytes, limit=2 * 1024 * 1024, mult=16):
    if rows * row_bytes <= limit or rows % mult:
        return rows
    best = mult
    t = mult
    while t <= rows:
        if rows % t == 0 and t * row_bytes <= limit:
            best = t
        t += mult
    return best


def _rms(v):
    return lax.rsqrt(jnp.mean(v * v, axis=-1, keepdims=True) + NORM_EPS)


class Rider(NamedTuple):
    operands: tuple
    out_shape: tuple
    aliases: dict
    sems: tuple
    start: object
    finish: object
    mid: object = None


NO_RIDER = Rider((), (), {}, (), None, None)


def _rider_end(rider, r_in, r_out, r_sems):
    if rider.mid is not None:
        rider.mid(r_in, r_out, r_sems)
    rider.finish(r_in, r_out, r_sems)


def _mm(name, a, b, mode, M, N, K, outs, *, a_spec=None, b_spec=None, out_specs=None, epilogue=None,
        extras=(), rider=NO_RIDER, tm=1024, tn=1024, tk=4096):
    tm, tn, tk = min(tm, M), min(tn, N), min(tk, K)
    assert M % tm == 0 and N % tn == 0 and K % tk == 0, (name, M, N, K)
    nk = K // tk
    if a_spec is None:
        a_spec = {"nn": pl.BlockSpec((tm, tk), lambda i, j, k: (i, k)),
                  "nt": pl.BlockSpec((tm, tk), lambda i, j, k: (i, k)),
                  "tn": pl.BlockSpec((tk, tm), lambda i, j, k: (k, i))}[mode]
    else:
        a_spec = a_spec(tm, tn, tk)
    if b_spec is None:
        b_spec = {"nn": pl.BlockSpec((tk, tn), lambda i, j, k: (k, j)),
                  "nt": pl.BlockSpec((tn, tk), lambda i, j, k: (j, k)),
                  "tn": pl.BlockSpec((tk, tn), lambda i, j, k: (k, j))}[mode]
    else:
        b_spec = b_spec(tm, tn, tk)
    if out_specs is None:
        out_specs = [pl.BlockSpec((tm, tn), lambda i, j, k: (i, j)) for _ in outs]
    else:
        out_specs = [s(tm, tn, tk) for s in out_specs]
    dims = {"nn": DIMS_NN, "nt": DIMS_NT, "tn": DIMS_TN}[mode]
    ne, no = len(extras), len(outs)
    n_ri, n_ro = len(rider.operands), len(rider.out_shape)
    grid = (M // tm, N // tn, nk)

    def body(*refs):
        a_ref, b_ref = refs[0], refs[1]
        ex = refs[2:2 + ne]
        r_in = refs[2 + ne:2 + ne + n_ri]
        o = refs[2 + ne + n_ri:2 + ne + n_ri + no]
        r_out = refs[2 + ne + n_ri + no:2 + ne + n_ri + no + n_ro]
        scratch = refs[2 + ne + n_ri + no + n_ro:]
        r_sems = scratch[1:] if nk > 1 else scratch
        ii, jj, kk = pl.program_id(0), pl.program_id(1), pl.program_id(2)

        if rider.start is not None:
            @pl.when((ii == 0) & (jj == 0) & (kk == 0))
            def _():
                rider.start(r_in, r_out, r_sems)

        part = lax.dot_general(a_ref[...].astype(BF16), b_ref[...].astype(BF16), dims,
                               preferred_element_type=F32)

        def finish(total):
            vals = epilogue(total, *[e[...] for e in ex]) if epilogue is not None else (total,)
            for r, v in zip(o, vals):
                r[...] = v.astype(r.dtype)

        if nk == 1:
            finish(part)
        else:
            acc = scratch[0]

            @pl.when(kk == 0)
            def _():
                acc[...] = part

            @pl.when(kk > 0)
            def _():
                acc[...] += part

            @pl.when(kk == nk - 1)
            def _():
                finish(acc[...])

        if rider.finish is not None:
            steps = grid[0] * grid[1] * nk
            if rider.mid is not None and steps >= 4:
                @pl.when((ii * grid[1] + jj) * nk + kk == steps // 2)
                def _():
                    rider.mid(r_in, r_out, r_sems)

            @pl.when((ii == grid[0] - 1) & (jj == grid[1] - 1) & (kk == nk - 1))
            def _():
                if rider.mid is not None and steps < 4:
                    rider.mid(r_in, r_out, r_sems)
                rider.finish(r_in, r_out, r_sems)

    operands = [a, b] + [e[0] for e in extras] + list(rider.operands)
    in_specs = [a_spec, b_spec] + [e[1](tm, tn, tk) for e in extras] + [ANY] * n_ri
    hosted = rider.start is not None
    res = pl.pallas_call(
        body, name=name, grid=grid,
        in_specs=in_specs, out_specs=out_specs + [ANY] * n_ro, out_shape=list(outs) + list(rider.out_shape),
        scratch_shapes=([pltpu.VMEM((tm, tn), F32)] if nk > 1 else []) + list(rider.sems),
        input_output_aliases={2 + ne + i: no + r for i, r in rider.aliases.items()},
        compiler_params=_cparams(*(("arbitrary",) * 3 if hosted else ("parallel", "parallel", "arbitrary"))),
    )(*operands)
    return (res[:no], res[no:]) if hosted else res


def _sds(shape, dtype):
    return jax.ShapeDtypeStruct(tuple(shape), dtype)


def _rope(t, cos_p, sin_lo, sin_hi):
    return t * cos_p + pltpu.roll(t, LANES - QK_ROPE // 2, 1) * sin_lo + pltpu.roll(t, QK_ROPE // 2, 1) * sin_hi


def _rope_t(d, cos_p, sin_lo, sin_hi):
    return d * cos_p + pltpu.roll(d * sin_lo, QK_ROPE // 2, 1) + pltpu.roll(d * sin_hi, LANES - QK_ROPE // 2, 1)


def _vec_spec(d):
    return pl.BlockSpec((1, d), lambda i: (0, 0))


def _fwd_boundary(name, x_prev, y, gate, ng_post, ng_pre, sc, sh):
    S, D = x_prev.shape
    ts = min(256, S)
    has_y = y is not None
    row = pl.BlockSpec((ts, D), lambda i: (i, 0))

    def body(*refs):
        if has_y:
            x_ref, y_ref, g_ref, ngp_ref, ngn_ref, sc_ref, sh_ref, xo_ref, h_ref = refs
            yv = y_ref[...].astype(F32)
            xn = x_ref[...] + g_ref[...] * (yv * _rms(yv) * ngp_ref[...])
            xo_ref[...] = xn
        else:
            x_ref, ngn_ref, sc_ref, sh_ref, h_ref = refs
            xn = x_ref[...]
        hn = xn * _rms(xn) * ngn_ref[...]
        h_ref[...] = (hn * (1.0 + sc_ref[...]) + sh_ref[...]).astype(BF16)

    vec = _vec_spec(D)
    if has_y:
        operands = (x_prev, y, gate, ng_post, ng_pre, sc, sh)
        in_specs = [row, row, vec, vec, vec, vec, vec]
        out_shape = [_sds((S, D), F32), _sds((S, D), BF16)]
        out_specs = [row, row]
    else:
        operands = (x_prev, ng_pre, sc, sh)
        in_specs = [row, vec, vec, vec]
        out_shape = [_sds((S, D), BF16)]
        out_specs = [row]
    return pl.pallas_call(body, name=name, grid=(S // ts,), in_specs=in_specs, out_specs=out_specs,
                          out_shape=out_shape, compiler_params=_cparams("parallel"))(*operands)


def _acc_rows(sums_ref, rows):
    for r, v in rows:
        sums_ref[r:r + 1, :] += jnp.sum(v, axis=0, keepdims=True)


def _post_norm_bwd(dxt, yv, gate, ng_post, sums_ref, dy_ref):
    r1 = _rms(yv)
    yhat = yv * r1
    dn = dxt * gate
    u = dn * ng_post
    dy = r1 * (u - yhat * jnp.mean(u * yhat, axis=-1, keepdims=True))
    dy_ref[...] = dy.astype(dy_ref.dtype)
    _acc_rows(sums_ref, [(3, dxt * (yhat * ng_post)), (4, dn * yhat)])


def _loss_boundary(name, x_prev, y, gate, ng_post, target):
    S, D = x_prev.shape
    ts = min(256, S)
    row = pl.BlockSpec((ts, D), lambda i: (i, 0))
    vec = _vec_spec(D)

    def body(x_ref, y_ref, g_ref, ngp_ref, t_ref, dx_ref, dy_ref, sums_ref, loss_ref):
        @pl.when(pl.program_id(0) == 0)
        def _():
            sums_ref[...] = jnp.zeros_like(sums_ref)
            loss_ref[...] = jnp.zeros_like(loss_ref)

        yv = y_ref[...].astype(F32)
        xf = x_ref[...] + g_ref[...] * (yv * _rms(yv) * ngp_ref[...])
        err = xf - t_ref[...]
        loss_ref[...] += 0.5 * jnp.sum(jnp.mean(err * err, axis=-1, keepdims=True))
        dxt = err / D
        dx_ref[...] = dxt
        _post_norm_bwd(dxt, yv, g_ref[...], ngp_ref[...], sums_ref, dy_ref)

    return pl.pallas_call(
        body, name=name, grid=(S // ts,),
        in_specs=[row, row, vec, vec, row],
        out_specs=[row, row, pl.BlockSpec((8, D), lambda i: (0, 0)), pl.BlockSpec((8, LANES), lambda i: (0, 0))],
        out_shape=[_sds((S, D), F32), _sds((S, D), BF16), _sds((8, D), F32), _sds((8, LANES), F32)],
        compiler_params=_cparams("arbitrary"))(x_prev, y, gate, ng_post, target)


def _bwd_boundary(name, dx_new, dh, x_new, y, gate, ng_post, ng_pre, sc, rider=NO_RIDER):
    S, D = x_new.shape
    ts = min(256, S)
    has_y = y is not None
    row = pl.BlockSpec((ts, D), lambda i: (i, 0))
    vec = _vec_spec(D)
    n_in, n_out = (8, 3) if has_y else (5, 2)
    n_ri, n_ro = len(rider.operands), len(rider.out_shape)

    def body(*refs):
        r_in = refs[n_in:n_in + n_ri]
        r_out = refs[n_in + n_ri + n_out:n_in + n_ri + n_out + n_ro]
        r_sems = refs[n_in + n_ri + n_out + n_ro:]
        own = refs[:n_in] + refs[n_in + n_ri:n_in + n_ri + n_out]
        if has_y:
            dxn_ref, dh_ref, x_ref, y_ref, g_ref, ngp_ref, ngn_ref, sc_ref, dxo_ref, dy_ref, sums_ref = own
        else:
            dxn_ref, dh_ref, x_ref, ngn_ref, sc_ref, dxo_ref, sums_ref = own

        @pl.when(pl.program_id(0) == 0)
        def _():
            sums_ref[...] = jnp.zeros_like(sums_ref)
            if rider.start is not None:
                rider.start(r_in, r_out, r_sems)

        xv = x_ref[...]
        dhv = dh_ref[...].astype(F32)
        ngn = ngn_ref[...]
        r2 = _rms(xv)
        xhat = xv * r2
        dn_pre = dhv * (1.0 + sc_ref[...])
        u2 = dn_pre * ngn
        dxt = dxn_ref[...] + r2 * (u2 - xhat * jnp.mean(u2 * xhat, axis=-1, keepdims=True))
        dxo_ref[...] = dxt
        _acc_rows(sums_ref, [(0, dhv), (1, dhv * (xhat * ngn)), (2, dn_pre * xhat)])
        if has_y:
            _post_norm_bwd(dxt, y_ref[...].astype(F32), g_ref[...], ngp_ref[...], sums_ref, dy_ref)

        if rider.finish is not None:
            @pl.when(pl.program_id(0) == S // ts - 1)
            def _():
                _rider_end(rider, r_in, r_out, r_sems)

    sums_spec = pl.BlockSpec((8, D), lambda i: (0, 0))
    if has_y:
        operands = (dx_new, dh, x_new, y, gate, ng_post, ng_pre, sc)
        in_specs = [row, row, row, row, vec, vec, vec, vec]
        out_shape = [_sds((S, D), F32), _sds((S, D), BF16), _sds((8, D), F32)]
        out_specs = [row, row, sums_spec]
    else:
        operands = (dx_new, dh, x_new, ng_pre, sc)
        in_specs = [row, row, row, vec, vec]
        out_shape = [_sds((S, D), F32), _sds((8, D), F32)]
        out_specs = [row, sums_spec]
    res = pl.pallas_call(
        body, name=name, grid=(S // ts,), in_specs=in_specs + [ANY] * n_ri, out_specs=out_specs + [ANY] * n_ro,
        out_shape=out_shape + list(rider.out_shape), scratch_shapes=list(rider.sems),
        input_output_aliases={n_in + i: n_out + o for i, o in rider.aliases.items()},
        compiler_params=_cparams("arbitrary"))(*operands, *rider.operands)
    return (*res[:n_out], res[n_out:])


def _latent_fwd(lat, g_q, g_kv, rope_tabs, rank):
    S, W = lat.shape
    ts = min(256, S)
    tab = pl.BlockSpec((ts, LANES), lambda i: (i, 0))

    def body(lat_ref, gq_ref, gkv_ref, cos_ref, slo_ref, shi_ref, cq_ref, ckv_ref, kr_ref):
        lq = lat_ref[:, 0:rank]
        lkv = lat_ref[:, rank:2 * rank]
        cq_ref[...] = (lq * _rms(lq) * gq_ref[...]).astype(BF16)
        ckv_ref[...] = (lkv * _rms(lkv) * gkv_ref[...]).astype(BF16)
        kr_ref[...] = _rope(lat_ref[:, 2 * rank:W], cos_ref[...], slo_ref[...], shi_ref[...]).astype(BF16)

    return pl.pallas_call(
        body, name="mla_latent_fwd", grid=(S // ts,),
        in_specs=[pl.BlockSpec((ts, W), lambda i: (i, 0)), _vec_spec(rank), _vec_spec(rank), tab, tab, tab],
        out_specs=[pl.BlockSpec((ts, rank), lambda i: (i, 0)), pl.BlockSpec((ts, rank), lambda i: (i, 0)), tab],
        out_shape=[_sds((S, rank), BF16), _sds((S, rank), BF16), _sds((S, LANES), BF16)],
        compiler_params=_cparams("parallel"))(lat, g_q, g_kv, *rope_tabs)


def _latent_bwd(lat, dcq, dckv, dkr, g_q, g_kv, rope_tabs, rank):
    S, W = lat.shape
    ts = min(256, S)
    tab = pl.BlockSpec((ts, LANES), lambda i: (i, 0))
    half = pl.BlockSpec((ts, rank), lambda i: (i, 0))

    def body(lat_ref, dcq_ref, dckv_ref, dkr_ref, gq_ref, gkv_ref, cos_ref, slo_ref, shi_ref, dlat_ref, sums_ref):
        @pl.when(pl.program_id(0) == 0)
        def _():
            sums_ref[...] = jnp.zeros_like(sums_ref)

        def norm_bwd(v, dn, g, r):
            rr = _rms(v)
            vhat = v * rr
            u = dn * g
            sums_ref[r:r + 1, :] += jnp.sum(dn * vhat, axis=0, keepdims=True)
            return rr * (u - vhat * jnp.mean(u * vhat, axis=-1, keepdims=True))

        dlat_ref[:, 0:rank] = norm_bwd(lat_ref[:, 0:rank], dcq_ref[...], gq_ref[...], 0).astype(BF16)
        dlat_ref[:, rank:2 * rank] = norm_bwd(lat_ref[:, rank:2 * rank], dckv_ref[...], gkv_ref[...], 1).astype(BF16)
        dlat_ref[:, 2 * rank:W] = _rope_t(dkr_ref[...], cos_ref[...], slo_ref[...], shi_ref[...]).astype(BF16)

    return pl.pallas_call(
        body, name="mla_latent_bwd", grid=(S // ts,),
        in_specs=[pl.BlockSpec((ts, W), lambda i: (i, 0)), half, half, tab, _vec_spec(rank), _vec_spec(rank),
                  tab, tab, tab],
        out_specs=[pl.BlockSpec((ts, W), lambda i: (i, 0)), pl.BlockSpec((8, rank), lambda i: (0, 0))],
        out_shape=[_sds((S, W), BF16), _sds((8, rank), F32)],
        compiler_params=_cparams("arbitrary"))(lat, dcq, dckv, dkr, g_q, g_kv, *rope_tabs)


def _attn_tiles(S):
    t = min(512, S)
    return t, S // t


def _causal_mask(t):
    return lax.broadcasted_iota(jnp.int32, (t, t), 1) <= lax.broadcasted_iota(jnp.int32, (t, t), 0)


def _causal_pairs(nb, q_major):
    if q_major:
        pairs = [(qi, ki) for qi in range(nb) for ki in range(qi + 1)]
    else:
        pairs = [(qi, ki) for ki in range(nb) for qi in range(ki, nb)]
    return jnp.array([p[0] for p in pairs], jnp.int32), jnp.array([p[1] for p in pairs], jnp.int32), len(pairs)


def _heads_per_step(heads):
    return 2 if heads % 2 == 0 else 1


def _attn_fwd_tri(q, kv, kr, heads, scale, rider=NO_RIDER):
    S = q.shape[0]
    t, nb = _attn_tiles(S)
    G = _heads_per_step(heads)
    q_tab, k_tab, n_pairs = _causal_pairs(nb, True)
    n_ri, n_ro = len(rider.operands), len(rider.out_shape)

    def body(qt_ref, kt_ref, *refs):
        q_ref, kv_ref, kr_ref = refs[:3]
        r_in = refs[3:3 + n_ri]
        o_ref, lse_ref = refs[3 + n_ri:5 + n_ri]
        r_out = refs[5 + n_ri:5 + n_ri + n_ro]
        m_scr, acc_scr = refs[5 + n_ri + n_ro:7 + n_ri + n_ro]
        r_sems = refs[7 + n_ri + n_ro:]
        h, p = pl.program_id(0), pl.program_id(1)
        qi, ki = qt_ref[p], kt_ref[p]

        if rider.start is not None:
            @pl.when((h == 0) & (p == 0))
            def _():
                rider.start(r_in, r_out, r_sems)

        @pl.when(ki == 0)
        def _():
            m_scr[...] = jnp.full_like(m_scr, NEG_INF)
            acc_scr[...] = jnp.zeros_like(acc_scr)

        def step(diagonal):
            ones = jnp.ones((t, LANES), BF16)
            for g in range(G):
                kcat = jnp.concatenate([kv_ref[:, g * QK_PAD:g * QK_PAD + QK_NOPE], kr_ref[...]], axis=1)
                vext = jnp.concatenate([kv_ref[:, g * QK_PAD + QK_NOPE:(g + 1) * QK_PAD], ones], axis=1)
                s = lax.dot_general(q_ref[:, g * QK_PAD:(g + 1) * QK_PAD], kcat, DIMS_NT,
                                    preferred_element_type=F32)
                if diagonal:
                    s = jnp.where(_causal_mask(t), s, NEG_INF)
                m_prev = m_scr[g]
                m_new = jnp.maximum(m_prev, jnp.max(s, axis=-1, keepdims=True))
                alpha = jnp.exp(m_prev - m_new)
                pr = jnp.exp(s - jnp.tile(m_new, (1, t // LANES)))
                acc_scr[g] = jnp.tile(alpha, (1, 2)) * acc_scr[g] + lax.dot_general(
                    pr.astype(BF16), vext, DIMS_NN, preferred_element_type=F32)
                m_scr[g] = m_new

        @pl.when(ki < qi)
        def _():
            step(False)

        @pl.when(ki == qi)
        def _():
            step(True)
            for g in range(G):
                acc = acc_scr[g]
                o_ref[:, g * V_HEAD:(g + 1) * V_HEAD] = (acc[:, 0:V_HEAD] / acc[:, V_HEAD:2 * V_HEAD]).astype(BF16)
                lse_ref[g] = m_scr[g] + jnp.log(acc[:, V_HEAD:2 * V_HEAD])

        if rider.finish is not None:
            halfway = rider.mid is not None and heads // G >= 2
            if halfway:
                @pl.when((h == heads // G // 2) & (p == 0))
                def _():
                    rider.mid(r_in, r_out, r_sems)

            @pl.when((h == heads // G - 1) & (p == n_pairs - 1))
            def _():
                if halfway:
                    rider.finish(r_in, r_out, r_sems)
                else:
                    _rider_end(rider, r_in, r_out, r_sems)

    res = pl.pallas_call(
        body, name="mla_attn_fwd",
        grid_spec=pltpu.PrefetchScalarGridSpec(
            num_scalar_prefetch=2, grid=(heads // G, n_pairs),
            in_specs=[pl.BlockSpec((t, G * QK_PAD), lambda h, p, qt, kt: (qt[p], h)),
                      pl.BlockSpec((t, G * QK_PAD), lambda h, p, qt, kt: (kt[p], h)),
                      pl.BlockSpec((t, LANES), lambda h, p, qt, kt: (kt[p], 0))] + [ANY] * n_ri,
            out_specs=[pl.BlockSpec((t, G * V_HEAD), lambda h, p, qt, kt: (qt[p], h)),
                       pl.BlockSpec((G, t, LANES), lambda h, p, qt, kt: (h, qt[p], 0))] + [ANY] * n_ro,
            scratch_shapes=[pltpu.VMEM((G, t, LANES), F32), pltpu.VMEM((G, t, 2 * V_HEAD), F32)] + list(rider.sems)),
        out_shape=[_sds((S, heads * V_HEAD), BF16), _sds((heads, S, LANES), F32)] + list(rider.out_shape),
        input_output_aliases={5 + i: 2 + o for i, o in rider.aliases.items()},
        compiler_params=_cparams("arbitrary", "arbitrary"))(q_tab, k_tab, q, kv, kr, *rider.operands)
    return res[0], res[1], res[2:]


def _attn_bwd_tri(q, kv, kr, o, do, lse, rope_tabs, heads, scale, rider=NO_RIDER):
    S = q.shape[0]
    t, nb = _attn_tiles(S)
    G = _heads_per_step(heads)
    q_tab, k_tab, n_pairs = _causal_pairs(nb, False)
    n_ri, n_ro = len(rider.operands), len(rider.out_shape)
    rep = t // LANES

    tabs = jnp.concatenate(rope_tabs, axis=1)

    def body(qt_ref, kt_ref, *refs):
        q_ref, kv_ref, kr_ref, o_ref, do_ref, lse_ref, tabs_ref = refs[:7]
        cos_ref, slo_ref, shi_ref = (tabs_ref.at[:, pl.ds(i * LANES, LANES)] for i in range(3))
        r_in = refs[7:7 + n_ri]
        dq_ref, dkv_ref, dkr_ref = refs[7 + n_ri:10 + n_ri]
        r_out = refs[10 + n_ri:10 + n_ri + n_ro]
        dq_scr, dk_scr, dv_scr, dkr_scr, delta_scr = refs[10 + n_ri + n_ro:15 + n_ri + n_ro]
        r_sems = refs[15 + n_ri + n_ro:]
        h, p = pl.program_id(0), pl.program_id(1)
        qi, ki = qt_ref[p], kt_ref[p]
        q_rows = pl.ds(pl.multiple_of(qi * t, t), t)
        k_rows = pl.ds(pl.multiple_of(ki * t, t), t)

        @pl.when(ki == 0)
        def _():
            for g in range(G):
                cols = slice(g * V_HEAD, (g + 1) * V_HEAD)
                d = jnp.sum(do_ref[:, cols].astype(F32) * o_ref[:, cols].astype(F32), axis=-1, keepdims=True)
                delta_scr[g, q_rows, :] = jnp.broadcast_to(d, (t, LANES))

        if rider.start is not None:
            @pl.when((h == 0) & (p == 0))
            def _():
                rider.start(r_in, r_out, r_sems)

        @pl.when(p == 0)
        def _():
            dq_scr[...] = jnp.zeros_like(dq_scr)

        @pl.when((h == 0) & (p == 0))
        def _():
            dkr_scr[...] = jnp.zeros_like(dkr_scr)

        @pl.when(qi == ki)
        def _():
            dk_scr[...] = jnp.zeros_like(dk_scr)
            dv_scr[...] = jnp.zeros_like(dv_scr)

        def step(diagonal):
            for g in range(G):
                qv = q_ref[:, g * QK_PAD:(g + 1) * QK_PAD]
                kcat = jnp.concatenate([kv_ref[:, g * QK_PAD:g * QK_PAD + QK_NOPE], kr_ref[...]], axis=1)
                s = lax.dot_general(qv, kcat, DIMS_NT, preferred_element_type=F32)
                pr = jnp.exp(s - jnp.tile(lse_ref[g], (1, rep)))
                if diagonal:
                    pr = jnp.where(_causal_mask(t), pr, 0.0)
                dov = do_ref[:, g * V_HEAD:(g + 1) * V_HEAD]
                dv_scr[g] += lax.dot_general(pr.astype(BF16), dov, DIMS_TN, preferred_element_type=F32)
                dp = lax.dot_general(dov, kv_ref[:, g * QK_PAD + QK_NOPE:(g + 1) * QK_PAD], DIMS_NT,
                                     preferred_element_type=F32)
                ds = (pr * (dp - jnp.tile(delta_scr[g, q_rows, :], (1, rep)))).astype(BF16)
                dk_scr[g] += lax.dot_general(ds, qv, DIMS_TN, preferred_element_type=F32)
                dq_scr[q_rows, g * QK_PAD:(g + 1) * QK_PAD] += lax.dot_general(ds, kcat, DIMS_NN,
                                                                               preferred_element_type=F32)

        @pl.when(qi > ki)
        def _():
            step(False)

        @pl.when(qi == ki)
        def _():
            step(True)
            for g in range(G):
                dqv = dq_scr[q_rows, g * QK_PAD:(g + 1) * QK_PAD] * scale
                dq_ref[q_rows, g * QK_PAD:(g + 1) * QK_PAD] = jnp.concatenate(
                    [dqv[:, 0:QK_NOPE], _rope_t(dqv[:, QK_NOPE:QK_PAD], cos_ref[...], slo_ref[...], shi_ref[...])],
                    axis=1).astype(BF16)

        @pl.when(qi == nb - 1)
        def _():
            for g in range(G):
                dkv_ref[:, g * QK_PAD:(g + 1) * QK_PAD] = jnp.concatenate(
                    [dk_scr[g][:, 0:QK_NOPE], dv_scr[g]], axis=1).astype(BF16)
                dkr_scr[k_rows, :] += dk_scr[g][:, QK_NOPE:QK_PAD]

        @pl.when((h == heads // G - 1) & (p == n_pairs - 1))
        def _():
            dkr_ref[...] = dkr_scr[...]
            if rider.finish is not None:
                _rider_end(rider, r_in, r_out, r_sems)

    q_blk = lambda w: pl.BlockSpec((t, G * w), lambda h, p, qt, kt: (qt[p], h))
    stat = pl.BlockSpec((G, t, LANES), lambda h, p, qt, kt: (h, qt[p], 0))
    tab = pl.BlockSpec((t, LANES), lambda h, p, qt, kt: (kt[p], 0))
    res = pl.pallas_call(
        body, name="mla_attn_bwd",
        grid_spec=pltpu.PrefetchScalarGridSpec(
            num_scalar_prefetch=2, grid=(heads // G, n_pairs),
            in_specs=[q_blk(QK_PAD),
                      pl.BlockSpec((t, G * QK_PAD), lambda h, p, qt, kt: (kt[p], h)),
                      tab, q_blk(V_HEAD), q_blk(V_HEAD), stat,
                      pl.BlockSpec((t, 3 * LANES), lambda h, p, qt, kt: (kt[p], 0))] + [ANY] * n_ri,
            out_specs=[pl.BlockSpec((S, G * QK_PAD), lambda h, p, qt, kt: (0, h)),
                       pl.BlockSpec((t, G * QK_PAD), lambda h, p, qt, kt: (kt[p], h)),
                       pl.BlockSpec((S, LANES), lambda h, p, qt, kt: (0, 0))] + [ANY] * n_ro,
            scratch_shapes=[pltpu.VMEM((S, G * QK_PAD), F32), pltpu.VMEM((G, t, QK_PAD), F32),
                            pltpu.VMEM((G, t, V_HEAD), F32), pltpu.VMEM((S, LANES), F32),
                            pltpu.VMEM((G, S, LANES), F32)] + list(rider.sems)),
        out_shape=[_sds((S, heads * QK_PAD), BF16), _sds((S, heads * QK_PAD), BF16), _sds((S, LANES), F32)]
        + list(rider.out_shape),
        input_output_aliases={9 + i: 3 + o for i, o in rider.aliases.items()},
        compiler_params=_cparams("arbitrary", "arbitrary"))(q_tab, k_tab, q, kv, kr, o, do, lse, tabs,
                                                            *rider.operands)
    return res[0], res[1], res[2], res[3:]


def _shift_down(z, n, rows):
    return jnp.where(rows >= n, pltpu.roll(z, n, 0), 0.0)


def _shift_up(z, n, rows, S):
    return jnp.where(rows < S - n, pltpu.roll(z, S - n, 0), 0.0)


def _conv_specs(S, tc):
    strip = lambda p: pl.BlockSpec((None, S, tc), lambda j: (p, 0, j))
    return strip(0), strip(1), strip(2), pl.BlockSpec((3, tc), lambda j: (0, j))


def _conv_fwd(proj3, w):
    _, S, D = proj3.shape
    tc = LANES

    def body(b_ref, c_ref, u_ref, w_ref, out_ref):
        z = c_ref[...].astype(F32) * u_ref[...].astype(F32)
        rows = lax.broadcasted_iota(jnp.int32, (S, tc), 0)
        zc = w_ref[0:1, :] * _shift_down(z, 2, rows) + w_ref[1:2, :] * _shift_down(z, 1, rows) + w_ref[2:3, :] * z
        out_ref[...] = (b_ref[...].astype(F32) * zc).astype(BF16)

    return pl.pallas_call(
        body, name="conv_fwd", grid=(D // tc,), in_specs=list(_conv_specs(S, tc)),
        out_specs=pl.BlockSpec((S, tc), lambda j: (0, j)), out_shape=_sds((S, D), BF16),
        compiler_params=_cparams("parallel"))(proj3, proj3, proj3, w)


def _conv_bwd(dbz, proj3, w):
    _, S, D = proj3.shape
    tc = LANES

    def body(d_ref, b_ref, c_ref, u_ref, w_ref, dp_ref, dw_ref):
        cv, uv, dv = c_ref[...].astype(F32), u_ref[...].astype(F32), d_ref[...].astype(F32)
        z = cv * uv
        rows = lax.broadcasted_iota(jnp.int32, (S, tc), 0)
        z1, z2 = _shift_down(z, 1, rows), _shift_down(z, 2, rows)
        zc = w_ref[0:1, :] * z2 + w_ref[1:2, :] * z1 + w_ref[2:3, :] * z
        dp_ref[0] = (dv * zc).astype(BF16)
        dzc = dv * b_ref[...].astype(F32)
        dz = w_ref[2:3, :] * dzc + w_ref[1:2, :] * _shift_up(dzc, 1, rows, S) + w_ref[0:1, :] * _shift_up(dzc, 2, rows, S)
        dp_ref[1] = (dz * uv).astype(BF16)
        dp_ref[2] = (dz * cv).astype(BF16)
        dw_ref[0:1, :] = jnp.sum(dzc * z2, axis=0, keepdims=True)
        dw_ref[1:2, :] = jnp.sum(dzc * z1, axis=0, keepdims=True)
        dw_ref[2:3, :] = jnp.sum(dzc * z, axis=0, keepdims=True)

    sb, sc_, su, sw = _conv_specs(S, tc)
    return pl.pallas_call(
        body, name="conv_bwd", grid=(D // tc,),
        in_specs=[pl.BlockSpec((S, tc), lambda j: (0, j)), sb, sc_, su, sw],
        out_specs=[pl.BlockSpec((3, S, tc), lambda j: (0, 0, j)), pl.BlockSpec((3, tc), lambda j: (0, j))],
        out_shape=[_sds((3, S, D), BF16), _sds((3, D), F32)],
        compiler_params=_cparams("parallel"))(dbz, proj3, proj3, proj3, w)


def _silu(c_all):
    def body(c_ref, o_ref):
        cv = c_ref[...]
        o_ref[...] = cv * (1.0 / (1.0 + jnp.exp(-cv)))

    vm = pl.BlockSpec(memory_space=pltpu.VMEM)
    return pl.pallas_call(body, name="cond_silu", in_specs=[vm], out_specs=vm, out_shape=_sds(c_all.shape, F32))(c_all)


def _mod_fwd(cond, w_mod, b_cols):
    L, D, ncol = w_mod.shape
    B = cond.shape[0]
    tk, tn = min(512, D), min(1024, ncol)
    nk = D // tk

    def body(c_ref, w_ref, b_ref, out_ref, acc):
        kk = pl.program_id(2)
        part = lax.dot_general(c_ref[...].astype(BF16), w_ref[...].astype(BF16), DIMS_NN, preferred_element_type=F32)

        @pl.when(kk == 0)
        def _():
            acc[...] = part

        @pl.when(kk > 0)
        def _():
            acc[...] += part

        @pl.when(kk == nk - 1)
        def _():
            out_ref[...] = acc[...] + b_ref[...]

    return pl.pallas_call(
        body, name="mod_fwd", grid=(L, ncol // tn, nk),
        in_specs=[pl.BlockSpec((B, tk), lambda l, j, k: (0, k)),
                  pl.BlockSpec((None, tk, tn), lambda l, j, k: (l, k, j)),
                  pl.BlockSpec((None, 1, tn), lambda l, j, k: (l, 0, j))],
        out_specs=pl.BlockSpec((None, B, tn), lambda l, j, k: (l, 0, j)),
        out_shape=_sds((L, B, ncol), F32),
        scratch_shapes=[pltpu.VMEM((B, tn), F32)],
        compiler_params=_cparams("parallel", "parallel", "arbitrary"))(cond, w_mod, b_cols)


def _adamw_math(w, g, m, v):
    m = ADAM_B1 * m + (1.0 - ADAM_B1) * g
    v = ADAM_B2 * v + (1.0 - ADAM_B2) * (g * g)
    m_hat = m / (1.0 - ADAM_B1 ** ADAM_STEP)
    v_hat = v / (1.0 - ADAM_B2 ** ADAM_STEP)
    delta = -ADAM_LR * (m_hat / (jnp.sqrt(v_hat) + ADAM_EPS) + ADAM_WD * w)
    return delta, m, v


def _adamw(name, w, g, m, v, emit_grad=False):
    shape = w.shape
    cols = shape[-1] if w.ndim <= 3 else shape[-2] * shape[-1]
    rows = w.size // cols
    w2, g2, m2, v2 = (t.reshape(rows, cols) for t in (w, g, m, v))
    tr = _row_tile(rows, cols * 4, limit=2 * 1024 * 1024, mult=8)
    spec = pl.BlockSpec((tr, cols), lambda i: (i, 0))
    n_out = 4 if emit_grad else 3

    def body(w_ref, g_ref, m_ref, v_ref, d_ref, nm_ref, nv_ref, *rest):
        gv = g_ref[...]
        d, nm, nv = _adamw_math(w_ref[...], gv, m_ref[...], v_ref[...])
        d_ref[...] = d
        nm_ref[...] = nm
        nv_ref[...] = nv
        if emit_grad:
            rest[0][...] = gv

    outs = pl.pallas_call(body, name=name, grid=(rows // tr,), in_specs=[spec] * 4, out_specs=[spec] * n_out,
                          out_shape=[_sds((rows, cols), F32)] * n_out, compiler_params=_cparams("parallel"))(w2, g2, m2, v2)
    return tuple(t.reshape(shape) for t in outs)


def _adamw_mod(w, cond_t, dmod_cols, m, v, rider=NO_RIDER):
    L, D, ncol = w.shape
    B = cond_t.shape[1]
    tr, tc = min(256, D), min(1024, ncol)
    blk = pl.BlockSpec((None, tr, tc), lambda l, i, j: (l, i, j))
    grid = (L, D // tr, ncol // tc)
    n_ri, n_ro = len(rider.operands), len(rider.out_shape)

    def body(*refs):
        w_ref, ct_ref, dm_ref, m_ref, v_ref = refs[:5]
        r_in = refs[5:5 + n_ri]
        g_ref, d_ref, nm_ref, nv_ref = refs[5 + n_ri:9 + n_ri]
        r_out = refs[9 + n_ri:9 + n_ri + n_ro]
        r_sems = refs[9 + n_ri + n_ro:]
        ids = [pl.program_id(a) for a in range(3)]

        if rider.start is not None:
            @pl.when((ids[0] == 0) & (ids[1] == 0) & (ids[2] == 0))
            def _():
                rider.start(r_in, r_out, r_sems)

        g = lax.dot_general(ct_ref[...], dm_ref[...], DIMS_NN, precision=lax.Precision.HIGHEST,
                            preferred_element_type=F32)
        d, nm, nv = _adamw_math(w_ref[...], g, m_ref[...], v_ref[...])
        g_ref[...] = g
        d_ref[...] = d
        nm_ref[...] = nm
        nv_ref[...] = nv

        if rider.finish is not None:
            @pl.when((ids[0] == grid[0] - 1) & (ids[1] == grid[1] - 1) & (ids[2] == grid[2] - 1))
            def _():
                _rider_end(rider, r_in, r_out, r_sems)

    hosted = rider.start is not None
    res = pl.pallas_call(
        body, name="adamw_w_mod", grid=grid,
        in_specs=[blk, pl.BlockSpec((tr, B), lambda l, i, j: (i, 0)),
                  pl.BlockSpec((None, B, tc), lambda l, i, j: (l, 0, j)), blk, blk] + [ANY] * n_ri,
        out_specs=[blk] * 4 + [ANY] * n_ro, out_shape=[_sds((L, D, ncol), F32)] * 4 + list(rider.out_shape),
        scratch_shapes=list(rider.sems), input_output_aliases={5 + i: 4 + o for i, o in rider.aliases.items()},
        compiler_params=_cparams(*(("arbitrary",) * 3 if hosted else ("parallel",) * 3)))(
            w, cond_t, dmod_cols, m, v, *rider.operands)
    return (*res[:4], res[4:])


def _cast_into_full(name, ws, kinds, k_idx, rider=NO_RIDER):
    L, R, C = ws[0].shape
    assert all(w.shape == (L, R, C) for w in ws)
    n = len(ws)
    Rh = R // 2
    tr = _row_tile(Rh, C * 4)
    grid = (L, 2, Rh // tr)
    out_shape, out_specs = [], []
    for kind in kinds:
        if kind == "row":
            out_shape.append(_sds((L, N_CHIPS, 2, Rh, C), BF16))
            out_specs.append(pl.BlockSpec((None, None, None, tr, C), lambda l, h, i, k_ref: (l, k_ref[0], h, i, 0)))
        else:
            out_shape.append(_sds((L, 2, Rh, N_CHIPS * C), BF16))
            out_specs.append(pl.BlockSpec((None, None, tr, C), lambda l, h, i, k_ref: (l, h, i, k_ref[0])))
    n_ri, n_ro = len(rider.operands), len(rider.out_shape)

    def body(k_ref, *refs):
        r_in = refs[n:n + n_ri]
        r_out = refs[2 * n + n_ri:2 * n + n_ri + n_ro]
        r_sems = refs[2 * n + n_ri + n_ro:]
        ids = [pl.program_id(a) for a in range(3)]
        if rider.start is not None:
            @pl.when((ids[0] == 0) & (ids[1] == 0) & (ids[2] == 0))
            def _():
                rider.start(r_in, r_out, r_sems)
        for a in range(n):
            refs[n + n_ri + a][...] = refs[a][...].astype(BF16)
        if rider.finish is not None:
            @pl.when((ids[0] == grid[0] - 1) & (ids[1] == grid[1] - 1) & (ids[2] == grid[2] - 1))
            def _():
                _rider_end(rider, r_in, r_out, r_sems)

    hosted = rider.start is not None
    res = pl.pallas_call(
        body, name=name,
        grid_spec=pltpu.PrefetchScalarGridSpec(
            num_scalar_prefetch=1, grid=grid,
            in_specs=[pl.BlockSpec((None, None, tr, C), lambda l, h, i, k_ref: (l, h, i, 0))] * n + [ANY] * n_ri,
            out_specs=out_specs + [ANY] * n_ro, scratch_shapes=list(rider.sems)),
        out_shape=out_shape + list(rider.out_shape),
        input_output_aliases={1 + n + i: n + o for i, o in rider.aliases.items()},
        compiler_params=_cparams(*(("arbitrary",) * 3 if hosted else ("parallel",) * 3)))(
            k_idx, *[w.reshape(L, 2, Rh, C) for w in ws], *rider.operands)
    return res[:n], res[n:]


def _pair_sum(name, g5, ra, c_idx):
    L, A, _, Rh, Cc = g5.shape
    tr = _row_tile(Rh, Cc * 4)

    def body(c_ref, g_ref, r_ref, o_ref):
        o_ref[...] = (g_ref[...].astype(F32) + r_ref[...].astype(F32)).astype(BF16)

    blk = pl.BlockSpec((None, None, tr, Cc), lambda l, a, i, c_ref: (l, a, i, 0))
    return pl.pallas_call(
        body, name=name,
        grid_spec=pltpu.PrefetchScalarGridSpec(
            num_scalar_prefetch=1, grid=(L, A, Rh // tr),
            in_specs=[pl.BlockSpec((None, None, None, tr, Cc), lambda l, a, i, c_ref: (l, a, c_ref[0], i, 0)), blk],
            out_specs=blk),
        out_shape=_sds((L, A, Rh, Cc), BF16),
        compiler_params=_cparams("parallel", "parallel", "parallel"))(c_idx, g5, ra)


def _chip_sum(name, p, rb, kc_idx, kind, layer=0, n_layers=1, prev=None):
    _, A, Rh, Cc = p.shape
    C = rb.shape[-1]
    tr = _row_tile(Rh, C * 4)
    if kind == "row":
        own = pl.BlockSpec((None, None, tr, C), lambda i, kc: (0, kc[0], i, 0))
    else:
        own = pl.BlockSpec((None, None, tr, C), lambda i, kc: (0, 0, i, kc[0]))
    peer = lambda j: pl.BlockSpec((None, None, tr, C), lambda i, kc: (j, 0, i, 0))

    def body(kc_ref, p_ref, r0_ref, r1_ref, r2_ref, *rest):
        o_ref = rest[-1]
        o_ref[...] = ((p_ref[...].astype(F32) + r0_ref[...].astype(F32)) + r1_ref[...].astype(F32)) + r2_ref[...].astype(F32)

    operands = [kc_idx, p, rb, rb, rb] + ([prev] if prev is not None else [])
    return pl.pallas_call(
        body, name=name,
        grid_spec=pltpu.PrefetchScalarGridSpec(
            num_scalar_prefetch=1, grid=(Rh // tr,),
            in_specs=[own, peer(0), peer(1), peer(2)] + ([ANY] if prev is not None else []),
            out_specs=pl.BlockSpec((None, None, tr, C), lambda i, kc: (layer, kc[1], i, 0))),
        out_shape=_sds((n_layers, 2, Rh, C), F32),
        input_output_aliases={5: 0} if prev is not None else {},
        compiler_params=_cparams("parallel"))(*operands)


def _mesh_place():
    x, y, c = lax.axis_index("x"), lax.axis_index("y"), lax.axis_index("c")
    chips = [(1 - x, y), (x, 1 - y), (1 - x, 1 - y)]
    return x, y, c, chips


def _remote(src, dst, send_sem, recv_sem, to):
    return pltpu.make_async_remote_copy(src_ref=src, dst_ref=dst, send_sem=send_sem, recv_sem=recv_sem,
                                        device_id=to, device_id_type=MESH_ID)


def _small_allgather(name, v, with_sum=False, rider=NO_RIDER):
    R, N = v.shape
    n_ri, n_ro, n_own = len(rider.operands), len(rider.out_shape), 2 if with_sum else 1

    def body(*refs):
        r_in = refs[1:1 + n_ri]
        r_out = refs[1 + n_ri + n_own:1 + n_ri + n_own + n_ro]
        r_sems = refs[1 + n_ri + n_own + n_ro + 3:]
        own = (refs[0],) + refs[1 + n_ri:1 + n_ri + n_own] + refs[1 + n_ri + n_own + n_ro:1 + n_ri + n_own + n_ro + 3]
        if with_sum:
            x_ref, out_ref, sum_ref, send_sems, recv_sems, local_sem = own
        else:
            x_ref, out_ref, send_sems, recv_sems, local_sem = own
        if rider.start is not None:
            rider.start(r_in, r_out, r_sems)
        x, y, c, chips = _mesh_place()
        me, sibling = (x, y, c), (x, y, 1 - c)

        def rows(px, py, pc):
            return out_ref.at[pl.ds((4 * px + 2 * py + pc) * R, R), :]

        def copy(k, block, to, src=None):
            return _remote(rows(*block) if src is None else src, rows(*block), send_sems.at[k], recv_sems.at[k], to)

        mine = pltpu.make_async_copy(x_ref, rows(*me), local_sem)
        mine.start()
        first = [copy(0, me, sibling, src=x_ref)]
        first += [copy(1 + j, me, (*chip, c), src=x_ref) for j, chip in enumerate(chips)]
        for cp in first:
            cp.start()
        passed = [copy(4 + j, (*chip, c), sibling) for j, chip in enumerate(chips)]
        for j, chip in enumerate(chips):
            copy(1 + j, (*chip, c), me).wait_recv()
            passed[j].start()
        copy(0, sibling, me).wait_recv()
        for j, chip in enumerate(chips):
            copy(4 + j, (*chip, 1 - c), me).wait_recv()
        for cp in first + passed:
            cp.wait_send()
        mine.wait()
        if with_sum:
            total = out_ref[0:R, :]
            for p in range(1, 8):
                total = total + out_ref[p * R:(p + 1) * R, :]
            sum_ref[...] = total
        if rider.finish is not None:
            _rider_end(rider, r_in, r_out, r_sems)

    vm = pl.BlockSpec(memory_space=pltpu.VMEM)
    out_shape = [_sds((8 * R, N), F32)] + ([_sds((R, N), F32)] if with_sum else [])
    res = pl.pallas_call(
        body, name=name, out_shape=out_shape + list(rider.out_shape), in_specs=[vm] + [ANY] * n_ri,
        out_specs=[vm] * n_own + [ANY] * n_ro,
        scratch_shapes=[pltpu.SemaphoreType.DMA((7,)), pltpu.SemaphoreType.DMA((7,)), pltpu.SemaphoreType.DMA]
        + list(rider.sems),
        input_output_aliases={1 + i: n_own + o for i, o in rider.aliases.items()},
        compiler_params=pltpu.CompilerParams(vmem_limit_bytes=VMEM_LIMIT_BYTES))(v, *rider.operands)
    if rider.start is not None:
        return (*res[:n_own], res[n_own:])
    return res if with_sum else res[0]


def _full_place(ref, kind, C, kk, half, layer=None):
    lead = slice(None) if layer is None else pl.ds(layer, 1)
    if kind == "row":
        return ref.at[lead, kk, half]
    return ref.at[lead, half, :, pl.ds(pl.multiple_of(kk * C, LANES), C)]


def _gather_rider(fulls, kinds, shard_cols, layers=None, peers=(0, 1, 2)):
    n = len(fulls)
    layers = layers or [None] * n
    rows = [f.shape[3] if kind == "row" else f.shape[2] for f, kind in zip(fulls, kinds)]
    n_chunks = 2 if all(r % 32 == 0 for r in rows) else 1

    def copies(outs, sems):
        x, y, c, chips = _mesh_place()
        k = 2 * x + y

        def place(a, kk, half, ch):
            rc = rows[a] // n_chunks
            return _full_place(outs[a], kinds[a], shard_cols[a], kk, half, layers[a]).at[:, pl.ds(ch * rc, rc), :]

        def copy(a, j, ch, ref, to):
            s = 6 * (n_chunks * a + ch) + j
            return _remote(ref, ref, sems[0].at[s], sems[1].at[s], to)

        return (x, y, c), [(j, chip) for j, chip in enumerate(chips) if j in peers], k, place, copy

    def start(_, outs, sems):
        (x, y, c), chips, k, place, copy = copies(outs, sems)
        for ch in range(n_chunks):
            for j, chip in chips:
                for a in range(n):
                    copy(a, j, ch, place(a, k, c, ch), (*chip, c)).start()

    def pass_on(outs, sems, ch):
        (x, y, c), chips, k, place, copy = copies(outs, sems)
        for j, chip in chips:
            kj = 2 * chip[0] + chip[1]
            for a in range(n):
                copy(a, j, ch, place(a, kj, c, ch), (x, y, c)).wait_recv()
                copy(a, 3 + j, ch, place(a, kj, c, ch), (x, y, 1 - c)).start()

    def mid(_, outs, sems):
        pass_on(outs, sems, 0)

    def finish(_, outs, sems):
        pass_on(outs, sems, n_chunks - 1)
        (x, y, c), chips, k, place, copy = copies(outs, sems)
        for ch in range(n_chunks):
            for j, chip in chips:
                kj = 2 * chip[0] + chip[1]
                for a in range(n):
                    copy(a, 3 + j, ch, place(a, kj, 1 - c, ch), (x, y, c)).wait_recv()
        for ch in range(n_chunks):
            for j, chip in chips:
                kj = 2 * chip[0] + chip[1]
                for a in range(n):
                    copy(a, j, ch, place(a, k, c, ch), (*chip, c)).wait_send()
                    copy(a, 3 + j, ch, place(a, kj, c, ch), (x, y, 1 - c)).wait_send()

    n_sems = 6 * n * n_chunks
    return Rider(tuple(fulls), tuple(_sds(f.shape, BF16) for f in fulls), {a: a for a in range(n)},
                 (pltpu.SemaphoreType.DMA((n_sems,)), pltpu.SemaphoreType.DMA((n_sems,))), start, finish,
                 mid if n_chunks == 2 else None)


def _scatter_rider(ps, kinds, shard_cols, peers=(0, 1, 2), into=None):
    n = len(ps)

    def copies(ins, outs, sems):
        x, y, c, chips = _mesh_place()
        cps = []
        for j, chip in enumerate(chips):
            if j not in peers:
                continue
            kj = 2 * chip[0] + chip[1]
            for a in range(n):
                C = shard_cols[a]
                src = ins[a].at[:, kj] if kinds[a] == "row" else ins[a].at[:, 0, :, pl.ds(pl.multiple_of(kj * C, LANES), C)]
                cps.append(_remote(src, outs[a].at[j], sems[0].at[3 * a + j], sems[1].at[3 * a + j], (*chip, c)))
        return cps

    def start(ins, outs, sems):
        for cp in copies(ins, outs, sems):
            cp.start()

    def finish(ins, outs, sems):
        cps = copies(ins, outs, sems)
        for cp in cps:
            cp.wait_recv()
        for cp in cps:
            cp.wait_send()

    out_shape = tuple(_sds((3, p.shape[0], p.shape[2], C), BF16) for p, C in zip(ps, shard_cols))
    aliases = {n + a: a for a in range(n)} if into is not None else {}
    return Rider(tuple(ps) + tuple(into or ()), out_shape, aliases,
                 (pltpu.SemaphoreType.DMA((3 * n,)), pltpu.SemaphoreType.DMA((3 * n,))), start, finish)


def _run_rider(name, rider):
    n_in, n_out = len(rider.operands), len(rider.out_shape)

    def body(*refs):
        ins, outs, sems = refs[:n_in], refs[n_in:n_in + n_out], refs[n_in + n_out:]
        rider.start(ins, outs, sems)
        _rider_end(rider, ins, outs, sems)

    return pl.pallas_call(
        body, name=name, out_shape=list(rider.out_shape), in_specs=[ANY] * n_in, out_specs=[ANY] * n_out,
        input_output_aliases=dict(rider.aliases), scratch_shapes=list(rider.sems),
        compiler_params=pltpu.CompilerParams(vmem_limit_bytes=VMEM_LIMIT_BYTES))(*rider.operands)


def _exchange_rider(g5s):
    n = len(g5s)

    def copies(ins, outs, sems):
        x, y, c, _ = _mesh_place()
        return [_remote(ins[a].at[:, :, 1 - c], outs[a], sems[0].at[a], sems[1].at[a], (x, y, 1 - c)) for a in range(n)]

    def start(ins, outs, sems):
        for cp in copies(ins, outs, sems):
            cp.start()

    def finish(ins, outs, sems):
        cps = copies(ins, outs, sems)
        for cp in cps:
            cp.wait_recv()
        for cp in cps:
            cp.wait_send()

    out_shape = tuple(_sds((g.shape[0], g.shape[1], g.shape[3], g.shape[4]), BF16) for g in g5s)
    return Rider(tuple(g5s), out_shape, {}, (pltpu.SemaphoreType.DMA((n,)), pltpu.SemaphoreType.DMA((n,))), start, finish)


def _share_rider(fs):
    n = len(fs)

    def start(_, outs, sems):
        x, y, c, _p = _mesh_place()
        for a in range(n):
            mine = outs[a].at[:, c]
            _remote(mine, mine, sems[0].at[a], sems[1].at[a], (x, y, 1 - c)).start()

    def finish(_, outs, sems):
        x, y, c, _p = _mesh_place()
        for a in range(n):
            theirs = outs[a].at[:, 1 - c]
            _remote(theirs, theirs, sems[0].at[a], sems[1].at[a], (x, y, c)).wait_recv()
        for a in range(n):
            mine = outs[a].at[:, c]
            _remote(mine, mine, sems[0].at[a], sems[1].at[a], (x, y, 1 - c)).wait_send()

    return Rider(tuple(fs), tuple(_sds(f.shape, F32) for f in fs), {a: a for a in range(n)},
                 (pltpu.SemaphoreType.DMA((n,)), pltpu.SemaphoreType.DMA((n,))), start, finish)


def _both_riders(r1, r2):
    ni, no, ns = len(r1.operands), len(r1.out_shape), len(r1.sems)
    aliases = dict(r1.aliases)
    aliases.update({ni + i: no + o for i, o in r2.aliases.items()})

    def start(ins, outs, sems):
        r1.start(ins[:ni], outs[:no], sems[:ns])
        r2.start(ins[ni:], outs[no:], sems[ns:])

    def finish(ins, outs, sems):
        _rider_end(r1, ins[:ni], outs[:no], sems[:ns])
        _rider_end(r2, ins[ni:], outs[no:], sems[ns:])

    return Rider(r1.operands + r2.operands, r1.out_shape + r2.out_shape, aliases, r1.sems + r2.sems, start, finish)


def _pack_rows(parts, lane_mult=1024):
    flat = jnp.concatenate([p.reshape(-1).astype(F32) for p in parts])
    n = -(-flat.shape[0] // (8 * lane_mult)) * lane_mult
    return jnp.pad(flat, (0, 8 * n - flat.shape[0])).reshape(8, n)


def _relu2(acc):
    r = jnp.maximum(acc, 0.0)
    return r, r * r


def _times_2r(acc, r):
    return (acc * (2.0 * r.astype(F32)),)


def kernel(x, c, positions, w_mod, b_mod, norm_g, mla_w_in, mla_g_q, mla_g_kv, mla_w_uq, mla_w_ukv, mla_w_o, conv_w_in, conv_w, conv_w_out, mlp_w_up, mlp_w_down, loss_target, m_w_mod, m_b_mod, m_norm_g, m_mla_w_in, m_mla_g_q, m_mla_g_kv, m_mla_w_uq, m_mla_w_ukv, m_mla_w_o, m_conv_w_in, m_conv_w, m_conv_w_out, m_mlp_w_up, m_mlp_w_down, v_w_mod, v_b_mod, v_norm_g, v_mla_w_in, v_mla_g_q, v_mla_g_kv, v_mla_w_uq, v_mla_w_ukv, v_mla_w_o, v_conv_w_in, v_conv_w, v_conv_w_out, v_mlp_w_up, v_mlp_w_down):
    S, D = x.shape[1], x.shape[2]
    Dq = D // N_CHIPS
    ncol = w_mod.shape[2]
    n_mod = N_CHIPS * ncol // D
    F = mlp_w_up.shape[2] * N_CHIPS
    lat_dim = mla_w_in.shape[2]
    rank = mla_g_q.shape[1]
    H = mla_w_uq.shape[2]
    d_qk = mla_w_uq.shape[3]
    assert mla_g_kv.shape[1] == rank and lat_dim == 2 * rank + QK_ROPE and d_qk == QK_NOPE + QK_ROPE
    assert mla_w_ukv.shape[3] == QK_NOPE + V_HEAD and x.shape[0] == 1 and n_mod == 6
    assert norm_g.shape[0] == 2 and mla_w_in.shape[0] == 1 and conv_w_in.shape[0] == 1
    lat_pad = 2 * rank + LANES
    scale = float(d_qk) ** -0.5

    xi, yi, ci = lax.axis_index("x"), lax.axis_index("y"), lax.axis_index("c")
    chip = 2 * xi + yi
    dev = 2 * chip + ci
    c_idx = jnp.reshape(ci, (1,)).astype(jnp.int32)
    k_idx = jnp.reshape(chip, (1,)).astype(jnp.int32)

    n1 = D + 2 * D + 3 * Dq
    g1 = _small_allgather("gather_small_inputs", _pack_rows([c, norm_g, conv_w])).reshape(8, -1)
    c_all = g1[:, :D]
    by_chip = g1[0::2]
    norm_full = jnp.concatenate([by_chip[kk, D:3 * D].reshape(2, 4, Dq) for kk in range(N_CHIPS)], axis=-1)
    convw_full = jnp.concatenate([by_chip[kk, 3 * D:n1].reshape(3, Dq) for kk in range(N_CHIPS)], axis=-1)

    b_cols = lax.dynamic_slice(b_mod, (0, chip * ncol), (2, ncol)).reshape(2, 1, ncol)
    cond_all = _silu(c_all)
    mod_cols = _mod_fwd(cond_all, w_mod, b_cols)
    g2 = _small_allgather("gather_mod", _pack_rows([mod_cols]))
    g2 = g2.reshape(8, -1)[0::2, :2 * 8 * ncol].reshape(N_CHIPS, 2, 8, ncol)
    mod_all = jnp.transpose(g2, (2, 1, 0, 3)).reshape(8, 2, n_mod * D)
    mod_me = lax.dynamic_index_in_dim(mod_all, dev, axis=0, keepdims=False)
    mods = [[mod_me[l, i * D:(i + 1) * D].reshape(1, D) for i in range(n_mod)] for l in range(2)]
    ng = [[norm_full[l, i].reshape(1, D) for i in range(4)] for l in range(2)]

    pos = positions[0].astype(F32)
    inv_freq = ROPE_THETA ** (-jnp.arange(0, QK_ROPE, 2, dtype=F32) / QK_ROPE)
    ang = pos[:, None] * inv_freq
    cos, sin = jnp.cos(ang), jnp.sin(ang)
    zero = jnp.zeros_like(cos)
    rope_tabs = (jnp.concatenate([cos, cos, zero, zero], axis=1),
                 jnp.concatenate([-sin, zero, zero, zero], axis=1),
                 jnp.concatenate([zero, sin, zero, zero], axis=1))

    weights = [("mla_w_in", mla_w_in, "row"), ("mla_w_uq", mla_w_uq.reshape(1, rank // N_CHIPS, H * d_qk), "row"),
               ("mla_w_ukv", mla_w_ukv.reshape(1, rank // N_CHIPS, H * QK_PAD), "row"), ("mla_w_o", mla_w_o, "row"),
               ("conv_w_in", conv_w_in, "col"), ("conv_w_out", conv_w_out, "row"),
               ("mlp_w_up", mlp_w_up, "col"), ("mlp_w_down", mlp_w_down, "row")]
    kinds = [k for _, _, k in weights]
    shard_shapes = [w.shape for _, w, _ in weights]
    shard_cols = [s[2] for s in shard_shapes]
    W_IN, W_UQ, W_UKV, W_O, W_CIN, W_COUT, W_UP, W_DOWN = range(8)
    mla_idx = [W_IN, W_UQ, W_UKV, W_O]
    casted = [_cast_into_full("cast_" + nm, [w], [kind], k_idx)[0][0] for nm, w, kind in weights[:W_UP]]

    def view(i, buf):
        L, R, C = shard_shapes[i]
        return buf.reshape((L, N_CHIPS * R, C) if kinds[i] == "row" else (L, R, N_CHIPS * C))

    NEIGHBOURS, DIAGONAL = (0, 1), (2,)

    def gather_of(bufs, idx, layers=None, peers=(0, 1, 2)):
        return _gather_rider(bufs, [kinds[i] for i in idx], [shard_cols[i] for i in idx], layers, peers)

    def scatter_of(ps, idx, peers=(0, 1, 2), into=None):
        return _scatter_rider(ps, [kinds[i] for i in idx], [shard_cols[i] for i in idx], peers, into)

    def halves(items):
        g5s = []
        for _, i, g in items:
            _, R, C = shard_shapes[i]
            g5s.append(g.reshape((1, N_CHIPS, 2, R // 2, C) if kinds[i] == "row" else (1, 1, 2, R // 2, N_CHIPS * C)))
        return g5s

    def pair_sums(items, g5s, ras):
        return [_pair_sum("pair_sum_" + nm, g5, ra, c_idx) for (nm, _, _), g5, ra in zip(items, g5s, ras)]

    mlp_casted, got = _cast_into_full("cast_mlp_w", [mlp_w_up, mlp_w_down], [kinds[W_UP], kinds[W_DOWN]], k_idx,
                                      gather_of([casted[W_IN]], [W_IN]))
    casted += list(mlp_casted)
    w_in_p = jnp.pad(view(W_IN, got[0])[0], ((0, 0), (0, lat_pad - lat_dim)))
    HV = H * V_HEAD

    def layer_b(l, transposed):
        if transposed:
            return lambda tm, tn, tk: pl.BlockSpec((None, tn, tk), lambda i, j, k: (l, j, k))
        return lambda tm, tn, tk: pl.BlockSpec((None, tk, tn), lambda i, j, k: (l, k, j))

    def mlp_up(tag, l, h, w, rider=NO_RIDER):
        return _mm("mlp_up_" + tag, h, w, "nn", S, F, D, [_sds((S, F), BF16)] * 2, epilogue=_relu2,
                   b_spec=layer_b(l, False), rider=rider)

    def mlp_down(tag, l, a2, w, rider=NO_RIDER):
        return _mm("mlp_down_" + tag, a2, w, "nn", S, D, F, [_sds((S, D), BF16)], b_spec=layer_b(l, False), rider=rider)

    def mlp_bwd(tag, l, h, r, a2, dy, first=NO_RIDER, plan=None):
        res = _mm("mlp_down_dx_" + tag, dy, w_down, "nt", S, F, D, [_sds((S, F), BF16)], epilogue=_times_2r,
                  b_spec=layer_b(l, True), rider=first,
                  extras=[(r, lambda tm, tn, tk: pl.BlockSpec((tm, tn), lambda i, j, k: (i, j)))])
        (da,), got_first = res if first.start is not None else (res, ())
        second, third, fourth = plan(got_first) if plan else (NO_RIDER,) * 3
        res = _mm("mlp_down_dw_" + tag, a2, dy, "tn", F, D, S, [_sds((F, D), BF16)], rider=second)
        (dw_down,), got_second = res if plan else (res, ())
        res = _mm("mlp_up_dx_" + tag, da, w_up, "nt", S, D, F, [_sds((S, D), BF16)], b_spec=layer_b(l, True), rider=third)
        (dh,), got_third = res if plan else (res, ())
        res = _mm("mlp_up_dw_" + tag, h, da, "tn", D, F, S, [_sds((D, F), BF16)], rider=fourth)
        (dw_up,), got_fourth = res if plan else (res, ())
        return dh, dw_up, dw_down, (got_first, got_second, got_third, got_fourth)

    x0 = x[0]
    sh1, sc1, gt1, sh2, sc2, gt2 = mods[0]
    (h1,) = _fwd_boundary("fwd_boundary_0", x0, None, None, None, ng[0][0], sc1, sh1)
    (lat,), (uq_buf, ukv_buf) = _mm("mla_in", h1, w_in_p, "nn", S, lat_pad, D, [_sds((S, lat_pad), F32)], tn=lat_pad,
                                    rider=gather_of([casted[W_UQ], casted[W_UKV]], [W_UQ, W_UKV]))
    w_q_p = jnp.pad(view(W_UQ, uq_buf)[0].reshape(rank, H, d_qk), ((0, 0), (0, 0), (0, QK_PAD - d_qk))).reshape(rank, H * QK_PAD)
    w_ukv = view(W_UKV, ukv_buf)[0]
    cq, ckv, kr = _latent_fwd(lat, mla_g_q, mla_g_kv, rope_tabs, rank)

    def rope_q(acc, cos_p, sin_lo, sin_hi):
        acc = acc * scale
        parts = []
        for hh in range(acc.shape[1] // QK_PAD):
            parts.append(acc[:, hh * QK_PAD:hh * QK_PAD + QK_NOPE])
            parts.append(_rope(acc[:, hh * QK_PAD + QK_NOPE:(hh + 1) * QK_PAD], cos_p, sin_lo, sin_hi))
        return (jnp.concatenate(parts, axis=1),)

    tab_extra = lambda tm, tn, tk: pl.BlockSpec((tm, LANES), lambda i, j, k: (i, 0))
    (q,), (o_buf,) = _mm("mla_q", cq, w_q_p, "nn", S, H * QK_PAD, rank, [_sds((S, H * QK_PAD), BF16)], epilogue=rope_q,
                         extras=[(t, tab_extra) for t in rope_tabs], tn=2 * QK_PAD,
                         rider=gather_of([casted[W_O]], [W_O]))
    w_o = view(W_O, o_buf)[0]
    (kv,) = _mm("mla_kv", ckv, w_ukv, "nn", S, H * QK_PAD, rank, [_sds((S, H * QK_PAD), BF16)])
    rest_idx = [W_UP, W_DOWN]
    o, lse, (up_buf, down_buf) = _attn_fwd_tri(
        q, kv, kr, H, scale, gather_of([casted[i] for i in rest_idx], rest_idx, [0, 0]))
    (y1,), (cout_buf,) = _mm("mla_out", o, w_o, "nn", S, D, HV, [_sds((S, D), BF16)],
                             rider=gather_of([casted[W_COUT]], [W_COUT]))
    x1, h2 = _fwd_boundary("fwd_boundary_1", x0, y1, gt1, ng[0][1], ng[0][2], sc2, sh2)
    (r2, a2), (cin_buf,) = mlp_up("0", 0, h2, view(W_UP, up_buf), gather_of([casted[W_CIN]], [W_CIN]))
    (y2,), (up_buf,) = mlp_down("0", 0, a2, view(W_DOWN, down_buf), gather_of([up_buf], [W_UP], [1]))
    w_cin, w_cout, w_up = view(W_CIN, cin_buf)[0], view(W_COUT, cout_buf)[0], view(W_UP, up_buf)

    sh1b, sc1b, gt1b, sh2b, sc2b, gt2b = mods[1]
    x2, h3 = _fwd_boundary("fwd_boundary_2", x1, y2, gt2, ng[0][3], ng[1][0], sc1b, sh1b)
    nD = lambda tn: D // tn
    (proj3,), (down_buf,) = _mm(
        "conv_in", h3, w_cin, "nn", S, 3 * D, D, [_sds((3, S, D), BF16)], tn=min(1024, D),
        rider=gather_of([down_buf], [W_DOWN], [1], NEIGHBOURS),
        out_specs=[lambda tm, tn, tk: pl.BlockSpec((None, tm, tn), lambda i, j, k: (j // nD(tn), i, j % nD(tn)))])
    bz = _conv_fwd(proj3, convw_full)
    (y3,) = _mm("conv_out", bz, w_cout, "nn", S, D, D, [_sds((S, D), BF16)])
    x3, h4 = _fwd_boundary("fwd_boundary_3", x2, y3, gt1b, ng[1][1], ng[1][2], sc2b, sh2b)
    (r4, a4), (down_buf,) = mlp_up("1", 1, h4, w_up, gather_of([down_buf], [W_DOWN], [1], DIAGONAL))
    w_down = view(W_DOWN, down_buf)
    (y4,) = mlp_down("1", 1, a4, w_down)

    dx4, dy4, sums_l, loss_acc = _loss_boundary("loss_boundary", x3, y4, gt2b, ng[1][3], loss_target[0])

    dh4, dw_up1, dw_down1, _ = mlp_bwd("1", 1, h4, r4, a4, dy4)
    dx3, dy3, sums_3, _ = _bwd_boundary("bwd_boundary_3", dx4, dh4, x3, y3, gt1b, ng[1][1], ng[1][2], sc2b)

    items = [("mlp_w_up_1", W_UP, dw_up1)]
    g5s = halves(items)
    (dbz,), ras = _mm("conv_out_dx", dy3, w_cout, "nt", S, D, D, [_sds((S, D), BF16)], rider=_exchange_rider(g5s))
    (ps_up1,) = pair_sums(items, g5s, ras)
    items = [("mlp_w_down_1", W_DOWN, dw_down1)]
    g5s = halves(items)
    (dw_cout,), ras = _mm("conv_out_dw", bz, dy3, "tn", D, D, S, [_sds((D, D), BF16)], rider=_exchange_rider(g5s))
    (ps_down1,) = pair_sums(items, g5s, ras)
    dproj3, dconvw = _conv_bwd(dbz, proj3, convw_full)
    (dh3,), (rb_up1,) = _mm(
        "conv_in_dx", dproj3, w_cin, "nt", S, D, 3 * D, [_sds((S, D), BF16)], tk=D,
        rider=scatter_of([ps_up1], [W_UP], NEIGHBOURS),
        a_spec=lambda tm, tn, tk: pl.BlockSpec((None, tm, tk), lambda i, j, k: (k // (D // tk), i, k % (D // tk))))
    (dw_cin,), (rb_up1,) = _mm(
        "conv_in_dw", h3, dproj3, "tn", D, 3 * D, S, [_sds((D, 3 * D), BF16)], tn=min(1024, D),
        rider=scatter_of([ps_up1], [W_UP], DIAGONAL, [rb_up1]),
        b_spec=lambda tm, tn, tk: pl.BlockSpec((None, tk, tn), lambda i, j, k: (j // nD(tn), k, j % nD(tn))))
    dx2, dy2, sums_2, _ = _bwd_boundary("bwd_boundary_2", dx3, dh3, x2, y2, gt2, ng[0][3], ng[1][0], sc1b)

    items = [("conv_w_in", W_CIN, dw_cin), ("conv_w_out", W_COUT, dw_cout)]
    g5s = halves(items)
    conv_ps = []

    def plan(got):
        conv_ps.extend(pair_sums(items, g5s, got[1:]))
        return (scatter_of([ps_down1], [W_DOWN], DIAGONAL, [got[0]]),
                scatter_of(conv_ps[:1], [W_CIN]), scatter_of(conv_ps[1:], [W_COUT]))

    dh2, dw_up0, dw_down0, (_, (rb_down1,), (rb_cin,), (rb_cout,)) = mlp_bwd(
        "0", 0, h2, r2, a2, dy2,
        _both_riders(scatter_of([ps_down1], [W_DOWN], NEIGHBOURS), _exchange_rider(g5s)), plan)
    ps_cin, ps_cout = conv_ps
    dx1, dy1, sums_1, _ = _bwd_boundary("bwd_boundary_1", dx2, dh2, x1, y1, gt1, ng[0][1], ng[0][2], sc2)

    items = [("mlp_w_up_0", W_UP, dw_up0)]
    g5s = halves(items)
    (dw_o,), ras = _mm("mla_out_dw", o, dy1, "tn", HV, D, S, [_sds((HV, D), BF16)], rider=_exchange_rider(g5s))
    (ps_up0,) = pair_sums(items, g5s, ras)
    items = [(weights[W_O][0], W_O, dw_o), ("mlp_w_down_0", W_DOWN, dw_down0)]
    g5s = halves(items)
    (do,), ras = _mm("mla_out_dx", dy1, w_o, "nt", S, HV, D, [_sds((S, HV), BF16)], rider=_exchange_rider(g5s))
    ps_o, ps_down0 = pair_sums(items, g5s, ras)
    dq, dkv, dkr, (rb_up0, rb_down0, rb_o) = _attn_bwd_tri(
        q, kv, kr, o, do, lse, rope_tabs, H, scale,
        scatter_of([ps_up0, ps_down0, ps_o], [W_UP, W_DOWN, W_O]))
    kc_idx = jnp.stack([chip, ci]).astype(jnp.int32)
    f_o, f_cin, f_cout = [_chip_sum("chip_sum_" + weights[i][0], p, rb, kc_idx, kinds[i])
                          for i, p, rb in [(W_O, ps_o, rb_o), (W_CIN, ps_cin, rb_cin), (W_COUT, ps_cout, rb_cout)]]
    f_mlp = []
    for i, (p1, r1), (p0, r0) in [(W_UP, (ps_up1, rb_up1), (ps_up0, rb_up0)), (W_DOWN, (ps_down1, rb_down1), (ps_down0, rb_down0))]:
        f = _chip_sum("chip_sum_" + weights[i][0] + "_1", p1, r1, kc_idx, kinds[i], layer=1, n_layers=2)
        f_mlp.append(_chip_sum("chip_sum_" + weights[i][0] + "_0", p0, r0, kc_idx, kinds[i], layer=0, n_layers=2, prev=f))
    (dcq,), fin_up = _mm("mla_q_dx", dq, w_q_p, "nt", S, rank, H * QK_PAD, [_sds((S, rank), F32)],
                         rider=_share_rider([f_mlp[0]]))
    (dw_q_p,), fin_small = _mm("mla_q_dw", cq, dq, "tn", rank, H * QK_PAD, S, [_sds((rank, H * QK_PAD), BF16)],
                               rider=_share_rider([f_o, f_cin, f_cout]))
    (dckv,), fin_down = _mm("mla_kv_dx", dkv, w_ukv, "nt", S, rank, H * QK_PAD, [_sds((S, rank), F32)],
                            rider=_share_rider([f_mlp[1]]))
    finals_rest = list(fin_small) + list(fin_up) + list(fin_down)
    (dw_ukv,) = _mm("mla_kv_dw", ckv, dkv, "tn", rank, H * QK_PAD, S, [_sds((rank, H * QK_PAD), BF16)])
    dlat, sums_lat = _latent_bwd(lat, dcq, dckv, dkr, mla_g_q, mla_g_kv, rope_tabs, rank)
    late_idx = [W_IN, W_UQ, W_UKV]
    items_qkv = [(weights[W_UQ][0], W_UQ, dw_q_p.reshape(rank, H, QK_PAD)[:, :, :d_qk].reshape(rank, H * d_qk)),
                 (weights[W_UKV][0], W_UKV, dw_ukv)]
    g5s_qkv = halves(items_qkv)
    (dw_in_p,), ras = _mm("mla_in_dw", h1, dlat, "tn", D, lat_pad, S, [_sds((D, lat_pad), BF16)], tn=lat_pad,
                          rider=_exchange_rider(g5s_qkv))
    ps_qkv = pair_sums(items_qkv, g5s_qkv, ras)
    items_in = [(weights[W_IN][0], W_IN, dw_in_p[:, :lat_dim])]
    g5s_in = halves(items_in)
    (dh1,), carried = _mm("mla_in_dx", dlat, w_in_p, "nt", S, D, lat_pad, [_sds((S, D), BF16)],
                          rider=_both_riders(scatter_of(ps_qkv, [W_UQ, W_UKV]), _exchange_rider(g5s_in)))
    rbs_qkv, ras = carried[:2], carried[2:]
    ps_in = pair_sums(items_in, g5s_in, ras)
    ps_mla = ps_in + ps_qkv
    grad_x, sums_0, _ = _bwd_boundary("bwd_boundary_0", dx1, dh1, x0, None, None, None, ng[0][0], sc1)

    dmod0 = [sums_0[0], sums_0[1], sums_1[3], sums_1[0], sums_1[1], sums_2[3]]
    dmod1 = [sums_2[0], sums_2[1], sums_3[3], sums_3[0], sums_3[1], sums_l[3]]
    dng0 = [sums_0[2], sums_1[4], sums_1[2], sums_2[4]]
    dng1 = [sums_2[2], sums_3[4], sums_3[2], sums_l[4]]
    small = _pack_rows(dmod0 + dmod1 + dng0 + dng1 + [sums_lat[0], sums_lat[1], dconvw, loss_acc[0, 0:1]],
                       lane_mult=LANES)
    gathered, total, carried = _small_allgather(
        "gather_small_grads", small, with_sum=True, rider=scatter_of(ps_in, [W_IN]))
    rbs_mla = list(carried) + list(rbs_qkv)
    n_dm = 2 * n_mod * D
    dmod_all = gathered.reshape(8, -1)[:, :n_dm].reshape(8, 2, n_mod * D)
    total = total.reshape(-1)
    g_b_mod = total[:n_dm].reshape(2, n_mod * D)
    g_norm = lax.dynamic_slice(total[n_dm:n_dm + 8 * D].reshape(2, 4, D), (0, 0, chip * Dq), (2, 4, Dq))
    off = n_dm + 8 * D
    g_gq = total[off:off + rank].reshape(1, rank)
    g_gkv = total[off + rank:off + 2 * rank].reshape(1, rank)
    off += 2 * rank
    g_convw = lax.dynamic_slice(total[off:off + 3 * D].reshape(1, 3, D), (0, 0, chip * Dq), (1, 3, Dq))
    loss = total[off + 3 * D]

    dmod_cols = jnp.transpose(lax.dynamic_slice(dmod_all.reshape(8, 2, N_CHIPS, ncol), (0, 0, chip, 0), (8, 2, 1, ncol))
                              .reshape(8, 2, ncol), (1, 0, 2))
    g_w_mod, d_w_mod, nm_w_mod, nv_w_mod, _ = _adamw_mod(w_mod, cond_all.T, dmod_cols, m_w_mod, v_w_mod)
    fs_mla = [_chip_sum("chip_sum_" + weights[i][0], p, rb, kc_idx, kinds[i]) for i, p, rb in zip(late_idx, ps_mla, rbs_mla)]
    finals = list(_run_rider("grad_pair_share_mla", _share_rider(fs_mla))) + list(finals_rest)
    orig = [mla_w_in, mla_w_uq, mla_w_ukv, mla_w_o, conv_w_in, conv_w_out, mlp_w_up, mlp_w_down]
    big_grads = [f.reshape(w.shape) for f, w in zip(finals, orig)]

    names = ["b_mod", "norm_g", "mla_w_in", "mla_g_q", "mla_g_kv", "mla_w_uq", "mla_w_ukv", "mla_w_o",
             "conv_w_in", "conv_w", "conv_w_out", "mlp_w_up", "mlp_w_down"]
    ws = [b_mod, norm_g, mla_w_in, mla_g_q, mla_g_kv, mla_w_uq, mla_w_ukv, mla_w_o, conv_w_in, conv_w, conv_w_out,
          mlp_w_up, mlp_w_down]
    ms = [m_b_mod, m_norm_g, m_mla_w_in, m_mla_g_q, m_mla_g_kv, m_mla_w_uq, m_mla_w_ukv, m_mla_w_o, m_conv_w_in,
          m_conv_w, m_conv_w_out, m_mlp_w_up, m_mlp_w_down]
    vs = [v_b_mod, v_norm_g, v_mla_w_in, v_mla_g_q, v_mla_g_kv, v_mla_w_uq, v_mla_w_ukv, v_mla_w_o, v_conv_w_in,
          v_conv_w, v_conv_w_out, v_mlp_w_up, v_mlp_w_down]
    gs = [g_b_mod, g_norm, big_grads[0], g_gq, g_gkv, big_grads[1], big_grads[2], big_grads[3], big_grads[4],
          g_convw, big_grads[5], big_grads[6], big_grads[7]]
    grads, deltas, new_ms, new_vs = [g_w_mod], [d_w_mod], [nm_w_mod], [nv_w_mod]
    for nm, w, g, m, v in zip(names, ws, gs, ms, vs):
        big = any(g is b for b in big_grads)
        d, nm_, nv_, *g_out = _adamw("adamw_" + nm, w, g, m, v, emit_grad=big)
        g = g_out[0] if big else g
        grads.append(g)
        deltas.append(d)
        new_ms.append(nm_)
        new_vs.append(nv_)
    return (loss, grad_x[None], *grads, *deltas, *new_ms, *new_vs)
```
